```python
import math
import jax, jax.numpy as jnp
from jax import lax
import numpy as np

D_MODEL = 1024
BATCH = 8
SEQ = 2048
DEPTH = 2

EPS = 1e-6
SSM_WIDTH = D_MODEL // 2
SSM_GROUP_SIZE = 16
SSM_GROUPS = SSM_WIDTH // SSM_GROUP_SIZE
SSM_STATE = 64
DT_MIN = 1e-3
DT_MAX = 1e-1
POOL_WIDTH = D_MODEL // 2
POOL_WINDOWS = (2, 4, 8, 16)
POOL_GROUPS = len(POOL_WINDOWS)
POOL_GROUP = POOL_WIDTH // POOL_GROUPS
N_IN = 2 * SSM_WIDTH + 2 * POOL_WIDTH + 2 * D_MODEL
SPLITS = (SSM_WIDTH, 2 * SSM_WIDTH, 2 * SSM_WIDTH + POOL_WIDTH,
          2 * SSM_WIDTH + 2 * POOL_WIDTH, 2 * SSM_WIDTH + 2 * POOL_WIDTH + D_MODEL)

kernel_name = "hawk_merge_s5_pool_hybrid"


def _rmsnorm(x, g):
    x32 = x.astype(jnp.float32)
    y = x32 * lax.rsqrt(jnp.mean(x32 * x32, axis=-1, keepdims=True) + EPS)
    return y.astype(x.dtype) * g


def _complex_linear_combine(left, right):
    a1r, a1i, b1r, b1i = left
    a2r, a2i, b2r, b2i = right
    ar = a2r * a1r - a2i * a1i
    ai = a2r * a1i + a2i * a1r
    br = a2r * b1r - a2i * b1i + b2r
    bi = a2r * b1i + a2i * b1r + b2i
    return ar, ai, br, bi


def _s5_branch(u, log_dt, lam_re, lam_im, b_re, b_im, c_re, c_im, d_skip, w_glu, b_glu):
    bsz, seq, _ = u.shape
    ug = u.reshape(bsz, seq, SSM_GROUPS, SSM_GROUP_SIZE)
    dt = jnp.exp(log_dt)[:, None]
    mag = jnp.exp(lam_re * dt)
    ang = lam_im * dt
    abar_re = mag * jnp.cos(ang)
    abar_im = mag * jnp.sin(ang)
    num_re = abar_re - 1.0
    num_im = abar_im
    den = lam_re * lam_re + lam_im * lam_im
    coef_re = (num_re * lam_re + num_im * lam_im) / den
    coef_im = (num_im * lam_re - num_re * lam_im) / den
    bbar_re = coef_re[..., None] * b_re - coef_im[..., None] * b_im
    bbar_im = coef_re[..., None] * b_im + coef_im[..., None] * b_re
    bu_re = jnp.einsum('blgc,gpc->blgp', ug, bbar_re)
    bu_im = jnp.einsum('blgc,gpc->blgp', ug, bbar_im)
    a_re = jnp.broadcast_to(abar_re, bu_re.shape)
    a_im = jnp.broadcast_to(abar_im, bu_im.shape)
    _, _, s_re, s_im = lax.associative_scan(_complex_linear_combine,
                                            (a_re, a_im, bu_re, bu_im), axis=1)
    y = (jnp.einsum('blgp,gcp->blgc', s_re, c_re)
         - jnp.einsum('blgp,gcp->blgc', s_im, c_im))
    y = y.reshape(bsz, seq, SSM_WIDTH) + d_skip * u
    y = jax.nn.gelu(y)
    return y * jax.nn.sigmoid(y @ w_glu + b_glu)


def _pool_branch(u, w_group, scale):
    bsz, seq, _ = u.shape
    u32 = u.astype(jnp.float32)
    cs = jnp.cumsum(u32, axis=1)
    pos = jnp.arange(seq)
    outs = []
    for gi, win in enumerate(POOL_WINDOWS):
        csg = cs[:, :, gi * POOL_GROUP:(gi + 1) * POOL_GROUP]
        shifted = jnp.pad(csg, ((0, 0), (win, 0), (0, 0)))[:, :seq]
        count = jnp.minimum(pos + 1, win).astype(jnp.float32)[None, :, None]
        mean = (csg - shifted) / count
        outs.append(mean - u32[:, :, gi * POOL_GROUP:(gi + 1) * POOL_GROUP])
    pooled = jnp.stack(outs, axis=2).astype(u.dtype)
    mixed = jnp.einsum('blgc,gcd->blgd', pooled, w_group).reshape(bsz, seq, POOL_WIDTH)
    return mixed * scale


def _fwd_setup_inputs(seed: int = 0) -> dict:
    key = jax.random.key(seed)
    ks = jax.random.split(key, 24)
    f32 = jnp.float32
    nrm = lambda k, shape, std: (jax.random.normal(k, shape, f32) * std)
    x = jax.random.normal(ks[0], (BATCH, SEQ, D_MODEL), f32)
    norm_g = 1.0 + nrm(ks[1], (DEPTH, D_MODEL), 0.05)
    w_in = nrm(ks[2], (DEPTH, D_MODEL, N_IN), D_MODEL ** -0.5)
    b_in = nrm(ks[3], (DEPTH, N_IN), 0.02)
    ssm_log_dt = jax.random.uniform(ks[4], (DEPTH, SSM_GROUPS), f32,
                                    math.log(DT_MIN), math.log(DT_MAX))
    n_idx = jnp.arange(SSM_STATE, dtype=f32)
    ssm_lam_re = -0.5 + nrm(ks[5], (DEPTH, SSM_GROUPS, SSM_STATE), 0.01)
    ssm_lam_im = math.pi * n_idx[None, None, :] + nrm(ks[6], (DEPTH, SSM_GROUPS, SSM_STATE), 0.01)
    b_std = (2.0 * SSM_GROUP_SIZE) ** -0.5
    ssm_b_re = nrm(ks[7], (DEPTH, SSM_GROUPS, SSM_STATE, SSM_GROUP_SIZE), b_std)
    ssm_b_im = nrm(ks[8], (DEPTH, SSM_GROUPS, SSM_STATE, SSM_GROUP_SIZE), b_std)
    c_std = SSM_STATE ** -0.5
    ssm_c_re = nrm(ks[9], (DEPTH, SSM_GROUPS, SSM_GROUP_SIZE, SSM_STATE), c_std)
    ssm_c_im = nrm(ks[10], (DEPTH, SSM_GROUPS, SSM_GROUP_SIZE, SSM_STATE), c_std)
    ssm_d = nrm(ks[11], (DEPTH, SSM_WIDTH), 1.0)
    ssm_w_glu = nrm(ks[12], (DEPTH, SSM_WIDTH, SSM_WIDTH), SSM_WIDTH ** -0.5)
    ssm_b_glu = nrm(ks[13], (DEPTH, SSM_WIDTH), 0.02)
    pool_w = nrm(ks[14], (DEPTH, POOL_GROUPS, POOL_GROUP, POOL_GROUP), POOL_GROUP ** -0.5)
    pool_scale = 1.0 + nrm(ks[15], (DEPTH, POOL_WIDTH), 0.1)
    w_branch_a = nrm(ks[16], (DEPTH, SSM_WIDTH, D_MODEL), SSM_WIDTH ** -0.5)
    w_branch_b = nrm(ks[17], (DEPTH, POOL_WIDTH, D_MODEL), POOL_WIDTH ** -0.5)
    w_out = nrm(ks[18], (DEPTH, D_MODEL, D_MODEL), D_MODEL ** -0.5)
    final_norm_g = 1.0 + nrm(ks[19], (D_MODEL,), 0.05)
    return {"x": x, "norm_g": norm_g, "w_in": w_in, "b_in": b_in,
            "ssm_log_dt": ssm_log_dt, "ssm_lam_re": ssm_lam_re, "ssm_lam_im": ssm_lam_im,
            "ssm_b_re": ssm_b_re, "ssm_b_im": ssm_b_im, "ssm_c_re": ssm_c_re, "ssm_c_im": ssm_c_im,
            "ssm_d": ssm_d, "ssm_w_glu": ssm_w_glu, "ssm_b_glu": ssm_b_glu,
            "pool_w": pool_w, "pool_scale": pool_scale,
            "w_branch_a": w_branch_a, "w_branch_b": w_branch_b, "w_out": w_out,
            "final_norm_g": final_norm_g}


def _fwd_reference(x, norm_g, w_in, b_in, ssm_log_dt, ssm_lam_re, ssm_lam_im, ssm_b_re, ssm_b_im,
              ssm_c_re, ssm_c_im, ssm_d, ssm_w_glu, ssm_b_glu, pool_w, pool_scale,
              w_branch_a, w_branch_b, w_out, final_norm_g):
    for l in range(DEPTH):
        h = _rmsnorm(x, norm_g[l])
        proj = h @ w_in[l] + b_in[l]
        ua, za, ub, zb, ga, gb = jnp.split(proj, SPLITS, axis=-1)
        ya = _s5_branch(ua, ssm_log_dt[l], ssm_lam_re[l], ssm_lam_im[l], ssm_b_re[l], ssm_b_im[l],
                        ssm_c_re[l], ssm_c_im[l], ssm_d[l], ssm_w_glu[l], ssm_b_glu[l])
        ya = ya * jax.nn.silu(za)
        yb = _pool_branch(ub, pool_w[l], pool_scale[l]) * jax.nn.silu(zb)
        merged = (jax.nn.sigmoid(ga) * (ya @ w_branch_a[l])
                  + jax.nn.sigmoid(gb) * (yb @ w_branch_b[l]))
        x = x + merged @ w_out[l]
    return _rmsnorm(x, final_norm_g)


import jax as _jax
import jax.numpy as _jnp

TWIN_FORMAT = 'train_step'
FWD_PARAMS = ['x', 'norm_g', 'w_in', 'b_in', 'ssm_log_dt', 'ssm_lam_re', 'ssm_lam_im', 'ssm_b_re', 'ssm_b_im', 'ssm_c_re', 'ssm_c_im', 'ssm_d', 'ssm_w_glu', 'ssm_b_glu', 'pool_w', 'pool_scale', 'w_branch_a', 'w_branch_b', 'w_out', 'final_norm_g']
TWIN_WEIGHTS = ['norm_g', 'w_in', 'b_in', 'ssm_log_dt', 'ssm_lam_re', 'ssm_lam_im', 'ssm_b_re', 'ssm_b_im', 'ssm_c_re', 'ssm_c_im', 'ssm_d', 'ssm_w_glu', 'ssm_b_glu', 'pool_w', 'pool_scale', 'w_branch_a', 'w_branch_b', 'w_out', 'final_norm_g']
TWIN_DIFF_INPUT = 'x'
TWIN_INPUTS = ['x', 'norm_g', 'w_in', 'b_in', 'ssm_log_dt', 'ssm_lam_re', 'ssm_lam_im', 'ssm_b_re', 'ssm_b_im', 'ssm_c_re', 'ssm_c_im', 'ssm_d', 'ssm_w_glu', 'ssm_b_glu', 'pool_w', 'pool_scale', 'w_branch_a', 'w_branch_b', 'w_out', 'final_norm_g', 'loss_target', 'm_norm_g', 'm_w_in', 'm_b_in', 'm_ssm_log_dt', 'm_ssm_lam_re', 'm_ssm_lam_im', 'm_ssm_b_re', 'm_ssm_b_im', 'm_ssm_c_re', 'm_ssm_c_im', 'm_ssm_d', 'm_ssm_w_glu', 'm_ssm_b_glu', 'm_pool_w', 'm_pool_scale', 'm_w_branch_a', 'm_w_branch_b', 'm_w_out', 'm_final_norm_g', 'v_norm_g', 'v_w_in', 'v_b_in', 'v_ssm_log_dt', 'v_ssm_lam_re', 'v_ssm_lam_im', 'v_ssm_b_re', 'v_ssm_b_im', 'v_ssm_c_re', 'v_ssm_c_im', 'v_ssm_d', 'v_ssm_w_glu', 'v_ssm_b_glu', 'v_pool_w', 'v_pool_scale', 'v_w_branch_a', 'v_w_branch_b', 'v_w_out', 'v_final_norm_g']
TWIN_OUTPUTS = ['loss', 'grad_x', 'grad_norm_g', 'grad_w_in', 'grad_b_in', 'grad_ssm_log_dt', 'grad_ssm_lam_re', 'grad_ssm_lam_im', 'grad_ssm_b_re', 'grad_ssm_b_im', 'grad_ssm_c_re', 'grad_ssm_c_im', 'grad_ssm_d', 'grad_ssm_w_glu', 'grad_ssm_b_glu', 'grad_pool_w', 'grad_pool_scale', 'grad_w_branch_a', 'grad_w_branch_b', 'grad_w_out', 'grad_final_norm_g', 'delta_norm_g', 'delta_w_in', 'delta_b_in', 'delta_ssm_log_dt', 'delta_ssm_lam_re', 'delta_ssm_lam_im', 'delta_ssm_b_re', 'delta_ssm_b_im', 'delta_ssm_c_re', 'delta_ssm_c_im', 'delta_ssm_d', 'delta_ssm_w_glu', 'delta_ssm_b_glu', 'delta_pool_w', 'delta_pool_scale', 'delta_w_branch_a', 'delta_w_branch_b', 'delta_w_out', 'delta_final_norm_g', 'new_m_norm_g', 'new_m_w_in', 'new_m_b_in', 'new_m_ssm_log_dt', 'new_m_ssm_lam_re', 'new_m_ssm_lam_im', 'new_m_ssm_b_re', 'new_m_ssm_b_im', 'new_m_ssm_c_re', 'new_m_ssm_c_im', 'new_m_ssm_d', 'new_m_ssm_w_glu', 'new_m_ssm_b_glu', 'new_m_pool_w', 'new_m_pool_scale', 'new_m_w_branch_a', 'new_m_w_branch_b', 'new_m_w_out', 'new_m_final_norm_g', 'new_v_norm_g', 'new_v_w_in', 'new_v_b_in', 'new_v_ssm_log_dt', 'new_v_ssm_lam_re', 'new_v_ssm_lam_im', 'new_v_ssm_b_re', 'new_v_ssm_b_im', 'new_v_ssm_c_re', 'new_v_ssm_c_im', 'new_v_ssm_d', 'new_v_ssm_w_glu', 'new_v_ssm_b_glu', 'new_v_pool_w', 'new_v_pool_scale', 'new_v_w_branch_a', 'new_v_w_branch_b', 'new_v_w_out', 'new_v_final_norm_g']
TWIN_LEAF_KINDS = {'loss': 'loss', 'grad_x': 'grad_x', 'grad_norm_g': 'grad_w', 'grad_w_in': 'grad_w', 'grad_b_in': 'grad_w', 'grad_ssm_log_dt': 'grad_w', 'grad_ssm_lam_re': 'grad_w', 'grad_ssm_lam_im': 'grad_w', 'grad_ssm_b_re': 'grad_w', 'grad_ssm_b_im': 'grad_w', 'grad_ssm_c_re': 'grad_w', 'grad_ssm_c_im': 'grad_w', 'grad_ssm_d': 'grad_w', 'grad_ssm_w_glu': 'grad_w', 'grad_ssm_b_glu': 'grad_w', 'grad_pool_w': 'grad_w', 'grad_pool_scale': 'grad_w', 'grad_w_branch_a': 'grad_w', 'grad_w_branch_b': 'grad_w', 'grad_w_out': 'grad_w', 'grad_final_norm_g': 'grad_w', 'delta_norm_g': 'delta_w', 'delta_w_in': 'delta_w', 'delta_b_in': 'delta_w', 'delta_ssm_log_dt': 'delta_w', 'delta_ssm_lam_re': 'delta_w', 'delta_ssm_lam_im': 'delta_w', 'delta_ssm_b_re': 'delta_w', 'delta_ssm_b_im': 'delta_w', 'delta_ssm_c_re': 'delta_w', 'delta_ssm_c_im': 'delta_w', 'delta_ssm_d': 'delta_w', 'delta_ssm_w_glu': 'delta_w', 'delta_ssm_b_glu': 'delta_w', 'delta_pool_w': 'delta_w', 'delta_pool_scale': 'delta_w', 'delta_w_branch_a': 'delta_w', 'delta_w_branch_b': 'delta_w', 'delta_w_out': 'delta_w', 'delta_final_norm_g': 'delta_w', 'new_m_norm_g': 'new_m', 'new_m_w_in': 'new_m', 'new_m_b_in': 'new_m', 'new_m_ssm_log_dt': 'new_m', 'new_m_ssm_lam_re': 'new_m', 'new_m_ssm_lam_im': 'new_m', 'new_m_ssm_b_re': 'new_m', 'new_m_ssm_b_im': 'new_m', 'new_m_ssm_c_re': 'new_m', 'new_m_ssm_c_im': 'new_m', 'new_m_ssm_d': 'new_m', 'new_m_ssm_w_glu': 'new_m', 'new_m_ssm_b_glu': 'new_m', 'new_m_pool_w': 'new_m', 'new_m_pool_scale': 'new_m', 'new_m_w_branch_a': 'new_m', 'new_m_w_branch_b': 'new_m', 'new_m_w_out': 'new_m', 'new_m_final_norm_g': 'new_m', 'new_v_norm_g': 'new_v', 'new_v_w_in': 'new_v', 'new_v_b_in': 'new_v', 'new_v_ssm_log_dt': 'new_v', 'new_v_ssm_lam_re': 'new_v', 'new_v_ssm_lam_im': 'new_v', 'new_v_ssm_b_re': 'new_v', 'new_v_ssm_b_im': 'new_v', 'new_v_ssm_c_re': 'new_v', 'new_v_ssm_c_im': 'new_v', 'new_v_ssm_d': 'new_v', 'new_v_ssm_w_glu': 'new_v', 'new_v_ssm_b_glu': 'new_v', 'new_v_pool_w': 'new_v', 'new_v_pool_scale': 'new_v', 'new_v_w_branch_a': 'new_v', 'new_v_w_branch_b': 'new_v', 'new_v_w_out': 'new_v', 'new_v_final_norm_g': 'new_v'}


def _forward(args):
    return _fwd_reference(*[args[k] for k in FWD_PARAMS])


def _output_shape():
    out = _jax.eval_shape(lambda: _forward(_fwd_setup_inputs(0)))
    return out.shape, out.dtype

N_MICROBATCH = 1
ADAM_LR = 0.001
ADAM_B1 = 0.9
ADAM_B2 = 0.999
ADAM_EPS = 1e-08
ADAM_WD = 0.01
ADAM_STEP = 10
PER_EXAMPLE_BATCH_AXIS = {'x': 0, 'loss_target': 0}
SHARED_INPUTS = []
_WEIGHT_DTYPES = {'norm_g': _jnp.float32, 'w_in': _jnp.float32, 'b_in': _jnp.float32, 'ssm_log_dt': _jnp.float32, 'ssm_lam_re': _jnp.float32, 'ssm_lam_im': _jnp.float32, 'ssm_b_re': _jnp.float32, 'ssm_b_im': _jnp.float32, 'ssm_c_re': _jnp.float32, 'ssm_c_im': _jnp.float32, 'ssm_d': _jnp.float32, 'ssm_w_glu': _jnp.float32, 'ssm_b_glu': _jnp.float32, 'pool_w': _jnp.float32, 'pool_scale': _jnp.float32, 'w_branch_a': _jnp.float32, 'w_branch_b': _jnp.float32, 'w_out': _jnp.float32, 'final_norm_g': _jnp.float32}
MOMENT_SCALE = {'norm_g': 5.537921e-02, 'w_in': 2.753182e-02, 'b_in': 2.286294e-02, 'ssm_log_dt': 1.195252e+00, 'ssm_lam_re': 1.394943e-03, 'ssm_lam_im': 1.321869e-03, 'ssm_b_re': 8.972127e-04, 'ssm_b_im': 8.951182e-04, 'ssm_c_re': 1.249456e-03, 'ssm_c_im': 1.251200e-03, 'ssm_d': 2.109677e-02, 'ssm_w_glu': 5.803262e-03, 'ssm_b_glu': 8.938468e-03, 'pool_w': 4.862920e-02, 'pool_scale': 4.902267e-02, 'w_branch_a': 1.358045e-02, 'w_branch_b': 3.441708e-02, 'w_out': 3.705689e-02, 'final_norm_g': 1.598412e+01}


def _to_microbatches(a, axis):
    t = _jnp.moveaxis(a, axis, 0)
    t = t.reshape((N_MICROBATCH, t.shape[0] // N_MICROBATCH) + t.shape[1:])
    return _jnp.moveaxis(t, 1, axis + 1)


def setup_inputs(seed: int = 0) -> dict:
    inp = _fwd_setup_inputs(seed)
    key = _jax.random.fold_in(_jax.random.key(seed), 7919)
    shape, _ = _output_shape()
    out = dict(inp)
    out["loss_target"] = _jax.random.normal(_jax.random.fold_in(key, 0), shape, _jnp.float32)
    for i, name in enumerate(TWIN_WEIGHTS):
        w = inp[name].astype(_jnp.float32)
        if MOMENT_SCALE is None:
            s = _jnp.sqrt(_jnp.mean(_jnp.square(w)) + 1e-30)
        else:
            s = MOMENT_SCALE[name]
        km, kv = _jax.random.split(_jax.random.fold_in(key, i + 1))
        out[name] = w
        out["m_" + name] = s * _jax.random.normal(km, w.shape, _jnp.float32)
        out["v_" + name] = (s * s) * _jax.random.uniform(kv, w.shape, _jnp.float32, 0.5, 1.5)
    if N_MICROBATCH > 1:
        for name, axis in PER_EXAMPLE_BATCH_AXIS.items():
            out[name] = _to_microbatches(out[name], axis)
    return {'x': out['x'], 'norm_g': out['norm_g'], 'w_in': out['w_in'], 'b_in': out['b_in'], 'ssm_log_dt': out['ssm_log_dt'], 'ssm_lam_re': out['ssm_lam_re'], 'ssm_lam_im': out['ssm_lam_im'], 'ssm_b_re': out['ssm_b_re'], 'ssm_b_im': out['ssm_b_im'], 'ssm_c_re': out['ssm_c_re'], 'ssm_c_im': out['ssm_c_im'], 'ssm_d': out['ssm_d'], 'ssm_w_glu': out['ssm_w_glu'], 'ssm_b_glu': out['ssm_b_glu'], 'pool_w': out['pool_w'], 'pool_scale': out['pool_scale'], 'w_branch_a': out['w_branch_a'], 'w_branch_b': out['w_branch_b'], 'w_out': out['w_out'], 'final_norm_g': out['final_norm_g'], 'loss_target': out['loss_target'], 'm_norm_g': out['m_norm_g'], 'm_w_in': out['m_w_in'], 'm_b_in': out['m_b_in'], 'm_ssm_log_dt': out['m_ssm_log_dt'], 'm_ssm_lam_re': out['m_ssm_lam_re'], 'm_ssm_lam_im': out['m_ssm_lam_im'], 'm_ssm_b_re': out['m_ssm_b_re'], 'm_ssm_b_im': out['m_ssm_b_im'], 'm_ssm_c_re': out['m_ssm_c_re'], 'm_ssm_c_im': out['m_ssm_c_im'], 'm_ssm_d': out['m_ssm_d'], 'm_ssm_w_glu': out['m_ssm_w_glu'], 'm_ssm_b_glu': out['m_ssm_b_glu'], 'm_pool_w': out['m_pool_w'], 'm_pool_scale': out['m_pool_scale'], 'm_w_branch_a': out['m_w_branch_a'], 'm_w_branch_b': out['m_w_branch_b'], 'm_w_out': out['m_w_out'], 'm_final_norm_g': out['m_final_norm_g'], 'v_norm_g': out['v_norm_g'], 'v_w_in': out['v_w_in'], 'v_b_in': out['v_b_in'], 'v_ssm_log_dt': out['v_ssm_log_dt'], 'v_ssm_lam_re': out['v_ssm_lam_re'], 'v_ssm_lam_im': out['v_ssm_lam_im'], 'v_ssm_b_re': out['v_ssm_b_re'], 'v_ssm_b_im': out['v_ssm_b_im'], 'v_ssm_c_re': out['v_ssm_c_re'], 'v_ssm_c_im': out['v_ssm_c_im'], 'v_ssm_d': out['v_ssm_d'], 'v_ssm_w_glu': out['v_ssm_w_glu'], 'v_ssm_b_glu': out['v_ssm_b_glu'], 'v_pool_w': out['v_pool_w'], 'v_pool_scale': out['v_pool_scale'], 'v_w_branch_a': out['v_w_branch_a'], 'v_w_branch_b': out['v_w_branch_b'], 'v_w_out': out['v_w_out'], 'v_final_norm_g': out['v_final_norm_g']}


def _loss(weights, diff, rest, loss_target):
    with _jax.named_scope("forward"):
        args = {**rest, TWIN_DIFF_INPUT: diff, **{k: w.astype(_WEIGHT_DTYPES[k]) for k, w in weights.items()}}
        y = _forward(args)
    with _jax.named_scope("loss_head"):
        err = _jnp.square(y.astype(_jnp.float32) - loss_target)
        return 0.5 * _jnp.sum(_jnp.mean(err, axis=-1)) if err.ndim else 0.5 * err


def _adamw(w, g, m, v):
    m = ADAM_B1 * m + (1.0 - ADAM_B1) * g
    v = ADAM_B2 * v + (1.0 - ADAM_B2) * _jnp.square(g)
    m_hat = m / (1.0 - ADAM_B1 ** ADAM_STEP)
    v_hat = v / (1.0 - ADAM_B2 ** ADAM_STEP)
    delta = -ADAM_LR * (m_hat / (_jnp.sqrt(v_hat) + ADAM_EPS) + ADAM_WD * w)
    return delta, m, v


def reference(x, norm_g, w_in, b_in, ssm_log_dt, ssm_lam_re, ssm_lam_im, ssm_b_re, ssm_b_im, ssm_c_re, ssm_c_im, ssm_d, ssm_w_glu, ssm_b_glu, pool_w, pool_scale, w_branch_a, w_branch_b, w_out, final_norm_g, loss_target, m_norm_g, m_w_in, m_b_in, m_ssm_log_dt, m_ssm_lam_re, m_ssm_lam_im, m_ssm_b_re, m_ssm_b_im, m_ssm_c_re, m_ssm_c_im, m_ssm_d, m_ssm_w_glu, m_ssm_b_glu, m_pool_w, m_pool_scale, m_w_branch_a, m_w_branch_b, m_w_out, m_final_norm_g, v_norm_g, v_w_in, v_b_in, v_ssm_log_dt, v_ssm_lam_re, v_ssm_lam_im, v_ssm_b_re, v_ssm_b_im, v_ssm_c_re, v_ssm_c_im, v_ssm_d, v_ssm_w_glu, v_ssm_b_glu, v_pool_w, v_pool_scale, v_w_branch_a, v_w_branch_b, v_w_out, v_final_norm_g):
    given = dict(x=x, norm_g=norm_g, w_in=w_in, b_in=b_in, ssm_log_dt=ssm_log_dt, ssm_lam_re=ssm_lam_re, ssm_lam_im=ssm_lam_im, ssm_b_re=ssm_b_re, ssm_b_im=ssm_b_im, ssm_c_re=ssm_c_re, ssm_c_im=ssm_c_im, ssm_d=ssm_d, ssm_w_glu=ssm_w_glu, ssm_b_glu=ssm_b_glu, pool_w=pool_w, pool_scale=pool_scale, w_branch_a=w_branch_a, w_branch_b=w_branch_b, w_out=w_out, final_norm_g=final_norm_g, loss_target=loss_target, m_norm_g=m_norm_g, m_w_in=m_w_in, m_b_in=m_b_in, m_ssm_log_dt=m_ssm_log_dt, m_ssm_lam_re=m_ssm_lam_re, m_ssm_lam_im=m_ssm_lam_im, m_ssm_b_re=m_ssm_b_re, m_ssm_b_im=m_ssm_b_im, m_ssm_c_re=m_ssm_c_re, m_ssm_c_im=m_ssm_c_im, m_ssm_d=m_ssm_d, m_ssm_w_glu=m_ssm_w_glu, m_ssm_b_glu=m_ssm_b_glu, m_pool_w=m_pool_w, m_pool_scale=m_pool_scale, m_w_branch_a=m_w_branch_a, m_w_branch_b=m_w_branch_b, m_w_out=m_w_out, m_final_norm_g=m_final_norm_g, v_norm_g=v_norm_g, v_w_in=v_w_in, v_b_in=v_b_in, v_ssm_log_dt=v_ssm_log_dt, v_ssm_lam_re=v_ssm_lam_re, v_ssm_lam_im=v_ssm_lam_im, v_ssm_b_re=v_ssm_b_re, v_ssm_b_im=v_ssm_b_im, v_ssm_c_re=v_ssm_c_re, v_ssm_c_im=v_ssm_c_im, v_ssm_d=v_ssm_d, v_ssm_w_glu=v_ssm_w_glu, v_ssm_b_glu=v_ssm_b_glu, v_pool_w=v_pool_w, v_pool_scale=v_pool_scale, v_w_branch_a=v_w_branch_a, v_w_branch_b=v_w_branch_b, v_w_out=v_w_out, v_final_norm_g=v_final_norm_g)
    weights = {n: given[n] for n in TWIN_WEIGHTS}
    shared = {n: given[n] for n in SHARED_INPUTS}
    per_example = {n: given[n] for n in ['x']}
    grad_fn = _jax.value_and_grad(_loss, argnums=(0, 1))

    def one_microbatch(ex, loss_target):
        ex = dict(ex)
        diff = ex.pop(TWIN_DIFF_INPUT)
        return grad_fn(weights, diff, {**shared, **ex}, loss_target)

    if N_MICROBATCH == 1:
        loss, (grad_w, grad_x) = one_microbatch(per_example, given["loss_target"])
    else:
        def body(carry, xs):
            loss_sum, grad_sum = carry
            l_k, (gw_k, gx_k) = one_microbatch(xs[0], xs[1])
            with _jax.named_scope("update"):
                return (loss_sum + l_k, _jax.tree.map(_jnp.add, grad_sum, gw_k)), gx_k

        init = (_jnp.zeros((), _jnp.float32), _jax.tree.map(_jnp.zeros_like, weights))
        (loss, grad_w), grad_x = _jax.lax.scan(body, init, (per_example, given["loss_target"]))
    with _jax.named_scope("update"):
        delta_w, new_m, new_v = {}, {}, {}
        for n in TWIN_WEIGHTS:
            delta_w[n], new_m[n], new_v[n] = _adamw(weights[n], grad_w[n], given["m_" + n], given["v_" + n])
    return (loss, grad_x, *[grad_w[n] for n in TWIN_WEIGHTS], *[delta_w[n] for n in TWIN_WEIGHTS],
            *[new_m[n] for n in TWIN_WEIGHTS], *[new_v[n] for n in TWIN_WEIGHTS])
```

```python
import functools

import jax
import jax.numpy as jnp
from jax import lax
from jax.experimental import pallas as pl
from jax.experimental.pallas import tpu as pltpu

F32, BF16 = jnp.float32, jnp.bfloat16
SDS = jax.ShapeDtypeStruct
MESH = pl.DeviceIdType.MESH

DEPTH = 2
L = 2048
D = 1024
NIN = 4096
W = 512
G, P, C = 32, 64, 16
GP = G * P
WINS = (2, 4, 8, 16)
TM = 256
NT = L // TM
EPS = 1e-6
VMEM_LIMIT = 56 * 2**20

LR, B1, B2, EPS_A, WD, STEP = 0.001, 0.9, 0.999, 1e-08, 0.01, 10


def _params(*sem):
    return pltpu.CompilerParams(dimension_semantics=sem, vmem_limit_bytes=VMEM_LIMIT)


def _full(shape):
    zeros = (0,) * len(shape)
    return pl.BlockSpec(shape, lambda *_: zeros)


def _rows(width, col=0, reverse=False):
    if reverse:
        return pl.BlockSpec((TM, width), lambda i: (NT - 1 - i, col))
    return pl.BlockSpec((TM, width), lambda i: (i, col))


def _dot(a, b):
    return jnp.dot(a, b, preferred_element_type=F32)


def _dot_nt(a, b):
    return lax.dot_general(a, b, (((1,), (1,)), ((), ())), preferred_element_type=F32)


def _dot_tn(a, b):
    return lax.dot_general(a, b, (((0,), (0,)), ((), ())), preferred_element_type=F32)


_K0 = 0.7978845608028654
_K1 = 0.044715


def _gelu(x):
    return 0.5 * x * (1.0 + jnp.tanh(_K0 * (x + _K1 * (x * x * x))))


def _gelu_grad(x):
    t = jnp.tanh(_K0 * (x + _K1 * (x * x * x)))
    return 0.5 * (1.0 + t) + 0.5 * x * (1.0 - t * t) * (_K0 * (1.0 + 3.0 * _K1 * x * x))


def _sigmoid(x):
    return jax.nn.sigmoid(x)


def _norm_inproj(l, x, g, w, b):
    def body(x_ref, g_ref, w_ref, b_ref, h_ref, proj_ref):
        xv = x_ref[...]
        r = lax.rsqrt(jnp.mean(xv * xv, axis=-1, keepdims=True) + EPS)
        hb = ((xv * r) * g_ref[...]).astype(BF16)
        h_ref[...] = hb
        for j in range(4):
            cs = slice(j * 1024, (j + 1) * 1024)
            proj_ref[:, cs] = _dot(hb, w_ref[j]) + b_ref[:, cs]

    return pl.pallas_call(
        body, name=f"norm_inproj_l{l}", grid=(NT,),
        in_specs=[_rows(D), _full((1, D)), _full((4, D, 1024)), _full((1, NIN))],
        out_specs=[_rows(D), _rows(NIN)],
        out_shape=[SDS((L, D), BF16), SDS((L, NIN), F32)],
        compiler_params=_params("arbitrary"),
    )(x, g, w, b)


def _scan_tile(re_ref, im_ref, st_re, st_im, cr_re, cr_im, carry, reverse):
    rowid = lax.broadcasted_iota(jnp.int32, (8, 512), 0)

    def chunk(ci, carry):
        c = (TM // 8 - 1 - ci) if reverse else ci
        rows = pl.ds(pl.multiple_of(c * 8, 8), 8)
        new = []
        for lb in range(GP // 512):
            cols = slice(lb * 512, (lb + 1) * 512)
            vr = re_ref[rows, cols]
            vi = im_ref[rows, cols]
            for d, prow in ((1, 0), (2, 1), (4, 3)):
                ar = st_re[prow:prow + 1, cols]
                ai = st_im[prow:prow + 1, cols]
                if reverse:
                    keep = rowid < 8 - d
                    sr = jnp.where(keep, pltpu.roll(vr, 8 - d, 0), 0.0)
                    si = jnp.where(keep, pltpu.roll(vi, 8 - d, 0), 0.0)
                else:
                    keep = rowid >= d
                    sr = jnp.where(keep, pltpu.roll(vr, d, 0), 0.0)
                    si = jnp.where(keep, pltpu.roll(vi, d, 0), 0.0)
                vr, vi = vr + ar * sr - ai * si, vi + ar * si + ai * sr
            cr, ci_ = carry[2 * lb], carry[2 * lb + 1]
            pr = cr_re[:, cols]
            pi = cr_im[:, cols]
            vr, vi = vr + pr * cr - pi * ci_, vi + pr * ci_ + pi * cr
            re_ref[rows, cols] = vr
            im_ref[rows, cols] = vi
            if reverse:
                new += [vr[0:1], vi[0:1]]
            else:
                new += [vr[7:8], vi[7:8]]
        return tuple(new)

    return lax.fori_loop(0, TM // 8, chunk, carry)


def _load_carry(car_ref):
    return tuple(car_ref[r:r + 1, lb * 512:(lb + 1) * 512] for lb in range(GP // 512) for r in (0, 1))


def _store_carry(car_ref, carry):
    for lb in range(GP // 512):
        car_ref[0:1, lb * 512:(lb + 1) * 512] = carry[2 * lb]
        car_ref[1:2, lb * 512:(lb + 1) * 512] = carry[2 * lb + 1]


def _s5_fwd(l, proj, bexp, cre, cimn, ap_re, ap_im, dsk, wglu, bglu):
    def body(ua_ref, za_ref, bexp_ref, cre_ref, cimn_ref, apr_ref, api_ref, d_ref, wg_ref, bg_ref,
             sre_ref, sim_ref, y1_ref, q_ref, ya_ref, car_ref):
        @pl.when(pl.program_id(0) == 0)
        def _():
            car_ref[...] = jnp.zeros_like(car_ref)

        u = ua_ref[...]
        ub = u.astype(BF16)
        for k in range(4):
            bu = _dot(ub[:, 128 * k:128 * (k + 1)], bexp_ref[k])
            sre_ref[:, 512 * k:512 * (k + 1)] = bu[:, :512]
            sim_ref[:, 512 * k:512 * (k + 1)] = bu[:, 512:]
        carry = _scan_tile(sre_ref, sim_ref, apr_ref, api_ref, apr_ref, api_ref, _load_carry(car_ref), False)
        _store_carry(car_ref, carry)
        for k in range(4):
            blk = slice(512 * k, 512 * (k + 1))
            ks = slice(128 * k, 128 * (k + 1))
            y0 = _dot(sre_ref[:, blk].astype(BF16), cre_ref[k]) + _dot(sim_ref[:, blk].astype(BF16), cimn_ref[k])
            y1_ref[:, ks] = y0 + d_ref[:, ks] * u[:, ks]
        y2 = _gelu(y1_ref[...])
        q = _dot(y2.astype(BF16), wg_ref[...]) + bg_ref[...]
        q_ref[...] = q
        za = za_ref[...]
        ya_ref[...] = ((y2 * _sigmoid(q)) * (za * _sigmoid(za))).astype(BF16)

    return pl.pallas_call(
        body, name=f"s5_fwd_l{l}", grid=(NT,),
        in_specs=[_rows(W, 0), _rows(W, 1), _full((4, 128, 1024)), _full((4, 512, 128)), _full((4, 512, 128)),
                  _full((8, GP)), _full((8, GP)), _full((1, W)), _full((W, W)), _full((1, W))],
        out_specs=[_rows(GP), _rows(GP), _rows(W), _rows(W), _rows(W)],
        out_shape=[SDS((L, GP), F32), SDS((L, GP), F32), SDS((L, W), F32), SDS((L, W), F32), SDS((L, W), BF16)],
        scratch_shapes=[pltpu.VMEM((8, GP), F32)],
        compiler_params=_params("arbitrary"),
    )(proj, proj, bexp, cre, cimn, ap_re, ap_im, dsk, wglu, bglu)


def _pool_fwd(l, proj, pw, scale):
    def body(ub_ref, zb_ref, pw_ref, sc_ref, pooled_ref, mixed_ref, yb_ref, buf):
        i = pl.program_id(0)

        @pl.when(i == 0)
        def _():
            buf[0:16, :] = jnp.zeros((16, W), F32)

        u = ub_ref[...]
        buf[16:16 + TM, :] = u
        t = i * TM + lax.broadcasted_iota(jnp.int32, (TM, 128), 0)
        for gi, win in enumerate(WINS):
            cs = slice(128 * gi, 128 * (gi + 1))
            acc = u[:, cs]
            for k in range(1, win):
                acc = acc + buf[16 - k:16 - k + TM, cs]
            cnt = jnp.minimum(t + 1, win).astype(F32)
            pb = (acc / cnt - u[:, cs]).astype(BF16)
            pooled_ref[:, cs] = pb
            mixed_ref[:, cs] = _dot(pb, pw_ref[gi])
        zb = zb_ref[...]
        yb_ref[...] = ((mixed_ref[...] * sc_ref[...]) * (zb * _sigmoid(zb))).astype(BF16)
        buf[0:16, :] = buf[TM:TM + 16, :]

    return pl.pallas_call(
        body, name=f"pool_fwd_l{l}", grid=(NT,),
        in_specs=[_rows(W, 2), _rows(W, 3), _full((4, 128, 128)), _full((1, W))],
        out_specs=[_rows(W), _rows(W), _rows(W)],
        out_shape=[SDS((L, W), BF16), SDS((L, W), F32), SDS((L, W), BF16)],
        scratch_shapes=[pltpu.VMEM((TM + 16, W), F32)],
        compiler_params=_params("arbitrary"),
    )(proj, proj, pw, scale)


def _merge_out(l, ya, yb, proj, x, wa, wb, wo):
    def body(ya_ref, yb_ref, ga_ref, gb_ref, x_ref, wa_ref, wb_ref, wo_ref, pa_ref, pb_ref, mg_ref, xo_ref):
        ya = ya_ref[...]
        yb = yb_ref[...]
        for j in range(4):
            cs = slice(256 * j, 256 * (j + 1))
            pa_ref[:, cs] = _dot(ya, wa_ref[j])
            pb_ref[:, cs] = _dot(yb, wb_ref[j])
        merged = _sigmoid(ga_ref[...]) * pa_ref[...] + _sigmoid(gb_ref[...]) * pb_ref[...]
        mb = merged.astype(BF16)
        mg_ref[...] = mb
        xo_ref[...] = x_ref[...] + _dot(mb, wo_ref[...])

    return pl.pallas_call(
        body, name=f"merge_out_l{l}", grid=(NT,),
        in_specs=[_rows(W), _rows(W), _rows(D, 2), _rows(D, 3), _rows(D),
                  _full((4, W, 256)), _full((4, W, 256)), _full((D, D))],
        out_specs=[_rows(D), _rows(D), _rows(D), _rows(D)],
        out_shape=[SDS((L, D), F32), SDS((L, D), F32), SDS((L, D), BF16), SDS((L, D), F32)],
        compiler_params=_params("arbitrary"),
    )(ya, yb, proj, proj, x, wa, wb, wo)


def _loss_head(x, gf, target):
    def body(x_ref, g_ref, t_ref, loss_ref, dx_ref, dg_ref):
        @pl.when(pl.program_id(0) == 0)
        def _():
            loss_ref[...] = jnp.zeros_like(loss_ref)
            dg_ref[...] = jnp.zeros_like(dg_ref)

        xv = x_ref[...]
        g = g_ref[...]
        r = lax.rsqrt(jnp.mean(xv * xv, axis=-1, keepdims=True) + EPS)
        xn = xv * r
        err = xn * g - t_ref[...]
        part = jnp.sum(jnp.mean(err * err, axis=-1, keepdims=True), axis=0, keepdims=True)
        loss_ref[...] += 0.5 * part
        dy = err * (1.0 / D)
        dg_ref[...] += jnp.sum(dy * xn, axis=0, keepdims=True)
        dxn = dy * g
        dx_ref[...] = r * (dxn - xn * jnp.mean(dxn * xn, axis=-1, keepdims=True))

    return pl.pallas_call(
        body, name="loss_head", grid=(NT,),
        in_specs=[_rows(D), _full((1, D)), _rows(D)],
        out_specs=[_full((1, 128)), _rows(D), _full((1, D))],
        out_shape=[SDS((1, 128), F32), SDS((L, D), F32), SDS((1, D), F32)],
        compiler_params=_params("arbitrary"),
    )(x, gf, target)


def _merge_out_bwd(l, dxn, mg, proj, pa, pb, ya, yb, wo, wa, wb):
    def body(dx_ref, mg_ref, ga_ref, gb_ref, pa_ref, pb_ref, ya_ref, yb_ref, wo_ref, wa_ref, wb_ref,
             dg_ref, dya_ref, dyb_ref, dwo_ref, dwa_ref, dwb_ref):
        @pl.when(pl.program_id(0) == 0)
        def _():
            dwo_ref[...] = jnp.zeros_like(dwo_ref)
            dwa_ref[...] = jnp.zeros_like(dwa_ref)
            dwb_ref[...] = jnp.zeros_like(dwb_ref)

        dxb = dx_ref[...].astype(BF16)
        dm = _dot_nt(dxb, wo_ref[...])
        sa = _sigmoid(ga_ref[...])
        sb = _sigmoid(gb_ref[...])
        dg_ref[:, :D] = dm * pa_ref[...] * (sa * (1.0 - sa))
        dg_ref[:, D:] = dm * pb_ref[...] * (sb * (1.0 - sb))
        dpa = (dm * sa).astype(BF16)
        dpb = (dm * sb).astype(BF16)
        ya = ya_ref[...]
        yb = yb_ref[...]
        dya = jnp.zeros((TM, W), F32)
        dyb = jnp.zeros((TM, W), F32)
        for j in range(4):
            cs = slice(256 * j, 256 * (j + 1))
            dya = dya + _dot_nt(dpa[:, cs], wa_ref[j])
            dyb = dyb + _dot_nt(dpb[:, cs], wb_ref[j])
            dwa_ref[j] += _dot_tn(ya, dpa[:, cs])
            dwb_ref[j] += _dot_tn(yb, dpb[:, cs])
        dya_ref[...] = dya
        dyb_ref[...] = dyb
        dwo_ref[...] += _dot_tn(mg_ref[...], dxb)

    return pl.pallas_call(
        body, name=f"merge_out_bwd_l{l}", grid=(NT,),
        in_specs=[_rows(D), _rows(D), _rows(D, 2), _rows(D, 3), _rows(D), _rows(D), _rows(W), _rows(W),
                  _full((D, D)), _full((4, W, 256)), _full((4, W, 256))],
        out_specs=[_rows(2 * D), _rows(W), _rows(W), _full((D, D)), _full((4, W, 256)), _full((4, W, 256))],
        out_shape=[SDS((L, 2 * D), F32), SDS((L, W), F32), SDS((L, W), F32),
                   SDS((D, D), F32), SDS((4, W, 256), F32), SDS((4, W, 256), F32)],
        compiler_params=_params("arbitrary"),
    )(dxn, mg, proj, proj, pa, pb, ya, yb, wo, wa, wb)


def _pool_bwd(l, dyb, proj, mixed, pooled, pw, scale):
    def body(dyb_ref, zb_ref, mixed_ref, pooled_ref, pw_ref, sc_ref, db_ref, dpw_ref, dsc_ref, buf):
        i = pl.program_id(0)
        tile = NT - 1 - i

        @pl.when(i == 0)
        def _():
            dpw_ref[...] = jnp.zeros_like(dpw_ref)
            dsc_ref[...] = jnp.zeros_like(dsc_ref)
            buf[TM:TM + 16, :] = jnp.zeros((16, W), F32)

        dyb = dyb_ref[...]
        zb = zb_ref[...]
        mixed = mixed_ref[...]
        sc = sc_ref[...]
        sg = _sigmoid(zb)
        dyb0 = dyb * (zb * sg)
        db_ref[:, W:] = dyb * (mixed * sc) * (sg * (1.0 + zb * (1.0 - sg)))
        dsc_ref[...] += jnp.sum(dyb0 * mixed, axis=0, keepdims=True)
        dmix = (dyb0 * sc).astype(BF16)
        t = tile * TM + lax.broadcasted_iota(jnp.int32, (TM, 128), 0)
        for gi, win in enumerate(WINS):
            cs = slice(128 * gi, 128 * (gi + 1))
            dpw_ref[gi] += _dot_tn(pooled_ref[:, cs], dmix[:, cs])
            dpool = _dot_nt(dmix[:, cs], pw_ref[gi])
            cnt = jnp.minimum(t + 1, win).astype(F32)
            e = dpool / cnt
            buf[0:TM, cs] = e
            acc = e - dpool
            for k in range(1, win):
                acc = acc + buf[k:k + TM, cs]
            db_ref[:, cs] = acc
        buf[TM:TM + 16, :] = buf[0:16, :]

    return pl.pallas_call(
        body, name=f"pool_bwd_l{l}", grid=(NT,),
        in_specs=[_rows(W, 0, True), _rows(W, 3, True), _rows(W, 0, True), _rows(W, 0, True),
                  _full((4, 128, 128)), _full((1, W))],
        out_specs=[_rows(2 * W, 0, True), _full((4, 128, 128)), _full((1, W))],
        out_shape=[SDS((L, 2 * W), F32), SDS((4, 128, 128), F32), SDS((1, W), F32)],
        scratch_shapes=[pltpu.VMEM((TM + 16, W), F32)],
        compiler_params=_params("arbitrary"),
    )(dyb, proj, mixed, pooled, pw, scale)


def _s5_bwd(l, dya, proj, y1, q, sre, sim, cret, cimnt, bret, bimt, st_re, st_im, cr_re, cr_im, dsk, wglu):
    def halo(i):
        return (jnp.maximum((NT - 1 - i) * (TM // 8) - 1, 0), 0)

    def body(dya_ref, ua_ref, za_ref, y1_ref, q_ref, sre_ref, sim_ref, hre_ref, him_ref,
             cret_ref, cimnt_ref, bret_ref, bimt_ref, st_re_ref, st_im_ref, cr_re_ref, cr_im_ref, d_ref, wg_ref,
             da_ref, dwg_ref, dbg_ref, dd_ref, dcre_ref, dcimn_ref, dbre_ref, dbim_ref, dare_ref, daim_ref,
             lre, lim, car_ref):
        i = pl.program_id(0)
        tile = NT - 1 - i

        @pl.when(i == 0)
        def _():
            for ref in (dwg_ref, dbg_ref, dd_ref, dcre_ref, dcimn_ref, dbre_ref, dbim_ref, dare_ref, daim_ref, car_ref):
                ref[...] = jnp.zeros_like(ref)

        u = ua_ref[...]
        za = za_ref[...]
        y1 = y1_ref[...]
        dya = dya_ref[...]
        y2 = _gelu(y1)
        sg = _sigmoid(q_ref[...])
        sgz = _sigmoid(za)
        dy3 = dya * (za * sgz)
        da_ref[:, W:] = dya * (y2 * sg) * (sgz * (1.0 + za * (1.0 - sgz)))
        dq = dy3 * y2 * (sg * (1.0 - sg))
        dqb = dq.astype(BF16)
        dy2 = dy3 * sg + _dot_nt(dqb, wg_ref[...])
        dwg_ref[...] += _dot_tn(y2.astype(BF16), dqb)
        dbg_ref[...] += jnp.sum(dq, axis=0, keepdims=True)
        dy1 = dy2 * _gelu_grad(y1)
        dd_ref[...] += jnp.sum(dy1 * u, axis=0, keepdims=True)
        dy1b = dy1.astype(BF16)
        ub = u.astype(BF16)
        for k in range(4):
            blk = slice(512 * k, 512 * (k + 1))
            ks = slice(128 * k, 128 * (k + 1))
            lre[:, blk] = _dot(dy1b[:, ks], cret_ref[k])
            lim[:, blk] = _dot(dy1b[:, ks], cimnt_ref[k])
            dcre_ref[k] += _dot_tn(sre_ref[:, blk].astype(BF16), dy1b[:, ks])
            dcimn_ref[k] += _dot_tn(sim_ref[:, blk].astype(BF16), dy1b[:, ks])
        carry = _scan_tile(lre, lim, st_re_ref, st_im_ref, cr_re_ref, cr_im_ref, _load_carry(car_ref), True)
        _store_carry(car_ref, carry)

        rowid = lax.broadcasted_iota(jnp.int32, (8, 512), 0)
        gate = (tile > 0).astype(F32)

        def chunk(c, _):
            rows = pl.ds(pl.multiple_of(c * 8, 8), 8)
            prows = pl.ds(pl.multiple_of(jnp.maximum(c - 1, 0) * 8, 8), 8)
            for lb in range(GP // 512):
                cols = slice(lb * 512, (lb + 1) * 512)
                sr = sre_ref[rows, cols]
                si = sim_ref[rows, cols]
                pr = jnp.where(c == 0, hre_ref[7:8, cols] * gate, sre_ref[prows, cols][7:8])
                pi = jnp.where(c == 0, him_ref[7:8, cols] * gate, sim_ref[prows, cols][7:8])
                sr = jnp.where(rowid == 0, pr, pltpu.roll(sr, 1, 0))
                si = jnp.where(rowid == 0, pi, pltpu.roll(si, 1, 0))
                lr = lre[rows, cols]
                li = lim[rows, cols]
                dare_ref[:, cols] += sr * lr + si * li
                daim_ref[:, cols] += sr * li - si * lr
            return 0

        lax.fori_loop(0, TM // 8, chunk, 0)

        for k in range(4):
            blk = slice(512 * k, 512 * (k + 1))
            ks = slice(128 * k, 128 * (k + 1))
            lrb = lre[:, blk].astype(BF16)
            lib = lim[:, blk].astype(BF16)
            da_ref[:, ks] = dy1[:, ks] * d_ref[:, ks] + _dot(lrb, bret_ref[k]) + _dot(lib, bimt_ref[k])
            dbre_ref[k] += _dot_tn(ub[:, ks], lrb)
            dbim_ref[k] += _dot_tn(ub[:, ks], lib)

    return pl.pallas_call(
        body, name=f"s5_bwd_l{l}", grid=(NT,),
        in_specs=[_rows(W, 0, True), _rows(W, 0, True), _rows(W, 1, True), _rows(W, 0, True), _rows(W, 0, True),
                  _rows(GP, 0, True), _rows(GP, 0, True),
                  pl.BlockSpec((8, GP), halo), pl.BlockSpec((8, GP), halo),
                  _full((4, 128, 512)), _full((4, 128, 512)), _full((4, 512, 128)), _full((4, 512, 128)),
                  _full((8, GP)), _full((8, GP)), _full((8, GP)), _full((8, GP)), _full((1, W)), _full((W, W))],
        out_specs=[_rows(2 * W, 0, True), _full((W, W)), _full((1, W)), _full((1, W)),
                   _full((4, 512, 128)), _full((4, 512, 128)), _full((4, 128, 512)), _full((4, 128, 512)),
                   _full((8, GP)), _full((8, GP))],
        out_shape=[SDS((L, 2 * W), F32), SDS((W, W), F32), SDS((1, W), F32), SDS((1, W), F32),
                   SDS((4, 512, 128), F32), SDS((4, 512, 128), F32), SDS((4, 128, 512), F32), SDS((4, 128, 512), F32),
                   SDS((8, GP), F32), SDS((8, GP), F32)],
        scratch_shapes=[pltpu.VMEM((TM, GP), F32), pltpu.VMEM((TM, GP), F32), pltpu.VMEM((8, GP), F32)],
        compiler_params=_params("arbitrary"),
    )(dya, proj, proj, y1, q, sre, sim, sre, sim, cret, cimnt, bret, bimt, st_re, st_im, cr_re, cr_im, dsk, wglu)


def _inproj_dw(l, h, dproj):
    def body(h_ref, dp_ref, dw_ref, db_ref):
        @pl.when(pl.program_id(1) == 0)
        def _():
            dw_ref[...] = jnp.zeros_like(dw_ref)
            db_ref[...] = jnp.zeros_like(db_ref)

        dp = dp_ref[...]
        dw_ref[0] += _dot_tn(h_ref[...], dp.astype(BF16))
        db_ref[...] += jnp.sum(dp, axis=0, keepdims=True)

    return pl.pallas_call(
        body, name=f"inproj_dw_l{l}", grid=(4, NT),
        in_specs=[pl.BlockSpec((TM, D), lambda j, i: (i, 0)), pl.BlockSpec((TM, 1024), lambda j, i: (i, j))],
        out_specs=[pl.BlockSpec((1, D, 1024), lambda j, i: (j, 0, 0)), pl.BlockSpec((1, 1024), lambda j, i: (0, j))],
        out_shape=[SDS((4, D, 1024), F32), SDS((1, NIN), F32)],
        compiler_params=_params("arbitrary", "arbitrary"),
    )(h, dproj)


def _inproj_dx(l, dproj, w, x, g, dxn):
    def body(dp_ref, w_ref, x_ref, g_ref, dxn_ref, dx_ref, dg_ref):
        @pl.when(pl.program_id(0) == 0)
        def _():
            dg_ref[...] = jnp.zeros_like(dg_ref)

        dh = jnp.zeros((TM, D), F32)
        for j in range(4):
            dh = dh + _dot_nt(dp_ref[:, j * 1024:(j + 1) * 1024].astype(BF16), w_ref[j])
        xv = x_ref[...]
        r = lax.rsqrt(jnp.mean(xv * xv, axis=-1, keepdims=True) + EPS)
        xn = xv * r
        dg_ref[...] += jnp.sum(dh * xn, axis=0, keepdims=True)
        dn = dh * g_ref[...]
        dx_ref[...] = dxn_ref[...] + r * (dn - xn * jnp.mean(dn * xn, axis=-1, keepdims=True))

    return pl.pallas_call(
        body, name=f"inproj_dx_l{l}", grid=(NT,),
        in_specs=[_rows(NIN), _full((4, D, 1024)), _rows(D), _full((1, D)), _rows(D)],
        out_specs=[_rows(D), _full((1, D))],
        out_shape=[SDS((L, D), F32), SDS((1, D), F32)],
        compiler_params=_params("arbitrary"),
    )(dproj, w, x, g, dxn)


def _discretize(log_dt, lam_re, lam_im, b_re, b_im):
    dt = jnp.exp(log_dt)[:, None]
    mag = jnp.exp(lam_re * dt)
    ang = lam_im * dt
    abar_re = mag * jnp.cos(ang)
    abar_im = mag * jnp.sin(ang)
    num_re = abar_re - 1.0
    num_im = abar_im
    den = lam_re * lam_re + lam_im * lam_im
    coef_re = (num_re * lam_re + num_im * lam_im) / den
    coef_im = (num_im * lam_re - num_re * lam_im) / den
    bbar_re = coef_re[..., None] * b_re - coef_im[..., None] * b_im
    bbar_im = coef_re[..., None] * b_im + coef_im[..., None] * b_re
    return abar_re, abar_im, bbar_re, bbar_im


def _powers(abar_re, abar_im):
    ar, ai = abar_re.reshape(1, GP), abar_im.reshape(1, GP)
    rows_re, rows_im = [ar], [ai]
    for _ in range(7):
        pr, pi = rows_re[-1], rows_im[-1]
        rows_re.append(pr * ar - pi * ai)
        rows_im.append(pr * ai + pi * ar)
    return jnp.concatenate(rows_re, axis=0), jnp.concatenate(rows_im, axis=0)


_EYE8 = functools.partial(jnp.eye, 8, dtype=F32)


def _expand_in(b):
    return jnp.einsum("kgpc,gh->kgchp", b.reshape(4, 8, P, C), _EYE8()).reshape(4, 128, 512)


def _extract_in(e):
    return jnp.einsum("kgchp,gh->kgpc", e.reshape(4, 8, C, 8, P), _EYE8()).reshape(G, P, C)


def _expand_out(c):
    return jnp.einsum("kgcp,gh->kgphc", c.reshape(4, 8, C, P), _EYE8()).reshape(4, 512, 128)


def _extract_out(e):
    return jnp.einsum("kgphc,gh->kgcp", e.reshape(4, 8, P, 8, C), _EYE8()).reshape(G, C, P)


SMALL = ("norm_g", "b_in", "ssm_log_dt", "ssm_lam_re", "ssm_lam_im", "ssm_b_re", "ssm_b_im",
         "ssm_c_re", "ssm_c_im", "ssm_d", "ssm_b_glu", "pool_w", "pool_scale")
BIG = ("w_in", "ssm_w_glu", "w_branch_a", "w_branch_b", "w_out")


def _local_step(x, target, sp, final_norm_g, wg):
    saved = []
    for l in range(DEPTH):
        disc_in = (sp["ssm_log_dt"][l], sp["ssm_lam_re"][l], sp["ssm_lam_im"][l], sp["ssm_b_re"][l], sp["ssm_b_im"][l])
        (abar_re, abar_im, bbar_re, bbar_im), disc_vjp = jax.vjp(_discretize, *disc_in)
        ap_re, ap_im = _powers(abar_re, abar_im)
        b_re_x, b_im_x = _expand_in(bbar_re), _expand_in(bbar_im)
        c_re_x, c_imn_x = _expand_out(sp["ssm_c_re"][l]), _expand_out(-sp["ssm_c_im"][l])
        g = sp["norm_g"][l].reshape(1, D)
        dsk = sp["ssm_d"][l].reshape(1, W)
        scale = sp["pool_scale"][l].reshape(1, W)
        pw = sp["pool_w"][l].astype(BF16)

        h, proj = _norm_inproj(l, x, g, wg["w_in"][l], sp["b_in"][l].reshape(1, NIN))
        sre, sim, y1, q, ya = _s5_fwd(
            l, proj, jnp.concatenate([b_re_x, b_im_x], axis=2).astype(BF16), c_re_x.astype(BF16), c_imn_x.astype(BF16),
            ap_re, ap_im, dsk, wg["ssm_w_glu"][l], sp["ssm_b_glu"][l].reshape(1, W))
        pooled, mixed, yb = _pool_fwd(l, proj, pw, scale)
        pa, pb, mg, x_next = _merge_out(l, ya, yb, proj, x, wg["w_branch_a"][l], wg["w_branch_b"][l], wg["w_out"][l])
        saved.append(dict(x=x, g=g, dsk=dsk, scale=scale, pw=pw, h=h, proj=proj, sre=sre, sim=sim, y1=y1, q=q, ya=ya,
                          pooled=pooled, mixed=mixed, yb=yb, pa=pa, pb=pb, mg=mg, disc_vjp=disc_vjp,
                          ap_re=ap_re, ap_im=ap_im, b_re_x=b_re_x, b_im_x=b_im_x, c_re_x=c_re_x, c_imn_x=c_imn_x))
        x = x_next

    loss, dx, dgf = _loss_head(x, final_norm_g.reshape(1, D), target)

    gs = {n: [None] * DEPTH for n in SMALL}
    gb = {n: [None] * DEPTH for n in BIG}
    for l in reversed(range(DEPTH)):
        s = saved[l]
        dgab, dya, dyb, dwo, dwa, dwb = _merge_out_bwd(
            l, dx, s["mg"], s["proj"], s["pa"], s["pb"], s["ya"], s["yb"],
            wg["w_out"][l], wg["w_branch_a"][l], wg["w_branch_b"][l])
        dbz, dpw, dsc = _pool_bwd(l, dyb, s["proj"], s["mixed"], s["pooled"], s["pw"], s["scale"])
        t = lambda a: jnp.swapaxes(a, 1, 2).astype(BF16)
        (daz, dwg, dbg, dd, dcre, dcimn, dbre, dbim, dare, daim) = _s5_bwd(
            l, dya, s["proj"], s["y1"], s["q"], s["sre"], s["sim"],
            t(s["c_re_x"]), t(s["c_imn_x"]), t(s["b_re_x"]), t(s["b_im_x"]),
            s["ap_re"], -s["ap_im"], s["ap_re"][::-1], -s["ap_im"][::-1], s["dsk"], wg["ssm_w_glu"][l])
        dproj = jnp.concatenate([daz, dbz, dgab], axis=1)
        dwin, dbin = _inproj_dw(l, s["h"], dproj)
        dx, dg = _inproj_dx(l, dproj, wg["w_in"][l], s["x"], s["g"], dx)

        d_abar_re = jnp.sum(dare, axis=0).reshape(G, P)
        d_abar_im = jnp.sum(daim, axis=0).reshape(G, P)
        dlog_dt, dlam_re, dlam_im, db_re, db_im = s["disc_vjp"](
            (d_abar_re, d_abar_im, _extract_in(dbre), _extract_in(dbim)))
        gs["norm_g"][l] = dg.reshape(D)
        gs["b_in"][l] = dbin.reshape(NIN)
        gs["ssm_log_dt"][l] = dlog_dt
        gs["ssm_lam_re"][l] = dlam_re
        gs["ssm_lam_im"][l] = dlam_im
        gs["ssm_b_re"][l] = db_re
        gs["ssm_b_im"][l] = db_im
        gs["ssm_c_re"][l] = _extract_out(dcre)
        gs["ssm_c_im"][l] = -_extract_out(dcimn)
        gs["ssm_d"][l] = dd.reshape(W)
        gs["ssm_b_glu"][l] = dbg.reshape(W)
        gs["pool_w"][l] = dpw
        gs["pool_scale"][l] = dsc.reshape(W)
        gb["w_in"][l] = dwin
        gb["ssm_w_glu"][l] = dwg
        gb["w_branch_a"][l] = dwa
        gb["w_branch_b"][l] = dwb
        gb["w_out"][l] = dwo
    gs = {n: jnp.stack(v) for n, v in gs.items()}
    gb = {n: jnp.stack(v) for n, v in gb.items()}
    return loss, dx, gs, gb, dgf


_ANY = pl.BlockSpec(memory_space=pl.ANY)


def _place():
    x, y, c = lax.axis_index("x"), lax.axis_index("y"), lax.axis_index("c")
    chips = [(1 - x, y), (x, 1 - y), (1 - x, 1 - y)]
    return x, y, c, 2 * x + y, chips, [2 * cx + cy for cx, cy in chips]


def _remote(src, dst, ssem, rsem, dev):
    return pltpu.make_async_remote_copy(src_ref=src, dst_ref=dst, send_sem=ssem, recv_sem=rsem,
                                        device_id=dev, device_id_type=MESH)


def _chip_allgather(name, arrs):
    n = len(arrs)

    def body(*refs):
        ins, outs = refs[:n], refs[n:2 * n]
        ssem, rsem, fsem, grsem, lsem = refs[2 * n:]
        x, y, c, me, chips, cid = _place()
        sib = (x, y, 1 - c)
        local = []
        for a in range(n):
            for h in range(2):
                cp = pltpu.make_async_copy(ins[a].at[h], outs[a].at[h, me], lsem.at[a, h])
                cp.start()
                local.append(cp)
        sends = []
        for a in range(n):
            for j, (cx, cy) in enumerate(chips):
                cp = _remote(ins[a].at[c], outs[a].at[c, me], ssem.at[a, j], rsem.at[a, j], (cx, cy, c))
                cp.start()
                sends.append(cp)
        for a in range(n):
            for j, (cx, cy) in enumerate(chips):
                blk = outs[a].at[c, cid[j]]
                _remote(blk, blk, ssem.at[a, j], rsem.at[a, j], (cx, cy, c)).wait_recv()
                cp = _remote(blk, blk, fsem.at[a, j], grsem.at[a, j], sib)
                cp.start()
                sends.append(cp)
        for a in range(n):
            for j in range(3):
                blk = outs[a].at[1 - c, cid[j]]
                _remote(blk, blk, fsem.at[a, j], grsem.at[a, j], sib).wait_recv()
        for cp in sends:
            cp.wait_send()
        for cp in local:
            cp.wait()

    return pl.pallas_call(
        body, name=name,
        in_specs=[_ANY] * n, out_specs=[_ANY] * n,
        out_shape=[SDS((2, 4) + a.shape[1:], a.dtype) for a in arrs],
        scratch_shapes=[pltpu.SemaphoreType.DMA((n, 3))] * 4 + [pltpu.SemaphoreType.DMA((n, 2))],
    )(*arrs)


def _sibling_exchange(name, arrs, halves):
    n = len(arrs)

    def body(*refs):
        ins, outs = refs[:n], refs[n:2 * n]
        ssem, rsem = refs[2 * n:]
        x, y, c, _, _, _ = _place()
        cps = []
        for a in range(n):
            src = ins[a].at[1 - c] if halves[a] else ins[a]
            cp = _remote(src, outs[a], ssem.at[a], rsem.at[a], (x, y, 1 - c))
            cp.start()
            cps.append(cp)
        for cp in cps:
            cp.wait()

    return pl.pallas_call(
        body, name=name,
        in_specs=[_ANY] * n, out_specs=[_ANY] * n,
        out_shape=[SDS(a.shape[1:] if hv else a.shape, a.dtype) for a, hv in zip(arrs, halves)],
        scratch_shapes=[pltpu.SemaphoreType.DMA((n,))] * 2,
    )(*arrs)


def _chip_scatter(name, parts):
    n = len(parts)

    def body(*refs):
        ins, outs = refs[:n], refs[n:2 * n]
        ssem, rsem = refs[2 * n:]
        _, _, c, _, chips, cid = _place()
        cps = []
        for a in range(n):
            for j, (cx, cy) in enumerate(chips):
                cp = _remote(ins[a].at[cid[j]], outs[a].at[j], ssem.at[a, j], rsem.at[a, j], (cx, cy, c))
                cp.start()
                cps.append(cp)
        for cp in cps:
            cp.wait()

    return pl.pallas_call(
        body, name=name,
        in_specs=[_ANY] * n, out_specs=[_ANY] * n,
        out_shape=[SDS((3,) + a.shape[1:], a.dtype) for a in parts],
        scratch_shapes=[pltpu.SemaphoreType.DMA((n, 3))] * 2,
    )(*parts)


def _sibling_assemble(name, arrs):
    n = len(arrs)

    def body(*refs):
        ins, outs = refs[:n], refs[n:2 * n]
        ssem, rsem, lsem = refs[2 * n:]
        x, y, c, _, _, _ = _place()
        cps = []
        for a in range(n):
            lc = pltpu.make_async_copy(ins[a], outs[a].at[c], lsem.at[a])
            lc.start()
            cps.append(lc)
        rem = []
        for a in range(n):
            cp = _remote(ins[a], outs[a].at[c], ssem.at[a], rsem.at[a], (x, y, 1 - c))
            cp.start()
            rem.append(cp)
        for a in range(n):
            _remote(ins[a], outs[a].at[1 - c], ssem.at[a], rsem.at[a], (x, y, 1 - c)).wait_recv()
        for cp in rem:
            cp.wait_send()
        for cp in cps:
            cp.wait()

    return pl.pallas_call(
        body, name=name,
        in_specs=[_ANY] * n, out_specs=[_ANY] * n,
        out_shape=[SDS((2,) + a.shape, a.dtype) for a in arrs],
        scratch_shapes=[pltpu.SemaphoreType.DMA((n,))] * 3,
    )(*arrs)


def _pair_sum_bf16(name, place, own, recv):
    _, _, a_, b_ = own.shape
    ta = min(a_, 256)

    def body(p_ref, own_ref, recv_ref, out_ref):
        out_ref[...] = (own_ref[0] + recv_ref[...]).astype(BF16)

    return pl.pallas_call(
        body, name=name,
        grid_spec=pltpu.PrefetchScalarGridSpec(
            num_scalar_prefetch=1, grid=(4, a_ // ta),
            in_specs=[pl.BlockSpec((1, 1, ta, b_), lambda s, i, p: (p[0], s, i, 0)),
                      pl.BlockSpec((1, ta, b_), lambda s, i, p: (s, i, 0))],
            out_specs=pl.BlockSpec((1, ta, b_), lambda s, i, p: (s, i, 0))),
        out_shape=SDS((4, a_, b_), BF16),
        compiler_params=_params("arbitrary", "arbitrary"),
    )(place, own, recv)


def _shard_sum(name, place, own, recv, rbuf):
    _, _, a_, b_ = own.shape
    ta = min(a_, 256)

    def body(p_ref, own_ref, recv_ref, r_ref, out_ref):
        acc = own_ref[0, 0] + recv_ref[0]
        for j in range(3):
            acc = acc + r_ref[j].astype(F32)
        out_ref[...] = acc

    return pl.pallas_call(
        body, name=name,
        grid_spec=pltpu.PrefetchScalarGridSpec(
            num_scalar_prefetch=1, grid=(a_ // ta,),
            in_specs=[pl.BlockSpec((1, 1, ta, b_), lambda i, p: (p[0], p[1], i, 0)),
                      pl.BlockSpec((1, ta, b_), lambda i, p: (p[1], i, 0)),
                      pl.BlockSpec((3, ta, b_), lambda i, p: (0, i, 0))],
            out_specs=pl.BlockSpec((ta, b_), lambda i, p: (i, 0))),
        out_shape=SDS((a_, b_), F32),
        compiler_params=_params("arbitrary"),
    )(place, own, recv, rbuf)


def _add(name, a, b):
    r_, c_ = a.shape
    tr = min(r_, 208)

    def body(a_ref, b_ref, o_ref):
        o_ref[...] = a_ref[...] + b_ref[...]

    spec = pl.BlockSpec((tr, c_), lambda i: (i, 0))
    return pl.pallas_call(body, name=name, grid=(r_ // tr,), in_specs=[spec, spec], out_specs=spec,
                          out_shape=SDS((r_, c_), F32), compiler_params=_params("arbitrary"))(a, b)


def _adam_math(w, g, m, v):
    m = B1 * m + (1.0 - B1) * g
    v = B2 * v + (1.0 - B2) * (g * g)
    m_hat = m / (1.0 - B1 ** STEP)
    v_hat = v / (1.0 - B2 ** STEP)
    delta = -LR * (m_hat / (jnp.sqrt(v_hat) + EPS_A) + WD * w)
    return delta, m, v


def _adamw(name, w, g, m, v):
    r_, c_ = w.shape
    tr = min(r_, 256)

    def body(w_ref, g_ref, m_ref, v_ref, d_ref, mo_ref, vo_ref):
        d_ref[...], mo_ref[...], vo_ref[...] = _adam_math(w_ref[...], g_ref[...], m_ref[...], v_ref[...])

    spec = pl.BlockSpec((tr, c_), lambda i: (i, 0))
    return pl.pallas_call(body, name=name, grid=(r_ // tr,), in_specs=[spec] * 4, out_specs=[spec] * 3,
                          out_shape=[SDS((r_, c_), F32)] * 3, compiler_params=_params("arbitrary"))(w, g, m, v)


def _adamw_small(w, parts, m, v):
    _, _, r_, c_ = parts.shape

    def body(w_ref, p_ref, m_ref, v_ref, g_ref, d_ref, mo_ref, vo_ref):
        g = ((p_ref[0, 0] + p_ref[0, 1]) + p_ref[0, 2]) + p_ref[0, 3]
        g_ref[0] = g
        d_ref[0], mo_ref[0], vo_ref[0] = _adam_math(w_ref[0], g, m_ref[0], v_ref[0])

    spec = pl.BlockSpec((1, r_, c_), lambda i: (i, 0, 0))
    return pl.pallas_call(
        body, name="adamw_small", grid=(2,),
        in_specs=[spec, pl.BlockSpec((1, 4, r_, c_), lambda i: (i, 0, 0, 0)), spec, spec], out_specs=[spec] * 4,
        out_shape=[SDS((2, r_, c_), F32)] * 4, compiler_params=_params("arbitrary"))(w, parts, m, v)


WEIGHTS = ("norm_g", "w_in", "b_in", "ssm_log_dt", "ssm_lam_re", "ssm_lam_im", "ssm_b_re", "ssm_b_im", "ssm_c_re",
           "ssm_c_im", "ssm_d", "ssm_w_glu", "ssm_b_glu", "pool_w", "pool_scale", "w_branch_a", "w_branch_b", "w_out",
           "final_norm_g")
PACKED = SMALL + ("final_norm_g",)
PACK_ROWS = 208


def _pack(arrs):
    flat = jnp.concatenate([a.reshape(-1) for a in arrs])
    return jnp.pad(flat, (0, 2 * PACK_ROWS * 1024 - flat.shape[0])).reshape(2, PACK_ROWS, 1024)


def _unpack(packed, like):
    flat = packed.reshape(-1)
    out, off = [], 0
    for a in like:
        out.append(flat[off:off + a.size].reshape(a.shape))
        off += a.size
    return out


def kernel(x, norm_g, w_in, b_in, ssm_log_dt, ssm_lam_re, ssm_lam_im, ssm_b_re, ssm_b_im, ssm_c_re, ssm_c_im, ssm_d, ssm_w_glu, ssm_b_glu, pool_w, pool_scale, w_branch_a, w_branch_b, w_out, final_norm_g, loss_target, m_norm_g, m_w_in, m_b_in, m_ssm_log_dt, m_ssm_lam_re, m_ssm_lam_im, m_ssm_b_re, m_ssm_b_im, m_ssm_c_re, m_ssm_c_im, m_ssm_d, m_ssm_w_glu, m_ssm_b_glu, m_pool_w, m_pool_scale, m_w_branch_a, m_w_branch_b, m_w_out, m_final_norm_g, v_norm_g, v_w_in, v_b_in, v_ssm_log_dt, v_ssm_lam_re, v_ssm_lam_im, v_ssm_b_re, v_ssm_b_im, v_ssm_c_re, v_ssm_c_im, v_ssm_d, v_ssm_w_glu, v_ssm_b_glu, v_pool_w, v_pool_scale, v_w_branch_a, v_w_branch_b, v_w_out, v_final_norm_g):
    w = dict(zip(WEIGHTS, (norm_g, w_in, b_in, ssm_log_dt, ssm_lam_re, ssm_lam_im, ssm_b_re, ssm_b_im, ssm_c_re,
                           ssm_c_im, ssm_d, ssm_w_glu, ssm_b_glu, pool_w, pool_scale, w_branch_a, w_branch_b, w_out,
                           final_norm_g)))
    m = dict(zip(WEIGHTS, (m_norm_g, m_w_in, m_b_in, m_ssm_log_dt, m_ssm_lam_re, m_ssm_lam_im, m_ssm_b_re, m_ssm_b_im,
                           m_ssm_c_re, m_ssm_c_im, m_ssm_d, m_ssm_w_glu, m_ssm_b_glu, m_pool_w, m_pool_scale,
                           m_w_branch_a, m_w_branch_b, m_w_out, m_final_norm_g)))
    v = dict(zip(WEIGHTS, (v_norm_g, v_w_in, v_b_in, v_ssm_log_dt, v_ssm_lam_re, v_ssm_lam_im, v_ssm_b_re, v_ssm_b_im,
                           v_ssm_c_re, v_ssm_c_im, v_ssm_d, v_ssm_w_glu, v_ssm_b_glu, v_pool_w, v_pool_scale,
                           v_w_branch_a, v_w_branch_b, v_w_out, v_final_norm_g)))
    c = lax.axis_index("c")
    place = jnp.stack([c, 2 * lax.axis_index("x") + lax.axis_index("y")]).astype(jnp.int32)

    gathered = _chip_allgather("gather_weights", [w[n].astype(BF16) for n in BIG])
    wg = dict(zip(BIG, gathered))
    wg["ssm_w_glu"] = wg["ssm_w_glu"].reshape(DEPTH, W, W)
    wg["w_out"] = wg["w_out"].reshape(DEPTH, D, D)

    loss, dx, gs, gb, dgf = _local_step(x[0], loss_target[0], {n: w[n] for n in SMALL}, final_norm_g, wg)
    loss = lax.psum(loss[0, 0], ("x", "y", "c"))

    gb["ssm_w_glu"] = gb["ssm_w_glu"].reshape(DEPTH, 4, W // 4, W)
    gb["w_out"] = gb["w_out"].reshape(DEPTH, 4, D // 4, D)
    own = [gb[n] for n in BIG]
    small = _pack([gs[n] for n in SMALL] + [dgf.reshape(D)])
    recv = _sibling_exchange("grads_to_sibling", own + [small], [True] * len(BIG) + [False])
    pair_small = _add("small_pair_sum", small.reshape(2 * PACK_ROWS, 1024),
                      recv[-1].reshape(2 * PACK_ROWS, 1024)).reshape(2, PACK_ROWS, 1024)
    parts = [_pair_sum_bf16(f"pair_sum_{n}", place, o, r) for n, o, r in zip(BIG, own, recv)]
    rbufs = _chip_scatter("scatter_partials", parts)
    small_parts = _chip_allgather("gather_small", [pair_small])[0]
    shard = [_shard_sum(f"shard_sum_{n}", place, o, r, rb) for n, o, r, rb in zip(BIG, own, recv, rbufs)]
    gbig = dict(zip(BIG, _sibling_assemble("shards_to_sibling", shard)))

    grads, delta, new_m, new_v = {}, {}, {}, {}
    for n in BIG:
        shp = w[n].shape
        two_d = (shp[0] * shp[1], shp[2])
        d_, m_, v_ = _adamw(f"adamw_{n}", w[n].reshape(two_d), gbig[n].reshape(two_d), m[n].reshape(two_d),
                            v[n].reshape(two_d))
        grads[n], delta[n], new_m[n], new_v[n] = gbig[n], d_.reshape(shp), m_.reshape(shp), v_.reshape(shp)
    like = [w[n] for n in PACKED]
    g_, d_, m_, v_ = _adamw_small(_pack(like), small_parts, _pack([m[n] for n in PACKED]), _pack([v[n] for n in PACKED]))
    for dst, packed in ((grads, g_), (delta, d_), (new_m, m_), (new_v, v_)):
        dst.update(zip(PACKED, _unpack(packed, like)))

    return (loss, dx[None], *[grads[n] for n in WEIGHTS], *[delta[n] for n in WEIGHTS],
            *[new_m[n] for n in WEIGHTS], *[new_v[n] for n in WEIGHTS])
```

```python
import functools

import jax
import jax.numpy as jnp
from jax import lax
from jax.experimental import pallas as pl
from jax.experimental.pallas import tpu as pltpu

F32, BF16 = jnp.float32, jnp.bfloat16
SDS = jax.ShapeDtypeStruct
MESH = pl.DeviceIdType.MESH

DEPTH = 2
L = 2048
D = 1024
NIN = 4096
W = 512
G, P, C = 32, 64, 16
GP = G * P
WINS = (2, 4, 8, 16)
TM = 256
NT = L // TM
EPS = 1e-6
VMEM_LIMIT = 56 * 2**20

LR, B1, B2, EPS_A, WD, STEP = 0.001, 0.9, 0.999, 1e-08, 0.01, 10


def _params(*sem):
    return pltpu.CompilerParams(dimension_semantics=sem, vmem_limit_bytes=VMEM_LIMIT)


_ANY = pl.BlockSpec(memory_space=pl.ANY)


def _full(shape):
    zeros = (0,) * len(shape)
    return pl.BlockSpec(shape, lambda *_: zeros)


def _layer(l, shape):
    zeros = (0,) * len(shape)
    return pl.BlockSpec((None,) + shape, lambda *_: (l,) + zeros)


def _rows(width, col=0, reverse=False):
    if reverse:
        return pl.BlockSpec((TM, width), lambda i: (NT - 1 - i, col))
    return pl.BlockSpec((TM, width), lambda i: (i, col))


def _dot(a, b):
    return jnp.dot(a, b, preferred_element_type=F32)


def _dot_nt(a, b):
    return lax.dot_general(a, b, (((1,), (1,)), ((), ())), preferred_element_type=F32)


def _dot_tn(a, b):
    return lax.dot_general(a, b, (((0,), (0,)), ((), ())), preferred_element_type=F32)


_K0 = 0.7978845608028654
_K1 = 0.044715


def _gelu(x):
    return 0.5 * x * (1.0 + jnp.tanh(_K0 * (x + _K1 * (x * x * x))))


def _gelu_grad(x):
    t = jnp.tanh(_K0 * (x + _K1 * (x * x * x)))
    return 0.5 * (1.0 + t) + 0.5 * x * (1.0 - t * t) * (_K0 * (1.0 + 3.0 * _K1 * x * x))


def _sigmoid(x):
    return jax.nn.sigmoid(x)


def _norm_inproj(l, x, g, w, b):
    def body(x_ref, g_ref, w_ref, b_ref, h_ref, proj_ref):
        xv = x_ref[...]
        r = lax.rsqrt(jnp.mean(xv * xv, axis=-1, keepdims=True) + EPS)
        hb = ((xv * r) * g_ref[...]).astype(BF16)
        h_ref[...] = hb
        for j in range(4):
            cs = slice(j * 1024, (j + 1) * 1024)
            proj_ref[:, cs] = _dot(hb, w_ref[j]) + b_ref[:, cs]

    return pl.pallas_call(
        body, name=f"norm_inproj_l{l}", grid=(NT,),
        in_specs=[_rows(D), _full((1, D)), _layer(l, (4, D, 1024)), _full((1, NIN))],
        out_specs=[_rows(D), _rows(NIN)],
        out_shape=[SDS((L, D), BF16), SDS((L, NIN), F32)],
        compiler_params=_params("arbitrary"),
    )(x, g, w, b)


def _scan_tile(re_ref, im_ref, st_re, st_im, cr_re, cr_im, carry, reverse):
    rowid = lax.broadcasted_iota(jnp.int32, (8, 512), 0)

    def chunk(ci, carry):
        c = (TM // 8 - 1 - ci) if reverse else ci
        rows = pl.ds(pl.multiple_of(c * 8, 8), 8)
        new = []
        for lb in range(GP // 512):
            cols = slice(lb * 512, (lb + 1) * 512)
            vr = re_ref[rows, cols]
            vi = im_ref[rows, cols]
            for d, prow in ((1, 0), (2, 1), (4, 3)):
                ar = st_re[prow:prow + 1, cols]
                ai = st_im[prow:prow + 1, cols]
                if reverse:
                    keep = rowid < 8 - d
                    sr = jnp.where(keep, pltpu.roll(vr, 8 - d, 0), 0.0)
                    si = jnp.where(keep, pltpu.roll(vi, 8 - d, 0), 0.0)
                else:
                    keep = rowid >= d
                    sr = jnp.where(keep, pltpu.roll(vr, d, 0), 0.0)
                    si = jnp.where(keep, pltpu.roll(vi, d, 0), 0.0)
                vr, vi = vr + ar * sr - ai * si, vi + ar * si + ai * sr
            cr, ci_ = carry[2 * lb], carry[2 * lb + 1]
            pr = cr_re[:, cols]
            pi = cr_im[:, cols]
            vr, vi = vr + pr * cr - pi * ci_, vi + pr * ci_ + pi * cr
            re_ref[rows, cols] = vr
            im_ref[rows, cols] = vi
            if reverse:
                new += [vr[0:1], vi[0:1]]
            else:
                new += [vr[7:8], vi[7:8]]
        return tuple(new)

    return lax.fori_loop(0, TM // 8, chunk, carry)


def _load_carry(car_ref):
    return tuple(car_ref[r:r + 1, lb * 512:(lb + 1) * 512] for lb in range(GP // 512) for r in (0, 1))


def _store_carry(car_ref, carry):
    for lb in range(GP // 512):
        car_ref[0:1, lb * 512:(lb + 1) * 512] = carry[2 * lb]
        car_ref[1:2, lb * 512:(lb + 1) * 512] = carry[2 * lb + 1]


def _s5_fwd(l, proj, bexp, cre, cimn, ap_re, ap_im, dsk, wglu, bglu):
    def body(ua_ref, za_ref, bexp_ref, cre_ref, cimn_ref, apr_ref, api_ref, d_ref, wg_ref, bg_ref,
             sre_ref, sim_ref, y1_ref, q_ref, ya_ref, car_ref):
        @pl.when(pl.program_id(0) == 0)
        def _():
            car_ref[...] = jnp.zeros_like(car_ref)

        u = ua_ref[...]
        ub = u.astype(BF16)
        for k in range(4):
            bu = _dot(ub[:, 128 * k:128 * (k + 1)], bexp_ref[k])
            sre_ref[:, 512 * k:512 * (k + 1)] = bu[:, :512]
            sim_ref[:, 512 * k:512 * (k + 1)] = bu[:, 512:]
        carry = _scan_tile(sre_ref, sim_ref, apr_ref, api_ref, apr_ref, api_ref, _load_carry(car_ref), False)
        _store_carry(car_ref, carry)
        for k in range(4):
            blk = slice(512 * k, 512 * (k + 1))
            ks = slice(128 * k, 128 * (k + 1))
            y0 = _dot(sre_ref[:, blk].astype(BF16), cre_ref[k]) + _dot(sim_ref[:, blk].astype(BF16), cimn_ref[k])
            y1_ref[:, ks] = y0 + d_ref[:, ks] * u[:, ks]
        y2 = _gelu(y1_ref[...])
        q = _dot(y2.astype(BF16), wg_ref[...]) + bg_ref[...]
        q_ref[...] = q
        za = za_ref[...]
        ya_ref[...] = ((y2 * _sigmoid(q)) * (za * _sigmoid(za))).astype(BF16)

    return pl.pallas_call(
        body, name=f"s5_fwd_l{l}", grid=(NT,),
        in_specs=[_rows(W, 0), _rows(W, 1), _full((4, 128, 1024)), _full((4, 512, 128)), _full((4, 512, 128)),
                  _full((8, GP)), _full((8, GP)), _full((1, W)), _layer(l, (W, W)), _full((1, W))],
        out_specs=[_rows(GP), _rows(GP), _rows(W), _rows(W), _rows(W)],
        out_shape=[SDS((L, GP), F32), SDS((L, GP), F32), SDS((L, W), F32), SDS((L, W), F32), SDS((L, W), BF16)],
        scratch_shapes=[pltpu.VMEM((8, GP), F32)],
        compiler_params=_params("arbitrary"),
    )(proj, proj, bexp, cre, cimn, ap_re, ap_im, dsk, wglu, bglu)


def _pool_fwd(l, proj, pw, scale):
    def body(ub_ref, zb_ref, pw_ref, sc_ref, pooled_ref, mixed_ref, yb_ref, buf):
        i = pl.program_id(0)

        @pl.when(i == 0)
        def _():
            buf[0:16, :] = jnp.zeros((16, W), F32)

        u = ub_ref[...]
        buf[16:16 + TM, :] = u
        t = i * TM + lax.broadcasted_iota(jnp.int32, (TM, 128), 0)
        for gi, win in enumerate(WINS):
            cs = slice(128 * gi, 128 * (gi + 1))
            acc = u[:, cs]
            for k in range(1, win):
                acc = acc + buf[16 - k:16 - k + TM, cs]
            cnt = jnp.minimum(t + 1, win).astype(F32)
            pb = (acc / cnt - u[:, cs]).astype(BF16)
            pooled_ref[:, cs] = pb
            mixed_ref[:, cs] = _dot(pb, pw_ref[gi])
        zb = zb_ref[...]
        yb_ref[...] = ((mixed_ref[...] * sc_ref[...]) * (zb * _sigmoid(zb))).astype(BF16)
        buf[0:16, :] = buf[TM:TM + 16, :]

    return pl.pallas_call(
        body, name=f"pool_fwd_l{l}", grid=(NT,),
        in_specs=[_rows(W, 2), _rows(W, 3), _full((4, 128, 128)), _full((1, W))],
        out_specs=[_rows(W), _rows(W), _rows(W)],
        out_shape=[SDS((L, W), BF16), SDS((L, W), F32), SDS((L, W), BF16)],
        scratch_shapes=[pltpu.VMEM((TM + 16, W), F32)],
        compiler_params=_params("arbitrary"),
    )(proj, proj, pw, scale)


def _merge_out(l, ya, yb, proj, x, wa, wb, wo):
    def body(ya_ref, yb_ref, ga_ref, gb_ref, x_ref, wa_ref, wb_ref, wo_ref, pa_ref, pb_ref, mg_ref, xo_ref):
        ya = ya_ref[...]
        yb = yb_ref[...]
        for j in range(4):
            cs = slice(256 * j, 256 * (j + 1))
            pa_ref[:, cs] = _dot(ya, wa_ref[j])
            pb_ref[:, cs] = _dot(yb, wb_ref[j])
        merged = _sigmoid(ga_ref[...]) * pa_ref[...] + _sigmoid(gb_ref[...]) * pb_ref[...]
        mb = merged.astype(BF16)
        mg_ref[...] = mb
        xo_ref[...] = x_ref[...] + _dot(mb, wo_ref[...])

    return pl.pallas_call(
        body, name=f"merge_out_l{l}", grid=(NT,),
        in_specs=[_rows(W), _rows(W), _rows(D, 2), _rows(D, 3), _rows(D),
                  _layer(l, (4, W, 256)), _layer(l, (4, W, 256)), _layer(l, (D, D))],
        out_specs=[_rows(D), _rows(D), _rows(D), _rows(D)],
        out_shape=[SDS((L, D), F32), SDS((L, D), F32), SDS((L, D), BF16), SDS((L, D), F32)],
        compiler_params=_params("arbitrary"),
    )(ya, yb, proj, proj, x, wa, wb, wo)


def _loss_head(x, gf, target):
    def body(x_ref, g_ref, t_ref, loss_ref, dx_ref, dg_ref):
        @pl.when(pl.program_id(0) == 0)
        def _():
            loss_ref[...] = jnp.zeros_like(loss_ref)
            dg_ref[...] = jnp.zeros_like(dg_ref)

        xv = x_ref[...]
        g = g_ref[...]
        r = lax.rsqrt(jnp.mean(xv * xv, axis=-1, keepdims=True) + EPS)
        xn = xv * r
        err = xn * g - t_ref[...]
        part = jnp.sum(jnp.mean(err * err, axis=-1, keepdims=True), axis=0, keepdims=True)
        loss_ref[...] += 0.5 * part
        dy = err * (1.0 / D)
        dg_ref[...] += jnp.sum(dy * xn, axis=0, keepdims=True)
        dxn = dy * g
        dx_ref[...] = r * (dxn - xn * jnp.mean(dxn * xn, axis=-1, keepdims=True))

    return pl.pallas_call(
        body, name="loss_head", grid=(NT,),
        in_specs=[_rows(D), _full((1, D)), _rows(D)],
        out_specs=[_full((2, 128)), _rows(D), _full((1, D))],
        out_shape=[SDS((2, 128), F32), SDS((L, D), F32), SDS((1, D), F32)],
        compiler_params=_params("arbitrary"),
    )(x, gf, target)


def _acc_specs(l, prevs, shapes):
    out_specs = [_layer(l, s) for s in shapes]
    out_shape = [SDS((DEPTH,) + s, F32) for s in shapes]
    if prevs is None:
        return [], (), out_specs, out_shape
    return [_ANY] * len(shapes), tuple(prevs), out_specs, out_shape


def _merge_out_bwd(l, dxn, mg, proj, pa, pb, ya, yb, wo, wa, wb, prevs):
    acc_in, acc_args, acc_specs, acc_shape = _acc_specs(l, prevs, [(D, D), (4, W, 256), (4, W, 256)])

    def body(dx_ref, mg_ref, ga_ref, gb_ref, pa_ref, pb_ref, ya_ref, yb_ref, wo_ref, wa_ref, wb_ref, *rest):
        dg_ref, dya_ref, dyb_ref, dwo_ref, dwa_ref, dwb_ref = rest[len(acc_args):]

        @pl.when(pl.program_id(0) == 0)
        def _():
            dwo_ref[...] = jnp.zeros_like(dwo_ref)
            dwa_ref[...] = jnp.zeros_like(dwa_ref)
            dwb_ref[...] = jnp.zeros_like(dwb_ref)

        dxb = dx_ref[...].astype(BF16)
        dm = _dot_nt(dxb, wo_ref[...])
        sa = _sigmoid(ga_ref[...])
        sb = _sigmoid(gb_ref[...])
        dg_ref[:, :D] = dm * pa_ref[...] * (sa * (1.0 - sa))
        dg_ref[:, D:] = dm * pb_ref[...] * (sb * (1.0 - sb))
        dpa = (dm * sa).astype(BF16)
        dpb = (dm * sb).astype(BF16)
        ya = ya_ref[...]
        yb = yb_ref[...]
        dya = jnp.zeros((TM, W), F32)
        dyb = jnp.zeros((TM, W), F32)
        for j in range(4):
            cs = slice(256 * j, 256 * (j + 1))
            dya = dya + _dot_nt(dpa[:, cs], wa_ref[j])
            dyb = dyb + _dot_nt(dpb[:, cs], wb_ref[j])
            dwa_ref[j] += _dot_tn(ya, dpa[:, cs])
            dwb_ref[j] += _dot_tn(yb, dpb[:, cs])
        dya_ref[...] = dya
        dyb_ref[...] = dyb
        dwo_ref[...] += _dot_tn(mg_ref[...], dxb)

    return pl.pallas_call(
        body, name=f"merge_out_bwd_l{l}", grid=(NT,),
        in_specs=[_rows(D), _rows(D), _rows(D, 2), _rows(D, 3), _rows(D), _rows(D), _rows(W), _rows(W),
                  _layer(l, (D, D)), _layer(l, (4, W, 256)), _layer(l, (4, W, 256))] + acc_in,
        out_specs=[_rows(2 * D, 1), _rows(W), _rows(W)] + acc_specs,
        out_shape=[SDS((L, NIN), F32), SDS((L, W), F32), SDS((L, W), F32)] + acc_shape,
        input_output_aliases={11 + k: 3 + k for k in range(len(acc_args))},
        compiler_params=_params("arbitrary"),
    )(dxn, mg, proj, proj, pa, pb, ya, yb, wo, wa, wb, *acc_args)


def _pool_bwd(l, dyb, proj, mixed, pooled, pw, scale, dproj):
    def body(dyb_ref, zb_ref, mixed_ref, pooled_ref, pw_ref, sc_ref, _, db_ref, dpw_ref, dsc_ref, buf):
        i = pl.program_id(0)
        tile = NT - 1 - i

        @pl.when(i == 0)
        def _():
            dpw_ref[...] = jnp.zeros_like(dpw_ref)
            dsc_ref[...] = jnp.zeros_like(dsc_ref)
            buf[TM:TM + 16, :] = jnp.zeros((16, W), F32)

        dyb = dyb_ref[...]
        zb = zb_ref[...]
        mixed = mixed_ref[...]
        sc = sc_ref[...]
        sg = _sigmoid(zb)
        dyb0 = dyb * (zb * sg)
        db_ref[:, W:] = dyb * (mixed * sc) * (sg * (1.0 + zb * (1.0 - sg)))
        dsc_ref[...] += jnp.sum(dyb0 * mixed, axis=0, keepdims=True)
        dmix = (dyb0 * sc).astype(BF16)
        t = tile * TM + lax.broadcasted_iota(jnp.int32, (TM, 128), 0)
        for gi, win in enumerate(WINS):
            cs = slice(128 * gi, 128 * (gi + 1))
            dpw_ref[gi] += _dot_tn(pooled_ref[:, cs], dmix[:, cs])
            dpool = _dot_nt(dmix[:, cs], pw_ref[gi])
            cnt = jnp.minimum(t + 1, win).astype(F32)
            e = dpool / cnt
            buf[0:TM, cs] = e
            acc = e - dpool
            for k in range(1, win):
                acc = acc + buf[k:k + TM, cs]
            db_ref[:, cs] = acc
        buf[TM:TM + 16, :] = buf[0:16, :]

    return pl.pallas_call(
        body, name=f"pool_bwd_l{l}", grid=(NT,),
        in_specs=[_rows(W, 0, True), _rows(W, 3, True), _rows(W, 0, True), _rows(W, 0, True),
                  _full((4, 128, 128)), _full((1, W)), _ANY],
        out_specs=[_rows(2 * W, 1, True), _full((4, 128, 128)), _full((1, W))],
        out_shape=[SDS((L, NIN), F32), SDS((4, 128, 128), F32), SDS((1, W), F32)],
        input_output_aliases={6: 0},
        scratch_shapes=[pltpu.VMEM((TM + 16, W), F32)],
        compiler_params=_params("arbitrary"),
    )(dyb, proj, mixed, pooled, pw, scale, dproj)


def _s5_bwd(l, dya, proj, y1, q, sre, sim, cret, cimnt, bret, bimt, st_re, st_im, cr_re, cr_im, dsk, wglu, dproj, prev):
    acc_in, acc_args, acc_specs, acc_shape = _acc_specs(l, prev, [(W, W)])
    def halo(i):
        return (jnp.maximum((NT - 1 - i) * (TM // 8) - 1, 0), 0)

    def body(dya_ref, ua_ref, za_ref, y1_ref, q_ref, sre_ref, sim_ref, hre_ref, him_ref,
             cret_ref, cimnt_ref, bret_ref, bimt_ref, st_re_ref, st_im_ref, cr_re_ref, cr_im_ref, d_ref, wg_ref,
             *rest):
        (da_ref, dwg_ref, dbg_ref, dd_ref, dcre_ref, dcimn_ref, dbre_ref, dbim_ref, dare_ref, daim_ref,
         lre, lim, car_ref) = rest[1 + len(acc_args):]
        i = pl.program_id(0)
        tile = NT - 1 - i

        @pl.when(i == 0)
        def _():
            for ref in (dwg_ref, dbg_ref, dd_ref, dcre_ref, dcimn_ref, dbre_ref, dbim_ref, dare_ref, daim_ref, car_ref):
                ref[...] = jnp.zeros_like(ref)

        u = ua_ref[...]
        za = za_ref[...]
        y1 = y1_ref[...]
        dya = dya_ref[...]
        y2 = _gelu(y1)
        sg = _sigmoid(q_ref[...])
        sgz = _sigmoid(za)
        dy3 = dya * (za * sgz)
        da_ref[:, W:] = dya * (y2 * sg) * (sgz * (1.0 + za * (1.0 - sgz)))
        dq = dy3 * y2 * (sg * (1.0 - sg))
        dqb = dq.astype(BF16)
        dy2 = dy3 * sg + _dot_nt(dqb, wg_ref[...])
        dwg_ref[...] += _dot_tn(y2.astype(BF16), dqb)
        dbg_ref[...] += jnp.sum(dq, axis=0, keepdims=True)
        dy1 = dy2 * _gelu_grad(y1)
        dd_ref[...] += jnp.sum(dy1 * u, axis=0, keepdims=True)
        dy1b = dy1.astype(BF16)
        ub = u.astype(BF16)
        for k in range(4):
            blk = slice(512 * k, 512 * (k + 1))
            ks = slice(128 * k, 128 * (k + 1))
            lre[:, blk] = _dot(dy1b[:, ks], cret_ref[k])
            lim[:, blk] = _dot(dy1b[:, ks], cimnt_ref[k])
            dcre_ref[k] += _dot_tn(sre_ref[:, blk].astype(BF16), dy1b[:, ks])
            dcimn_ref[k] += _dot_tn(sim_ref[:, blk].astype(BF16), dy1b[:, ks])
        carry = _scan_tile(lre, lim, st_re_ref, st_im_ref, cr_re_ref, cr_im_ref, _load_carry(car_ref), True)
        _store_carry(car_ref, carry)

        rowid = lax.broadcasted_iota(jnp.int32, (8, 512), 0)
        gate = (tile > 0).astype(F32)

        def chunk(c, _):
            rows = pl.ds(pl.multiple_of(c * 8, 8), 8)
            prows = pl.ds(pl.multiple_of(jnp.maximum(c - 1, 0) * 8, 8), 8)
            for lb in range(GP // 512):
                cols = slice(lb * 512, (lb + 1) * 512)
                sr = sre_ref[rows, cols]
                si = sim_ref[rows, cols]
                pr = jnp.where(c == 0, hre_ref[7:8, cols] * gate, sre_ref[prows, cols][7:8])
                pi = jnp.where(c == 0, him_ref[7:8, cols] * gate, sim_ref[prows, cols][7:8])
                sr = jnp.where(rowid == 0, pr, pltpu.roll(sr, 1, 0))
                si = jnp.where(rowid == 0, pi, pltpu.roll(si, 1, 0))
                lr = lre[rows, cols]
                li = lim[rows, cols]
                dare_ref[:, cols] += sr * lr + si * li
                daim_ref[:, cols] += sr * li - si * lr
            return 0

        lax.fori_loop(0, TM // 8, chunk, 0)

        for k in range(4):
            blk = slice(512 * k, 512 * (k + 1))
            ks = slice(128 * k, 128 * (k + 1))
            lrb = lre[:, blk].astype(BF16)
            lib = lim[:, blk].astype(BF16)
            da_ref[:, ks] = dy1[:, ks] * d_ref[:, ks] + _dot(lrb, bret_ref[k]) + _dot(lib, bimt_ref[k])
            dbre_ref[k] += _dot_tn(ub[:, ks], lrb)
            dbim_ref[k] += _dot_tn(ub[:, ks], lib)

    return pl.pallas_call(
        body, name=f"s5_bwd_l{l}", grid=(NT,),
        in_specs=[_rows(W, 0, True), _rows(W, 0, True), _rows(W, 1, True), _rows(W, 0, True), _rows(W, 0, True),
                  _rows(GP, 0, True), _rows(GP, 0, True),
                  pl.BlockSpec((8, GP), halo), pl.BlockSpec((8, GP), halo),
                  _full((4, 128, 512)), _full((4, 128, 512)), _full((4, 512, 128)), _full((4, 512, 128)),
                  _full((8, GP)), _full((8, GP)), _full((8, GP)), _full((8, GP)), _full((1, W)), _layer(l, (W, W)),
                  _ANY] + acc_in,
        out_specs=[_rows(2 * W, 0, True)] + acc_specs + [_full((1, W)), _full((1, W)),
                   _full((4, 512, 128)), _full((4, 512, 128)), _full((4, 128, 512)), _full((4, 128, 512)),
                   _full((8, GP)), _full((8, GP))],
        out_shape=[SDS((L, NIN), F32)] + acc_shape + [SDS((1, W), F32), SDS((1, W), F32),
                   SDS((4, 512, 128), F32), SDS((4, 512, 128), F32), SDS((4, 128, 512), F32), SDS((4, 128, 512), F32),
                   SDS((8, GP), F32), SDS((8, GP), F32)],
        input_output_aliases={19: 0, **{20 + k: 1 + k for k in range(len(acc_args))}},
        scratch_shapes=[pltpu.VMEM((TM, GP), F32), pltpu.VMEM((TM, GP), F32), pltpu.VMEM((8, GP), F32)],
        compiler_params=_params("arbitrary"),
    )(dya, proj, proj, y1, q, sre, sim, sre, sim, cret, cimnt, bret, bimt, st_re, st_im, cr_re, cr_im, dsk, wglu,
      dproj, *acc_args)


def _inproj_dw(l, h, dproj, prev):
    acc_in, acc_args = ([], ()) if prev is None else ([_ANY], (prev,))

    def body(h_ref, dp_ref, *rest):
        dw_ref, db_ref = rest[len(acc_args):]

        @pl.when(pl.program_id(1) == 0)
        def _():
            dw_ref[...] = jnp.zeros_like(dw_ref)
            db_ref[...] = jnp.zeros_like(db_ref)

        dp = dp_ref[...]
        dw_ref[...] += _dot_tn(h_ref[...], dp.astype(BF16))
        db_ref[...] += jnp.sum(dp, axis=0, keepdims=True)

    return pl.pallas_call(
        body, name=f"inproj_dw_l{l}", grid=(4, NT),
        in_specs=[pl.BlockSpec((TM, D), lambda j, i: (i, 0)), pl.BlockSpec((TM, 1024), lambda j, i: (i, j))] + acc_in,
        out_specs=[pl.BlockSpec((None, None, D, 1024), lambda j, i: (l, j, 0, 0)),
                   pl.BlockSpec((1, 1024), lambda j, i: (0, j))],
        out_shape=[SDS((DEPTH, 4, D, 1024), F32), SDS((1, NIN), F32)],
        input_output_aliases={2: 0} if acc_args else {},
        compiler_params=_params("arbitrary", "arbitrary"),
    )(h, dproj, *acc_args)


def _inproj_dx(l, dproj, w, x, g, dxn):
    def body(dp_ref, w_ref, x_ref, g_ref, dxn_ref, dx_ref, dg_ref):
        @pl.when(pl.program_id(0) == 0)
        def _():
            dg_ref[...] = jnp.zeros_like(dg_ref)

        dh = jnp.zeros((TM, D), F32)
        for j in range(4):
            dh = dh + _dot_nt(dp_ref[:, j * 1024:(j + 1) * 1024].astype(BF16), w_ref[j])
        xv = x_ref[...]
        r = lax.rsqrt(jnp.mean(xv * xv, axis=-1, keepdims=True) + EPS)
        xn = xv * r
        dg_ref[...] += jnp.sum(dh * xn, axis=0, keepdims=True)
        dn = dh * g_ref[...]
        dx_ref[...] = dxn_ref[...] + r * (dn - xn * jnp.mean(dn * xn, axis=-1, keepdims=True))

    return pl.pallas_call(
        body, name=f"inproj_dx_l{l}", grid=(NT,),
        in_specs=[_rows(NIN), _layer(l, (4, D, 1024)), _rows(D), _full((1, D)), _rows(D)],
        out_specs=[_rows(D), _full((1, D))],
        out_shape=[SDS((L, D), F32), SDS((1, D), F32)],
        compiler_params=_params("arbitrary"),
    )(dproj, w, x, g, dxn)


def _discretize(log_dt, lam_re, lam_im, b_re, b_im):
    dt = jnp.exp(log_dt)[:, None]
    mag = jnp.exp(lam_re * dt)
    ang = lam_im * dt
    abar_re = mag * jnp.cos(ang)
    abar_im = mag * jnp.sin(ang)
    num_re = abar_re - 1.0
    num_im = abar_im
    den = lam_re * lam_re + lam_im * lam_im
    coef_re = (num_re * lam_re + num_im * lam_im) / den
    coef_im = (num_im * lam_re - num_re * lam_im) / den
    bbar_re = coef_re[..., None] * b_re - coef_im[..., None] * b_im
    bbar_im = coef_re[..., None] * b_im + coef_im[..., None] * b_re
    return abar_re, abar_im, bbar_re, bbar_im


def _powers(abar_re, abar_im):
    ar, ai = abar_re.reshape(1, GP), abar_im.reshape(1, GP)
    rows_re, rows_im = [ar], [ai]
    for _ in range(7):
        pr, pi = rows_re[-1], rows_im[-1]
        rows_re.append(pr * ar - pi * ai)
        rows_im.append(pr * ai + pi * ar)
    neg_im = [-r for r in rows_im]
    return (jnp.concatenate(rows_re, axis=0), jnp.concatenate(rows_im, axis=0), jnp.concatenate(neg_im, axis=0),
            jnp.concatenate(rows_re[::-1], axis=0), jnp.concatenate(neg_im[::-1], axis=0))


_EYE8 = functools.partial(jnp.eye, 8, dtype=F32)


def _expand_in(b):
    return jnp.einsum("kgpc,gh->kgchp", b.reshape(4, 8, P, C), _EYE8()).reshape(4, 128, 512)


def _extract_in(e):
    return jnp.einsum("kgchp,gh->kgpc", e.reshape(4, 8, C, 8, P), _EYE8()).reshape(G, P, C)


def _expand_out(c):
    return jnp.einsum("kgcp,gh->kgphc", c.reshape(4, 8, C, P), _EYE8()).reshape(4, 512, 128)


def _extract_out(e):
    return jnp.einsum("kgphc,gh->kgcp", e.reshape(4, 8, P, 8, C), _EYE8()).reshape(G, C, P)


SMALL = ("norm_g", "b_in", "ssm_log_dt", "ssm_lam_re", "ssm_lam_im", "ssm_b_re", "ssm_b_im",
         "ssm_c_re", "ssm_c_im", "ssm_d", "ssm_b_glu", "pool_w", "pool_scale")
BIG = ("w_in", "ssm_w_glu", "w_branch_a", "w_branch_b", "w_out")


def _local_step(x, target, sp, final_norm_g, wg):
    saved = []
    for l in range(DEPTH):
        disc_in = (sp["ssm_log_dt"][l], sp["ssm_lam_re"][l], sp["ssm_lam_im"][l], sp["ssm_b_re"][l], sp["ssm_b_im"][l])
        (abar_re, abar_im, bbar_re, bbar_im), disc_vjp = jax.vjp(_discretize, *disc_in)
        pw_re, pw_im, pw_imn, pw_re_rev, pw_imn_rev = _powers(abar_re, abar_im)
        b_re_x, b_im_x = _expand_in(bbar_re), _expand_in(bbar_im)
        c_re_x, c_imn_x = _expand_out(sp["ssm_c_re"][l]), _expand_out(-sp["ssm_c_im"][l])
        g = sp["norm_g"][l].reshape(1, D)
        dsk = sp["ssm_d"][l].reshape(1, W)
        scale = sp["pool_scale"][l].reshape(1, W)
        pw = sp["pool_w"][l].astype(BF16)

        h, proj = _norm_inproj(l, x, g, wg["w_in"], sp["b_in"][l].reshape(1, NIN))
        sre, sim, y1, q, ya = _s5_fwd(
            l, proj, jnp.concatenate([b_re_x, b_im_x], axis=2).astype(BF16), c_re_x.astype(BF16), c_imn_x.astype(BF16),
            pw_re, pw_im, dsk, wg["ssm_w_glu"], sp["ssm_b_glu"][l].reshape(1, W))
        pooled, mixed, yb = _pool_fwd(l, proj, pw, scale)
        pa, pb, mg, x_next = _merge_out(l, ya, yb, proj, x, wg["w_branch_a"], wg["w_branch_b"], wg["w_out"])
        saved.append(dict(x=x, g=g, dsk=dsk, scale=scale, pw=pw, h=h, proj=proj, sre=sre, sim=sim, y1=y1, q=q, ya=ya,
                          pooled=pooled, mixed=mixed, yb=yb, pa=pa, pb=pb, mg=mg, disc_vjp=disc_vjp,
                          powers=(pw_re, pw_imn, pw_re_rev, pw_imn_rev),
                          b_re_x=b_re_x, b_im_x=b_im_x, c_re_x=c_re_x, c_imn_x=c_imn_x))
        x = x_next

    loss, dx, dgf = _loss_head(x, final_norm_g.reshape(1, D), target)

    gs = {n: [None] * DEPTH for n in SMALL}
    acc_mo, acc_glu, acc_in = None, None, None
    for l in reversed(range(DEPTH)):
        s = saved[l]
        dproj, dya, dyb, *acc_mo = _merge_out_bwd(
            l, dx, s["mg"], s["proj"], s["pa"], s["pb"], s["ya"], s["yb"],
            wg["w_out"], wg["w_branch_a"], wg["w_branch_b"], acc_mo)
        dproj, dpw, dsc = _pool_bwd(l, dyb, s["proj"], s["mixed"], s["pooled"], s["pw"], s["scale"], dproj)
        t = lambda a: jnp.swapaxes(a, 1, 2).astype(BF16)
        (dproj, dwg, dbg, dd, dcre, dcimn, dbre, dbim, dare, daim) = _s5_bwd(
            l, dya, s["proj"], s["y1"], s["q"], s["sre"], s["sim"],
            t(s["c_re_x"]), t(s["c_imn_x"]), t(s["b_re_x"]), t(s["b_im_x"]),
            *s["powers"], s["dsk"], wg["ssm_w_glu"], dproj, acc_glu)
        acc_glu = [dwg]
        acc_in, dbin = _inproj_dw(l, s["h"], dproj, acc_in)
        dx, dg = _inproj_dx(l, dproj, wg["w_in"], s["x"], s["g"], dx)

        d_abar_re = jnp.sum(dare, axis=0).reshape(G, P)
        d_abar_im = jnp.sum(daim, axis=0).reshape(G, P)
        dlog_dt, dlam_re, dlam_im, db_re, db_im = s["disc_vjp"](
            (d_abar_re, d_abar_im, _extract_in(dbre), _extract_in(dbim)))
        gs["norm_g"][l] = dg.reshape(D)
        gs["b_in"][l] = dbin.reshape(NIN)
        gs["ssm_log_dt"][l] = dlog_dt
        gs["ssm_lam_re"][l] = dlam_re
        gs["ssm_lam_im"][l] = dlam_im
        gs["ssm_b_re"][l] = db_re
        gs["ssm_b_im"][l] = db_im
        gs["ssm_c_re"][l] = _extract_out(dcre)
        gs["ssm_c_im"][l] = -_extract_out(dcimn)
        gs["ssm_d"][l] = dd.reshape(W)
        gs["ssm_b_glu"][l] = dbg.reshape(W)
        gs["pool_w"][l] = dpw
        gs["pool_scale"][l] = dsc.reshape(W)
    gs = {n: jnp.stack(v) for n, v in gs.items()}
    dwo, dwa, dwb = acc_mo
    gb = {"w_in": acc_in, "ssm_w_glu": acc_glu[0], "w_branch_a": dwa, "w_branch_b": dwb, "w_out": dwo}
    return loss, dx, gs, gb, dgf


def _place():
    x, y, c = lax.axis_index("x"), lax.axis_index("y"), lax.axis_index("c")
    chips = [(1 - x, y), (x, 1 - y), (1 - x, 1 - y)]
    return x, y, c, 2 * x + y, chips, [2 * cx + cy for cx, cy in chips]


def _remote(src, dst, ssem, rsem, dev):
    return pltpu.make_async_remote_copy(src_ref=src, dst_ref=dst, send_sem=ssem, recv_sem=rsem,
                                        device_id=dev, device_id_type=MESH)


def _chip_allgather(name, bufs, chip_major):
    n = len(bufs)

    def body(*refs):
        bufs_ = refs[n:2 * n]
        ssem, rsem, fsem, grsem = refs[2 * n:]
        x, y, c, me, chips, cid = _place()
        sib = (x, y, 1 - c)

        def blk(a, h, k):
            return bufs_[a].at[k, h] if chip_major else bufs_[a].at[h, k]

        sends = []
        for a in range(n):
            for j, (cx, cy) in enumerate(chips):
                cp = _remote(blk(a, c, me), blk(a, c, me), ssem.at[a, j], rsem.at[a, j], (cx, cy, c))
                cp.start()
                sends.append(cp)
        for a in range(n):
            for j, (cx, cy) in enumerate(chips):
                got = blk(a, c, cid[j])
                _remote(got, got, ssem.at[a, j], rsem.at[a, j], (cx, cy, c)).wait_recv()
                cp = _remote(got, got, fsem.at[a, j], grsem.at[a, j], sib)
                cp.start()
                sends.append(cp)
        for a in range(n):
            for j in range(3):
                got = blk(a, 1 - c, cid[j])
                _remote(got, got, fsem.at[a, j], grsem.at[a, j], sib).wait_recv()
        for cp in sends:
            cp.wait_send()

    return pl.pallas_call(
        body, name=name,
        in_specs=[_ANY] * n, out_specs=[_ANY] * n,
        out_shape=[SDS(a.shape, a.dtype) for a in bufs],
        input_output_aliases={a: a for a in range(n)},
        scratch_shapes=[pltpu.SemaphoreType.DMA((n, 3))] * 4,
    )(*bufs)


def _sibling_exchange(name, arrs, halves):
    n = len(arrs)

    def body(*refs):
        ins, outs = refs[:n], refs[n:2 * n]
        ssem, rsem = refs[2 * n:]
        x, y, c, _, _, _ = _place()
        cps = []
        for a in range(n):
            src = ins[a].at[1 - c] if halves[a] else ins[a]
            cp = _remote(src, outs[a], ssem.at[a], rsem.at[a], (x, y, 1 - c))
            cp.start()
            cps.append(cp)
        for cp in cps:
            cp.wait()

    return pl.pallas_call(
        body, name=name,
        in_specs=[_ANY] * n, out_specs=[_ANY] * n,
        out_shape=[SDS(a.shape[1:] if hv else a.shape, a.dtype) for a, hv in zip(arrs, halves)],
        scratch_shapes=[pltpu.SemaphoreType.DMA((n,))] * 2,
    )(*arrs)


def _chip_scatter(name, parts):
    n = len(parts)

    def body(*refs):
        ins, outs = refs[:n], refs[n:2 * n]
        ssem, rsem = refs[2 * n:]
        _, _, c, _, chips, cid = _place()
        cps = []
        for a in range(n):
            for j, (cx, cy) in enumerate(chips):
                cp = _remote(ins[a].at[cid[j]], outs[a].at[j], ssem.at[a, j], rsem.at[a, j], (cx, cy, c))
                cp.start()
                cps.append(cp)
        for cp in cps:
            cp.wait()

    return pl.pallas_call(
        body, name=name,
        in_specs=[_ANY] * n, out_specs=[_ANY] * n,
        out_shape=[SDS((3,) + a.shape[1:], a.dtype) for a in parts],
        scratch_shapes=[pltpu.SemaphoreType.DMA((n, 3))] * 2,
    )(*parts)


def _cast_own(place, ws):
    n = len(ws)

    def body(p_ref, *refs):
        for i_ref, o_ref in zip(refs[:n], refs[n:]):
            o_ref[...] = i_ref[...].astype(BF16)

    return pl.pallas_call(
        body, name="cast_own_shards",
        grid_spec=pltpu.PrefetchScalarGridSpec(
            num_scalar_prefetch=1, grid=(DEPTH,),
            in_specs=[pl.BlockSpec((None,) + a.shape[1:], lambda l, p: (l, 0, 0)) for a in ws],
            out_specs=[pl.BlockSpec((None, None) + a.shape[1:], lambda l, p: (l, p[1], 0, 0)) for a in ws]),
        out_shape=[SDS((DEPTH, 4) + a.shape[1:], BF16) for a in ws],
        compiler_params=_params("arbitrary"),
    )(place, *ws)


def _pair_sum_bf16(name, place, own, recv):
    _, _, a_, b_ = own.shape
    ta = min(a_, 256)

    def body(p_ref, own_ref, recv_ref, out_ref):
        out_ref[...] = (own_ref[...] + recv_ref[...]).astype(BF16)

    return pl.pallas_call(
        body, name=name,
        grid_spec=pltpu.PrefetchScalarGridSpec(
            num_scalar_prefetch=1, grid=(4, a_ // ta),
            in_specs=[pl.BlockSpec((None, None, ta, b_), lambda s, i, p: (p[0], s, i, 0)),
                      pl.BlockSpec((None, ta, b_), lambda s, i, p: (s, i, 0))],
            out_specs=pl.BlockSpec((None, ta, b_), lambda s, i, p: (s, i, 0))),
        out_shape=SDS((4, a_, b_), BF16),
        compiler_params=_params("arbitrary", "arbitrary"),
    )(place, own, recv)


def _shard_sum(name, place, own, recv, rbuf):
    _, _, a_, b_ = own.shape
    ta = min(a_, 256)

    def body(p_ref, own_ref, recv_ref, r_ref, out_ref):
        acc = own_ref[...] + recv_ref[...]
        for j in range(3):
            acc = acc + r_ref[j].astype(F32)
        out_ref[...] = acc

    return pl.pallas_call(
        body, name=name,
        grid_spec=pltpu.PrefetchScalarGridSpec(
            num_scalar_prefetch=1, grid=(a_ // ta,),
            in_specs=[pl.BlockSpec((None, None, ta, b_), lambda i, p: (p[0], p[1], i, 0)),
                      pl.BlockSpec((None, ta, b_), lambda i, p: (p[1], i, 0)),
                      pl.BlockSpec((3, ta, b_), lambda i, p: (0, i, 0))],
            out_specs=pl.BlockSpec((ta, b_), lambda i, p: (i, 0))),
        out_shape=SDS((a_, b_), F32),
        compiler_params=_params("arbitrary"),
    )(place, own, recv, rbuf)


def _small_pair_sum(place, mine, recv):
    n = len(mine)

    def body(p_ref, *refs):
        for m_ref, r_ref, o_ref in zip(refs[:n], refs[n:2 * n], refs[2 * n:]):
            o_ref[...] = m_ref[...] + r_ref[...]

    def whole(a):
        zeros = (0,) * a.ndim
        return pl.BlockSpec(a.shape, lambda i, p: zeros)

    def mine_blk(a):
        zeros = (0,) * a.ndim
        return pl.BlockSpec((None,) + a.shape, lambda i, p: (p[1],) + zeros)

    return pl.pallas_call(
        body, name="small_pair_sum",
        grid_spec=pltpu.PrefetchScalarGridSpec(
            num_scalar_prefetch=1, grid=(1,),
            in_specs=[whole(a) for a in mine] + [whole(a) for a in recv],
            out_specs=[mine_blk(a) for a in mine]),
        out_shape=[SDS((4,) + a.shape, F32) for a in mine],
        compiler_params=_params("arbitrary"),
    )(place, *mine, *recv)


def _adam_math(w, g, m, v):
    m = B1 * m + (1.0 - B1) * g
    v = B2 * v + (1.0 - B2) * (g * g)
    m_hat = m / (1.0 - B1 ** STEP)
    v_hat = v / (1.0 - B2 ** STEP)
    delta = -LR * (m_hat / (jnp.sqrt(v_hat) + EPS_A) + WD * w)
    return delta, m, v


def _adamw_big(name, place, w, m, v, mine, recv):
    _, a_, b_ = w.shape
    ta = min(a_, 256)

    def body(p_ref, w_ref, m_ref, v_ref, mine_ref, recv_ref, g_ref, d_ref, mo_ref, vo_ref):
        g = jnp.where(pl.program_id(0) == p_ref[0], mine_ref[...], recv_ref[...])
        g_ref[...] = g
        d_ref[...], mo_ref[...], vo_ref[...] = _adam_math(w_ref[...], g, m_ref[...], v_ref[...])

    slab = pl.BlockSpec((None, ta, b_), lambda h, i, p: (h, i, 0))
    half = pl.BlockSpec((ta, b_), lambda h, i, p: (i, 0))
    return pl.pallas_call(
        body, name=name,
        grid_spec=pltpu.PrefetchScalarGridSpec(
            num_scalar_prefetch=1, grid=(2, a_ // ta),
            in_specs=[slab, slab, slab, half, half], out_specs=[slab] * 4),
        out_shape=[SDS(w.shape, F32)] * 4,
        compiler_params=_params("arbitrary", "arbitrary"),
    )(place, w, m, v, mine, recv)


def _adamw_small(name, ws, parts, ms, vs, loss_parts=None):
    k = len(ws)
    extra = [] if loss_parts is None else [loss_parts]

    def chip_sum(p_ref):
        return ((p_ref[0] + p_ref[1]) + p_ref[2]) + p_ref[3]

    def body(*refs):
        w_refs, p_refs, m_refs, v_refs = refs[:k], refs[k:2 * k], refs[2 * k:3 * k], refs[3 * k:4 * k]
        outs = refs[4 * k + len(extra):]
        if extra:
            outs[0][...] = chip_sum(refs[4 * k])
            outs = outs[1:]
        for a in range(k):
            g = chip_sum(p_refs[a])
            outs[a][...] = g
            outs[k + a][...], outs[2 * k + a][...], outs[3 * k + a][...] = _adam_math(
                w_refs[a][...], g, m_refs[a][...], v_refs[a][...])

    like = [SDS(a.shape, F32) for a in ws]
    return pl.pallas_call(
        body, name=name,
        out_shape=([SDS((2, 128), F32)] if extra else []) + like * 4,
        compiler_params=pltpu.CompilerParams(vmem_limit_bytes=VMEM_LIMIT),
    )(*ws, *parts, *ms, *vs, *extra)


WEIGHTS = ("norm_g", "w_in", "b_in", "ssm_log_dt", "ssm_lam_re", "ssm_lam_im", "ssm_b_re", "ssm_b_im", "ssm_c_re",
           "ssm_c_im", "ssm_d", "ssm_w_glu", "ssm_b_glu", "pool_w", "pool_scale", "w_branch_a", "w_branch_b", "w_out",
           "final_norm_g")
REPLICATED = SMALL + ("final_norm_g",)
ADAM_GROUPS = (tuple(n for n in REPLICATED if n not in ("ssm_b_re", "ssm_b_im")), ("ssm_b_re",), ("ssm_b_im",))


def kernel(x, norm_g, w_in, b_in, ssm_log_dt, ssm_lam_re, ssm_lam_im, ssm_b_re, ssm_b_im, ssm_c_re, ssm_c_im, ssm_d, ssm_w_glu, ssm_b_glu, pool_w, pool_scale, w_branch_a, w_branch_b, w_out, final_norm_g, loss_target, m_norm_g, m_w_in, m_b_in, m_ssm_log_dt, m_ssm_lam_re, m_ssm_lam_im, m_ssm_b_re, m_ssm_b_im, m_ssm_c_re, m_ssm_c_im, m_ssm_d, m_ssm_w_glu, m_ssm_b_glu, m_pool_w, m_pool_scale, m_w_branch_a, m_w_branch_b, m_w_out, m_final_norm_g, v_norm_g, v_w_in, v_b_in, v_ssm_log_dt, v_ssm_lam_re, v_ssm_lam_im, v_ssm_b_re, v_ssm_b_im, v_ssm_c_re, v_ssm_c_im, v_ssm_d, v_ssm_w_glu, v_ssm_b_glu, v_pool_w, v_pool_scale, v_w_branch_a, v_w_branch_b, v_w_out, v_final_norm_g):
    w = dict(zip(WEIGHTS, (norm_g, w_in, b_in, ssm_log_dt, ssm_lam_re, ssm_lam_im, ssm_b_re, ssm_b_im, ssm_c_re,
                           ssm_c_im, ssm_d, ssm_w_glu, ssm_b_glu, pool_w, pool_scale, w_branch_a, w_branch_b, w_out,
                           final_norm_g)))
    m = dict(zip(WEIGHTS, (m_norm_g, m_w_in, m_b_in, m_ssm_log_dt, m_ssm_lam_re, m_ssm_lam_im, m_ssm_b_re, m_ssm_b_im,
                           m_ssm_c_re, m_ssm_c_im, m_ssm_d, m_ssm_w_glu, m_ssm_b_glu, m_pool_w, m_pool_scale,
                           m_w_branch_a, m_w_branch_b, m_w_out, m_final_norm_g)))
    v = dict(zip(WEIGHTS, (v_norm_g, v_w_in, v_b_in, v_ssm_log_dt, v_ssm_lam_re, v_ssm_lam_im, v_ssm_b_re, v_ssm_b_im,
                           v_ssm_c_re, v_ssm_c_im, v_ssm_d, v_ssm_w_glu, v_ssm_b_glu, v_pool_w, v_pool_scale,
                           v_w_branch_a, v_w_branch_b, v_w_out, v_final_norm_g)))
    place = jnp.stack([lax.axis_index("c"), 2 * lax.axis_index("x") + lax.axis_index("y")]).astype(jnp.int32)

    gathered = _chip_allgather("gather_weights", _cast_own(place, [w[n] for n in BIG]), False)
    wg = dict(zip(BIG, gathered))
    wg["ssm_w_glu"] = wg["ssm_w_glu"].reshape(DEPTH, W, W)
    wg["w_out"] = wg["w_out"].reshape(DEPTH, D, D)

    loss, dx, gs, gb, dgf = _local_step(x[0], loss_target[0], {n: w[n] for n in SMALL}, final_norm_g, wg)

    gb["ssm_w_glu"] = gb["ssm_w_glu"].reshape(DEPTH, 4, W // 4, W)
    gb["w_out"] = gb["w_out"].reshape(DEPTH, 4, D // 4, D)
    own = [gb[n] for n in BIG]
    small = [gs[n] for n in SMALL] + [dgf.reshape(2, D // 2), loss]
    recv = _sibling_exchange("grads_to_sibling", own + small, [True] * len(BIG) + [False] * len(small))
    pair_small = _small_pair_sum(place, small, recv[len(BIG):])
    parts = [_pair_sum_bf16(f"pair_sum_{n}", place, o, r) for n, o, r in zip(BIG, own, recv)]
    rbufs = _chip_scatter("scatter_partials", parts)
    small_parts = dict(zip(REPLICATED + ("loss",), _chip_allgather("gather_small", pair_small, True)))
    shard = [_shard_sum(f"shard_sum_{n}", place, o, r, rb) for n, o, r, rb in zip(BIG, own, recv, rbufs)]
    other = _sibling_exchange("shards_to_sibling", shard, [False] * len(BIG))

    grads, delta, new_m, new_v = {}, {}, {}, {}
    for n, mine, theirs in zip(BIG, shard, other):
        grads[n], delta[n], new_m[n], new_v[n] = _adamw_big(f"adamw_{n}", place, w[n], m[n], v[n], mine, theirs)
    half = lambda a: a.reshape(2, D // 2)
    w["final_norm_g"], m["final_norm_g"], v["final_norm_g"] = half(final_norm_g), half(m_final_norm_g), half(v_final_norm_g)
    total_loss = None
    for gi, names in enumerate(ADAM_GROUPS):
        outs = _adamw_small(f"adamw_small_{gi}", [w[n] for n in names], [small_parts[n] for n in names],
                            [m[n] for n in names], [v[n] for n in names], small_parts["loss"] if gi == 0 else None)
        if gi == 0:
            total_loss, outs = outs[0][0, 0], outs[1:]
        k = len(names)
        for dst, vals in ((grads, outs[:k]), (delta, outs[k:2 * k]), (new_m, outs[2 * k:3 * k]), (new_v, outs[3 * k:])):
            dst.update(zip(names, vals))
    for dst in (grads, delta, new_m, new_v):
        dst["final_norm_g"] = dst["final_norm_g"].reshape(D)

    return (total_loss, dx[None], *[grads[n] for n in WEIGHTS], *[delta[n] for n in WEIGHTS],
            *[new_m[n] for n in WEIGHTS], *[new_v[n] for n in WEIGHTS])
```

```python
import functools

import jax
import jax.numpy as jnp
from jax import lax
from jax.experimental import pallas as pl
from jax.experimental.pallas import tpu as pltpu

F32, BF16 = jnp.float32, jnp.bfloat16
SDS = jax.ShapeDtypeStruct
MESH = pl.DeviceIdType.MESH

DEPTH = 2
L = 2048
D = 1024
NIN = 4096
W = 512
G, P, C = 32, 64, 16
GP = G * P
WINS = (2, 4, 8, 16)
TM = 256
NT = L // TM
EPS = 1e-6
VMEM_LIMIT = 56 * 2**20

LR, B1, B2, EPS_A, WD, STEP = 0.001, 0.9, 0.999, 1e-08, 0.01, 10


def _params(*sem):
    return pltpu.CompilerParams(dimension_semantics=sem, vmem_limit_bytes=VMEM_LIMIT)


_ANY = pl.BlockSpec(memory_space=pl.ANY)


def _full(shape):
    zeros = (0,) * len(shape)
    return pl.BlockSpec(shape, lambda *_: zeros)


def _layer(l, shape):
    zeros = (0,) * len(shape)
    return pl.BlockSpec((None,) + shape, lambda *_: (l,) + zeros)


def _rows(width, col=0, reverse=False):
    if reverse:
        return pl.BlockSpec((TM, width), lambda i: (NT - 1 - i, col))
    return pl.BlockSpec((TM, width), lambda i: (i, col))


def _pcall(body, name, grid, in_specs, out_specs, out_shape, args, scratch=(), aliases=None, job=None):
    in_specs, out_specs, out_shape, args, scratch = list(in_specs), list(out_specs), list(out_shape), list(args), list(scratch)
    aliases = dict(aliases or {})
    n_out = len(out_specs)
    if job is not None:
        nj = len(job.bufs)
        body = job.wrap(body, len(in_specs), n_out, len(scratch), grid)
        aliases.update({len(in_specs) + k: n_out + k for k in range(nj)})
        in_specs += [_ANY] * nj
        out_specs += [_ANY] * nj
        out_shape += [SDS(b.shape, b.dtype) for b in job.bufs]
        scratch += job.scratch
        args += list(job.bufs)
    outs = pl.pallas_call(
        body, name=name, **({"grid": grid} if grid else {}), in_specs=in_specs, out_specs=out_specs, out_shape=out_shape,
        input_output_aliases=aliases, scratch_shapes=scratch,
        compiler_params=_params(*(("arbitrary",) * len(grid))))(*args)
    return list(outs[:n_out]), list(outs[n_out:])


def _dot(a, b):
    return jnp.dot(a, b, preferred_element_type=F32)


def _dot_nt(a, b):
    return lax.dot_general(a, b, (((1,), (1,)), ((), ())), preferred_element_type=F32)


def _dot_tn(a, b):
    return lax.dot_general(a, b, (((0,), (0,)), ((), ())), preferred_element_type=F32)


_K0 = 0.7978845608028654
_K1 = 0.044715


def _gelu(x):
    return 0.5 * x * (1.0 + jnp.tanh(_K0 * (x + _K1 * (x * x * x))))


def _gelu_grad(x):
    t = jnp.tanh(_K0 * (x + _K1 * (x * x * x)))
    return 0.5 * (1.0 + t) + 0.5 * x * (1.0 - t * t) * (_K0 * (1.0 + 3.0 * _K1 * x * x))


def _sigmoid(x):
    return jax.nn.sigmoid(x)


def _norm_inproj(l, x, g, w, b, job=None):
    def body(x_ref, g_ref, w_ref, b_ref, h_ref, proj_ref):
        xv = x_ref[...]
        r = lax.rsqrt(jnp.mean(xv * xv, axis=-1, keepdims=True) + EPS)
        hb = ((xv * r) * g_ref[...]).astype(BF16)
        h_ref[...] = hb
        for j in range(4):
            cs = slice(j * 1024, (j + 1) * 1024)
            proj_ref[:, cs] = _dot(hb, w_ref[j]) + b_ref[:, cs]

    return _pcall(
        body, f"norm_inproj_l{l}", (NT,),
        [_rows(D), _full((1, D)), _layer(l, (4, D, 1024)), _full((1, NIN))],
        [_rows(D), _rows(NIN)],
        [SDS((L, D), BF16), SDS((L, NIN), F32)],
        (x, g, w, b), job=job)


def _scan_tile(re_ref, im_ref, st_re, st_im, cr_re, cr_im, carry, reverse):
    rowid = lax.broadcasted_iota(jnp.int32, (8, 512), 0)

    def chunk(ci, carry):
        c = (TM // 8 - 1 - ci) if reverse else ci
        rows = pl.ds(pl.multiple_of(c * 8, 8), 8)
        new = []
        for lb in range(GP // 512):
            cols = slice(lb * 512, (lb + 1) * 512)
            vr = re_ref[rows, cols]
            vi = im_ref[rows, cols]
            for d, prow in ((1, 0), (2, 1), (4, 3)):
                ar = st_re[prow:prow + 1, cols]
                ai = st_im[prow:prow + 1, cols]
                if reverse:
                    keep = rowid < 8 - d
                    sr = jnp.where(keep, pltpu.roll(vr, 8 - d, 0), 0.0)
                    si = jnp.where(keep, pltpu.roll(vi, 8 - d, 0), 0.0)
                else:
                    keep = rowid >= d
                    sr = jnp.where(keep, pltpu.roll(vr, d, 0), 0.0)
                    si = jnp.where(keep, pltpu.roll(vi, d, 0), 0.0)
                vr, vi = vr + ar * sr - ai * si, vi + ar * si + ai * sr
            cr, ci_ = carry[2 * lb], carry[2 * lb + 1]
            pr = cr_re[:, cols]
            pi = cr_im[:, cols]
            vr, vi = vr + pr * cr - pi * ci_, vi + pr * ci_ + pi * cr
            re_ref[rows, cols] = vr
            im_ref[rows, cols] = vi
            if reverse:
                new += [vr[0:1], vi[0:1]]
            else:
                new += [vr[7:8], vi[7:8]]
        return tuple(new)

    return lax.fori_loop(0, TM // 8, chunk, carry)


def _load_carry(car_ref):
    return tuple(car_ref[r:r + 1, lb * 512:(lb + 1) * 512] for lb in range(GP // 512) for r in (0, 1))


def _store_carry(car_ref, carry):
    for lb in range(GP // 512):
        car_ref[0:1, lb * 512:(lb + 1) * 512] = carry[2 * lb]
        car_ref[1:2, lb * 512:(lb + 1) * 512] = carry[2 * lb + 1]


def _s5_fwd(l, proj, bexp, cre, cimn, ap_re, ap_im, dsk, wglu, bglu, job=None):
    def body(ua_ref, za_ref, bexp_ref, cre_ref, cimn_ref, apr_ref, api_ref, d_ref, wg_ref, bg_ref,
             sre_ref, sim_ref, y1_ref, q_ref, ya_ref, car_ref):
        @pl.when(pl.program_id(0) == 0)
        def _():
            car_ref[...] = jnp.zeros_like(car_ref)

        u = ua_ref[...]
        ub = u.astype(BF16)
        for k in range(4):
            bu = _dot(ub[:, 128 * k:128 * (k + 1)], bexp_ref[k])
            sre_ref[:, 512 * k:512 * (k + 1)] = bu[:, :512]
            sim_ref[:, 512 * k:512 * (k + 1)] = bu[:, 512:]
        carry = _scan_tile(sre_ref, sim_ref, apr_ref, api_ref, apr_ref, api_ref, _load_carry(car_ref), False)
        _store_carry(car_ref, carry)
        for k in range(4):
            blk = slice(512 * k, 512 * (k + 1))
            ks = slice(128 * k, 128 * (k + 1))
            y0 = _dot(sre_ref[:, blk].astype(BF16), cre_ref[k]) + _dot(sim_ref[:, blk].astype(BF16), cimn_ref[k])
            y1_ref[:, ks] = y0 + d_ref[:, ks] * u[:, ks]
        y2 = _gelu(y1_ref[...])
        q = _dot(y2.astype(BF16), wg_ref[...]) + bg_ref[...]
        q_ref[...] = q
        za = za_ref[...]
        ya_ref[...] = ((y2 * _sigmoid(q)) * (za * _sigmoid(za))).astype(BF16)

    return _pcall(
        body, f"s5_fwd_l{l}", (NT,),
        [_rows(W, 0), _rows(W, 1), _full((4, 128, 1024)), _full((4, 512, 128)), _full((4, 512, 128)),
         _full((8, GP)), _full((8, GP)), _full((1, W)), _layer(l, (W, W)), _full((1, W))],
        [_rows(GP), _rows(GP), _rows(W), _rows(W), _rows(W)],
        [SDS((L, GP), F32), SDS((L, GP), F32), SDS((L, W), F32), SDS((L, W), F32), SDS((L, W), BF16)],
        (proj, proj, bexp, cre, cimn, ap_re, ap_im, dsk, wglu, bglu),
        scratch=[pltpu.VMEM((8, GP), F32)], job=job)


def _pool_fwd(l, proj, pw, scale):
    def body(ub_ref, zb_ref, pw_ref, sc_ref, pooled_ref, mixed_ref, yb_ref, buf):
        i = pl.program_id(0)

        @pl.when(i == 0)
        def _():
            buf[0:16, :] = jnp.zeros((16, W), F32)

        u = ub_ref[...]
        buf[16:16 + TM, :] = u
        t = i * TM + lax.broadcasted_iota(jnp.int32, (TM, 128), 0)
        for gi, win in enumerate(WINS):
            cs = slice(128 * gi, 128 * (gi + 1))
            acc = u[:, cs]
            for k in range(1, win):
                acc = acc + buf[16 - k:16 - k + TM, cs]
            cnt = jnp.minimum(t + 1, win).astype(F32)
            pb = (acc / cnt - u[:, cs]).astype(BF16)
            pooled_ref[:, cs] = pb
            mixed_ref[:, cs] = _dot(pb, pw_ref[gi])
        zb = zb_ref[...]
        yb_ref[...] = ((mixed_ref[...] * sc_ref[...]) * (zb * _sigmoid(zb))).astype(BF16)
        buf[0:16, :] = buf[TM:TM + 16, :]

    return pl.pallas_call(
        body, name=f"pool_fwd_l{l}", grid=(NT,),
        in_specs=[_rows(W, 2), _rows(W, 3), _full((4, 128, 128)), _full((1, W))],
        out_specs=[_rows(W), _rows(W), _rows(W)],
        out_shape=[SDS((L, W), BF16), SDS((L, W), F32), SDS((L, W), BF16)],
        scratch_shapes=[pltpu.VMEM((TM + 16, W), F32)],
        compiler_params=_params("arbitrary"),
    )(proj, proj, pw, scale)


def _merge_out(l, ya, yb, proj, x, wa, wb, wo):
    def body(ya_ref, yb_ref, ga_ref, gb_ref, x_ref, wa_ref, wb_ref, wo_ref, pa_ref, pb_ref, mg_ref, xo_ref):
        ya = ya_ref[...]
        yb = yb_ref[...]
        for j in range(4):
            cs = slice(256 * j, 256 * (j + 1))
            pa_ref[:, cs] = _dot(ya, wa_ref[j])
            pb_ref[:, cs] = _dot(yb, wb_ref[j])
        merged = _sigmoid(ga_ref[...]) * pa_ref[...] + _sigmoid(gb_ref[...]) * pb_ref[...]
        mb = merged.astype(BF16)
        mg_ref[...] = mb
        xo_ref[...] = x_ref[...] + _dot(mb, wo_ref[...])

    return pl.pallas_call(
        body, name=f"merge_out_l{l}", grid=(NT,),
        in_specs=[_rows(W), _rows(W), _rows(D, 2), _rows(D, 3), _rows(D),
                  _layer(l, (4, W, 256)), _layer(l, (4, W, 256)), _layer(l, (D, D))],
        out_specs=[_rows(D), _rows(D), _rows(D), _rows(D)],
        out_shape=[SDS((L, D), F32), SDS((L, D), F32), SDS((L, D), BF16), SDS((L, D), F32)],
        compiler_params=_params("arbitrary"),
    )(ya, yb, proj, proj, x, wa, wb, wo)


def _loss_head(x, gf, target):
    def body(x_ref, g_ref, t_ref, loss_ref, dx_ref, dg_ref):
        @pl.when(pl.program_id(0) == 0)
        def _():
            loss_ref[...] = jnp.zeros_like(loss_ref)
            dg_ref[...] = jnp.zeros_like(dg_ref)

        xv = x_ref[...]
        g = g_ref[...]
        r = lax.rsqrt(jnp.mean(xv * xv, axis=-1, keepdims=True) + EPS)
        xn = xv * r
        err = xn * g - t_ref[...]
        part = jnp.sum(jnp.mean(err * err, axis=-1, keepdims=True), axis=0, keepdims=True)
        loss_ref[...] += 0.5 * part
        dy = err * (1.0 / D)
        dg_ref[...] += jnp.sum(dy * xn, axis=0, keepdims=True)
        dxn = dy * g
        dx_ref[...] = r * (dxn - xn * jnp.mean(dxn * xn, axis=-1, keepdims=True))

    return pl.pallas_call(
        body, name="loss_head", grid=(NT,),
        in_specs=[_rows(D), _full((1, D)), _rows(D)],
        out_specs=[_full((2, 128)), _rows(D), _full((1, D))],
        out_shape=[SDS((2, 128), F32), SDS((L, D), F32), SDS((1, D), F32)],
        compiler_params=_params("arbitrary"),
    )(x, gf, target)


def _acc_specs(l, prevs, shapes):
    out_specs = [_layer(l, s) for s in shapes]
    out_shape = [SDS((DEPTH,) + s, F32) for s in shapes]
    if prevs is None:
        return [], (), out_specs, out_shape
    return [_ANY] * len(shapes), tuple(prevs), out_specs, out_shape


def _merge_out_bwd(l, dxn, mg, proj, pa, pb, ya, yb, wo, wa, wb, prevs):
    acc_in, acc_args, acc_specs, acc_shape = _acc_specs(l, prevs, [(D, D), (4, W, 256), (4, W, 256)])

    def body(dx_ref, mg_ref, ga_ref, gb_ref, pa_ref, pb_ref, ya_ref, yb_ref, wo_ref, wa_ref, wb_ref, *rest):
        dg_ref, dya_ref, dyb_ref, dwo_ref, dwa_ref, dwb_ref = rest[len(acc_args):]

        @pl.when(pl.program_id(0) == 0)
        def _():
            dwo_ref[...] = jnp.zeros_like(dwo_ref)
            dwa_ref[...] = jnp.zeros_like(dwa_ref)
            dwb_ref[...] = jnp.zeros_like(dwb_ref)

        dxb = dx_ref[...].astype(BF16)
        dm = _dot_nt(dxb, wo_ref[...])
        sa = _sigmoid(ga_ref[...])
        sb = _sigmoid(gb_ref[...])
        dg_ref[:, :D] = dm * pa_ref[...] * (sa * (1.0 - sa))
        dg_ref[:, D:] = dm * pb_ref[...] * (sb * (1.0 - sb))
        dpa = (dm * sa).astype(BF16)
        dpb = (dm * sb).astype(BF16)
        ya = ya_ref[...]
        yb = yb_ref[...]
        dya = jnp.zeros((TM, W), F32)
        dyb = jnp.zeros((TM, W), F32)
        for j in range(4):
            cs = slice(256 * j, 256 * (j + 1))
            dya = dya + _dot_nt(dpa[:, cs], wa_ref[j])
            dyb = dyb + _dot_nt(dpb[:, cs], wb_ref[j])
            dwa_ref[j] += _dot_tn(ya, dpa[:, cs])
            dwb_ref[j] += _dot_tn(yb, dpb[:, cs])
        dya_ref[...] = dya
        dyb_ref[...] = dyb
        dwo_ref[...] += _dot_tn(mg_ref[...], dxb)

    return pl.pallas_call(
        body, name=f"merge_out_bwd_l{l}", grid=(NT,),
        in_specs=[_rows(D), _rows(D), _rows(D, 2), _rows(D, 3), _rows(D), _rows(D), _rows(W), _rows(W),
                  _layer(l, (D, D)), _layer(l, (4, W, 256)), _layer(l, (4, W, 256))] + acc_in,
        out_specs=[_rows(2 * D, 1), _rows(W), _rows(W)] + acc_specs,
        out_shape=[SDS((L, NIN), F32), SDS((L, W), F32), SDS((L, W), F32)] + acc_shape,
        input_output_aliases={11 + k: 3 + k for k in range(len(acc_args))},
        compiler_params=_params("arbitrary"),
    )(dxn, mg, proj, proj, pa, pb, ya, yb, wo, wa, wb, *acc_args)


def _pool_bwd(l, dyb, proj, mixed, pooled, pw, scale, dproj):
    def body(dyb_ref, zb_ref, mixed_ref, pooled_ref, pw_ref, sc_ref, _, db_ref, dpw_ref, dsc_ref, buf):
        i = pl.program_id(0)
        tile = NT - 1 - i

        @pl.when(i == 0)
        def _():
            dpw_ref[...] = jnp.zeros_like(dpw_ref)
            dsc_ref[...] = jnp.zeros_like(dsc_ref)
            buf[TM:TM + 16, :] = jnp.zeros((16, W), F32)

        dyb = dyb_ref[...]
        zb = zb_ref[...]
        mixed = mixed_ref[...]
        sc = sc_ref[...]
        sg = _sigmoid(zb)
        dyb0 = dyb * (zb * sg)
        db_ref[:, W:] = dyb * (mixed * sc) * (sg * (1.0 + zb * (1.0 - sg)))
        dsc_ref[...] += jnp.sum(dyb0 * mixed, axis=0, keepdims=True)
        dmix = (dyb0 * sc).astype(BF16)
        t = tile * TM + lax.broadcasted_iota(jnp.int32, (TM, 128), 0)
        for gi, win in enumerate(WINS):
            cs = slice(128 * gi, 128 * (gi + 1))
            dpw_ref[gi] += _dot_tn(pooled_ref[:, cs], dmix[:, cs])
            dpool = _dot_nt(dmix[:, cs], pw_ref[gi])
            cnt = jnp.minimum(t + 1, win).astype(F32)
            e = dpool / cnt
            buf[0:TM, cs] = e
            acc = e - dpool
            for k in range(1, win):
                acc = acc + buf[k:k + TM, cs]
            db_ref[:, cs] = acc
        buf[TM:TM + 16, :] = buf[0:16, :]

    return pl.pallas_call(
        body, name=f"pool_bwd_l{l}", grid=(NT,),
        in_specs=[_rows(W, 0, True), _rows(W, 3, True), _rows(W, 0, True), _rows(W, 0, True),
                  _full((4, 128, 128)), _full((1, W)), _ANY],
        out_specs=[_rows(2 * W, 1, True), _full((4, 128, 128)), _full((1, W))],
        out_shape=[SDS((L, NIN), F32), SDS((4, 128, 128), F32), SDS((1, W), F32)],
        input_output_aliases={6: 0},
        scratch_shapes=[pltpu.VMEM((TM + 16, W), F32)],
        compiler_params=_params("arbitrary"),
    )(dyb, proj, mixed, pooled, pw, scale, dproj)


def _s5_bwd(l, dya, proj, y1, q, sre, sim, cret, cimnt, bret, bimt, st_re, st_im, cr_re, cr_im, dsk, wglu, dproj, prev):
    acc_in, acc_args, acc_specs, acc_shape = _acc_specs(l, prev, [(W, W)])
    def halo(i):
        return (jnp.maximum((NT - 1 - i) * (TM // 8) - 1, 0), 0)

    def body(dya_ref, ua_ref, za_ref, y1_ref, q_ref, sre_ref, sim_ref, hre_ref, him_ref,
             cret_ref, cimnt_ref, bret_ref, bimt_ref, st_re_ref, st_im_ref, cr_re_ref, cr_im_ref, d_ref, wg_ref,
             *rest):
        (da_ref, dwg_ref, dbg_ref, dd_ref, dcre_ref, dcimn_ref, dbre_ref, dbim_ref, dare_ref, daim_ref,
         lre, lim, car_ref) = rest[1 + len(acc_args):]
        i = pl.program_id(0)
        tile = NT - 1 - i

        @pl.when(i == 0)
        def _():
            for ref in (dwg_ref, dbg_ref, dd_ref, dcre_ref, dcimn_ref, dbre_ref, dbim_ref, dare_ref, daim_ref, car_ref):
                ref[...] = jnp.zeros_like(ref)

        u = ua_ref[...]
        za = za_ref[...]
        y1 = y1_ref[...]
        dya = dya_ref[...]
        y2 = _gelu(y1)
        sg = _sigmoid(q_ref[...])
        sgz = _sigmoid(za)
        dy3 = dya * (za * sgz)
        da_ref[:, W:] = dya * (y2 * sg) * (sgz * (1.0 + za * (1.0 - sgz)))
        dq = dy3 * y2 * (sg * (1.0 - sg))
        dqb = dq.astype(BF16)
        dy2 = dy3 * sg + _dot_nt(dqb, wg_ref[...])
        dwg_ref[...] += _dot_tn(y2.astype(BF16), dqb)
        dbg_ref[...] += jnp.sum(dq, axis=0, keepdims=True)
        dy1 = dy2 * _gelu_grad(y1)
        dd_ref[...] += jnp.sum(dy1 * u, axis=0, keepdims=True)
        dy1b = dy1.astype(BF16)
        ub = u.astype(BF16)
        for k in range(4):
            blk = slice(512 * k, 512 * (k + 1))
            ks = slice(128 * k, 128 * (k + 1))
            lre[:, blk] = _dot(dy1b[:, ks], cret_ref[k])
            lim[:, blk] = _dot(dy1b[:, ks], cimnt_ref[k])
            dcre_ref[k] += _dot_tn(sre_ref[:, blk].astype(BF16), dy1b[:, ks])
            dcimn_ref[k] += _dot_tn(sim_ref[:, blk].astype(BF16), dy1b[:, ks])
        carry = _scan_tile(lre, lim, st_re_ref, st_im_ref, cr_re_ref, cr_im_ref, _load_carry(car_ref), True)
        _store_carry(car_ref, carry)

        rowid = lax.broadcasted_iota(jnp.int32, (8, 512), 0)
        gate = (tile > 0).astype(F32)

        def chunk(c, _):
            rows = pl.ds(pl.multiple_of(c * 8, 8), 8)
            prows = pl.ds(pl.multiple_of(jnp.maximum(c - 1, 0) * 8, 8), 8)
            for lb in range(GP // 512):
                cols = slice(lb * 512, (lb + 1) * 512)
                sr = sre_ref[rows, cols]
                si = sim_ref[rows, cols]
                pr = jnp.where(c == 0, hre_ref[7:8, cols] * gate, sre_ref[prows, cols][7:8])
                pi = jnp.where(c == 0, him_ref[7:8, cols] * gate, sim_ref[prows, cols][7:8])
                sr = jnp.where(rowid == 0, pr, pltpu.roll(sr, 1, 0))
                si = jnp.where(rowid == 0, pi, pltpu.roll(si, 1, 0))
                lr = lre[rows, cols]
                li = lim[rows, cols]
                dare_ref[:, cols] += sr * lr + si * li
                daim_ref[:, cols] += sr * li - si * lr
            return 0

        lax.fori_loop(0, TM // 8, chunk, 0)

        for k in range(4):
            blk = slice(512 * k, 512 * (k + 1))
            ks = slice(128 * k, 128 * (k + 1))
            lrb = lre[:, blk].astype(BF16)
            lib = lim[:, blk].astype(BF16)
            da_ref[:, ks] = dy1[:, ks] * d_ref[:, ks] + _dot(lrb, bret_ref[k]) + _dot(lib, bimt_ref[k])
            dbre_ref[k] += _dot_tn(ub[:, ks], lrb)
            dbim_ref[k] += _dot_tn(ub[:, ks], lib)

    return pl.pallas_call(
        body, name=f"s5_bwd_l{l}", grid=(NT,),
        in_specs=[_rows(W, 0, True), _rows(W, 0, True), _rows(W, 1, True), _rows(W, 0, True), _rows(W, 0, True),
                  _rows(GP, 0, True), _rows(GP, 0, True),
                  pl.BlockSpec((8, GP), halo), pl.BlockSpec((8, GP), halo),
                  _full((4, 128, 512)), _full((4, 128, 512)), _full((4, 512, 128)), _full((4, 512, 128)),
                  _full((8, GP)), _full((8, GP)), _full((8, GP)), _full((8, GP)), _full((1, W)), _layer(l, (W, W)),
                  _ANY] + acc_in,
        out_specs=[_rows(2 * W, 0, True)] + acc_specs + [_full((1, W)), _full((1, W)),
                   _full((4, 512, 128)), _full((4, 512, 128)), _full((4, 128, 512)), _full((4, 128, 512)),
                   _full((8, GP)), _full((8, GP))],
        out_shape=[SDS((L, NIN), F32)] + acc_shape + [SDS((1, W), F32), SDS((1, W), F32),
                   SDS((4, 512, 128), F32), SDS((4, 512, 128), F32), SDS((4, 128, 512), F32), SDS((4, 128, 512), F32),
                   SDS((8, GP), F32), SDS((8, GP), F32)],
        input_output_aliases={19: 0, **{20 + k: 1 + k for k in range(len(acc_args))}},
        scratch_shapes=[pltpu.VMEM((TM, GP), F32), pltpu.VMEM((TM, GP), F32), pltpu.VMEM((8, GP), F32)],
        compiler_params=_params("arbitrary"),
    )(dya, proj, proj, y1, q, sre, sim, sre, sim, cret, cimnt, bret, bimt, st_re, st_im, cr_re, cr_im, dsk, wglu,
      dproj, *acc_args)


def _inproj_dw(l, h, dproj, prev):
    acc_in, acc_args = ([], ()) if prev is None else ([_ANY], (prev,))

    def body(h_ref, dp_ref, *rest):
        dw_ref, db_ref = rest[len(acc_args):]

        @pl.when(pl.program_id(1) == 0)
        def _():
            dw_ref[...] = jnp.zeros_like(dw_ref)
            db_ref[...] = jnp.zeros_like(db_ref)

        dp = dp_ref[...]
        dw_ref[...] += _dot_tn(h_ref[...], dp.astype(BF16))
        db_ref[...] += jnp.sum(dp, axis=0, keepdims=True)

    return pl.pallas_call(
        body, name=f"inproj_dw_l{l}", grid=(4, NT),
        in_specs=[pl.BlockSpec((TM, D), lambda j, i: (i, 0)), pl.BlockSpec((TM, 1024), lambda j, i: (i, j))] + acc_in,
        out_specs=[pl.BlockSpec((None, None, D, 1024), lambda j, i: (l, j, 0, 0)),
                   pl.BlockSpec((1, 1024), lambda j, i: (0, j))],
        out_shape=[SDS((DEPTH, 4, D, 1024), F32), SDS((1, NIN), F32)],
        input_output_aliases={2: 0} if acc_args else {},
        compiler_params=_params("arbitrary", "arbitrary"),
    )(h, dproj, *acc_args)


def _inproj_dx(l, dproj, w, x, g, dxn):
    def body(dp_ref, w_ref, x_ref, g_ref, dxn_ref, dx_ref, dg_ref):
        @pl.when(pl.program_id(0) == 0)
        def _():
            dg_ref[...] = jnp.zeros_like(dg_ref)

        dh = jnp.zeros((TM, D), F32)
        for j in range(4):
            dh = dh + _dot_nt(dp_ref[:, j * 1024:(j + 1) * 1024].astype(BF16), w_ref[j])
        xv = x_ref[...]
        r = lax.rsqrt(jnp.mean(xv * xv, axis=-1, keepdims=True) + EPS)
        xn = xv * r
        dg_ref[...] += jnp.sum(dh * xn, axis=0, keepdims=True)
        dn = dh * g_ref[...]
        dx_ref[...] = dxn_ref[...] + r * (dn - xn * jnp.mean(dn * xn, axis=-1, keepdims=True))

    return pl.pallas_call(
        body, name=f"inproj_dx_l{l}", grid=(NT,),
        in_specs=[_rows(NIN), _layer(l, (4, D, 1024)), _rows(D), _full((1, D)), _rows(D)],
        out_specs=[_rows(D), _full((1, D))],
        out_shape=[SDS((L, D), F32), SDS((1, D), F32)],
        compiler_params=_params("arbitrary"),
    )(dproj, w, x, g, dxn)


def _discretize(log_dt, lam_re, lam_im, b_re, b_im):
    dt = jnp.exp(log_dt)[:, None]
    mag = jnp.exp(lam_re * dt)
    ang = lam_im * dt
    abar_re = mag * jnp.cos(ang)
    abar_im = mag * jnp.sin(ang)
    num_re = abar_re - 1.0
    num_im = abar_im
    den = lam_re * lam_re + lam_im * lam_im
    coef_re = (num_re * lam_re + num_im * lam_im) / den
    coef_im = (num_im * lam_re - num_re * lam_im) / den
    bbar_re = coef_re[..., None] * b_re - coef_im[..., None] * b_im
    bbar_im = coef_re[..., None] * b_im + coef_im[..., None] * b_re
    return abar_re, abar_im, bbar_re, bbar_im


def _powers(abar_re, abar_im):
    ar, ai = abar_re.reshape(1, GP), abar_im.reshape(1, GP)
    rows_re, rows_im = [ar], [ai]
    for _ in range(7):
        pr, pi = rows_re[-1], rows_im[-1]
        rows_re.append(pr * ar - pi * ai)
        rows_im.append(pr * ai + pi * ar)
    neg_im = [-r for r in rows_im]
    return (jnp.concatenate(rows_re, axis=0), jnp.concatenate(rows_im, axis=0), jnp.concatenate(neg_im, axis=0),
            jnp.concatenate(rows_re[::-1], axis=0), jnp.concatenate(neg_im[::-1], axis=0))


_EYE8 = functools.partial(jnp.eye, 8, dtype=F32)


def _expand_in(b):
    return jnp.einsum("kgpc,gh->kgchp", b.reshape(4, 8, P, C), _EYE8()).reshape(4, 128, 512)


def _extract_in(e):
    return jnp.einsum("kgchp,gh->kgpc", e.reshape(4, 8, C, 8, P), _EYE8()).reshape(G, P, C)


def _expand_out(c):
    return jnp.einsum("kgcp,gh->kgphc", c.reshape(4, 8, C, P), _EYE8()).reshape(4, 512, 128)


def _extract_out(e):
    return jnp.einsum("kgphc,gh->kgcp", e.reshape(4, 8, P, 8, C), _EYE8()).reshape(G, C, P)


SMALL = ("norm_g", "b_in", "ssm_log_dt", "ssm_lam_re", "ssm_lam_im", "ssm_b_re", "ssm_b_im",
         "ssm_c_re", "ssm_c_im", "ssm_d", "ssm_b_glu", "pool_w", "pool_scale")
BIG = ("w_in", "ssm_w_glu", "w_branch_a", "w_branch_b", "w_out")


def _local_step(x, target, sp, final_norm_g, wbuf):
    saved = []
    wbuf = dict(wbuf)
    for l in range(DEPTH):
        disc_in = (sp["ssm_log_dt"][l], sp["ssm_lam_re"][l], sp["ssm_lam_im"][l], sp["ssm_b_re"][l], sp["ssm_b_im"][l])
        (abar_re, abar_im, bbar_re, bbar_im), disc_vjp = jax.vjp(_discretize, *disc_in)
        pw_re, pw_im, pw_imn, pw_re_rev, pw_imn_rev = _powers(abar_re, abar_im)
        b_re_x, b_im_x = _expand_in(bbar_re), _expand_in(bbar_im)
        c_re_x, c_imn_x = _expand_out(sp["ssm_c_re"][l]), _expand_out(-sp["ssm_c_im"][l])
        g = sp["norm_g"][l].reshape(1, D)
        dsk = sp["ssm_d"][l].reshape(1, W)
        scale = sp["pool_scale"][l].reshape(1, W)
        pw = sp["pool_w"][l].astype(BF16)

        if l == 0:
            (wbuf["w_in"],) = _comm_only("gather_w_in_l0", _GatherJob([wbuf["w_in"]], 0))
        (h, proj), rest = _norm_inproj(l, x, g, wbuf["w_in"], sp["b_in"][l].reshape(1, NIN),
                                       _GatherJob([wbuf[n] for n in BIG[1:]], l))
        wbuf.update(zip(BIG[1:], rest))
        wg = dict(wbuf, ssm_w_glu=wbuf["ssm_w_glu"].reshape(DEPTH, W, W), w_out=wbuf["w_out"].reshape(DEPTH, D, D))
        (sre, sim, y1, q, ya), nxt = _s5_fwd(
            l, proj, jnp.concatenate([b_re_x, b_im_x], axis=2).astype(BF16), c_re_x.astype(BF16), c_imn_x.astype(BF16),
            pw_re, pw_im, dsk, wg["ssm_w_glu"], sp["ssm_b_glu"][l].reshape(1, W),
            _GatherJob([wbuf["w_in"]], l + 1) if l + 1 < DEPTH else None)
        if nxt:
            (wbuf["w_in"],) = nxt
            wg["w_in"] = wbuf["w_in"]
        pooled, mixed, yb = _pool_fwd(l, proj, pw, scale)
        pa, pb, mg, x_next = _merge_out(l, ya, yb, proj, x, wg["w_branch_a"], wg["w_branch_b"], wg["w_out"])
        saved.append(dict(x=x, g=g, dsk=dsk, scale=scale, pw=pw, h=h, proj=proj, sre=sre, sim=sim, y1=y1, q=q, ya=ya,
                          pooled=pooled, mixed=mixed, yb=yb, pa=pa, pb=pb, mg=mg, disc_vjp=disc_vjp,
                          powers=(pw_re, pw_imn, pw_re_rev, pw_imn_rev),
                          b_re_x=b_re_x, b_im_x=b_im_x, c_re_x=c_re_x, c_imn_x=c_imn_x))
        x = x_next

    loss, dx, dgf = _loss_head(x, final_norm_g.reshape(1, D), target)

    gs = {n: [None] * DEPTH for n in SMALL}
    acc_mo, acc_glu, acc_in = None, None, None
    for l in reversed(range(DEPTH)):
        s = saved[l]
        dproj, dya, dyb, *acc_mo = _merge_out_bwd(
            l, dx, s["mg"], s["proj"], s["pa"], s["pb"], s["ya"], s["yb"],
            wg["w_out"], wg["w_branch_a"], wg["w_branch_b"], acc_mo)
        dproj, dpw, dsc = _pool_bwd(l, dyb, s["proj"], s["mixed"], s["pooled"], s["pw"], s["scale"], dproj)
        t = lambda a: jnp.swapaxes(a, 1, 2).astype(BF16)
        (dproj, dwg, dbg, dd, dcre, dcimn, dbre, dbim, dare, daim) = _s5_bwd(
            l, dya, s["proj"], s["y1"], s["q"], s["sre"], s["sim"],
            t(s["c_re_x"]), t(s["c_imn_x"]), t(s["b_re_x"]), t(s["b_im_x"]),
            *s["powers"], s["dsk"], wg["ssm_w_glu"], dproj, acc_glu)
        acc_glu = [dwg]
        acc_in, dbin = _inproj_dw(l, s["h"], dproj, acc_in)
        dx, dg = _inproj_dx(l, dproj, wg["w_in"], s["x"], s["g"], dx)

        d_abar_re = jnp.sum(dare, axis=0).reshape(G, P)
        d_abar_im = jnp.sum(daim, axis=0).reshape(G, P)
        dlog_dt, dlam_re, dlam_im, db_re, db_im = s["disc_vjp"](
            (d_abar_re, d_abar_im, _extract_in(dbre), _extract_in(dbim)))
        gs["norm_g"][l] = dg.reshape(D)
        gs["b_in"][l] = dbin.reshape(NIN)
        gs["ssm_log_dt"][l] = dlog_dt
        gs["ssm_lam_re"][l] = dlam_re
        gs["ssm_lam_im"][l] = dlam_im
        gs["ssm_b_re"][l] = db_re
        gs["ssm_b_im"][l] = db_im
        gs["ssm_c_re"][l] = _extract_out(dcre)
        gs["ssm_c_im"][l] = -_extract_out(dcimn)
        gs["ssm_d"][l] = dd.reshape(W)
        gs["ssm_b_glu"][l] = dbg.reshape(W)
        gs["pool_w"][l] = dpw
        gs["pool_scale"][l] = dsc.reshape(W)
    gs = {n: jnp.stack(v) for n, v in gs.items()}
    dwo, dwa, dwb = acc_mo
    gb = {"w_in": acc_in, "ssm_w_glu": acc_glu[0], "w_branch_a": dwa, "w_branch_b": dwb, "w_out": dwo}
    return loss, dx, gs, gb, dgf


def _place():
    x, y, c = lax.axis_index("x"), lax.axis_index("y"), lax.axis_index("c")
    chips = [(1 - x, y), (x, 1 - y), (1 - x, 1 - y)]
    return x, y, c, 2 * x + y, chips, [2 * cx + cy for cx, cy in chips]


def _remote(src, dst, ssem, rsem, dev):
    return pltpu.make_async_remote_copy(src_ref=src, dst_ref=dst, send_sem=ssem, recv_sem=rsem,
                                        device_id=dev, device_id_type=MESH)


class _Job:
    def wrap(self, body, n_in, n_out, n_scr, grid):
        nj = len(self.bufs)

        def at_step(step):
            return functools.reduce(jnp.logical_and, [pl.program_id(d) == step(d) for d in range(len(grid))])

        def hosted(*refs):
            ins = refs[:n_in]
            outs = refs[n_in + nj:n_in + nj + n_out]
            bufs = refs[n_in + nj + n_out:n_in + 2 * nj + n_out]
            scr = refs[n_in + 2 * nj + n_out:n_in + 2 * nj + n_out + n_scr]
            sems = refs[n_in + 2 * nj + n_out + n_scr:]
            if not grid:
                self.start(bufs, sems)
                self.finish(bufs, sems)
                return
            pl.when(at_step(lambda d: 0))(lambda: self.start(bufs, sems))
            body(*ins, *outs, *scr)
            pl.when(at_step(lambda d: grid[d] - 1))(lambda: self.finish(bufs, sems))

        return hosted


class _GatherJob(_Job):
    def __init__(self, bufs, l):
        self.bufs, self.l = list(bufs), l
        self.scratch = [pltpu.SemaphoreType.DMA((len(self.bufs), 3))] * 4

    def _half(self, ref, k, h):
        rows = ref.shape[2] // 2
        return ref.at[self.l, k, pl.ds(pl.multiple_of(h * rows, 8), rows), :]

    def _ici(self, bufs, sems, a, j, k):
        _, _, c, _, chips, _ = _place()
        blk = self._half(bufs[a], k, c)
        return _remote(blk, blk, sems[0].at[a, j], sems[1].at[a, j], (*chips[j], c))

    def _d2d(self, bufs, sems, a, j, k, h):
        x, y, c, _, _, _ = _place()
        blk = self._half(bufs[a], k, h)
        return _remote(blk, blk, sems[2].at[a, j], sems[3].at[a, j], (x, y, 1 - c))

    def start(self, bufs, sems):
        me = _place()[3]
        for a in range(len(self.bufs)):
            for j in range(3):
                self._ici(bufs, sems, a, j, me).start()

    def finish(self, bufs, sems):
        _, _, c, me, _, cid = _place()
        pairs = [(a, j) for a in range(len(self.bufs)) for j in range(3)]
        for a, j in pairs:
            self._ici(bufs, sems, a, j, cid[j]).wait_recv()
            self._d2d(bufs, sems, a, j, cid[j], c).start()
        for a, j in pairs:
            self._d2d(bufs, sems, a, j, cid[j], 1 - c).wait_recv()
        for a, j in pairs:
            self._ici(bufs, sems, a, j, me).wait_send()
            self._d2d(bufs, sems, a, j, cid[j], c).wait_send()


def _comm_only(name, job):
    return _pcall(None, name, (), [], [], [], [], job=job)[1]


def _chip_allgather(name, bufs, chip_major):
    n = len(bufs)

    def body(*refs):
        bufs_ = refs[n:2 * n]
        ssem, rsem, fsem, grsem = refs[2 * n:]
        x, y, c, me, chips, cid = _place()
        sib = (x, y, 1 - c)

        def blk(a, h, k):
            return bufs_[a].at[k, h] if chip_major else bufs_[a].at[h, k]

        sends = []
        for a in range(n):
            for j, (cx, cy) in enumerate(chips):
                cp = _remote(blk(a, c, me), blk(a, c, me), ssem.at[a, j], rsem.at[a, j], (cx, cy, c))
                cp.start()
                sends.append(cp)
        for a in range(n):
            for j, (cx, cy) in enumerate(chips):
                got = blk(a, c, cid[j])
                _remote(got, got, ssem.at[a, j], rsem.at[a, j], (cx, cy, c)).wait_recv()
                cp = _remote(got, got, fsem.at[a, j], grsem.at[a, j], sib)
                cp.start()
                sends.append(cp)
        for a in range(n):
            for j in range(3):
                got = blk(a, 1 - c, cid[j])
                _remote(got, got, fsem.at[a, j], grsem.at[a, j], sib).wait_recv()
        for cp in sends:
            cp.wait_send()

    return pl.pallas_call(
        body, name=name,
        in_specs=[_ANY] * n, out_specs=[_ANY] * n,
        out_shape=[SDS(a.shape, a.dtype) for a in bufs],
        input_output_aliases={a: a for a in range(n)},
        scratch_shapes=[pltpu.SemaphoreType.DMA((n, 3))] * 4,
    )(*bufs)


def _sibling_exchange(name, arrs, halves):
    n = len(arrs)

    def body(*refs):
        ins, outs = refs[:n], refs[n:2 * n]
        ssem, rsem = refs[2 * n:]
        x, y, c, _, _, _ = _place()
        cps = []
        for a in range(n):
            src = ins[a].at[1 - c] if halves[a] else ins[a]
            cp = _remote(src, outs[a], ssem.at[a], rsem.at[a], (x, y, 1 - c))
            cp.start()
            cps.append(cp)
        for cp in cps:
            cp.wait()

    return pl.pallas_call(
        body, name=name,
        in_specs=[_ANY] * n, out_specs=[_ANY] * n,
        out_shape=[SDS(a.shape[1:] if hv else a.shape, a.dtype) for a, hv in zip(arrs, halves)],
        scratch_shapes=[pltpu.SemaphoreType.DMA((n,))] * 2,
    )(*arrs)


def _chip_scatter(name, parts):
    n = len(parts)

    def body(*refs):
        ins, outs = refs[:n], refs[n:2 * n]
        ssem, rsem = refs[2 * n:]
        _, _, c, _, chips, cid = _place()
        cps = []
        for a in range(n):
            for j, (cx, cy) in enumerate(chips):
                cp = _remote(ins[a].at[cid[j]], outs[a].at[j], ssem.at[a, j], rsem.at[a, j], (cx, cy, c))
                cp.start()
                cps.append(cp)
        for cp in cps:
            cp.wait()

    return pl.pallas_call(
        body, name=name,
        in_specs=[_ANY] * n, out_specs=[_ANY] * n,
        out_shape=[SDS((3,) + a.shape[1:], a.dtype) for a in parts],
        scratch_shapes=[pltpu.SemaphoreType.DMA((n, 3))] * 2,
    )(*parts)


def _cast_own(place, ws):
    n = len(ws)

    def body(p_ref, *refs):
        for i_ref, o_ref in zip(refs[:n], refs[n:]):
            o_ref[...] = i_ref[...].astype(BF16)

    return pl.pallas_call(
        body, name="cast_own_shards",
        grid_spec=pltpu.PrefetchScalarGridSpec(
            num_scalar_prefetch=1, grid=(DEPTH,),
            in_specs=[pl.BlockSpec((None,) + a.shape[1:], lambda l, p: (l, 0, 0)) for a in ws],
            out_specs=[pl.BlockSpec((None, None) + a.shape[1:], lambda l, p: (l, p[1], 0, 0)) for a in ws]),
        out_shape=[SDS((DEPTH, 4) + a.shape[1:], BF16) for a in ws],
        compiler_params=_params("arbitrary"),
    )(place, *ws)


def _pair_sum_bf16(name, place, own, recv):
    _, _, a_, b_ = own.shape
    ta = min(a_, 256)

    def body(p_ref, own_ref, recv_ref, out_ref):
        out_ref[...] = (own_ref[...] + recv_ref[...]).astype(BF16)

    return pl.pallas_call(
        body, name=name,
        grid_spec=pltpu.PrefetchScalarGridSpec(
            num_scalar_prefetch=1, grid=(4, a_ // ta),
            in_specs=[pl.BlockSpec((None, None, ta, b_), lambda s, i, p: (p[0], s, i, 0)),
                      pl.BlockSpec((None, ta, b_), lambda s, i, p: (s, i, 0))],
            out_specs=pl.BlockSpec((None, ta, b_), lambda s, i, p: (s, i, 0))),
        out_shape=SDS((4, a_, b_), BF16),
        compiler_params=_params("arbitrary", "arbitrary"),
    )(place, own, recv)


def _shard_sum(name, place, own, recv, rbuf):
    _, _, a_, b_ = own.shape
    ta = min(a_, 256)

    def body(p_ref, own_ref, recv_ref, r_ref, out_ref):
        acc = own_ref[...] + recv_ref[...]
        for j in range(3):
            acc = acc + r_ref[j].astype(F32)
        out_ref[...] = acc

    return pl.pallas_call(
        body, name=name,
        grid_spec=pltpu.PrefetchScalarGridSpec(
            num_scalar_prefetch=1, grid=(a_ // ta,),
            in_specs=[pl.BlockSpec((None, None, ta, b_), lambda i, p: (p[0], p[1], i, 0)),
                      pl.BlockSpec((None, ta, b_), lambda i, p: (p[1], i, 0)),
                      pl.BlockSpec((3, ta, b_), lambda i, p: (0, i, 0))],
            out_specs=pl.BlockSpec((ta, b_), lambda i, p: (i, 0))),
        out_shape=SDS((a_, b_), F32),
        compiler_params=_params("arbitrary"),
    )(place, own, recv, rbuf)


def _small_pair_sum(place, mine, recv):
    n = len(mine)

    def body(p_ref, *refs):
        for m_ref, r_ref, o_ref in zip(refs[:n], refs[n:2 * n], refs[2 * n:]):
            o_ref[...] = m_ref[...] + r_ref[...]

    def whole(a):
        zeros = (0,) * a.ndim
        return pl.BlockSpec(a.shape, lambda i, p: zeros)

    def mine_blk(a):
        zeros = (0,) * a.ndim
        return pl.BlockSpec((None,) + a.shape, lambda i, p: (p[1],) + zeros)

    return pl.pallas_call(
        body, name="small_pair_sum",
        grid_spec=pltpu.PrefetchScalarGridSpec(
            num_scalar_prefetch=1, grid=(1,),
            in_specs=[whole(a) for a in mine] + [whole(a) for a in recv],
            out_specs=[mine_blk(a) for a in mine]),
        out_shape=[SDS((4,) + a.shape, F32) for a in mine],
        compiler_params=_params("arbitrary"),
    )(place, *mine, *recv)


def _adam_math(w, g, m, v):
    m = B1 * m + (1.0 - B1) * g
    v = B2 * v + (1.0 - B2) * (g * g)
    m_hat = m / (1.0 - B1 ** STEP)
    v_hat = v / (1.0 - B2 ** STEP)
    delta = -LR * (m_hat / (jnp.sqrt(v_hat) + EPS_A) + WD * w)
    return delta, m, v


def _adamw_big(name, place, w, m, v, mine, recv):
    _, a_, b_ = w.shape
    ta = min(a_, 256)

    def body(p_ref, w_ref, m_ref, v_ref, mine_ref, recv_ref, g_ref, d_ref, mo_ref, vo_ref):
        g = jnp.where(pl.program_id(0) == p_ref[0], mine_ref[...], recv_ref[...])
        g_ref[...] = g
        d_ref[...], mo_ref[...], vo_ref[...] = _adam_math(w_ref[...], g, m_ref[...], v_ref[...])

    slab = pl.BlockSpec((None, ta, b_), lambda h, i, p: (h, i, 0))
    half = pl.BlockSpec((ta, b_), lambda h, i, p: (i, 0))
    return pl.pallas_call(
        body, name=name,
        grid_spec=pltpu.PrefetchScalarGridSpec(
            num_scalar_prefetch=1, grid=(2, a_ // ta),
            in_specs=[slab, slab, slab, half, half], out_specs=[slab] * 4),
        out_shape=[SDS(w.shape, F32)] * 4,
        compiler_params=_params("arbitrary", "arbitrary"),
    )(place, w, m, v, mine, recv)


def _adamw_small(name, ws, parts, ms, vs, loss_parts=None):
    k = len(ws)
    extra = [] if loss_parts is None else [loss_parts]

    def chip_sum(p_ref):
        return ((p_ref[0] + p_ref[1]) + p_ref[2]) + p_ref[3]

    def body(*refs):
        w_refs, p_refs, m_refs, v_refs = refs[:k], refs[k:2 * k], refs[2 * k:3 * k], refs[3 * k:4 * k]
        outs = refs[4 * k + len(extra):]
        if extra:
            outs[0][...] = chip_sum(refs[4 * k])
            outs = outs[1:]
        for a in range(k):
            g = chip_sum(p_refs[a])
            outs[a][...] = g
            outs[k + a][...], outs[2 * k + a][...], outs[3 * k + a][...] = _adam_math(
                w_refs[a][...], g, m_refs[a][...], v_refs[a][...])

    like = [SDS(a.shape, F32) for a in ws]
    return pl.pallas_call(
        body, name=name,
        out_shape=([SDS((2, 128), F32)] if extra else []) + like * 4,
        compiler_params=pltpu.CompilerParams(vmem_limit_bytes=VMEM_LIMIT),
    )(*ws, *parts, *ms, *vs, *extra)


WEIGHTS = ("norm_g", "w_in", "b_in", "ssm_log_dt", "ssm_lam_re", "ssm_lam_im", "ssm_b_re", "ssm_b_im", "ssm_c_re",
           "ssm_c_im", "ssm_d", "ssm_w_glu", "ssm_b_glu", "pool_w", "pool_scale", "w_branch_a", "w_branch_b", "w_out",
           "final_norm_g")
REPLICATED = SMALL + ("final_norm_g",)
ADAM_GROUPS = (tuple(n for n in REPLICATED if n not in ("ssm_b_re", "ssm_b_im")), ("ssm_b_re",), ("ssm_b_im",))


def kernel(x, norm_g, w_in, b_in, ssm_log_dt, ssm_lam_re, ssm_lam_im, ssm_b_re, ssm_b_im, ssm_c_re, ssm_c_im, ssm_d, ssm_w_glu, ssm_b_glu, pool_w, pool_scale, w_branch_a, w_branch_b, w_out, final_norm_g, loss_target, m_norm_g, m_w_in, m_b_in, m_ssm_log_dt, m_ssm_lam_re, m_ssm_lam_im, m_ssm_b_re, m_ssm_b_im, m_ssm_c_re, m_ssm_c_im, m_ssm_d, m_ssm_w_glu, m_ssm_b_glu, m_pool_w, m_pool_scale, m_w_branch_a, m_w_branch_b, m_w_out, m_final_norm_g, v_norm_g, v_w_in, v_b_in, v_ssm_log_dt, v_ssm_lam_re, v_ssm_lam_im, v_ssm_b_re, v_ssm_b_im, v_ssm_c_re, v_ssm_c_im, v_ssm_d, v_ssm_w_glu, v_ssm_b_glu, v_pool_w, v_pool_scale, v_w_branch_a, v_w_branch_b, v_w_out, v_final_norm_g):
    w = dict(zip(WEIGHTS, (norm_g, w_in, b_in, ssm_log_dt, ssm_lam_re, ssm_lam_im, ssm_b_re, ssm_b_im, ssm_c_re,
                           ssm_c_im, ssm_d, ssm_w_glu, ssm_b_glu, pool_w, pool_scale, w_branch_a, w_branch_b, w_out,
                           final_norm_g)))
    m = dict(zip(WEIGHTS, (m_norm_g, m_w_in, m_b_in, m_ssm_log_dt, m_ssm_lam_re, m_ssm_lam_im, m_ssm_b_re, m_ssm_b_im,
                           m_ssm_c_re, m_ssm_c_im, m_ssm_d, m_ssm_w_glu, m_ssm_b_glu, m_pool_w, m_pool_scale,
                           m_w_branch_a, m_w_branch_b, m_w_out, m_final_norm_g)))
    v = dict(zip(WEIGHTS, (v_norm_g, v_w_in, v_b_in, v_ssm_log_dt, v_ssm_lam_re, v_ssm_lam_im, v_ssm_b_re, v_ssm_b_im,
                           v_ssm_c_re, v_ssm_c_im, v_ssm_d, v_ssm_w_glu, v_ssm_b_glu, v_pool_w, v_pool_scale,
                           v_w_branch_a, v_w_branch_b, v_w_out, v_final_norm_g)))
    place = jnp.stack([lax.axis_index("c"), 2 * lax.axis_index("x") + lax.axis_index("y")]).astype(jnp.int32)

    wbuf = dict(zip(BIG, _cast_own(place, [w[n] for n in BIG])))

    loss, dx, gs, gb, dgf = _local_step(x[0], loss_target[0], {n: w[n] for n in SMALL}, final_norm_g, wbuf)

    gb["ssm_w_glu"] = gb["ssm_w_glu"].reshape(DEPTH, 4, W // 4, W)
    gb["w_out"] = gb["w_out"].reshape(DEPTH, 4, D // 4, D)
    own = [gb[n] for n in BIG]
    small = [gs[n] for n in SMALL] + [dgf.reshape(2, D // 2), loss]
    recv = _sibling_exchange("grads_to_sibling", own + small, [True] * len(BIG) + [False] * len(small))
    pair_small = _small_pair_sum(place, small, recv[len(BIG):])
    parts = [_pair_sum_bf16(f"pair_sum_{n}", place, o, r) for n, o, r in zip(BIG, own, recv)]
    rbufs = _chip_scatter("scatter_partials", parts)
    small_parts = dict(zip(REPLICATED + ("loss",), _chip_allgather("gather_small", pair_small, True)))
    shard = [_shard_sum(f"shard_sum_{n}", place, o, r, rb) for n, o, r, rb in zip(BIG, own, recv, rbufs)]
    other = _sibling_exchange("shards_to_sibling", shard, [False] * len(BIG))

    grads, delta, new_m, new_v = {}, {}, {}, {}
    for n, mine, theirs in zip(BIG, shard, other):
        grads[n], delta[n], new_m[n], new_v[n] = _adamw_big(f"adamw_{n}", place, w[n], m[n], v[n], mine, theirs)
    half = lambda a: a.reshape(2, D // 2)
    w["final_norm_g"], m["final_norm_g"], v["final_norm_g"] = half(final_norm_g), half(m_final_norm_g), half(v_final_norm_g)
    total_loss = None
    for gi, names in enumerate(ADAM_GROUPS):
        outs = _adamw_small(f"adamw_small_{gi}", [w[n] for n in names], [small_parts[n] for n in names],
                            [m[n] for n in names], [v[n] for n in names], small_parts["loss"] if gi == 0 else None)
        if gi == 0:
            total_loss, outs = outs[0][0, 0], outs[1:]
        k = len(names)
        for dst, vals in ((grads, outs[:k]), (delta, outs[k:2 * k]), (new_m, outs[2 * k:3 * k]), (new_v, outs[3 * k:])):
            dst.update(zip(names, vals))
    for dst in (grads, delta, new_m, new_v):
        dst["final_norm_g"] = dst["final_norm_g"].reshape(D)

    return (total_loss, dx[None], *[grads[n] for n in WEIGHTS], *[delta[n] for n in WEIGHTS],
            *[new_m[n] for n in WEIGHTS], *[new_v[n] for n in WEIGHTS])
```

```python
import functools

import jax
import jax.numpy as jnp
from jax import lax
from jax.experimental import pallas as pl
from jax.experimental.pallas import tpu as pltpu

F32, BF16 = jnp.float32, jnp.bfloat16
SDS = jax.ShapeDtypeStruct
MESH = pl.DeviceIdType.MESH

DEPTH = 2
L = 2048
D = 1024
NIN = 4096
W = 512
G, P, C = 32, 64, 16
GP = G * P
WINS = (2, 4, 8, 16)
TM = 256
NT = L // TM
EPS = 1e-6
VMEM_LIMIT = 56 * 2**20

LR, B1, B2, EPS_A, WD, STEP = 0.001, 0.9, 0.999, 1e-08, 0.01, 10


def _params(*sem):
    return pltpu.CompilerParams(dimension_semantics=sem, vmem_limit_bytes=VMEM_LIMIT)


_ANY = pl.BlockSpec(memory_space=pl.ANY)


def _full(shape):
    zeros = (0,) * len(shape)
    return pl.BlockSpec(shape, lambda *_: zeros)


def _layer(l, shape):
    zeros = (0,) * len(shape)
    return pl.BlockSpec((None,) + shape, lambda *_: (l,) + zeros)


def _rows(width, col=0, reverse=False):
    if reverse:
        return pl.BlockSpec((TM, width), lambda i: (NT - 1 - i, col))
    return pl.BlockSpec((TM, width), lambda i: (i, col))


def _pcall(body, name, grid, in_specs, out_specs, out_shape, args, scratch=(), aliases=None, jobs=()):
    in_specs, out_specs, out_shape, args, scratch = list(in_specs), list(out_specs), list(out_shape), list(args), list(scratch)
    aliases = dict(aliases or {})
    jobs = [j for j in jobs if j is not None]
    n_in, n_out, n_scr = len(in_specs), len(out_specs), len(scratch)
    srcs = [s for j in jobs for s in j.srcs]
    bufs = [b for j in jobs for b in j.bufs]
    news = [s for j in jobs for s in j.news]
    aliases.update({n_in + len(srcs) + k: n_out + k for k in range(len(bufs))})

    def hosted(*refs):
        cuts = [n_in, len(srcs), len(bufs), n_out, len(bufs), len(news), n_scr]
        parts, p = [], 0
        for n in cuts:
            parts.append(refs[p:p + n])
            p += n
        ins, src_r, _, outs, buf_r, new_r, scr = parts
        sem_r = refs[p:]
        views, ps, pb, pn, pm = [], 0, 0, 0, 0
        for j in jobs:
            views.append((src_r[ps:ps + len(j.srcs)], buf_r[pb:pb + len(j.bufs)], new_r[pn:pn + len(j.news)],
                          sem_r[pm:pm + len(j.scratch)]))
            ps, pb, pn, pm = ps + len(j.srcs), pb + len(j.bufs), pn + len(j.news), pm + len(j.scratch)

        def run(phase):
            for j, v in zip(jobs, views):
                getattr(j, phase)(*v)

        def at_step(step):
            return functools.reduce(jnp.logical_and, [pl.program_id(d) == step(d) for d in range(len(grid))])

        if not grid:
            run("start")
            run("finish")
            return
        pl.when(at_step(lambda d: 0))(lambda: run("start"))
        body(*ins, *outs, *scr)
        pl.when(at_step(lambda d: grid[d] - 1))(lambda: run("finish"))

    outs = pl.pallas_call(
        hosted if jobs else body, name=name, **({"grid": grid} if grid else {}),
        in_specs=in_specs + [_ANY] * (len(srcs) + len(bufs)), out_specs=out_specs + [_ANY] * (len(bufs) + len(news)),
        out_shape=out_shape + [SDS(b.shape, b.dtype) for b in bufs] + news,
        input_output_aliases=aliases, scratch_shapes=scratch + [s for j in jobs for s in j.scratch],
        compiler_params=_params(*(("arbitrary",) * len(grid))))(*args, *srcs, *bufs)
    res, pb, pn = [], n_out, n_out + len(bufs)
    for j in jobs:
        res.append((list(outs[pb:pb + len(j.bufs)]), list(outs[pn:pn + len(j.news)])))
        pb, pn = pb + len(j.bufs), pn + len(j.news)
    return list(outs[:n_out]), res


def _dot(a, b):
    return jnp.dot(a, b, preferred_element_type=F32)


def _dot_nt(a, b):
    return lax.dot_general(a, b, (((1,), (1,)), ((), ())), preferred_element_type=F32)


def _dot_tn(a, b):
    return lax.dot_general(a, b, (((0,), (0,)), ((), ())), preferred_element_type=F32)


_K0 = 0.7978845608028654
_K1 = 0.044715


def _gelu(x):
    return 0.5 * x * (1.0 + jnp.tanh(_K0 * (x + _K1 * (x * x * x))))


def _gelu_grad(x):
    t = jnp.tanh(_K0 * (x + _K1 * (x * x * x)))
    return 0.5 * (1.0 + t) + 0.5 * x * (1.0 - t * t) * (_K0 * (1.0 + 3.0 * _K1 * x * x))


def _sigmoid(x):
    return jax.nn.sigmoid(x)


def _norm_inproj(l, x, g, w, b, job=None):
    def body(x_ref, g_ref, w_ref, b_ref, h_ref, proj_ref):
        xv = x_ref[...]
        r = lax.rsqrt(jnp.mean(xv * xv, axis=-1, keepdims=True) + EPS)
        hb = ((xv * r) * g_ref[...]).astype(BF16)
        h_ref[...] = hb
        for j in range(4):
            cs = slice(j * 1024, (j + 1) * 1024)
            proj_ref[:, cs] = _dot(hb, w_ref[j]) + b_ref[:, cs]

    return _pcall(
        body, f"norm_inproj_l{l}", (NT,),
        [_rows(D), _full((1, D)), _layer(l, (4, D, 1024)), _full((1, NIN))],
        [_rows(D), _rows(NIN)],
        [SDS((L, D), BF16), SDS((L, NIN), F32)],
        (x, g, w, b), jobs=[job])


def _scan_tile(re_ref, im_ref, st_re, st_im, cr_re, cr_im, carry, reverse):
    rowid = lax.broadcasted_iota(jnp.int32, (8, 512), 0)

    def chunk(ci, carry):
        c = (TM // 8 - 1 - ci) if reverse else ci
        rows = pl.ds(pl.multiple_of(c * 8, 8), 8)
        new = []
        for lb in range(GP // 512):
            cols = slice(lb * 512, (lb + 1) * 512)
            vr = re_ref[rows, cols]
            vi = im_ref[rows, cols]
            for d, prow in ((1, 0), (2, 1), (4, 3)):
                ar = st_re[prow:prow + 1, cols]
                ai = st_im[prow:prow + 1, cols]
                if reverse:
                    keep = rowid < 8 - d
                    sr = jnp.where(keep, pltpu.roll(vr, 8 - d, 0), 0.0)
                    si = jnp.where(keep, pltpu.roll(vi, 8 - d, 0), 0.0)
                else:
                    keep = rowid >= d
                    sr = jnp.where(keep, pltpu.roll(vr, d, 0), 0.0)
                    si = jnp.where(keep, pltpu.roll(vi, d, 0), 0.0)
                vr, vi = vr + ar * sr - ai * si, vi + ar * si + ai * sr
            cr, ci_ = carry[2 * lb], carry[2 * lb + 1]
            pr = cr_re[:, cols]
            pi = cr_im[:, cols]
            vr, vi = vr + pr * cr - pi * ci_, vi + pr * ci_ + pi * cr
            re_ref[rows, cols] = vr
            im_ref[rows, cols] = vi
            if reverse:
                new += [vr[0:1], vi[0:1]]
            else:
                new += [vr[7:8], vi[7:8]]
        return tuple(new)

    return lax.fori_loop(0, TM // 8, chunk, carry)


def _load_carry(car_ref):
    return tuple(car_ref[r:r + 1, lb * 512:(lb + 1) * 512] for lb in range(GP // 512) for r in (0, 1))


def _store_carry(car_ref, carry):
    for lb in range(GP // 512):
        car_ref[0:1, lb * 512:(lb + 1) * 512] = carry[2 * lb]
        car_ref[1:2, lb * 512:(lb + 1) * 512] = carry[2 * lb + 1]


def _s5_fwd(l, proj, bexp, cre, cimn, ap_re, ap_im, dsk, wglu, bglu, job=None):
    def body(ua_ref, za_ref, bexp_ref, cre_ref, cimn_ref, apr_ref, api_ref, d_ref, wg_ref, bg_ref,
             sre_ref, sim_ref, y1_ref, q_ref, ya_ref, car_ref):
        @pl.when(pl.program_id(0) == 0)
        def _():
            car_ref[...] = jnp.zeros_like(car_ref)

        u = ua_ref[...]
        ub = u.astype(BF16)
        for k in range(4):
            bu = _dot(ub[:, 128 * k:128 * (k + 1)], bexp_ref[k])
            sre_ref[:, 512 * k:512 * (k + 1)] = bu[:, :512]
            sim_ref[:, 512 * k:512 * (k + 1)] = bu[:, 512:]
        carry = _scan_tile(sre_ref, sim_ref, apr_ref, api_ref, apr_ref, api_ref, _load_carry(car_ref), False)
        _store_carry(car_ref, carry)
        for k in range(4):
            blk = slice(512 * k, 512 * (k + 1))
            ks = slice(128 * k, 128 * (k + 1))
            y0 = _dot(sre_ref[:, blk].astype(BF16), cre_ref[k]) + _dot(sim_ref[:, blk].astype(BF16), cimn_ref[k])
            y1_ref[:, ks] = y0 + d_ref[:, ks] * u[:, ks]
        y2 = _gelu(y1_ref[...])
        q = _dot(y2.astype(BF16), wg_ref[...]) + bg_ref[...]
        q_ref[...] = q
        za = za_ref[...]
        ya_ref[...] = ((y2 * _sigmoid(q)) * (za * _sigmoid(za))).astype(BF16)

    return _pcall(
        body, f"s5_fwd_l{l}", (NT,),
        [_rows(W, 0), _rows(W, 1), _full((4, 128, 1024)), _full((4, 512, 128)), _full((4, 512, 128)),
         _full((8, GP)), _full((8, GP)), _full((1, W)), _layer(l, (W, W)), _full((1, W))],
        [_rows(GP), _rows(GP), _rows(W), _rows(W), _rows(W)],
        [SDS((L, GP), F32), SDS((L, GP), F32), SDS((L, W), F32), SDS((L, W), F32), SDS((L, W), BF16)],
        (proj, proj, bexp, cre, cimn, ap_re, ap_im, dsk, wglu, bglu),
        scratch=[pltpu.VMEM((8, GP), F32)], jobs=[job])


def _pool_fwd(l, proj, pw, scale):
    def body(ub_ref, zb_ref, pw_ref, sc_ref, pooled_ref, mixed_ref, yb_ref, buf):
        i = pl.program_id(0)

        @pl.when(i == 0)
        def _():
            buf[0:16, :] = jnp.zeros((16, W), F32)

        u = ub_ref[...]
        buf[16:16 + TM, :] = u
        t = i * TM + lax.broadcasted_iota(jnp.int32, (TM, 128), 0)
        for gi, win in enumerate(WINS):
            cs = slice(128 * gi, 128 * (gi + 1))
            acc = u[:, cs]
            for k in range(1, win):
                acc = acc + buf[16 - k:16 - k + TM, cs]
            cnt = jnp.minimum(t + 1, win).astype(F32)
            pb = (acc / cnt - u[:, cs]).astype(BF16)
            pooled_ref[:, cs] = pb
            mixed_ref[:, cs] = _dot(pb, pw_ref[gi])
        zb = zb_ref[...]
        yb_ref[...] = ((mixed_ref[...] * sc_ref[...]) * (zb * _sigmoid(zb))).astype(BF16)
        buf[0:16, :] = buf[TM:TM + 16, :]

    return pl.pallas_call(
        body, name=f"pool_fwd_l{l}", grid=(NT,),
        in_specs=[_rows(W, 2), _rows(W, 3), _full((4, 128, 128)), _full((1, W))],
        out_specs=[_rows(W), _rows(W), _rows(W)],
        out_shape=[SDS((L, W), BF16), SDS((L, W), F32), SDS((L, W), BF16)],
        scratch_shapes=[pltpu.VMEM((TM + 16, W), F32)],
        compiler_params=_params("arbitrary"),
    )(proj, proj, pw, scale)


def _merge_out(l, ya, yb, proj, x, wa, wb, wo):
    def body(ya_ref, yb_ref, ga_ref, gb_ref, x_ref, wa_ref, wb_ref, wo_ref, pa_ref, pb_ref, mg_ref, xo_ref):
        ya = ya_ref[...]
        yb = yb_ref[...]
        for j in range(4):
            cs = slice(256 * j, 256 * (j + 1))
            pa_ref[:, cs] = _dot(ya, wa_ref[j])
            pb_ref[:, cs] = _dot(yb, wb_ref[j])
        merged = _sigmoid(ga_ref[...]) * pa_ref[...] + _sigmoid(gb_ref[...]) * pb_ref[...]
        mb = merged.astype(BF16)
        mg_ref[...] = mb
        xo_ref[...] = x_ref[...] + _dot(mb, wo_ref[...])

    return pl.pallas_call(
        body, name=f"merge_out_l{l}", grid=(NT,),
        in_specs=[_rows(W), _rows(W), _rows(D, 2), _rows(D, 3), _rows(D),
                  _layer(l, (4, W, 256)), _layer(l, (4, W, 256)), _layer(l, (D, D))],
        out_specs=[_rows(D), _rows(D), _rows(D), _rows(D)],
        out_shape=[SDS((L, D), F32), SDS((L, D), F32), SDS((L, D), BF16), SDS((L, D), F32)],
        compiler_params=_params("arbitrary"),
    )(ya, yb, proj, proj, x, wa, wb, wo)


def _loss_head(x, gf, target):
    def body(x_ref, g_ref, t_ref, loss_ref, dx_ref, dg_ref):
        @pl.when(pl.program_id(0) == 0)
        def _():
            loss_ref[...] = jnp.zeros_like(loss_ref)
            dg_ref[...] = jnp.zeros_like(dg_ref)

        xv = x_ref[...]
        g = g_ref[...]
        r = lax.rsqrt(jnp.mean(xv * xv, axis=-1, keepdims=True) + EPS)
        xn = xv * r
        err = xn * g - t_ref[...]
        part = jnp.sum(jnp.mean(err * err, axis=-1, keepdims=True), axis=0, keepdims=True)
        loss_ref[...] += 0.5 * part
        dy = err * (1.0 / D)
        dg_ref[...] += jnp.sum(dy * xn, axis=0, keepdims=True)
        dxn = dy * g
        dx_ref[...] = r * (dxn - xn * jnp.mean(dxn * xn, axis=-1, keepdims=True))

    return pl.pallas_call(
        body, name="loss_head", grid=(NT,),
        in_specs=[_rows(D), _full((1, D)), _rows(D)],
        out_specs=[_full((2, 128)), _rows(D), _full((1, D))],
        out_shape=[SDS((2, 128), F32), SDS((L, D), F32), SDS((1, D), F32)],
        compiler_params=_params("arbitrary"),
    )(x, gf, target)


def _merge_out_bwd(l, dxn, mg, proj, pa, pb, ya, yb, wo, wa, wb):
    def body(dx_ref, mg_ref, ga_ref, gb_ref, pa_ref, pb_ref, ya_ref, yb_ref, wo_ref, wa_ref, wb_ref,
             dg_ref, dya_ref, dyb_ref, dwo_ref, dwa_ref, dwb_ref):
        @pl.when(pl.program_id(0) == 0)
        def _():
            dwo_ref[...] = jnp.zeros_like(dwo_ref)
            dwa_ref[...] = jnp.zeros_like(dwa_ref)
            dwb_ref[...] = jnp.zeros_like(dwb_ref)

        dxb = dx_ref[...].astype(BF16)
        dm = _dot_nt(dxb, wo_ref[...])
        sa = _sigmoid(ga_ref[...])
        sb = _sigmoid(gb_ref[...])
        dg_ref[:, :D] = dm * pa_ref[...] * (sa * (1.0 - sa))
        dg_ref[:, D:] = dm * pb_ref[...] * (sb * (1.0 - sb))
        dpa = (dm * sa).astype(BF16)
        dpb = (dm * sb).astype(BF16)
        ya = ya_ref[...]
        yb = yb_ref[...]
        dya = jnp.zeros((TM, W), F32)
        dyb = jnp.zeros((TM, W), F32)
        for j in range(4):
            cs = slice(256 * j, 256 * (j + 1))
            dya = dya + _dot_nt(dpa[:, cs], wa_ref[j])
            dyb = dyb + _dot_nt(dpb[:, cs], wb_ref[j])
            dwa_ref[j] += _dot_tn(ya, dpa[:, cs])
            dwb_ref[j] += _dot_tn(yb, dpb[:, cs])
        dya_ref[...] = dya
        dyb_ref[...] = dyb
        dwo_ref[...] += _dot_tn(mg_ref[...], dxb)

    return pl.pallas_call(
        body, name=f"merge_out_bwd_l{l}", grid=(NT,),
        in_specs=[_rows(D), _rows(D), _rows(D, 2), _rows(D, 3), _rows(D), _rows(D), _rows(W), _rows(W),
                  _layer(l, (D, D)), _layer(l, (4, W, 256)), _layer(l, (4, W, 256))],
        out_specs=[_rows(2 * D, 1), _rows(W), _rows(W), _full((D, D)), _full((4, W, 256)), _full((4, W, 256))],
        out_shape=[SDS((L, NIN), F32), SDS((L, W), F32), SDS((L, W), F32),
                   SDS((D, D), F32), SDS((4, W, 256), F32), SDS((4, W, 256), F32)],
        compiler_params=_params("arbitrary"),
    )(dxn, mg, proj, proj, pa, pb, ya, yb, wo, wa, wb)


def _pool_bwd(l, dyb, proj, mixed, pooled, pw, scale, dproj):
    def body(dyb_ref, zb_ref, mixed_ref, pooled_ref, pw_ref, sc_ref, _, db_ref, dpw_ref, dsc_ref, buf):
        i = pl.program_id(0)
        tile = NT - 1 - i

        @pl.when(i == 0)
        def _():
            dpw_ref[...] = jnp.zeros_like(dpw_ref)
            dsc_ref[...] = jnp.zeros_like(dsc_ref)
            buf[TM:TM + 16, :] = jnp.zeros((16, W), F32)

        dyb = dyb_ref[...]
        zb = zb_ref[...]
        mixed = mixed_ref[...]
        sc = sc_ref[...]
        sg = _sigmoid(zb)
        dyb0 = dyb * (zb * sg)
        db_ref[:, W:] = dyb * (mixed * sc) * (sg * (1.0 + zb * (1.0 - sg)))
        dsc_ref[...] += jnp.sum(dyb0 * mixed, axis=0, keepdims=True)
        dmix = (dyb0 * sc).astype(BF16)
        t = tile * TM + lax.broadcasted_iota(jnp.int32, (TM, 128), 0)
        for gi, win in enumerate(WINS):
            cs = slice(128 * gi, 128 * (gi + 1))
            dpw_ref[gi] += _dot_tn(pooled_ref[:, cs], dmix[:, cs])
            dpool = _dot_nt(dmix[:, cs], pw_ref[gi])
            cnt = jnp.minimum(t + 1, win).astype(F32)
            e = dpool / cnt
            buf[0:TM, cs] = e
            acc = e - dpool
            for k in range(1, win):
                acc = acc + buf[k:k + TM, cs]
            db_ref[:, cs] = acc
        buf[TM:TM + 16, :] = buf[0:16, :]

    return pl.pallas_call(
        body, name=f"pool_bwd_l{l}", grid=(NT,),
        in_specs=[_rows(W, 0, True), _rows(W, 3, True), _rows(W, 0, True), _rows(W, 0, True),
                  _full((4, 128, 128)), _full((1, W)), _ANY],
        out_specs=[_rows(2 * W, 1, True), _full((4, 128, 128)), _full((1, W))],
        out_shape=[SDS((L, NIN), F32), SDS((4, 128, 128), F32), SDS((1, W), F32)],
        input_output_aliases={6: 0},
        scratch_shapes=[pltpu.VMEM((TM + 16, W), F32)],
        compiler_params=_params("arbitrary"),
    )(dyb, proj, mixed, pooled, pw, scale, dproj)


def _s5_bwd(l, dya, proj, y1, q, sre, sim, cret, cimnt, bret, bimt, st_re, st_im, cr_re, cr_im, dsk, wglu, dproj, jobs=()):
    def halo(i):
        return (jnp.maximum((NT - 1 - i) * (TM // 8) - 1, 0), 0)

    def body(dya_ref, ua_ref, za_ref, y1_ref, q_ref, sre_ref, sim_ref, hre_ref, him_ref,
             cret_ref, cimnt_ref, bret_ref, bimt_ref, st_re_ref, st_im_ref, cr_re_ref, cr_im_ref, d_ref, wg_ref, _,
             da_ref, dwg_ref, dbg_ref, dd_ref, dcre_ref, dcimn_ref, dbre_ref, dbim_ref, dare_ref, daim_ref,
             lre, lim, car_ref):
        i = pl.program_id(0)
        tile = NT - 1 - i

        @pl.when(i == 0)
        def _():
            for ref in (dwg_ref, dbg_ref, dd_ref, dcre_ref, dcimn_ref, dbre_ref, dbim_ref, dare_ref, daim_ref, car_ref):
                ref[...] = jnp.zeros_like(ref)

        u = ua_ref[...]
        za = za_ref[...]
        y1 = y1_ref[...]
        dya = dya_ref[...]
        y2 = _gelu(y1)
        sg = _sigmoid(q_ref[...])
        sgz = _sigmoid(za)
        dy3 = dya * (za * sgz)
        da_ref[:, W:] = dya * (y2 * sg) * (sgz * (1.0 + za * (1.0 - sgz)))
        dq = dy3 * y2 * (sg * (1.0 - sg))
        dqb = dq.astype(BF16)
        dy2 = dy3 * sg + _dot_nt(dqb, wg_ref[...])
        dwg_ref[...] += _dot_tn(y2.astype(BF16), dqb)
        dbg_ref[...] += jnp.sum(dq, axis=0, keepdims=True)
        dy1 = dy2 * _gelu_grad(y1)
        dd_ref[...] += jnp.sum(dy1 * u, axis=0, keepdims=True)
        dy1b = dy1.astype(BF16)
        ub = u.astype(BF16)
        for k in range(4):
            blk = slice(512 * k, 512 * (k + 1))
            ks = slice(128 * k, 128 * (k + 1))
            lre[:, blk] = _dot(dy1b[:, ks], cret_ref[k])
            lim[:, blk] = _dot(dy1b[:, ks], cimnt_ref[k])
            dcre_ref[k] += _dot_tn(sre_ref[:, blk].astype(BF16), dy1b[:, ks])
            dcimn_ref[k] += _dot_tn(sim_ref[:, blk].astype(BF16), dy1b[:, ks])
        carry = _scan_tile(lre, lim, st_re_ref, st_im_ref, cr_re_ref, cr_im_ref, _load_carry(car_ref), True)
        _store_carry(car_ref, carry)

        rowid = lax.broadcasted_iota(jnp.int32, (8, 512), 0)
        gate = (tile > 0).astype(F32)

        def chunk(c, _):
            rows = pl.ds(pl.multiple_of(c * 8, 8), 8)
            prows = pl.ds(pl.multiple_of(jnp.maximum(c - 1, 0) * 8, 8), 8)
            for lb in range(GP // 512):
                cols = slice(lb * 512, (lb + 1) * 512)
                sr = sre_ref[rows, cols]
                si = sim_ref[rows, cols]
                pr = jnp.where(c == 0, hre_ref[7:8, cols] * gate, sre_ref[prows, cols][7:8])
                pi = jnp.where(c == 0, him_ref[7:8, cols] * gate, sim_ref[prows, cols][7:8])
                sr = jnp.where(rowid == 0, pr, pltpu.roll(sr, 1, 0))
                si = jnp.where(rowid == 0, pi, pltpu.roll(si, 1, 0))
                lr = lre[rows, cols]
                li = lim[rows, cols]
                dare_ref[:, cols] += sr * lr + si * li
                daim_ref[:, cols] += sr * li - si * lr
            return 0

        lax.fori_loop(0, TM // 8, chunk, 0)

        for k in range(4):
            blk = slice(512 * k, 512 * (k + 1))
            ks = slice(128 * k, 128 * (k + 1))
            lrb = lre[:, blk].astype(BF16)
            lib = lim[:, blk].astype(BF16)
            da_ref[:, ks] = dy1[:, ks] * d_ref[:, ks] + _dot(lrb, bret_ref[k]) + _dot(lib, bimt_ref[k])
            dbre_ref[k] += _dot_tn(ub[:, ks], lrb)
            dbim_ref[k] += _dot_tn(ub[:, ks], lib)

    return _pcall(
        body, f"s5_bwd_l{l}", (NT,),
        [_rows(W, 0, True), _rows(W, 0, True), _rows(W, 1, True), _rows(W, 0, True), _rows(W, 0, True),
         _rows(GP, 0, True), _rows(GP, 0, True),
         pl.BlockSpec((8, GP), halo), pl.BlockSpec((8, GP), halo),
         _full((4, 128, 512)), _full((4, 128, 512)), _full((4, 512, 128)), _full((4, 512, 128)),
         _full((8, GP)), _full((8, GP)), _full((8, GP)), _full((8, GP)), _full((1, W)), _layer(l, (W, W)), _ANY],
        [_rows(2 * W, 0, True), _full((W, W)), _full((1, W)), _full((1, W)),
         _full((4, 512, 128)), _full((4, 512, 128)), _full((4, 128, 512)), _full((4, 128, 512)),
         _full((8, GP)), _full((8, GP))],
        [SDS((L, NIN), F32), SDS((W, W), F32), SDS((1, W), F32), SDS((1, W), F32),
         SDS((4, 512, 128), F32), SDS((4, 512, 128), F32), SDS((4, 128, 512), F32), SDS((4, 128, 512), F32),
         SDS((8, GP), F32), SDS((8, GP), F32)],
        (dya, proj, proj, y1, q, sre, sim, sre, sim, cret, cimnt, bret, bimt, st_re, st_im, cr_re, cr_im, dsk, wglu,
         dproj),
        scratch=[pltpu.VMEM((TM, GP), F32), pltpu.VMEM((TM, GP), F32), pltpu.VMEM((8, GP), F32)],
        aliases={19: 0}, jobs=jobs)


def _inproj_dw(l, h, dproj, jobs=()):
    def body(h_ref, dp_ref, dw_ref, db_ref):
        @pl.when(pl.program_id(1) == 0)
        def _():
            dw_ref[...] = jnp.zeros_like(dw_ref)
            db_ref[...] = jnp.zeros_like(db_ref)

        dp = dp_ref[...]
        dw_ref[...] += _dot_tn(h_ref[...], dp.astype(BF16))
        db_ref[...] += jnp.sum(dp, axis=0, keepdims=True)

    return _pcall(
        body, f"inproj_dw_l{l}", (4, NT),
        [pl.BlockSpec((TM, D), lambda j, i: (i, 0)), pl.BlockSpec((TM, 1024), lambda j, i: (i, j))],
        [pl.BlockSpec((None, D, 1024), lambda j, i: (j, 0, 0)), pl.BlockSpec((1, 1024), lambda j, i: (0, j))],
        [SDS((4, D, 1024), F32), SDS((1, NIN), F32)],
        (h, dproj), jobs=jobs)


def _inproj_dx(l, dproj, w, x, g, dxn, jobs=()):
    def body(dp_ref, w_ref, x_ref, g_ref, dxn_ref, dx_ref, dg_ref):
        @pl.when(pl.program_id(0) == 0)
        def _():
            dg_ref[...] = jnp.zeros_like(dg_ref)

        dh = jnp.zeros((TM, D), F32)
        for j in range(4):
            dh = dh + _dot_nt(dp_ref[:, j * 1024:(j + 1) * 1024].astype(BF16), w_ref[j])
        xv = x_ref[...]
        r = lax.rsqrt(jnp.mean(xv * xv, axis=-1, keepdims=True) + EPS)
        xn = xv * r
        dg_ref[...] += jnp.sum(dh * xn, axis=0, keepdims=True)
        dn = dh * g_ref[...]
        dx_ref[...] = dxn_ref[...] + r * (dn - xn * jnp.mean(dn * xn, axis=-1, keepdims=True))

    return _pcall(
        body, f"inproj_dx_l{l}", (NT,),
        [_rows(NIN), _layer(l, (4, D, 1024)), _rows(D), _full((1, D)), _rows(D)],
        [_rows(D), _full((1, D))],
        [SDS((L, D), F32), SDS((1, D), F32)],
        (dproj, w, x, g, dxn), jobs=jobs)


def _discretize(log_dt, lam_re, lam_im, b_re, b_im):
    dt = jnp.exp(log_dt)[:, None]
    mag = jnp.exp(lam_re * dt)
    ang = lam_im * dt
    abar_re = mag * jnp.cos(ang)
    abar_im = mag * jnp.sin(ang)
    num_re = abar_re - 1.0
    num_im = abar_im
    den = lam_re * lam_re + lam_im * lam_im
    coef_re = (num_re * lam_re + num_im * lam_im) / den
    coef_im = (num_im * lam_re - num_re * lam_im) / den
    bbar_re = coef_re[..., None] * b_re - coef_im[..., None] * b_im
    bbar_im = coef_re[..., None] * b_im + coef_im[..., None] * b_re
    return abar_re, abar_im, bbar_re, bbar_im


def _powers(abar_re, abar_im):
    ar, ai = abar_re.reshape(1, GP), abar_im.reshape(1, GP)
    rows_re, rows_im = [ar], [ai]
    for _ in range(7):
        pr, pi = rows_re[-1], rows_im[-1]
        rows_re.append(pr * ar - pi * ai)
        rows_im.append(pr * ai + pi * ar)
    neg_im = [-r for r in rows_im]
    return (jnp.concatenate(rows_re, axis=0), jnp.concatenate(rows_im, axis=0), jnp.concatenate(neg_im, axis=0),
            jnp.concatenate(rows_re[::-1], axis=0), jnp.concatenate(neg_im[::-1], axis=0))


_EYE8 = functools.partial(jnp.eye, 8, dtype=F32)


def _expand_in(b):
    return jnp.einsum("kgpc,gh->kgchp", b.reshape(4, 8, P, C), _EYE8()).reshape(4, 128, 512)


def _extract_in(e):
    return jnp.einsum("kgchp,gh->kgpc", e.reshape(4, 8, C, 8, P), _EYE8()).reshape(G, P, C)


def _expand_out(c):
    return jnp.einsum("kgcp,gh->kgphc", c.reshape(4, 8, C, P), _EYE8()).reshape(4, 512, 128)


def _extract_out(e):
    return jnp.einsum("kgphc,gh->kgcp", e.reshape(4, 8, P, 8, C), _EYE8()).reshape(G, C, P)


SMALL = ("norm_g", "b_in", "ssm_log_dt", "ssm_lam_re", "ssm_lam_im", "ssm_b_re", "ssm_b_im",
         "ssm_c_re", "ssm_c_im", "ssm_d", "ssm_b_glu", "pool_w", "pool_scale")
BIG = ("w_in", "ssm_w_glu", "w_branch_a", "w_branch_b", "w_out")


def _step(x, target, w, m, v, place):
    sp = {n: w[n] for n in SMALL}
    final_norm_g = w["final_norm_g"]
    wbuf = dict(zip(BIG, _cast_own(place, [w[n] for n in BIG])))
    saved = []
    for l in range(DEPTH):
        disc_in = (sp["ssm_log_dt"][l], sp["ssm_lam_re"][l], sp["ssm_lam_im"][l], sp["ssm_b_re"][l], sp["ssm_b_im"][l])
        (abar_re, abar_im, bbar_re, bbar_im), disc_vjp = jax.vjp(_discretize, *disc_in)
        pw_re, pw_im, pw_imn, pw_re_rev, pw_imn_rev = _powers(abar_re, abar_im)
        b_re_x, b_im_x = _expand_in(bbar_re), _expand_in(bbar_im)
        c_re_x, c_imn_x = _expand_out(sp["ssm_c_re"][l]), _expand_out(-sp["ssm_c_im"][l])
        g = sp["norm_g"][l].reshape(1, D)
        dsk = sp["ssm_d"][l].reshape(1, W)
        scale = sp["pool_scale"][l].reshape(1, W)
        pw = sp["pool_w"][l].astype(BF16)

        if l == 0:
            (wbuf["w_in"],) = _comm_only("gather_w_in_l0", _GatherJob([wbuf["w_in"]], 0))[0]
        (h, proj), res = _norm_inproj(l, x, g, wbuf["w_in"], sp["b_in"][l].reshape(1, NIN),
                                      _GatherJob([wbuf[n] for n in BIG[1:]], l))
        wbuf.update(zip(BIG[1:], res[0][0]))
        wg = dict(wbuf, ssm_w_glu=wbuf["ssm_w_glu"].reshape(DEPTH, W, W), w_out=wbuf["w_out"].reshape(DEPTH, D, D))
        (sre, sim, y1, q, ya), res = _s5_fwd(
            l, proj, jnp.concatenate([b_re_x, b_im_x], axis=2).astype(BF16), c_re_x.astype(BF16), c_imn_x.astype(BF16),
            pw_re, pw_im, dsk, wg["ssm_w_glu"], sp["ssm_b_glu"][l].reshape(1, W),
            _GatherJob([wbuf["w_in"]], l + 1) if l + 1 < DEPTH else None)
        if res:
            (wbuf["w_in"],) = res[0][0]
            wg["w_in"] = wbuf["w_in"]
        pooled, mixed, yb = _pool_fwd(l, proj, pw, scale)
        pa, pb, mg, x_next = _merge_out(l, ya, yb, proj, x, wg["w_branch_a"], wg["w_branch_b"], wg["w_out"])
        saved.append(dict(x=x, g=g, dsk=dsk, scale=scale, pw=pw, h=h, proj=proj, sre=sre, sim=sim, y1=y1, q=q, ya=ya,
                          pooled=pooled, mixed=mixed, yb=yb, pa=pa, pb=pb, mg=mg, disc_vjp=disc_vjp,
                          powers=(pw_re, pw_imn, pw_re_rev, pw_imn_rev),
                          b_re_x=b_re_x, b_im_x=b_im_x, c_re_x=c_re_x, c_imn_x=c_imn_x))
        x = x_next

    loss, dx, dgf = _loss_head(x, final_norm_g.reshape(1, D), target)

    gs = {n: [None] * DEPTH for n in SMALL}
    big = {n: None for n in BIG}

    def adamw_layer(l, st, scatter=None):
        res = None
        for n, mine, other in zip(BIG, st["shard"], st["other"]):
            big[n], r = _adamw_big(f"adamw_{n}_l{l}", l, w[n], m[n], v[n], mine, other, big[n],
                                   [scatter] if n == "w_in" else [])
            res = r if n == "w_in" else res
        return res

    def shard_sums(l, st):
        return [_shard_sum(f"shard_sum_{n}_l{l}", place, o, r, rb)
                for n, o, r, rb in zip(BIG, st["own"], st["recv"], st["rbuf"])]

    prev = None
    for l in reversed(range(DEPTH)):
        s = saved[l]
        dproj, dya, dyb, dwo, dwa, dwb = _merge_out_bwd(
            l, dx, s["mg"], s["proj"], s["pa"], s["pb"], s["ya"], s["yb"],
            wg["w_out"], wg["w_branch_a"], wg["w_branch_b"])
        dproj, dpw, dsc = _pool_bwd(l, dyb, s["proj"], s["mixed"], s["pooled"], s["pw"], s["scale"], dproj)
        t = lambda a: jnp.swapaxes(a, 1, 2).astype(BF16)
        (dproj, dwg, dbg, dd, dcre, dcimn, dbre, dbim, dare, daim), res = _s5_bwd(
            l, dya, s["proj"], s["y1"], s["q"], s["sre"], s["sim"],
            t(s["c_re_x"]), t(s["c_imn_x"]), t(s["b_re_x"]), t(s["b_im_x"]),
            *s["powers"], s["dsk"], wg["ssm_w_glu"], dproj, [_ScatterJob(prev["parts"])] if prev else [])
        if prev:
            prev["rbuf"] = res[0][1]
            prev["shard"] = shard_sums(l + 1, prev)
        (dwin, dbin), res = _inproj_dw(l, s["h"], dproj, [_SiblingJob(prev["shard"], False)] if prev else [])
        if prev:
            prev["other"] = res[0][1]
        own = [dwin, dwg.reshape(4, W // 4, W), dwa, dwb, dwo.reshape(4, D // 4, D)]
        (dx, dg), res = _inproj_dx(l, dproj, wg["w_in"], s["x"], s["g"], dx, [_SiblingJob(own, True)])
        recv = res[0][1]
        parts = [_pair_sum_bf16(f"pair_sum_{n}_l{l}", place, o, r) for n, o, r in zip(BIG, own, recv)]
        cur = dict(own=own, recv=recv, parts=parts)
        if prev:
            res = adamw_layer(l + 1, prev, _ScatterJob(parts) if l == 0 else None)
            if l == 0:
                cur["rbuf"] = res[0][1]
        prev = cur

        d_abar_re = jnp.sum(dare, axis=0).reshape(G, P)
        d_abar_im = jnp.sum(daim, axis=0).reshape(G, P)
        dlog_dt, dlam_re, dlam_im, db_re, db_im = s["disc_vjp"](
            (d_abar_re, d_abar_im, _extract_in(dbre), _extract_in(dbim)))
        gs["norm_g"][l] = dg.reshape(D)
        gs["b_in"][l] = dbin.reshape(NIN)
        gs["ssm_log_dt"][l] = dlog_dt
        gs["ssm_lam_re"][l] = dlam_re
        gs["ssm_lam_im"][l] = dlam_im
        gs["ssm_b_re"][l] = db_re
        gs["ssm_b_im"][l] = db_im
        gs["ssm_c_re"][l] = _extract_out(dcre)
        gs["ssm_c_im"][l] = -_extract_out(dcimn)
        gs["ssm_d"][l] = dd.reshape(W)
        gs["ssm_b_glu"][l] = dbg.reshape(W)
        gs["pool_w"][l] = dpw
        gs["pool_scale"][l] = dsc.reshape(W)
    prev["shard"] = shard_sums(0, prev)
    prev["other"] = _comm_only("shards_to_sibling_l0", _SiblingJob(prev["shard"], False))[1]
    adamw_layer(0, prev)
    gs = {n: jnp.stack(v) for n, v in gs.items()}
    return loss, dx, gs, dgf, big


def _place():
    x, y, c = lax.axis_index("x"), lax.axis_index("y"), lax.axis_index("c")
    chips = [(1 - x, y), (x, 1 - y), (1 - x, 1 - y)]
    return x, y, c, 2 * x + y, chips, [2 * cx + cy for cx, cy in chips]


def _remote(src, dst, ssem, rsem, dev):
    return pltpu.make_async_remote_copy(src_ref=src, dst_ref=dst, send_sem=ssem, recv_sem=rsem,
                                        device_id=dev, device_id_type=MESH)


class _GatherJob:
    def __init__(self, bufs, l):
        self.srcs, self.bufs, self.news, self.l = [], list(bufs), [], l
        self.scratch = [pltpu.SemaphoreType.DMA((len(self.bufs), 3))] * 4

    def _half(self, ref, k, h):
        rows = ref.shape[2] // 2
        return ref.at[self.l, k, pl.ds(pl.multiple_of(h * rows, 8), rows), :]

    def _ici(self, bufs, sems, a, j, k):
        _, _, c, _, chips, _ = _place()
        blk = self._half(bufs[a], k, c)
        return _remote(blk, blk, sems[0].at[a, j], sems[1].at[a, j], (*chips[j], c))

    def _d2d(self, bufs, sems, a, j, k, h):
        x, y, c, _, _, _ = _place()
        blk = self._half(bufs[a], k, h)
        return _remote(blk, blk, sems[2].at[a, j], sems[3].at[a, j], (x, y, 1 - c))

    def start(self, srcs, bufs, news, sems):
        me = _place()[3]
        for a in range(len(self.bufs)):
            for j in range(3):
                self._ici(bufs, sems, a, j, me).start()

    def finish(self, srcs, bufs, news, sems):
        _, _, c, me, _, cid = _place()
        pairs = [(a, j) for a in range(len(self.bufs)) for j in range(3)]
        for a, j in pairs:
            self._ici(bufs, sems, a, j, cid[j]).wait_recv()
            self._d2d(bufs, sems, a, j, cid[j], c).start()
        for a, j in pairs:
            self._d2d(bufs, sems, a, j, cid[j], 1 - c).wait_recv()
        for a, j in pairs:
            self._ici(bufs, sems, a, j, me).wait_send()
            self._d2d(bufs, sems, a, j, cid[j], c).wait_send()


class _SiblingJob:
    def __init__(self, srcs, rows_half):
        self.srcs, self.bufs, self.rows_half = list(srcs), [], rows_half
        self.news = [SDS((s.shape[0], s.shape[1] // 2, s.shape[2]) if rows_half else s.shape, s.dtype) for s in srcs]
        self.scratch = [pltpu.SemaphoreType.DMA((len(self.srcs),))] * 2

    def _copy(self, srcs, news, sems, a):
        x, y, c, _, _, _ = _place()
        src = srcs[a]
        if self.rows_half:
            rows = src.shape[1] // 2
            src = src.at[:, pl.ds(pl.multiple_of((1 - c) * rows, 8), rows), :]
        return _remote(src, news[a], sems[0].at[a], sems[1].at[a], (x, y, 1 - c))

    def start(self, srcs, bufs, news, sems):
        for a in range(len(self.srcs)):
            self._copy(srcs, news, sems, a).start()

    def finish(self, srcs, bufs, news, sems):
        for a in range(len(self.srcs)):
            self._copy(srcs, news, sems, a).wait()


class _ScatterJob:
    def __init__(self, parts):
        self.srcs, self.bufs = list(parts), []
        self.news = [SDS((3,) + p.shape[1:], p.dtype) for p in parts]
        self.scratch = [pltpu.SemaphoreType.DMA((len(self.srcs), 3))] * 2

    def _copy(self, srcs, news, sems, a, j):
        _, _, c, _, chips, cid = _place()
        return _remote(srcs[a].at[cid[j]], news[a].at[j], sems[0].at[a, j], sems[1].at[a, j], (*chips[j], c))

    def start(self, srcs, bufs, news, sems):
        for a in range(len(self.srcs)):
            for j in range(3):
                self._copy(srcs, news, sems, a, j).start()

    def finish(self, srcs, bufs, news, sems):
        for a in range(len(self.srcs)):
            for j in range(3):
                self._copy(srcs, news, sems, a, j).wait()


def _comm_only(name, job):
    return _pcall(None, name, (), [], [], [], [], jobs=[job])[1][0]


def _chip_allgather(name, bufs, chip_major):
    n = len(bufs)

    def body(*refs):
        bufs_ = refs[n:2 * n]
        ssem, rsem, fsem, grsem = refs[2 * n:]
        x, y, c, me, chips, cid = _place()
        sib = (x, y, 1 - c)

        def blk(a, h, k):
            return bufs_[a].at[k, h] if chip_major else bufs_[a].at[h, k]

        sends = []
        for a in range(n):
            for j, (cx, cy) in enumerate(chips):
                cp = _remote(blk(a, c, me), blk(a, c, me), ssem.at[a, j], rsem.at[a, j], (cx, cy, c))
                cp.start()
                sends.append(cp)
        for a in range(n):
            for j, (cx, cy) in enumerate(chips):
                got = blk(a, c, cid[j])
                _remote(got, got, ssem.at[a, j], rsem.at[a, j], (cx, cy, c)).wait_recv()
                cp = _remote(got, got, fsem.at[a, j], grsem.at[a, j], sib)
                cp.start()
                sends.append(cp)
        for a in range(n):
            for j in range(3):
                got = blk(a, 1 - c, cid[j])
                _remote(got, got, fsem.at[a, j], grsem.at[a, j], sib).wait_recv()
        for cp in sends:
            cp.wait_send()

    return pl.pallas_call(
        body, name=name,
        in_specs=[_ANY] * n, out_specs=[_ANY] * n,
        out_shape=[SDS(a.shape, a.dtype) for a in bufs],
        input_output_aliases={a: a for a in range(n)},
        scratch_shapes=[pltpu.SemaphoreType.DMA((n, 3))] * 4,
    )(*bufs)


def _cast_own(place, ws):
    n = len(ws)

    def body(p_ref, *refs):
        for i_ref, o_ref in zip(refs[:n], refs[n:]):
            o_ref[...] = i_ref[...].astype(BF16)

    return pl.pallas_call(
        body, name="cast_own_shards",
        grid_spec=pltpu.PrefetchScalarGridSpec(
            num_scalar_prefetch=1, grid=(DEPTH,),
            in_specs=[pl.BlockSpec((None,) + a.shape[1:], lambda l, p: (l, 0, 0)) for a in ws],
            out_specs=[pl.BlockSpec((None, None) + a.shape[1:], lambda l, p: (l, p[1], 0, 0)) for a in ws]),
        out_shape=[SDS((DEPTH, 4) + a.shape[1:], BF16) for a in ws],
        compiler_params=_params("arbitrary"),
    )(place, *ws)


def _half_tiles(a_):
    rows = a_ // 2
    ta = min(rows, 256)
    return rows, ta, rows // ta


def _pair_sum_bf16(name, place, own, recv):
    _, a_, b_ = own.shape
    rows, ta, nh = _half_tiles(a_)

    def body(p_ref, own_ref, recv_ref, out_ref):
        out_ref[...] = (own_ref[...] + recv_ref[...]).astype(BF16)

    return pl.pallas_call(
        body, name=name,
        grid_spec=pltpu.PrefetchScalarGridSpec(
            num_scalar_prefetch=1, grid=(4, nh),
            in_specs=[pl.BlockSpec((None, ta, b_), lambda s, i, p: (s, p[0] * nh + i, 0)),
                      pl.BlockSpec((None, ta, b_), lambda s, i, p: (s, i, 0))],
            out_specs=pl.BlockSpec((None, ta, b_), lambda s, i, p: (s, i, 0))),
        out_shape=SDS((4, rows, b_), BF16),
        compiler_params=_params("arbitrary", "arbitrary"),
    )(place, own, recv)


def _shard_sum(name, place, own, recv, rbuf):
    _, a_, b_ = own.shape
    rows, ta, nh = _half_tiles(a_)

    def body(p_ref, own_ref, recv_ref, r_ref, out_ref):
        acc = own_ref[...] + recv_ref[...]
        for j in range(3):
            acc = acc + r_ref[j].astype(F32)
        out_ref[...] = acc

    return pl.pallas_call(
        body, name=name,
        grid_spec=pltpu.PrefetchScalarGridSpec(
            num_scalar_prefetch=1, grid=(nh,),
            in_specs=[pl.BlockSpec((None, ta, b_), lambda i, p: (p[1], p[0] * nh + i, 0)),
                      pl.BlockSpec((None, ta, b_), lambda i, p: (p[1], i, 0)),
                      pl.BlockSpec((3, ta, b_), lambda i, p: (0, i, 0))],
            out_specs=pl.BlockSpec((ta, b_), lambda i, p: (i, 0))),
        out_shape=SDS((rows, b_), F32),
        compiler_params=_params("arbitrary"),
    )(place, own, recv, rbuf)


def _small_pair_sum(place, mine, recv):
    n = len(mine)

    def body(p_ref, *refs):
        for m_ref, r_ref, o_ref in zip(refs[:n], refs[n:2 * n], refs[2 * n:]):
            o_ref[...] = m_ref[...] + r_ref[...]

    def whole(a):
        zeros = (0,) * a.ndim
        return pl.BlockSpec(a.shape, lambda i, p: zeros)

    def mine_blk(a):
        zeros = (0,) * a.ndim
        return pl.BlockSpec((None,) + a.shape, lambda i, p: (p[1],) + zeros)

    return pl.pallas_call(
        body, name="small_pair_sum",
        grid_spec=pltpu.PrefetchScalarGridSpec(
            num_scalar_prefetch=1, grid=(1,),
            in_specs=[whole(a) for a in mine] + [whole(a) for a in recv],
            out_specs=[mine_blk(a) for a in mine]),
        out_shape=[SDS((4,) + a.shape, F32) for a in mine],
        compiler_params=_params("arbitrary"),
    )(place, *mine, *recv)


def _adam_math(w, g, m, v):
    m = B1 * m + (1.0 - B1) * g
    v = B2 * v + (1.0 - B2) * (g * g)
    m_hat = m / (1.0 - B1 ** STEP)
    v_hat = v / (1.0 - B2 ** STEP)
    delta = -LR * (m_hat / (jnp.sqrt(v_hat) + EPS_A) + WD * w)
    return delta, m, v


def _adamw_big(name, l, w, m, v, mine, other, prev, jobs=()):
    _, a_, b_ = w.shape
    _, ta, nh = _half_tiles(a_)
    prev = list(prev or [])

    def body(w_ref, m_ref, v_ref, mine_ref, other_ref, *rest):
        g_ref, d_ref, mo_ref, vo_ref = rest[len(prev):]
        g = jnp.where(pl.program_id(0) == lax.axis_index("c"), mine_ref[...], other_ref[...])
        g_ref[...] = g
        d_ref[...], mo_ref[...], vo_ref[...] = _adam_math(w_ref[...], g, m_ref[...], v_ref[...])

    slab = pl.BlockSpec((None, ta, b_), lambda h, i: (l, h * nh + i, 0))
    half = pl.BlockSpec((ta, b_), lambda h, i: (i, 0))
    outs, res = _pcall(
        body, name, (2, nh), [slab, slab, slab, half, half] + [_ANY] * len(prev), [slab] * 4, [SDS(w.shape, F32)] * 4,
        (w, m, v, mine, other, *prev), aliases={5 + k: k for k in range(len(prev))}, jobs=jobs)
    return outs, res


def _adamw_small(name, ws, parts, ms, vs, loss_parts=None):
    k = len(ws)
    extra = [] if loss_parts is None else [loss_parts]

    def chip_sum(p_ref):
        return ((p_ref[0] + p_ref[1]) + p_ref[2]) + p_ref[3]

    def body(*refs):
        w_refs, p_refs, m_refs, v_refs = refs[:k], refs[k:2 * k], refs[2 * k:3 * k], refs[3 * k:4 * k]
        outs = refs[4 * k + len(extra):]
        if extra:
            outs[0][...] = chip_sum(refs[4 * k])
            outs = outs[1:]
        for a in range(k):
            g = chip_sum(p_refs[a])
            outs[a][...] = g
            outs[k + a][...], outs[2 * k + a][...], outs[3 * k + a][...] = _adam_math(
                w_refs[a][...], g, m_refs[a][...], v_refs[a][...])

    like = [SDS(a.shape, F32) for a in ws]
    return pl.pallas_call(
        body, name=name,
        out_shape=([SDS((2, 128), F32)] if extra else []) + like * 4,
        compiler_params=pltpu.CompilerParams(vmem_limit_bytes=VMEM_LIMIT),
    )(*ws, *parts, *ms, *vs, *extra)


WEIGHTS = ("norm_g", "w_in", "b_in", "ssm_log_dt", "ssm_lam_re", "ssm_lam_im", "ssm_b_re", "ssm_b_im", "ssm_c_re",
           "ssm_c_im", "ssm_d", "ssm_w_glu", "ssm_b_glu", "pool_w", "pool_scale", "w_branch_a", "w_branch_b", "w_out",
           "final_norm_g")
REPLICATED = SMALL + ("final_norm_g",)
ADAM_GROUPS = (tuple(n for n in REPLICATED if n not in ("ssm_b_re", "ssm_b_im")), ("ssm_b_re",), ("ssm_b_im",))


def kernel(x, norm_g, w_in, b_in, ssm_log_dt, ssm_lam_re, ssm_lam_im, ssm_b_re, ssm_b_im, ssm_c_re, ssm_c_im, ssm_d, ssm_w_glu, ssm_b_glu, pool_w, pool_scale, w_branch_a, w_branch_b, w_out, final_norm_g, loss_target, m_norm_g, m_w_in, m_b_in, m_ssm_log_dt, m_ssm_lam_re, m_ssm_lam_im, m_ssm_b_re, m_ssm_b_im, m_ssm_c_re, m_ssm_c_im, m_ssm_d, m_ssm_w_glu, m_ssm_b_glu, m_pool_w, m_pool_scale, m_w_branch_a, m_w_branch_b, m_w_out, m_final_norm_g, v_norm_g, v_w_in, v_b_in, v_ssm_log_dt, v_ssm_lam_re, v_ssm_lam_im, v_ssm_b_re, v_ssm_b_im, v_ssm_c_re, v_ssm_c_im, v_ssm_d, v_ssm_w_glu, v_ssm_b_glu, v_pool_w, v_pool_scale, v_w_branch_a, v_w_branch_b, v_w_out, v_final_norm_g):
    w = dict(zip(WEIGHTS, (norm_g, w_in, b_in, ssm_log_dt, ssm_lam_re, ssm_lam_im, ssm_b_re, ssm_b_im, ssm_c_re,
                           ssm_c_im, ssm_d, ssm_w_glu, ssm_b_glu, pool_w, pool_scale, w_branch_a, w_branch_b, w_out,
                           final_norm_g)))
    m = dict(zip(WEIGHTS, (m_norm_g, m_w_in, m_b_in, m_ssm_log_dt, m_ssm_lam_re, m_ssm_lam_im, m_ssm_b_re, m_ssm_b_im,
                           m_ssm_c_re, m_ssm_c_im, m_ssm_d, m_ssm_w_glu, m_ssm_b_glu, m_pool_w, m_pool_scale,
                           m_w_branch_a, m_w_branch_b, m_w_out, m_final_norm_g)))
    v = dict(zip(WEIGHTS, (v_norm_g, v_w_in, v_b_in, v_ssm_log_dt, v_ssm_lam_re, v_ssm_lam_im, v_ssm_b_re, v_ssm_b_im,
                           v_ssm_c_re, v_ssm_c_im, v_ssm_d, v_ssm_w_glu, v_ssm_b_glu, v_pool_w, v_pool_scale,
                           v_w_branch_a, v_w_branch_b, v_w_out, v_final_norm_g)))
    place = jnp.stack([lax.axis_index("c"), 2 * lax.axis_index("x") + lax.axis_index("y")]).astype(jnp.int32)

    loss, dx, gs, dgf, big = _step(x[0], loss_target[0], w, m, v, place)
    grads, delta, new_m, new_v = ({n: big[n][k] for n in BIG} for k in range(4))

    small = [gs[n] for n in SMALL] + [dgf.reshape(2, D // 2), loss]
    recv = _comm_only("small_to_sibling", _SiblingJob(small, False))[1]
    pair_small = _small_pair_sum(place, small, recv)
    small_parts = dict(zip(REPLICATED + ("loss",), _chip_allgather("gather_small", pair_small, True)))

    half = lambda a: a.reshape(2, D // 2)
    w["final_norm_g"], m["final_norm_g"], v["final_norm_g"] = half(final_norm_g), half(m_final_norm_g), half(v_final_norm_g)
    total_loss = None
    for gi, names in enumerate(ADAM_GROUPS):
        outs = _adamw_small(f"adamw_small_{gi}", [w[n] for n in names], [small_parts[n] for n in names],
                            [m[n] for n in names], [v[n] for n in names], small_parts["loss"] if gi == 0 else None)
        if gi == 0:
            total_loss, outs = outs[0][0, 0], outs[1:]
        k = len(names)
        for dst, vals in ((grads, outs[:k]), (delta, outs[k:2 * k]), (new_m, outs[2 * k:3 * k]), (new_v, outs[3 * k:])):
            dst.update(zip(names, vals))
    for dst in (grads, delta, new_m, new_v):
        dst["final_norm_g"] = dst["final_norm_g"].reshape(D)

    return (total_loss, dx[None], *[grads[n] for n in WEIGHTS], *[delta[n] for n in WEIGHTS],
            *[new_m[n] for n in WEIGHTS], *[new_v[n] for n in WEIGHTS])
```

```python
import functools

import jax
import jax.numpy as jnp
from jax import lax
from jax.experimental import pallas as pl
from jax.experimental.pallas import tpu as pltpu

F32, BF16 = jnp.float32, jnp.bfloat16
SDS = jax.ShapeDtypeStruct
MESH = pl.DeviceIdType.MESH

DEPTH = 2
L = 2048
D = 1024
NIN = 4096
W = 512
G, P, C = 32, 64, 16
GP = G * P
WINS = (2, 4, 8, 16)
TM = 256
NT = L // TM
EPS = 1e-6
VMEM_LIMIT = 56 * 2**20

LR, B1, B2, EPS_A, WD, STEP = 0.001, 0.9, 0.999, 1e-08, 0.01, 10


def _params(*sem):
    return pltpu.CompilerParams(dimension_semantics=sem, vmem_limit_bytes=VMEM_LIMIT)


_ANY = pl.BlockSpec(memory_space=pl.ANY)


def _full(shape):
    zeros = (0,) * len(shape)
    return pl.BlockSpec(shape, lambda *_: zeros)


def _layer(l, shape):
    zeros = (0,) * len(shape)
    return pl.BlockSpec((None,) + shape, lambda *_: (l,) + zeros)


def _rows(width, col=0, reverse=False):
    if reverse:
        return pl.BlockSpec((TM, width), lambda i: (NT - 1 - i, col))
    return pl.BlockSpec((TM, width), lambda i: (i, col))


def _pcall(body, name, grid, in_specs, out_specs, out_shape, args, scratch=(), aliases=None, jobs=()):
    in_specs, out_specs, out_shape, args, scratch = list(in_specs), list(out_specs), list(out_shape), list(args), list(scratch)
    aliases = dict(aliases or {})
    jobs = [j for j in jobs if j is not None]
    n_in, n_out, n_scr = len(in_specs), len(out_specs), len(scratch)
    srcs = [s for j in jobs for s in j.srcs]
    bufs = [b for j in jobs for b in j.bufs]
    news = [s for j in jobs for s in j.news]
    aliases.update({n_in + len(srcs) + k: n_out + k for k in range(len(bufs))})

    def hosted(*refs):
        cuts = [n_in, len(srcs), len(bufs), n_out, len(bufs), len(news), n_scr]
        parts, p = [], 0
        for n in cuts:
            parts.append(refs[p:p + n])
            p += n
        ins, src_r, _, outs, buf_r, new_r, scr = parts
        sem_r = refs[p:]
        views, ps, pb, pn, pm = [], 0, 0, 0, 0
        for j in jobs:
            views.append((src_r[ps:ps + len(j.srcs)], buf_r[pb:pb + len(j.bufs)], new_r[pn:pn + len(j.news)],
                          sem_r[pm:pm + len(j.scratch)]))
            ps, pb, pn, pm = ps + len(j.srcs), pb + len(j.bufs), pn + len(j.news), pm + len(j.scratch)

        def run(phase):
            for j, v in zip(jobs, views):
                getattr(j, phase)(*v)

        def at_step(step):
            return functools.reduce(jnp.logical_and, [pl.program_id(d) == step(d) for d in range(len(grid))])

        if not grid:
            run("start")
            run("finish")
            return
        pl.when(at_step(lambda d: 0))(lambda: run("start"))
        body(*ins, *outs, *scr)
        pl.when(at_step(lambda d: grid[d] - 1))(lambda: run("finish"))

    outs = pl.pallas_call(
        hosted if jobs else body, name=name, **({"grid": grid} if grid else {}),
        in_specs=in_specs + [_ANY] * (len(srcs) + len(bufs)), out_specs=out_specs + [_ANY] * (len(bufs) + len(news)),
        out_shape=out_shape + [SDS(b.shape, b.dtype) for b in bufs] + news,
        input_output_aliases=aliases, scratch_shapes=scratch + [s for j in jobs for s in j.scratch],
        compiler_params=_params(*(("arbitrary",) * len(grid))))(*args, *srcs, *bufs)
    res, pb, pn = [], n_out, n_out + len(bufs)
    for j in jobs:
        res.append((list(outs[pb:pb + len(j.bufs)]), list(outs[pn:pn + len(j.news)])))
        pb, pn = pb + len(j.bufs), pn + len(j.news)
    return list(outs[:n_out]), res


def _dot(a, b):
    return jnp.dot(a, b, preferred_element_type=F32)


def _dot_nt(a, b):
    return lax.dot_general(a, b, (((1,), (1,)), ((), ())), preferred_element_type=F32)


def _dot_tn(a, b):
    return lax.dot_general(a, b, (((0,), (0,)), ((), ())), preferred_element_type=F32)


_K0 = 0.7978845608028654
_K1 = 0.044715


def _gelu(x):
    return 0.5 * x * (1.0 + jnp.tanh(_K0 * (x + _K1 * (x * x * x))))


def _gelu_grad(x):
    t = jnp.tanh(_K0 * (x + _K1 * (x * x * x)))
    return 0.5 * (1.0 + t) + 0.5 * x * (1.0 - t * t) * (_K0 * (1.0 + 3.0 * _K1 * x * x))


def _sigmoid(x):
    return jax.nn.sigmoid(x)


def _norm_inproj(l, x, g, w, b, job=None):
    def body(x_ref, g_ref, w_ref, b_ref, h_ref, proj_ref):
        xv = x_ref[...]
        r = lax.rsqrt(jnp.mean(xv * xv, axis=-1, keepdims=True) + EPS)
        hb = ((xv * r) * g_ref[...]).astype(BF16)
        h_ref[...] = hb
        for j in range(4):
            cs = slice(j * 1024, (j + 1) * 1024)
            proj_ref[:, cs] = _dot(hb, w_ref[j]) + b_ref[:, cs]

    return _pcall(
        body, f"norm_inproj_l{l}", (NT,),
        [_rows(D), _full((1, D)), _layer(l, (4, D, 1024)), _full((1, NIN))],
        [_rows(D), _rows(NIN)],
        [SDS((L, D), BF16), SDS((L, NIN), F32)],
        (x, g, w, b), jobs=[job])


def _scan_tile(re_ref, im_ref, st_re, st_im, cr_re, cr_im, carry, reverse):
    rowid = lax.broadcasted_iota(jnp.int32, (8, 512), 0)

    def chunk(ci, carry):
        c = (TM // 8 - 1 - ci) if reverse else ci
        rows = pl.ds(pl.multiple_of(c * 8, 8), 8)
        new = []
        for lb in range(GP // 512):
            cols = slice(lb * 512, (lb + 1) * 512)
            vr = re_ref[rows, cols]
            vi = im_ref[rows, cols]
            for d, prow in ((1, 0), (2, 1), (4, 3)):
                ar = st_re[prow:prow + 1, cols]
                ai = st_im[prow:prow + 1, cols]
                if reverse:
                    keep = rowid < 8 - d
                    sr = jnp.where(keep, pltpu.roll(vr, 8 - d, 0), 0.0)
                    si = jnp.where(keep, pltpu.roll(vi, 8 - d, 0), 0.0)
                else:
                    keep = rowid >= d
                    sr = jnp.where(keep, pltpu.roll(vr, d, 0), 0.0)
                    si = jnp.where(keep, pltpu.roll(vi, d, 0), 0.0)
                vr, vi = vr + ar * sr - ai * si, vi + ar * si + ai * sr
            cr, ci_ = carry[2 * lb], carry[2 * lb + 1]
            pr = cr_re[:, cols]
            pi = cr_im[:, cols]
            vr, vi = vr + pr * cr - pi * ci_, vi + pr * ci_ + pi * cr
            re_ref[rows, cols] = vr
            im_ref[rows, cols] = vi
            if reverse:
                new += [vr[0:1], vi[0:1]]
            else:
                new += [vr[7:8], vi[7:8]]
        return tuple(new)

    return lax.fori_loop(0, TM // 8, chunk, carry)


def _load_carry(car_ref):
    return tuple(car_ref[r:r + 1, lb * 512:(lb + 1) * 512] for lb in range(GP // 512) for r in (0, 1))


def _store_carry(car_ref, carry):
    for lb in range(GP // 512):
        car_ref[0:1, lb * 512:(lb + 1) * 512] = carry[2 * lb]
        car_ref[1:2, lb * 512:(lb + 1) * 512] = carry[2 * lb + 1]


def _s5_fwd(l, proj, bexp, cre, cimn, ap_re, ap_im, dsk, wglu, bglu, job=None):
    def body(ua_ref, za_ref, bexp_ref, cre_ref, cimn_ref, apr_ref, api_ref, d_ref, wg_ref, bg_ref,
             sre_ref, sim_ref, y1_ref, q_ref, ya_ref, car_ref):
        @pl.when(pl.program_id(0) == 0)
        def _():
            car_ref[...] = jnp.zeros_like(car_ref)

        u = ua_ref[...]
        ub = u.astype(BF16)
        for k in range(4):
            bu = _dot(ub[:, 128 * k:128 * (k + 1)], bexp_ref[k])
            sre_ref[:, 512 * k:512 * (k + 1)] = bu[:, :512]
            sim_ref[:, 512 * k:512 * (k + 1)] = bu[:, 512:]
        carry = _scan_tile(sre_ref, sim_ref, apr_ref, api_ref, apr_ref, api_ref, _load_carry(car_ref), False)
        _store_carry(car_ref, carry)
        for k in range(4):
            blk = slice(512 * k, 512 * (k + 1))
            ks = slice(128 * k, 128 * (k + 1))
            y0 = _dot(sre_ref[:, blk].astype(BF16), cre_ref[k]) + _dot(sim_ref[:, blk].astype(BF16), cimn_ref[k])
            y1_ref[:, ks] = y0 + d_ref[:, ks] * u[:, ks]
        y2 = _gelu(y1_ref[...])
        q = _dot(y2.astype(BF16), wg_ref[...]) + bg_ref[...]
        q_ref[...] = q
        za = za_ref[...]
        ya_ref[...] = ((y2 * _sigmoid(q)) * (za * _sigmoid(za))).astype(BF16)

    return _pcall(
        body, f"s5_fwd_l{l}", (NT,),
        [_rows(W, 0), _rows(W, 1), _full((4, 128, 1024)), _full((4, 512, 128)), _full((4, 512, 128)),
         _full((8, GP)), _full((8, GP)), _full((1, W)), _layer(l, (W, W)), _full((1, W))],
        [_rows(GP), _rows(GP), _rows(W), _rows(W), _rows(W)],
        [SDS((L, GP), F32), SDS((L, GP), F32), SDS((L, W), F32), SDS((L, W), F32), SDS((L, W), BF16)],
        (proj, proj, bexp, cre, cimn, ap_re, ap_im, dsk, wglu, bglu),
        scratch=[pltpu.VMEM((8, GP), F32)], jobs=[job])


def _pool_fwd(l, proj, pw, scale):
    def body(ub_ref, zb_ref, pw_ref, sc_ref, pooled_ref, mixed_ref, yb_ref, buf):
        i = pl.program_id(0)

        @pl.when(i == 0)
        def _():
            buf[0:16, :] = jnp.zeros((16, W), F32)

        u = ub_ref[...]
        buf[16:16 + TM, :] = u
        t = i * TM + lax.broadcasted_iota(jnp.int32, (TM, 128), 0)
        for gi, win in enumerate(WINS):
            cs = slice(128 * gi, 128 * (gi + 1))
            acc = u[:, cs]
            for k in range(1, win):
                acc = acc + buf[16 - k:16 - k + TM, cs]
            cnt = jnp.minimum(t + 1, win).astype(F32)
            pb = (acc / cnt - u[:, cs]).astype(BF16)
            pooled_ref[:, cs] = pb
            mixed_ref[:, cs] = _dot(pb, pw_ref[gi])
        zb = zb_ref[...]
        yb_ref[...] = ((mixed_ref[...] * sc_ref[...]) * (zb * _sigmoid(zb))).astype(BF16)
        buf[0:16, :] = buf[TM:TM + 16, :]

    return pl.pallas_call(
        body, name=f"pool_fwd_l{l}", grid=(NT,),
        in_specs=[_rows(W, 2), _rows(W, 3), _full((4, 128, 128)), _full((1, W))],
        out_specs=[_rows(W), _rows(W), _rows(W)],
        out_shape=[SDS((L, W), BF16), SDS((L, W), F32), SDS((L, W), BF16)],
        scratch_shapes=[pltpu.VMEM((TM + 16, W), F32)],
        compiler_params=_params("arbitrary"),
    )(proj, proj, pw, scale)


def _merge_out(l, ya, yb, proj, x, wa, wb, wo):
    def body(ya_ref, yb_ref, ga_ref, gb_ref, x_ref, wa_ref, wb_ref, wo_ref, pa_ref, pb_ref, mg_ref, xo_ref):
        ya = ya_ref[...]
        yb = yb_ref[...]
        for j in range(4):
            cs = slice(256 * j, 256 * (j + 1))
            pa_ref[:, cs] = _dot(ya, wa_ref[j])
            pb_ref[:, cs] = _dot(yb, wb_ref[j])
        merged = _sigmoid(ga_ref[...]) * pa_ref[...] + _sigmoid(gb_ref[...]) * pb_ref[...]
        mb = merged.astype(BF16)
        mg_ref[...] = mb
        xo_ref[...] = x_ref[...] + _dot(mb, wo_ref[...])

    return pl.pallas_call(
        body, name=f"merge_out_l{l}", grid=(NT,),
        in_specs=[_rows(W), _rows(W), _rows(D, 2), _rows(D, 3), _rows(D),
                  _layer(l, (4, W, 256)), _layer(l, (4, W, 256)), _layer(l, (D, D))],
        out_specs=[_rows(D), _rows(D), _rows(D), _rows(D)],
        out_shape=[SDS((L, D), F32), SDS((L, D), F32), SDS((L, D), BF16), SDS((L, D), F32)],
        compiler_params=_params("arbitrary"),
    )(ya, yb, proj, proj, x, wa, wb, wo)


def _loss_head(x, gf, target):
    def body(x_ref, g_ref, t_ref, loss_ref, dx_ref, dg_ref):
        @pl.when(pl.program_id(0) == 0)
        def _():
            loss_ref[...] = jnp.zeros_like(loss_ref)
            dg_ref[...] = jnp.zeros_like(dg_ref)

        xv = x_ref[...]
        g = g_ref[...]
        r = lax.rsqrt(jnp.mean(xv * xv, axis=-1, keepdims=True) + EPS)
        xn = xv * r
        err = xn * g - t_ref[...]
        part = jnp.sum(jnp.mean(err * err, axis=-1, keepdims=True), axis=0, keepdims=True)
        loss_ref[...] += 0.5 * part
        dy = err * (1.0 / D)
        dg_ref[...] += jnp.sum(dy * xn, axis=0, keepdims=True)
        dxn = dy * g
        dx_ref[...] = r * (dxn - xn * jnp.mean(dxn * xn, axis=-1, keepdims=True))

    return pl.pallas_call(
        body, name="loss_head", grid=(NT,),
        in_specs=[_rows(D), _full((1, D)), _rows(D)],
        out_specs=[_full((2, 128)), _rows(D), _full((1, D))],
        out_shape=[SDS((2, 128), F32), SDS((L, D), F32), SDS((1, D), F32)],
        compiler_params=_params("arbitrary"),
    )(x, gf, target)


def _merge_out_bwd(l, dxn, mg, proj, pa, pb, ya, yb, wo, wa, wb):
    def body(dx_ref, mg_ref, ga_ref, gb_ref, pa_ref, pb_ref, ya_ref, yb_ref, wo_ref, wa_ref, wb_ref,
             dg_ref, dya_ref, dyb_ref, dwo_ref, dwa_ref, dwb_ref):
        @pl.when(pl.program_id(0) == 0)
        def _():
            dwo_ref[...] = jnp.zeros_like(dwo_ref)
            dwa_ref[...] = jnp.zeros_like(dwa_ref)
            dwb_ref[...] = jnp.zeros_like(dwb_ref)

        dxb = dx_ref[...].astype(BF16)
        dm = _dot_nt(dxb, wo_ref[...])
        sa = _sigmoid(ga_ref[...])
        sb = _sigmoid(gb_ref[...])
        dg_ref[:, :D] = dm * pa_ref[...] * (sa * (1.0 - sa))
        dg_ref[:, D:] = dm * pb_ref[...] * (sb * (1.0 - sb))
        dpa = (dm * sa).astype(BF16)
        dpb = (dm * sb).astype(BF16)
        ya = ya_ref[...]
        yb = yb_ref[...]
        dya = jnp.zeros((TM, W), F32)
        dyb = jnp.zeros((TM, W), F32)
        for j in range(4):
            cs = slice(256 * j, 256 * (j + 1))
            dya = dya + _dot_nt(dpa[:, cs], wa_ref[j])
            dyb = dyb + _dot_nt(dpb[:, cs], wb_ref[j])
            dwa_ref[j] += _dot_tn(ya, dpa[:, cs])
            dwb_ref[j] += _dot_tn(yb, dpb[:, cs])
        dya_ref[...] = dya
        dyb_ref[...] = dyb
        dwo_ref[...] += _dot_tn(mg_ref[...], dxb)

    return pl.pallas_call(
        body, name=f"merge_out_bwd_l{l}", grid=(NT,),
        in_specs=[_rows(D), _rows(D), _rows(D, 2), _rows(D, 3), _rows(D), _rows(D), _rows(W), _rows(W),
                  _layer(l, (D, D)), _layer(l, (4, W, 256)), _layer(l, (4, W, 256))],
        out_specs=[_rows(2 * D, 1), _rows(W), _rows(W), _full((D, D)), _full((4, W, 256)), _full((4, W, 256))],
        out_shape=[SDS((L, NIN), F32), SDS((L, W), F32), SDS((L, W), F32),
                   SDS((D, D), F32), SDS((4, W, 256), F32), SDS((4, W, 256), F32)],
        compiler_params=_params("arbitrary"),
    )(dxn, mg, proj, proj, pa, pb, ya, yb, wo, wa, wb)


def _pool_bwd(l, dyb, proj, mixed, pooled, pw, scale, dproj):
    def body(dyb_ref, zb_ref, mixed_ref, pooled_ref, pw_ref, sc_ref, _, db_ref, dpw_ref, dsc_ref, buf):
        i = pl.program_id(0)
        tile = NT - 1 - i

        @pl.when(i == 0)
        def _():
            dpw_ref[...] = jnp.zeros_like(dpw_ref)
            dsc_ref[...] = jnp.zeros_like(dsc_ref)
            buf[TM:TM + 16, :] = jnp.zeros((16, W), F32)

        dyb = dyb_ref[...]
        zb = zb_ref[...]
        mixed = mixed_ref[...]
        sc = sc_ref[...]
        sg = _sigmoid(zb)
        dyb0 = dyb * (zb * sg)
        db_ref[:, W:] = dyb * (mixed * sc) * (sg * (1.0 + zb * (1.0 - sg)))
        dsc_ref[...] += jnp.sum(dyb0 * mixed, axis=0, keepdims=True)
        dmix = (dyb0 * sc).astype(BF16)
        t = tile * TM + lax.broadcasted_iota(jnp.int32, (TM, 128), 0)
        for gi, win in enumerate(WINS):
            cs = slice(128 * gi, 128 * (gi + 1))
            dpw_ref[gi] += _dot_tn(pooled_ref[:, cs], dmix[:, cs])
            dpool = _dot_nt(dmix[:, cs], pw_ref[gi])
            cnt = jnp.minimum(t + 1, win).astype(F32)
            e = dpool / cnt
            buf[0:TM, cs] = e
            acc = e - dpool
            for k in range(1, win):
                acc = acc + buf[k:k + TM, cs]
            db_ref[:, cs] = acc
        buf[TM:TM + 16, :] = buf[0:16, :]

    return pl.pallas_call(
        body, name=f"pool_bwd_l{l}", grid=(NT,),
        in_specs=[_rows(W, 0, True), _rows(W, 3, True), _rows(W, 0, True), _rows(W, 0, True),
                  _full((4, 128, 128)), _full((1, W)), _ANY],
        out_specs=[_rows(2 * W, 1, True), _full((4, 128, 128)), _full((1, W))],
        out_shape=[SDS((L, NIN), F32), SDS((4, 128, 128), F32), SDS((1, W), F32)],
        input_output_aliases={6: 0},
        scratch_shapes=[pltpu.VMEM((TM + 16, W), F32)],
        compiler_params=_params("arbitrary"),
    )(dyb, proj, mixed, pooled, pw, scale, dproj)


def _s5_bwd(l, dya, proj, y1, q, sre, sim, cret, cimnt, bret, bimt, st_re, st_im, cr_re, cr_im, dsk, wglu, dproj, jobs=()):
    def halo(i):
        return (jnp.maximum((NT - 1 - i) * (TM // 8) - 1, 0), 0)

    def body(dya_ref, ua_ref, za_ref, y1_ref, q_ref, sre_ref, sim_ref, hre_ref, him_ref,
             cret_ref, cimnt_ref, bret_ref, bimt_ref, st_re_ref, st_im_ref, cr_re_ref, cr_im_ref, d_ref, wg_ref, _,
             da_ref, dwg_ref, dbg_ref, dd_ref, dcre_ref, dcimn_ref, dbre_ref, dbim_ref, dare_ref, daim_ref,
             lre, lim, car_ref):
        i = pl.program_id(0)
        tile = NT - 1 - i

        @pl.when(i == 0)
        def _():
            for ref in (dwg_ref, dbg_ref, dd_ref, dcre_ref, dcimn_ref, dbre_ref, dbim_ref, dare_ref, daim_ref, car_ref):
                ref[...] = jnp.zeros_like(ref)

        u = ua_ref[...]
        za = za_ref[...]
        y1 = y1_ref[...]
        dya = dya_ref[...]
        y2 = _gelu(y1)
        sg = _sigmoid(q_ref[...])
        sgz = _sigmoid(za)
        dy3 = dya * (za * sgz)
        da_ref[:, W:] = dya * (y2 * sg) * (sgz * (1.0 + za * (1.0 - sgz)))
        dq = dy3 * y2 * (sg * (1.0 - sg))
        dqb = dq.astype(BF16)
        dy2 = dy3 * sg + _dot_nt(dqb, wg_ref[...])
        dwg_ref[...] += _dot_tn(y2.astype(BF16), dqb)
        dbg_ref[...] += jnp.sum(dq, axis=0, keepdims=True)
        dy1 = dy2 * _gelu_grad(y1)
        dd_ref[...] += jnp.sum(dy1 * u, axis=0, keepdims=True)
        dy1b = dy1.astype(BF16)
        ub = u.astype(BF16)
        for k in range(4):
            blk = slice(512 * k, 512 * (k + 1))
            ks = slice(128 * k, 128 * (k + 1))
            lre[:, blk] = _dot(dy1b[:, ks], cret_ref[k])
            lim[:, blk] = _dot(dy1b[:, ks], cimnt_ref[k])
            dcre_ref[k] += _dot_tn(sre_ref[:, blk].astype(BF16), dy1b[:, ks])
            dcimn_ref[k] += _dot_tn(sim_ref[:, blk].astype(BF16), dy1b[:, ks])
        carry = _scan_tile(lre, lim, st_re_ref, st_im_ref, cr_re_ref, cr_im_ref, _load_carry(car_ref), True)
        _store_carry(car_ref, carry)

        rowid = lax.broadcasted_iota(jnp.int32, (8, 512), 0)
        gate = (tile > 0).astype(F32)

        def chunk(c, _):
            rows = pl.ds(pl.multiple_of(c * 8, 8), 8)
            prows = pl.ds(pl.multiple_of(jnp.maximum(c - 1, 0) * 8, 8), 8)
            for lb in range(GP // 512):
                cols = slice(lb * 512, (lb + 1) * 512)
                sr = sre_ref[rows, cols]
                si = sim_ref[rows, cols]
                pr = jnp.where(c == 0, hre_ref[7:8, cols] * gate, sre_ref[prows, cols][7:8])
                pi = jnp.where(c == 0, him_ref[7:8, cols] * gate, sim_ref[prows, cols][7:8])
                sr = jnp.where(rowid == 0, pr, pltpu.roll(sr, 1, 0))
                si = jnp.where(rowid == 0, pi, pltpu.roll(si, 1, 0))
                lr = lre[rows, cols]
                li = lim[rows, cols]
                dare_ref[:, cols] += sr * lr + si * li
                daim_ref[:, cols] += sr * li - si * lr
            return 0

        lax.fori_loop(0, TM // 8, chunk, 0)

        for k in range(4):
            blk = slice(512 * k, 512 * (k + 1))
            ks = slice(128 * k, 128 * (k + 1))
            lrb = lre[:, blk].astype(BF16)
            lib = lim[:, blk].astype(BF16)
            da_ref[:, ks] = dy1[:, ks] * d_ref[:, ks] + _dot(lrb, bret_ref[k]) + _dot(lib, bimt_ref[k])
            dbre_ref[k] += _dot_tn(ub[:, ks], lrb)
            dbim_ref[k] += _dot_tn(ub[:, ks], lib)

    return _pcall(
        body, f"s5_bwd_l{l}", (NT,),
        [_rows(W, 0, True), _rows(W, 0, True), _rows(W, 1, True), _rows(W, 0, True), _rows(W, 0, True),
         _rows(GP, 0, True), _rows(GP, 0, True),
         pl.BlockSpec((8, GP), halo), pl.BlockSpec((8, GP), halo),
         _full((4, 128, 512)), _full((4, 128, 512)), _full((4, 512, 128)), _full((4, 512, 128)),
         _full((8, GP)), _full((8, GP)), _full((8, GP)), _full((8, GP)), _full((1, W)), _layer(l, (W, W)), _ANY],
        [_rows(2 * W, 0, True), _full((W, W)), _full((1, W)), _full((1, W)),
         _full((4, 512, 128)), _full((4, 512, 128)), _full((4, 128, 512)), _full((4, 128, 512)),
         _full((8, GP)), _full((8, GP))],
        [SDS((L, NIN), F32), SDS((W, W), F32), SDS((1, W), F32), SDS((1, W), F32),
         SDS((4, 512, 128), F32), SDS((4, 512, 128), F32), SDS((4, 128, 512), F32), SDS((4, 128, 512), F32),
         SDS((8, GP), F32), SDS((8, GP), F32)],
        (dya, proj, proj, y1, q, sre, sim, sre, sim, cret, cimnt, bret, bimt, st_re, st_im, cr_re, cr_im, dsk, wglu,
         dproj),
        scratch=[pltpu.VMEM((TM, GP), F32), pltpu.VMEM((TM, GP), F32), pltpu.VMEM((8, GP), F32)],
        aliases={19: 0}, jobs=jobs)


def _inproj_dw(l, h, dproj, jobs=()):
    def body(h_ref, dp_ref, dw_ref, db_ref):
        @pl.when(pl.program_id(1) == 0)
        def _():
            dw_ref[...] = jnp.zeros_like(dw_ref)
            db_ref[...] = jnp.zeros_like(db_ref)

        dp = dp_ref[...]
        dw_ref[...] += _dot_tn(h_ref[...], dp.astype(BF16))
        db_ref[...] += jnp.sum(dp, axis=0, keepdims=True)

    return _pcall(
        body, f"inproj_dw_l{l}", (4, NT),
        [pl.BlockSpec((TM, D), lambda j, i: (i, 0)), pl.BlockSpec((TM, 1024), lambda j, i: (i, j))],
        [pl.BlockSpec((None, D, 1024), lambda j, i: (j, 0, 0)), pl.BlockSpec((1, 1024), lambda j, i: (0, j))],
        [SDS((4, D, 1024), F32), SDS((1, NIN), F32)],
        (h, dproj), jobs=jobs)


def _inproj_dx(l, dproj, w, x, g, dxn, jobs=()):
    def body(dp_ref, w_ref, x_ref, g_ref, dxn_ref, dx_ref, dg_ref):
        @pl.when(pl.program_id(0) == 0)
        def _():
            dg_ref[...] = jnp.zeros_like(dg_ref)

        dh = jnp.zeros((TM, D), F32)
        for j in range(4):
            dh = dh + _dot_nt(dp_ref[:, j * 1024:(j + 1) * 1024].astype(BF16), w_ref[j])
        xv = x_ref[...]
        r = lax.rsqrt(jnp.mean(xv * xv, axis=-1, keepdims=True) + EPS)
        xn = xv * r
        dg_ref[...] += jnp.sum(dh * xn, axis=0, keepdims=True)
        dn = dh * g_ref[...]
        dx_ref[...] = dxn_ref[...] + r * (dn - xn * jnp.mean(dn * xn, axis=-1, keepdims=True))

    return _pcall(
        body, f"inproj_dx_l{l}", (NT,),
        [_rows(NIN), _layer(l, (4, D, 1024)), _rows(D), _full((1, D)), _rows(D)],
        [_rows(D), _full((1, D))],
        [SDS((L, D), F32), SDS((1, D), F32)],
        (dproj, w, x, g, dxn), jobs=jobs)


def _discretize(log_dt, lam_re, lam_im, b_re, b_im):
    dt = jnp.exp(log_dt)[:, None]
    mag = jnp.exp(lam_re * dt)
    ang = lam_im * dt
    abar_re = mag * jnp.cos(ang)
    abar_im = mag * jnp.sin(ang)
    num_re = abar_re - 1.0
    num_im = abar_im
    den = lam_re * lam_re + lam_im * lam_im
    coef_re = (num_re * lam_re + num_im * lam_im) / den
    coef_im = (num_im * lam_re - num_re * lam_im) / den
    bbar_re = coef_re[..., None] * b_re - coef_im[..., None] * b_im
    bbar_im = coef_re[..., None] * b_im + coef_im[..., None] * b_re
    return abar_re, abar_im, bbar_re, bbar_im


def _powers(abar_re, abar_im):
    ar, ai = abar_re.reshape(1, GP), abar_im.reshape(1, GP)
    rows_re, rows_im = [ar], [ai]
    for _ in range(7):
        pr, pi = rows_re[-1], rows_im[-1]
        rows_re.append(pr * ar - pi * ai)
        rows_im.append(pr * ai + pi * ar)
    neg_im = [-r for r in rows_im]
    return (jnp.concatenate(rows_re, axis=0), jnp.concatenate(rows_im, axis=0), jnp.concatenate(neg_im, axis=0),
            jnp.concatenate(rows_re[::-1], axis=0), jnp.concatenate(neg_im[::-1], axis=0))


_EYE8 = functools.partial(jnp.eye, 8, dtype=F32)


def _expand_in(b):
    return jnp.einsum("kgpc,gh->kgchp", b.reshape(4, 8, P, C), _EYE8()).reshape(4, 128, 512)


def _extract_in(e):
    return jnp.einsum("kgchp,gh->kgpc", e.reshape(4, 8, C, 8, P), _EYE8()).reshape(G, P, C)


def _expand_out(c):
    return jnp.einsum("kgcp,gh->kgphc", c.reshape(4, 8, C, P), _EYE8()).reshape(4, 512, 128)


def _extract_out(e):
    return jnp.einsum("kgphc,gh->kgcp", e.reshape(4, 8, P, 8, C), _EYE8()).reshape(G, C, P)


SMALL = ("norm_g", "b_in", "ssm_log_dt", "ssm_lam_re", "ssm_lam_im", "ssm_b_re", "ssm_b_im",
         "ssm_c_re", "ssm_c_im", "ssm_d", "ssm_b_glu", "pool_w", "pool_scale")
BIG = ("w_in", "ssm_w_glu", "w_branch_a", "w_branch_b", "w_out")


def _step(x, target, w, m, v, place):
    sp = {n: w[n] for n in SMALL}
    final_norm_g = w["final_norm_g"]
    wbuf = dict(zip(BIG, _cast_own(place, [w[n] for n in BIG])))
    saved = []
    for l in range(DEPTH):
        disc_in = (sp["ssm_log_dt"][l], sp["ssm_lam_re"][l], sp["ssm_lam_im"][l], sp["ssm_b_re"][l], sp["ssm_b_im"][l])
        (abar_re, abar_im, bbar_re, bbar_im), disc_vjp = jax.vjp(_discretize, *disc_in)
        pw_re, pw_im, pw_imn, pw_re_rev, pw_imn_rev = _powers(abar_re, abar_im)
        b_re_x, b_im_x = _expand_in(bbar_re), _expand_in(bbar_im)
        c_re_x, c_imn_x = _expand_out(sp["ssm_c_re"][l]), _expand_out(-sp["ssm_c_im"][l])
        g = sp["norm_g"][l].reshape(1, D)
        dsk = sp["ssm_d"][l].reshape(1, W)
        scale = sp["pool_scale"][l].reshape(1, W)
        pw = sp["pool_w"][l].astype(BF16)

        if l == 0:
            (wbuf["w_in"],) = _comm_only("gather_w_in_l0", _GatherJob([wbuf["w_in"]], 0))[0]
        (h, proj), res = _norm_inproj(l, x, g, wbuf["w_in"], sp["b_in"][l].reshape(1, NIN),
                                      _GatherJob([wbuf[n] for n in BIG[1:]], l))
        wbuf.update(zip(BIG[1:], res[0][0]))
        wg = dict(wbuf, ssm_w_glu=wbuf["ssm_w_glu"].reshape(DEPTH, W, W), w_out=wbuf["w_out"].reshape(DEPTH, D, D))
        (sre, sim, y1, q, ya), res = _s5_fwd(
            l, proj, jnp.concatenate([b_re_x, b_im_x], axis=2).astype(BF16), c_re_x.astype(BF16), c_imn_x.astype(BF16),
            pw_re, pw_im, dsk, wg["ssm_w_glu"], sp["ssm_b_glu"][l].reshape(1, W),
            _GatherJob([wbuf["w_in"]], l + 1) if l + 1 < DEPTH else None)
        if res:
            (wbuf["w_in"],) = res[0][0]
            wg["w_in"] = wbuf["w_in"]
        pooled, mixed, yb = _pool_fwd(l, proj, pw, scale)
        pa, pb, mg, x_next = _merge_out(l, ya, yb, proj, x, wg["w_branch_a"], wg["w_branch_b"], wg["w_out"])
        saved.append(dict(x=x, g=g, dsk=dsk, scale=scale, pw=pw, h=h, proj=proj, sre=sre, sim=sim, y1=y1, q=q, ya=ya,
                          pooled=pooled, mixed=mixed, yb=yb, pa=pa, pb=pb, mg=mg, disc_vjp=disc_vjp,
                          powers=(pw_re, pw_imn, pw_re_rev, pw_imn_rev),
                          b_re_x=b_re_x, b_im_x=b_im_x, c_re_x=c_re_x, c_imn_x=c_imn_x))
        x = x_next

    loss, dx, dgf = _loss_head(x, final_norm_g.reshape(1, D), target)

    gs = {n: [None] * DEPTH for n in SMALL}
    big = {n: None for n in BIG}

    def adamw_layer(l, st, scatter=None):
        res = None
        for n, mine, other in zip(BIG, st["shard"], st["other"]):
            big[n], r = _adamw_big(f"adamw_{n}_l{l}", l, w[n], m[n], v[n], mine, other, big[n],
                                   [scatter] if n == "w_in" else [])
            res = r if n == "w_in" else res
        return res

    def shard_sums(l, st):
        return [_shard_sum(f"shard_sum_{n}_l{l}", place, o, r, rb)
                for n, o, r, rb in zip(BIG, st["own"], st["recv"], st["rbuf"])]

    prev = None
    for l in reversed(range(DEPTH)):
        s = saved[l]
        dproj, dya, dyb, dwo, dwa, dwb = _merge_out_bwd(
            l, dx, s["mg"], s["proj"], s["pa"], s["pb"], s["ya"], s["yb"],
            wg["w_out"], wg["w_branch_a"], wg["w_branch_b"])
        dproj, dpw, dsc = _pool_bwd(l, dyb, s["proj"], s["mixed"], s["pooled"], s["pw"], s["scale"], dproj)
        t = lambda a: jnp.swapaxes(a, 1, 2).astype(BF16)
        (dproj, dwg, dbg, dd, dcre, dcimn, dbre, dbim, dare, daim), res = _s5_bwd(
            l, dya, s["proj"], s["y1"], s["q"], s["sre"], s["sim"],
            t(s["c_re_x"]), t(s["c_imn_x"]), t(s["b_re_x"]), t(s["b_im_x"]),
            *s["powers"], s["dsk"], wg["ssm_w_glu"], dproj, [_ScatterJob(prev["parts"])] if prev else [])
        if prev:
            prev["rbuf"] = res[0][1]
            prev["shard"] = shard_sums(l + 1, prev)
        (dwin, dbin), res = _inproj_dw(l, s["h"], dproj, [_SiblingJob(prev["shard"], False)] if prev else [])
        if prev:
            prev["other"] = res[0][1]
        own = [dwin, dwg.reshape(4, W // 4, W), dwa, dwb, dwo.reshape(4, D // 4, D)]
        (dx, dg), res = _inproj_dx(l, dproj, wg["w_in"], s["x"], s["g"], dx, [_SiblingJob(own, True)])
        recv = res[0][1]
        parts = [_pair_sum_bf16(f"pair_sum_{n}_l{l}", place, o, r) for n, o, r in zip(BIG, own, recv)]
        cur = dict(own=own, recv=recv, parts=parts)
        if prev:
            res = adamw_layer(l + 1, prev, _ScatterJob(parts) if l == 0 else None)
            if l == 0:
                cur["rbuf"] = res[0][1]
        prev = cur

        d_abar_re = jnp.sum(dare, axis=0).reshape(G, P)
        d_abar_im = jnp.sum(daim, axis=0).reshape(G, P)
        dlog_dt, dlam_re, dlam_im, db_re, db_im = s["disc_vjp"](
            (d_abar_re, d_abar_im, _extract_in(dbre), _extract_in(dbim)))
        gs["norm_g"][l] = dg.reshape(D)
        gs["b_in"][l] = dbin.reshape(NIN)
        gs["ssm_log_dt"][l] = dlog_dt
        gs["ssm_lam_re"][l] = dlam_re
        gs["ssm_lam_im"][l] = dlam_im
        gs["ssm_b_re"][l] = db_re
        gs["ssm_b_im"][l] = db_im
        gs["ssm_c_re"][l] = _extract_out(dcre)
        gs["ssm_c_im"][l] = -_extract_out(dcimn)
        gs["ssm_d"][l] = dd.reshape(W)
        gs["ssm_b_glu"][l] = dbg.reshape(W)
        gs["pool_w"][l] = dpw
        gs["pool_scale"][l] = dsc.reshape(W)
    prev["shard"] = shard_sums(0, prev)
    prev["other"] = _comm_only("shards_to_sibling_l0", _SiblingJob(prev["shard"], False))[1]
    adamw_layer(0, prev)
    gs = {n: jnp.stack(v) for n, v in gs.items()}
    return loss, dx, gs, dgf, big


def _place():
    x, y, c = lax.axis_index("x"), lax.axis_index("y"), lax.axis_index("c")
    chips = [(1 - x, y), (x, 1 - y), (1 - x, 1 - y)]
    return x, y, c, 2 * x + y, chips, [2 * cx + cy for cx, cy in chips]


def _remote(src, dst, ssem, rsem, dev):
    return pltpu.make_async_remote_copy(src_ref=src, dst_ref=dst, send_sem=ssem, recv_sem=rsem,
                                        device_id=dev, device_id_type=MESH)


class _GatherJob:
    def __init__(self, bufs, l):
        self.srcs, self.bufs, self.news, self.l = [], list(bufs), [], l
        self.scratch = [pltpu.SemaphoreType.DMA((len(self.bufs), 3))] * 4

    def _half(self, ref, k, h):
        rows = ref.shape[2] // 2
        return ref.at[self.l, k, pl.ds(pl.multiple_of(h * rows, 8), rows), :]

    def _ici(self, bufs, sems, a, j, k):
        _, _, c, _, chips, _ = _place()
        blk = self._half(bufs[a], k, c)
        return _remote(blk, blk, sems[0].at[a, j], sems[1].at[a, j], (*chips[j], c))

    def _d2d(self, bufs, sems, a, j, k, h):
        x, y, c, _, _, _ = _place()
        blk = self._half(bufs[a], k, h)
        return _remote(blk, blk, sems[2].at[a, j], sems[3].at[a, j], (x, y, 1 - c))

    def start(self, srcs, bufs, news, sems):
        me = _place()[3]
        for a in range(len(self.bufs)):
            for j in range(3):
                self._ici(bufs, sems, a, j, me).start()

    def finish(self, srcs, bufs, news, sems):
        _, _, c, me, _, cid = _place()
        pairs = [(a, j) for a in range(len(self.bufs)) for j in range(3)]
        for a, j in pairs:
            self._ici(bufs, sems, a, j, cid[j]).wait_recv()
            self._d2d(bufs, sems, a, j, cid[j], c).start()
        for a, j in pairs:
            self._d2d(bufs, sems, a, j, cid[j], 1 - c).wait_recv()
        for a, j in pairs:
            self._ici(bufs, sems, a, j, me).wait_send()
            self._d2d(bufs, sems, a, j, cid[j], c).wait_send()


class _SiblingJob:
    def __init__(self, srcs, rows_half):
        self.srcs, self.bufs, self.rows_half = list(srcs), [], rows_half
        self.news = [SDS((s.shape[0], s.shape[1] // 2, s.shape[2]) if rows_half else s.shape, s.dtype) for s in srcs]
        self.scratch = [pltpu.SemaphoreType.DMA((len(self.srcs),))] * 2

    def _copy(self, srcs, news, sems, a):
        x, y, c, _, _, _ = _place()
        src = srcs[a]
        if self.rows_half:
            rows = src.shape[1] // 2
            src = src.at[:, pl.ds(pl.multiple_of((1 - c) * rows, 8), rows), :]
        return _remote(src, news[a], sems[0].at[a], sems[1].at[a], (x, y, 1 - c))

    def start(self, srcs, bufs, news, sems):
        for a in range(len(self.srcs)):
            self._copy(srcs, news, sems, a).start()

    def finish(self, srcs, bufs, news, sems):
        for a in range(len(self.srcs)):
            self._copy(srcs, news, sems, a).wait()


class _ScatterJob:
    def __init__(self, parts):
        self.srcs, self.bufs = list(parts), []
        self.news = [SDS((3,) + p.shape[1:], p.dtype) for p in parts]
        self.scratch = [pltpu.SemaphoreType.DMA((len(self.srcs), 3))] * 2

    def _copy(self, srcs, news, sems, a, j):
        _, _, c, _, chips, cid = _place()
        return _remote(srcs[a].at[cid[j]], news[a].at[j], sems[0].at[a, j], sems[1].at[a, j], (*chips[j], c))

    def start(self, srcs, bufs, news, sems):
        for a in range(len(self.srcs)):
            for j in range(3):
                self._copy(srcs, news, sems, a, j).start()

    def finish(self, srcs, bufs, news, sems):
        for a in range(len(self.srcs)):
            for j in range(3):
                self._copy(srcs, news, sems, a, j).wait()


def _comm_only(name, job):
    return _pcall(None, name, (), [], [], [], [], jobs=[job])[1][0]


def _chip_allgather(name, bufs, chip_major):
    n = len(bufs)

    def body(*refs):
        bufs_ = refs[n:2 * n]
        ssem, rsem, fsem, grsem = refs[2 * n:]
        x, y, c, me, chips, cid = _place()
        sib = (x, y, 1 - c)

        def blk(a, h, k):
            return bufs_[a].at[k, h] if chip_major else bufs_[a].at[h, k]

        sends = []
        for a in range(n):
            for j, (cx, cy) in enumerate(chips):
                cp = _remote(blk(a, c, me), blk(a, c, me), ssem.at[a, j], rsem.at[a, j], (cx, cy, c))
                cp.start()
                sends.append(cp)
        for a in range(n):
            for j, (cx, cy) in enumerate(chips):
                got = blk(a, c, cid[j])
                _remote(got, got, ssem.at[a, j], rsem.at[a, j], (cx, cy, c)).wait_recv()
                cp = _remote(got, got, fsem.at[a, j], grsem.at[a, j], sib)
                cp.start()
                sends.append(cp)
        for a in range(n):
            for j in range(3):
                got = blk(a, 1 - c, cid[j])
                _remote(got, got, fsem.at[a, j], grsem.at[a, j], sib).wait_recv()
        for cp in sends:
            cp.wait_send()

    return pl.pallas_call(
        body, name=name,
        in_specs=[_ANY] * n, out_specs=[_ANY] * n,
        out_shape=[SDS(a.shape, a.dtype) for a in bufs],
        input_output_aliases={a: a for a in range(n)},
        scratch_shapes=[pltpu.SemaphoreType.DMA((n, 3))] * 4,
    )(*bufs)


def _cast_own(place, ws):
    n = len(ws)

    def body(p_ref, *refs):
        for i_ref, o_ref in zip(refs[:n], refs[n:]):
            o_ref[...] = i_ref[...].astype(BF16)

    return pl.pallas_call(
        body, name="cast_own_shards",
        grid_spec=pltpu.PrefetchScalarGridSpec(
            num_scalar_prefetch=1, grid=(DEPTH,),
            in_specs=[pl.BlockSpec((None,) + a.shape[1:], lambda l, p: (l, 0, 0)) for a in ws],
            out_specs=[pl.BlockSpec((None, None) + a.shape[1:], lambda l, p: (l, p[1], 0, 0)) for a in ws]),
        out_shape=[SDS((DEPTH, 4) + a.shape[1:], BF16) for a in ws],
        compiler_params=_params("arbitrary"),
    )(place, *ws)


def _half_tiles(a_):
    rows = a_ // 2
    ta = min(rows, 256)
    return rows, ta, rows // ta


def _pair_sum_bf16(name, place, own, recv):
    _, a_, b_ = own.shape
    rows, ta, nh = _half_tiles(a_)

    def body(p_ref, own_ref, recv_ref, out_ref):
        out_ref[...] = (own_ref[...] + recv_ref[...]).astype(BF16)

    return pl.pallas_call(
        body, name=name,
        grid_spec=pltpu.PrefetchScalarGridSpec(
            num_scalar_prefetch=1, grid=(4, nh),
            in_specs=[pl.BlockSpec((None, ta, b_), lambda s, i, p: (s, p[0] * nh + i, 0)),
                      pl.BlockSpec((None, ta, b_), lambda s, i, p: (s, i, 0))],
            out_specs=pl.BlockSpec((None, ta, b_), lambda s, i, p: (s, i, 0))),
        out_shape=SDS((4, rows, b_), BF16),
        compiler_params=_params("arbitrary", "arbitrary"),
    )(place, own, recv)


def _shard_sum(name, place, own, recv, rbuf):
    _, a_, b_ = own.shape
    rows, ta, nh = _half_tiles(a_)

    def body(p_ref, own_ref, recv_ref, r_ref, out_ref):
        acc = own_ref[...] + recv_ref[...]
        for j in range(3):
            acc = acc + r_ref[j].astype(F32)
        out_ref[...] = acc

    return pl.pallas_call(
        body, name=name,
        grid_spec=pltpu.PrefetchScalarGridSpec(
            num_scalar_prefetch=1, grid=(nh,),
            in_specs=[pl.BlockSpec((None, ta, b_), lambda i, p: (p[1], p[0] * nh + i, 0)),
                      pl.BlockSpec((None, ta, b_), lambda i, p: (p[1], i, 0)),
                      pl.BlockSpec((3, ta, b_), lambda i, p: (0, i, 0))],
            out_specs=pl.BlockSpec((ta, b_), lambda i, p: (i, 0))),
        out_shape=SDS((rows, b_), F32),
        compiler_params=_params("arbitrary"),
    )(place, own, recv, rbuf)


def _small_pair_sum(place, mine, recv, dtypes):
    n = len(mine)

    def body(p_ref, *refs):
        for m_ref, r_ref, o_ref in zip(refs[:n], refs[n:2 * n], refs[2 * n:]):
            o_ref[...] = (m_ref[...] + r_ref[...]).astype(o_ref.dtype)

    def whole(a):
        zeros = (0,) * a.ndim
        return pl.BlockSpec(a.shape, lambda i, p: zeros)

    def mine_blk(a):
        zeros = (0,) * a.ndim
        return pl.BlockSpec((None,) + a.shape, lambda i, p: (p[1],) + zeros)

    return pl.pallas_call(
        body, name="small_pair_sum",
        grid_spec=pltpu.PrefetchScalarGridSpec(
            num_scalar_prefetch=1, grid=(1,),
            in_specs=[whole(a) for a in mine] + [whole(a) for a in recv],
            out_specs=[mine_blk(a) for a in mine]),
        out_shape=[SDS((4,) + a.shape, dt) for a, dt in zip(mine, dtypes)],
        compiler_params=_params("arbitrary"),
    )(place, *mine, *recv)


def _adam_math(w, g, m, v):
    m = B1 * m + (1.0 - B1) * g
    v = B2 * v + (1.0 - B2) * (g * g)
    m_hat = m / (1.0 - B1 ** STEP)
    v_hat = v / (1.0 - B2 ** STEP)
    delta = -LR * (m_hat / (jnp.sqrt(v_hat) + EPS_A) + WD * w)
    return delta, m, v


def _adamw_big(name, l, w, m, v, mine, other, prev, jobs=()):
    _, a_, b_ = w.shape
    _, ta, nh = _half_tiles(a_)
    prev = list(prev or [])

    def body(w_ref, m_ref, v_ref, mine_ref, other_ref, *rest):
        g_ref, d_ref, mo_ref, vo_ref = rest[len(prev):]
        g = jnp.where(pl.program_id(0) == lax.axis_index("c"), mine_ref[...], other_ref[...])
        g_ref[...] = g
        d_ref[...], mo_ref[...], vo_ref[...] = _adam_math(w_ref[...], g, m_ref[...], v_ref[...])

    slab = pl.BlockSpec((None, ta, b_), lambda h, i: (l, h * nh + i, 0))
    half = pl.BlockSpec((ta, b_), lambda h, i: (i, 0))
    outs, res = _pcall(
        body, name, (2, nh), [slab, slab, slab, half, half] + [_ANY] * len(prev), [slab] * 4, [SDS(w.shape, F32)] * 4,
        (w, m, v, mine, other, *prev), aliases={5 + k: k for k in range(len(prev))}, jobs=jobs)
    return outs, res


def _adamw_small(name, ws, parts, ms, vs, loss_parts=None):
    k = len(ws)
    extra = [] if loss_parts is None else [loss_parts]

    def chip_sum(p_ref):
        p = [p_ref[k].astype(F32) for k in range(4)]
        return ((p[0] + p[1]) + p[2]) + p[3]

    def body(*refs):
        w_refs, p_refs, m_refs, v_refs = refs[:k], refs[k:2 * k], refs[2 * k:3 * k], refs[3 * k:4 * k]
        outs = refs[4 * k + len(extra):]
        if extra:
            outs[0][...] = chip_sum(refs[4 * k])
            outs = outs[1:]
        for a in range(k):
            g = chip_sum(p_refs[a])
            outs[a][...] = g
            outs[k + a][...], outs[2 * k + a][...], outs[3 * k + a][...] = _adam_math(
                w_refs[a][...], g, m_refs[a][...], v_refs[a][...])

    like = [SDS(a.shape, F32) for a in ws]
    return pl.pallas_call(
        body, name=name,
        out_shape=([SDS(loss_parts.shape[1:], F32)] if extra else []) + like * 4,
        compiler_params=pltpu.CompilerParams(vmem_limit_bytes=VMEM_LIMIT),
    )(*ws, *parts, *ms, *vs, *extra)


WEIGHTS = ("norm_g", "w_in", "b_in", "ssm_log_dt", "ssm_lam_re", "ssm_lam_im", "ssm_b_re", "ssm_b_im", "ssm_c_re",
           "ssm_c_im", "ssm_d", "ssm_w_glu", "ssm_b_glu", "pool_w", "pool_scale", "w_branch_a", "w_branch_b", "w_out",
           "final_norm_g")
REPLICATED = SMALL + ("final_norm_g",)
DENSE = {"ssm_b_re": (DEPTH, G, P * C), "ssm_b_im": (DEPTH, G, P * C), "final_norm_g": (2, D // 2)}


def kernel(x, norm_g, w_in, b_in, ssm_log_dt, ssm_lam_re, ssm_lam_im, ssm_b_re, ssm_b_im, ssm_c_re, ssm_c_im, ssm_d, ssm_w_glu, ssm_b_glu, pool_w, pool_scale, w_branch_a, w_branch_b, w_out, final_norm_g, loss_target, m_norm_g, m_w_in, m_b_in, m_ssm_log_dt, m_ssm_lam_re, m_ssm_lam_im, m_ssm_b_re, m_ssm_b_im, m_ssm_c_re, m_ssm_c_im, m_ssm_d, m_ssm_w_glu, m_ssm_b_glu, m_pool_w, m_pool_scale, m_w_branch_a, m_w_branch_b, m_w_out, m_final_norm_g, v_norm_g, v_w_in, v_b_in, v_ssm_log_dt, v_ssm_lam_re, v_ssm_lam_im, v_ssm_b_re, v_ssm_b_im, v_ssm_c_re, v_ssm_c_im, v_ssm_d, v_ssm_w_glu, v_ssm_b_glu, v_pool_w, v_pool_scale, v_w_branch_a, v_w_branch_b, v_w_out, v_final_norm_g):
    w = dict(zip(WEIGHTS, (norm_g, w_in, b_in, ssm_log_dt, ssm_lam_re, ssm_lam_im, ssm_b_re, ssm_b_im, ssm_c_re,
                           ssm_c_im, ssm_d, ssm_w_glu, ssm_b_glu, pool_w, pool_scale, w_branch_a, w_branch_b, w_out,
                           final_norm_g)))
    m = dict(zip(WEIGHTS, (m_norm_g, m_w_in, m_b_in, m_ssm_log_dt, m_ssm_lam_re, m_ssm_lam_im, m_ssm_b_re, m_ssm_b_im,
                           m_ssm_c_re, m_ssm_c_im, m_ssm_d, m_ssm_w_glu, m_ssm_b_glu, m_pool_w, m_pool_scale,
                           m_w_branch_a, m_w_branch_b, m_w_out, m_final_norm_g)))
    v = dict(zip(WEIGHTS, (v_norm_g, v_w_in, v_b_in, v_ssm_log_dt, v_ssm_lam_re, v_ssm_lam_im, v_ssm_b_re, v_ssm_b_im,
                           v_ssm_c_re, v_ssm_c_im, v_ssm_d, v_ssm_w_glu, v_ssm_b_glu, v_pool_w, v_pool_scale,
                           v_w_branch_a, v_w_branch_b, v_w_out, v_final_norm_g)))
    place = jnp.stack([lax.axis_index("c"), 2 * lax.axis_index("x") + lax.axis_index("y")]).astype(jnp.int32)

    loss, dx, gs, dgf, big = _step(x[0], loss_target[0], w, m, v, place)
    grads, delta, new_m, new_v = ({n: big[n][k] for n in BIG} for k in range(4))

    natural = {n: w[n].shape for n in REPLICATED}
    gs["final_norm_g"] = dgf
    for n in REPLICATED:
        shape = DENSE.get(n, natural[n])
        shape = shape if len(shape) > 2 else (shape[0], 1, shape[1])
        gs[n], w[n], m[n], v[n] = (a.reshape(shape) for a in (gs[n], w[n], m[n], v[n]))
    small = [gs[n] for n in REPLICATED] + [loss.reshape(2, 1, 128)]
    recv = _comm_only("small_to_sibling", _SiblingJob(small, False))[1]
    pair_small = _small_pair_sum(place, small, recv, [BF16] * len(SMALL) + [F32, F32])
    small_parts = dict(zip(REPLICATED + ("loss",), _chip_allgather("gather_small", pair_small, True)))

    k = len(REPLICATED)
    outs = _adamw_small("adamw_small", [w[n] for n in REPLICATED], [small_parts[n] for n in REPLICATED],
                        [m[n] for n in REPLICATED], [v[n] for n in REPLICATED], small_parts["loss"])
    total_loss, outs = outs[0][0, 0, 0], outs[1:]
    for dst, vals in ((grads, outs[:k]), (delta, outs[k:2 * k]), (new_m, outs[2 * k:3 * k]), (new_v, outs[3 * k:])):
        dst.update({n: a.reshape(natural[n]) for n, a in zip(REPLICATED, vals)})

    return (total_loss, dx[None], *[grads[n] for n in WEIGHTS], *[delta[n] for n in WEIGHTS],
            *[new_m[n] for n in WEIGHTS], *[new_v[n] for n in WEIGHTS])
```

```python
import functools

import jax
import jax.numpy as jnp
from jax import lax
from jax.experimental import pallas as pl
from jax.experimental.pallas import tpu as pltpu

F32, BF16 = jnp.float32, jnp.bfloat16
SDS = jax.ShapeDtypeStruct
MESH = pl.DeviceIdType.MESH

DEPTH = 2
L = 2048
D = 1024
NIN = 4096
W = 512
G, P, C = 32, 64, 16
GP = G * P
WINS = (2, 4, 8, 16)
TM = 256
NT = L // TM
EPS = 1e-6
VMEM_LIMIT = 56 * 2**20

LR, B1, B2, EPS_A, WD, STEP = 0.001, 0.9, 0.999, 1e-08, 0.01, 10


def _params(*sem):
    return pltpu.CompilerParams(dimension_semantics=sem, vmem_limit_bytes=VMEM_LIMIT)


_ANY = pl.BlockSpec(memory_space=pl.ANY)


def _full(shape):
    zeros = (0,) * len(shape)
    return pl.BlockSpec(shape, lambda *_: zeros)


def _layer(l, shape):
    zeros = (0,) * len(shape)
    return pl.BlockSpec((None,) + shape, lambda *_: (l,) + zeros)


def _rows(width, col=0, reverse=False):
    if reverse:
        return pl.BlockSpec((TM, width), lambda i: (NT - 1 - i, col))
    return pl.BlockSpec((TM, width), lambda i: (i, col))


def _pcall(body, name, grid, in_specs, out_specs, out_shape, args, scratch=(), aliases=None, jobs=()):
    in_specs, out_specs, out_shape, args, scratch = list(in_specs), list(out_specs), list(out_shape), list(args), list(scratch)
    aliases = dict(aliases or {})
    jobs = [j for j in jobs if j is not None]
    n_in, n_out, n_scr = len(in_specs), len(out_specs), len(scratch)
    srcs = [s for j in jobs for s in j.srcs]
    bufs = [b for j in jobs for b in j.bufs]
    news = [s for j in jobs for s in j.news]
    aliases.update({n_in + len(srcs) + k: n_out + k for k in range(len(bufs))})

    def hosted(*refs):
        cuts = [n_in, len(srcs), len(bufs), n_out, len(bufs), len(news), n_scr]
        parts, p = [], 0
        for n in cuts:
            parts.append(refs[p:p + n])
            p += n
        ins, src_r, _, outs, buf_r, new_r, scr = parts
        sem_r = refs[p:]
        views, ps, pb, pn, pm = [], 0, 0, 0, 0
        for j in jobs:
            views.append((src_r[ps:ps + len(j.srcs)], buf_r[pb:pb + len(j.bufs)], new_r[pn:pn + len(j.news)],
                          sem_r[pm:pm + len(j.scratch)]))
            ps, pb, pn, pm = ps + len(j.srcs), pb + len(j.bufs), pn + len(j.news), pm + len(j.scratch)

        def run(phase):
            for j, v in zip(jobs, views):
                getattr(j, phase)(*v)

        def at_step(step):
            return functools.reduce(jnp.logical_and, [pl.program_id(d) == step(d) for d in range(len(grid))])

        if not grid:
            run("start")
            run("finish")
            return
        pl.when(at_step(lambda d: 0))(lambda: run("start"))
        body(*ins, *outs, *scr)
        pl.when(at_step(lambda d: grid[d] - 1))(lambda: run("finish"))

    outs = pl.pallas_call(
        hosted if jobs else body, name=name, **({"grid": grid} if grid else {}),
        in_specs=in_specs + [_ANY] * (len(srcs) + len(bufs)), out_specs=out_specs + [_ANY] * (len(bufs) + len(news)),
        out_shape=out_shape + [SDS(b.shape, b.dtype) for b in bufs] + news,
        input_output_aliases=aliases, scratch_shapes=scratch + [s for j in jobs for s in j.scratch],
        compiler_params=_params(*(("arbitrary",) * len(grid))))(*args, *srcs, *bufs)
    res, pb, pn = [], n_out, n_out + len(bufs)
    for j in jobs:
        res.append((list(outs[pb:pb + len(j.bufs)]), list(outs[pn:pn + len(j.news)])))
        pb, pn = pb + len(j.bufs), pn + len(j.news)
    return list(outs[:n_out]), res


def _dot(a, b):
    return jnp.dot(a, b, preferred_element_type=F32)


def _dot_nt(a, b):
    return lax.dot_general(a, b, (((1,), (1,)), ((), ())), preferred_element_type=F32)


def _dot_tn(a, b):
    return lax.dot_general(a, b, (((0,), (0,)), ((), ())), preferred_element_type=F32)


_K0 = 0.7978845608028654
_K1 = 0.044715


def _gelu(x):
    return 0.5 * x * (1.0 + jnp.tanh(_K0 * (x + _K1 * (x * x * x))))


def _gelu_grad(x):
    t = jnp.tanh(_K0 * (x + _K1 * (x * x * x)))
    return 0.5 * (1.0 + t) + 0.5 * x * (1.0 - t * t) * (_K0 * (1.0 + 3.0 * _K1 * x * x))


def _sigmoid(x):
    return jax.nn.sigmoid(x)


def _norm_inproj(l, x, g, w, b, job=None):
    def body(x_ref, g_ref, w_ref, b_ref, h_ref, proj_ref):
        xv = x_ref[...]
        r = lax.rsqrt(jnp.mean(xv * xv, axis=-1, keepdims=True) + EPS)
        hb = ((xv * r) * g_ref[...]).astype(BF16)
        h_ref[...] = hb
        for j in range(4):
            cs = slice(j * 1024, (j + 1) * 1024)
            proj_ref[:, cs] = _dot(hb, w_ref[j]) + b_ref[:, cs]

    return _pcall(
        body, f"norm_inproj_l{l}", (NT,),
        [_rows(D), _full((1, D)), _layer(l, (4, D, 1024)), _full((1, NIN))],
        [_rows(D), _rows(NIN)],
        [SDS((L, D), BF16), SDS((L, NIN), F32)],
        (x, g, w, b), jobs=[job])


def _scan_tile(re_ref, im_ref, st_re, st_im, cr_re, cr_im, carry, reverse):
    rowid = lax.broadcasted_iota(jnp.int32, (8, 512), 0)

    def chunk(ci, carry):
        c = (TM // 8 - 1 - ci) if reverse else ci
        rows = pl.ds(pl.multiple_of(c * 8, 8), 8)
        new = []
        for lb in range(GP // 512):
            cols = slice(lb * 512, (lb + 1) * 512)
            vr = re_ref[rows, cols]
            vi = im_ref[rows, cols]
            for d, prow in ((1, 0), (2, 1), (4, 3)):
                ar = st_re[prow:prow + 1, cols]
                ai = st_im[prow:prow + 1, cols]
                if reverse:
                    keep = rowid < 8 - d
                    sr = jnp.where(keep, pltpu.roll(vr, 8 - d, 0), 0.0)
                    si = jnp.where(keep, pltpu.roll(vi, 8 - d, 0), 0.0)
                else:
                    keep = rowid >= d
                    sr = jnp.where(keep, pltpu.roll(vr, d, 0), 0.0)
                    si = jnp.where(keep, pltpu.roll(vi, d, 0), 0.0)
                vr, vi = vr + ar * sr - ai * si, vi + ar * si + ai * sr
            cr, ci_ = carry[2 * lb], carry[2 * lb + 1]
            pr = cr_re[:, cols]
            pi = cr_im[:, cols]
            vr, vi = vr + pr * cr - pi * ci_, vi + pr * ci_ + pi * cr
            re_ref[rows, cols] = vr
            im_ref[rows, cols] = vi
            if reverse:
                new += [vr[0:1], vi[0:1]]
            else:
                new += [vr[7:8], vi[7:8]]
        return tuple(new)

    return lax.fori_loop(0, TM // 8, chunk, carry)


def _load_carry(car_ref):
    return tuple(car_ref[r:r + 1, lb * 512:(lb + 1) * 512] for lb in range(GP // 512) for r in (0, 1))


def _store_carry(car_ref, carry):
    for lb in range(GP // 512):
        car_ref[0:1, lb * 512:(lb + 1) * 512] = carry[2 * lb]
        car_ref[1:2, lb * 512:(lb + 1) * 512] = carry[2 * lb + 1]


def _s5_fwd(l, proj, bexp, cre, cimn, ap_re, ap_im, dsk, wglu, bglu, job=None):
    def body(ua_ref, za_ref, bexp_ref, cre_ref, cimn_ref, apr_ref, api_ref, d_ref, wg_ref, bg_ref,
             sre_ref, sim_ref, y1_ref, q_ref, ya_ref, car_ref):
        @pl.when(pl.program_id(0) == 0)
        def _():
            car_ref[...] = jnp.zeros_like(car_ref)

        u = ua_ref[...]
        ub = u.astype(BF16)
        for k in range(4):
            bu = _dot(ub[:, 128 * k:128 * (k + 1)], bexp_ref[k])
            sre_ref[:, 512 * k:512 * (k + 1)] = bu[:, :512]
            sim_ref[:, 512 * k:512 * (k + 1)] = bu[:, 512:]
        carry = _scan_tile(sre_ref, sim_ref, apr_ref, api_ref, apr_ref, api_ref, _load_carry(car_ref), False)
        _store_carry(car_ref, carry)
        for k in range(4):
            blk = slice(512 * k, 512 * (k + 1))
            ks = slice(128 * k, 128 * (k + 1))
            y0 = _dot(sre_ref[:, blk].astype(BF16), cre_ref[k]) + _dot(sim_ref[:, blk].astype(BF16), cimn_ref[k])
            y1_ref[:, ks] = y0 + d_ref[:, ks] * u[:, ks]
        y2 = _gelu(y1_ref[...])
        q = _dot(y2.astype(BF16), wg_ref[...]) + bg_ref[...]
        q_ref[...] = q
        za = za_ref[...]
        ya_ref[...] = ((y2 * _sigmoid(q)) * (za * _sigmoid(za))).astype(BF16)

    return _pcall(
        body, f"s5_fwd_l{l}", (NT,),
        [_rows(W, 0), _rows(W, 1), _full((4, 128, 1024)), _full((4, 512, 128)), _full((4, 512, 128)),
         _full((8, GP)), _full((8, GP)), _full((1, W)), _layer(l, (W, W)), _full((1, W))],
        [_rows(GP), _rows(GP), _rows(W), _rows(W), _rows(W)],
        [SDS((L, GP), F32), SDS((L, GP), F32), SDS((L, W), F32), SDS((L, W), F32), SDS((L, W), BF16)],
        (proj, proj, bexp, cre, cimn, ap_re, ap_im, dsk, wglu, bglu),
        scratch=[pltpu.VMEM((8, GP), F32)], jobs=[job])


def _pool_fwd(l, proj, pw, scale):
    def body(ub_ref, zb_ref, pw_ref, sc_ref, pooled_ref, mixed_ref, yb_ref, buf):
        i = pl.program_id(0)

        @pl.when(i == 0)
        def _():
            buf[0:16, :] = jnp.zeros((16, W), F32)

        u = ub_ref[...]
        buf[16:16 + TM, :] = u
        t = i * TM + lax.broadcasted_iota(jnp.int32, (TM, 128), 0)
        for gi, win in enumerate(WINS):
            cs = slice(128 * gi, 128 * (gi + 1))
            acc = u[:, cs]
            for k in range(1, win):
                acc = acc + buf[16 - k:16 - k + TM, cs]
            cnt = jnp.minimum(t + 1, win).astype(F32)
            pb = (acc / cnt - u[:, cs]).astype(BF16)
            pooled_ref[:, cs] = pb
            mixed_ref[:, cs] = _dot(pb, pw_ref[gi])
        zb = zb_ref[...]
        yb_ref[...] = ((mixed_ref[...] * sc_ref[...]) * (zb * _sigmoid(zb))).astype(BF16)
        buf[0:16, :] = buf[TM:TM + 16, :]

    return pl.pallas_call(
        body, name=f"pool_fwd_l{l}", grid=(NT,),
        in_specs=[_rows(W, 2), _rows(W, 3), _full((4, 128, 128)), _full((1, W))],
        out_specs=[_rows(W), _rows(W), _rows(W)],
        out_shape=[SDS((L, W), BF16), SDS((L, W), F32), SDS((L, W), BF16)],
        scratch_shapes=[pltpu.VMEM((TM + 16, W), F32)],
        compiler_params=_params("arbitrary"),
    )(proj, proj, pw, scale)


def _merge_out(l, ya, yb, proj, x, wa, wb, wo):
    def body(ya_ref, yb_ref, ga_ref, gb_ref, x_ref, wa_ref, wb_ref, wo_ref, pa_ref, pb_ref, mg_ref, xo_ref):
        ya = ya_ref[...]
        yb = yb_ref[...]
        for j in range(4):
            cs = slice(256 * j, 256 * (j + 1))
            pa_ref[:, cs] = _dot(ya, wa_ref[j])
            pb_ref[:, cs] = _dot(yb, wb_ref[j])
        merged = _sigmoid(ga_ref[...]) * pa_ref[...] + _sigmoid(gb_ref[...]) * pb_ref[...]
        mb = merged.astype(BF16)
        mg_ref[...] = mb
        xo_ref[...] = x_ref[...] + _dot(mb, wo_ref[...])

    return pl.pallas_call(
        body, name=f"merge_out_l{l}", grid=(NT,),
        in_specs=[_rows(W), _rows(W), _rows(D, 2), _rows(D, 3), _rows(D),
                  _layer(l, (4, W, 256)), _layer(l, (4, W, 256)), _layer(l, (D, D))],
        out_specs=[_rows(D), _rows(D), _rows(D), _rows(D)],
        out_shape=[SDS((L, D), F32), SDS((L, D), F32), SDS((L, D), BF16), SDS((L, D), F32)],
        compiler_params=_params("arbitrary"),
    )(ya, yb, proj, proj, x, wa, wb, wo)


def _loss_head(x, gf, target):
    def body(x_ref, g_ref, t_ref, loss_ref, dx_ref, dg_ref):
        @pl.when(pl.program_id(0) == 0)
        def _():
            loss_ref[...] = jnp.zeros_like(loss_ref)
            dg_ref[...] = jnp.zeros_like(dg_ref)

        xv = x_ref[...]
        g = g_ref[...]
        r = lax.rsqrt(jnp.mean(xv * xv, axis=-1, keepdims=True) + EPS)
        xn = xv * r
        err = xn * g - t_ref[...]
        part = jnp.sum(jnp.mean(err * err, axis=-1, keepdims=True), axis=0, keepdims=True)
        loss_ref[...] += 0.5 * part
        dy = err * (1.0 / D)
        dg_ref[...] += jnp.sum(dy * xn, axis=0, keepdims=True)
        dxn = dy * g
        dx_ref[...] = r * (dxn - xn * jnp.mean(dxn * xn, axis=-1, keepdims=True))

    return pl.pallas_call(
        body, name="loss_head", grid=(NT,),
        in_specs=[_rows(D), _full((1, D)), _rows(D)],
        out_specs=[_full((2, 128)), _rows(D), _full((1, D))],
        out_shape=[SDS((2, 128), F32), SDS((L, D), F32), SDS((1, D), F32)],
        compiler_params=_params("arbitrary"),
    )(x, gf, target)


def _merge_out_bwd(l, dxn, mg, proj, pa, pb, ya, yb, wo, wa, wb, jobs=()):
    def body(dx_ref, mg_ref, ga_ref, gb_ref, pa_ref, pb_ref, ya_ref, yb_ref, wo_ref, wa_ref, wb_ref,
             dg_ref, dya_ref, dyb_ref, dwo_ref, dwa_ref, dwb_ref):
        @pl.when(pl.program_id(0) == 0)
        def _():
            dwo_ref[...] = jnp.zeros_like(dwo_ref)
            dwa_ref[...] = jnp.zeros_like(dwa_ref)
            dwb_ref[...] = jnp.zeros_like(dwb_ref)

        dxb = dx_ref[...].astype(BF16)
        dm = _dot_nt(dxb, wo_ref[...])
        sa = _sigmoid(ga_ref[...])
        sb = _sigmoid(gb_ref[...])
        dg_ref[:, :D] = dm * pa_ref[...] * (sa * (1.0 - sa))
        dg_ref[:, D:] = dm * pb_ref[...] * (sb * (1.0 - sb))
        dpa = (dm * sa).astype(BF16)
        dpb = (dm * sb).astype(BF16)
        ya = ya_ref[...]
        yb = yb_ref[...]
        dya = jnp.zeros((TM, W), F32)
        dyb = jnp.zeros((TM, W), F32)
        for j in range(4):
            cs = slice(256 * j, 256 * (j + 1))
            dya = dya + _dot_nt(dpa[:, cs], wa_ref[j])
            dyb = dyb + _dot_nt(dpb[:, cs], wb_ref[j])
            dwa_ref[j] += _dot_tn(ya, dpa[:, cs])
            dwb_ref[j] += _dot_tn(yb, dpb[:, cs])
        dya_ref[...] = dya
        dyb_ref[...] = dyb
        dwo_ref[...] += _dot_tn(mg_ref[...], dxb)

    return _pcall(
        body, f"merge_out_bwd_l{l}", (NT,),
        [_rows(D), _rows(D), _rows(D, 2), _rows(D, 3), _rows(D), _rows(D), _rows(W), _rows(W),
         _layer(l, (D, D)), _layer(l, (4, W, 256)), _layer(l, (4, W, 256))],
        [_rows(2 * D, 1), _rows(W), _rows(W), _full((D, D)), _full((4, W, 256)), _full((4, W, 256))],
        [SDS((L, NIN), F32), SDS((L, W), F32), SDS((L, W), F32),
         SDS((D, D), F32), SDS((4, W, 256), F32), SDS((4, W, 256), F32)],
        (dxn, mg, proj, proj, pa, pb, ya, yb, wo, wa, wb), jobs=jobs)


def _pool_bwd(l, dyb, proj, mixed, pooled, pw, scale, dproj, jobs=()):
    def body(dyb_ref, zb_ref, mixed_ref, pooled_ref, pw_ref, sc_ref, _, db_ref, dpw_ref, dsc_ref, buf):
        i = pl.program_id(0)
        tile = NT - 1 - i

        @pl.when(i == 0)
        def _():
            dpw_ref[...] = jnp.zeros_like(dpw_ref)
            dsc_ref[...] = jnp.zeros_like(dsc_ref)
            buf[TM:TM + 16, :] = jnp.zeros((16, W), F32)

        dyb = dyb_ref[...]
        zb = zb_ref[...]
        mixed = mixed_ref[...]
        sc = sc_ref[...]
        sg = _sigmoid(zb)
        dyb0 = dyb * (zb * sg)
        db_ref[:, W:] = dyb * (mixed * sc) * (sg * (1.0 + zb * (1.0 - sg)))
        dsc_ref[...] += jnp.sum(dyb0 * mixed, axis=0, keepdims=True)
        dmix = (dyb0 * sc).astype(BF16)
        t = tile * TM + lax.broadcasted_iota(jnp.int32, (TM, 128), 0)
        for gi, win in enumerate(WINS):
            cs = slice(128 * gi, 128 * (gi + 1))
            dpw_ref[gi] += _dot_tn(pooled_ref[:, cs], dmix[:, cs])
            dpool = _dot_nt(dmix[:, cs], pw_ref[gi])
            cnt = jnp.minimum(t + 1, win).astype(F32)
            e = dpool / cnt
            buf[0:TM, cs] = e
            acc = e - dpool
            for k in range(1, win):
                acc = acc + buf[k:k + TM, cs]
            db_ref[:, cs] = acc
        buf[TM:TM + 16, :] = buf[0:16, :]

    return _pcall(
        body, f"pool_bwd_l{l}", (NT,),
        [_rows(W, 0, True), _rows(W, 3, True), _rows(W, 0, True), _rows(W, 0, True),
         _full((4, 128, 128)), _full((1, W)), _ANY],
        [_rows(2 * W, 1, True), _full((4, 128, 128)), _full((1, W))],
        [SDS((L, NIN), F32), SDS((4, 128, 128), F32), SDS((1, W), F32)],
        (dyb, proj, mixed, pooled, pw, scale, dproj),
        scratch=[pltpu.VMEM((TM + 16, W), F32)], aliases={6: 0}, jobs=jobs)


def _s5_bwd(l, dya, proj, y1, q, sre, sim, cret, cimnt, bret, bimt, st_re, st_im, cr_re, cr_im, dsk, wglu, dproj, jobs=()):
    def halo(i):
        return (jnp.maximum((NT - 1 - i) * (TM // 8) - 1, 0), 0)

    def body(dya_ref, ua_ref, za_ref, y1_ref, q_ref, sre_ref, sim_ref, hre_ref, him_ref,
             cret_ref, cimnt_ref, bret_ref, bimt_ref, st_re_ref, st_im_ref, cr_re_ref, cr_im_ref, d_ref, wg_ref, _,
             da_ref, dwg_ref, dbg_ref, dd_ref, dcre_ref, dcimn_ref, dbre_ref, dbim_ref, dare_ref, daim_ref,
             lre, lim, car_ref):
        i = pl.program_id(0)
        tile = NT - 1 - i

        @pl.when(i == 0)
        def _():
            for ref in (dwg_ref, dbg_ref, dd_ref, dcre_ref, dcimn_ref, dbre_ref, dbim_ref, dare_ref, daim_ref, car_ref):
                ref[...] = jnp.zeros_like(ref)

        u = ua_ref[...]
        za = za_ref[...]
        y1 = y1_ref[...]
        dya = dya_ref[...]
        y2 = _gelu(y1)
        sg = _sigmoid(q_ref[...])
        sgz = _sigmoid(za)
        dy3 = dya * (za * sgz)
        da_ref[:, W:] = dya * (y2 * sg) * (sgz * (1.0 + za * (1.0 - sgz)))
        dq = dy3 * y2 * (sg * (1.0 - sg))
        dqb = dq.astype(BF16)
        dy2 = dy3 * sg + _dot_nt(dqb, wg_ref[...])
        dwg_ref[...] += _dot_tn(y2.astype(BF16), dqb)
        dbg_ref[...] += jnp.sum(dq, axis=0, keepdims=True)
        dy1 = dy2 * _gelu_grad(y1)
        dd_ref[...] += jnp.sum(dy1 * u, axis=0, keepdims=True)
        dy1b = dy1.astype(BF16)
        ub = u.astype(BF16)
        for k in range(4):
            blk = slice(512 * k, 512 * (k + 1))
            ks = slice(128 * k, 128 * (k + 1))
            lre[:, blk] = _dot(dy1b[:, ks], cret_ref[k])
            lim[:, blk] = _dot(dy1b[:, ks], cimnt_ref[k])
            dcre_ref[k] += _dot_tn(sre_ref[:, blk].astype(BF16), dy1b[:, ks])
            dcimn_ref[k] += _dot_tn(sim_ref[:, blk].astype(BF16), dy1b[:, ks])
        carry = _scan_tile(lre, lim, st_re_ref, st_im_ref, cr_re_ref, cr_im_ref, _load_carry(car_ref), True)
        _store_carry(car_ref, carry)

        rowid = lax.broadcasted_iota(jnp.int32, (8, 512), 0)
        gate = (tile > 0).astype(F32)

        def chunk(c, _):
            rows = pl.ds(pl.multiple_of(c * 8, 8), 8)
            prows = pl.ds(pl.multiple_of(jnp.maximum(c - 1, 0) * 8, 8), 8)
            for lb in range(GP // 512):
                cols = slice(lb * 512, (lb + 1) * 512)
                sr = sre_ref[rows, cols]
                si = sim_ref[rows, cols]
                pr = jnp.where(c == 0, hre_ref[7:8, cols] * gate, sre_ref[prows, cols][7:8])
                pi = jnp.where(c == 0, him_ref[7:8, cols] * gate, sim_ref[prows, cols][7:8])
                sr = jnp.where(rowid == 0, pr, pltpu.roll(sr, 1, 0))
                si = jnp.where(rowid == 0, pi, pltpu.roll(si, 1, 0))
                lr = lre[rows, cols]
                li = lim[rows, cols]
                dare_ref[:, cols] += sr * lr + si * li
                daim_ref[:, cols] += sr * li - si * lr
            return 0

        lax.fori_loop(0, TM // 8, chunk, 0)

        for k in range(4):
            blk = slice(512 * k, 512 * (k + 1))
            ks = slice(128 * k, 128 * (k + 1))
            lrb = lre[:, blk].astype(BF16)
            lib = lim[:, blk].astype(BF16)
            da_ref[:, ks] = dy1[:, ks] * d_ref[:, ks] + _dot(lrb, bret_ref[k]) + _dot(lib, bimt_ref[k])
            dbre_ref[k] += _dot_tn(ub[:, ks], lrb)
            dbim_ref[k] += _dot_tn(ub[:, ks], lib)

    return _pcall(
        body, f"s5_bwd_l{l}", (NT,),
        [_rows(W, 0, True), _rows(W, 0, True), _rows(W, 1, True), _rows(W, 0, True), _rows(W, 0, True),
         _rows(GP, 0, True), _rows(GP, 0, True),
         pl.BlockSpec((8, GP), halo), pl.BlockSpec((8, GP), halo),
         _full((4, 128, 512)), _full((4, 128, 512)), _full((4, 512, 128)), _full((4, 512, 128)),
         _full((8, GP)), _full((8, GP)), _full((8, GP)), _full((8, GP)), _full((1, W)), _layer(l, (W, W)), _ANY],
        [_rows(2 * W, 0, True), _full((W, W)), _full((1, W)), _full((1, W)),
         _full((4, 512, 128)), _full((4, 512, 128)), _full((4, 128, 512)), _full((4, 128, 512)),
         _full((8, GP)), _full((8, GP))],
        [SDS((L, NIN), F32), SDS((W, W), F32), SDS((1, W), F32), SDS((1, W), F32),
         SDS((4, 512, 128), F32), SDS((4, 512, 128), F32), SDS((4, 128, 512), F32), SDS((4, 128, 512), F32),
         SDS((8, GP), F32), SDS((8, GP), F32)],
        (dya, proj, proj, y1, q, sre, sim, sre, sim, cret, cimnt, bret, bimt, st_re, st_im, cr_re, cr_im, dsk, wglu,
         dproj),
        scratch=[pltpu.VMEM((TM, GP), F32), pltpu.VMEM((TM, GP), F32), pltpu.VMEM((8, GP), F32)],
        aliases={19: 0}, jobs=jobs)


def _inproj_dw(l, r, h, dproj, jobs=()):
    def body(h_ref, dp_ref, dw_ref, *db_ref):
        @pl.when(pl.program_id(1) == 0)
        def _():
            dw_ref[...] = jnp.zeros_like(dw_ref)
            for ref in db_ref:
                ref[...] = jnp.zeros_like(ref)

        dp = dp_ref[...]
        dw_ref[...] += _dot_tn(h_ref[...], dp.astype(BF16))
        for ref in db_ref:
            ref[...] += jnp.sum(dp, axis=0, keepdims=True)

    bias = r == 0
    return _pcall(
        body, f"inproj_dw{r}_l{l}", (4, NT),
        [pl.BlockSpec((TM, D // 2), lambda j, i: (i, r)), pl.BlockSpec((TM, 1024), lambda j, i: (i, j))],
        [pl.BlockSpec((None, D // 2, 1024), lambda j, i: (j, 0, 0))] + [pl.BlockSpec((1, 1024), lambda j, i: (0, j))] * bias,
        [SDS((4, D // 2, 1024), F32)] + [SDS((1, NIN), F32)] * bias,
        (h, dproj), jobs=jobs)


def _inproj_dx(l, dproj, w, x, g, dxn, jobs=()):
    def body(dp_ref, w_ref, x_ref, g_ref, dxn_ref, dx_ref, dg_ref):
        @pl.when(pl.program_id(0) == 0)
        def _():
            dg_ref[...] = jnp.zeros_like(dg_ref)

        dh = jnp.zeros((TM, D), F32)
        for j in range(4):
            dh = dh + _dot_nt(dp_ref[:, j * 1024:(j + 1) * 1024].astype(BF16), w_ref[j])
        xv = x_ref[...]
        r = lax.rsqrt(jnp.mean(xv * xv, axis=-1, keepdims=True) + EPS)
        xn = xv * r
        dg_ref[...] += jnp.sum(dh * xn, axis=0, keepdims=True)
        dn = dh * g_ref[...]
        dx_ref[...] = dxn_ref[...] + r * (dn - xn * jnp.mean(dn * xn, axis=-1, keepdims=True))

    return _pcall(
        body, f"inproj_dx_l{l}", (NT,),
        [_rows(NIN), _layer(l, (4, D, 1024)), _rows(D), _full((1, D)), _rows(D)],
        [_rows(D), _full((1, D))],
        [SDS((L, D), F32), SDS((1, D), F32)],
        (dproj, w, x, g, dxn), jobs=jobs)


def _discretize(log_dt, lam_re, lam_im, b_re, b_im):
    dt = jnp.exp(log_dt)[:, None]
    mag = jnp.exp(lam_re * dt)
    ang = lam_im * dt
    abar_re = mag * jnp.cos(ang)
    abar_im = mag * jnp.sin(ang)
    num_re = abar_re - 1.0
    num_im = abar_im
    den = lam_re * lam_re + lam_im * lam_im
    coef_re = (num_re * lam_re + num_im * lam_im) / den
    coef_im = (num_im * lam_re - num_re * lam_im) / den
    bbar_re = coef_re[..., None] * b_re - coef_im[..., None] * b_im
    bbar_im = coef_re[..., None] * b_im + coef_im[..., None] * b_re
    return abar_re, abar_im, bbar_re, bbar_im


def _powers(abar_re, abar_im):
    ar, ai = abar_re.reshape(1, GP), abar_im.reshape(1, GP)
    rows_re, rows_im = [ar], [ai]
    for _ in range(7):
        pr, pi = rows_re[-1], rows_im[-1]
        rows_re.append(pr * ar - pi * ai)
        rows_im.append(pr * ai + pi * ar)
    neg_im = [-r for r in rows_im]
    return (jnp.concatenate(rows_re, axis=0), jnp.concatenate(rows_im, axis=0), jnp.concatenate(neg_im, axis=0),
            jnp.concatenate(rows_re[::-1], axis=0), jnp.concatenate(neg_im[::-1], axis=0))


_EYE8 = functools.partial(jnp.eye, 8, dtype=F32)


def _expand_in(b):
    return jnp.einsum("kgpc,gh->kgchp", b.reshape(4, 8, P, C), _EYE8()).reshape(4, 128, 512)


def _extract_in(e):
    return jnp.einsum("kgchp,gh->kgpc", e.reshape(4, 8, C, 8, P), _EYE8()).reshape(G, P, C)


def _expand_out(c):
    return jnp.einsum("kgcp,gh->kgphc", c.reshape(4, 8, C, P), _EYE8()).reshape(4, 512, 128)


def _extract_out(e):
    return jnp.einsum("kgphc,gh->kgcp", e.reshape(4, 8, P, 8, C), _EYE8()).reshape(G, C, P)


SMALL = ("norm_g", "b_in", "ssm_log_dt", "ssm_lam_re", "ssm_lam_im", "ssm_b_re", "ssm_b_im",
         "ssm_c_re", "ssm_c_im", "ssm_d", "ssm_b_glu", "pool_w", "pool_scale")
BIG = ("w_in", "ssm_w_glu", "w_branch_a", "w_branch_b", "w_out")


def _step(x, target, w, m, v, place):
    sp = {n: w[n] for n in SMALL}
    final_norm_g = w["final_norm_g"]
    wbuf = dict(zip(BIG, _cast_own(place, [w[n] for n in BIG])))
    saved = []
    for l in range(DEPTH):
        disc_in = (sp["ssm_log_dt"][l], sp["ssm_lam_re"][l], sp["ssm_lam_im"][l], sp["ssm_b_re"][l], sp["ssm_b_im"][l])
        (abar_re, abar_im, bbar_re, bbar_im), disc_vjp = jax.vjp(_discretize, *disc_in)
        pw_re, pw_im, pw_imn, pw_re_rev, pw_imn_rev = _powers(abar_re, abar_im)
        b_re_x, b_im_x = _expand_in(bbar_re), _expand_in(bbar_im)
        c_re_x, c_imn_x = _expand_out(sp["ssm_c_re"][l]), _expand_out(-sp["ssm_c_im"][l])
        g = sp["norm_g"][l].reshape(1, D)
        dsk = sp["ssm_d"][l].reshape(1, W)
        scale = sp["pool_scale"][l].reshape(1, W)
        pw = sp["pool_w"][l].astype(BF16)

        if l == 0:
            (wbuf["w_in"],) = _comm_only("gather_w_in_l0", _GatherJob([wbuf["w_in"]], 0))[0]
        (h, proj), res = _norm_inproj(l, x, g, wbuf["w_in"], sp["b_in"][l].reshape(1, NIN),
                                      _GatherJob([wbuf[n] for n in BIG[1:]], l))
        wbuf.update(zip(BIG[1:], res[0][0]))
        wg = dict(wbuf, ssm_w_glu=wbuf["ssm_w_glu"].reshape(DEPTH, W, W), w_out=wbuf["w_out"].reshape(DEPTH, D, D))
        (sre, sim, y1, q, ya), res = _s5_fwd(
            l, proj, jnp.concatenate([b_re_x, b_im_x], axis=2).astype(BF16), c_re_x.astype(BF16), c_imn_x.astype(BF16),
            pw_re, pw_im, dsk, wg["ssm_w_glu"], sp["ssm_b_glu"][l].reshape(1, W),
            _GatherJob([wbuf["w_in"]], l + 1) if l + 1 < DEPTH else None)
        if res:
            (wbuf["w_in"],) = res[0][0]
            wg["w_in"] = wbuf["w_in"]
        pooled, mixed, yb = _pool_fwd(l, proj, pw, scale)
        pa, pb, mg, x_next = _merge_out(l, ya, yb, proj, x, wg["w_branch_a"], wg["w_branch_b"], wg["w_out"])
        saved.append(dict(x=x, g=g, dsk=dsk, scale=scale, pw=pw, h=h, proj=proj, sre=sre, sim=sim, y1=y1, q=q, ya=ya,
                          pooled=pooled, mixed=mixed, yb=yb, pa=pa, pb=pb, mg=mg, disc_vjp=disc_vjp,
                          powers=(pw_re, pw_imn, pw_re_rev, pw_imn_rev),
                          b_re_x=b_re_x, b_im_x=b_im_x, c_re_x=c_re_x, c_imn_x=c_imn_x))
        x = x_next

    loss, dx, dgf = _loss_head(x, final_norm_g.reshape(1, D), target)

    gs = {n: [None] * DEPTH for n in SMALL}
    red = _Reducer(place, w, m, v)
    for l in reversed(range(DEPTH)):
        s = saved[l]
        (dproj, dya, dyb, dwo, dwa, dwb), res = _merge_out_bwd(
            l, dx, s["mg"], s["proj"], s["pa"], s["pb"], s["ya"], s["yb"],
            wg["w_out"], wg["w_branch_a"], wg["w_branch_b"], red.jobs())
        red.land(res)
        (dproj, dpw, dsc), res = _pool_bwd(l, dyb, s["proj"], s["mixed"], s["pooled"], s["pw"], s["scale"], dproj,
                                          red.jobs())
        red.land(res)
        t = lambda a: jnp.swapaxes(a, 1, 2).astype(BF16)
        (dproj, dwg, dbg, dd, dcre, dcimn, dbre, dbim, dare, daim), res = _s5_bwd(
            l, dya, s["proj"], s["y1"], s["q"], s["sre"], s["sim"],
            t(s["c_re_x"]), t(s["c_imn_x"]), t(s["b_re_x"]), t(s["b_im_x"]),
            *s["powers"], s["dsk"], wg["ssm_w_glu"], dproj, red.jobs())
        red.land(res)
        red.add(l, "rest", BIG[1:], [dwg.reshape(4, W // 4, W), dwa, dwb, dwo.reshape(4, D // 4, D)])
        for r in range(2):
            outs, res = _inproj_dw(l, r, s["h"], dproj, red.jobs())
            red.land(res)
            red.add(l, f"in{r}", BIG[:1], outs[:1], r * (D // 2))
            if r == 0:
                dbin = outs[1]
        (dx, dg), res = _inproj_dx(l, dproj, wg["w_in"], s["x"], s["g"], dx, red.jobs())
        red.land(res)

        d_abar_re = jnp.sum(dare, axis=0).reshape(G, P)
        d_abar_im = jnp.sum(daim, axis=0).reshape(G, P)
        dlog_dt, dlam_re, dlam_im, db_re, db_im = s["disc_vjp"](
            (d_abar_re, d_abar_im, _extract_in(dbre), _extract_in(dbim)))
        gs["norm_g"][l] = dg.reshape(D)
        gs["b_in"][l] = dbin.reshape(NIN)
        gs["ssm_log_dt"][l] = dlog_dt
        gs["ssm_lam_re"][l] = dlam_re
        gs["ssm_lam_im"][l] = dlam_im
        gs["ssm_b_re"][l] = db_re
        gs["ssm_b_im"][l] = db_im
        gs["ssm_c_re"][l] = _extract_out(dcre)
        gs["ssm_c_im"][l] = -_extract_out(dcimn)
        gs["ssm_d"][l] = dd.reshape(W)
        gs["ssm_b_glu"][l] = dbg.reshape(W)
        gs["pool_w"][l] = dpw
        gs["pool_scale"][l] = dsc.reshape(W)
    gs = {n: jnp.stack(v) for n, v in gs.items()}
    gs["final_norm_g"] = dgf

    natural = {n: w[n].shape for n in REPLICATED}
    rw, rm, rv = {}, {}, {}
    for n in REPLICATED:
        shape = DENSE.get(n, natural[n])
        shape = shape if len(shape) > 2 else (shape[0], 1, shape[1])
        gs[n], rw[n], rm[n], rv[n] = (a.reshape(shape) for a in (gs[n], w[n], m[n], v[n]))
    small = [gs[n] for n in REPLICATED] + [loss.reshape(2, 1, 128)]
    jobs = red.jobs()
    res = _pcall(None, "tail_exchange", (), [], [], [], [], jobs=jobs + [_SiblingJob(small, False)])[1]
    red.land(res[:len(jobs)])
    pair_small = _small_pair_sum(place, small, res[-1][1], [BF16] * len(SMALL) + [F32, F32])
    jobs = red.jobs()
    res = _pcall(None, "tail_gather", (), [], [], [], [], jobs=jobs + [_ChipGatherJob(pair_small)])[1]
    red.land(res[:len(jobs)])
    assert not red.active
    small_parts = dict(zip(REPLICATED + ("loss",), res[-1][0]))

    k = len(REPLICATED)
    outs = _adamw_small("adamw_small", [rw[n] for n in REPLICATED], [small_parts[n] for n in REPLICATED],
                        [rm[n] for n in REPLICATED], [rv[n] for n in REPLICATED], small_parts["loss"])
    results = {n: red.big[n] for n in BIG}
    results.update({n: [outs[1 + q * k + i].reshape(natural[n]) for q in range(4)] for i, n in enumerate(REPLICATED)})
    return outs[0][0, 0, 0], dx, results


def _place():
    x, y, c = lax.axis_index("x"), lax.axis_index("y"), lax.axis_index("c")
    chips = [(1 - x, y), (x, 1 - y), (1 - x, 1 - y)]
    return x, y, c, 2 * x + y, chips, [2 * cx + cy for cx, cy in chips]


def _remote(src, dst, ssem, rsem, dev):
    return pltpu.make_async_remote_copy(src_ref=src, dst_ref=dst, send_sem=ssem, recv_sem=rsem,
                                        device_id=dev, device_id_type=MESH)


class _GatherJob:
    def __init__(self, bufs, l):
        self.srcs, self.bufs, self.news, self.l = [], list(bufs), [], l
        self.scratch = [pltpu.SemaphoreType.DMA((len(self.bufs), 3))] * 4

    def _half(self, ref, k, h):
        rows = ref.shape[2] // 2
        return ref.at[self.l, k, pl.ds(pl.multiple_of(h * rows, 8), rows), :]

    def _ici(self, bufs, sems, a, j, k):
        _, _, c, _, chips, _ = _place()
        blk = self._half(bufs[a], k, c)
        return _remote(blk, blk, sems[0].at[a, j], sems[1].at[a, j], (*chips[j], c))

    def _d2d(self, bufs, sems, a, j, k, h):
        x, y, c, _, _, _ = _place()
        blk = self._half(bufs[a], k, h)
        return _remote(blk, blk, sems[2].at[a, j], sems[3].at[a, j], (x, y, 1 - c))

    def start(self, srcs, bufs, news, sems):
        me = _place()[3]
        for a in range(len(self.bufs)):
            for j in range(3):
                self._ici(bufs, sems, a, j, me).start()

    def finish(self, srcs, bufs, news, sems):
        _, _, c, me, _, cid = _place()
        pairs = [(a, j) for a in range(len(self.bufs)) for j in range(3)]
        for a, j in pairs:
            self._ici(bufs, sems, a, j, cid[j]).wait_recv()
            self._d2d(bufs, sems, a, j, cid[j], c).start()
        for a, j in pairs:
            self._d2d(bufs, sems, a, j, cid[j], 1 - c).wait_recv()
        for a, j in pairs:
            self._ici(bufs, sems, a, j, me).wait_send()
            self._d2d(bufs, sems, a, j, cid[j], c).wait_send()


class _SiblingJob:
    def __init__(self, srcs, rows_half):
        self.srcs, self.bufs, self.rows_half = list(srcs), [], rows_half
        self.news = [SDS((s.shape[0], s.shape[1] // 2, s.shape[2]) if rows_half else s.shape, s.dtype) for s in srcs]
        self.scratch = [pltpu.SemaphoreType.DMA((len(self.srcs),))] * 2

    def _copy(self, srcs, news, sems, a):
        x, y, c, _, _, _ = _place()
        src = srcs[a]
        if self.rows_half:
            rows = src.shape[1] // 2
            src = src.at[:, pl.ds(pl.multiple_of((1 - c) * rows, 8), rows), :]
        return _remote(src, news[a], sems[0].at[a], sems[1].at[a], (x, y, 1 - c))

    def start(self, srcs, bufs, news, sems):
        for a in range(len(self.srcs)):
            self._copy(srcs, news, sems, a).start()

    def finish(self, srcs, bufs, news, sems):
        for a in range(len(self.srcs)):
            self._copy(srcs, news, sems, a).wait()


class _ScatterJob:
    def __init__(self, parts):
        self.srcs, self.bufs = list(parts), []
        self.news = [SDS((3,) + p.shape[1:], p.dtype) for p in parts]
        self.scratch = [pltpu.SemaphoreType.DMA((len(self.srcs), 3))] * 2

    def _copy(self, srcs, news, sems, a, j):
        _, _, c, _, chips, cid = _place()
        return _remote(srcs[a].at[cid[j]], news[a].at[j], sems[0].at[a, j], sems[1].at[a, j], (*chips[j], c))

    def start(self, srcs, bufs, news, sems):
        for a in range(len(self.srcs)):
            for j in range(3):
                self._copy(srcs, news, sems, a, j).start()

    def finish(self, srcs, bufs, news, sems):
        for a in range(len(self.srcs)):
            for j in range(3):
                self._copy(srcs, news, sems, a, j).wait()


def _comm_only(name, job):
    return _pcall(None, name, (), [], [], [], [], jobs=[job])[1][0]


class _ChipGatherJob(_GatherJob):
    def __init__(self, bufs):
        super().__init__(bufs, None)

    def _half(self, ref, k, h):
        return ref.at[k, h]


def _cast_own(place, ws):
    n = len(ws)

    def body(p_ref, *refs):
        for i_ref, o_ref in zip(refs[:n], refs[n:]):
            o_ref[...] = i_ref[...].astype(BF16)

    return pl.pallas_call(
        body, name="cast_own_shards",
        grid_spec=pltpu.PrefetchScalarGridSpec(
            num_scalar_prefetch=1, grid=(DEPTH,),
            in_specs=[pl.BlockSpec((None,) + a.shape[1:], lambda l, p: (l, 0, 0)) for a in ws],
            out_specs=[pl.BlockSpec((None, None) + a.shape[1:], lambda l, p: (l, p[1], 0, 0)) for a in ws]),
        out_shape=[SDS((DEPTH, 4) + a.shape[1:], BF16) for a in ws],
        compiler_params=_params("arbitrary"),
    )(place, *ws)


def _half_tiles(a_):
    rows = a_ // 2
    ta = min(rows, 256)
    return rows, ta, rows // ta


def _pair_sum_bf16(name, place, own, recv):
    _, a_, b_ = own.shape
    rows, ta, nh = _half_tiles(a_)

    def body(p_ref, own_ref, recv_ref, out_ref):
        out_ref[...] = (own_ref[...] + recv_ref[...]).astype(BF16)

    return pl.pallas_call(
        body, name=name,
        grid_spec=pltpu.PrefetchScalarGridSpec(
            num_scalar_prefetch=1, grid=(4, nh),
            in_specs=[pl.BlockSpec((None, ta, b_), lambda s, i, p: (s, p[0] * nh + i, 0)),
                      pl.BlockSpec((None, ta, b_), lambda s, i, p: (s, i, 0))],
            out_specs=pl.BlockSpec((None, ta, b_), lambda s, i, p: (s, i, 0))),
        out_shape=SDS((4, rows, b_), BF16),
        compiler_params=_params("arbitrary", "arbitrary"),
    )(place, own, recv)


def _shard_sum(name, place, own, recv, rbuf):
    _, a_, b_ = own.shape
    rows, ta, nh = _half_tiles(a_)

    def body(p_ref, own_ref, recv_ref, r_ref, out_ref):
        acc = own_ref[...] + recv_ref[...]
        for j in range(3):
            acc = acc + r_ref[j].astype(F32)
        out_ref[...] = acc

    return pl.pallas_call(
        body, name=name,
        grid_spec=pltpu.PrefetchScalarGridSpec(
            num_scalar_prefetch=1, grid=(nh,),
            in_specs=[pl.BlockSpec((None, ta, b_), lambda i, p: (p[1], p[0] * nh + i, 0)),
                      pl.BlockSpec((None, ta, b_), lambda i, p: (p[1], i, 0)),
                      pl.BlockSpec((3, ta, b_), lambda i, p: (0, i, 0))],
            out_specs=pl.BlockSpec((ta, b_), lambda i, p: (i, 0))),
        out_shape=SDS((rows, b_), F32),
        compiler_params=_params("arbitrary"),
    )(place, own, recv, rbuf)


def _small_pair_sum(place, mine, recv, dtypes):
    n = len(mine)

    def body(p_ref, *refs):
        for m_ref, r_ref, o_ref in zip(refs[:n], refs[n:2 * n], refs[2 * n:]):
            o_ref[...] = (m_ref[...] + r_ref[...]).astype(o_ref.dtype)

    def whole(a):
        zeros = (0,) * a.ndim
        return pl.BlockSpec(a.shape, lambda i, p: zeros)

    def mine_blk(a):
        zeros = (0,) * a.ndim
        return pl.BlockSpec((None,) + a.shape, lambda i, p: (p[1],) + zeros)

    return pl.pallas_call(
        body, name="small_pair_sum",
        grid_spec=pltpu.PrefetchScalarGridSpec(
            num_scalar_prefetch=1, grid=(1,),
            in_specs=[whole(a) for a in mine] + [whole(a) for a in recv],
            out_specs=[mine_blk(a) for a in mine]),
        out_shape=[SDS((4,) + a.shape, dt) for a, dt in zip(mine, dtypes)],
        compiler_params=_params("arbitrary"),
    )(place, *mine, *recv)


def _adam_math(w, g, m, v):
    m = B1 * m + (1.0 - B1) * g
    v = B2 * v + (1.0 - B2) * (g * g)
    m_hat = m / (1.0 - B1 ** STEP)
    v_hat = v / (1.0 - B2 ** STEP)
    delta = -LR * (m_hat / (jnp.sqrt(v_hat) + EPS_A) + WD * w)
    return delta, m, v


def _adamw_big(name, l, row0, w, m, v, mine, other, prev, jobs=()):
    _, _, b_ = w.shape
    _, ta, nh = _half_tiles(2 * mine.shape[0])
    prev = list(prev or [])

    def body(w_ref, m_ref, v_ref, mine_ref, other_ref, *rest):
        g_ref, d_ref, mo_ref, vo_ref = rest[len(prev):]
        g = jnp.where(pl.program_id(0) == lax.axis_index("c"), mine_ref[...], other_ref[...])
        g_ref[...] = g
        d_ref[...], mo_ref[...], vo_ref[...] = _adam_math(w_ref[...], g, m_ref[...], v_ref[...])

    slab = pl.BlockSpec((None, ta, b_), lambda h, i: (l, row0 // ta + h * nh + i, 0))
    half = pl.BlockSpec((ta, b_), lambda h, i: (i, 0))
    outs, res = _pcall(
        body, name, (2, nh), [slab, slab, slab, half, half] + [_ANY] * len(prev), [slab] * 4, [SDS(w.shape, F32)] * 4,
        (w, m, v, mine, other, *prev), aliases={5 + k: k for k in range(len(prev))}, jobs=jobs)
    return outs, res


def _adamw_small(name, ws, parts, ms, vs, loss_parts=None):
    k = len(ws)
    extra = [] if loss_parts is None else [loss_parts]

    def chip_sum(p_ref):
        p = [p_ref[k].astype(F32) for k in range(4)]
        return ((p[0] + p[1]) + p[2]) + p[3]

    def body(*refs):
        w_refs, p_refs, m_refs, v_refs = refs[:k], refs[k:2 * k], refs[2 * k:3 * k], refs[3 * k:4 * k]
        outs = refs[4 * k + len(extra):]
        if extra:
            outs[0][...] = chip_sum(refs[4 * k])
            outs = outs[1:]
        for a in range(k):
            g = chip_sum(p_refs[a])
            outs[a][...] = g
            outs[k + a][...], outs[2 * k + a][...], outs[3 * k + a][...] = _adam_math(
                w_refs[a][...], g, m_refs[a][...], v_refs[a][...])

    like = [SDS(a.shape, F32) for a in ws]
    return pl.pallas_call(
        body, name=name,
        out_shape=([SDS(loss_parts.shape[1:], F32)] if extra else []) + like * 4,
        compiler_params=pltpu.CompilerParams(vmem_limit_bytes=VMEM_LIMIT),
    )(*ws, *parts, *ms, *vs, *extra)


class _Reducer:
    def __init__(self, place, w, m, v):
        self.place, self.w, self.m, self.v = place, w, m, v
        self.active, self.riding = [], []
        self.big = {n: None for n in BIG}

    def add(self, l, tag, names, own, row0=0):
        self.active.append(dict(l=l, key=f"{tag}_l{l}", names=names, own=list(own), row0=row0, stage=0))

    def jobs(self):
        self.riding = list(self.active)
        return [(_SiblingJob(g["own"], True), _ScatterJob(g.get("parts", [])), _SiblingJob(g.get("shard", []), False))
                [g["stage"]] for g in self.riding]

    def land(self, res):
        for g, (_, news) in zip(self.riding, res):
            if g["stage"] == 0:
                g["recv"] = news
                g["parts"] = [_pair_sum_bf16(f"pair_sum_{n}_{g['key']}", self.place, o, r)
                              for n, o, r in zip(g["names"], g["own"], news)]
            elif g["stage"] == 1:
                g["shard"] = [_shard_sum(f"shard_sum_{n}_{g['key']}", self.place, o, r, rb)
                              for n, o, r, rb in zip(g["names"], g["own"], g["recv"], news)]
            else:
                for n, mine, other in zip(g["names"], g["shard"], news):
                    self.big[n] = _adamw_big(f"adamw_{n}_{g['key']}", g["l"], g["row0"], self.w[n], self.m[n], self.v[n],
                                             mine, other, self.big[n])[0]
                self.active.remove(g)
            g["stage"] += 1
        self.riding = []


WEIGHTS = ("norm_g", "w_in", "b_in", "ssm_log_dt", "ssm_lam_re", "ssm_lam_im", "ssm_b_re", "ssm_b_im", "ssm_c_re",
           "ssm_c_im", "ssm_d", "ssm_w_glu", "ssm_b_glu", "pool_w", "pool_scale", "w_branch_a", "w_branch_b", "w_out",
           "final_norm_g")
REPLICATED = SMALL + ("final_norm_g",)
DENSE = {"ssm_b_re": (DEPTH, G, P * C), "ssm_b_im": (DEPTH, G, P * C), "final_norm_g": (2, D // 2)}


def kernel(x, norm_g, w_in, b_in, ssm_log_dt, ssm_lam_re, ssm_lam_im, ssm_b_re, ssm_b_im, ssm_c_re, ssm_c_im, ssm_d, ssm_w_glu, ssm_b_glu, pool_w, pool_scale, w_branch_a, w_branch_b, w_out, final_norm_g, loss_target, m_norm_g, m_w_in, m_b_in, m_ssm_log_dt, m_ssm_lam_re, m_ssm_lam_im, m_ssm_b_re, m_ssm_b_im, m_ssm_c_re, m_ssm_c_im, m_ssm_d, m_ssm_w_glu, m_ssm_b_glu, m_pool_w, m_pool_scale, m_w_branch_a, m_w_branch_b, m_w_out, m_final_norm_g, v_norm_g, v_w_in, v_b_in, v_ssm_log_dt, v_ssm_lam_re, v_ssm_lam_im, v_ssm_b_re, v_ssm_b_im, v_ssm_c_re, v_ssm_c_im, v_ssm_d, v_ssm_w_glu, v_ssm_b_glu, v_pool_w, v_pool_scale, v_w_branch_a, v_w_branch_b, v_w_out, v_final_norm_g):
    w = dict(zip(WEIGHTS, (norm_g, w_in, b_in, ssm_log_dt, ssm_lam_re, ssm_lam_im, ssm_b_re, ssm_b_im, ssm_c_re,
                           ssm_c_im, ssm_d, ssm_w_glu, ssm_b_glu, pool_w, pool_scale, w_branch_a, w_branch_b, w_out,
                           final_norm_g)))
    m = dict(zip(WEIGHTS, (m_norm_g, m_w_in, m_b_in, m_ssm_log_dt, m_ssm_lam_re, m_ssm_lam_im, m_ssm_b_re, m_ssm_b_im,
                           m_ssm_c_re, m_ssm_c_im, m_ssm_d, m_ssm_w_glu, m_ssm_b_glu, m_pool_w, m_pool_scale,
                           m_w_branch_a, m_w_branch_b, m_w_out, m_final_norm_g)))
    v = dict(zip(WEIGHTS, (v_norm_g, v_w_in, v_b_in, v_ssm_log_dt, v_ssm_lam_re, v_ssm_lam_im, v_ssm_b_re, v_ssm_b_im,
                           v_ssm_c_re, v_ssm_c_im, v_ssm_d, v_ssm_w_glu, v_ssm_b_glu, v_pool_w, v_pool_scale,
                           v_w_branch_a, v_w_branch_b, v_w_out, v_final_norm_g)))
    place = jnp.stack([lax.axis_index("c"), 2 * lax.axis_index("x") + lax.axis_index("y")]).astype(jnp.int32)

    total_loss, dx, results = _step(x[0], loss_target[0], w, m, v, place)
    return (total_loss, dx[None], *[results[n][q] for q in range(4) for n in WEIGHTS])
```

```python
import functools

import jax
import jax.numpy as jnp
from jax import lax
from jax.experimental import pallas as pl
from jax.experimental.pallas import tpu as pltpu

F32, BF16 = jnp.float32, jnp.bfloat16
SDS = jax.ShapeDtypeStruct
MESH = pl.DeviceIdType.MESH

DEPTH = 2
L = 2048
D = 1024
NIN = 4096
W = 512
G, P, C = 32, 64, 16
GP = G * P
WINS = (2, 4, 8, 16)
TM = 256
NT = L // TM
TK = 1024
EPS = 1e-6
VMEM_LIMIT = 56 * 2**20

LR, B1, B2, EPS_A, WD, STEP = 0.001, 0.9, 0.999, 1e-08, 0.01, 10


def _params(*sem):
    return pltpu.CompilerParams(dimension_semantics=sem, vmem_limit_bytes=VMEM_LIMIT)


_ANY = pl.BlockSpec(memory_space=pl.ANY)


def _full(shape):
    zeros = (0,) * len(shape)
    return pl.BlockSpec(shape, lambda *_: zeros)


def _layer(l, shape):
    zeros = (0,) * len(shape)
    return pl.BlockSpec((None,) + shape, lambda *_: (l,) + zeros)


def _rows(width, col=0, reverse=False):
    if reverse:
        return pl.BlockSpec((TM, width), lambda i: (NT - 1 - i, col))
    return pl.BlockSpec((TM, width), lambda i: (i, col))


def _pcall(body, name, grid, in_specs, out_specs, out_shape, args, scratch=(), aliases=None, jobs=()):
    in_specs, out_specs, out_shape, args, scratch = list(in_specs), list(out_specs), list(out_shape), list(args), list(scratch)
    aliases = dict(aliases or {})
    jobs = [j for j in jobs if j is not None]
    n_in, n_out, n_scr = len(in_specs), len(out_specs), len(scratch)
    srcs = [s for j in jobs for s in j.srcs]
    bufs = [b for j in jobs for b in j.bufs]
    news = [s for j in jobs for s in j.news]
    aliases.update({n_in + len(srcs) + k: n_out + k for k in range(len(bufs))})

    def hosted(*refs):
        cuts = [n_in, len(srcs), len(bufs), n_out, len(bufs), len(news), n_scr]
        parts, p = [], 0
        for n in cuts:
            parts.append(refs[p:p + n])
            p += n
        ins, src_r, _, outs, buf_r, new_r, scr = parts
        sem_r = refs[p:]
        views, ps, pb, pn, pm = [], 0, 0, 0, 0
        for j in jobs:
            views.append((src_r[ps:ps + len(j.srcs)], buf_r[pb:pb + len(j.bufs)], new_r[pn:pn + len(j.news)],
                          sem_r[pm:pm + len(j.scratch)]))
            ps, pb, pn, pm = ps + len(j.srcs), pb + len(j.bufs), pn + len(j.news), pm + len(j.scratch)

        def run(phase):
            for j, v in zip(jobs, views):
                getattr(j, phase)(*v)

        def at_step(step):
            return functools.reduce(jnp.logical_and, [pl.program_id(d) == step(d) for d in range(len(grid))])

        if not grid:
            run("start")
            run("finish")
            return
        pl.when(at_step(lambda d: 0))(lambda: run("start"))
        body(*ins, *outs, *scr)
        pl.when(at_step(lambda d: grid[d] - 1))(lambda: run("finish"))

    outs = pl.pallas_call(
        hosted if jobs else body, name=name, **({"grid": grid} if grid else {}),
        in_specs=in_specs + [_ANY] * (len(srcs) + len(bufs)), out_specs=out_specs + [_ANY] * (len(bufs) + len(news)),
        out_shape=out_shape + [SDS(b.shape, b.dtype) for b in bufs] + news,
        input_output_aliases=aliases, scratch_shapes=scratch + [s for j in jobs for s in j.scratch],
        compiler_params=_params(*(("arbitrary",) * len(grid))))(*args, *srcs, *bufs)
    res, pb, pn = [], n_out, n_out + len(bufs)
    for j in jobs:
        res.append((list(outs[pb:pb + len(j.bufs)]), list(outs[pn:pn + len(j.news)])))
        pb, pn = pb + len(j.bufs), pn + len(j.news)
    return list(outs[:n_out]), res


def _dot(a, b):
    return jnp.dot(a, b, preferred_element_type=F32)


def _dot_nt(a, b):
    return lax.dot_general(a, b, (((1,), (1,)), ((), ())), preferred_element_type=F32)


def _dot_tn(a, b):
    return lax.dot_general(a, b, (((0,), (0,)), ((), ())), preferred_element_type=F32)


_K0 = 0.7978845608028654
_K1 = 0.044715


def _gelu(x):
    return 0.5 * x * (1.0 + jnp.tanh(_K0 * (x + _K1 * (x * x * x))))


def _gelu_grad(x):
    t = jnp.tanh(_K0 * (x + _K1 * (x * x * x)))
    return 0.5 * (1.0 + t) + 0.5 * x * (1.0 - t * t) * (_K0 * (1.0 + 3.0 * _K1 * x * x))


def _sigmoid(x):
    return jax.nn.sigmoid(x)


def _norm_inproj(l, x, g, w, b, job=None):
    def body(x_ref, g_ref, w_ref, b_ref, h_ref, proj_ref):
        xv = x_ref[...]
        r = lax.rsqrt(jnp.mean(xv * xv, axis=-1, keepdims=True) + EPS)
        hb = ((xv * r) * g_ref[...]).astype(BF16)
        h_ref[...] = hb
        for j in range(4):
            cs = slice(j * 1024, (j + 1) * 1024)
            proj_ref[:, cs] = _dot(hb, w_ref[j]) + b_ref[:, cs]

    return _pcall(
        body, f"norm_inproj_l{l}", (NT,),
        [_rows(D), _full((1, D)), _layer(l, (4, D, 1024)), _full((1, NIN))],
        [_rows(D), _rows(NIN)],
        [SDS((L, D), BF16), SDS((L, NIN), F32)],
        (x, g, w, b), jobs=[job])


def _scan_tile(re_ref, im_ref, st_re, st_im, cr_re, cr_im, carry, reverse):
    def chunk(ci, carry):
        c = (TM // 8 - 1 - ci) if reverse else ci
        rows = pl.ds(pl.multiple_of(c * 8, 8), 8)
        new = []
        for lb in range(GP // 512):
            cols = slice(lb * 512, (lb + 1) * 512)
            vr = re_ref[rows, cols]
            vi = im_ref[rows, cols]
            for s, d in enumerate((1, 2, 4)):
                ar = st_re[8 * s:8 * s + 8, cols]
                ai = st_im[8 * s:8 * s + 8, cols]
                sr = pltpu.roll(vr, 8 - d if reverse else d, 0)
                si = pltpu.roll(vi, 8 - d if reverse else d, 0)
                vr, vi = vr + ar * sr - ai * si, vi + ar * si + ai * sr
            cr, ci_ = carry[2 * lb], carry[2 * lb + 1]
            pr = cr_re[:, cols]
            pi = cr_im[:, cols]
            vr, vi = vr + pr * cr - pi * ci_, vi + pr * ci_ + pi * cr
            re_ref[rows, cols] = vr
            im_ref[rows, cols] = vi
            if reverse:
                new += [vr[0:1], vi[0:1]]
            else:
                new += [vr[7:8], vi[7:8]]
        return tuple(new)

    return lax.fori_loop(0, TM // 8, chunk, carry)


def _load_carry(car_ref):
    return tuple(car_ref[r:r + 1, lb * 512:(lb + 1) * 512] for lb in range(GP // 512) for r in (0, 1))


def _store_carry(car_ref, carry):
    for lb in range(GP // 512):
        car_ref[0:1, lb * 512:(lb + 1) * 512] = carry[2 * lb]
        car_ref[1:2, lb * 512:(lb + 1) * 512] = carry[2 * lb + 1]


def _s5_fwd(l, proj, bexp, cre, cimn, powers, dsk, wglu, bglu, job=None):
    def body(ua_ref, za_ref, bexp_ref, cre_ref, cimn_ref, st_re_ref, st_im_ref, cr_re_ref, cr_im_ref,
             d_ref, wg_ref, bg_ref, sre_ref, sim_ref, y1_ref, q_ref, ya_ref, car_ref):
        @pl.when(pl.program_id(0) == 0)
        def _():
            car_ref[...] = jnp.zeros_like(car_ref)

        u = ua_ref[...]
        ub = u.astype(BF16)
        for k in range(4):
            bu = _dot(ub[:, 128 * k:128 * (k + 1)], bexp_ref[k])
            sre_ref[:, 512 * k:512 * (k + 1)] = bu[:, :512]
            sim_ref[:, 512 * k:512 * (k + 1)] = bu[:, 512:]
        carry = _scan_tile(sre_ref, sim_ref, st_re_ref, st_im_ref, cr_re_ref, cr_im_ref, _load_carry(car_ref), False)
        _store_carry(car_ref, carry)
        for k in range(4):
            blk = slice(512 * k, 512 * (k + 1))
            ks = slice(128 * k, 128 * (k + 1))
            y0 = _dot(sre_ref[:, blk].astype(BF16), cre_ref[k]) + _dot(sim_ref[:, blk].astype(BF16), cimn_ref[k])
            y1_ref[:, ks] = y0 + d_ref[:, ks] * u[:, ks]
        y2 = _gelu(y1_ref[...])
        q = _dot(y2.astype(BF16), wg_ref[...]) + bg_ref[...]
        q_ref[...] = q
        za = za_ref[...]
        ya_ref[...] = ((y2 * _sigmoid(q)) * (za * _sigmoid(za))).astype(BF16)

    return _pcall(
        body, f"s5_fwd_l{l}", (NT,),
        [_rows(W, 0), _rows(W, 1), _full((4, 128, 1024)), _full((4, 512, 128)), _full((4, 512, 128)),
         _full((24, GP)), _full((24, GP)), _full((8, GP)), _full((8, GP)), _full((1, W)), _layer(l, (W, W)),
         _full((1, W))],
        [_rows(GP), _rows(GP), _rows(W), _rows(W), _rows(W)],
        [SDS((L, GP), F32), SDS((L, GP), F32), SDS((L, W), F32), SDS((L, W), F32), SDS((L, W), BF16)],
        (proj, proj, bexp, cre, cimn, *powers, dsk, wglu, bglu),
        scratch=[pltpu.VMEM((8, GP), F32)], jobs=[job])


def _pool_fwd(l, proj, pw, scale):
    def body(ub_ref, zb_ref, pw_ref, sc_ref, pooled_ref, mixed_ref, yb_ref, buf):
        i = pl.program_id(0)

        @pl.when(i == 0)
        def _():
            buf[0:16, :] = jnp.zeros((16, W), F32)

        u = ub_ref[...]
        buf[16:16 + TM, :] = u
        t = i * TM + lax.broadcasted_iota(jnp.int32, (TM, 128), 0)
        for gi, win in enumerate(WINS):
            cs = slice(128 * gi, 128 * (gi + 1))
            acc = u[:, cs]
            for k in range(1, win):
                acc = acc + buf[16 - k:16 - k + TM, cs]
            cnt = jnp.minimum(t + 1, win).astype(F32)
            pb = (acc / cnt - u[:, cs]).astype(BF16)
            pooled_ref[:, cs] = pb
            mixed_ref[:, cs] = _dot(pb, pw_ref[gi])
        zb = zb_ref[...]
        yb_ref[...] = ((mixed_ref[...] * sc_ref[...]) * (zb * _sigmoid(zb))).astype(BF16)
        buf[0:16, :] = buf[TM:TM + 16, :]

    return pl.pallas_call(
        body, name=f"pool_fwd_l{l}", grid=(NT,),
        in_specs=[_rows(W, 2), _rows(W, 3), _full((4, 128, 128)), _full((1, W))],
        out_specs=[_rows(W), _rows(W), _rows(W)],
        out_shape=[SDS((L, W), BF16), SDS((L, W), F32), SDS((L, W), BF16)],
        scratch_shapes=[pltpu.VMEM((TM + 16, W), F32)],
        compiler_params=_params("arbitrary"),
    )(proj, proj, pw, scale)


def _merge_out(l, ya, yb, proj, x, wa, wb, wo):
    def body(ya_ref, yb_ref, ga_ref, gb_ref, x_ref, wa_ref, wb_ref, wo_ref, pa_ref, pb_ref, mg_ref, xo_ref):
        ya = ya_ref[...]
        yb = yb_ref[...]
        for j in range(4):
            cs = slice(256 * j, 256 * (j + 1))
            pa_ref[:, cs] = _dot(ya, wa_ref[j])
            pb_ref[:, cs] = _dot(yb, wb_ref[j])
        merged = _sigmoid(ga_ref[...]) * pa_ref[...] + _sigmoid(gb_ref[...]) * pb_ref[...]
        mb = merged.astype(BF16)
        mg_ref[...] = mb
        xo_ref[...] = x_ref[...] + _dot(mb, wo_ref[...])

    return pl.pallas_call(
        body, name=f"merge_out_l{l}", grid=(NT,),
        in_specs=[_rows(W), _rows(W), _rows(D, 2), _rows(D, 3), _rows(D),
                  _layer(l, (4, W, 256)), _layer(l, (4, W, 256)), _layer(l, (D, D))],
        out_specs=[_rows(D), _rows(D), _rows(D), _rows(D)],
        out_shape=[SDS((L, D), F32), SDS((L, D), F32), SDS((L, D), BF16), SDS((L, D), F32)],
        compiler_params=_params("arbitrary"),
    )(ya, yb, proj, proj, x, wa, wb, wo)


def _loss_head(x, gf, target):
    def body(x_ref, g_ref, t_ref, loss_ref, dx_ref, dg_ref):
        @pl.when(pl.program_id(0) == 0)
        def _():
            loss_ref[...] = jnp.zeros_like(loss_ref)
            dg_ref[...] = jnp.zeros_like(dg_ref)

        xv = x_ref[...]
        g = g_ref[...]
        r = lax.rsqrt(jnp.mean(xv * xv, axis=-1, keepdims=True) + EPS)
        xn = xv * r
        err = xn * g - t_ref[...]
        part = jnp.sum(jnp.mean(err * err, axis=-1, keepdims=True), axis=0, keepdims=True)
        loss_ref[...] += 0.5 * part
        dy = err * (1.0 / D)
        dg_ref[...] += jnp.sum(dy * xn, axis=0, keepdims=True)
        dxn = dy * g
        dx_ref[...] = r * (dxn - xn * jnp.mean(dxn * xn, axis=-1, keepdims=True))

    return pl.pallas_call(
        body, name="loss_head", grid=(NT,),
        in_specs=[_rows(D), _full((1, D)), _rows(D)],
        out_specs=[_full((2, 128)), _rows(D), _full((1, D))],
        out_shape=[SDS((2, 128), F32), SDS((L, D), F32), SDS((1, D), F32)],
        compiler_params=_params("arbitrary"),
    )(x, gf, target)


def _merge_out_bwd(l, dxn, mg, proj, pa, pb, ya, yb, wo, wa, wb, jobs=()):
    def body(dx_ref, mg_ref, ga_ref, gb_ref, pa_ref, pb_ref, ya_ref, yb_ref, wo_ref, wa_ref, wb_ref,
             dg_ref, dya_ref, dyb_ref, dwo_ref, dwa_ref, dwb_ref, dbias_ref):
        @pl.when(pl.program_id(0) == 0)
        def _():
            for ref in (dwo_ref, dwa_ref, dwb_ref, dbias_ref):
                ref[...] = jnp.zeros_like(ref)

        dxb = dx_ref[...].astype(BF16)
        dm = _dot_nt(dxb, wo_ref[...])
        sa = _sigmoid(ga_ref[...])
        sb = _sigmoid(gb_ref[...])
        dga = dm * pa_ref[...] * (sa * (1.0 - sa))
        dgb = dm * pb_ref[...] * (sb * (1.0 - sb))
        dg_ref[:, :D] = dga.astype(BF16)
        dg_ref[:, D:] = dgb.astype(BF16)
        dbias_ref[:, :D] += jnp.sum(dga, axis=0, keepdims=True)
        dbias_ref[:, D:] += jnp.sum(dgb, axis=0, keepdims=True)
        dpa = (dm * sa).astype(BF16)
        dpb = (dm * sb).astype(BF16)
        ya = ya_ref[...]
        yb = yb_ref[...]
        dya = jnp.zeros((TM, W), F32)
        dyb = jnp.zeros((TM, W), F32)
        for j in range(4):
            cs = slice(256 * j, 256 * (j + 1))
            dya = dya + _dot_nt(dpa[:, cs], wa_ref[j])
            dyb = dyb + _dot_nt(dpb[:, cs], wb_ref[j])
            dwa_ref[j] += _dot_tn(ya, dpa[:, cs])
            dwb_ref[j] += _dot_tn(yb, dpb[:, cs])
        dya_ref[...] = dya
        dyb_ref[...] = dyb
        dwo_ref[...] += _dot_tn(mg_ref[...], dxb)

    return _pcall(
        body, f"merge_out_bwd_l{l}", (NT,),
        [_rows(D), _rows(D), _rows(D, 2), _rows(D, 3), _rows(D), _rows(D), _rows(W), _rows(W),
         _layer(l, (D, D)), _layer(l, (4, W, 256)), _layer(l, (4, W, 256))],
        [_rows(2 * D, 1), _rows(W), _rows(W), _full((D, D)), _full((4, W, 256)), _full((4, W, 256)), _full((1, 2 * D))],
        [SDS((L, NIN), BF16), SDS((L, W), F32), SDS((L, W), F32),
         SDS((D, D), F32), SDS((4, W, 256), F32), SDS((4, W, 256), F32), SDS((1, 2 * D), F32)],
        (dxn, mg, proj, proj, pa, pb, ya, yb, wo, wa, wb), jobs=jobs)


def _pool_bwd(l, dyb, proj, mixed, pooled, pw, scale, dproj, jobs=()):
    def body(dyb_ref, zb_ref, mixed_ref, pooled_ref, pw_ref, sc_ref, _, db_ref, dpw_ref, dsc_ref, dbias_ref, buf):
        i = pl.program_id(0)
        tile = NT - 1 - i

        @pl.when(i == 0)
        def _():
            dpw_ref[...] = jnp.zeros_like(dpw_ref)
            dsc_ref[...] = jnp.zeros_like(dsc_ref)
            dbias_ref[...] = jnp.zeros_like(dbias_ref)
            buf[TM:TM + 16, :] = jnp.zeros((16, W), F32)

        dyb = dyb_ref[...]
        zb = zb_ref[...]
        mixed = mixed_ref[...]
        sc = sc_ref[...]
        sg = _sigmoid(zb)
        dyb0 = dyb * (zb * sg)
        dzb = dyb * (mixed * sc) * (sg * (1.0 + zb * (1.0 - sg)))
        db_ref[:, W:] = dzb.astype(BF16)
        dbias_ref[:, W:] += jnp.sum(dzb, axis=0, keepdims=True)
        dsc_ref[...] += jnp.sum(dyb0 * mixed, axis=0, keepdims=True)
        dmix = (dyb0 * sc).astype(BF16)
        t = tile * TM + lax.broadcasted_iota(jnp.int32, (TM, 128), 0)
        for gi, win in enumerate(WINS):
            cs = slice(128 * gi, 128 * (gi + 1))
            dpw_ref[gi] += _dot_tn(pooled_ref[:, cs], dmix[:, cs])
            dpool = _dot_nt(dmix[:, cs], pw_ref[gi])
            cnt = jnp.minimum(t + 1, win).astype(F32)
            e = dpool / cnt
            buf[0:TM, cs] = e
            acc = e - dpool
            for k in range(1, win):
                acc = acc + buf[k:k + TM, cs]
            db_ref[:, cs] = acc.astype(BF16)
            dbias_ref[:, cs] += jnp.sum(acc, axis=0, keepdims=True)
        buf[TM:TM + 16, :] = buf[0:16, :]

    return _pcall(
        body, f"pool_bwd_l{l}", (NT,),
        [_rows(W, 0, True), _rows(W, 3, True), _rows(W, 0, True), _rows(W, 0, True),
         _full((4, 128, 128)), _full((1, W)), _ANY],
        [_rows(2 * W, 1, True), _full((4, 128, 128)), _full((1, W)), _full((1, 2 * W))],
        [SDS((L, NIN), BF16), SDS((4, 128, 128), F32), SDS((1, W), F32), SDS((1, 2 * W), F32)],
        (dyb, proj, mixed, pooled, pw, scale, dproj),
        scratch=[pltpu.VMEM((TM + 16, W), F32)], aliases={6: 0}, jobs=jobs)


def _s5_bwd(l, dya, proj, y1, q, sre, sim, cret, cimnt, bret, bimt, st_re, st_im, cr_re, cr_im, dsk, wglu, dproj, jobs=()):
    def halo(i):
        return (jnp.maximum((NT - 1 - i) * (TM // 8) - 1, 0), 0)

    def body(dya_ref, ua_ref, za_ref, y1_ref, q_ref, sre_ref, sim_ref, hre_ref, him_ref,
             cret_ref, cimnt_ref, bret_ref, bimt_ref, st_re_ref, st_im_ref, cr_re_ref, cr_im_ref, d_ref, wg_ref, _,
             da_ref, dwg_ref, dbg_ref, dd_ref, dcre_ref, dcimn_ref, dbre_ref, dbim_ref, dare_ref, daim_ref, dbias_ref,
             lre, lim, car_ref):
        i = pl.program_id(0)
        tile = NT - 1 - i

        @pl.when(i == 0)
        def _():
            for ref in (dwg_ref, dbg_ref, dd_ref, dcre_ref, dcimn_ref, dbre_ref, dbim_ref, dare_ref, daim_ref, dbias_ref,
                        car_ref):
                ref[...] = jnp.zeros_like(ref)

        u = ua_ref[...]
        za = za_ref[...]
        y1 = y1_ref[...]
        dya = dya_ref[...]
        y2 = _gelu(y1)
        sg = _sigmoid(q_ref[...])
        sgz = _sigmoid(za)
        dy3 = dya * (za * sgz)
        dza = dya * (y2 * sg) * (sgz * (1.0 + za * (1.0 - sgz)))
        da_ref[:, W:] = dza.astype(BF16)
        dbias_ref[:, W:] += jnp.sum(dza, axis=0, keepdims=True)
        dq = dy3 * y2 * (sg * (1.0 - sg))
        dqb = dq.astype(BF16)
        dy2 = dy3 * sg + _dot_nt(dqb, wg_ref[...])
        dwg_ref[...] += _dot_tn(y2.astype(BF16), dqb)
        dbg_ref[...] += jnp.sum(dq, axis=0, keepdims=True)
        dy1 = dy2 * _gelu_grad(y1)
        dd_ref[...] += jnp.sum(dy1 * u, axis=0, keepdims=True)
        dy1b = dy1.astype(BF16)
        ub = u.astype(BF16)
        for k in range(4):
            blk = slice(512 * k, 512 * (k + 1))
            ks = slice(128 * k, 128 * (k + 1))
            lre[:, blk] = _dot(dy1b[:, ks], cret_ref[k])
            lim[:, blk] = _dot(dy1b[:, ks], cimnt_ref[k])
            dcre_ref[k] += _dot_tn(sre_ref[:, blk].astype(BF16), dy1b[:, ks])
            dcimn_ref[k] += _dot_tn(sim_ref[:, blk].astype(BF16), dy1b[:, ks])
        carry = _scan_tile(lre, lim, st_re_ref, st_im_ref, cr_re_ref, cr_im_ref, _load_carry(car_ref), True)
        _store_carry(car_ref, carry)

        rowid = lax.broadcasted_iota(jnp.int32, (8, 512), 0)
        gate = (tile > 0).astype(F32)

        def chunk(c, _):
            rows = pl.ds(pl.multiple_of(c * 8, 8), 8)
            prows = pl.ds(pl.multiple_of(jnp.maximum(c - 1, 0) * 8, 8), 8)
            for lb in range(GP // 512):
                cols = slice(lb * 512, (lb + 1) * 512)
                sr = sre_ref[rows, cols]
                si = sim_ref[rows, cols]
                pr = jnp.where(c == 0, hre_ref[7:8, cols] * gate, sre_ref[prows, cols][7:8])
                pi = jnp.where(c == 0, him_ref[7:8, cols] * gate, sim_ref[prows, cols][7:8])
                sr = jnp.where(rowid == 0, pr, pltpu.roll(sr, 1, 0))
                si = jnp.where(rowid == 0, pi, pltpu.roll(si, 1, 0))
                lr = lre[rows, cols]
                li = lim[rows, cols]
                dare_ref[:, cols] += sr * lr + si * li
                daim_ref[:, cols] += sr * li - si * lr
            return 0

        lax.fori_loop(0, TM // 8, chunk, 0)

        for k in range(4):
            blk = slice(512 * k, 512 * (k + 1))
            ks = slice(128 * k, 128 * (k + 1))
            lrb = lre[:, blk].astype(BF16)
            lib = lim[:, blk].astype(BF16)
            du = dy1[:, ks] * d_ref[:, ks] + _dot(lrb, bret_ref[k]) + _dot(lib, bimt_ref[k])
            da_ref[:, ks] = du.astype(BF16)
            dbias_ref[:, ks] += jnp.sum(du, axis=0, keepdims=True)
            dbre_ref[k] += _dot_tn(ub[:, ks], lrb)
            dbim_ref[k] += _dot_tn(ub[:, ks], lib)

    return _pcall(
        body, f"s5_bwd_l{l}", (NT,),
        [_rows(W, 0, True), _rows(W, 0, True), _rows(W, 1, True), _rows(W, 0, True), _rows(W, 0, True),
         _rows(GP, 0, True), _rows(GP, 0, True),
         pl.BlockSpec((8, GP), halo), pl.BlockSpec((8, GP), halo),
         _full((4, 128, 512)), _full((4, 128, 512)), _full((4, 512, 128)), _full((4, 512, 128)),
         _full((24, GP)), _full((24, GP)), _full((8, GP)), _full((8, GP)), _full((1, W)), _layer(l, (W, W)), _ANY],
        [_rows(2 * W, 0, True), _full((W, W)), _full((1, W)), _full((1, W)),
         _full((4, 512, 128)), _full((4, 512, 128)), _full((4, 128, 512)), _full((4, 128, 512)),
         _full((8, GP)), _full((8, GP)), _full((1, 2 * W))],
        [SDS((L, NIN), BF16), SDS((W, W), F32), SDS((1, W), F32), SDS((1, W), F32),
         SDS((4, 512, 128), F32), SDS((4, 512, 128), F32), SDS((4, 128, 512), F32), SDS((4, 128, 512), F32),
         SDS((8, GP), F32), SDS((8, GP), F32), SDS((1, 2 * W), F32)],
        (dya, proj, proj, y1, q, sre, sim, sre, sim, cret, cimnt, bret, bimt, st_re, st_im, cr_re, cr_im, dsk, wglu,
         dproj),
        scratch=[pltpu.VMEM((TM, GP), F32), pltpu.VMEM((TM, GP), F32), pltpu.VMEM((8, GP), F32)],
        aliases={19: 0}, jobs=jobs)


def _inproj_dw(l, r, h, dproj, jobs=()):
    def body(h_ref, dp_ref, dw_ref):
        part = _dot_tn(h_ref[...], dp_ref[...])

        @pl.when(pl.program_id(1) == 0)
        def _():
            dw_ref[...] = part

        @pl.when(pl.program_id(1) > 0)
        def _():
            dw_ref[...] += part

    return _pcall(
        body, f"inproj_dw{r}_l{l}", (4, L // TK),
        [pl.BlockSpec((TK, D // 2), lambda j, i: (i, r)), pl.BlockSpec((TK, 1024), lambda j, i: (i, j))],
        [pl.BlockSpec((None, D // 2, 1024), lambda j, i: (j, 0, 0))],
        [SDS((4, D // 2, 1024), F32)],
        (h, dproj), jobs=jobs)


def _inproj_dx(l, dproj, w, x, g, dxn, jobs=()):
    def body(dp_ref, w_ref, x_ref, g_ref, dxn_ref, dx_ref, dg_ref):
        @pl.when(pl.program_id(0) == 0)
        def _():
            dg_ref[...] = jnp.zeros_like(dg_ref)

        dh = jnp.zeros((TM, D), F32)
        for j in range(4):
            dh = dh + _dot_nt(dp_ref[:, j * 1024:(j + 1) * 1024], w_ref[j])
        xv = x_ref[...]
        r = lax.rsqrt(jnp.mean(xv * xv, axis=-1, keepdims=True) + EPS)
        xn = xv * r
        dg_ref[...] += jnp.sum(dh * xn, axis=0, keepdims=True)
        dn = dh * g_ref[...]
        dx_ref[...] = dxn_ref[...] + r * (dn - xn * jnp.mean(dn * xn, axis=-1, keepdims=True))

    return _pcall(
        body, f"inproj_dx_l{l}", (NT,),
        [_rows(NIN), _layer(l, (4, D, 1024)), _rows(D), _full((1, D)), _rows(D)],
        [_rows(D), _full((1, D))],
        [SDS((L, D), F32), SDS((1, D), F32)],
        (dproj, w, x, g, dxn), jobs=jobs)


def _discretize(log_dt, lam_re, lam_im, b_re, b_im):
    dt = jnp.exp(log_dt)[:, None]
    mag = jnp.exp(lam_re * dt)
    ang = lam_im * dt
    abar_re = mag * jnp.cos(ang)
    abar_im = mag * jnp.sin(ang)
    num_re = abar_re - 1.0
    num_im = abar_im
    den = lam_re * lam_re + lam_im * lam_im
    coef_re = (num_re * lam_re + num_im * lam_im) / den
    coef_im = (num_im * lam_re - num_re * lam_im) / den
    bbar_re = coef_re[..., None] * b_re - coef_im[..., None] * b_im
    bbar_im = coef_re[..., None] * b_im + coef_im[..., None] * b_re
    return abar_re, abar_im, bbar_re, bbar_im


def _powers(abar_re, abar_im):
    ar, ai = abar_re.reshape(1, GP), abar_im.reshape(1, GP)
    rows_re, rows_im = [ar], [ai]
    for _ in range(7):
        pr, pi = rows_re[-1], rows_im[-1]
        rows_re.append(pr * ar - pi * ai)
        rows_im.append(pr * ai + pi * ar)
    row = jnp.arange(8)[:, None]

    def steps(rows, keep):
        return jnp.concatenate([jnp.where(keep(d), rows[d - 1], 0.0) for d in (1, 2, 4)], axis=0)

    neg_im = [-r for r in rows_im]
    fwd = (steps(rows_re, lambda d: row >= d), steps(rows_im, lambda d: row >= d),
           jnp.concatenate(rows_re, axis=0), jnp.concatenate(rows_im, axis=0))
    rev = (steps(rows_re, lambda d: row < 8 - d), steps(neg_im, lambda d: row < 8 - d),
           jnp.concatenate(rows_re[::-1], axis=0), jnp.concatenate(neg_im[::-1], axis=0))
    return fwd, rev


_EYE8 = functools.partial(jnp.eye, 8, dtype=F32)


def _expand_in(b):
    return jnp.einsum("kgpc,gh->kgchp", b.reshape(4, 8, P, C), _EYE8()).reshape(4, 128, 512)


def _extract_in(e):
    return jnp.einsum("kgchp,gh->kgpc", e.reshape(4, 8, C, 8, P), _EYE8()).reshape(G, P, C)


def _expand_out(c):
    return jnp.einsum("kgcp,gh->kgphc", c.reshape(4, 8, C, P), _EYE8()).reshape(4, 512, 128)


def _extract_out(e):
    return jnp.einsum("kgphc,gh->kgcp", e.reshape(4, 8, P, 8, C), _EYE8()).reshape(G, C, P)


SMALL = ("norm_g", "b_in", "ssm_log_dt", "ssm_lam_re", "ssm_lam_im", "ssm_b_re", "ssm_b_im",
         "ssm_c_re", "ssm_c_im", "ssm_d", "ssm_b_glu", "pool_w", "pool_scale")
BIG = ("w_in", "ssm_w_glu", "w_branch_a", "w_branch_b", "w_out")


def _step(x, target, w, m, v, place):
    sp = {n: w[n] for n in SMALL}
    final_norm_g = w["final_norm_g"]
    wbuf = dict(zip(BIG, _cast_own(place, [w[n] for n in BIG])))
    saved = []
    for l in range(DEPTH):
        disc_in = (sp["ssm_log_dt"][l], sp["ssm_lam_re"][l], sp["ssm_lam_im"][l], sp["ssm_b_re"][l], sp["ssm_b_im"][l])
        (abar_re, abar_im, bbar_re, bbar_im), disc_vjp = jax.vjp(_discretize, *disc_in)
        powers_fwd, powers_rev = _powers(abar_re, abar_im)
        b_re_x, b_im_x = _expand_in(bbar_re), _expand_in(bbar_im)
        c_re_x, c_imn_x = _expand_out(sp["ssm_c_re"][l]), _expand_out(-sp["ssm_c_im"][l])
        g = sp["norm_g"][l].reshape(1, D)
        dsk = sp["ssm_d"][l].reshape(1, W)
        scale = sp["pool_scale"][l].reshape(1, W)
        pw = sp["pool_w"][l].astype(BF16)

        if l == 0:
            (wbuf["w_in"],) = _comm_only("gather_w_in_l0", _GatherJob([wbuf["w_in"]], 0))[0]
        (h, proj), res = _norm_inproj(l, x, g, wbuf["w_in"], sp["b_in"][l].reshape(1, NIN),
                                      _GatherJob([wbuf[n] for n in BIG[1:]], l))
        wbuf.update(zip(BIG[1:], res[0][0]))
        wg = dict(wbuf, ssm_w_glu=wbuf["ssm_w_glu"].reshape(DEPTH, W, W), w_out=wbuf["w_out"].reshape(DEPTH, D, D))
        (sre, sim, y1, q, ya), res = _s5_fwd(
            l, proj, jnp.concatenate([b_re_x, b_im_x], axis=2).astype(BF16), c_re_x.astype(BF16), c_imn_x.astype(BF16),
            powers_fwd, dsk, wg["ssm_w_glu"], sp["ssm_b_glu"][l].reshape(1, W),
            _GatherJob([wbuf["w_in"]], l + 1) if l + 1 < DEPTH else None)
        if res:
            (wbuf["w_in"],) = res[0][0]
            wg["w_in"] = wbuf["w_in"]
        pooled, mixed, yb = _pool_fwd(l, proj, pw, scale)
        pa, pb, mg, x_next = _merge_out(l, ya, yb, proj, x, wg["w_branch_a"], wg["w_branch_b"], wg["w_out"])
        saved.append(dict(x=x, g=g, dsk=dsk, scale=scale, pw=pw, h=h, proj=proj, sre=sre, sim=sim, y1=y1, q=q, ya=ya,
                          pooled=pooled, mixed=mixed, yb=yb, pa=pa, pb=pb, mg=mg, disc_vjp=disc_vjp,
                          powers=powers_rev,
                          b_re_x=b_re_x, b_im_x=b_im_x, c_re_x=c_re_x, c_imn_x=c_imn_x))
        x = x_next

    loss, dx, dgf = _loss_head(x, final_norm_g.reshape(1, D), target)

    gs = {n: [None] * DEPTH for n in SMALL}
    red = _Reducer(place, w, m, v)
    for l in reversed(range(DEPTH)):
        s = saved[l]
        (dproj, dya, dyb, dwo, dwa, dwb, dbias_g), res = _merge_out_bwd(
            l, dx, s["mg"], s["proj"], s["pa"], s["pb"], s["ya"], s["yb"],
            wg["w_out"], wg["w_branch_a"], wg["w_branch_b"], red.jobs())
        red.land(res)
        (dproj, dpw, dsc, dbias_b), res = _pool_bwd(l, dyb, s["proj"], s["mixed"], s["pooled"], s["pw"], s["scale"], dproj,
                                          red.jobs())
        red.land(res)
        t = lambda a: jnp.swapaxes(a, 1, 2).astype(BF16)
        (dproj, dwg, dbg, dd, dcre, dcimn, dbre, dbim, dare, daim, dbias_a), res = _s5_bwd(
            l, dya, s["proj"], s["y1"], s["q"], s["sre"], s["sim"],
            t(s["c_re_x"]), t(s["c_imn_x"]), t(s["b_re_x"]), t(s["b_im_x"]),
            *s["powers"], s["dsk"], wg["ssm_w_glu"], dproj, red.jobs())
        red.land(res)
        red.add(l, "rest", BIG[1:], [dwg.reshape(4, W // 4, W), dwa, dwb, dwo.reshape(4, D // 4, D)])
        for r in range(2):
            outs, res = _inproj_dw(l, r, s["h"], dproj, red.jobs())
            red.land(res)
            red.add(l, f"in{r}", BIG[:1], outs[:1], r * (D // 2))
        (dx, dg), res = _inproj_dx(l, dproj, wg["w_in"], s["x"], s["g"], dx, red.jobs())
        red.land(res)

        d_abar_re = jnp.sum(dare, axis=0).reshape(G, P)
        d_abar_im = jnp.sum(daim, axis=0).reshape(G, P)
        dlog_dt, dlam_re, dlam_im, db_re, db_im = s["disc_vjp"](
            (d_abar_re, d_abar_im, _extract_in(dbre), _extract_in(dbim)))
        gs["norm_g"][l] = dg.reshape(D)
        gs["b_in"][l] = jnp.concatenate([dbias_a, dbias_b, dbias_g], axis=1).reshape(NIN)
        gs["ssm_log_dt"][l] = dlog_dt
        gs["ssm_lam_re"][l] = dlam_re
        gs["ssm_lam_im"][l] = dlam_im
        gs["ssm_b_re"][l] = db_re
        gs["ssm_b_im"][l] = db_im
        gs["ssm_c_re"][l] = _extract_out(dcre)
        gs["ssm_c_im"][l] = -_extract_out(dcimn)
        gs["ssm_d"][l] = dd.reshape(W)
        gs["ssm_b_glu"][l] = dbg.reshape(W)
        gs["pool_w"][l] = dpw
        gs["pool_scale"][l] = dsc.reshape(W)
    gs = {n: jnp.stack(v) for n, v in gs.items()}
    gs["final_norm_g"] = dgf

    natural = {n: w[n].shape for n in REPLICATED}
    rw, rm, rv = {}, {}, {}
    for n in REPLICATED:
        shape = DENSE.get(n, natural[n])
        shape = shape if len(shape) > 2 else (shape[0], 1, shape[1])
        gs[n], rw[n], rm[n], rv[n] = (a.reshape(shape) for a in (gs[n], w[n], m[n], v[n]))
    small = [gs[n] for n in REPLICATED] + [loss.reshape(2, 1, 128)]
    jobs = red.jobs()
    res = _pcall(None, "tail_exchange", (), [], [], [], [], jobs=jobs + [_SiblingJob(small, False)])[1]
    red.land(res[:len(jobs)])
    pair_small = _small_pair_sum(place, small, res[-1][1], [BF16] * len(SMALL) + [F32, F32])
    jobs = red.jobs()
    res = _pcall(None, "tail_gather", (), [], [], [], [], jobs=jobs + [_ChipGatherJob(pair_small)])[1]
    red.land(res[:len(jobs)])
    assert not red.active
    small_parts = dict(zip(REPLICATED + ("loss",), res[-1][0]))

    k = len(REPLICATED)
    outs = _adamw_small("adamw_small", [rw[n] for n in REPLICATED], [small_parts[n] for n in REPLICATED],
                        [rm[n] for n in REPLICATED], [rv[n] for n in REPLICATED], small_parts["loss"])
    results = {n: red.big[n] for n in BIG}
    results.update({n: [outs[1 + q * k + i].reshape(natural[n]) for q in range(4)] for i, n in enumerate(REPLICATED)})
    return outs[0][0, 0, 0], dx, results


def _place():
    x, y, c = lax.axis_index("x"), lax.axis_index("y"), lax.axis_index("c")
    chips = [(1 - x, y), (x, 1 - y), (1 - x, 1 - y)]
    return x, y, c, 2 * x + y, chips, [2 * cx + cy for cx, cy in chips]


def _remote(src, dst, ssem, rsem, dev):
    return pltpu.make_async_remote_copy(src_ref=src, dst_ref=dst, send_sem=ssem, recv_sem=rsem,
                                        device_id=dev, device_id_type=MESH)


class _GatherJob:
    def __init__(self, bufs, l):
        self.srcs, self.bufs, self.news, self.l = [], list(bufs), [], l
        self.scratch = [pltpu.SemaphoreType.DMA((len(self.bufs), 3))] * 4

    def _half(self, ref, k, h):
        rows = ref.shape[2] // 2
        return ref.at[self.l, k, pl.ds(pl.multiple_of(h * rows, 8), rows), :]

    def _ici(self, bufs, sems, a, j, k):
        _, _, c, _, chips, _ = _place()
        blk = self._half(bufs[a], k, c)
        return _remote(blk, blk, sems[0].at[a, j], sems[1].at[a, j], (*chips[j], c))

    def _d2d(self, bufs, sems, a, j, k, h):
        x, y, c, _, _, _ = _place()
        blk = self._half(bufs[a], k, h)
        return _remote(blk, blk, sems[2].at[a, j], sems[3].at[a, j], (x, y, 1 - c))

    def start(self, srcs, bufs, news, sems):
        me = _place()[3]
        for a in range(len(self.bufs)):
            for j in range(3):
                self._ici(bufs, sems, a, j, me).start()

    def finish(self, srcs, bufs, news, sems):
        _, _, c, me, _, cid = _place()
        pairs = [(a, j) for a in range(len(self.bufs)) for j in range(3)]
        for a, j in pairs:
            self._ici(bufs, sems, a, j, cid[j]).wait_recv()
            self._d2d(bufs, sems, a, j, cid[j], c).start()
        for a, j in pairs:
            self._d2d(bufs, sems, a, j, cid[j], 1 - c).wait_recv()
        for a, j in pairs:
            self._ici(bufs, sems, a, j, me).wait_send()
            self._d2d(bufs, sems, a, j, cid[j], c).wait_send()


class _SiblingJob:
    def __init__(self, srcs, rows_half):
        self.srcs, self.bufs, self.rows_half = list(srcs), [], rows_half
        self.news = [SDS((s.shape[0], s.shape[1] // 2, s.shape[2]) if rows_half else s.shape, s.dtype) for s in srcs]
        self.scratch = [pltpu.SemaphoreType.DMA((len(self.srcs),))] * 2

    def _copy(self, srcs, news, sems, a):
        x, y, c, _, _, _ = _place()
        src = srcs[a]
        if self.rows_half:
            rows = src.shape[1] // 2
            src = src.at[:, pl.ds(pl.multiple_of((1 - c) * rows, 8), rows), :]
        return _remote(src, news[a], sems[0].at[a], sems[1].at[a], (x, y, 1 - c))

    def start(self, srcs, bufs, news, sems):
        for a in range(len(self.srcs)):
            self._copy(srcs, news, sems, a).start()

    def finish(self, srcs, bufs, news, sems):
        for a in range(len(self.srcs)):
            self._copy(srcs, news, sems, a).wait()


class _ScatterJob:
    def __init__(self, parts):
        self.srcs, self.bufs = list(parts), []
        self.news = [SDS((3,) + p.shape[1:], p.dtype) for p in parts]
        self.scratch = [pltpu.SemaphoreType.DMA((len(self.srcs), 3))] * 2

    def _copy(self, srcs, news, sems, a, j):
        _, _, c, _, chips, cid = _place()
        return _remote(srcs[a].at[cid[j]], news[a].at[j], sems[0].at[a, j], sems[1].at[a, j], (*chips[j], c))

    def start(self, srcs, bufs, news, sems):
        for a in range(len(self.srcs)):
            for j in range(3):
                self._copy(srcs, news, sems, a, j).start()

    def finish(self, srcs, bufs, news, sems):
        for a in range(len(self.srcs)):
            for j in range(3):
                self._copy(srcs, news, sems, a, j).wait()


def _comm_only(name, job):
    return _pcall(None, name, (), [], [], [], [], jobs=[job])[1][0]


class _ChipGatherJob(_GatherJob):
    def __init__(self, bufs):
        super().__init__(bufs, None)

    def _half(self, ref, k, h):
        return ref.at[k, h]


def _cast_own(place, ws):
    n = len(ws)

    def body(p_ref, *refs):
        for i_ref, o_ref in zip(refs[:n], refs[n:]):
            o_ref[...] = i_ref[...].astype(BF16)

    return pl.pallas_call(
        body, name="cast_own_shards",
        grid_spec=pltpu.PrefetchScalarGridSpec(
            num_scalar_prefetch=1, grid=(DEPTH,),
            in_specs=[pl.BlockSpec((None,) + a.shape[1:], lambda l, p: (l, 0, 0)) for a in ws],
            out_specs=[pl.BlockSpec((None, None) + a.shape[1:], lambda l, p: (l, p[1], 0, 0)) for a in ws]),
        out_shape=[SDS((DEPTH, 4) + a.shape[1:], BF16) for a in ws],
        compiler_params=_params("arbitrary"),
    )(place, *ws)


def _half_tiles(a_):
    rows = a_ // 2
    ta = min(rows, 256)
    return rows, ta, rows // ta


def _pair_sum_bf16(name, place, own, recv):
    _, a_, b_ = own.shape
    rows, ta, nh = _half_tiles(a_)

    def body(p_ref, own_ref, recv_ref, out_ref):
        out_ref[...] = (own_ref[...] + recv_ref[...]).astype(BF16)

    return pl.pallas_call(
        body, name=name,
        grid_spec=pltpu.PrefetchScalarGridSpec(
            num_scalar_prefetch=1, grid=(4, nh),
            in_specs=[pl.BlockSpec((None, ta, b_), lambda s, i, p: (s, p[0] * nh + i, 0)),
                      pl.BlockSpec((None, ta, b_), lambda s, i, p: (s, i, 0))],
            out_specs=pl.BlockSpec((None, ta, b_), lambda s, i, p: (s, i, 0))),
        out_shape=SDS((4, rows, b_), BF16),
        compiler_params=_params("arbitrary", "arbitrary"),
    )(place, own, recv)


def _shard_sum(name, place, own, recv, rbuf):
    _, a_, b_ = own.shape
    rows, ta, nh = _half_tiles(a_)

    def body(p_ref, own_ref, recv_ref, r_ref, out_ref):
        acc = own_ref[...] + recv_ref[...]
        for j in range(3):
            acc = acc + r_ref[j].astype(F32)
        out_ref[...] = acc

    return pl.pallas_call(
        body, name=name,
        grid_spec=pltpu.PrefetchScalarGridSpec(
            num_scalar_prefetch=1, grid=(nh,),
            in_specs=[pl.BlockSpec((None, ta, b_), lambda i, p: (p[1], p[0] * nh + i, 0)),
                      pl.BlockSpec((None, ta, b_), lambda i, p: (p[1], i, 0)),
                      pl.BlockSpec((3, ta, b_), lambda i, p: (0, i, 0))],
            out_specs=pl.BlockSpec((ta, b_), lambda i, p: (i, 0))),
        out_shape=SDS((rows, b_), F32),
        compiler_params=_params("arbitrary"),
    )(place, own, recv, rbuf)


def _small_pair_sum(place, mine, recv, dtypes):
    n = len(mine)

    def body(p_ref, *refs):
        for m_ref, r_ref, o_ref in zip(refs[:n], refs[n:2 * n], refs[2 * n:]):
            o_ref[...] = (m_ref[...] + r_ref[...]).astype(o_ref.dtype)

    def whole(a):
        zeros = (0,) * a.ndim
        return pl.BlockSpec(a.shape, lambda i, p: zeros)

    def mine_blk(a):
        zeros = (0,) * a.ndim
        return pl.BlockSpec((None,) + a.shape, lambda i, p: (p[1],) + zeros)

    return pl.pallas_call(
        body, name="small_pair_sum",
        grid_spec=pltpu.PrefetchScalarGridSpec(
            num_scalar_prefetch=1, grid=(1,),
            in_specs=[whole(a) for a in mine] + [whole(a) for a in recv],
            out_specs=[mine_blk(a) for a in mine]),
        out_shape=[SDS((4,) + a.shape, dt) for a, dt in zip(mine, dtypes)],
        compiler_params=_params("arbitrary"),
    )(place, *mine, *recv)


def _adam_math(w, g, m, v):
    m = B1 * m + (1.0 - B1) * g
    v = B2 * v + (1.0 - B2) * (g * g)
    m_hat = m / (1.0 - B1 ** STEP)
    v_hat = v / (1.0 - B2 ** STEP)
    delta = -LR * (m_hat / (jnp.sqrt(v_hat) + EPS_A) + WD * w)
    return delta, m, v


def _adamw_big(name, l, row0, w, m, v, mine, other, prev, jobs=()):
    _, _, b_ = w.shape
    _, ta, nh = _half_tiles(2 * mine.shape[0])
    prev = list(prev or [])

    def body(w_ref, m_ref, v_ref, mine_ref, other_ref, *rest):
        g_ref, d_ref, mo_ref, vo_ref = rest[len(prev):]
        g = jnp.where(pl.program_id(0) == lax.axis_index("c"), mine_ref[...], other_ref[...])
        g_ref[...] = g
        d_ref[...], mo_ref[...], vo_ref[...] = _adam_math(w_ref[...], g, m_ref[...], v_ref[...])

    slab = pl.BlockSpec((None, ta, b_), lambda h, i: (l, row0 // ta + h * nh + i, 0))
    half = pl.BlockSpec((ta, b_), lambda h, i: (i, 0))
    outs, res = _pcall(
        body, name, (2, nh), [slab, slab, slab, half, half] + [_ANY] * len(prev), [slab] * 4, [SDS(w.shape, F32)] * 4,
        (w, m, v, mine, other, *prev), aliases={5 + k: k for k in range(len(prev))}, jobs=jobs)
    return outs, res


def _adamw_small(name, ws, parts, ms, vs, loss_parts=None):
    k = len(ws)
    extra = [] if loss_parts is None else [loss_parts]

    def chip_sum(p_ref):
        p = [p_ref[k].astype(F32) for k in range(4)]
        return ((p[0] + p[1]) + p[2]) + p[3]

    def body(*refs):
        w_refs, p_refs, m_refs, v_refs = refs[:k], refs[k:2 * k], refs[2 * k:3 * k], refs[3 * k:4 * k]
        outs = refs[4 * k + len(extra):]
        if extra:
            outs[0][...] = chip_sum(refs[4 * k])
            outs = outs[1:]
        for a in range(k):
            g = chip_sum(p_refs[a])
            outs[a][...] = g
            outs[k + a][...], outs[2 * k + a][...], outs[3 * k + a][...] = _adam_math(
                w_refs[a][...], g, m_refs[a][...], v_refs[a][...])

    like = [SDS(a.shape, F32) for a in ws]
    return pl.pallas_call(
        body, name=name,
        out_shape=([SDS(loss_parts.shape[1:], F32)] if extra else []) + like * 4,
        compiler_params=pltpu.CompilerParams(vmem_limit_bytes=VMEM_LIMIT),
    )(*ws, *parts, *ms, *vs, *extra)


class _Reducer:
    def __init__(self, place, w, m, v):
        self.place, self.w, self.m, self.v = place, w, m, v
        self.active, self.riding = [], []
        self.big = {n: None for n in BIG}

    def add(self, l, tag, names, own, row0=0):
        self.active.append(dict(l=l, key=f"{tag}_l{l}", names=names, own=list(own), row0=row0, stage=0))

    def jobs(self):
        self.riding = list(self.active)
        return [(_SiblingJob(g["own"], True), _ScatterJob(g.get("parts", [])), _SiblingJob(g.get("shard", []), False))
                [g["stage"]] for g in self.riding]

    def land(self, res):
        for g, (_, news) in zip(self.riding, res):
            if g["stage"] == 0:
                g["recv"] = news
                g["parts"] = [_pair_sum_bf16(f"pair_sum_{n}_{g['key']}", self.place, o, r)
                              for n, o, r in zip(g["names"], g["own"], news)]
            elif g["stage"] == 1:
                g["shard"] = [_shard_sum(f"shard_sum_{n}_{g['key']}", self.place, o, r, rb)
                              for n, o, r, rb in zip(g["names"], g["own"], g["recv"], news)]
            else:
                for n, mine, other in zip(g["names"], g["shard"], news):
                    self.big[n] = _adamw_big(f"adamw_{n}_{g['key']}", g["l"], g["row0"], self.w[n], self.m[n], self.v[n],
                                             mine, other, self.big[n])[0]
                self.active.remove(g)
            g["stage"] += 1
        self.riding = []


WEIGHTS = ("norm_g", "w_in", "b_in", "ssm_log_dt", "ssm_lam_re", "ssm_lam_im", "ssm_b_re", "ssm_b_im", "ssm_c_re",
           "ssm_c_im", "ssm_d", "ssm_w_glu", "ssm_b_glu", "pool_w", "pool_scale", "w_branch_a", "w_branch_b", "w_out",
           "final_norm_g")
REPLICATED = SMALL + ("final_norm_g",)
DENSE = {"ssm_b_re": (DEPTH, G, P * C), "ssm_b_im": (DEPTH, G, P * C), "final_norm_g": (2, D // 2)}


def kernel(x, norm_g, w_in, b_in, ssm_log_dt, ssm_lam_re, ssm_lam_im, ssm_b_re, ssm_b_im, ssm_c_re, ssm_c_im, ssm_d, ssm_w_glu, ssm_b_glu, pool_w, pool_scale, w_branch_a, w_branch_b, w_out, final_norm_g, loss_target, m_norm_g, m_w_in, m_b_in, m_ssm_log_dt, m_ssm_lam_re, m_ssm_lam_im, m_ssm_b_re, m_ssm_b_im, m_ssm_c_re, m_ssm_c_im, m_ssm_d, m_ssm_w_glu, m_ssm_b_glu, m_pool_w, m_pool_scale, m_w_branch_a, m_w_branch_b, m_w_out, m_final_norm_g, v_norm_g, v_w_in, v_b_in, v_ssm_log_dt, v_ssm_lam_re, v_ssm_lam_im, v_ssm_b_re, v_ssm_b_im, v_ssm_c_re, v_ssm_c_im, v_ssm_d, v_ssm_w_glu, v_ssm_b_glu, v_pool_w, v_pool_scale, v_w_branch_a, v_w_branch_b, v_w_out, v_final_norm_g):
    w = dict(zip(WEIGHTS, (norm_g, w_in, b_in, ssm_log_dt, ssm_lam_re, ssm_lam_im, ssm_b_re, ssm_b_im, ssm_c_re,
                           ssm_c_im, ssm_d, ssm_w_glu, ssm_b_glu, pool_w, pool_scale, w_branch_a, w_branch_b, w_out,
                           final_norm_g)))
    m = dict(zip(WEIGHTS, (m_norm_g, m_w_in, m_b_in, m_ssm_log_dt, m_ssm_lam_re, m_ssm_lam_im, m_ssm_b_re, m_ssm_b_im,
                           m_ssm_c_re, m_ssm_c_im, m_ssm_d, m_ssm_w_glu, m_ssm_b_glu, m_pool_w, m_pool_scale,
                           m_w_branch_a, m_w_branch_b, m_w_out, m_final_norm_g)))
    v = dict(zip(WEIGHTS, (v_norm_g, v_w_in, v_b_in, v_ssm_log_dt, v_ssm_lam_re, v_ssm_lam_im, v_ssm_b_re, v_ssm_b_im,
                           v_ssm_c_re, v_ssm_c_im, v_ssm_d, v_ssm_w_glu, v_ssm_b_glu, v_pool_w, v_pool_scale,
                           v_w_branch_a, v_w_branch_b, v_w_out, v_final_norm_g)))
    place = jnp.stack([lax.axis_index("c"), 2 * lax.axis_index("x") + lax.axis_index("y")]).astype(jnp.int32)

    total_loss, dx, results = _step(x[0], loss_target[0], w, m, v, place)
    return (total_loss, dx[None], *[results[n][q] for q in range(4) for n in WEIGHTS])
```

```python
import functools

import jax
import jax.numpy as jnp
from jax import lax
from jax.experimental import pallas as pl
from jax.experimental.pallas import tpu as pltpu

F32, BF16 = jnp.float32, jnp.bfloat16
SDS = jax.ShapeDtypeStruct
MESH = pl.DeviceIdType.MESH

DEPTH = 2
L = 2048
D = 1024
NIN = 4096
W = 512
G, P, C = 32, 64, 16
GP = G * P
WINS = (2, 4, 8, 16)
TM = 256
NT = L // TM
TMM = 512
TK = 1024
EPS = 1e-6
VMEM_LIMIT = 56 * 2**20

LR, B1, B2, EPS_A, WD, STEP = 0.001, 0.9, 0.999, 1e-08, 0.01, 10


def _params(*sem):
    return pltpu.CompilerParams(dimension_semantics=sem, vmem_limit_bytes=VMEM_LIMIT)


_ANY = pl.BlockSpec(memory_space=pl.ANY)


def _full(shape):
    zeros = (0,) * len(shape)
    return pl.BlockSpec(shape, lambda *_: zeros)


def _layer(l, shape):
    zeros = (0,) * len(shape)
    return pl.BlockSpec((None,) + shape, lambda *_: (l,) + zeros)


def _rows(width, col=0, reverse=False, tm=TM):
    if reverse:
        return pl.BlockSpec((tm, width), lambda i: (L // tm - 1 - i, col))
    return pl.BlockSpec((tm, width), lambda i: (i, col))


def _rows_mm(width, col=0):
    return _rows(width, col, False, TMM)


def _pcall(body, name, grid, in_specs, out_specs, out_shape, args, scratch=(), aliases=None, jobs=()):
    in_specs, out_specs, out_shape, args, scratch = list(in_specs), list(out_specs), list(out_shape), list(args), list(scratch)
    aliases = dict(aliases or {})
    jobs = [j for j in jobs if j is not None]
    n_in, n_out, n_scr = len(in_specs), len(out_specs), len(scratch)
    srcs = [s for j in jobs for s in j.srcs]
    bufs = [b for j in jobs for b in j.bufs]
    news = [s for j in jobs for s in j.news]
    aliases.update({n_in + len(srcs) + k: n_out + k for k in range(len(bufs))})

    def hosted(*refs):
        cuts = [n_in, len(srcs), len(bufs), n_out, len(bufs), len(news), n_scr]
        parts, p = [], 0
        for n in cuts:
            parts.append(refs[p:p + n])
            p += n
        ins, src_r, _, outs, buf_r, new_r, scr = parts
        sem_r = refs[p:]
        views, ps, pb, pn, pm = [], 0, 0, 0, 0
        for j in jobs:
            views.append((src_r[ps:ps + len(j.srcs)], buf_r[pb:pb + len(j.bufs)], new_r[pn:pn + len(j.news)],
                          sem_r[pm:pm + len(j.scratch)]))
            ps, pb, pn, pm = ps + len(j.srcs), pb + len(j.bufs), pn + len(j.news), pm + len(j.scratch)

        def run(phase):
            for j, v in zip(jobs, views):
                getattr(j, phase)(*v)

        def at_step(step):
            return functools.reduce(jnp.logical_and, [pl.program_id(d) == step(d) for d in range(len(grid))])

        if not grid:
            run("start")
            run("finish")
            return
        pl.when(at_step(lambda d: 0))(lambda: run("start"))
        body(*ins, *outs, *scr)
        pl.when(at_step(lambda d: grid[d] - 1))(lambda: run("finish"))

    outs = pl.pallas_call(
        hosted if jobs else body, name=name, **({"grid": grid} if grid else {}),
        in_specs=in_specs + [_ANY] * (len(srcs) + len(bufs)), out_specs=out_specs + [_ANY] * (len(bufs) + len(news)),
        out_shape=out_shape + [SDS(b.shape, b.dtype) for b in bufs] + news,
        input_output_aliases=aliases, scratch_shapes=scratch + [s for j in jobs for s in j.scratch],
        compiler_params=_params(*(("arbitrary",) * len(grid))))(*args, *srcs, *bufs)
    res, pb, pn = [], n_out, n_out + len(bufs)
    for j in jobs:
        res.append((list(outs[pb:pb + len(j.bufs)]), list(outs[pn:pn + len(j.news)])))
        pb, pn = pb + len(j.bufs), pn + len(j.news)
    return list(outs[:n_out]), res


def _dot(a, b):
    return jnp.dot(a, b, preferred_element_type=F32)


def _dot_nt(a, b):
    return lax.dot_general(a, b, (((1,), (1,)), ((), ())), preferred_element_type=F32)


def _dot_tn(a, b):
    return lax.dot_general(a, b, (((0,), (0,)), ((), ())), preferred_element_type=F32)


_K0 = 0.7978845608028654
_K1 = 0.044715


def _gelu(x):
    return 0.5 * x * (1.0 + jnp.tanh(_K0 * (x + _K1 * (x * x * x))))


def _gelu_grad(x):
    t = jnp.tanh(_K0 * (x + _K1 * (x * x * x)))
    return 0.5 * (1.0 + t) + 0.5 * x * (1.0 - t * t) * (_K0 * (1.0 + 3.0 * _K1 * x * x))


def _sigmoid(x):
    return jax.nn.sigmoid(x)


def _norm_inproj(l, x, g, w, b, job=None):
    def body(x_ref, g_ref, w_ref, b_ref, h_ref, proj_ref):
        xv = x_ref[...]
        r = lax.rsqrt(jnp.mean(xv * xv, axis=-1, keepdims=True) + EPS)
        hb = ((xv * r) * g_ref[...]).astype(BF16)
        h_ref[...] = hb
        for j in range(4):
            cs = slice(j * 1024, (j + 1) * 1024)
            proj_ref[:, cs] = _dot(hb, w_ref[j]) + b_ref[:, cs]

    return _pcall(
        body, f"norm_inproj_l{l}", (L // TMM,),
        [_rows_mm(D), _layer(l, (1, D)), _layer(l, (4, D, 1024)), _layer(l, (1, NIN))],
        [_rows_mm(D), _rows_mm(NIN)],
        [SDS((L, D), BF16), SDS((L, NIN), F32)],
        (x, g, w, b), jobs=[job])


def _scan_tile(re_ref, im_ref, st_re, st_im, cr_re, cr_im, carry, reverse):
    def chunk(ci, carry):
        c = (TM // 8 - 1 - ci) if reverse else ci
        rows = pl.ds(pl.multiple_of(c * 8, 8), 8)
        new = []
        for lb in range(GP // 512):
            cols = slice(lb * 512, (lb + 1) * 512)
            vr = re_ref[rows, cols]
            vi = im_ref[rows, cols]
            for s, d in enumerate((1, 2, 4)):
                ar = st_re[8 * s:8 * s + 8, cols]
                ai = st_im[8 * s:8 * s + 8, cols]
                sr = pltpu.roll(vr, 8 - d if reverse else d, 0)
                si = pltpu.roll(vi, 8 - d if reverse else d, 0)
                vr, vi = vr + ar * sr - ai * si, vi + ar * si + ai * sr
            cr, ci_ = carry[2 * lb], carry[2 * lb + 1]
            pr = cr_re[:, cols]
            pi = cr_im[:, cols]
            vr, vi = vr + pr * cr - pi * ci_, vi + pr * ci_ + pi * cr
            re_ref[rows, cols] = vr
            im_ref[rows, cols] = vi
            if reverse:
                new += [vr[0:1], vi[0:1]]
            else:
                new += [vr[7:8], vi[7:8]]
        return tuple(new)

    return lax.fori_loop(0, TM // 8, chunk, carry)


def _load_carry(car_ref):
    return tuple(car_ref[r:r + 1, lb * 512:(lb + 1) * 512] for lb in range(GP // 512) for r in (0, 1))


def _store_carry(car_ref, carry):
    for lb in range(GP // 512):
        car_ref[0:1, lb * 512:(lb + 1) * 512] = carry[2 * lb]
        car_ref[1:2, lb * 512:(lb + 1) * 512] = carry[2 * lb + 1]


def _s5_fwd(l, proj, bexp, cre, cimn, powers, dsk, wglu, bglu, job=None):
    def body(ua_ref, za_ref, bexp_ref, cre_ref, cimn_ref, st_re_ref, st_im_ref, cr_re_ref, cr_im_ref,
             d_ref, wg_ref, bg_ref, sre_ref, sim_ref, y1_ref, q_ref, ya_ref, car_ref):
        @pl.when(pl.program_id(0) == 0)
        def _():
            car_ref[...] = jnp.zeros_like(car_ref)

        u = ua_ref[...]
        ub = u.astype(BF16)
        for k in range(4):
            bu = _dot(ub[:, 128 * k:128 * (k + 1)], bexp_ref[k])
            sre_ref[:, 512 * k:512 * (k + 1)] = bu[:, :512]
            sim_ref[:, 512 * k:512 * (k + 1)] = bu[:, 512:]
        carry = _scan_tile(sre_ref, sim_ref, st_re_ref, st_im_ref, cr_re_ref, cr_im_ref, _load_carry(car_ref), False)
        _store_carry(car_ref, carry)
        for k in range(4):
            blk = slice(512 * k, 512 * (k + 1))
            ks = slice(128 * k, 128 * (k + 1))
            y0 = _dot(sre_ref[:, blk].astype(BF16), cre_ref[k]) + _dot(sim_ref[:, blk].astype(BF16), cimn_ref[k])
            y1_ref[:, ks] = y0 + d_ref[:, ks] * u[:, ks]
        y2 = _gelu(y1_ref[...])
        q = _dot(y2.astype(BF16), wg_ref[...]) + bg_ref[...]
        q_ref[...] = q
        za = za_ref[...]
        ya_ref[...] = ((y2 * _sigmoid(q)) * (za * _sigmoid(za))).astype(BF16)

    return _pcall(
        body, f"s5_fwd_l{l}", (NT,),
        [_rows(W, 0), _rows(W, 1), _layer(l, (4, 128, 1024)), _layer(l, (4, 512, 128)), _layer(l, (4, 512, 128)),
         _layer(l, (24, GP)), _layer(l, (24, GP)), _layer(l, (8, GP)), _layer(l, (8, GP)), _layer(l, (1, W)),
         _layer(l, (W, W)), _layer(l, (1, W))],
        [_rows(GP), _rows(GP), _rows(W), _rows(W), _rows(W)],
        [SDS((L, GP), F32), SDS((L, GP), F32), SDS((L, W), F32), SDS((L, W), F32), SDS((L, W), BF16)],
        (proj, proj, bexp, cre, cimn, *powers, dsk, wglu, bglu),
        scratch=[pltpu.VMEM((8, GP), F32)], jobs=[job])


def _pool_fwd(l, proj, pw, scale):
    def body(ub_ref, zb_ref, pw_ref, sc_ref, pooled_ref, mixed_ref, yb_ref, buf):
        i = pl.program_id(0)

        @pl.when(i == 0)
        def _():
            buf[0:16, :] = jnp.zeros((16, W), F32)

        u = ub_ref[...]
        buf[16:16 + TM, :] = u
        t = i * TM + lax.broadcasted_iota(jnp.int32, (TM, 128), 0)
        for gi, win in enumerate(WINS):
            cs = slice(128 * gi, 128 * (gi + 1))
            acc = u[:, cs]
            for k in range(1, win):
                acc = acc + buf[16 - k:16 - k + TM, cs]
            cnt = jnp.minimum(t + 1, win).astype(F32)
            pb = (acc / cnt - u[:, cs]).astype(BF16)
            pooled_ref[:, cs] = pb
            mixed_ref[:, cs] = _dot(pb, pw_ref[gi])
        zb = zb_ref[...]
        yb_ref[...] = ((mixed_ref[...] * sc_ref[...]) * (zb * _sigmoid(zb))).astype(BF16)
        buf[0:16, :] = buf[TM:TM + 16, :]

    return pl.pallas_call(
        body, name=f"pool_fwd_l{l}", grid=(NT,),
        in_specs=[_rows(W, 2), _rows(W, 3), _layer(l, (4, 128, 128)), _layer(l, (1, W))],
        out_specs=[_rows(W), _rows(W), _rows(W)],
        out_shape=[SDS((L, W), BF16), SDS((L, W), F32), SDS((L, W), BF16)],
        scratch_shapes=[pltpu.VMEM((TM + 16, W), F32)],
        compiler_params=_params("arbitrary"),
    )(proj, proj, pw, scale)


def _merge_out(l, ya, yb, proj, x, wa, wb, wo):
    def body(ya_ref, yb_ref, ga_ref, gb_ref, x_ref, wa_ref, wb_ref, wo_ref, pa_ref, pb_ref, mg_ref, xo_ref):
        ya = ya_ref[...]
        yb = yb_ref[...]
        for j in range(4):
            cs = slice(256 * j, 256 * (j + 1))
            pa_ref[:, cs] = _dot(ya, wa_ref[j])
            pb_ref[:, cs] = _dot(yb, wb_ref[j])
        merged = _sigmoid(ga_ref[...]) * pa_ref[...] + _sigmoid(gb_ref[...]) * pb_ref[...]
        mb = merged.astype(BF16)
        mg_ref[...] = mb
        xo_ref[...] = x_ref[...] + _dot(mb, wo_ref[...])

    return pl.pallas_call(
        body, name=f"merge_out_l{l}", grid=(L // TMM,),
        in_specs=[_rows_mm(W), _rows_mm(W), _rows_mm(D, 2), _rows_mm(D, 3), _rows_mm(D),
                  _layer(l, (4, W, 256)), _layer(l, (4, W, 256)), _layer(l, (D, D))],
        out_specs=[_rows_mm(D), _rows_mm(D), _rows_mm(D), _rows_mm(D)],
        out_shape=[SDS((L, D), F32), SDS((L, D), F32), SDS((L, D), BF16), SDS((L, D), F32)],
        compiler_params=_params("arbitrary"),
    )(ya, yb, proj, proj, x, wa, wb, wo)


def _loss_head(x, gf, target):
    def body(x_ref, g_ref, t_ref, loss_ref, dx_ref, dg_ref):
        @pl.when(pl.program_id(0) == 0)
        def _():
            loss_ref[...] = jnp.zeros_like(loss_ref)
            dg_ref[...] = jnp.zeros_like(dg_ref)

        xv = x_ref[...]
        g = g_ref[...]
        r = lax.rsqrt(jnp.mean(xv * xv, axis=-1, keepdims=True) + EPS)
        xn = xv * r
        err = xn * g - t_ref[...]
        part = jnp.sum(jnp.mean(err * err, axis=-1, keepdims=True), axis=0, keepdims=True)
        loss_ref[...] += 0.5 * part
        dy = err * (1.0 / D)
        dg_ref[...] += jnp.sum(dy * xn, axis=0, keepdims=True)
        dxn = dy * g
        dx_ref[...] = r * (dxn - xn * jnp.mean(dxn * xn, axis=-1, keepdims=True))

    return pl.pallas_call(
        body, name="loss_head", grid=(NT,),
        in_specs=[_rows(D), _full((1, D)), _rows(D)],
        out_specs=[_full((2, 128)), _rows(D), _full((1, D))],
        out_shape=[SDS((2, 128), F32), SDS((L, D), F32), SDS((1, D), F32)],
        compiler_params=_params("arbitrary"),
    )(x, gf, target)


def _merge_out_bwd(l, dxn, mg, proj, pa, pb, ya, yb, wo, wa, wb, jobs=()):
    def body(dx_ref, mg_ref, ga_ref, gb_ref, pa_ref, pb_ref, ya_ref, yb_ref, wo_ref, wa_ref, wb_ref,
             dg_ref, dya_ref, dyb_ref, dwo_ref, dwa_ref, dwb_ref, dbias_ref):
        @pl.when(pl.program_id(0) == 0)
        def _():
            for ref in (dwo_ref, dwa_ref, dwb_ref, dbias_ref):
                ref[...] = jnp.zeros_like(ref)

        dxb = dx_ref[...].astype(BF16)
        dm = _dot_nt(dxb, wo_ref[...])
        sa = _sigmoid(ga_ref[...])
        sb = _sigmoid(gb_ref[...])
        dga = dm * pa_ref[...] * (sa * (1.0 - sa))
        dgb = dm * pb_ref[...] * (sb * (1.0 - sb))
        dg_ref[:, :D] = dga.astype(BF16)
        dg_ref[:, D:] = dgb.astype(BF16)
        dbias_ref[:, :D] += jnp.sum(dga, axis=0, keepdims=True)
        dbias_ref[:, D:] += jnp.sum(dgb, axis=0, keepdims=True)
        dpa = (dm * sa).astype(BF16)
        dpb = (dm * sb).astype(BF16)
        ya = ya_ref[...]
        yb = yb_ref[...]
        dya = jnp.zeros((TM, W), F32)
        dyb = jnp.zeros((TM, W), F32)
        for j in range(4):
            cs = slice(256 * j, 256 * (j + 1))
            dya = dya + _dot_nt(dpa[:, cs], wa_ref[j])
            dyb = dyb + _dot_nt(dpb[:, cs], wb_ref[j])
            dwa_ref[j] += _dot_tn(ya, dpa[:, cs])
            dwb_ref[j] += _dot_tn(yb, dpb[:, cs])
        dya_ref[...] = dya
        dyb_ref[...] = dyb
        dwo_ref[...] += _dot_tn(mg_ref[...], dxb)

    return _pcall(
        body, f"merge_out_bwd_l{l}", (NT,),
        [_rows(D), _rows(D), _rows(D, 2), _rows(D, 3), _rows(D), _rows(D), _rows(W), _rows(W),
         _layer(l, (D, D)), _layer(l, (4, W, 256)), _layer(l, (4, W, 256))],
        [_rows(2 * D, 1), _rows(W), _rows(W), _full((D, D)), _full((4, W, 256)), _full((4, W, 256)), _full((1, 2 * D))],
        [SDS((L, NIN), BF16), SDS((L, W), F32), SDS((L, W), F32),
         SDS((D, D), F32), SDS((4, W, 256), F32), SDS((4, W, 256), F32), SDS((1, 2 * D), F32)],
        (dxn, mg, proj, proj, pa, pb, ya, yb, wo, wa, wb), jobs=jobs)


def _pool_bwd(l, dyb, proj, mixed, pooled, pw, scale, dproj, jobs=()):
    def body(dyb_ref, zb_ref, mixed_ref, pooled_ref, pw_ref, sc_ref, _, db_ref, dpw_ref, dsc_ref, dbias_ref, buf):
        i = pl.program_id(0)
        tile = NT - 1 - i

        @pl.when(i == 0)
        def _():
            dpw_ref[...] = jnp.zeros_like(dpw_ref)
            dsc_ref[...] = jnp.zeros_like(dsc_ref)
            dbias_ref[...] = jnp.zeros_like(dbias_ref)
            buf[TM:TM + 16, :] = jnp.zeros((16, W), F32)

        dyb = dyb_ref[...]
        zb = zb_ref[...]
        mixed = mixed_ref[...]
        sc = sc_ref[...]
        sg = _sigmoid(zb)
        dyb0 = dyb * (zb * sg)
        dzb = dyb * (mixed * sc) * (sg * (1.0 + zb * (1.0 - sg)))
        db_ref[:, W:] = dzb.astype(BF16)
        dbias_ref[:, W:] += jnp.sum(dzb, axis=0, keepdims=True)
        dsc_ref[...] += jnp.sum(dyb0 * mixed, axis=0, keepdims=True)
        dmix = (dyb0 * sc).astype(BF16)
        t = tile * TM + lax.broadcasted_iota(jnp.int32, (TM, 128), 0)
        for gi, win in enumerate(WINS):
            cs = slice(128 * gi, 128 * (gi + 1))
            dpw_ref[gi] += _dot_tn(pooled_ref[:, cs], dmix[:, cs])
            dpool = _dot_nt(dmix[:, cs], pw_ref[gi])
            cnt = jnp.minimum(t + 1, win).astype(F32)
            e = dpool / cnt
            buf[0:TM, cs] = e
            acc = e - dpool
            for k in range(1, win):
                acc = acc + buf[k:k + TM, cs]
            db_ref[:, cs] = acc.astype(BF16)
            dbias_ref[:, cs] += jnp.sum(acc, axis=0, keepdims=True)
        buf[TM:TM + 16, :] = buf[0:16, :]

    return _pcall(
        body, f"pool_bwd_l{l}", (NT,),
        [_rows(W, 0, True), _rows(W, 3, True), _rows(W, 0, True), _rows(W, 0, True),
         _layer(l, (4, 128, 128)), _layer(l, (1, W)), _ANY],
        [_rows(2 * W, 1, True), _full((4, 128, 128)), _full((1, W)), _full((1, 2 * W))],
        [SDS((L, NIN), BF16), SDS((4, 128, 128), F32), SDS((1, W), F32), SDS((1, 2 * W), F32)],
        (dyb, proj, mixed, pooled, pw, scale, dproj),
        scratch=[pltpu.VMEM((TM + 16, W), F32)], aliases={6: 0}, jobs=jobs)


def _s5_bwd(l, dya, proj, y1, q, sre, sim, cret, cimnt, bret, bimt, st_re, st_im, cr_re, cr_im, dsk, wglu, dproj, jobs=()):
    def halo(i):
        return (jnp.maximum((NT - 1 - i) * (TM // 8) - 1, 0), 0)

    def body(dya_ref, ua_ref, za_ref, y1_ref, q_ref, sre_ref, sim_ref, hre_ref, him_ref,
             cret_ref, cimnt_ref, bret_ref, bimt_ref, st_re_ref, st_im_ref, cr_re_ref, cr_im_ref, d_ref, wg_ref, _,
             da_ref, dwg_ref, dbg_ref, dd_ref, dcre_ref, dcimn_ref, dbre_ref, dbim_ref, dare_ref, daim_ref, dbias_ref,
             lre, lim, car_ref):
        i = pl.program_id(0)
        tile = NT - 1 - i

        @pl.when(i == 0)
        def _():
            for ref in (dwg_ref, dbg_ref, dd_ref, dcre_ref, dcimn_ref, dbre_ref, dbim_ref, dare_ref, daim_ref, dbias_ref,
                        car_ref):
                ref[...] = jnp.zeros_like(ref)

        u = ua_ref[...]
        za = za_ref[...]
        y1 = y1_ref[...]
        dya = dya_ref[...]
        y2 = _gelu(y1)
        sg = _sigmoid(q_ref[...])
        sgz = _sigmoid(za)
        dy3 = dya * (za * sgz)
        dza = dya * (y2 * sg) * (sgz * (1.0 + za * (1.0 - sgz)))
        da_ref[:, W:] = dza.astype(BF16)
        dbias_ref[:, W:] += jnp.sum(dza, axis=0, keepdims=True)
        dq = dy3 * y2 * (sg * (1.0 - sg))
        dqb = dq.astype(BF16)
        dy2 = dy3 * sg + _dot_nt(dqb, wg_ref[...])
        dwg_ref[...] += _dot_tn(y2.astype(BF16), dqb)
        dbg_ref[...] += jnp.sum(dq, axis=0, keepdims=True)
        dy1 = dy2 * _gelu_grad(y1)
        dd_ref[...] += jnp.sum(dy1 * u, axis=0, keepdims=True)
        dy1b = dy1.astype(BF16)
        ub = u.astype(BF16)
        for k in range(4):
            blk = slice(512 * k, 512 * (k + 1))
            ks = slice(128 * k, 128 * (k + 1))
            lre[:, blk] = _dot(dy1b[:, ks], cret_ref[k])
            lim[:, blk] = _dot(dy1b[:, ks], cimnt_ref[k])
            dcre_ref[k] += _dot_tn(sre_ref[:, blk].astype(BF16), dy1b[:, ks])
            dcimn_ref[k] += _dot_tn(sim_ref[:, blk].astype(BF16), dy1b[:, ks])
        carry = _scan_tile(lre, lim, st_re_ref, st_im_ref, cr_re_ref, cr_im_ref, _load_carry(car_ref), True)
        _store_carry(car_ref, carry)

        rowid = lax.broadcasted_iota(jnp.int32, (8, 512), 0)
        gate = (tile > 0).astype(F32)

        def chunk(c, _):
            rows = pl.ds(pl.multiple_of(c * 8, 8), 8)
            prows = pl.ds(pl.multiple_of(jnp.maximum(c - 1, 0) * 8, 8), 8)
            for lb in range(GP // 512):
                cols = slice(lb * 512, (lb + 1) * 512)
                sr = sre_ref[rows, cols]
                si = sim_ref[rows, cols]
                pr = jnp.where(c == 0, hre_ref[7:8, cols] * gate, sre_ref[prows, cols][7:8])
                pi = jnp.where(c == 0, him_ref[7:8, cols] * gate, sim_ref[prows, cols][7:8])
                sr = jnp.where(rowid == 0, pr, pltpu.roll(sr, 1, 0))
                si = jnp.where(rowid == 0, pi, pltpu.roll(si, 1, 0))
                lr = lre[rows, cols]
                li = lim[rows, cols]
                dare_ref[:, cols] += sr * lr + si * li
                daim_ref[:, cols] += sr * li - si * lr
            return 0

        lax.fori_loop(0, TM // 8, chunk, 0)

        for k in range(4):
            blk = slice(512 * k, 512 * (k + 1))
            ks = slice(128 * k, 128 * (k + 1))
            lrb = lre[:, blk].astype(BF16)
            lib = lim[:, blk].astype(BF16)
            du = dy1[:, ks] * d_ref[:, ks] + _dot(lrb, bret_ref[k]) + _dot(lib, bimt_ref[k])
            da_ref[:, ks] = du.astype(BF16)
            dbias_ref[:, ks] += jnp.sum(du, axis=0, keepdims=True)
            dbre_ref[k] += _dot_tn(ub[:, ks], lrb)
            dbim_ref[k] += _dot_tn(ub[:, ks], lib)

    return _pcall(
        body, f"s5_bwd_l{l}", (NT,),
        [_rows(W, 0, True), _rows(W, 0, True), _rows(W, 1, True), _rows(W, 0, True), _rows(W, 0, True),
         _rows(GP, 0, True), _rows(GP, 0, True),
         pl.BlockSpec((8, GP), halo), pl.BlockSpec((8, GP), halo),
         _layer(l, (4, 128, 512)), _layer(l, (4, 128, 512)), _layer(l, (4, 512, 128)), _layer(l, (4, 512, 128)),
         _layer(l, (24, GP)), _layer(l, (24, GP)), _layer(l, (8, GP)), _layer(l, (8, GP)), _layer(l, (1, W)),
         _layer(l, (W, W)), _ANY],
        [_rows(2 * W, 0, True), _full((W, W)), _full((1, W)), _full((1, W)),
         _full((4, 512, 128)), _full((4, 512, 128)), _full((4, 128, 512)), _full((4, 128, 512)),
         _full((8, GP)), _full((8, GP)), _full((1, 2 * W))],
        [SDS((L, NIN), BF16), SDS((W, W), F32), SDS((1, W), F32), SDS((1, W), F32),
         SDS((4, 512, 128), F32), SDS((4, 512, 128), F32), SDS((4, 128, 512), F32), SDS((4, 128, 512), F32),
         SDS((8, GP), F32), SDS((8, GP), F32), SDS((1, 2 * W), F32)],
        (dya, proj, proj, y1, q, sre, sim, sre, sim, cret, cimnt, bret, bimt, st_re, st_im, cr_re, cr_im, dsk, wglu,
         dproj),
        scratch=[pltpu.VMEM((TM, GP), F32), pltpu.VMEM((TM, GP), F32), pltpu.VMEM((8, GP), F32)],
        aliases={19: 0}, jobs=jobs)


def _inproj_dw(l, r, h, dproj, jobs=()):
    def body(h_ref, dp_ref, dw_ref):
        part = _dot_tn(h_ref[...], dp_ref[...])

        @pl.when(pl.program_id(1) == 0)
        def _():
            dw_ref[...] = part

        @pl.when(pl.program_id(1) > 0)
        def _():
            dw_ref[...] += part

    return _pcall(
        body, f"inproj_dw{r}_l{l}", (4, L // TK),
        [pl.BlockSpec((TK, D // 2), lambda j, i: (i, r)), pl.BlockSpec((TK, 1024), lambda j, i: (i, j))],
        [pl.BlockSpec((None, D // 2, 1024), lambda j, i: (j, 0, 0))],
        [SDS((4, D // 2, 1024), F32)],
        (h, dproj), jobs=jobs)


def _inproj_dx(l, dproj, w, x, g, dxn, jobs=()):
    def body(dp_ref, w_ref, x_ref, g_ref, dxn_ref, dx_ref, dg_ref):
        @pl.when(pl.program_id(0) == 0)
        def _():
            dg_ref[...] = jnp.zeros_like(dg_ref)

        dh = _dot_nt(dp_ref[:, 0:1024], w_ref[0])
        for j in range(1, 4):
            dh = dh + _dot_nt(dp_ref[:, j * 1024:(j + 1) * 1024], w_ref[j])
        xv = x_ref[...]
        r = lax.rsqrt(jnp.mean(xv * xv, axis=-1, keepdims=True) + EPS)
        xn = xv * r
        dg_ref[...] += jnp.sum(dh * xn, axis=0, keepdims=True)
        dn = dh * g_ref[...]
        dx_ref[...] = dxn_ref[...] + r * (dn - xn * jnp.mean(dn * xn, axis=-1, keepdims=True))

    return _pcall(
        body, f"inproj_dx_l{l}", (L // TMM,),
        [_rows_mm(NIN), _layer(l, (4, D, 1024)), _rows_mm(D), _layer(l, (1, D)), _rows_mm(D)],
        [_rows_mm(D), _full((1, D))],
        [SDS((L, D), F32), SDS((1, D), F32)],
        (dproj, w, x, g, dxn), jobs=jobs)


def _discretize(log_dt, lam_re, lam_im, b_re, b_im):
    dt = jnp.exp(log_dt)[..., None]
    mag = jnp.exp(lam_re * dt)
    ang = lam_im * dt
    abar_re = mag * jnp.cos(ang)
    abar_im = mag * jnp.sin(ang)
    num_re = abar_re - 1.0
    num_im = abar_im
    den = lam_re * lam_re + lam_im * lam_im
    coef_re = (num_re * lam_re + num_im * lam_im) / den
    coef_im = (num_im * lam_re - num_re * lam_im) / den
    bbar_re = coef_re[..., None] * b_re - coef_im[..., None] * b_im
    bbar_im = coef_re[..., None] * b_im + coef_im[..., None] * b_re
    return abar_re, abar_im, bbar_re, bbar_im


def _powers(abar_re, abar_im):
    ar, ai = abar_re.reshape(DEPTH, 1, GP), abar_im.reshape(DEPTH, 1, GP)
    rows_re, rows_im = [ar], [ai]
    for _ in range(7):
        pr, pi = rows_re[-1], rows_im[-1]
        rows_re.append(pr * ar - pi * ai)
        rows_im.append(pr * ai + pi * ar)
    row = jnp.arange(8)[:, None]

    def steps(rows, keep):
        return jnp.concatenate([jnp.where(keep(d), rows[d - 1], 0.0) for d in (1, 2, 4)], axis=1)

    neg_im = [-r for r in rows_im]
    fwd = (steps(rows_re, lambda d: row >= d), steps(rows_im, lambda d: row >= d),
           jnp.concatenate(rows_re, axis=1), jnp.concatenate(rows_im, axis=1))
    rev = (steps(rows_re, lambda d: row < 8 - d), steps(neg_im, lambda d: row < 8 - d),
           jnp.concatenate(rows_re[::-1], axis=1), jnp.concatenate(neg_im[::-1], axis=1))
    return fwd, rev


_EYE8 = functools.partial(jnp.eye, 8, dtype=F32)


def _expand_in(b):
    return jnp.einsum("lkgpc,gh->lkgchp", b.reshape(DEPTH, 4, 8, P, C), _EYE8()).reshape(DEPTH, 4, 128, 512)


def _extract_in(e):
    return jnp.einsum("lkgchp,gh->lkgpc", e.reshape(DEPTH, 4, 8, C, 8, P), _EYE8()).reshape(DEPTH, G, P, C)


def _expand_out(c):
    return jnp.einsum("lkgcp,gh->lkgphc", c.reshape(DEPTH, 4, 8, C, P), _EYE8()).reshape(DEPTH, 4, 512, 128)


def _extract_out(e):
    return jnp.einsum("lkgphc,gh->lkgcp", e.reshape(DEPTH, 4, 8, P, 8, C), _EYE8()).reshape(DEPTH, G, C, P)


SMALL = ("norm_g", "b_in", "ssm_log_dt", "ssm_lam_re", "ssm_lam_im", "ssm_b_re", "ssm_b_im",
         "ssm_c_re", "ssm_c_im", "ssm_d", "ssm_b_glu", "pool_w", "pool_scale")
BIG = ("w_in", "ssm_w_glu", "w_branch_a", "w_branch_b", "w_out")


def _step(x, target, w, m, v, place):
    sp = {n: w[n] for n in SMALL}
    final_norm_g = w["final_norm_g"]
    wbuf = dict(zip(BIG, _cast_own(place, [w[n] for n in BIG])))
    (abar_re, abar_im, bbar_re, bbar_im), disc_vjp = jax.vjp(
        _discretize, *(sp[n] for n in ("ssm_log_dt", "ssm_lam_re", "ssm_lam_im", "ssm_b_re", "ssm_b_im")))
    powers_fwd, powers_rev = _powers(abar_re, abar_im)
    b_re_x, b_im_x = _expand_in(bbar_re), _expand_in(bbar_im)
    c_re_x, c_imn_x = _expand_out(sp["ssm_c_re"]), _expand_out(-sp["ssm_c_im"])
    b_x = jnp.concatenate([b_re_x, b_im_x], axis=3).astype(BF16)
    t = lambda a: jnp.swapaxes(a, 2, 3).astype(BF16)
    c_re_t, c_imn_t, b_re_t, b_im_t = t(c_re_x), t(c_imn_x), t(b_re_x), t(b_im_x)
    c_re_x, c_imn_x = c_re_x.astype(BF16), c_imn_x.astype(BF16)
    row = lambda n: sp[n].reshape(DEPTH, 1, -1)
    g, b_in, dsk, b_glu, scale = row("norm_g"), row("b_in"), row("ssm_d"), row("ssm_b_glu"), row("pool_scale")
    pw = sp["pool_w"].astype(BF16)

    saved = []
    for l in range(DEPTH):
        if l == 0:
            (wbuf["w_in"],) = _comm_only("gather_w_in_l0", _GatherJob([wbuf["w_in"]], 0))[0]
        (h, proj), res = _norm_inproj(l, x, g, wbuf["w_in"], b_in, _GatherJob([wbuf[n] for n in BIG[1:]], l))
        wbuf.update(zip(BIG[1:], res[0][0]))
        wg = dict(wbuf, ssm_w_glu=wbuf["ssm_w_glu"].reshape(DEPTH, W, W), w_out=wbuf["w_out"].reshape(DEPTH, D, D))
        (sre, sim, y1, q, ya), res = _s5_fwd(
            l, proj, b_x, c_re_x, c_imn_x, powers_fwd, dsk, wg["ssm_w_glu"], b_glu,
            _GatherJob([wbuf["w_in"]], l + 1) if l + 1 < DEPTH else None)
        if res:
            (wbuf["w_in"],) = res[0][0]
            wg["w_in"] = wbuf["w_in"]
        pooled, mixed, yb = _pool_fwd(l, proj, pw, scale)
        pa, pb, mg, x_next = _merge_out(l, ya, yb, proj, x, wg["w_branch_a"], wg["w_branch_b"], wg["w_out"])
        saved.append(dict(x=x, h=h, proj=proj, sre=sre, sim=sim, y1=y1, q=q, ya=ya,
                          pooled=pooled, mixed=mixed, yb=yb, pa=pa, pb=pb, mg=mg))
        x = x_next

    loss, dx, dgf = _loss_head(x, final_norm_g.reshape(1, D), target)

    per_layer = {n: [None] * DEPTH for n in ("norm_g", "b_in", "ssm_d", "ssm_b_glu", "pool_w", "pool_scale",
                                             "dare", "daim", "dbre", "dbim", "dcre", "dcimn")}
    red = _Reducer(place, w, m, v)
    for l in reversed(range(DEPTH)):
        s = saved[l]
        (dproj, dya, dyb, dwo, dwa, dwb, dbias_g), res = _merge_out_bwd(
            l, dx, s["mg"], s["proj"], s["pa"], s["pb"], s["ya"], s["yb"],
            wg["w_out"], wg["w_branch_a"], wg["w_branch_b"], red.jobs())
        red.land(res)
        (dproj, dpw, dsc, dbias_b), res = _pool_bwd(l, dyb, s["proj"], s["mixed"], s["pooled"], pw, scale, dproj,
                                                    red.jobs())
        red.land(res)
        (dproj, dwg, dbg, dd, dcre, dcimn, dbre, dbim, dare, daim, dbias_a), res = _s5_bwd(
            l, dya, s["proj"], s["y1"], s["q"], s["sre"], s["sim"], c_re_t, c_imn_t, b_re_t, b_im_t,
            *powers_rev, dsk, wg["ssm_w_glu"], dproj, red.jobs())
        red.land(res)
        red.add(l, "rest", BIG[1:], [dwg.reshape(4, W // 4, W), dwa, dwb, dwo.reshape(4, D // 4, D)])
        for r in range(2):
            outs, res = _inproj_dw(l, r, s["h"], dproj, red.jobs())
            red.land(res)
            red.add(l, f"in{r}", BIG[:1], outs[:1], r * (D // 2))
        (dx, dg), res = _inproj_dx(l, dproj, wg["w_in"], s["x"], g, dx, red.jobs())
        red.land(res)
        for n, a in (("norm_g", dg.reshape(D)), ("b_in", jnp.concatenate([dbias_a, dbias_b, dbias_g], axis=1).reshape(NIN)),
                     ("ssm_d", dd.reshape(W)), ("ssm_b_glu", dbg.reshape(W)), ("pool_w", dpw), ("pool_scale", dsc.reshape(W)),
                     ("dare", dare), ("daim", daim), ("dbre", dbre), ("dbim", dbim), ("dcre", dcre), ("dcimn", dcimn)):
            per_layer[n][l] = a
    gs = {n: jnp.stack(a) for n, a in per_layer.items()}
    d_abar = [jnp.sum(gs.pop(n), axis=1).reshape(DEPTH, G, P) for n in ("dare", "daim")]
    (gs["ssm_log_dt"], gs["ssm_lam_re"], gs["ssm_lam_im"], gs["ssm_b_re"], gs["ssm_b_im"]) = disc_vjp(
        (*d_abar, _extract_in(gs.pop("dbre")), _extract_in(gs.pop("dbim"))))
    gs["ssm_c_re"], gs["ssm_c_im"] = _extract_out(gs.pop("dcre")), -_extract_out(gs.pop("dcimn"))
    gs["final_norm_g"] = dgf

    natural = {n: w[n].shape for n in REPLICATED}
    rw, rm, rv = {}, {}, {}
    for n in REPLICATED:
        shape = DENSE.get(n, natural[n])
        shape = shape if len(shape) > 2 else (shape[0], 1, shape[1])
        gs[n], rw[n], rm[n], rv[n] = (a.reshape(shape) for a in (gs[n], w[n], m[n], v[n]))
    small = [gs[n] for n in REPLICATED] + [loss.reshape(2, 1, 128)]
    jobs = red.jobs()
    res = _pcall(None, "tail_exchange", (), [], [], [], [], jobs=jobs + [_SiblingJob(small, False)])[1]
    red.land(res[:len(jobs)])
    pair_small = _small_pair_sum(place, small, res[-1][1], [BF16] * len(SMALL) + [F32, F32])
    jobs = red.jobs()
    res = _pcall(None, "tail_gather", (), [], [], [], [], jobs=jobs + [_ChipGatherJob(pair_small)])[1]
    red.land(res[:len(jobs)])
    assert not red.active
    small_parts = dict(zip(REPLICATED + ("loss",), res[-1][0]))

    k = len(REPLICATED)
    outs = _adamw_small("adamw_small", [rw[n] for n in REPLICATED], [small_parts[n] for n in REPLICATED],
                        [rm[n] for n in REPLICATED], [rv[n] for n in REPLICATED], small_parts["loss"])
    results = {n: red.big[n] for n in BIG}
    results.update({n: [outs[1 + q * k + i].reshape(natural[n]) for q in range(4)] for i, n in enumerate(REPLICATED)})
    return outs[0][0, 0, 0], dx, results


def _place():
    x, y, c = lax.axis_index("x"), lax.axis_index("y"), lax.axis_index("c")
    chips = [(1 - x, y), (x, 1 - y), (1 - x, 1 - y)]
    return x, y, c, 2 * x + y, chips, [2 * cx + cy for cx, cy in chips]


def _remote(src, dst, ssem, rsem, dev):
    return pltpu.make_async_remote_copy(src_ref=src, dst_ref=dst, send_sem=ssem, recv_sem=rsem,
                                        device_id=dev, device_id_type=MESH)


class _GatherJob:
    def __init__(self, bufs, l):
        self.srcs, self.bufs, self.news, self.l = [], list(bufs), [], l
        self.scratch = [pltpu.SemaphoreType.DMA((len(self.bufs), 3))] * 4

    def _half(self, ref, k, h):
        rows = ref.shape[2] // 2
        return ref.at[self.l, k, pl.ds(pl.multiple_of(h * rows, 8), rows), :]

    def _ici(self, bufs, sems, a, j, k):
        _, _, c, _, chips, _ = _place()
        blk = self._half(bufs[a], k, c)
        return _remote(blk, blk, sems[0].at[a, j], sems[1].at[a, j], (*chips[j], c))

    def _d2d(self, bufs, sems, a, j, k, h):
        x, y, c, _, _, _ = _place()
        blk = self._half(bufs[a], k, h)
        return _remote(blk, blk, sems[2].at[a, j], sems[3].at[a, j], (x, y, 1 - c))

    def start(self, srcs, bufs, news, sems):
        me = _place()[3]
        for a in range(len(self.bufs)):
            for j in range(3):
                self._ici(bufs, sems, a, j, me).start()

    def finish(self, srcs, bufs, news, sems):
        _, _, c, me, _, cid = _place()
        pairs = [(a, j) for a in range(len(self.bufs)) for j in range(3)]
        for a, j in pairs:
            self._ici(bufs, sems, a, j, cid[j]).wait_recv()
            self._d2d(bufs, sems, a, j, cid[j], c).start()
        for a, j in pairs:
            self._d2d(bufs, sems, a, j, cid[j], 1 - c).wait_recv()
        for a, j in pairs:
            self._ici(bufs, sems, a, j, me).wait_send()
            self._d2d(bufs, sems, a, j, cid[j], c).wait_send()


class _SiblingJob:
    def __init__(self, srcs, rows_half):
        self.srcs, self.bufs, self.rows_half = list(srcs), [], rows_half
        self.news = [SDS((s.shape[0], s.shape[1] // 2, s.shape[2]) if rows_half else s.shape, s.dtype) for s in srcs]
        self.scratch = [pltpu.SemaphoreType.DMA((len(self.srcs),))] * 2

    def _copy(self, srcs, news, sems, a):
        x, y, c, _, _, _ = _place()
        src = srcs[a]
        if self.rows_half:
            rows = src.shape[1] // 2
            src = src.at[:, pl.ds(pl.multiple_of((1 - c) * rows, 8), rows), :]
        return _remote(src, news[a], sems[0].at[a], sems[1].at[a], (x, y, 1 - c))

    def start(self, srcs, bufs, news, sems):
        for a in range(len(self.srcs)):
            self._copy(srcs, news, sems, a).start()

    def finish(self, srcs, bufs, news, sems):
        for a in range(len(self.srcs)):
            self._copy(srcs, news, sems, a).wait()


class _ScatterJob:
    def __init__(self, parts):
        self.srcs, self.bufs = list(parts), []
        self.news = [SDS((3,) + p.shape[1:], p.dtype) for p in parts]
        self.scratch = [pltpu.SemaphoreType.DMA((len(self.srcs), 3))] * 2

    def _copy(self, srcs, news, sems, a, j):
        _, _, c, _, chips, cid = _place()
        return _remote(srcs[a].at[cid[j]], news[a].at[j], sems[0].at[a, j], sems[1].at[a, j], (*chips[j], c))

    def start(self, srcs, bufs, news, sems):
        for a in range(len(self.srcs)):
            for j in range(3):
                self._copy(srcs, news, sems, a, j).start()

    def finish(self, srcs, bufs, news, sems):
        for a in range(len(self.srcs)):
            for j in range(3):
                self._copy(srcs, news, sems, a, j).wait()


def _comm_only(name, job):
    return _pcall(None, name, (), [], [], [], [], jobs=[job])[1][0]


class _ChipGatherJob(_GatherJob):
    def __init__(self, bufs):
        super().__init__(bufs, None)

    def _half(self, ref, k, h):
        return ref.at[k, h]


def _cast_own(place, ws):
    n = len(ws)

    def body(p_ref, *refs):
        for i_ref, o_ref in zip(refs[:n], refs[n:]):
            o_ref[...] = i_ref[...].astype(BF16)

    return pl.pallas_call(
        body, name="cast_own_shards",
        grid_spec=pltpu.PrefetchScalarGridSpec(
            num_scalar_prefetch=1, grid=(DEPTH,),
            in_specs=[pl.BlockSpec((None,) + a.shape[1:], lambda l, p: (l, 0, 0)) for a in ws],
            out_specs=[pl.BlockSpec((None, None) + a.shape[1:], lambda l, p: (l, p[1], 0, 0)) for a in ws]),
        out_shape=[SDS((DEPTH, 4) + a.shape[1:], BF16) for a in ws],
        compiler_params=_params("arbitrary"),
    )(place, *ws)


def _half_tiles(a_):
    rows = a_ // 2
    ta = min(rows, 256)
    return rows, ta, rows // ta


def _pair_sums_bf16(name, place, owns, recvs):
    n = len(owns)

    def body(p_ref, *refs):
        for own_ref, recv_ref, out_ref in zip(refs[:n], refs[n:2 * n], refs[2 * n:]):
            out_ref[...] = (own_ref[...] + recv_ref[...]).astype(BF16)

    def own_half(a):
        return pl.BlockSpec((None, a.shape[1] // 2, a.shape[2]), lambda s, p: (s, p[0], 0))

    def block(a):
        return pl.BlockSpec((None,) + a.shape[1:], lambda s, p: (s, 0, 0))

    return pl.pallas_call(
        body, name=name,
        grid_spec=pltpu.PrefetchScalarGridSpec(
            num_scalar_prefetch=1, grid=(4,),
            in_specs=[own_half(a) for a in owns] + [block(r) for r in recvs],
            out_specs=[block(r) for r in recvs]),
        out_shape=[SDS(r.shape, BF16) for r in recvs],
        compiler_params=_params("arbitrary"),
    )(place, *owns, *recvs)


def _shard_sums(name, place, owns, recvs, rbufs):
    n = len(owns)

    def body(p_ref, *refs):
        for own_ref, recv_ref, r_ref, out_ref in zip(refs[:n], refs[n:2 * n], refs[2 * n:3 * n], refs[3 * n:]):
            acc = own_ref[...] + recv_ref[...]
            for j in range(3):
                acc = acc + r_ref[j].astype(F32)
            out_ref[...] = acc

    def own_half(a):
        return pl.BlockSpec((None, a.shape[1] // 2, a.shape[2]), lambda i, p: (p[1], p[0], 0))

    def recv_block(a):
        return pl.BlockSpec((None,) + a.shape[1:], lambda i, p: (p[1], 0, 0))

    return pl.pallas_call(
        body, name=name,
        grid_spec=pltpu.PrefetchScalarGridSpec(
            num_scalar_prefetch=1, grid=(1,),
            in_specs=([own_half(a) for a in owns] + [recv_block(r) for r in recvs]
                      + [pl.BlockSpec(rb.shape, lambda i, p: (0, 0, 0)) for rb in rbufs]),
            out_specs=[pl.BlockSpec(r.shape[1:], lambda i, p: (0, 0)) for r in recvs]),
        out_shape=[SDS(r.shape[1:], F32) for r in recvs],
        compiler_params=_params("arbitrary"),
    )(place, *owns, *recvs, *rbufs)


def _small_pair_sum(place, mine, recv, dtypes):
    n = len(mine)

    def body(p_ref, *refs):
        for m_ref, r_ref, o_ref in zip(refs[:n], refs[n:2 * n], refs[2 * n:]):
            o_ref[...] = (m_ref[...] + r_ref[...]).astype(o_ref.dtype)

    def whole(a):
        zeros = (0,) * a.ndim
        return pl.BlockSpec(a.shape, lambda i, p: zeros)

    def mine_blk(a):
        zeros = (0,) * a.ndim
        return pl.BlockSpec((None,) + a.shape, lambda i, p: (p[1],) + zeros)

    return pl.pallas_call(
        body, name="small_pair_sum",
        grid_spec=pltpu.PrefetchScalarGridSpec(
            num_scalar_prefetch=1, grid=(1,),
            in_specs=[whole(a) for a in mine] + [whole(a) for a in recv],
            out_specs=[mine_blk(a) for a in mine]),
        out_shape=[SDS((4,) + a.shape, dt) for a, dt in zip(mine, dtypes)],
        compiler_params=_params("arbitrary"),
    )(place, *mine, *recv)


def _adam_math(w, g, m, v):
    m = B1 * m + (1.0 - B1) * g
    v = B2 * v + (1.0 - B2) * (g * g)
    m_hat = m / (1.0 - B1 ** STEP)
    v_hat = v / (1.0 - B2 ** STEP)
    delta = -LR * (m_hat / (jnp.sqrt(v_hat) + EPS_A) + WD * w)
    return delta, m, v


def _adamw_big(name, l, row0, w, m, v, mine, other, prev, jobs=()):
    _, _, b_ = w.shape
    _, ta, nh = _half_tiles(2 * mine.shape[0])
    prev = list(prev or [])

    def body(w_ref, m_ref, v_ref, mine_ref, other_ref, *rest):
        g_ref, d_ref, mo_ref, vo_ref = rest[len(prev):]
        g = jnp.where(pl.program_id(0) == lax.axis_index("c"), mine_ref[...], other_ref[...])
        g_ref[...] = g
        d_ref[...], mo_ref[...], vo_ref[...] = _adam_math(w_ref[...], g, m_ref[...], v_ref[...])

    slab = pl.BlockSpec((None, ta, b_), lambda h, i: (l, row0 // ta + h * nh + i, 0))
    half = pl.BlockSpec((ta, b_), lambda h, i: (i, 0))
    outs, res = _pcall(
        body, name, (2, nh), [slab, slab, slab, half, half] + [_ANY] * len(prev), [slab] * 4, [SDS(w.shape, F32)] * 4,
        (w, m, v, mine, other, *prev), aliases={5 + k: k for k in range(len(prev))}, jobs=jobs)
    return outs, res


def _adamw_small(name, ws, parts, ms, vs, loss_parts=None):
    k = len(ws)
    extra = [] if loss_parts is None else [loss_parts]

    def chip_sum(p_ref):
        p = [p_ref[k].astype(F32) for k in range(4)]
        return ((p[0] + p[1]) + p[2]) + p[3]

    def body(*refs):
        w_refs, p_refs, m_refs, v_refs = refs[:k], refs[k:2 * k], refs[2 * k:3 * k], refs[3 * k:4 * k]
        outs = refs[4 * k + len(extra):]
        if extra:
            outs[0][...] = chip_sum(refs[4 * k])
            outs = outs[1:]
        for a in range(k):
            g = chip_sum(p_refs[a])
            outs[a][...] = g
            outs[k + a][...], outs[2 * k + a][...], outs[3 * k + a][...] = _adam_math(
                w_refs[a][...], g, m_refs[a][...], v_refs[a][...])

    like = [SDS(a.shape, F32) for a in ws]
    return pl.pallas_call(
        body, name=name,
        out_shape=([SDS(loss_parts.shape[1:], F32)] if extra else []) + like * 4,
        compiler_params=pltpu.CompilerParams(vmem_limit_bytes=VMEM_LIMIT),
    )(*ws, *parts, *ms, *vs, *extra)


class _Reducer:
    def __init__(self, place, w, m, v):
        self.place, self.w, self.m, self.v = place, w, m, v
        self.active, self.riding = [], []
        self.big = {n: None for n in BIG}

    def add(self, l, tag, names, own, row0=0):
        self.active.append(dict(l=l, key=f"{tag}_l{l}", names=names, own=list(own), row0=row0, stage=0))

    def jobs(self):
        self.riding = list(self.active)
        return [(_SiblingJob(g["own"], True), _ScatterJob(g.get("parts", [])), _SiblingJob(g.get("shard", []), False))
                [g["stage"]] for g in self.riding]

    def land(self, res):
        for g, (_, news) in zip(self.riding, res):
            if g["stage"] == 0:
                g["recv"] = news
                g["parts"] = _pair_sums_bf16(f"pair_sums_{g['key']}", self.place, g["own"], news)
            elif g["stage"] == 1:
                g["shard"] = _shard_sums(f"shard_sums_{g['key']}", self.place, g["own"], g["recv"], news)
            else:
                for n, mine, other in zip(g["names"], g["shard"], news):
                    self.big[n] = _adamw_big(f"adamw_{n}_{g['key']}", g["l"], g["row0"], self.w[n], self.m[n], self.v[n],
                                             mine, other, self.big[n])[0]
                self.active.remove(g)
            g["stage"] += 1
        self.riding = []


WEIGHTS = ("norm_g", "w_in", "b_in", "ssm_log_dt", "ssm_lam_re", "ssm_lam_im", "ssm_b_re", "ssm_b_im", "ssm_c_re",
           "ssm_c_im", "ssm_d", "ssm_w_glu", "ssm_b_glu", "pool_w", "pool_scale", "w_branch_a", "w_branch_b", "w_out",
           "final_norm_g")
REPLICATED = SMALL + ("final_norm_g",)
DENSE = {"ssm_b_re": (DEPTH, G, P * C), "ssm_b_im": (DEPTH, G, P * C), "final_norm_g": (2, D // 2)}


def kernel(x, norm_g, w_in, b_in, ssm_log_dt, ssm_lam_re, ssm_lam_im, ssm_b_re, ssm_b_im, ssm_c_re, ssm_c_im, ssm_d, ssm_w_glu, ssm_b_glu, pool_w, pool_scale, w_branch_a, w_branch_b, w_out, final_norm_g, loss_target, m_norm_g, m_w_in, m_b_in, m_ssm_log_dt, m_ssm_lam_re, m_ssm_lam_im, m_ssm_b_re, m_ssm_b_im, m_ssm_c_re, m_ssm_c_im, m_ssm_d, m_ssm_w_glu, m_ssm_b_glu, m_pool_w, m_pool_scale, m_w_branch_a, m_w_branch_b, m_w_out, m_final_norm_g, v_norm_g, v_w_in, v_b_in, v_ssm_log_dt, v_ssm_lam_re, v_ssm_lam_im, v_ssm_b_re, v_ssm_b_im, v_ssm_c_re, v_ssm_c_im, v_ssm_d, v_ssm_w_glu, v_ssm_b_glu, v_pool_w, v_pool_scale, v_w_branch_a, v_w_branch_b, v_w_out, v_final_norm_g):
    w = dict(zip(WEIGHTS, (norm_g, w_in, b_in, ssm_log_dt, ssm_lam_re, ssm_lam_im, ssm_b_re, ssm_b_im, ssm_c_re,
                           ssm_c_im, ssm_d, ssm_w_glu, ssm_b_glu, pool_w, pool_scale, w_branch_a, w_branch_b, w_out,
                           final_norm_g)))
    m = dict(zip(WEIGHTS, (m_norm_g, m_w_in, m_b_in, m_ssm_log_dt, m_ssm_lam_re, m_ssm_lam_im, m_ssm_b_re, m_ssm_b_im,
                           m_ssm_c_re, m_ssm_c_im, m_ssm_d, m_ssm_w_glu, m_ssm_b_glu, m_pool_w, m_pool_scale,
                           m_w_branch_a, m_w_branch_b, m_w_out, m_final_norm_g)))
    v = dict(zip(WEIGHTS, (v_norm_g, v_w_in, v_b_in, v_ssm_log_dt, v_ssm_lam_re, v_ssm_lam_im, v_ssm_b_re, v_ssm_b_im,
                           v_ssm_c_re, v_ssm_c_im, v_ssm_d, v_ssm_w_glu, v_ssm_b_glu, v_pool_w, v_pool_scale,
                           v_w_branch_a, v_w_branch_b, v_w_out, v_final_norm_g)))
    place = jnp.stack([lax.axis_index("c"), 2 * lax.axis_index("x") + lax.axis_index("y")]).astype(jnp.int32)

    total_loss, dx, results = _step(x[0], loss_target[0], w, m, v, place)
    return (total_loss, dx[None], *[results[n][q] for q in range(4) for n in WEIGHTS])
```

```python
import functools

import jax
import jax.numpy as jnp
from jax import lax
from jax.experimental import pallas as pl
from jax.experimental.pallas import tpu as pltpu

F32, BF16 = jnp.float32, jnp.bfloat16
SDS = jax.ShapeDtypeStruct
MESH = pl.DeviceIdType.MESH

DEPTH = 2
L = 2048
D = 1024
NIN = 4096
W = 512
G, P, C = 32, 64, 16
GP = G * P
WINS = (2, 4, 8, 16)
TM = 256
NT = L // TM
TMM = 512
TK = 1024
EPS = 1e-6
VMEM_LIMIT = 56 * 2**20

LR, B1, B2, EPS_A, WD, STEP = 0.001, 0.9, 0.999, 1e-08, 0.01, 10


def _params(*sem):
    return pltpu.CompilerParams(dimension_semantics=sem, vmem_limit_bytes=VMEM_LIMIT)


_ANY = pl.BlockSpec(memory_space=pl.ANY)


def _full(shape):
    zeros = (0,) * len(shape)
    return pl.BlockSpec(shape, lambda *_: zeros)


def _layer(l, shape):
    zeros = (0,) * len(shape)
    return pl.BlockSpec((None,) + shape, lambda *_: (l,) + zeros)


def _rows(width, col=0, reverse=False, tm=TM):
    if reverse:
        return pl.BlockSpec((tm, width), lambda i: (L // tm - 1 - i, col))
    return pl.BlockSpec((tm, width), lambda i: (i, col))


def _rows_mm(width, col=0):
    return _rows(width, col, False, TMM)


def _pcall(body, name, grid, in_specs, out_specs, out_shape, args, scratch=(), aliases=None, jobs=()):
    in_specs, out_specs, out_shape, args, scratch = list(in_specs), list(out_specs), list(out_shape), list(args), list(scratch)
    aliases = dict(aliases or {})
    jobs = [j for j in jobs if j is not None]
    n_in, n_out, n_scr = len(in_specs), len(out_specs), len(scratch)
    srcs = [s for j in jobs for s in j.srcs]
    bufs = [b for j in jobs for b in j.bufs]
    news = [s for j in jobs for s in j.news]
    aliases.update({n_in + len(srcs) + k: n_out + k for k in range(len(bufs))})

    def hosted(*refs):
        cuts = [n_in, len(srcs), len(bufs), n_out, len(bufs), len(news), n_scr]
        parts, p = [], 0
        for n in cuts:
            parts.append(refs[p:p + n])
            p += n
        ins, src_r, _, outs, buf_r, new_r, scr = parts
        sem_r = refs[p:]
        views, ps, pb, pn, pm = [], 0, 0, 0, 0
        for j in jobs:
            views.append((src_r[ps:ps + len(j.srcs)], buf_r[pb:pb + len(j.bufs)], new_r[pn:pn + len(j.news)],
                          sem_r[pm:pm + len(j.scratch)]))
            ps, pb, pn, pm = ps + len(j.srcs), pb + len(j.bufs), pn + len(j.news), pm + len(j.scratch)

        def run(phase):
            for j, v in zip(jobs, views):
                getattr(j, phase)(*v)

        def at_step(step):
            return functools.reduce(jnp.logical_and, [pl.program_id(d) == step(d) for d in range(len(grid))])

        if not grid:
            run("start")
            run("finish")
            return
        pl.when(at_step(lambda d: 0))(lambda: run("start"))
        body(*ins, *outs, *scr)
        pl.when(at_step(lambda d: grid[d] - 1))(lambda: run("finish"))

    outs = pl.pallas_call(
        hosted if jobs else body, name=name, **({"grid": grid} if grid else {}),
        in_specs=in_specs + [_ANY] * (len(srcs) + len(bufs)), out_specs=out_specs + [_ANY] * (len(bufs) + len(news)),
        out_shape=out_shape + [SDS(b.shape, b.dtype) for b in bufs] + news,
        input_output_aliases=aliases, scratch_shapes=scratch + [s for j in jobs for s in j.scratch],
        compiler_params=_params(*(("arbitrary",) * len(grid))))(*args, *srcs, *bufs)
    res, pb, pn = [], n_out, n_out + len(bufs)
    for j in jobs:
        res.append((list(outs[pb:pb + len(j.bufs)]), list(outs[pn:pn + len(j.news)])))
        pb, pn = pb + len(j.bufs), pn + len(j.news)
    return list(outs[:n_out]), res


def _dot(a, b):
    return jnp.dot(a, b, preferred_element_type=F32)


def _dot_nt(a, b):
    return lax.dot_general(a, b, (((1,), (1,)), ((), ())), preferred_element_type=F32)


def _dot_tn(a, b):
    return lax.dot_general(a, b, (((0,), (0,)), ((), ())), preferred_element_type=F32)


_K0 = 0.7978845608028654
_K1 = 0.044715


def _gelu(x):
    return 0.5 * x * (1.0 + jnp.tanh(_K0 * (x + _K1 * (x * x * x))))


def _gelu_grad(x):
    t = jnp.tanh(_K0 * (x + _K1 * (x * x * x)))
    return 0.5 * (1.0 + t) + 0.5 * x * (1.0 - t * t) * (_K0 * (1.0 + 3.0 * _K1 * x * x))


def _sigmoid(x):
    return jax.nn.sigmoid(x)


def _norm_inproj(l, x, g, w, b, job=None):
    def body(x_ref, g_ref, w_ref, b_ref, h_ref, proj_ref):
        xv = x_ref[...]
        r = lax.rsqrt(jnp.mean(xv * xv, axis=-1, keepdims=True) + EPS)
        hb = ((xv * r) * g_ref[...]).astype(BF16)
        h_ref[...] = hb
        for j in range(4):
            cs = slice(j * 1024, (j + 1) * 1024)
            proj_ref[:, cs] = _dot(hb, w_ref[j]) + b_ref[:, cs]

    return _pcall(
        body, f"norm_inproj_l{l}", (L // TMM,),
        [_rows_mm(D), _layer(l, (1, D)), _layer(l, (4, D, 1024)), _layer(l, (1, NIN))],
        [_rows_mm(D), _rows_mm(NIN)],
        [SDS((L, D), BF16), SDS((L, NIN), F32)],
        (x, g, w, b), jobs=[job])


def _scan_tile(re_ref, im_ref, st_re, st_im, cr_re, cr_im, carry, reverse):
    def chunk(ci, carry):
        c = (TM // 8 - 1 - ci) if reverse else ci
        rows = pl.ds(pl.multiple_of(c * 8, 8), 8)
        new = []
        for lb in range(GP // 512):
            cols = slice(lb * 512, (lb + 1) * 512)
            vr = re_ref[rows, cols]
            vi = im_ref[rows, cols]
            for s, d in enumerate((1, 2, 4)):
                ar = st_re[8 * s:8 * s + 8, cols]
                ai = st_im[8 * s:8 * s + 8, cols]
                sr = pltpu.roll(vr, 8 - d if reverse else d, 0)
                si = pltpu.roll(vi, 8 - d if reverse else d, 0)
                vr, vi = vr + ar * sr - ai * si, vi + ar * si + ai * sr
            cr, ci_ = carry[2 * lb], carry[2 * lb + 1]
            pr = cr_re[:, cols]
            pi = cr_im[:, cols]
            vr, vi = vr + pr * cr - pi * ci_, vi + pr * ci_ + pi * cr
            re_ref[rows, cols] = vr
            im_ref[rows, cols] = vi
            if reverse:
                new += [vr[0:1], vi[0:1]]
            else:
                new += [vr[7:8], vi[7:8]]
        return tuple(new)

    return lax.fori_loop(0, TM // 8, chunk, carry)


def _load_carry(car_ref):
    return tuple(car_ref[r:r + 1, lb * 512:(lb + 1) * 512] for lb in range(GP // 512) for r in (0, 1))


def _store_carry(car_ref, carry):
    for lb in range(GP // 512):
        car_ref[0:1, lb * 512:(lb + 1) * 512] = carry[2 * lb]
        car_ref[1:2, lb * 512:(lb + 1) * 512] = carry[2 * lb + 1]


def _s5_fwd(l, proj, bexp, cre, cimn, powers, dsk, wglu, bglu, job=None):
    def body(ua_ref, za_ref, bexp_ref, cre_ref, cimn_ref, st_re_ref, st_im_ref, cr_re_ref, cr_im_ref,
             d_ref, wg_ref, bg_ref, sre_ref, sim_ref, y1_ref, q_ref, ya_ref, car_ref):
        @pl.when(pl.program_id(0) == 0)
        def _():
            car_ref[...] = jnp.zeros_like(car_ref)

        u = ua_ref[...]
        ub = u.astype(BF16)
        for k in range(4):
            bu = _dot(ub[:, 128 * k:128 * (k + 1)], bexp_ref[k])
            sre_ref[:, 512 * k:512 * (k + 1)] = bu[:, :512]
            sim_ref[:, 512 * k:512 * (k + 1)] = bu[:, 512:]
        carry = _scan_tile(sre_ref, sim_ref, st_re_ref, st_im_ref, cr_re_ref, cr_im_ref, _load_carry(car_ref), False)
        _store_carry(car_ref, carry)
        for k in range(4):
            blk = slice(512 * k, 512 * (k + 1))
            ks = slice(128 * k, 128 * (k + 1))
            y0 = _dot(sre_ref[:, blk].astype(BF16), cre_ref[k]) + _dot(sim_ref[:, blk].astype(BF16), cimn_ref[k])
            y1_ref[:, ks] = y0 + d_ref[:, ks] * u[:, ks]
        y2 = _gelu(y1_ref[...])
        q = _dot(y2.astype(BF16), wg_ref[...]) + bg_ref[...]
        q_ref[...] = q
        za = za_ref[...]
        ya_ref[...] = ((y2 * _sigmoid(q)) * (za * _sigmoid(za))).astype(BF16)

    return _pcall(
        body, f"s5_fwd_l{l}", (NT,),
        [_rows(W, 0), _rows(W, 1), _layer(l, (4, 128, 1024)), _layer(l, (4, 512, 128)), _layer(l, (4, 512, 128)),
         _layer(l, (24, GP)), _layer(l, (24, GP)), _layer(l, (8, GP)), _layer(l, (8, GP)), _layer(l, (1, W)),
         _layer(l, (W, W)), _layer(l, (1, W))],
        [_rows(GP), _rows(GP), _rows(W), _rows(W), _rows(W)],
        [SDS((L, GP), F32), SDS((L, GP), F32), SDS((L, W), F32), SDS((L, W), F32), SDS((L, W), BF16)],
        (proj, proj, bexp, cre, cimn, *powers, dsk, wglu, bglu),
        scratch=[pltpu.VMEM((8, GP), F32)], jobs=[job])


def _pool_fwd(l, proj, pw, scale):
    def body(ub_ref, zb_ref, pw_ref, sc_ref, pooled_ref, mixed_ref, yb_ref, buf):
        i = pl.program_id(0)

        @pl.when(i == 0)
        def _():
            buf[0:16, :] = jnp.zeros((16, W), F32)

        u = ub_ref[...]
        buf[16:16 + TM, :] = u
        t = i * TM + lax.broadcasted_iota(jnp.int32, (TM, 128), 0)
        for gi, win in enumerate(WINS):
            cs = slice(128 * gi, 128 * (gi + 1))
            acc = u[:, cs]
            for k in range(1, win):
                acc = acc + buf[16 - k:16 - k + TM, cs]
            cnt = jnp.minimum(t + 1, win).astype(F32)
            pb = (acc / cnt - u[:, cs]).astype(BF16)
            pooled_ref[:, cs] = pb
            mixed_ref[:, cs] = _dot(pb, pw_ref[gi])
        zb = zb_ref[...]
        yb_ref[...] = ((mixed_ref[...] * sc_ref[...]) * (zb * _sigmoid(zb))).astype(BF16)
        buf[0:16, :] = buf[TM:TM + 16, :]

    return pl.pallas_call(
        body, name=f"pool_fwd_l{l}", grid=(NT,),
        in_specs=[_rows(W, 2), _rows(W, 3), _layer(l, (4, 128, 128)), _layer(l, (1, W))],
        out_specs=[_rows(W), _rows(W), _rows(W)],
        out_shape=[SDS((L, W), BF16), SDS((L, W), F32), SDS((L, W), BF16)],
        scratch_shapes=[pltpu.VMEM((TM + 16, W), F32)],
        compiler_params=_params("arbitrary"),
    )(proj, proj, pw, scale)


def _merge_out(l, ya, yb, proj, x, wa, wb, wo):
    def body(ya_ref, yb_ref, ga_ref, gb_ref, x_ref, wa_ref, wb_ref, wo_ref, pa_ref, pb_ref, mg_ref, xo_ref):
        ya = ya_ref[...]
        yb = yb_ref[...]
        for j in range(4):
            cs = slice(256 * j, 256 * (j + 1))
            pa_ref[:, cs] = _dot(ya, wa_ref[j])
            pb_ref[:, cs] = _dot(yb, wb_ref[j])
        merged = _sigmoid(ga_ref[...]) * pa_ref[...] + _sigmoid(gb_ref[...]) * pb_ref[...]
        mb = merged.astype(BF16)
        mg_ref[...] = mb
        xo_ref[...] = x_ref[...] + _dot(mb, wo_ref[...])

    return pl.pallas_call(
        body, name=f"merge_out_l{l}", grid=(L // TMM,),
        in_specs=[_rows_mm(W), _rows_mm(W), _rows_mm(D, 2), _rows_mm(D, 3), _rows_mm(D),
                  _layer(l, (4, W, 256)), _layer(l, (4, W, 256)), _layer(l, (D, D))],
        out_specs=[_rows_mm(D), _rows_mm(D), _rows_mm(D), _rows_mm(D)],
        out_shape=[SDS((L, D), F32), SDS((L, D), F32), SDS((L, D), BF16), SDS((L, D), F32)],
        compiler_params=_params("arbitrary"),
    )(ya, yb, proj, proj, x, wa, wb, wo)


def _loss_head(x, gf, target):
    def body(x_ref, g_ref, t_ref, loss_ref, dx_ref, dg_ref):
        @pl.when(pl.program_id(0) == 0)
        def _():
            loss_ref[...] = jnp.zeros_like(loss_ref)
            dg_ref[...] = jnp.zeros_like(dg_ref)

        xv = x_ref[...]
        g = g_ref[...]
        r = lax.rsqrt(jnp.mean(xv * xv, axis=-1, keepdims=True) + EPS)
        xn = xv * r
        err = xn * g - t_ref[...]
        part = jnp.sum(jnp.mean(err * err, axis=-1, keepdims=True), axis=0, keepdims=True)
        loss_ref[...] += 0.5 * part
        dy = err * (1.0 / D)
        dg_ref[...] += jnp.sum(dy * xn, axis=0, keepdims=True)
        dxn = dy * g
        dx_ref[...] = r * (dxn - xn * jnp.mean(dxn * xn, axis=-1, keepdims=True))

    return pl.pallas_call(
        body, name="loss_head", grid=(NT,),
        in_specs=[_rows(D), _full((1, D)), _rows(D)],
        out_specs=[_full((2, 128)), _rows(D), _full((1, D))],
        out_shape=[SDS((2, 128), F32), SDS((L, D), F32), SDS((1, D), F32)],
        compiler_params=_params("arbitrary"),
    )(x, gf, target)


def _merge_out_bwd(l, dxn, mg, proj, pa, pb, ya, yb, wo, wa, wb, jobs=()):
    def body(dx_ref, mg_ref, ga_ref, gb_ref, pa_ref, pb_ref, ya_ref, yb_ref, wo_ref, wa_ref, wb_ref,
             dg_ref, dya_ref, dyb_ref, dwo_ref, dwa_ref, dwb_ref, dbias_ref):
        @pl.when(pl.program_id(0) == 0)
        def _():
            for ref in (dwo_ref, dwa_ref, dwb_ref, dbias_ref):
                ref[...] = jnp.zeros_like(ref)

        dxb = dx_ref[...].astype(BF16)
        dm = _dot_nt(dxb, wo_ref[...])
        sa = _sigmoid(ga_ref[...])
        sb = _sigmoid(gb_ref[...])
        dga = dm * pa_ref[...] * (sa * (1.0 - sa))
        dgb = dm * pb_ref[...] * (sb * (1.0 - sb))
        dg_ref[:, :D] = dga.astype(BF16)
        dg_ref[:, D:] = dgb.astype(BF16)
        dbias_ref[:, :D] += jnp.sum(dga, axis=0, keepdims=True)
        dbias_ref[:, D:] += jnp.sum(dgb, axis=0, keepdims=True)
        dpa = (dm * sa).astype(BF16)
        dpb = (dm * sb).astype(BF16)
        ya = ya_ref[...]
        yb = yb_ref[...]
        dya = jnp.zeros((TM, W), F32)
        dyb = jnp.zeros((TM, W), F32)
        for j in range(4):
            cs = slice(256 * j, 256 * (j + 1))
            dya = dya + _dot_nt(dpa[:, cs], wa_ref[j])
            dyb = dyb + _dot_nt(dpb[:, cs], wb_ref[j])
            dwa_ref[j] += _dot_tn(ya, dpa[:, cs])
            dwb_ref[j] += _dot_tn(yb, dpb[:, cs])
        dya_ref[...] = dya
        dyb_ref[...] = dyb
        dwo_ref[...] += _dot_tn(mg_ref[...], dxb)

    return _pcall(
        body, f"merge_out_bwd_l{l}", (NT,),
        [_rows(D), _rows(D), _rows(D, 2), _rows(D, 3), _rows(D), _rows(D), _rows(W), _rows(W),
         _layer(l, (D, D)), _layer(l, (4, W, 256)), _layer(l, (4, W, 256))],
        [_rows(2 * D, 1), _rows(W), _rows(W), _full((D, D)), _full((4, W, 256)), _full((4, W, 256)), _full((1, 2 * D))],
        [SDS((L, NIN), BF16), SDS((L, W), F32), SDS((L, W), F32),
         SDS((D, D), F32), SDS((4, W, 256), F32), SDS((4, W, 256), F32), SDS((1, 2 * D), F32)],
        (dxn, mg, proj, proj, pa, pb, ya, yb, wo, wa, wb), jobs=jobs)


def _pool_bwd(l, dyb, proj, mixed, pooled, pw, scale, dproj, jobs=()):
    def body(dyb_ref, zb_ref, mixed_ref, pooled_ref, pw_ref, sc_ref, _, db_ref, dpw_ref, dsc_ref, dbias_ref, buf):
        i = pl.program_id(0)
        tile = NT - 1 - i

        @pl.when(i == 0)
        def _():
            dpw_ref[...] = jnp.zeros_like(dpw_ref)
            dsc_ref[...] = jnp.zeros_like(dsc_ref)
            dbias_ref[...] = jnp.zeros_like(dbias_ref)
            buf[TM:TM + 16, :] = jnp.zeros((16, W), F32)

        dyb = dyb_ref[...]
        zb = zb_ref[...]
        mixed = mixed_ref[...]
        sc = sc_ref[...]
        sg = _sigmoid(zb)
        dyb0 = dyb * (zb * sg)
        dzb = dyb * (mixed * sc) * (sg * (1.0 + zb * (1.0 - sg)))
        db_ref[:, W:] = dzb.astype(BF16)
        dbias_ref[:, W:] += jnp.sum(dzb, axis=0, keepdims=True)
        dsc_ref[...] += jnp.sum(dyb0 * mixed, axis=0, keepdims=True)
        dmix = (dyb0 * sc).astype(BF16)
        t = tile * TM + lax.broadcasted_iota(jnp.int32, (TM, 128), 0)
        for gi, win in enumerate(WINS):
            cs = slice(128 * gi, 128 * (gi + 1))
            dpw_ref[gi] += _dot_tn(pooled_ref[:, cs], dmix[:, cs])
            dpool = _dot_nt(dmix[:, cs], pw_ref[gi])
            cnt = jnp.minimum(t + 1, win).astype(F32)
            e = dpool / cnt
            buf[0:TM, cs] = e
            acc = e - dpool
            for k in range(1, win):
                acc = acc + buf[k:k + TM, cs]
            db_ref[:, cs] = acc.astype(BF16)
            dbias_ref[:, cs] += jnp.sum(acc, axis=0, keepdims=True)
        buf[TM:TM + 16, :] = buf[0:16, :]

    return _pcall(
        body, f"pool_bwd_l{l}", (NT,),
        [_rows(W, 0, True), _rows(W, 3, True), _rows(W, 0, True), _rows(W, 0, True),
         _layer(l, (4, 128, 128)), _layer(l, (1, W)), _ANY],
        [_rows(2 * W, 1, True), _full((4, 128, 128)), _full((1, W)), _full((1, 2 * W))],
        [SDS((L, NIN), BF16), SDS((4, 128, 128), F32), SDS((1, W), F32), SDS((1, 2 * W), F32)],
        (dyb, proj, mixed, pooled, pw, scale, dproj),
        scratch=[pltpu.VMEM((TM + 16, W), F32)], aliases={6: 0}, jobs=jobs)


def _s5_bwd(l, dya, proj, y1, q, sre, sim, cret, cimnt, bret, bimt, st_re, st_im, cr_re, cr_im, dsk, wglu, dproj, jobs=()):
    def halo(i):
        return (jnp.maximum((NT - 1 - i) * (TM // 8) - 1, 0), 0)

    def body(dya_ref, ua_ref, za_ref, y1_ref, q_ref, sre_ref, sim_ref, hre_ref, him_ref,
             cret_ref, cimnt_ref, bret_ref, bimt_ref, st_re_ref, st_im_ref, cr_re_ref, cr_im_ref, d_ref, wg_ref, _,
             da_ref, dwg_ref, dbg_ref, dd_ref, dcre_ref, dcimn_ref, dbre_ref, dbim_ref, dare_ref, daim_ref, dbias_ref,
             lre, lim, car_ref):
        i = pl.program_id(0)
        tile = NT - 1 - i

        @pl.when(i == 0)
        def _():
            for ref in (dwg_ref, dbg_ref, dd_ref, dcre_ref, dcimn_ref, dbre_ref, dbim_ref, dare_ref, daim_ref, dbias_ref,
                        car_ref):
                ref[...] = jnp.zeros_like(ref)

        u = ua_ref[...]
        za = za_ref[...]
        y1 = y1_ref[...]
        dya = dya_ref[...]
        y2 = _gelu(y1)
        sg = _sigmoid(q_ref[...])
        sgz = _sigmoid(za)
        dy3 = dya * (za * sgz)
        dza = dya * (y2 * sg) * (sgz * (1.0 + za * (1.0 - sgz)))
        da_ref[:, W:] = dza.astype(BF16)
        dbias_ref[:, W:] += jnp.sum(dza, axis=0, keepdims=True)
        dq = dy3 * y2 * (sg * (1.0 - sg))
        dqb = dq.astype(BF16)
        dy2 = dy3 * sg + _dot_nt(dqb, wg_ref[...])
        dwg_ref[...] += _dot_tn(y2.astype(BF16), dqb)
        dbg_ref[...] += jnp.sum(dq, axis=0, keepdims=True)
        dy1 = dy2 * _gelu_grad(y1)
        dd_ref[...] += jnp.sum(dy1 * u, axis=0, keepdims=True)
        dy1b = dy1.astype(BF16)
        ub = u.astype(BF16)
        for k in range(4):
            blk = slice(512 * k, 512 * (k + 1))
            ks = slice(128 * k, 128 * (k + 1))
            lre[:, blk] = _dot(dy1b[:, ks], cret_ref[k])
            lim[:, blk] = _dot(dy1b[:, ks], cimnt_ref[k])
            dcre_ref[k] += _dot_tn(sre_ref[:, blk].astype(BF16), dy1b[:, ks])
            dcimn_ref[k] += _dot_tn(sim_ref[:, blk].astype(BF16), dy1b[:, ks])
        carry = _scan_tile(lre, lim, st_re_ref, st_im_ref, cr_re_ref, cr_im_ref, _load_carry(car_ref), True)
        _store_carry(car_ref, carry)

        rowid = lax.broadcasted_iota(jnp.int32, (8, 512), 0)
        gate = (tile > 0).astype(F32)

        def chunk(c, _):
            rows = pl.ds(pl.multiple_of(c * 8, 8), 8)
            prows = pl.ds(pl.multiple_of(jnp.maximum(c - 1, 0) * 8, 8), 8)
            for lb in range(GP // 512):
                cols = slice(lb * 512, (lb + 1) * 512)
                sr = sre_ref[rows, cols]
                si = sim_ref[rows, cols]
                pr = jnp.where(c == 0, hre_ref[7:8, cols] * gate, sre_ref[prows, cols][7:8])
                pi = jnp.where(c == 0, him_ref[7:8, cols] * gate, sim_ref[prows, cols][7:8])
                sr = jnp.where(rowid == 0, pr, pltpu.roll(sr, 1, 0))
                si = jnp.where(rowid == 0, pi, pltpu.roll(si, 1, 0))
                lr = lre[rows, cols]
                li = lim[rows, cols]
                dare_ref[:, cols] += sr * lr + si * li
                daim_ref[:, cols] += sr * li - si * lr
            return 0

        lax.fori_loop(0, TM // 8, chunk, 0)

        for k in range(4):
            blk = slice(512 * k, 512 * (k + 1))
            ks = slice(128 * k, 128 * (k + 1))
            lrb = lre[:, blk].astype(BF16)
            lib = lim[:, blk].astype(BF16)
            du = dy1[:, ks] * d_ref[:, ks] + _dot(lrb, bret_ref[k]) + _dot(lib, bimt_ref[k])
            da_ref[:, ks] = du.astype(BF16)
            dbias_ref[:, ks] += jnp.sum(du, axis=0, keepdims=True)
            dbre_ref[k] += _dot_tn(ub[:, ks], lrb)
            dbim_ref[k] += _dot_tn(ub[:, ks], lib)

    return _pcall(
        body, f"s5_bwd_l{l}", (NT,),
        [_rows(W, 0, True), _rows(W, 0, True), _rows(W, 1, True), _rows(W, 0, True), _rows(W, 0, True),
         _rows(GP, 0, True), _rows(GP, 0, True),
         pl.BlockSpec((8, GP), halo), pl.BlockSpec((8, GP), halo),
         _layer(l, (4, 128, 512)), _layer(l, (4, 128, 512)), _layer(l, (4, 512, 128)), _layer(l, (4, 512, 128)),
         _layer(l, (24, GP)), _layer(l, (24, GP)), _layer(l, (8, GP)), _layer(l, (8, GP)), _layer(l, (1, W)),
         _layer(l, (W, W)), _ANY],
        [_rows(2 * W, 0, True), _full((W, W)), _full((1, W)), _full((1, W)),
         _full((4, 512, 128)), _full((4, 512, 128)), _full((4, 128, 512)), _full((4, 128, 512)),
         _full((8, GP)), _full((8, GP)), _full((1, 2 * W))],
        [SDS((L, NIN), BF16), SDS((W, W), F32), SDS((1, W), F32), SDS((1, W), F32),
         SDS((4, 512, 128), F32), SDS((4, 512, 128), F32), SDS((4, 128, 512), F32), SDS((4, 128, 512), F32),
         SDS((8, GP), F32), SDS((8, GP), F32), SDS((1, 2 * W), F32)],
        (dya, proj, proj, y1, q, sre, sim, sre, sim, cret, cimnt, bret, bimt, st_re, st_im, cr_re, cr_im, dsk, wglu,
         dproj),
        scratch=[pltpu.VMEM((TM, GP), F32), pltpu.VMEM((TM, GP), F32), pltpu.VMEM((8, GP), F32)],
        aliases={19: 0}, jobs=jobs)


def _inproj_dw(l, r, h, dproj, jobs=()):
    def body(h_ref, dp_ref, dw_ref):
        part = _dot_tn(h_ref[...], dp_ref[...])

        @pl.when(pl.program_id(1) == 0)
        def _():
            dw_ref[...] = part

        @pl.when(pl.program_id(1) > 0)
        def _():
            dw_ref[...] += part

    return _pcall(
        body, f"inproj_dw{r}_l{l}", (4, L // TK),
        [pl.BlockSpec((TK, D // 2), lambda j, i: (i, r)), pl.BlockSpec((TK, 1024), lambda j, i: (i, j))],
        [pl.BlockSpec((None, D // 2, 1024), lambda j, i: (j, 0, 0))],
        [SDS((4, D // 2, 1024), F32)],
        (h, dproj), jobs=jobs)


def _inproj_dx(l, dproj, w, x, g, dxn, jobs=()):
    def body(dp_ref, w_ref, x_ref, g_ref, dxn_ref, dx_ref, dg_ref):
        @pl.when(pl.program_id(0) == 0)
        def _():
            dg_ref[...] = jnp.zeros_like(dg_ref)

        dh = _dot_nt(dp_ref[:, 0:1024], w_ref[0])
        for j in range(1, 4):
            dh = dh + _dot_nt(dp_ref[:, j * 1024:(j + 1) * 1024], w_ref[j])
        xv = x_ref[...]
        r = lax.rsqrt(jnp.mean(xv * xv, axis=-1, keepdims=True) + EPS)
        xn = xv * r
        dg_ref[...] += jnp.sum(dh * xn, axis=0, keepdims=True)
        dn = dh * g_ref[...]
        dx_ref[...] = dxn_ref[...] + r * (dn - xn * jnp.mean(dn * xn, axis=-1, keepdims=True))

    return _pcall(
        body, f"inproj_dx_l{l}", (L // TMM,),
        [_rows_mm(NIN), _layer(l, (4, D, 1024)), _rows_mm(D), _layer(l, (1, D)), _rows_mm(D)],
        [_rows_mm(D), _full((1, D))],
        [SDS((L, D), F32), SDS((1, D), F32)],
        (dproj, w, x, g, dxn), jobs=jobs)


def _discretize(log_dt, lam_re, lam_im, b_re, b_im):
    dt = jnp.exp(log_dt)[..., None]
    mag = jnp.exp(lam_re * dt)
    ang = lam_im * dt
    abar_re = mag * jnp.cos(ang)
    abar_im = mag * jnp.sin(ang)
    num_re = abar_re - 1.0
    num_im = abar_im
    den = lam_re * lam_re + lam_im * lam_im
    coef_re = (num_re * lam_re + num_im * lam_im) / den
    coef_im = (num_im * lam_re - num_re * lam_im) / den
    bbar_re = coef_re[..., None] * b_re - coef_im[..., None] * b_im
    bbar_im = coef_re[..., None] * b_im + coef_im[..., None] * b_re
    return abar_re, abar_im, bbar_re, bbar_im


def _powers(abar_re, abar_im):
    ar, ai = abar_re.reshape(DEPTH, 1, GP), abar_im.reshape(DEPTH, 1, GP)
    rows_re, rows_im = [ar], [ai]
    for _ in range(7):
        pr, pi = rows_re[-1], rows_im[-1]
        rows_re.append(pr * ar - pi * ai)
        rows_im.append(pr * ai + pi * ar)
    row = jnp.arange(8)[:, None]

    def steps(rows, keep):
        return jnp.concatenate([jnp.where(keep(d), rows[d - 1], 0.0) for d in (1, 2, 4)], axis=1)

    neg_im = [-r for r in rows_im]
    fwd = (steps(rows_re, lambda d: row >= d), steps(rows_im, lambda d: row >= d),
           jnp.concatenate(rows_re, axis=1), jnp.concatenate(rows_im, axis=1))
    rev = (steps(rows_re, lambda d: row < 8 - d), steps(neg_im, lambda d: row < 8 - d),
           jnp.concatenate(rows_re[::-1], axis=1), jnp.concatenate(neg_im[::-1], axis=1))
    return fwd, rev


_EYE8 = functools.partial(jnp.eye, 8, dtype=F32)


def _expand_in(b):
    return jnp.einsum("lkgpc,gh->lkgchp", b.reshape(DEPTH, 4, 8, P, C), _EYE8()).reshape(DEPTH, 4, 128, 512)


def _extract_in(e):
    return jnp.einsum("lkgchp,gh->lkgpc", e.reshape(DEPTH, 4, 8, C, 8, P), _EYE8()).reshape(DEPTH, G, P, C)


def _expand_out(c):
    return jnp.einsum("lkgcp,gh->lkgphc", c.reshape(DEPTH, 4, 8, C, P), _EYE8()).reshape(DEPTH, 4, 512, 128)


def _extract_out(e):
    return jnp.einsum("lkgphc,gh->lkgcp", e.reshape(DEPTH, 4, 8, P, 8, C), _EYE8()).reshape(DEPTH, G, C, P)


SMALL = ("norm_g", "b_in", "ssm_log_dt", "ssm_lam_re", "ssm_lam_im", "ssm_b_re", "ssm_b_im",
         "ssm_c_re", "ssm_c_im", "ssm_d", "ssm_b_glu", "pool_w", "pool_scale")
BIG = ("w_in", "ssm_w_glu", "w_branch_a", "w_branch_b", "w_out")


def _step(x, target, w, m, v, place):
    sp = {n: w[n] for n in SMALL}
    final_norm_g = w["final_norm_g"]
    wbuf = dict(zip(BIG, _cast_own(place, [w[n] for n in BIG])))
    (abar_re, abar_im, bbar_re, bbar_im), disc_vjp = jax.vjp(
        _discretize, *(sp[n] for n in ("ssm_log_dt", "ssm_lam_re", "ssm_lam_im", "ssm_b_re", "ssm_b_im")))
    powers_fwd, powers_rev = _powers(abar_re, abar_im)
    b_re_x, b_im_x = _expand_in(bbar_re), _expand_in(bbar_im)
    c_re_x, c_imn_x = _expand_out(sp["ssm_c_re"]), _expand_out(-sp["ssm_c_im"])
    b_x = jnp.concatenate([b_re_x, b_im_x], axis=3).astype(BF16)
    t = lambda a: jnp.swapaxes(a, 2, 3).astype(BF16)
    c_re_t, c_imn_t, b_re_t, b_im_t = t(c_re_x), t(c_imn_x), t(b_re_x), t(b_im_x)
    c_re_x, c_imn_x = c_re_x.astype(BF16), c_imn_x.astype(BF16)
    row = lambda n: sp[n].reshape(DEPTH, 1, -1)
    g, b_in, dsk, b_glu, scale = row("norm_g"), row("b_in"), row("ssm_d"), row("ssm_b_glu"), row("pool_scale")
    pw = sp["pool_w"].astype(BF16)

    saved = []
    for l in range(DEPTH):
        if l == 0:
            (wbuf["w_in"],) = _comm_only("gather_w_in_l0", _GatherJob([wbuf["w_in"]], 0))[0]
        (h, proj), res = _norm_inproj(l, x, g, wbuf["w_in"], b_in, _GatherJob([wbuf[n] for n in BIG[1:]], l))
        wbuf.update(zip(BIG[1:], res[0][0]))
        wg = dict(wbuf, ssm_w_glu=wbuf["ssm_w_glu"].reshape(DEPTH, W, W), w_out=wbuf["w_out"].reshape(DEPTH, D, D))
        (sre, sim, y1, q, ya), res = _s5_fwd(
            l, proj, b_x, c_re_x, c_imn_x, powers_fwd, dsk, wg["ssm_w_glu"], b_glu,
            _GatherJob([wbuf["w_in"]], l + 1) if l + 1 < DEPTH else None)
        if res:
            (wbuf["w_in"],) = res[0][0]
            wg["w_in"] = wbuf["w_in"]
        pooled, mixed, yb = _pool_fwd(l, proj, pw, scale)
        pa, pb, mg, x_next = _merge_out(l, ya, yb, proj, x, wg["w_branch_a"], wg["w_branch_b"], wg["w_out"])
        saved.append(dict(x=x, h=h, proj=proj, sre=sre, sim=sim, y1=y1, q=q, ya=ya,
                          pooled=pooled, mixed=mixed, yb=yb, pa=pa, pb=pb, mg=mg))
        x = x_next

    loss, dx, dgf = _loss_head(x, final_norm_g.reshape(1, D), target)

    per_layer = {n: [None] * DEPTH for n in ("norm_g", "b_in", "ssm_d", "ssm_b_glu", "pool_w", "pool_scale",
                                             "dare", "daim", "dbre", "dbim", "dcre", "dcimn")}
    red = _Reducer(place, w, m, v)
    for l in reversed(range(DEPTH)):
        s = saved[l]
        (dproj, dya, dyb, dwo, dwa, dwb, dbias_g), res = _merge_out_bwd(
            l, dx, s["mg"], s["proj"], s["pa"], s["pb"], s["ya"], s["yb"],
            wg["w_out"], wg["w_branch_a"], wg["w_branch_b"], red.jobs())
        red.land(res)
        (dproj, dpw, dsc, dbias_b), res = _pool_bwd(l, dyb, s["proj"], s["mixed"], s["pooled"], pw, scale, dproj,
                                                    red.jobs("pool_bwd"))
        red.land(res)
        (dproj, dwg, dbg, dd, dcre, dcimn, dbre, dbim, dare, daim, dbias_a), res = _s5_bwd(
            l, dya, s["proj"], s["y1"], s["q"], s["sre"], s["sim"], c_re_t, c_imn_t, b_re_t, b_im_t,
            *powers_rev, dsk, wg["ssm_w_glu"], dproj, red.jobs())
        red.land(res)
        rest = [dwg.reshape(4, W // 4, W), dwa, dwb, dwo.reshape(4, D // 4, D)]
        if l == 0:
            red.add(l, "rest", BIG[1:], rest)
        halves = []
        for r in range(2):
            outs, res = _inproj_dw(l, r, s["h"], dproj, red.jobs())
            red.land(res)
            halves += outs
            if l == 0:
                red.add(l, f"in{r}", BIG[:1], outs, [r * (D // 2)])
        (dx, dg), res = _inproj_dx(l, dproj, wg["w_in"], s["x"], g, dx, red.jobs())
        red.land(res)
        if l > 0:
            red.add(l, "all", BIG[:1] * 2 + BIG[1:], halves + rest, [0, D // 2, 0, 0, 0, 0], avoid=("pool_bwd",))
        for n, a in (("norm_g", dg.reshape(D)), ("b_in", jnp.concatenate([dbias_a, dbias_b, dbias_g], axis=1).reshape(NIN)),
                     ("ssm_d", dd.reshape(W)), ("ssm_b_glu", dbg.reshape(W)), ("pool_w", dpw), ("pool_scale", dsc.reshape(W)),
                     ("dare", dare), ("daim", daim), ("dbre", dbre), ("dbim", dbim), ("dcre", dcre), ("dcimn", dcimn)):
            per_layer[n][l] = a
    gs = {n: jnp.stack(a) for n, a in per_layer.items()}
    d_abar = [jnp.sum(gs.pop(n), axis=1).reshape(DEPTH, G, P) for n in ("dare", "daim")]
    (gs["ssm_log_dt"], gs["ssm_lam_re"], gs["ssm_lam_im"], gs["ssm_b_re"], gs["ssm_b_im"]) = disc_vjp(
        (*d_abar, _extract_in(gs.pop("dbre")), _extract_in(gs.pop("dbim"))))
    gs["ssm_c_re"], gs["ssm_c_im"] = _extract_out(gs.pop("dcre")), -_extract_out(gs.pop("dcimn"))
    gs["final_norm_g"] = dgf

    natural = {n: w[n].shape for n in REPLICATED}
    rw, rm, rv = {}, {}, {}
    for n in REPLICATED:
        shape = DENSE.get(n, natural[n])
        gs[n], rw[n], rm[n], rv[n] = (a.reshape(shape) for a in (gs[n], w[n], m[n], v[n]))
    small = [gs[n] for n in REPLICATED] + [loss]
    jobs = red.jobs()
    res = _pcall(None, "tail_exchange", (), [], [], [], [], jobs=jobs + [_SiblingJob(small, False)])[1]
    red.land(res[:len(jobs)])
    pair_small = _small_pair_sum(place, small, res[-1][1], [BF16 if a.ndim > 2 else F32 for a in small])
    jobs = red.jobs()
    res = _pcall(None, "tail_gather", (), [], [], [], [], jobs=jobs + [_ChipGatherJob(pair_small)])[1]
    red.land(res[:len(jobs)])
    assert not red.active
    small_parts = dict(zip(REPLICATED + ("loss",), res[-1][0]))

    k = len(REPLICATED)
    outs = _adamw_small("adamw_small", [rw[n] for n in REPLICATED], [small_parts[n] for n in REPLICATED],
                        [rm[n] for n in REPLICATED], [rv[n] for n in REPLICATED], small_parts["loss"])
    results = {n: red.big[n] for n in BIG}
    results.update({n: [outs[1 + q * k + i].reshape(natural[n]) for q in range(4)] for i, n in enumerate(REPLICATED)})
    return outs[0][0, 0], dx, results


def _place():
    x, y, c = lax.axis_index("x"), lax.axis_index("y"), lax.axis_index("c")
    chips = [(1 - x, y), (x, 1 - y), (1 - x, 1 - y)]
    return x, y, c, 2 * x + y, chips, [2 * cx + cy for cx, cy in chips]


def _remote(src, dst, ssem, rsem, dev):
    return pltpu.make_async_remote_copy(src_ref=src, dst_ref=dst, send_sem=ssem, recv_sem=rsem,
                                        device_id=dev, device_id_type=MESH)


class _GatherJob:
    def __init__(self, bufs, l):
        self.srcs, self.bufs, self.news, self.l = [], list(bufs), [], l
        self.scratch = [pltpu.SemaphoreType.DMA((len(self.bufs), 3))] * 4

    def _half(self, ref, k, h):
        rows = ref.shape[2] // 2
        return ref.at[self.l, k, pl.ds(pl.multiple_of(h * rows, 8), rows), :]

    def _ici(self, bufs, sems, a, j, k):
        _, _, c, _, chips, _ = _place()
        blk = self._half(bufs[a], k, c)
        return _remote(blk, blk, sems[0].at[a, j], sems[1].at[a, j], (*chips[j], c))

    def _d2d(self, bufs, sems, a, j, k, h):
        x, y, c, _, _, _ = _place()
        blk = self._half(bufs[a], k, h)
        return _remote(blk, blk, sems[2].at[a, j], sems[3].at[a, j], (x, y, 1 - c))

    def start(self, srcs, bufs, news, sems):
        me = _place()[3]
        for a in range(len(self.bufs)):
            for j in range(3):
                self._ici(bufs, sems, a, j, me).start()

    def finish(self, srcs, bufs, news, sems):
        _, _, c, me, _, cid = _place()
        pairs = [(a, j) for a in range(len(self.bufs)) for j in range(3)]
        for a, j in pairs:
            self._ici(bufs, sems, a, j, cid[j]).wait_recv()
            self._d2d(bufs, sems, a, j, cid[j], c).start()
        for a, j in pairs:
            self._d2d(bufs, sems, a, j, cid[j], 1 - c).wait_recv()
        for a, j in pairs:
            self._ici(bufs, sems, a, j, me).wait_send()
            self._d2d(bufs, sems, a, j, cid[j], c).wait_send()


class _SiblingJob:
    def __init__(self, srcs, rows_half):
        self.srcs, self.bufs, self.rows_half = list(srcs), [], rows_half
        self.news = [SDS((s.shape[0], s.shape[1] // 2, s.shape[2]) if rows_half else s.shape, s.dtype) for s in srcs]
        self.scratch = [pltpu.SemaphoreType.DMA((len(self.srcs),))] * 2

    def _copy(self, srcs, news, sems, a):
        x, y, c, _, _, _ = _place()
        src = srcs[a]
        if self.rows_half:
            rows = src.shape[1] // 2
            src = src.at[:, pl.ds(pl.multiple_of((1 - c) * rows, 8), rows), :]
        return _remote(src, news[a], sems[0].at[a], sems[1].at[a], (x, y, 1 - c))

    def start(self, srcs, bufs, news, sems):
        for a in range(len(self.srcs)):
            self._copy(srcs, news, sems, a).start()

    def finish(self, srcs, bufs, news, sems):
        for a in range(len(self.srcs)):
            self._copy(srcs, news, sems, a).wait()


class _ScatterJob:
    def __init__(self, parts):
        self.srcs, self.bufs = list(parts), []
        self.news = [SDS((3,) + p.shape[1:], p.dtype) for p in parts]
        self.scratch = [pltpu.SemaphoreType.DMA((len(self.srcs), 3))] * 2

    def _copy(self, srcs, news, sems, a, j):
        _, _, c, _, chips, cid = _place()
        return _remote(srcs[a].at[cid[j]], news[a].at[j], sems[0].at[a, j], sems[1].at[a, j], (*chips[j], c))

    def start(self, srcs, bufs, news, sems):
        for a in range(len(self.srcs)):
            for j in range(3):
                self._copy(srcs, news, sems, a, j).start()

    def finish(self, srcs, bufs, news, sems):
        for a in range(len(self.srcs)):
            for j in range(3):
                self._copy(srcs, news, sems, a, j).wait()


def _comm_only(name, job):
    return _pcall(None, name, (), [], [], [], [], jobs=[job])[1][0]


class _ChipGatherJob(_GatherJob):
    def __init__(self, bufs):
        super().__init__(bufs, None)

    def _half(self, ref, k, h):
        return ref.at[k, h]


def _cast_own(place, ws):
    n = len(ws)

    def body(p_ref, *refs):
        for i_ref, o_ref in zip(refs[:n], refs[n:]):
            o_ref[...] = i_ref[...].astype(BF16)

    return pl.pallas_call(
        body, name="cast_own_shards",
        grid_spec=pltpu.PrefetchScalarGridSpec(
            num_scalar_prefetch=1, grid=(DEPTH,),
            in_specs=[pl.BlockSpec((None,) + a.shape[1:], lambda l, p: (l, 0, 0)) for a in ws],
            out_specs=[pl.BlockSpec((None, None) + a.shape[1:], lambda l, p: (l, p[1], 0, 0)) for a in ws]),
        out_shape=[SDS((DEPTH, 4) + a.shape[1:], BF16) for a in ws],
        compiler_params=_params("arbitrary"),
    )(place, *ws)


def _half_tiles(a_):
    rows = a_ // 2
    ta = min(rows, 256)
    return rows, ta, rows // ta


def _pair_sums_bf16(name, place, owns, recvs):
    n = len(owns)

    def body(p_ref, *refs):
        for own_ref, recv_ref, out_ref in zip(refs[:n], refs[n:2 * n], refs[2 * n:]):
            out_ref[...] = (own_ref[...] + recv_ref[...]).astype(BF16)

    def own_half(a):
        return pl.BlockSpec((None, a.shape[1] // 2, a.shape[2]), lambda s, p: (s, p[0], 0))

    def block(a):
        return pl.BlockSpec((None,) + a.shape[1:], lambda s, p: (s, 0, 0))

    return pl.pallas_call(
        body, name=name,
        grid_spec=pltpu.PrefetchScalarGridSpec(
            num_scalar_prefetch=1, grid=(4,),
            in_specs=[own_half(a) for a in owns] + [block(r) for r in recvs],
            out_specs=[block(r) for r in recvs]),
        out_shape=[SDS(r.shape, BF16) for r in recvs],
        compiler_params=_params("arbitrary"),
    )(place, *owns, *recvs)


def _shard_sums(name, place, owns, recvs, rbufs):
    n = len(owns)

    def body(p_ref, *refs):
        for own_ref, recv_ref, r_ref, out_ref in zip(refs[:n], refs[n:2 * n], refs[2 * n:3 * n], refs[3 * n:]):
            acc = own_ref[...] + recv_ref[...]
            for j in range(3):
                acc = acc + r_ref[j].astype(F32)
            out_ref[...] = acc

    def own_half(a):
        return pl.BlockSpec((None, a.shape[1] // 2, a.shape[2]), lambda i, p: (p[1], p[0], 0))

    def recv_block(a):
        return pl.BlockSpec((None,) + a.shape[1:], lambda i, p: (p[1], 0, 0))

    return pl.pallas_call(
        body, name=name,
        grid_spec=pltpu.PrefetchScalarGridSpec(
            num_scalar_prefetch=1, grid=(1,),
            in_specs=([own_half(a) for a in owns] + [recv_block(r) for r in recvs]
                      + [pl.BlockSpec(rb.shape, lambda i, p: (0, 0, 0)) for rb in rbufs]),
            out_specs=[pl.BlockSpec(r.shape[1:], lambda i, p: (0, 0)) for r in recvs]),
        out_shape=[SDS(r.shape[1:], F32) for r in recvs],
        compiler_params=_params("arbitrary"),
    )(place, *owns, *recvs, *rbufs)


def _small_pair_sum(place, mine, recv, dtypes):
    n = len(mine)

    def body(p_ref, *refs):
        for m_ref, r_ref, o_ref in zip(refs[:n], refs[n:2 * n], refs[2 * n:]):
            o_ref[...] = (m_ref[...] + r_ref[...]).astype(o_ref.dtype)

    def whole(a):
        zeros = (0,) * a.ndim
        return pl.BlockSpec(a.shape, lambda i, p: zeros)

    def mine_blk(a):
        zeros = (0,) * a.ndim
        return pl.BlockSpec((None,) + a.shape, lambda i, p: (p[1],) + zeros)

    return pl.pallas_call(
        body, name="small_pair_sum",
        grid_spec=pltpu.PrefetchScalarGridSpec(
            num_scalar_prefetch=1, grid=(1,),
            in_specs=[whole(a) for a in mine] + [whole(a) for a in recv],
            out_specs=[mine_blk(a) for a in mine]),
        out_shape=[SDS((4,) + a.shape, dt) for a, dt in zip(mine, dtypes)],
        compiler_params=_params("arbitrary"),
    )(place, *mine, *recv)


def _adam_math(w, g, m, v):
    m = B1 * m + (1.0 - B1) * g
    v = B2 * v + (1.0 - B2) * (g * g)
    m_hat = m / (1.0 - B1 ** STEP)
    v_hat = v / (1.0 - B2 ** STEP)
    delta = -LR * (m_hat / (jnp.sqrt(v_hat) + EPS_A) + WD * w)
    return delta, m, v


def _adamw_big(name, l, row0, w, m, v, mine, other, prev, jobs=()):
    _, _, b_ = w.shape
    _, ta, nh = _half_tiles(2 * mine.shape[0])
    prev = list(prev or [])

    def body(w_ref, m_ref, v_ref, mine_ref, other_ref, *rest):
        g_ref, d_ref, mo_ref, vo_ref = rest[len(prev):]
        g = jnp.where(pl.program_id(0) == lax.axis_index("c"), mine_ref[...], other_ref[...])
        g_ref[...] = g
        d_ref[...], mo_ref[...], vo_ref[...] = _adam_math(w_ref[...], g, m_ref[...], v_ref[...])

    slab = pl.BlockSpec((None, ta, b_), lambda h, i: (l, row0 // ta + h * nh + i, 0))
    half = pl.BlockSpec((ta, b_), lambda h, i: (i, 0))
    outs, res = _pcall(
        body, name, (2, nh), [slab, slab, slab, half, half] + [_ANY] * len(prev), [slab] * 4, [SDS(w.shape, F32)] * 4,
        (w, m, v, mine, other, *prev), aliases={5 + k: k for k in range(len(prev))}, jobs=jobs)
    return outs, res


def _adamw_small(name, ws, parts, ms, vs, loss_parts=None):
    k = len(ws)
    extra = [] if loss_parts is None else [loss_parts]

    def chip_sum(p_ref):
        p = [p_ref[k].astype(F32) for k in range(4)]
        return ((p[0] + p[1]) + p[2]) + p[3]

    def body(*refs):
        w_refs, p_refs, m_refs, v_refs = refs[:k], refs[k:2 * k], refs[2 * k:3 * k], refs[3 * k:4 * k]
        outs = refs[4 * k + len(extra):]
        if extra:
            outs[0][...] = chip_sum(refs[4 * k])
            outs = outs[1:]
        for a in range(k):
            g = chip_sum(p_refs[a])
            outs[a][...] = g
            outs[k + a][...], outs[2 * k + a][...], outs[3 * k + a][...] = _adam_math(
                w_refs[a][...], g, m_refs[a][...], v_refs[a][...])

    like = [SDS(a.shape, F32) for a in ws]
    return pl.pallas_call(
        body, name=name,
        out_shape=([SDS(loss_parts.shape[1:], F32)] if extra else []) + like * 4,
        compiler_params=pltpu.CompilerParams(vmem_limit_bytes=VMEM_LIMIT),
    )(*ws, *parts, *ms, *vs, *extra)


class _Reducer:
    def __init__(self, place, w, m, v):
        self.place, self.w, self.m, self.v = place, w, m, v
        self.active, self.riding = [], []
        self.big = {n: None for n in BIG}

    def add(self, l, tag, names, own, row0s=None, avoid=()):
        self.active.append(dict(l=l, key=f"{tag}_l{l}", names=names, own=list(own), row0s=row0s or [0] * len(names),
                                avoid=avoid, stage=0))

    def jobs(self, kind=""):
        self.riding = [g for g in self.active if kind not in g["avoid"]]
        return [(_SiblingJob(g["own"], True), _ScatterJob(g.get("parts", [])), _SiblingJob(g.get("shard", []), False))
                [g["stage"]] for g in self.riding]

    def land(self, res):
        for g, (_, news) in zip(self.riding, res):
            if g["stage"] == 0:
                g["recv"] = news
                g["parts"] = _pair_sums_bf16(f"pair_sums_{g['key']}", self.place, g["own"], news)
            elif g["stage"] == 1:
                g["shard"] = _shard_sums(f"shard_sums_{g['key']}", self.place, g["own"], g["recv"], news)
            else:
                for n, mine, other, row0 in zip(g["names"], g["shard"], news, g["row0s"]):
                    self.big[n] = _adamw_big(f"adamw_{n}_{row0}_{g['key']}", g["l"], row0, self.w[n], self.m[n], self.v[n],
                                             mine, other, self.big[n])[0]
                self.active.remove(g)
            g["stage"] += 1
        self.riding = []


WEIGHTS = ("norm_g", "w_in", "b_in", "ssm_log_dt", "ssm_lam_re", "ssm_lam_im", "ssm_b_re", "ssm_b_im", "ssm_c_re",
           "ssm_c_im", "ssm_d", "ssm_w_glu", "ssm_b_glu", "pool_w", "pool_scale", "w_branch_a", "w_branch_b", "w_out",
           "final_norm_g")
REPLICATED = SMALL + ("final_norm_g",)
DENSE = {"ssm_b_re": (DEPTH, G, P * C), "ssm_b_im": (DEPTH, G, P * C), "final_norm_g": (2, D // 2)}


def kernel(x, norm_g, w_in, b_in, ssm_log_dt, ssm_lam_re, ssm_lam_im, ssm_b_re, ssm_b_im, ssm_c_re, ssm_c_im, ssm_d, ssm_w_glu, ssm_b_glu, pool_w, pool_scale, w_branch_a, w_branch_b, w_out, final_norm_g, loss_target, m_norm_g, m_w_in, m_b_in, m_ssm_log_dt, m_ssm_lam_re, m_ssm_lam_im, m_ssm_b_re, m_ssm_b_im, m_ssm_c_re, m_ssm_c_im, m_ssm_d, m_ssm_w_glu, m_ssm_b_glu, m_pool_w, m_pool_scale, m_w_branch_a, m_w_branch_b, m_w_out, m_final_norm_g, v_norm_g, v_w_in, v_b_in, v_ssm_log_dt, v_ssm_lam_re, v_ssm_lam_im, v_ssm_b_re, v_ssm_b_im, v_ssm_c_re, v_ssm_c_im, v_ssm_d, v_ssm_w_glu, v_ssm_b_glu, v_pool_w, v_pool_scale, v_w_branch_a, v_w_branch_b, v_w_out, v_final_norm_g):
    w = dict(zip(WEIGHTS, (norm_g, w_in, b_in, ssm_log_dt, ssm_lam_re, ssm_lam_im, ssm_b_re, ssm_b_im, ssm_c_re,
                           ssm_c_im, ssm_d, ssm_w_glu, ssm_b_glu, pool_w, pool_scale, w_branch_a, w_branch_b, w_out,
                           final_norm_g)))
    m = dict(zip(WEIGHTS, (m_norm_g, m_w_in, m_b_in, m_ssm_log_dt, m_ssm_lam_re, m_ssm_lam_im, m_ssm_b_re, m_ssm_b_im,
                           m_ssm_c_re, m_ssm_c_im, m_ssm_d, m_ssm_w_glu, m_ssm_b_glu, m_pool_w, m_pool_scale,
                           m_w_branch_a, m_w_branch_b, m_w_out, m_final_norm_g)))
    v = dict(zip(WEIGHTS, (v_norm_g, v_w_in, v_b_in, v_ssm_log_dt, v_ssm_lam_re, v_ssm_lam_im, v_ssm_b_re, v_ssm_b_im,
                           v_ssm_c_re, v_ssm_c_im, v_ssm_d, v_ssm_w_glu, v_ssm_b_glu, v_pool_w, v_pool_scale,
                           v_w_branch_a, v_w_branch_b, v_w_out, v_final_norm_g)))
    place = jnp.stack([lax.axis_index("c"), 2 * lax.axis_index("x") + lax.axis_index("y")]).astype(jnp.int32)

    total_loss, dx, results = _step(x[0], loss_target[0], w, m, v, place)
    return (total_loss, dx[None], *[results[n][q] for q in range(4) for n in WEIGHTS])
```

```python
import functools

import jax
import jax.numpy as jnp
from jax import lax
from jax.experimental import pallas as pl
from jax.experimental.pallas import tpu as pltpu

F32, BF16 = jnp.float32, jnp.bfloat16
SDS = jax.ShapeDtypeStruct
MESH = pl.DeviceIdType.MESH

DEPTH = 2
L = 2048
D = 1024
NIN = 4096
W = 512
G, P, C = 32, 64, 16
GP = G * P
WINS = (2, 4, 8, 16)
TM = 256
NT = L // TM
TMM = 512
TK = 1024
EPS = 1e-6
VMEM_LIMIT = 56 * 2**20

LR, B1, B2, EPS_A, WD, STEP = 0.001, 0.9, 0.999, 1e-08, 0.01, 10


def _params(*sem):
    return pltpu.CompilerParams(dimension_semantics=sem, vmem_limit_bytes=VMEM_LIMIT)


_ANY = pl.BlockSpec(memory_space=pl.ANY)


def _full(shape):
    zeros = (0,) * len(shape)
    return pl.BlockSpec(shape, lambda *_: zeros)


def _layer(l, shape):
    zeros = (0,) * len(shape)
    return pl.BlockSpec((None,) + shape, lambda *_: (l,) + zeros)


def _rows(width, col=0, reverse=False, tm=TM):
    if reverse:
        return pl.BlockSpec((tm, width), lambda i: (L // tm - 1 - i, col))
    return pl.BlockSpec((tm, width), lambda i: (i, col))


def _rows_mm(width, col=0):
    return _rows(width, col, False, TMM)


def _pcall(body, name, grid, in_specs, out_specs, out_shape, args, scratch=(), aliases=None, jobs=()):
    in_specs, out_specs, out_shape, args, scratch = list(in_specs), list(out_specs), list(out_shape), list(args), list(scratch)
    aliases = dict(aliases or {})
    jobs = [j for j in jobs if j is not None]
    n_in, n_out, n_scr = len(in_specs), len(out_specs), len(scratch)
    srcs = [s for j in jobs for s in j.srcs]
    bufs = [b for j in jobs for b in j.bufs]
    news = [s for j in jobs for s in j.news]
    aliases.update({n_in + len(srcs) + k: n_out + k for k in range(len(bufs))})

    def hosted(*refs):
        cuts = [n_in, len(srcs), len(bufs), n_out, len(bufs), len(news), n_scr]
        parts, p = [], 0
        for n in cuts:
            parts.append(refs[p:p + n])
            p += n
        ins, src_r, _, outs, buf_r, new_r, scr = parts
        sem_r = refs[p:]
        views, ps, pb, pn, pm = [], 0, 0, 0, 0
        for j in jobs:
            views.append((src_r[ps:ps + len(j.srcs)], buf_r[pb:pb + len(j.bufs)], new_r[pn:pn + len(j.news)],
                          sem_r[pm:pm + len(j.scratch)]))
            ps, pb, pn, pm = ps + len(j.srcs), pb + len(j.bufs), pn + len(j.news), pm + len(j.scratch)

        def run(phase):
            for j, v in zip(jobs, views):
                getattr(j, phase)(*v)

        def at_step(step):
            return functools.reduce(jnp.logical_and, [pl.program_id(d) == step(d) for d in range(len(grid))])

        if not grid:
            run("start")
            run("finish")
            return
        pl.when(at_step(lambda d: 0))(lambda: run("start"))
        body(*ins, *outs, *scr)
        pl.when(at_step(lambda d: grid[d] - 1))(lambda: run("finish"))

    outs = pl.pallas_call(
        hosted if jobs else body, name=name, **({"grid": grid} if grid else {}),
        in_specs=in_specs + [_ANY] * (len(srcs) + len(bufs)), out_specs=out_specs + [_ANY] * (len(bufs) + len(news)),
        out_shape=out_shape + [SDS(b.shape, b.dtype) for b in bufs] + news,
        input_output_aliases=aliases, scratch_shapes=scratch + [s for j in jobs for s in j.scratch],
        compiler_params=_params(*(("arbitrary",) * len(grid))))(*args, *srcs, *bufs)
    res, pb, pn = [], n_out, n_out + len(bufs)
    for j in jobs:
        res.append((list(outs[pb:pb + len(j.bufs)]), list(outs[pn:pn + len(j.news)])))
        pb, pn = pb + len(j.bufs), pn + len(j.news)
    return list(outs[:n_out]), res


def _dot(a, b):
    return jnp.dot(a, b, preferred_element_type=F32)


def _dot_nt(a, b):
    return lax.dot_general(a, b, (((1,), (1,)), ((), ())), preferred_element_type=F32)


def _dot_tn(a, b):
    return lax.dot_general(a, b, (((0,), (0,)), ((), ())), preferred_element_type=F32)


_K0 = 0.7978845608028654
_K1 = 0.044715


def _gelu(x):
    return 0.5 * x * (1.0 + jnp.tanh(_K0 * (x + _K1 * (x * x * x))))


def _gelu_grad(x):
    t = jnp.tanh(_K0 * (x + _K1 * (x * x * x)))
    return 0.5 * (1.0 + t) + 0.5 * x * (1.0 - t * t) * (_K0 * (1.0 + 3.0 * _K1 * x * x))


def _sigmoid(x):
    return jax.nn.sigmoid(x)


def _norm_inproj(l, x, g, w, b, jobs=()):
    def body(x_ref, g_ref, w_ref, b_ref, h_ref, proj_ref):
        xv = x_ref[...]
        r = lax.rsqrt(jnp.mean(xv * xv, axis=-1, keepdims=True) + EPS)
        hb = ((xv * r) * g_ref[...]).astype(BF16)
        h_ref[...] = hb
        for j in range(4):
            cs = slice(j * 1024, (j + 1) * 1024)
            proj_ref[:, cs] = _dot(hb, w_ref[j]) + b_ref[:, cs]

    return _pcall(
        body, f"norm_inproj_l{l}", (L // TMM,),
        [_rows_mm(D), _layer(l, (1, D)), _layer(l, (4, D, 1024)), _layer(l, (1, NIN))],
        [_rows_mm(D), _rows_mm(NIN)],
        [SDS((L, D), BF16), SDS((L, NIN), F32)],
        (x, g, w, b), jobs=jobs)


def _scan_tile(re_ref, im_ref, st_re, st_im, cr_re, cr_im, carry, reverse):
    def chunk(ci, carry):
        c = (TM // 8 - 1 - ci) if reverse else ci
        rows = pl.ds(pl.multiple_of(c * 8, 8), 8)
        new = []
        for lb in range(GP // 512):
            cols = slice(lb * 512, (lb + 1) * 512)
            vr = re_ref[rows, cols]
            vi = im_ref[rows, cols]
            for s, d in enumerate((1, 2, 4)):
                ar = st_re[8 * s:8 * s + 8, cols]
                ai = st_im[8 * s:8 * s + 8, cols]
                sr = pltpu.roll(vr, 8 - d if reverse else d, 0)
                si = pltpu.roll(vi, 8 - d if reverse else d, 0)
                vr, vi = vr + ar * sr - ai * si, vi + ar * si + ai * sr
            cr, ci_ = carry[2 * lb], carry[2 * lb + 1]
            pr = cr_re[:, cols]
            pi = cr_im[:, cols]
            vr, vi = vr + pr * cr - pi * ci_, vi + pr * ci_ + pi * cr
            re_ref[rows, cols] = vr
            im_ref[rows, cols] = vi
            if reverse:
                new += [vr[0:1], vi[0:1]]
            else:
                new += [vr[7:8], vi[7:8]]
        return tuple(new)

    return lax.fori_loop(0, TM // 8, chunk, carry)


def _load_carry(car_ref):
    return tuple(car_ref[r:r + 1, lb * 512:(lb + 1) * 512] for lb in range(GP // 512) for r in (0, 1))


def _store_carry(car_ref, carry):
    for lb in range(GP // 512):
        car_ref[0:1, lb * 512:(lb + 1) * 512] = carry[2 * lb]
        car_ref[1:2, lb * 512:(lb + 1) * 512] = carry[2 * lb + 1]


def _s5_fwd(l, proj, bexp, cre, cimn, powers, dsk, wglu, bglu, job=None):
    def body(ua_ref, za_ref, bexp_ref, cre_ref, cimn_ref, st_re_ref, st_im_ref, cr_re_ref, cr_im_ref,
             d_ref, wg_ref, bg_ref, sre_ref, sim_ref, y1_ref, q_ref, ya_ref, car_ref):
        @pl.when(pl.program_id(0) == 0)
        def _():
            car_ref[...] = jnp.zeros_like(car_ref)

        u = ua_ref[...]
        ub = u.astype(BF16)
        for k in range(4):
            bu = _dot(ub[:, 128 * k:128 * (k + 1)], bexp_ref[k])
            sre_ref[:, 512 * k:512 * (k + 1)] = bu[:, :512]
            sim_ref[:, 512 * k:512 * (k + 1)] = bu[:, 512:]
        carry = _scan_tile(sre_ref, sim_ref, st_re_ref, st_im_ref, cr_re_ref, cr_im_ref, _load_carry(car_ref), False)
        _store_carry(car_ref, carry)
        for k in range(4):
            blk = slice(512 * k, 512 * (k + 1))
            ks = slice(128 * k, 128 * (k + 1))
            y0 = _dot(sre_ref[:, blk].astype(BF16), cre_ref[k]) + _dot(sim_ref[:, blk].astype(BF16), cimn_ref[k])
            y1_ref[:, ks] = y0 + d_ref[:, ks] * u[:, ks]
        y2 = _gelu(y1_ref[...])
        q = _dot(y2.astype(BF16), wg_ref[...]) + bg_ref[...]
        q_ref[...] = q
        za = za_ref[...]
        ya_ref[...] = ((y2 * _sigmoid(q)) * (za * _sigmoid(za))).astype(BF16)

    return _pcall(
        body, f"s5_fwd_l{l}", (NT,),
        [_rows(W, 0), _rows(W, 1), _layer(l, (4, 128, 1024)), _layer(l, (4, 512, 128)), _layer(l, (4, 512, 128)),
         _layer(l, (24, GP)), _layer(l, (24, GP)), _layer(l, (8, GP)), _layer(l, (8, GP)), _layer(l, (1, W)),
         _layer(l, (W, W)), _layer(l, (1, W))],
        [_rows(GP), _rows(GP), _rows(W), _rows(W), _rows(W)],
        [SDS((L, GP), F32), SDS((L, GP), F32), SDS((L, W), F32), SDS((L, W), F32), SDS((L, W), BF16)],
        (proj, proj, bexp, cre, cimn, *powers, dsk, wglu, bglu),
        scratch=[pltpu.VMEM((8, GP), F32)], jobs=[job])


def _pool_fwd(l, proj, pw, scale):
    def body(ub_ref, zb_ref, pw_ref, sc_ref, pooled_ref, mixed_ref, yb_ref, buf):
        i = pl.program_id(0)

        @pl.when(i == 0)
        def _():
            buf[0:16, :] = jnp.zeros((16, W), F32)

        u = ub_ref[...]
        buf[16:16 + TM, :] = u
        t = i * TM + lax.broadcasted_iota(jnp.int32, (TM, 128), 0)
        for gi, win in enumerate(WINS):
            cs = slice(128 * gi, 128 * (gi + 1))
            acc = u[:, cs]
            for k in range(1, win):
                acc = acc + buf[16 - k:16 - k + TM, cs]
            cnt = jnp.minimum(t + 1, win).astype(F32)
            pb = (acc / cnt - u[:, cs]).astype(BF16)
            pooled_ref[:, cs] = pb
            mixed_ref[:, cs] = _dot(pb, pw_ref[gi])
        zb = zb_ref[...]
        yb_ref[...] = ((mixed_ref[...] * sc_ref[...]) * (zb * _sigmoid(zb))).astype(BF16)
        buf[0:16, :] = buf[TM:TM + 16, :]

    return pl.pallas_call(
        body, name=f"pool_fwd_l{l}", grid=(NT,),
        in_specs=[_rows(W, 2), _rows(W, 3), _layer(l, (4, 128, 128)), _layer(l, (1, W))],
        out_specs=[_rows(W), _rows(W), _rows(W)],
        out_shape=[SDS((L, W), BF16), SDS((L, W), F32), SDS((L, W), BF16)],
        scratch_shapes=[pltpu.VMEM((TM + 16, W), F32)],
        compiler_params=_params("arbitrary"),
    )(proj, proj, pw, scale)


def _merge_out(l, ya, yb, proj, x, wa, wb, wo):
    def body(ya_ref, yb_ref, ga_ref, gb_ref, x_ref, wa_ref, wb_ref, wo_ref, pa_ref, pb_ref, mg_ref, xo_ref):
        ya = ya_ref[...]
        yb = yb_ref[...]
        for j in range(4):
            cs = slice(256 * j, 256 * (j + 1))
            pa_ref[:, cs] = _dot(ya, wa_ref[j])
            pb_ref[:, cs] = _dot(yb, wb_ref[j])
        merged = _sigmoid(ga_ref[...]) * pa_ref[...] + _sigmoid(gb_ref[...]) * pb_ref[...]
        mb = merged.astype(BF16)
        mg_ref[...] = mb
        xo_ref[...] = x_ref[...] + _dot(mb, wo_ref[...])

    return pl.pallas_call(
        body, name=f"merge_out_l{l}", grid=(L // TMM,),
        in_specs=[_rows_mm(W), _rows_mm(W), _rows_mm(D, 2), _rows_mm(D, 3), _rows_mm(D),
                  _layer(l, (4, W, 256)), _layer(l, (4, W, 256)), _layer(l, (D, D))],
        out_specs=[_rows_mm(D), _rows_mm(D), _rows_mm(D), _rows_mm(D)],
        out_shape=[SDS((L, D), F32), SDS((L, D), F32), SDS((L, D), BF16), SDS((L, D), F32)],
        compiler_params=_params("arbitrary"),
    )(ya, yb, proj, proj, x, wa, wb, wo)


def _loss_head(x, gf, target):
    def body(x_ref, g_ref, t_ref, loss_ref, dx_ref, dg_ref):
        @pl.when(pl.program_id(0) == 0)
        def _():
            loss_ref[...] = jnp.zeros_like(loss_ref)
            dg_ref[...] = jnp.zeros_like(dg_ref)

        xv = x_ref[...]
        g = g_ref[...]
        r = lax.rsqrt(jnp.mean(xv * xv, axis=-1, keepdims=True) + EPS)
        xn = xv * r
        err = xn * g - t_ref[...]
        part = jnp.sum(jnp.mean(err * err, axis=-1, keepdims=True), axis=0, keepdims=True)
        loss_ref[...] += 0.5 * part
        dy = err * (1.0 / D)
        dg_ref[...] += jnp.sum(dy * xn, axis=0, keepdims=True)
        dxn = dy * g
        dx_ref[...] = r * (dxn - xn * jnp.mean(dxn * xn, axis=-1, keepdims=True))

    return pl.pallas_call(
        body, name="loss_head", grid=(NT,),
        in_specs=[_rows(D), _full((1, D)), _rows(D)],
        out_specs=[_full((2, 128)), _rows(D), _full((1, D))],
        out_shape=[SDS((2, 128), F32), SDS((L, D), F32), SDS((1, D), F32)],
        compiler_params=_params("arbitrary"),
    )(x, gf, target)


def _merge_out_bwd(l, dxn, mg, proj, pa, pb, ya, yb, wo, wa, wb, jobs=()):
    def body(dx_ref, mg_ref, ga_ref, gb_ref, pa_ref, pb_ref, ya_ref, yb_ref, wo_ref, wa_ref, wb_ref,
             dg_ref, dya_ref, dyb_ref, dwo_ref, dwa_ref, dwb_ref, dbias_ref):
        @pl.when(pl.program_id(0) == 0)
        def _():
            for ref in (dwo_ref, dwa_ref, dwb_ref, dbias_ref):
                ref[...] = jnp.zeros_like(ref)

        dxb = dx_ref[...].astype(BF16)
        dm = _dot_nt(dxb, wo_ref[...])
        sa = _sigmoid(ga_ref[...])
        sb = _sigmoid(gb_ref[...])
        dga = dm * pa_ref[...] * (sa * (1.0 - sa))
        dgb = dm * pb_ref[...] * (sb * (1.0 - sb))
        dg_ref[:, :D] = dga.astype(BF16)
        dg_ref[:, D:] = dgb.astype(BF16)
        dbias_ref[:, :D] += jnp.sum(dga, axis=0, keepdims=True)
        dbias_ref[:, D:] += jnp.sum(dgb, axis=0, keepdims=True)
        dpa = (dm * sa).astype(BF16)
        dpb = (dm * sb).astype(BF16)
        ya = ya_ref[...]
        yb = yb_ref[...]
        dya = jnp.zeros((TM, W), F32)
        dyb = jnp.zeros((TM, W), F32)
        for j in range(4):
            cs = slice(256 * j, 256 * (j + 1))
            dya = dya + _dot_nt(dpa[:, cs], wa_ref[j])
            dyb = dyb + _dot_nt(dpb[:, cs], wb_ref[j])
            dwa_ref[j] += _dot_tn(ya, dpa[:, cs])
            dwb_ref[j] += _dot_tn(yb, dpb[:, cs])
        dya_ref[...] = dya
        dyb_ref[...] = dyb
        dwo_ref[...] += _dot_tn(mg_ref[...], dxb)

    return _pcall(
        body, f"merge_out_bwd_l{l}", (NT,),
        [_rows(D), _rows(D), _rows(D, 2), _rows(D, 3), _rows(D), _rows(D), _rows(W), _rows(W),
         _layer(l, (D, D)), _layer(l, (4, W, 256)), _layer(l, (4, W, 256))],
        [_rows(2 * D, 1), _rows(W), _rows(W), _full((D, D)), _full((4, W, 256)), _full((4, W, 256)), _full((1, 2 * D))],
        [SDS((L, NIN), BF16), SDS((L, W), F32), SDS((L, W), F32),
         SDS((D, D), F32), SDS((4, W, 256), F32), SDS((4, W, 256), F32), SDS((1, 2 * D), F32)],
        (dxn, mg, proj, proj, pa, pb, ya, yb, wo, wa, wb), jobs=jobs)


def _pool_bwd(l, dyb, proj, mixed, pooled, pw, scale, dproj, jobs=()):
    def body(dyb_ref, zb_ref, mixed_ref, pooled_ref, pw_ref, sc_ref, _, db_ref, dpw_ref, dsc_ref, dbias_ref, buf):
        i = pl.program_id(0)
        tile = NT - 1 - i

        @pl.when(i == 0)
        def _():
            dpw_ref[...] = jnp.zeros_like(dpw_ref)
            dsc_ref[...] = jnp.zeros_like(dsc_ref)
            dbias_ref[...] = jnp.zeros_like(dbias_ref)
            buf[TM:TM + 16, :] = jnp.zeros((16, W), F32)

        dyb = dyb_ref[...]
        zb = zb_ref[...]
        mixed = mixed_ref[...]
        sc = sc_ref[...]
        sg = _sigmoid(zb)
        dyb0 = dyb * (zb * sg)
        dzb = dyb * (mixed * sc) * (sg * (1.0 + zb * (1.0 - sg)))
        db_ref[:, W:] = dzb.astype(BF16)
        dbias_ref[:, W:] += jnp.sum(dzb, axis=0, keepdims=True)
        dsc_ref[...] += jnp.sum(dyb0 * mixed, axis=0, keepdims=True)
        dmix = (dyb0 * sc).astype(BF16)
        t = tile * TM + lax.broadcasted_iota(jnp.int32, (TM, 128), 0)
        for gi, win in enumerate(WINS):
            cs = slice(128 * gi, 128 * (gi + 1))
            dpw_ref[gi] += _dot_tn(pooled_ref[:, cs], dmix[:, cs])
            dpool = _dot_nt(dmix[:, cs], pw_ref[gi])
            cnt = jnp.minimum(t + 1, win).astype(F32)
            e = dpool / cnt
            buf[0:TM, cs] = e
            acc = e - dpool
            for k in range(1, win):
                acc = acc + buf[k:k + TM, cs]
            db_ref[:, cs] = acc.astype(BF16)
            dbias_ref[:, cs] += jnp.sum(acc, axis=0, keepdims=True)
        buf[TM:TM + 16, :] = buf[0:16, :]

    return _pcall(
        body, f"pool_bwd_l{l}", (NT,),
        [_rows(W, 0, True), _rows(W, 3, True), _rows(W, 0, True), _rows(W, 0, True),
         _layer(l, (4, 128, 128)), _layer(l, (1, W)), _ANY],
        [_rows(2 * W, 1, True), _full((4, 128, 128)), _full((1, W)), _full((1, 2 * W))],
        [SDS((L, NIN), BF16), SDS((4, 128, 128), F32), SDS((1, W), F32), SDS((1, 2 * W), F32)],
        (dyb, proj, mixed, pooled, pw, scale, dproj),
        scratch=[pltpu.VMEM((TM + 16, W), F32)], aliases={6: 0}, jobs=jobs)


def _s5_bwd(l, dya, proj, y1, q, sre, sim, cret, cimnt, bret, bimt, st_re, st_im, cr_re, cr_im, dsk, wglu, dproj, jobs=()):
    def halo(i):
        return (jnp.maximum((NT - 1 - i) * (TM // 8) - 1, 0), 0)

    def body(dya_ref, ua_ref, za_ref, y1_ref, q_ref, sre_ref, sim_ref, hre_ref, him_ref,
             cret_ref, cimnt_ref, bret_ref, bimt_ref, st_re_ref, st_im_ref, cr_re_ref, cr_im_ref, d_ref, wg_ref, _,
             da_ref, dwg_ref, dbg_ref, dd_ref, dcre_ref, dcimn_ref, dbre_ref, dbim_ref, dare_ref, daim_ref, dbias_ref,
             lre, lim, car_ref):
        i = pl.program_id(0)
        tile = NT - 1 - i

        @pl.when(i == 0)
        def _():
            for ref in (dwg_ref, dbg_ref, dd_ref, dcre_ref, dcimn_ref, dbre_ref, dbim_ref, dare_ref, daim_ref, dbias_ref,
                        car_ref):
                ref[...] = jnp.zeros_like(ref)

        u = ua_ref[...]
        za = za_ref[...]
        y1 = y1_ref[...]
        dya = dya_ref[...]
        y2 = _gelu(y1)
        sg = _sigmoid(q_ref[...])
        sgz = _sigmoid(za)
        dy3 = dya * (za * sgz)
        dza = dya * (y2 * sg) * (sgz * (1.0 + za * (1.0 - sgz)))
        da_ref[:, W:] = dza.astype(BF16)
        dbias_ref[:, W:] += jnp.sum(dza, axis=0, keepdims=True)
        dq = dy3 * y2 * (sg * (1.0 - sg))
        dqb = dq.astype(BF16)
        dy2 = dy3 * sg + _dot_nt(dqb, wg_ref[...])
        dwg_ref[...] += _dot_tn(y2.astype(BF16), dqb)
        dbg_ref[...] += jnp.sum(dq, axis=0, keepdims=True)
        dy1 = dy2 * _gelu_grad(y1)
        dd_ref[...] += jnp.sum(dy1 * u, axis=0, keepdims=True)
        dy1b = dy1.astype(BF16)
        ub = u.astype(BF16)
        for k in range(4):
            blk = slice(512 * k, 512 * (k + 1))
            ks = slice(128 * k, 128 * (k + 1))
            lre[:, blk] = _dot(dy1b[:, ks], cret_ref[k])
            lim[:, blk] = _dot(dy1b[:, ks], cimnt_ref[k])
            dcre_ref[k] += _dot_tn(sre_ref[:, blk].astype(BF16), dy1b[:, ks])
            dcimn_ref[k] += _dot_tn(sim_ref[:, blk].astype(BF16), dy1b[:, ks])
        carry = _scan_tile(lre, lim, st_re_ref, st_im_ref, cr_re_ref, cr_im_ref, _load_carry(car_ref), True)
        _store_carry(car_ref, carry)

        rowid = lax.broadcasted_iota(jnp.int32, (8, 512), 0)
        gate = (tile > 0).astype(F32)

        def chunk(c, _):
            rows = pl.ds(pl.multiple_of(c * 8, 8), 8)
            prows = pl.ds(pl.multiple_of(jnp.maximum(c - 1, 0) * 8, 8), 8)
            for lb in range(GP // 512):
                cols = slice(lb * 512, (lb + 1) * 512)
                sr = sre_ref[rows, cols]
                si = sim_ref[rows, cols]
                pr = jnp.where(c == 0, hre_ref[7:8, cols] * gate, sre_ref[prows, cols][7:8])
                pi = jnp.where(c == 0, him_ref[7:8, cols] * gate, sim_ref[prows, cols][7:8])
                sr = jnp.where(rowid == 0, pr, pltpu.roll(sr, 1, 0))
                si = jnp.where(rowid == 0, pi, pltpu.roll(si, 1, 0))
                lr = lre[rows, cols]
                li = lim[rows, cols]
                dare_ref[:, cols] += sr * lr + si * li
                daim_ref[:, cols] += sr * li - si * lr
            return 0

        lax.fori_loop(0, TM // 8, chunk, 0)

        for k in range(4):
            blk = slice(512 * k, 512 * (k + 1))
            ks = slice(128 * k, 128 * (k + 1))
            lrb = lre[:, blk].astype(BF16)
            lib = lim[:, blk].astype(BF16)
            du = dy1[:, ks] * d_ref[:, ks] + _dot(lrb, bret_ref[k]) + _dot(lib, bimt_ref[k])
            da_ref[:, ks] = du.astype(BF16)
            dbias_ref[:, ks] += jnp.sum(du, axis=0, keepdims=True)
            dbre_ref[k] += _dot_tn(ub[:, ks], lrb)
            dbim_ref[k] += _dot_tn(ub[:, ks], lib)

    return _pcall(
        body, f"s5_bwd_l{l}", (NT,),
        [_rows(W, 0, True), _rows(W, 0, True), _rows(W, 1, True), _rows(W, 0, True), _rows(W, 0, True),
         _rows(GP, 0, True), _rows(GP, 0, True),
         pl.BlockSpec((8, GP), halo), pl.BlockSpec((8, GP), halo),
         _layer(l, (4, 128, 512)), _layer(l, (4, 128, 512)), _layer(l, (4, 512, 128)), _layer(l, (4, 512, 128)),
         _layer(l, (24, GP)), _layer(l, (24, GP)), _layer(l, (8, GP)), _layer(l, (8, GP)), _layer(l, (1, W)),
         _layer(l, (W, W)), _ANY],
        [_rows(2 * W, 0, True), _full((W, W)), _full((1, W)), _full((1, W)),
         _full((4, 512, 128)), _full((4, 512, 128)), _full((4, 128, 512)), _full((4, 128, 512)),
         _full((8, GP)), _full((8, GP)), _full((1, 2 * W))],
        [SDS((L, NIN), BF16), SDS((W, W), F32), SDS((1, W), F32), SDS((1, W), F32),
         SDS((4, 512, 128), F32), SDS((4, 512, 128), F32), SDS((4, 128, 512), F32), SDS((4, 128, 512), F32),
         SDS((8, GP), F32), SDS((8, GP), F32), SDS((1, 2 * W), F32)],
        (dya, proj, proj, y1, q, sre, sim, sre, sim, cret, cimnt, bret, bimt, st_re, st_im, cr_re, cr_im, dsk, wglu,
         dproj),
        scratch=[pltpu.VMEM((TM, GP), F32), pltpu.VMEM((TM, GP), F32), pltpu.VMEM((8, GP), F32)],
        aliases={19: 0}, jobs=jobs)


def _inproj_dw(l, r, h, dproj, jobs=()):
    def body(h_ref, dp_ref, dw_ref):
        part = _dot_tn(h_ref[...], dp_ref[...])

        @pl.when(pl.program_id(1) == 0)
        def _():
            dw_ref[...] = part

        @pl.when(pl.program_id(1) > 0)
        def _():
            dw_ref[...] += part

    return _pcall(
        body, f"inproj_dw{r}_l{l}", (4, L // TK),
        [pl.BlockSpec((TK, D // 2), lambda j, i: (i, r)), pl.BlockSpec((TK, 1024), lambda j, i: (i, j))],
        [pl.BlockSpec((None, D // 2, 1024), lambda j, i: (j, 0, 0))],
        [SDS((4, D // 2, 1024), F32)],
        (h, dproj), jobs=jobs)


def _inproj_dx(l, dproj, w, x, g, dxn, jobs=()):
    def body(dp_ref, w_ref, x_ref, g_ref, dxn_ref, dx_ref, dg_ref):
        @pl.when(pl.program_id(0) == 0)
        def _():
            dg_ref[...] = jnp.zeros_like(dg_ref)

        dh = _dot_nt(dp_ref[:, 0:1024], w_ref[0])
        for j in range(1, 4):
            dh = dh + _dot_nt(dp_ref[:, j * 1024:(j + 1) * 1024], w_ref[j])
        xv = x_ref[...]
        r = lax.rsqrt(jnp.mean(xv * xv, axis=-1, keepdims=True) + EPS)
        xn = xv * r
        dg_ref[...] += jnp.sum(dh * xn, axis=0, keepdims=True)
        dn = dh * g_ref[...]
        dx_ref[...] = dxn_ref[...] + r * (dn - xn * jnp.mean(dn * xn, axis=-1, keepdims=True))

    return _pcall(
        body, f"inproj_dx_l{l}", (L // TMM,),
        [_rows_mm(NIN), _layer(l, (4, D, 1024)), _rows_mm(D), _layer(l, (1, D)), _rows_mm(D)],
        [_rows_mm(D), _full((1, D))],
        [SDS((L, D), F32), SDS((1, D), F32)],
        (dproj, w, x, g, dxn), jobs=jobs)


def _discretize(log_dt, lam_re, lam_im, b_re, b_im):
    dt = jnp.exp(log_dt)[..., None]
    mag = jnp.exp(lam_re * dt)
    ang = lam_im * dt
    abar_re = mag * jnp.cos(ang)
    abar_im = mag * jnp.sin(ang)
    num_re = abar_re - 1.0
    num_im = abar_im
    den = lam_re * lam_re + lam_im * lam_im
    coef_re = (num_re * lam_re + num_im * lam_im) / den
    coef_im = (num_im * lam_re - num_re * lam_im) / den
    bbar_re = coef_re[..., None] * b_re - coef_im[..., None] * b_im
    bbar_im = coef_re[..., None] * b_im + coef_im[..., None] * b_re
    return abar_re, abar_im, bbar_re, bbar_im


def _powers(abar_re, abar_im):
    ar, ai = abar_re.reshape(DEPTH, 1, GP), abar_im.reshape(DEPTH, 1, GP)
    rows_re, rows_im = [ar], [ai]
    for _ in range(7):
        pr, pi = rows_re[-1], rows_im[-1]
        rows_re.append(pr * ar - pi * ai)
        rows_im.append(pr * ai + pi * ar)
    row = jnp.arange(8)[:, None]

    def steps(rows, keep):
        return jnp.concatenate([jnp.where(keep(d), rows[d - 1], 0.0) for d in (1, 2, 4)], axis=1)

    neg_im = [-r for r in rows_im]
    fwd = (steps(rows_re, lambda d: row >= d), steps(rows_im, lambda d: row >= d),
           jnp.concatenate(rows_re, axis=1), jnp.concatenate(rows_im, axis=1))
    rev = (steps(rows_re, lambda d: row < 8 - d), steps(neg_im, lambda d: row < 8 - d),
           jnp.concatenate(rows_re[::-1], axis=1), jnp.concatenate(neg_im[::-1], axis=1))
    return fwd, rev


_EYE8 = functools.partial(jnp.eye, 8, dtype=F32)


def _expand_in(b):
    return jnp.einsum("lkgpc,gh->lkgchp", b.reshape(DEPTH, 4, 8, P, C), _EYE8()).reshape(DEPTH, 4, 128, 512)


def _extract_in(e):
    return jnp.einsum("lkgchp,gh->lkgpc", e.reshape(DEPTH, 4, 8, C, 8, P), _EYE8()).reshape(DEPTH, G, P, C)


def _expand_out(c):
    return jnp.einsum("lkgcp,gh->lkgphc", c.reshape(DEPTH, 4, 8, C, P), _EYE8()).reshape(DEPTH, 4, 512, 128)


def _extract_out(e):
    return jnp.einsum("lkgphc,gh->lkgcp", e.reshape(DEPTH, 4, 8, P, 8, C), _EYE8()).reshape(DEPTH, G, C, P)


SMALL = ("norm_g", "b_in", "ssm_log_dt", "ssm_lam_re", "ssm_lam_im", "ssm_b_re", "ssm_b_im",
         "ssm_c_re", "ssm_c_im", "ssm_d", "ssm_b_glu", "pool_w", "pool_scale")
BIG = ("w_in", "ssm_w_glu", "w_branch_a", "w_branch_b", "w_out")


def _step(x, target, w, m, v, place):
    sp = {n: w[n] for n in SMALL}
    final_norm_g = w["final_norm_g"]
    wbuf = dict(zip(BIG, _cast_own(place, [w[n] for n in BIG])))
    (abar_re, abar_im, bbar_re, bbar_im), disc_vjp = jax.vjp(
        _discretize, *(sp[n] for n in ("ssm_log_dt", "ssm_lam_re", "ssm_lam_im", "ssm_b_re", "ssm_b_im")))
    powers_fwd, powers_rev = _powers(abar_re, abar_im)
    b_re_x, b_im_x = _expand_in(bbar_re), _expand_in(bbar_im)
    c_re_x, c_imn_x = _expand_out(sp["ssm_c_re"]), _expand_out(-sp["ssm_c_im"])
    b_x = jnp.concatenate([b_re_x, b_im_x], axis=3).astype(BF16)
    t = lambda a: jnp.swapaxes(a, 2, 3).astype(BF16)
    c_re_t, c_imn_t, b_re_t, b_im_t = t(c_re_x), t(c_imn_x), t(b_re_x), t(b_im_x)
    c_re_x, c_imn_x = c_re_x.astype(BF16), c_imn_x.astype(BF16)
    row = lambda n: sp[n].reshape(DEPTH, 1, -1)
    g, b_in, dsk, b_glu, scale = row("norm_g"), row("b_in"), row("ssm_d"), row("ssm_b_glu"), row("pool_scale")
    pw = sp["pool_w"].astype(BF16)

    saved = []
    for l in range(DEPTH):
        three = BIG[2:]
        if l == 0:
            wbuf["w_in"], wbuf["ssm_w_glu"] = _comm_only(
                "gather_first", _GatherJob([wbuf["w_in"], wbuf["ssm_w_glu"]], 0))[0]
            jobs = [_GatherJob([wbuf[n] for n in three], 0), _GatherJob([wbuf["ssm_w_glu"]], 1)]
        else:
            jobs = []
        (h, proj), res = _norm_inproj(l, x, g, wbuf["w_in"], b_in, jobs)
        if res:
            wbuf.update(zip(three, res[0][0]))
            (wbuf["ssm_w_glu"],) = res[1][0]
        wg = dict(wbuf, ssm_w_glu=wbuf["ssm_w_glu"].reshape(DEPTH, W, W), w_out=wbuf["w_out"].reshape(DEPTH, D, D))
        job = _GatherJob([wbuf["w_in"]], l + 1) if l + 1 < DEPTH else _GatherJob([wbuf[n] for n in three], l)
        (sre, sim, y1, q, ya), res = _s5_fwd(
            l, proj, b_x, c_re_x, c_imn_x, powers_fwd, dsk, wg["ssm_w_glu"], b_glu, job)
        if l + 1 < DEPTH:
            (wbuf["w_in"],) = res[0][0]
        else:
            wbuf.update(zip(three, res[0][0]))
        wg = dict(wbuf, ssm_w_glu=wbuf["ssm_w_glu"].reshape(DEPTH, W, W), w_out=wbuf["w_out"].reshape(DEPTH, D, D))
        pooled, mixed, yb = _pool_fwd(l, proj, pw, scale)
        pa, pb, mg, x_next = _merge_out(l, ya, yb, proj, x, wg["w_branch_a"], wg["w_branch_b"], wg["w_out"])
        saved.append(dict(x=x, h=h, proj=proj, sre=sre, sim=sim, y1=y1, q=q, ya=ya,
                          pooled=pooled, mixed=mixed, yb=yb, pa=pa, pb=pb, mg=mg))
        x = x_next

    loss, dx, dgf = _loss_head(x, final_norm_g.reshape(1, D), target)

    per_layer = {n: [None] * DEPTH for n in ("norm_g", "b_in", "ssm_d", "ssm_b_glu", "pool_w", "pool_scale",
                                             "dare", "daim", "dbre", "dbim", "dcre", "dcimn")}
    red = _Reducer(place, w, m, v)
    for l in reversed(range(DEPTH)):
        s = saved[l]
        (dproj, dya, dyb, dwo, dwa, dwb, dbias_g), res = _merge_out_bwd(
            l, dx, s["mg"], s["proj"], s["pa"], s["pb"], s["ya"], s["yb"],
            wg["w_out"], wg["w_branch_a"], wg["w_branch_b"], red.jobs())
        red.land(res)
        (dproj, dpw, dsc, dbias_b), res = _pool_bwd(l, dyb, s["proj"], s["mixed"], s["pooled"], pw, scale, dproj,
                                                    red.jobs("pool_bwd"))
        red.land(res)
        (dproj, dwg, dbg, dd, dcre, dcimn, dbre, dbim, dare, daim, dbias_a), res = _s5_bwd(
            l, dya, s["proj"], s["y1"], s["q"], s["sre"], s["sim"], c_re_t, c_imn_t, b_re_t, b_im_t,
            *powers_rev, dsk, wg["ssm_w_glu"], dproj, red.jobs())
        red.land(res)
        rest = [dwg.reshape(4, W // 4, W), dwa, dwb, dwo.reshape(4, D // 4, D)]
        if l == 0:
            red.add(l, "rest", BIG[1:], rest)
        halves = []
        for r in range(2):
            outs, res = _inproj_dw(l, r, s["h"], dproj, red.jobs())
            red.land(res)
            halves += outs
            if l == 0:
                red.add(l, f"in{r}", BIG[:1], outs, [r * (D // 2)])
        (dx, dg), res = _inproj_dx(l, dproj, wg["w_in"], s["x"], g, dx, red.jobs())
        red.land(res)
        if l > 0:
            red.add(l, "all", BIG[:1] * 2 + BIG[1:], halves + rest, [0, D // 2, 0, 0, 0, 0], avoid=("pool_bwd",))
        for n, a in (("norm_g", dg.reshape(D)), ("b_in", jnp.concatenate([dbias_a, dbias_b, dbias_g], axis=1).reshape(NIN)),
                     ("ssm_d", dd.reshape(W)), ("ssm_b_glu", dbg.reshape(W)), ("pool_w", dpw), ("pool_scale", dsc.reshape(W)),
                     ("dare", dare), ("daim", daim), ("dbre", dbre), ("dbim", dbim), ("dcre", dcre), ("dcimn", dcimn)):
            per_layer[n][l] = a
    gs = {n: jnp.stack(a) for n, a in per_layer.items()}
    d_abar = [jnp.sum(gs.pop(n), axis=1).reshape(DEPTH, G, P) for n in ("dare", "daim")]
    (gs["ssm_log_dt"], gs["ssm_lam_re"], gs["ssm_lam_im"], gs["ssm_b_re"], gs["ssm_b_im"]) = disc_vjp(
        (*d_abar, _extract_in(gs.pop("dbre")), _extract_in(gs.pop("dbim"))))
    gs["ssm_c_re"], gs["ssm_c_im"] = _extract_out(gs.pop("dcre")), -_extract_out(gs.pop("dcimn"))
    gs["final_norm_g"] = dgf

    natural = {n: w[n].shape for n in REPLICATED}
    rw, rm, rv = {}, {}, {}
    for n in REPLICATED:
        shape = DENSE.get(n, natural[n])
        gs[n], rw[n], rm[n], rv[n] = (a.reshape(shape) for a in (gs[n], w[n], m[n], v[n]))
    small = [gs[n] for n in REPLICATED] + [loss]
    jobs = red.jobs()
    res = _pcall(None, "tail_exchange", (), [], [], [], [], jobs=jobs + [_SiblingJob(small, False)])[1]
    red.land(res[:len(jobs)])
    pair_small = _small_pair_sum(place, small, res[-1][1], [BF16 if a.ndim > 2 else F32 for a in small])
    jobs = red.jobs()
    res = _pcall(None, "tail_gather", (), [], [], [], [], jobs=jobs + [_ChipGatherJob(pair_small)])[1]
    red.land(res[:len(jobs)])
    assert not red.active
    small_parts = dict(zip(REPLICATED + ("loss",), res[-1][0]))

    k = len(REPLICATED)
    outs = _adamw_small("adamw_small", [rw[n] for n in REPLICATED], [small_parts[n] for n in REPLICATED],
                        [rm[n] for n in REPLICATED], [rv[n] for n in REPLICATED], small_parts["loss"])
    results = {n: red.big[n] for n in BIG}
    results.update({n: [outs[1 + q * k + i].reshape(natural[n]) for q in range(4)] for i, n in enumerate(REPLICATED)})
    return outs[0][0, 0], dx, results


def _place():
    x, y, c = lax.axis_index("x"), lax.axis_index("y"), lax.axis_index("c")
    chips = [(1 - x, y), (x, 1 - y), (1 - x, 1 - y)]
    return x, y, c, 2 * x + y, chips, [2 * cx + cy for cx, cy in chips]


def _remote(src, dst, ssem, rsem, dev):
    return pltpu.make_async_remote_copy(src_ref=src, dst_ref=dst, send_sem=ssem, recv_sem=rsem,
                                        device_id=dev, device_id_type=MESH)


class _GatherJob:
    def __init__(self, bufs, l):
        self.srcs, self.bufs, self.news, self.l = [], list(bufs), [], l
        self.scratch = [pltpu.SemaphoreType.DMA((len(self.bufs), 3))] * 4

    def _half(self, ref, k, h):
        rows = ref.shape[2] // 2
        return ref.at[self.l, k, pl.ds(pl.multiple_of(h * rows, 8), rows), :]

    def _ici(self, bufs, sems, a, j, k):
        _, _, c, _, chips, _ = _place()
        blk = self._half(bufs[a], k, c)
        return _remote(blk, blk, sems[0].at[a, j], sems[1].at[a, j], (*chips[j], c))

    def _d2d(self, bufs, sems, a, j, k, h):
        x, y, c, _, _, _ = _place()
        blk = self._half(bufs[a], k, h)
        return _remote(blk, blk, sems[2].at[a, j], sems[3].at[a, j], (x, y, 1 - c))

    def start(self, srcs, bufs, news, sems):
        me = _place()[3]
        for a in range(len(self.bufs)):
            for j in range(3):
                self._ici(bufs, sems, a, j, me).start()

    def finish(self, srcs, bufs, news, sems):
        _, _, c, me, _, cid = _place()
        pairs = [(a, j) for a in range(len(self.bufs)) for j in range(3)]
        for a, j in pairs:
            self._ici(bufs, sems, a, j, cid[j]).wait_recv()
            self._d2d(bufs, sems, a, j, cid[j], c).start()
        for a, j in pairs:
            self._d2d(bufs, sems, a, j, cid[j], 1 - c).wait_recv()
        for a, j in pairs:
            self._ici(bufs, sems, a, j, me).wait_send()
            self._d2d(bufs, sems, a, j, cid[j], c).wait_send()


class _SiblingJob:
    def __init__(self, srcs, rows_half):
        self.srcs, self.bufs, self.rows_half = list(srcs), [], rows_half
        self.news = [SDS((s.shape[0], s.shape[1] // 2, s.shape[2]) if rows_half else s.shape, s.dtype) for s in srcs]
        self.scratch = [pltpu.SemaphoreType.DMA((len(self.srcs),))] * 2

    def _copy(self, srcs, news, sems, a):
        x, y, c, _, _, _ = _place()
        src = srcs[a]
        if self.rows_half:
            rows = src.shape[1] // 2
            src = src.at[:, pl.ds(pl.multiple_of((1 - c) * rows, 8), rows), :]
        return _remote(src, news[a], sems[0].at[a], sems[1].at[a], (x, y, 1 - c))

    def start(self, srcs, bufs, news, sems):
        for a in range(len(self.srcs)):
            self._copy(srcs, news, sems, a).start()

    def finish(self, srcs, bufs, news, sems):
        for a in range(len(self.srcs)):
            self._copy(srcs, news, sems, a).wait()


class _ScatterJob:
    def __init__(self, parts):
        self.srcs, self.bufs = list(parts), []
        self.news = [SDS((3,) + p.shape[1:], p.dtype) for p in parts]
        self.scratch = [pltpu.SemaphoreType.DMA((len(self.srcs), 3))] * 2

    def _copy(self, srcs, news, sems, a, j):
        _, _, c, _, chips, cid = _place()
        return _remote(srcs[a].at[cid[j]], news[a].at[j], sems[0].at[a, j], sems[1].at[a, j], (*chips[j], c))

    def start(self, srcs, bufs, news, sems):
        for a in range(len(self.srcs)):
            for j in range(3):
                self._copy(srcs, news, sems, a, j).start()

    def finish(self, srcs, bufs, news, sems):
        for a in range(len(self.srcs)):
            for j in range(3):
                self._copy(srcs, news, sems, a, j).wait()


def _comm_only(name, job):
    return _pcall(None, name, (), [], [], [], [], jobs=[job])[1][0]


class _ChipGatherJob(_GatherJob):
    def __init__(self, bufs):
        super().__init__(bufs, None)

    def _half(self, ref, k, h):
        return ref.at[k, h]


def _cast_own(place, ws):
    n = len(ws)

    def body(p_ref, *refs):
        for i_ref, o_ref in zip(refs[:n], refs[n:]):
            o_ref[...] = i_ref[...].astype(BF16)

    return pl.pallas_call(
        body, name="cast_own_shards",
        grid_spec=pltpu.PrefetchScalarGridSpec(
            num_scalar_prefetch=1, grid=(DEPTH,),
            in_specs=[pl.BlockSpec((None,) + a.shape[1:], lambda l, p: (l, 0, 0)) for a in ws],
            out_specs=[pl.BlockSpec((None, None) + a.shape[1:], lambda l, p: (l, p[1], 0, 0)) for a in ws]),
        out_shape=[SDS((DEPTH, 4) + a.shape[1:], BF16) for a in ws],
        compiler_params=_params("arbitrary"),
    )(place, *ws)


def _half_tiles(a_):
    rows = a_ // 2
    ta = min(rows, 256)
    return rows, ta, rows // ta


def _pair_sums_bf16(name, place, owns, recvs):
    n = len(owns)

    def body(p_ref, *refs):
        for own_ref, recv_ref, out_ref in zip(refs[:n], refs[n:2 * n], refs[2 * n:]):
            out_ref[...] = (own_ref[...] + recv_ref[...]).astype(BF16)

    def own_half(a):
        return pl.BlockSpec((None, a.shape[1] // 2, a.shape[2]), lambda s, p: (s, p[0], 0))

    def block(a):
        return pl.BlockSpec((None,) + a.shape[1:], lambda s, p: (s, 0, 0))

    return pl.pallas_call(
        body, name=name,
        grid_spec=pltpu.PrefetchScalarGridSpec(
            num_scalar_prefetch=1, grid=(4,),
            in_specs=[own_half(a) for a in owns] + [block(r) for r in recvs],
            out_specs=[block(r) for r in recvs]),
        out_shape=[SDS(r.shape, BF16) for r in recvs],
        compiler_params=_params("arbitrary"),
    )(place, *owns, *recvs)


def _shard_sums(name, place, owns, recvs, rbufs):
    n = len(owns)

    def body(p_ref, *refs):
        for own_ref, recv_ref, r_ref, out_ref in zip(refs[:n], refs[n:2 * n], refs[2 * n:3 * n], refs[3 * n:]):
            acc = own_ref[...] + recv_ref[...]
            for j in range(3):
                acc = acc + r_ref[j].astype(F32)
            out_ref[...] = acc

    def own_half(a):
        return pl.BlockSpec((None, a.shape[1] // 2, a.shape[2]), lambda i, p: (p[1], p[0], 0))

    def recv_block(a):
        return pl.BlockSpec((None,) + a.shape[1:], lambda i, p: (p[1], 0, 0))

    return pl.pallas_call(
        body, name=name,
        grid_spec=pltpu.PrefetchScalarGridSpec(
            num_scalar_prefetch=1, grid=(1,),
            in_specs=([own_half(a) for a in owns] + [recv_block(r) for r in recvs]
                      + [pl.BlockSpec(rb.shape, lambda i, p: (0, 0, 0)) for rb in rbufs]),
            out_specs=[pl.BlockSpec(r.shape[1:], lambda i, p: (0, 0)) for r in recvs]),
        out_shape=[SDS(r.shape[1:], F32) for r in recvs],
        compiler_params=_params("arbitrary"),
    )(place, *owns, *recvs, *rbufs)


def _small_pair_sum(place, mine, recv, dtypes):
    n = len(mine)

    def body(p_ref, *refs):
        for m_ref, r_ref, o_ref in zip(refs[:n], refs[n:2 * n], refs[2 * n:]):
            o_ref[...] = (m_ref[...] + r_ref[...]).astype(o_ref.dtype)

    def whole(a):
        zeros = (0,) * a.ndim
        return pl.BlockSpec(a.shape, lambda i, p: zeros)

    def mine_blk(a):
        zeros = (0,) * a.ndim
        return pl.BlockSpec((None,) + a.shape, lambda i, p: (p[1],) + zeros)

    return pl.pallas_call(
        body, name="small_pair_sum",
        grid_spec=pltpu.PrefetchScalarGridSpec(
            num_scalar_prefetch=1, grid=(1,),
            in_specs=[whole(a) for a in mine] + [whole(a) for a in recv],
            out_specs=[mine_blk(a) for a in mine]),
        out_shape=[SDS((4,) + a.shape, dt) for a, dt in zip(mine, dtypes)],
        compiler_params=_params("arbitrary"),
    )(place, *mine, *recv)


def _adam_math(w, g, m, v):
    m = B1 * m + (1.0 - B1) * g
    v = B2 * v + (1.0 - B2) * (g * g)
    m_hat = m / (1.0 - B1 ** STEP)
    v_hat = v / (1.0 - B2 ** STEP)
    delta = -LR * (m_hat / (jnp.sqrt(v_hat) + EPS_A) + WD * w)
    return delta, m, v


def _adamw_big(name, l, row0, w, m, v, mine, other, prev, jobs=()):
    _, _, b_ = w.shape
    _, ta, nh = _half_tiles(2 * mine.shape[0])
    prev = list(prev or [])

    def body(w_ref, m_ref, v_ref, mine_ref, other_ref, *rest):
        g_ref, d_ref, mo_ref, vo_ref = rest[len(prev):]
        g = jnp.where(pl.program_id(0) == lax.axis_index("c"), mine_ref[...], other_ref[...])
        g_ref[...] = g
        d_ref[...], mo_ref[...], vo_ref[...] = _adam_math(w_ref[...], g, m_ref[...], v_ref[...])

    slab = pl.BlockSpec((None, ta, b_), lambda h, i: (l, row0 // ta + h * nh + i, 0))
    half = pl.BlockSpec((ta, b_), lambda h, i: (i, 0))
    outs, res = _pcall(
        body, name, (2, nh), [slab, slab, slab, half, half] + [_ANY] * len(prev), [slab] * 4, [SDS(w.shape, F32)] * 4,
        (w, m, v, mine, other, *prev), aliases={5 + k: k for k in range(len(prev))}, jobs=jobs)
    return outs, res


def _adamw_small(name, ws, parts, ms, vs, loss_parts=None):
    k = len(ws)
    extra = [] if loss_parts is None else [loss_parts]

    def chip_sum(p_ref):
        p = [p_ref[k].astype(F32) for k in range(4)]
        return ((p[0] + p[1]) + p[2]) + p[3]

    def body(*refs):
        w_refs, p_refs, m_refs, v_refs = refs[:k], refs[k:2 * k], refs[2 * k:3 * k], refs[3 * k:4 * k]
        outs = refs[4 * k + len(extra):]
        if extra:
            outs[0][...] = chip_sum(refs[4 * k])
            outs = outs[1:]
        for a in range(k):
            g = chip_sum(p_refs[a])
            outs[a][...] = g
            outs[k + a][...], outs[2 * k + a][...], outs[3 * k + a][...] = _adam_math(
                w_refs[a][...], g, m_refs[a][...], v_refs[a][...])

    like = [SDS(a.shape, F32) for a in ws]
    return pl.pallas_call(
        body, name=name,
        out_shape=([SDS(loss_parts.shape[1:], F32)] if extra else []) + like * 4,
        compiler_params=pltpu.CompilerParams(vmem_limit_bytes=VMEM_LIMIT),
    )(*ws, *parts, *ms, *vs, *extra)


class _Reducer:
    def __init__(self, place, w, m, v):
        self.place, self.w, self.m, self.v = place, w, m, v
        self.active, self.riding = [], []
        self.big = {n: None for n in BIG}

    def add(self, l, tag, names, own, row0s=None, avoid=()):
        self.active.append(dict(l=l, key=f"{tag}_l{l}", names=names, own=list(own), row0s=row0s or [0] * len(names),
                                avoid=avoid, stage=0))

    def jobs(self, kind=""):
        self.riding = [g for g in self.active if kind not in g["avoid"]]
        return [(_SiblingJob(g["own"], True), _ScatterJob(g.get("parts", [])), _SiblingJob(g.get("shard", []), False))
                [g["stage"]] for g in self.riding]

    def land(self, res):
        for g, (_, news) in zip(self.riding, res):
            if g["stage"] == 0:
                g["recv"] = news
                g["parts"] = _pair_sums_bf16(f"pair_sums_{g['key']}", self.place, g["own"], news)
            elif g["stage"] == 1:
                g["shard"] = _shard_sums(f"shard_sums_{g['key']}", self.place, g["own"], g["recv"], news)
            else:
                for n, mine, other, row0 in zip(g["names"], g["shard"], news, g["row0s"]):
                    self.big[n] = _adamw_big(f"adamw_{n}_{row0}_{g['key']}", g["l"], row0, self.w[n], self.m[n], self.v[n],
                                             mine, other, self.big[n])[0]
                self.active.remove(g)
            g["stage"] += 1
        self.riding = []


WEIGHTS = ("norm_g", "w_in", "b_in", "ssm_log_dt", "ssm_lam_re", "ssm_lam_im", "ssm_b_re", "ssm_b_im", "ssm_c_re",
           "ssm_c_im", "ssm_d", "ssm_w_glu", "ssm_b_glu", "pool_w", "pool_scale", "w_branch_a", "w_branch_b", "w_out",
           "final_norm_g")
REPLICATED = SMALL + ("final_norm_g",)
DENSE = {"ssm_b_re": (DEPTH, G, P * C), "ssm_b_im": (DEPTH, G, P * C), "final_norm_g": (2, D // 2)}


def kernel(x, norm_g, w_in, b_in, ssm_log_dt, ssm_lam_re, ssm_lam_im, ssm_b_re, ssm_b_im, ssm_c_re, ssm_c_im, ssm_d, ssm_w_glu, ssm_b_glu, pool_w, pool_scale, w_branch_a, w_branch_b, w_out, final_norm_g, loss_target, m_norm_g, m_w_in, m_b_in, m_ssm_log_dt, m_ssm_lam_re, m_ssm_lam_im, m_ssm_b_re, m_ssm_b_im, m_ssm_c_re, m_ssm_c_im, m_ssm_d, m_ssm_w_glu, m_ssm_b_glu, m_pool_w, m_pool_scale, m_w_branch_a, m_w_branch_b, m_w_out, m_final_norm_g, v_norm_g, v_w_in, v_b_in, v_ssm_log_dt, v_ssm_lam_re, v_ssm_lam_im, v_ssm_b_re, v_ssm_b_im, v_ssm_c_re, v_ssm_c_im, v_ssm_d, v_ssm_w_glu, v_ssm_b_glu, v_pool_w, v_pool_scale, v_w_branch_a, v_w_branch_b, v_w_out, v_final_norm_g):
    w = dict(zip(WEIGHTS, (norm_g, w_in, b_in, ssm_log_dt, ssm_lam_re, ssm_lam_im, ssm_b_re, ssm_b_im, ssm_c_re,
                           ssm_c_im, ssm_d, ssm_w_glu, ssm_b_glu, pool_w, pool_scale, w_branch_a, w_branch_b, w_out,
                           final_norm_g)))
    m = dict(zip(WEIGHTS, (m_norm_g, m_w_in, m_b_in, m_ssm_log_dt, m_ssm_lam_re, m_ssm_lam_im, m_ssm_b_re, m_ssm_b_im,
                           m_ssm_c_re, m_ssm_c_im, m_ssm_d, m_ssm_w_glu, m_ssm_b_glu, m_pool_w, m_pool_scale,
                           m_w_branch_a, m_w_branch_b, m_w_out, m_final_norm_g)))
    v = dict(zip(WEIGHTS, (v_norm_g, v_w_in, v_b_in, v_ssm_log_dt, v_ssm_lam_re, v_ssm_lam_im, v_ssm_b_re, v_ssm_b_im,
                           v_ssm_c_re, v_ssm_c_im, v_ssm_d, v_ssm_w_glu, v_ssm_b_glu, v_pool_w, v_pool_scale,
                           v_w_branch_a, v_w_branch_b, v_w_out, v_final_norm_g)))
    place = jnp.stack([lax.axis_index("c"), 2 * lax.axis_index("x") + lax.axis_index("y")]).astype(jnp.int32)

    total_loss, dx, results = _step(x[0], loss_target[0], w, m, v, place)
    return (total_loss, dx[None], *[results[n][q] for q in range(4) for n in WEIGHTS])
```

```python
import functools

import jax
import jax.numpy as jnp
from jax import lax
from jax.experimental import pallas as pl
from jax.experimental.pallas import tpu as pltpu

F32, BF16 = jnp.float32, jnp.bfloat16
SDS = jax.ShapeDtypeStruct
MESH = pl.DeviceIdType.MESH

DEPTH = 2
L = 2048
D = 1024
NIN = 4096
W = 512
G, P, C = 32, 64, 16
GP = G * P
WINS = (2, 4, 8, 16)
TM = 256
NT = L // TM
TMM = 512
TK = 1024
EPS = 1e-6
VMEM_LIMIT = 56 * 2**20

LR, B1, B2, EPS_A, WD, STEP = 0.001, 0.9, 0.999, 1e-08, 0.01, 10


def _params(*sem):
    return pltpu.CompilerParams(dimension_semantics=sem, vmem_limit_bytes=VMEM_LIMIT)


_ANY = pl.BlockSpec(memory_space=pl.ANY)


def _full(shape):
    zeros = (0,) * len(shape)
    return pl.BlockSpec(shape, lambda *_: zeros)


def _layer(l, shape):
    zeros = (0,) * len(shape)
    return pl.BlockSpec((None,) + shape, lambda *_: (l,) + zeros)


def _rows(width, col=0, reverse=False, tm=TM):
    if reverse:
        return pl.BlockSpec((tm, width), lambda i: (L // tm - 1 - i, col))
    return pl.BlockSpec((tm, width), lambda i: (i, col))


def _rows_mm(width, col=0):
    return _rows(width, col, False, TMM)


def _pcall(body, name, grid, in_specs, out_specs, out_shape, args, scratch=(), aliases=None, jobs=()):
    in_specs, out_specs, out_shape, args, scratch = list(in_specs), list(out_specs), list(out_shape), list(args), list(scratch)
    aliases = dict(aliases or {})
    jobs = [j for j in jobs if j is not None]
    n_in, n_out, n_scr = len(in_specs), len(out_specs), len(scratch)
    srcs = [s for j in jobs for s in j.srcs]
    bufs = [b for j in jobs for b in j.bufs]
    news = [s for j in jobs for s in j.news]
    aliases.update({n_in + len(srcs) + k: n_out + k for k in range(len(bufs))})

    def hosted(*refs):
        cuts = [n_in, len(srcs), len(bufs), n_out, len(bufs), len(news), n_scr]
        parts, p = [], 0
        for n in cuts:
            parts.append(refs[p:p + n])
            p += n
        ins, src_r, _, outs, buf_r, new_r, scr = parts
        sem_r = refs[p:]
        views, ps, pb, pn, pm = [], 0, 0, 0, 0
        for j in jobs:
            views.append((src_r[ps:ps + len(j.srcs)], buf_r[pb:pb + len(j.bufs)], new_r[pn:pn + len(j.news)],
                          sem_r[pm:pm + len(j.scratch)]))
            ps, pb, pn, pm = ps + len(j.srcs), pb + len(j.bufs), pn + len(j.news), pm + len(j.scratch)

        def run(phase):
            for j, v in zip(jobs, views):
                getattr(j, phase)(*v)

        def at_step(step):
            return functools.reduce(jnp.logical_and, [pl.program_id(d) == step(d) for d in range(len(grid))])

        if not grid:
            run("start")
            run("finish")
            return
        pl.when(at_step(lambda d: 0))(lambda: run("start"))
        body(*ins, *outs, *scr)
        pl.when(at_step(lambda d: grid[d] - 1))(lambda: run("finish"))

    outs = pl.pallas_call(
        hosted if jobs else body, name=name, **({"grid": grid} if grid else {}),
        in_specs=in_specs + [_ANY] * (len(srcs) + len(bufs)), out_specs=out_specs + [_ANY] * (len(bufs) + len(news)),
        out_shape=out_shape + [SDS(b.shape, b.dtype) for b in bufs] + news,
        input_output_aliases=aliases, scratch_shapes=scratch + [s for j in jobs for s in j.scratch],
        compiler_params=_params(*(("arbitrary",) * len(grid))))(*args, *srcs, *bufs)
    res, pb, pn = [], n_out, n_out + len(bufs)
    for j in jobs:
        res.append((list(outs[pb:pb + len(j.bufs)]), list(outs[pn:pn + len(j.news)])))
        pb, pn = pb + len(j.bufs), pn + len(j.news)
    return list(outs[:n_out]), res


def _dot(a, b):
    return jnp.dot(a, b, preferred_element_type=F32)


def _dot_nt(a, b):
    return lax.dot_general(a, b, (((1,), (1,)), ((), ())), preferred_element_type=F32)


def _dot_tn(a, b):
    return lax.dot_general(a, b, (((0,), (0,)), ((), ())), preferred_element_type=F32)


_K0 = 0.7978845608028654
_K1 = 0.044715


def _gelu(x):
    return 0.5 * x * (1.0 + jnp.tanh(_K0 * (x + _K1 * (x * x * x))))


def _gelu_grad(x):
    t = jnp.tanh(_K0 * (x + _K1 * (x * x * x)))
    return 0.5 * (1.0 + t) + 0.5 * x * (1.0 - t * t) * (_K0 * (1.0 + 3.0 * _K1 * x * x))


def _sigmoid(x):
    return jax.nn.sigmoid(x)


def _norm_inproj(l, x, g, w, b, jobs=()):
    def body(x_ref, g_ref, w_ref, b_ref, h_ref, proj_ref):
        xv = x_ref[...]
        r = lax.rsqrt(jnp.mean(xv * xv, axis=-1, keepdims=True) + EPS)
        hb = ((xv * r) * g_ref[...]).astype(BF16)
        h_ref[...] = hb
        for j in range(4):
            cs = slice(j * 1024, (j + 1) * 1024)
            proj_ref[:, cs] = _dot(hb, w_ref[j]) + b_ref[:, cs]

    return _pcall(
        body, f"norm_inproj_l{l}", (L // TMM,),
        [_rows_mm(D), _layer(l, (1, D)), _layer(l, (4, D, 1024)), _layer(l, (1, NIN))],
        [_rows_mm(D), _rows_mm(NIN)],
        [SDS((L, D), BF16), SDS((L, NIN), F32)],
        (x, g, w, b), jobs=jobs)


SEG = TM // 8
LB = 1024


def _scan_tile(re_ref, im_ref, co, car_ref, cseg_ref, reverse):
    a1_re, a1_im, sst_re, sst_im, scr_re, scr_im, tab_re, tab_im = co
    rowid = lax.broadcasted_iota(jnp.int32, (8, LB), 0)
    for lb in range(GP // LB):
        cols = slice(lb * LB, (lb + 1) * LB)

        def within(k, state):
            i = (SEG - 1 - k) if reverse else k
            rows = pl.ds(pl.multiple_of(i * 8, 8), 8)
            sr, si = state
            ar = a1_re[:, cols]
            ai = a1_im[:, cols]
            nr = ar * sr - ai * si + re_ref[rows, cols]
            ni = ar * si + ai * sr + im_ref[rows, cols]
            re_ref[rows, cols] = nr
            im_ref[rows, cols] = ni
            return nr, ni

        zero = jnp.zeros((8, LB), F32)
        er, ei = lax.fori_loop(0, SEG, within, (zero, zero))
        for s, d in enumerate((1, 2, 4)):
            cr = sst_re[8 * s:8 * s + 8, cols]
            ci = sst_im[8 * s:8 * s + 8, cols]
            rr = pltpu.roll(er, 8 - d if reverse else d, 0)
            ri = pltpu.roll(ei, 8 - d if reverse else d, 0)
            er, ei = er + cr * rr - ci * ri, ei + cr * ri + ci * rr
        in_r = car_ref[0:1, cols]
        in_i = car_ref[1:2, cols]
        pr = scr_re[:, cols]
        pi = scr_im[:, cols]
        er, ei = er + pr * in_r - pi * in_i, ei + pr * in_i + pi * in_r
        if reverse:
            car_ref[0:1, cols] = er[0:1]
            car_ref[1:2, cols] = ei[0:1]
            cseg_ref[0:8, cols] = jnp.where(rowid == 7, in_r, pltpu.roll(er, 7, 0))
            cseg_ref[8:16, cols] = jnp.where(rowid == 7, in_i, pltpu.roll(ei, 7, 0))
        else:
            car_ref[0:1, cols] = er[7:8]
            car_ref[1:2, cols] = ei[7:8]
            cseg_ref[0:8, cols] = jnp.where(rowid == 0, in_r, pltpu.roll(er, 1, 0))
            cseg_ref[8:16, cols] = jnp.where(rowid == 0, in_i, pltpu.roll(ei, 1, 0))

        def entering(i, _):
            rows = pl.ds(pl.multiple_of(i * 8, 8), 8)
            tr = tab_re[rows, cols]
            ti = tab_im[rows, cols]
            cr = cseg_ref[0:8, cols]
            ci = cseg_ref[8:16, cols]
            re_ref[rows, cols] += tr * cr - ti * ci
            im_ref[rows, cols] += tr * ci + ti * cr
            return 0

        lax.fori_loop(0, SEG, entering, 0)


def _permute_f32(pm, x):
    h1 = x.astype(BF16)
    r1 = x - h1.astype(F32)
    h2 = r1.astype(BF16)
    h3 = (r1 - h2.astype(F32)).astype(BF16)
    return (_dot(pm, h1) + _dot(pm, h2)) + _dot(pm, h3)


_SCAN_CO = [(8, GP)] * 2 + [(24, GP)] * 2 + [(8, GP)] * 2 + [(TM, GP)] * 2


def _s5_fwd(l, proj, bexp, cre, cimn, co, dsk, wglu, bglu, pm, pmt, job=None):
    def body(ua_ref, za_ref, bexp_ref, cre_ref, cimn_ref, *rest):
        co_refs, (d_ref, wg_ref, bg_ref, pm_ref, pmt_ref, sre_ref, sim_ref, y1_ref, q_ref, ya_ref, up_ref, zp_ref,
                  car_ref, cseg_ref) = rest[:8], rest[8:]

        @pl.when(pl.program_id(0) == 0)
        def _():
            car_ref[...] = jnp.zeros_like(car_ref)

        pm = pm_ref[...]
        u = _permute_f32(pm, ua_ref[...])
        za = _permute_f32(pm, za_ref[...])
        up_ref[...] = u
        zp_ref[...] = za
        ub = u.astype(BF16)
        for k in range(4):
            bu = _dot(ub[:, 128 * k:128 * (k + 1)], bexp_ref[k])
            sre_ref[:, 512 * k:512 * (k + 1)] = bu[:, :512]
            sim_ref[:, 512 * k:512 * (k + 1)] = bu[:, 512:]
        _scan_tile(sre_ref, sim_ref, co_refs, car_ref, cseg_ref, False)
        for k in range(4):
            blk = slice(512 * k, 512 * (k + 1))
            ks = slice(128 * k, 128 * (k + 1))
            y0 = _dot(sre_ref[:, blk].astype(BF16), cre_ref[k]) + _dot(sim_ref[:, blk].astype(BF16), cimn_ref[k])
            y1_ref[:, ks] = y0 + d_ref[:, ks] * u[:, ks]
        y2 = _gelu(y1_ref[...])
        q = _dot(y2.astype(BF16), wg_ref[...]) + bg_ref[...]
        q_ref[...] = q
        ya = ((y2 * _sigmoid(q)) * (za * _sigmoid(za))).astype(BF16)
        ya_ref[...] = _dot(pmt_ref[...], ya).astype(BF16)

    return _pcall(
        body, f"s5_fwd_l{l}", (NT,),
        [_rows(W, 0), _rows(W, 1), _layer(l, (4, 128, 1024)), _layer(l, (4, 512, 128)), _layer(l, (4, 512, 128))]
        + [_layer(l, s) for s in _SCAN_CO]
        + [_layer(l, (1, W)), _layer(l, (W, W)), _layer(l, (1, W)), _full((TM, TM)), _full((TM, TM))],
        [_rows(GP), _rows(GP), _rows(W), _rows(W), _rows(W), _rows(W), _rows(W)],
        [SDS((L, GP), F32), SDS((L, GP), F32), SDS((L, W), F32), SDS((L, W), F32), SDS((L, W), BF16),
         SDS((L, W), F32), SDS((L, W), F32)],
        (proj, proj, bexp, cre, cimn, *co, dsk, wglu, bglu, pm, pmt),
        scratch=[pltpu.VMEM((8, GP), F32), pltpu.VMEM((16, GP), F32)], jobs=[job])


def _pool_fwd(l, proj, pw, scale):
    def body(ub_ref, zb_ref, pw_ref, sc_ref, pooled_ref, mixed_ref, yb_ref, buf):
        i = pl.program_id(0)

        @pl.when(i == 0)
        def _():
            buf[0:16, :] = jnp.zeros((16, W), F32)

        u = ub_ref[...]
        buf[16:16 + TM, :] = u
        t = i * TM + lax.broadcasted_iota(jnp.int32, (TM, 128), 0)
        for gi, win in enumerate(WINS):
            cs = slice(128 * gi, 128 * (gi + 1))
            acc = u[:, cs]
            for k in range(1, win):
                acc = acc + buf[16 - k:16 - k + TM, cs]
            cnt = jnp.minimum(t + 1, win).astype(F32)
            pb = (acc / cnt - u[:, cs]).astype(BF16)
            pooled_ref[:, cs] = pb
            mixed_ref[:, cs] = _dot(pb, pw_ref[gi])
        zb = zb_ref[...]
        yb_ref[...] = ((mixed_ref[...] * sc_ref[...]) * (zb * _sigmoid(zb))).astype(BF16)
        buf[0:16, :] = buf[TM:TM + 16, :]

    return pl.pallas_call(
        body, name=f"pool_fwd_l{l}", grid=(NT,),
        in_specs=[_rows(W, 2), _rows(W, 3), _layer(l, (4, 128, 128)), _layer(l, (1, W))],
        out_specs=[_rows(W), _rows(W), _rows(W)],
        out_shape=[SDS((L, W), BF16), SDS((L, W), F32), SDS((L, W), BF16)],
        scratch_shapes=[pltpu.VMEM((TM + 16, W), F32)],
        compiler_params=_params("arbitrary"),
    )(proj, proj, pw, scale)


def _merge_out(l, ya, yb, proj, x, wa, wb, wo):
    def body(ya_ref, yb_ref, ga_ref, gb_ref, x_ref, wa_ref, wb_ref, wo_ref, pa_ref, pb_ref, mg_ref, xo_ref):
        ya = ya_ref[...]
        yb = yb_ref[...]
        for j in range(4):
            cs = slice(256 * j, 256 * (j + 1))
            pa_ref[:, cs] = _dot(ya, wa_ref[j])
            pb_ref[:, cs] = _dot(yb, wb_ref[j])
        merged = _sigmoid(ga_ref[...]) * pa_ref[...] + _sigmoid(gb_ref[...]) * pb_ref[...]
        mb = merged.astype(BF16)
        mg_ref[...] = mb
        xo_ref[...] = x_ref[...] + _dot(mb, wo_ref[...])

    return pl.pallas_call(
        body, name=f"merge_out_l{l}", grid=(L // TMM,),
        in_specs=[_rows_mm(W), _rows_mm(W), _rows_mm(D, 2), _rows_mm(D, 3), _rows_mm(D),
                  _layer(l, (4, W, 256)), _layer(l, (4, W, 256)), _layer(l, (D, D))],
        out_specs=[_rows_mm(D), _rows_mm(D), _rows_mm(D), _rows_mm(D)],
        out_shape=[SDS((L, D), F32), SDS((L, D), F32), SDS((L, D), BF16), SDS((L, D), F32)],
        compiler_params=_params("arbitrary"),
    )(ya, yb, proj, proj, x, wa, wb, wo)


def _loss_head(x, gf, target):
    def body(x_ref, g_ref, t_ref, loss_ref, dx_ref, dg_ref):
        @pl.when(pl.program_id(0) == 0)
        def _():
            loss_ref[...] = jnp.zeros_like(loss_ref)
            dg_ref[...] = jnp.zeros_like(dg_ref)

        xv = x_ref[...]
        g = g_ref[...]
        r = lax.rsqrt(jnp.mean(xv * xv, axis=-1, keepdims=True) + EPS)
        xn = xv * r
        err = xn * g - t_ref[...]
        part = jnp.sum(jnp.mean(err * err, axis=-1, keepdims=True), axis=0, keepdims=True)
        loss_ref[...] += 0.5 * part
        dy = err * (1.0 / D)
        dg_ref[...] += jnp.sum(dy * xn, axis=0, keepdims=True)
        dxn = dy * g
        dx_ref[...] = r * (dxn - xn * jnp.mean(dxn * xn, axis=-1, keepdims=True))

    return pl.pallas_call(
        body, name="loss_head", grid=(NT,),
        in_specs=[_rows(D), _full((1, D)), _rows(D)],
        out_specs=[_full((2, 128)), _rows(D), _full((1, D))],
        out_shape=[SDS((2, 128), F32), SDS((L, D), F32), SDS((1, D), F32)],
        compiler_params=_params("arbitrary"),
    )(x, gf, target)


def _merge_out_bwd(l, dxn, mg, proj, pa, pb, ya, yb, wo, wa, wb, jobs=()):
    def body(dx_ref, mg_ref, ga_ref, gb_ref, pa_ref, pb_ref, ya_ref, yb_ref, wo_ref, wa_ref, wb_ref,
             dg_ref, dya_ref, dyb_ref, dwo_ref, dwa_ref, dwb_ref, dbias_ref):
        @pl.when(pl.program_id(0) == 0)
        def _():
            for ref in (dwo_ref, dwa_ref, dwb_ref, dbias_ref):
                ref[...] = jnp.zeros_like(ref)

        dxb = dx_ref[...].astype(BF16)
        dm = _dot_nt(dxb, wo_ref[...])
        sa = _sigmoid(ga_ref[...])
        sb = _sigmoid(gb_ref[...])
        dga = dm * pa_ref[...] * (sa * (1.0 - sa))
        dgb = dm * pb_ref[...] * (sb * (1.0 - sb))
        dg_ref[:, :D] = dga.astype(BF16)
        dg_ref[:, D:] = dgb.astype(BF16)
        dbias_ref[:, :D] += jnp.sum(dga, axis=0, keepdims=True)
        dbias_ref[:, D:] += jnp.sum(dgb, axis=0, keepdims=True)
        dpa = (dm * sa).astype(BF16)
        dpb = (dm * sb).astype(BF16)
        ya = ya_ref[...]
        yb = yb_ref[...]
        dya = jnp.zeros((TM, W), F32)
        dyb = jnp.zeros((TM, W), F32)
        for j in range(4):
            cs = slice(256 * j, 256 * (j + 1))
            dya = dya + _dot_nt(dpa[:, cs], wa_ref[j])
            dyb = dyb + _dot_nt(dpb[:, cs], wb_ref[j])
            dwa_ref[j] += _dot_tn(ya, dpa[:, cs])
            dwb_ref[j] += _dot_tn(yb, dpb[:, cs])
        dya_ref[...] = dya
        dyb_ref[...] = dyb
        dwo_ref[...] += _dot_tn(mg_ref[...], dxb)

    return _pcall(
        body, f"merge_out_bwd_l{l}", (NT,),
        [_rows(D), _rows(D), _rows(D, 2), _rows(D, 3), _rows(D), _rows(D), _rows(W), _rows(W),
         _layer(l, (D, D)), _layer(l, (4, W, 256)), _layer(l, (4, W, 256))],
        [_rows(2 * D, 1), _rows(W), _rows(W), _full((D, D)), _full((4, W, 256)), _full((4, W, 256)), _full((1, 2 * D))],
        [SDS((L, NIN), BF16), SDS((L, W), F32), SDS((L, W), F32),
         SDS((D, D), F32), SDS((4, W, 256), F32), SDS((4, W, 256), F32), SDS((1, 2 * D), F32)],
        (dxn, mg, proj, proj, pa, pb, ya, yb, wo, wa, wb), jobs=jobs)


def _pool_bwd(l, dyb, proj, mixed, pooled, pw, scale, dproj, jobs=()):
    def body(dyb_ref, zb_ref, mixed_ref, pooled_ref, pw_ref, sc_ref, _, db_ref, dpw_ref, dsc_ref, dbias_ref, buf):
        i = pl.program_id(0)
        tile = NT - 1 - i

        @pl.when(i == 0)
        def _():
            dpw_ref[...] = jnp.zeros_like(dpw_ref)
            dsc_ref[...] = jnp.zeros_like(dsc_ref)
            dbias_ref[...] = jnp.zeros_like(dbias_ref)
            buf[TM:TM + 16, :] = jnp.zeros((16, W), F32)

        dyb = dyb_ref[...]
        zb = zb_ref[...]
        mixed = mixed_ref[...]
        sc = sc_ref[...]
        sg = _sigmoid(zb)
        dyb0 = dyb * (zb * sg)
        dzb = dyb * (mixed * sc) * (sg * (1.0 + zb * (1.0 - sg)))
        db_ref[:, W:] = dzb.astype(BF16)
        dbias_ref[:, W:] += jnp.sum(dzb, axis=0, keepdims=True)
        dsc_ref[...] += jnp.sum(dyb0 * mixed, axis=0, keepdims=True)
        dmix = (dyb0 * sc).astype(BF16)
        t = tile * TM + lax.broadcasted_iota(jnp.int32, (TM, 128), 0)
        for gi, win in enumerate(WINS):
            cs = slice(128 * gi, 128 * (gi + 1))
            dpw_ref[gi] += _dot_tn(pooled_ref[:, cs], dmix[:, cs])
            dpool = _dot_nt(dmix[:, cs], pw_ref[gi])
            cnt = jnp.minimum(t + 1, win).astype(F32)
            e = dpool / cnt
            buf[0:TM, cs] = e
            acc = e - dpool
            for k in range(1, win):
                acc = acc + buf[k:k + TM, cs]
            db_ref[:, cs] = acc.astype(BF16)
            dbias_ref[:, cs] += jnp.sum(acc, axis=0, keepdims=True)
        buf[TM:TM + 16, :] = buf[0:16, :]

    return _pcall(
        body, f"pool_bwd_l{l}", (NT,),
        [_rows(W, 0, True), _rows(W, 3, True), _rows(W, 0, True), _rows(W, 0, True),
         _layer(l, (4, 128, 128)), _layer(l, (1, W)), _ANY],
        [_rows(2 * W, 1, True), _full((4, 128, 128)), _full((1, W)), _full((1, 2 * W))],
        [SDS((L, NIN), BF16), SDS((4, 128, 128), F32), SDS((1, W), F32), SDS((1, 2 * W), F32)],
        (dyb, proj, mixed, pooled, pw, scale, dproj),
        scratch=[pltpu.VMEM((TM + 16, W), F32)], aliases={6: 0}, jobs=jobs)


def _s5_bwd(l, dya, up, zp, y1, q, sre, sim, cret, cimnt, bret, bimt, co, dsk, wglu, pm, pmt, dproj, jobs=()):
    def halo(i):
        return (jnp.maximum((NT - 1 - i) * (TM // 8) - 1, 0), 0)

    def body(dya_ref, u_ref, z_ref, y1_ref, q_ref, sre_ref, sim_ref, hre_ref, him_ref,
             cret_ref, cimnt_ref, bret_ref, bimt_ref, *rest):
        co_refs, (d_ref, wg_ref, pm_ref, pmt_ref, _,
                  da_ref, dwg_ref, dbg_ref, dd_ref, dcre_ref, dcimn_ref, dbre_ref, dbim_ref, dare_ref, daim_ref,
                  dbias_ref, lre, lim, car_ref, cseg_ref) = rest[:8], rest[8:]
        i = pl.program_id(0)
        tile = NT - 1 - i

        @pl.when(i == 0)
        def _():
            for ref in (dwg_ref, dbg_ref, dd_ref, dcre_ref, dcimn_ref, dbre_ref, dbim_ref, dare_ref, daim_ref, dbias_ref,
                        car_ref):
                ref[...] = jnp.zeros_like(ref)

        pmt = pmt_ref[...]
        u = u_ref[...]
        za = z_ref[...]
        y1 = y1_ref[...]
        dya = _permute_f32(pm_ref[...], dya_ref[...])
        y2 = _gelu(y1)
        sg = _sigmoid(q_ref[...])
        sgz = _sigmoid(za)
        dy3 = dya * (za * sgz)
        dza = dya * (y2 * sg) * (sgz * (1.0 + za * (1.0 - sgz)))
        da_ref[:, W:] = _dot(pmt, dza.astype(BF16)).astype(BF16)
        dbias_ref[:, W:] += jnp.sum(dza, axis=0, keepdims=True)
        dq = dy3 * y2 * (sg * (1.0 - sg))
        dqb = dq.astype(BF16)
        dy2 = dy3 * sg + _dot_nt(dqb, wg_ref[...])
        dwg_ref[...] += _dot_tn(y2.astype(BF16), dqb)
        dbg_ref[...] += jnp.sum(dq, axis=0, keepdims=True)
        dy1 = dy2 * _gelu_grad(y1)
        dd_ref[...] += jnp.sum(dy1 * u, axis=0, keepdims=True)
        dy1b = dy1.astype(BF16)
        ub = u.astype(BF16)
        for k in range(4):
            blk = slice(512 * k, 512 * (k + 1))
            ks = slice(128 * k, 128 * (k + 1))
            lre[:, blk] = _dot(dy1b[:, ks], cret_ref[k])
            lim[:, blk] = _dot(dy1b[:, ks], cimnt_ref[k])
            dcre_ref[k] += _dot_tn(sre_ref[:, blk].astype(BF16), dy1b[:, ks])
            dcimn_ref[k] += _dot_tn(sim_ref[:, blk].astype(BF16), dy1b[:, ks])
        _scan_tile(lre, lim, co_refs, car_ref, cseg_ref, True)

        rowid = lax.broadcasted_iota(jnp.int32, (8, LB), 0)
        gate = (tile > 0).astype(F32)
        for lb in range(GP // LB):
            cols = slice(lb * LB, (lb + 1) * LB)
            pr = jnp.where(rowid == 0, hre_ref[7:8, cols] * gate, pltpu.roll(sre_ref[TM - 8:TM, cols], 1, 0))
            pi = jnp.where(rowid == 0, him_ref[7:8, cols] * gate, pltpu.roll(sim_ref[TM - 8:TM, cols], 1, 0))
            lr = lre[0:8, cols]
            li = lim[0:8, cols]

            def step(j, acc):
                rows = pl.ds(pl.multiple_of(j * 8, 8), 8)
                prev = pl.ds(pl.multiple_of((j - 1) * 8, 8), 8)
                sr = sre_ref[prev, cols]
                si = sim_ref[prev, cols]
                lr = lre[rows, cols]
                li = lim[rows, cols]
                return acc[0] + (sr * lr + si * li), acc[1] + (sr * li - si * lr)

            acc = lax.fori_loop(1, SEG, step, (pr * lr + pi * li, pr * li - pi * lr))
            dare_ref[:, cols] += acc[0]
            daim_ref[:, cols] += acc[1]

        for k in range(4):
            blk = slice(512 * k, 512 * (k + 1))
            ks = slice(128 * k, 128 * (k + 1))
            lrb = lre[:, blk].astype(BF16)
            lib = lim[:, blk].astype(BF16)
            du = dy1[:, ks] * d_ref[:, ks] + _dot(lrb, bret_ref[k]) + _dot(lib, bimt_ref[k])
            da_ref[:, ks] = _dot(pmt, du.astype(BF16)).astype(BF16)
            dbias_ref[:, ks] += jnp.sum(du, axis=0, keepdims=True)
            dbre_ref[k] += _dot_tn(ub[:, ks], lrb)
            dbim_ref[k] += _dot_tn(ub[:, ks], lib)

    return _pcall(
        body, f"s5_bwd_l{l}", (NT,),
        [_rows(W, 0, True), _rows(W, 0, True), _rows(W, 0, True), _rows(W, 0, True), _rows(W, 0, True),
         _rows(GP, 0, True), _rows(GP, 0, True),
         pl.BlockSpec((8, GP), halo), pl.BlockSpec((8, GP), halo),
         _layer(l, (4, 128, 512)), _layer(l, (4, 128, 512)), _layer(l, (4, 512, 128)), _layer(l, (4, 512, 128))]
        + [_layer(l, s) for s in _SCAN_CO]
        + [_layer(l, (1, W)), _layer(l, (W, W)), _full((TM, TM)), _full((TM, TM)), _ANY],
        [_rows(2 * W, 0, True), _full((W, W)), _full((1, W)), _full((1, W)),
         _full((4, 512, 128)), _full((4, 512, 128)), _full((4, 128, 512)), _full((4, 128, 512)),
         _full((8, GP)), _full((8, GP)), _full((1, 2 * W))],
        [SDS((L, NIN), BF16), SDS((W, W), F32), SDS((1, W), F32), SDS((1, W), F32),
         SDS((4, 512, 128), F32), SDS((4, 512, 128), F32), SDS((4, 128, 512), F32), SDS((4, 128, 512), F32),
         SDS((8, GP), F32), SDS((8, GP), F32), SDS((1, 2 * W), F32)],
        (dya, up, zp, y1, q, sre, sim, sre, sim, cret, cimnt, bret, bimt, *co, dsk, wglu, pm, pmt, dproj),
        scratch=[pltpu.VMEM((TM, GP), F32), pltpu.VMEM((TM, GP), F32), pltpu.VMEM((8, GP), F32),
                 pltpu.VMEM((16, GP), F32)],
        aliases={25: 0}, jobs=jobs)


def _inproj_dw(l, r, h, dproj, jobs=()):
    def body(h_ref, dp_ref, dw_ref):
        part = _dot_tn(h_ref[...], dp_ref[...])

        @pl.when(pl.program_id(1) == 0)
        def _():
            dw_ref[...] = part

        @pl.when(pl.program_id(1) > 0)
        def _():
            dw_ref[...] += part

    return _pcall(
        body, f"inproj_dw{r}_l{l}", (4, L // TK),
        [pl.BlockSpec((TK, D // 2), lambda j, i: (i, r)), pl.BlockSpec((TK, 1024), lambda j, i: (i, j))],
        [pl.BlockSpec((None, D // 2, 1024), lambda j, i: (j, 0, 0))],
        [SDS((4, D // 2, 1024), F32)],
        (h, dproj), jobs=jobs)


def _inproj_dx(l, dproj, w, x, g, dxn, jobs=()):
    def body(dp_ref, w_ref, x_ref, g_ref, dxn_ref, dx_ref, dg_ref):
        @pl.when(pl.program_id(0) == 0)
        def _():
            dg_ref[...] = jnp.zeros_like(dg_ref)

        dh = _dot_nt(dp_ref[:, 0:1024], w_ref[0])
        for j in range(1, 4):
            dh = dh + _dot_nt(dp_ref[:, j * 1024:(j + 1) * 1024], w_ref[j])
        xv = x_ref[...]
        r = lax.rsqrt(jnp.mean(xv * xv, axis=-1, keepdims=True) + EPS)
        xn = xv * r
        dg_ref[...] += jnp.sum(dh * xn, axis=0, keepdims=True)
        dn = dh * g_ref[...]
        dx_ref[...] = dxn_ref[...] + r * (dn - xn * jnp.mean(dn * xn, axis=-1, keepdims=True))

    return _pcall(
        body, f"inproj_dx_l{l}", (L // TMM,),
        [_rows_mm(NIN), _layer(l, (4, D, 1024)), _rows_mm(D), _layer(l, (1, D)), _rows_mm(D)],
        [_rows_mm(D), _full((1, D))],
        [SDS((L, D), F32), SDS((1, D), F32)],
        (dproj, w, x, g, dxn), jobs=jobs)


def _discretize(log_dt, lam_re, lam_im, b_re, b_im):
    dt = jnp.exp(log_dt)[..., None]
    mag = jnp.exp(lam_re * dt)
    ang = lam_im * dt
    abar_re = mag * jnp.cos(ang)
    abar_im = mag * jnp.sin(ang)
    num_re = abar_re - 1.0
    num_im = abar_im
    den = lam_re * lam_re + lam_im * lam_im
    coef_re = (num_re * lam_re + num_im * lam_im) / den
    coef_im = (num_im * lam_re - num_re * lam_im) / den
    bbar_re = coef_re[..., None] * b_re - coef_im[..., None] * b_im
    bbar_im = coef_re[..., None] * b_im + coef_im[..., None] * b_re
    return abar_re, abar_im, bbar_re, bbar_im


def _cmul(ar, ai, br, bi):
    return ar * br - ai * bi, ar * bi + ai * br


def _power_table(ar, ai, doublings):
    tr, ti, pr, pi = ar, ai, ar, ai
    for _ in range(doublings):
        nr, ni = _cmul(tr, ti, pr, pi)
        tr, ti = jnp.concatenate([tr, nr], axis=1), jnp.concatenate([ti, ni], axis=1)
        pr, pi = _cmul(pr, pi, pr, pi)
    return tr, ti


def _powers(abar_re, abar_im):
    ar, ai = abar_re.reshape(DEPTH, 1, GP), abar_im.reshape(DEPTH, 1, GP)
    tr, ti = _power_table(ar, ai, 5)
    sr, si = _power_table(tr[:, -1:], ti[:, -1:], 3)
    row = jnp.arange(8)[:, None]
    eight = lambda t: jnp.broadcast_to(t, (DEPTH, 8, GP))
    rows8 = lambda t: jnp.repeat(t, 8, axis=1)

    def steps(rows, keep):
        return jnp.concatenate([jnp.where(keep(d), rows[:, d - 1:d], 0.0) for d in (1, 2, 4)], axis=1)

    fwd = (eight(ar), eight(ai), steps(sr, lambda d: row >= d), steps(si, lambda d: row >= d), sr, si,
           rows8(tr), rows8(ti))
    rev = (eight(ar), eight(-ai), steps(sr, lambda d: row < 8 - d), steps(-si, lambda d: row < 8 - d),
           sr[:, ::-1], -si[:, ::-1], rows8(tr[:, ::-1]), rows8(-ti[:, ::-1]))
    return fwd, rev


def _segment_order():
    rho = jnp.arange(TM)
    src = (rho % 8) * SEG + rho // 8
    pm = (src[:, None] == jnp.arange(TM)[None, :]).astype(BF16)
    return pm, pm.T


_EYE8 = functools.partial(jnp.eye, 8, dtype=F32)


def _expand_in(b):
    return jnp.einsum("lkgpc,gh->lkgchp", b.reshape(DEPTH, 4, 8, P, C), _EYE8()).reshape(DEPTH, 4, 128, 512)


def _extract_in(e):
    return jnp.einsum("lkgchp,gh->lkgpc", e.reshape(DEPTH, 4, 8, C, 8, P), _EYE8()).reshape(DEPTH, G, P, C)


def _expand_out(c):
    return jnp.einsum("lkgcp,gh->lkgphc", c.reshape(DEPTH, 4, 8, C, P), _EYE8()).reshape(DEPTH, 4, 512, 128)


def _extract_out(e):
    return jnp.einsum("lkgphc,gh->lkgcp", e.reshape(DEPTH, 4, 8, P, 8, C), _EYE8()).reshape(DEPTH, G, C, P)


SMALL = ("norm_g", "b_in", "ssm_log_dt", "ssm_lam_re", "ssm_lam_im", "ssm_b_re", "ssm_b_im",
         "ssm_c_re", "ssm_c_im", "ssm_d", "ssm_b_glu", "pool_w", "pool_scale")
BIG = ("w_in", "ssm_w_glu", "w_branch_a", "w_branch_b", "w_out")


def _step(x, target, w, m, v, place):
    sp = {n: w[n] for n in SMALL}
    final_norm_g = w["final_norm_g"]
    wbuf = dict(zip(BIG, _cast_own(place, [w[n] for n in BIG])))
    (abar_re, abar_im, bbar_re, bbar_im), disc_vjp = jax.vjp(
        _discretize, *(sp[n] for n in ("ssm_log_dt", "ssm_lam_re", "ssm_lam_im", "ssm_b_re", "ssm_b_im")))
    powers_fwd, powers_rev = _powers(abar_re, abar_im)
    b_re_x, b_im_x = _expand_in(bbar_re), _expand_in(bbar_im)
    c_re_x, c_imn_x = _expand_out(sp["ssm_c_re"]), _expand_out(-sp["ssm_c_im"])
    b_x = jnp.concatenate([b_re_x, b_im_x], axis=3).astype(BF16)
    t = lambda a: jnp.swapaxes(a, 2, 3).astype(BF16)
    c_re_t, c_imn_t, b_re_t, b_im_t = t(c_re_x), t(c_imn_x), t(b_re_x), t(b_im_x)
    c_re_x, c_imn_x = c_re_x.astype(BF16), c_imn_x.astype(BF16)
    row = lambda n: sp[n].reshape(DEPTH, 1, -1)
    g, b_in, dsk, b_glu, scale = row("norm_g"), row("b_in"), row("ssm_d"), row("ssm_b_glu"), row("pool_scale")
    pw = sp["pool_w"].astype(BF16)
    pm, pmt = _segment_order()

    saved = []
    for l in range(DEPTH):
        three = BIG[2:]
        if l == 0:
            wbuf["w_in"], wbuf["ssm_w_glu"] = _comm_only(
                "gather_first", _GatherJob([wbuf["w_in"], wbuf["ssm_w_glu"]], 0))[0]
            jobs = [_GatherJob([wbuf[n] for n in three], 0), _GatherJob([wbuf["ssm_w_glu"]], 1)]
        else:
            jobs = []
        (h, proj), res = _norm_inproj(l, x, g, wbuf["w_in"], b_in, jobs)
        if res:
            wbuf.update(zip(three, res[0][0]))
            (wbuf["ssm_w_glu"],) = res[1][0]
        wg = dict(wbuf, ssm_w_glu=wbuf["ssm_w_glu"].reshape(DEPTH, W, W), w_out=wbuf["w_out"].reshape(DEPTH, D, D))
        job = _GatherJob([wbuf["w_in"]], l + 1) if l + 1 < DEPTH else _GatherJob([wbuf[n] for n in three], l)
        (sre, sim, y1, q, ya, up, zp), res = _s5_fwd(
            l, proj, b_x, c_re_x, c_imn_x, powers_fwd, dsk, wg["ssm_w_glu"], b_glu, pm, pmt, job)
        if l + 1 < DEPTH:
            (wbuf["w_in"],) = res[0][0]
        else:
            wbuf.update(zip(three, res[0][0]))
        wg = dict(wbuf, ssm_w_glu=wbuf["ssm_w_glu"].reshape(DEPTH, W, W), w_out=wbuf["w_out"].reshape(DEPTH, D, D))
        pooled, mixed, yb = _pool_fwd(l, proj, pw, scale)
        pa, pb, mg, x_next = _merge_out(l, ya, yb, proj, x, wg["w_branch_a"], wg["w_branch_b"], wg["w_out"])
        saved.append(dict(x=x, h=h, proj=proj, sre=sre, sim=sim, y1=y1, q=q, ya=ya, up=up, zp=zp,
                          pooled=pooled, mixed=mixed, yb=yb, pa=pa, pb=pb, mg=mg))
        x = x_next

    loss, dx, dgf = _loss_head(x, final_norm_g.reshape(1, D), target)

    per_layer = {n: [None] * DEPTH for n in ("norm_g", "b_in", "ssm_d", "ssm_b_glu", "pool_w", "pool_scale",
                                             "dare", "daim", "dbre", "dbim", "dcre", "dcimn")}
    red = _Reducer(place, w, m, v)
    for l in reversed(range(DEPTH)):
        s = saved[l]
        (dproj, dya, dyb, dwo, dwa, dwb, dbias_g), res = _merge_out_bwd(
            l, dx, s["mg"], s["proj"], s["pa"], s["pb"], s["ya"], s["yb"],
            wg["w_out"], wg["w_branch_a"], wg["w_branch_b"], red.jobs())
        red.land(res)
        (dproj, dpw, dsc, dbias_b), res = _pool_bwd(l, dyb, s["proj"], s["mixed"], s["pooled"], pw, scale, dproj,
                                                    red.jobs("pool_bwd"))
        red.land(res)
        (dproj, dwg, dbg, dd, dcre, dcimn, dbre, dbim, dare, daim, dbias_a), res = _s5_bwd(
            l, dya, s["up"], s["zp"], s["y1"], s["q"], s["sre"], s["sim"], c_re_t, c_imn_t, b_re_t, b_im_t,
            powers_rev, dsk, wg["ssm_w_glu"], pm, pmt, dproj, red.jobs())
        red.land(res)
        rest = [dwg.reshape(4, W // 4, W), dwa, dwb, dwo.reshape(4, D // 4, D)]
        if l == 0:
            red.add(l, "rest", BIG[1:], rest)
        halves = []
        for r in range(2):
            outs, res = _inproj_dw(l, r, s["h"], dproj, red.jobs())
            red.land(res)
            halves += outs
            if l == 0:
                red.add(l, f"in{r}", BIG[:1], outs, [r * (D // 2)])
        (dx, dg), res = _inproj_dx(l, dproj, wg["w_in"], s["x"], g, dx, red.jobs())
        red.land(res)
        if l > 0:
            red.add(l, "all", BIG[:1] * 2 + BIG[1:], halves + rest, [0, D // 2, 0, 0, 0, 0], avoid=("pool_bwd",))
        for n, a in (("norm_g", dg.reshape(D)), ("b_in", jnp.concatenate([dbias_a, dbias_b, dbias_g], axis=1).reshape(NIN)),
                     ("ssm_d", dd.reshape(W)), ("ssm_b_glu", dbg.reshape(W)), ("pool_w", dpw), ("pool_scale", dsc.reshape(W)),
                     ("dare", dare), ("daim", daim), ("dbre", dbre), ("dbim", dbim), ("dcre", dcre), ("dcimn", dcimn)):
            per_layer[n][l] = a
    gs = {n: jnp.stack(a) for n, a in per_layer.items()}
    d_abar = [jnp.sum(gs.pop(n), axis=1).reshape(DEPTH, G, P) for n in ("dare", "daim")]
    (gs["ssm_log_dt"], gs["ssm_lam_re"], gs["ssm_lam_im"], gs["ssm_b_re"], gs["ssm_b_im"]) = disc_vjp(
        (*d_abar, _extract_in(gs.pop("dbre")), _extract_in(gs.pop("dbim"))))
    gs["ssm_c_re"], gs["ssm_c_im"] = _extract_out(gs.pop("dcre")), -_extract_out(gs.pop("dcimn"))
    gs["final_norm_g"] = dgf

    natural = {n: w[n].shape for n in REPLICATED}
    rw, rm, rv = {}, {}, {}
    for n in REPLICATED:
        shape = DENSE.get(n, natural[n])
        gs[n], rw[n], rm[n], rv[n] = (a.reshape(shape) for a in (gs[n], w[n], m[n], v[n]))
    small = [gs[n] for n in REPLICATED] + [loss]
    jobs = red.jobs()
    res = _pcall(None, "tail_exchange", (), [], [], [], [], jobs=jobs + [_SiblingJob(small, False)])[1]
    red.land(res[:len(jobs)])
    pair_small = _small_pair_sum(place, small, res[-1][1], [BF16 if a.ndim > 2 else F32 for a in small])
    jobs = red.jobs()
    res = _pcall(None, "tail_gather", (), [], [], [], [], jobs=jobs + [_ChipGatherJob(pair_small)])[1]
    red.land(res[:len(jobs)])
    assert not red.active
    small_parts = dict(zip(REPLICATED + ("loss",), res[-1][0]))

    k = len(REPLICATED)
    outs = _adamw_small("adamw_small", [rw[n] for n in REPLICATED], [small_parts[n] for n in REPLICATED],
                        [rm[n] for n in REPLICATED], [rv[n] for n in REPLICATED], small_parts["loss"])
    results = {n: red.big[n] for n in BIG}
    results.update({n: [outs[1 + q * k + i].reshape(natural[n]) for q in range(4)] for i, n in enumerate(REPLICATED)})
    return outs[0][0, 0], dx, results


def _place():
    x, y, c = lax.axis_index("x"), lax.axis_index("y"), lax.axis_index("c")
    chips = [(1 - x, y), (x, 1 - y), (1 - x, 1 - y)]
    return x, y, c, 2 * x + y, chips, [2 * cx + cy for cx, cy in chips]


def _remote(src, dst, ssem, rsem, dev):
    return pltpu.make_async_remote_copy(src_ref=src, dst_ref=dst, send_sem=ssem, recv_sem=rsem,
                                        device_id=dev, device_id_type=MESH)


class _GatherJob:
    def __init__(self, bufs, l):
        self.srcs, self.bufs, self.news, self.l = [], list(bufs), [], l
        self.scratch = [pltpu.SemaphoreType.DMA((len(self.bufs), 3))] * 4

    def _half(self, ref, k, h):
        rows = ref.shape[2] // 2
        return ref.at[self.l, k, pl.ds(pl.multiple_of(h * rows, 8), rows), :]

    def _ici(self, bufs, sems, a, j, k):
        _, _, c, _, chips, _ = _place()
        blk = self._half(bufs[a], k, c)
        return _remote(blk, blk, sems[0].at[a, j], sems[1].at[a, j], (*chips[j], c))

    def _d2d(self, bufs, sems, a, j, k, h):
        x, y, c, _, _, _ = _place()
        blk = self._half(bufs[a], k, h)
        return _remote(blk, blk, sems[2].at[a, j], sems[3].at[a, j], (x, y, 1 - c))

    def start(self, srcs, bufs, news, sems):
        me = _place()[3]
        for a in range(len(self.bufs)):
            for j in range(3):
                self._ici(bufs, sems, a, j, me).start()

    def finish(self, srcs, bufs, news, sems):
        _, _, c, me, _, cid = _place()
        pairs = [(a, j) for a in range(len(self.bufs)) for j in range(3)]
        for a, j in pairs:
            self._ici(bufs, sems, a, j, cid[j]).wait_recv()
            self._d2d(bufs, sems, a, j, cid[j], c).start()
        for a, j in pairs:
            self._d2d(bufs, sems, a, j, cid[j], 1 - c).wait_recv()
        for a, j in pairs:
            self._ici(bufs, sems, a, j, me).wait_send()
            self._d2d(bufs, sems, a, j, cid[j], c).wait_send()


class _SiblingJob:
    def __init__(self, srcs, rows_half):
        self.srcs, self.bufs, self.rows_half = list(srcs), [], rows_half
        self.news = [SDS((s.shape[0], s.shape[1] // 2, s.shape[2]) if rows_half else s.shape, s.dtype) for s in srcs]
        self.scratch = [pltpu.SemaphoreType.DMA((len(self.srcs),))] * 2

    def _copy(self, srcs, news, sems, a):
        x, y, c, _, _, _ = _place()
        src = srcs[a]
        if self.rows_half:
            rows = src.shape[1] // 2
            src = src.at[:, pl.ds(pl.multiple_of((1 - c) * rows, 8), rows), :]
        return _remote(src, news[a], sems[0].at[a], sems[1].at[a], (x, y, 1 - c))

    def start(self, srcs, bufs, news, sems):
        for a in range(len(self.srcs)):
            self._copy(srcs, news, sems, a).start()

    def finish(self, srcs, bufs, news, sems):
        for a in range(len(self.srcs)):
            self._copy(srcs, news, sems, a).wait()


class _ScatterJob:
    def __init__(self, parts):
        self.srcs, self.bufs = list(parts), []
        self.news = [SDS((3,) + p.shape[1:], p.dtype) for p in parts]
        self.scratch = [pltpu.SemaphoreType.DMA((len(self.srcs), 3))] * 2

    def _copy(self, srcs, news, sems, a, j):
        _, _, c, _, chips, cid = _place()
        return _remote(srcs[a].at[cid[j]], news[a].at[j], sems[0].at[a, j], sems[1].at[a, j], (*chips[j], c))

    def start(self, srcs, bufs, news, sems):
        for a in range(len(self.srcs)):
            for j in range(3):
                self._copy(srcs, news, sems, a, j).start()

    def finish(self, srcs, bufs, news, sems):
        for a in range(len(self.srcs)):
            for j in range(3):
                self._copy(srcs, news, sems, a, j).wait()


def _comm_only(name, job):
    return _pcall(None, name, (), [], [], [], [], jobs=[job])[1][0]


class _ChipGatherJob(_GatherJob):
    def __init__(self, bufs):
        super().__init__(bufs, None)

    def _half(self, ref, k, h):
        return ref.at[k, h]


def _cast_own(place, ws):
    n = len(ws)

    def body(p_ref, *refs):
        for i_ref, o_ref in zip(refs[:n], refs[n:]):
            o_ref[...] = i_ref[...].astype(BF16)

    return pl.pallas_call(
        body, name="cast_own_shards",
        grid_spec=pltpu.PrefetchScalarGridSpec(
            num_scalar_prefetch=1, grid=(DEPTH,),
            in_specs=[pl.BlockSpec((None,) + a.shape[1:], lambda l, p: (l, 0, 0)) for a in ws],
            out_specs=[pl.BlockSpec((None, None) + a.shape[1:], lambda l, p: (l, p[1], 0, 0)) for a in ws]),
        out_shape=[SDS((DEPTH, 4) + a.shape[1:], BF16) for a in ws],
        compiler_params=_params("arbitrary"),
    )(place, *ws)


def _half_tiles(a_):
    rows = a_ // 2
    ta = min(rows, 256)
    return rows, ta, rows // ta


def _pair_sums_bf16(name, place, owns, recvs):
    n = len(owns)

    def body(p_ref, *refs):
        for own_ref, recv_ref, out_ref in zip(refs[:n], refs[n:2 * n], refs[2 * n:]):
            out_ref[...] = (own_ref[...] + recv_ref[...]).astype(BF16)

    def own_half(a):
        return pl.BlockSpec((None, a.shape[1] // 2, a.shape[2]), lambda s, p: (s, p[0], 0))

    def block(a):
        return pl.BlockSpec((None,) + a.shape[1:], lambda s, p: (s, 0, 0))

    return pl.pallas_call(
        body, name=name,
        grid_spec=pltpu.PrefetchScalarGridSpec(
            num_scalar_prefetch=1, grid=(4,),
            in_specs=[own_half(a) for a in owns] + [block(r) for r in recvs],
            out_specs=[block(r) for r in recvs]),
        out_shape=[SDS(r.shape, BF16) for r in recvs],
        compiler_params=_params("arbitrary"),
    )(place, *owns, *recvs)


def _shard_sums(name, place, owns, recvs, rbufs):
    n = len(owns)

    def body(p_ref, *refs):
        for own_ref, recv_ref, r_ref, out_ref in zip(refs[:n], refs[n:2 * n], refs[2 * n:3 * n], refs[3 * n:]):
            acc = own_ref[...] + recv_ref[...]
            for j in range(3):
                acc = acc + r_ref[j].astype(F32)
            out_ref[...] = acc

    def own_half(a):
        return pl.BlockSpec((None, a.shape[1] // 2, a.shape[2]), lambda i, p: (p[1], p[0], 0))

    def recv_block(a):
        return pl.BlockSpec((None,) + a.shape[1:], lambda i, p: (p[1], 0, 0))

    return pl.pallas_call(
        body, name=name,
        grid_spec=pltpu.PrefetchScalarGridSpec(
            num_scalar_prefetch=1, grid=(1,),
            in_specs=([own_half(a) for a in owns] + [recv_block(r) for r in recvs]
                      + [pl.BlockSpec(rb.shape, lambda i, p: (0, 0, 0)) for rb in rbufs]),
            out_specs=[pl.BlockSpec(r.shape[1:], lambda i, p: (0, 0)) for r in recvs]),
        out_shape=[SDS(r.shape[1:], F32) for r in recvs],
        compiler_params=_params("arbitrary"),
    )(place, *owns, *recvs, *rbufs)


def _small_pair_sum(place, mine, recv, dtypes):
    n = len(mine)

    def body(p_ref, *refs):
        for m_ref, r_ref, o_ref in zip(refs[:n], refs[n:2 * n], refs[2 * n:]):
            o_ref[...] = (m_ref[...] + r_ref[...]).astype(o_ref.dtype)

    def whole(a):
        zeros = (0,) * a.ndim
        return pl.BlockSpec(a.shape, lambda i, p: zeros)

    def mine_blk(a):
        zeros = (0,) * a.ndim
        return pl.BlockSpec((None,) + a.shape, lambda i, p: (p[1],) + zeros)

    return pl.pallas_call(
        body, name="small_pair_sum",
        grid_spec=pltpu.PrefetchScalarGridSpec(
            num_scalar_prefetch=1, grid=(1,),
            in_specs=[whole(a) for a in mine] + [whole(a) for a in recv],
            out_specs=[mine_blk(a) for a in mine]),
        out_shape=[SDS((4,) + a.shape, dt) for a, dt in zip(mine, dtypes)],
        compiler_params=_params("arbitrary"),
    )(place, *mine, *recv)


def _adam_math(w, g, m, v):
    m = B1 * m + (1.0 - B1) * g
    v = B2 * v + (1.0 - B2) * (g * g)
    m_hat = m / (1.0 - B1 ** STEP)
    v_hat = v / (1.0 - B2 ** STEP)
    delta = -LR * (m_hat / (jnp.sqrt(v_hat) + EPS_A) + WD * w)
    return delta, m, v


def _adamw_big(name, l, row0, w, m, v, mine, other, prev, jobs=()):
    _, _, b_ = w.shape
    _, ta, nh = _half_tiles(2 * mine.shape[0])
    prev = list(prev or [])

    def body(w_ref, m_ref, v_ref, mine_ref, other_ref, *rest):
        g_ref, d_ref, mo_ref, vo_ref = rest[len(prev):]
        g = jnp.where(pl.program_id(0) == lax.axis_index("c"), mine_ref[...], other_ref[...])
        g_ref[...] = g
        d_ref[...], mo_ref[...], vo_ref[...] = _adam_math(w_ref[...], g, m_ref[...], v_ref[...])

    slab = pl.BlockSpec((None, ta, b_), lambda h, i: (l, row0 // ta + h * nh + i, 0))
    half = pl.BlockSpec((ta, b_), lambda h, i: (i, 0))
    outs, res = _pcall(
        body, name, (2, nh), [slab, slab, slab, half, half] + [_ANY] * len(prev), [slab] * 4, [SDS(w.shape, F32)] * 4,
        (w, m, v, mine, other, *prev), aliases={5 + k: k for k in range(len(prev))}, jobs=jobs)
    return outs, res


def _adamw_small(name, ws, parts, ms, vs, loss_parts=None):
    k = len(ws)
    extra = [] if loss_parts is None else [loss_parts]

    def chip_sum(p_ref):
        p = [p_ref[k].astype(F32) for k in range(4)]
        return ((p[0] + p[1]) + p[2]) + p[3]

    def body(*refs):
        w_refs, p_refs, m_refs, v_refs = refs[:k], refs[k:2 * k], refs[2 * k:3 * k], refs[3 * k:4 * k]
        outs = refs[4 * k + len(extra):]
        if extra:
            outs[0][...] = chip_sum(refs[4 * k])
            outs = outs[1:]
        for a in range(k):
            g = chip_sum(p_refs[a])
            outs[a][...] = g
            outs[k + a][...], outs[2 * k + a][...], outs[3 * k + a][...] = _adam_math(
                w_refs[a][...], g, m_refs[a][...], v_refs[a][...])

    like = [SDS(a.shape, F32) for a in ws]
    return pl.pallas_call(
        body, name=name,
        out_shape=([SDS(loss_parts.shape[1:], F32)] if extra else []) + like * 4,
        compiler_params=pltpu.CompilerParams(vmem_limit_bytes=VMEM_LIMIT),
    )(*ws, *parts, *ms, *vs, *extra)


class _Reducer:
    def __init__(self, place, w, m, v):
        self.place, self.w, self.m, self.v = place, w, m, v
        self.active, self.riding = [], []
        self.big = {n: None for n in BIG}

    def add(self, l, tag, names, own, row0s=None, avoid=()):
        self.active.append(dict(l=l, key=f"{tag}_l{l}", names=names, own=list(own), row0s=row0s or [0] * len(names),
                                avoid=avoid, stage=0))

    def jobs(self, kind=""):
        self.riding = [g for g in self.active if kind not in g["avoid"]]
        return [(_SiblingJob(g["own"], True), _ScatterJob(g.get("parts", [])), _SiblingJob(g.get("shard", []), False))
                [g["stage"]] for g in self.riding]

    def land(self, res):
        for g, (_, news) in zip(self.riding, res):
            if g["stage"] == 0:
                g["recv"] = news
                g["parts"] = _pair_sums_bf16(f"pair_sums_{g['key']}", self.place, g["own"], news)
            elif g["stage"] == 1:
                g["shard"] = _shard_sums(f"shard_sums_{g['key']}", self.place, g["own"], g["recv"], news)
            else:
                for n, mine, other, row0 in zip(g["names"], g["shard"], news, g["row0s"]):
                    self.big[n] = _adamw_big(f"adamw_{n}_{row0}_{g['key']}", g["l"], row0, self.w[n], self.m[n], self.v[n],
                                             mine, other, self.big[n])[0]
                self.active.remove(g)
            g["stage"] += 1
        self.riding = []


WEIGHTS = ("norm_g", "w_in", "b_in", "ssm_log_dt", "ssm_lam_re", "ssm_lam_im", "ssm_b_re", "ssm_b_im", "ssm_c_re",
           "ssm_c_im", "ssm_d", "ssm_w_glu", "ssm_b_glu", "pool_w", "pool_scale", "w_branch_a", "w_branch_b", "w_out",
           "final_norm_g")
REPLICATED = SMALL + ("final_norm_g",)
DENSE = {"ssm_b_re": (DEPTH, G, P * C), "ssm_b_im": (DEPTH, G, P * C), "final_norm_g": (2, D // 2)}


def kernel(x, norm_g, w_in, b_in, ssm_log_dt, ssm_lam_re, ssm_lam_im, ssm_b_re, ssm_b_im, ssm_c_re, ssm_c_im, ssm_d, ssm_w_glu, ssm_b_glu, pool_w, pool_scale, w_branch_a, w_branch_b, w_out, final_norm_g, loss_target, m_norm_g, m_w_in, m_b_in, m_ssm_log_dt, m_ssm_lam_re, m_ssm_lam_im, m_ssm_b_re, m_ssm_b_im, m_ssm_c_re, m_ssm_c_im, m_ssm_d, m_ssm_w_glu, m_ssm_b_glu, m_pool_w, m_pool_scale, m_w_branch_a, m_w_branch_b, m_w_out, m_final_norm_g, v_norm_g, v_w_in, v_b_in, v_ssm_log_dt, v_ssm_lam_re, v_ssm_lam_im, v_ssm_b_re, v_ssm_b_im, v_ssm_c_re, v_ssm_c_im, v_ssm_d, v_ssm_w_glu, v_ssm_b_glu, v_pool_w, v_pool_scale, v_w_branch_a, v_w_branch_b, v_w_out, v_final_norm_g):
    w = dict(zip(WEIGHTS, (norm_g, w_in, b_in, ssm_log_dt, ssm_lam_re, ssm_lam_im, ssm_b_re, ssm_b_im, ssm_c_re,
                           ssm_c_im, ssm_d, ssm_w_glu, ssm_b_glu, pool_w, pool_scale, w_branch_a, w_branch_b, w_out,
                           final_norm_g)))
    m = dict(zip(WEIGHTS, (m_norm_g, m_w_in, m_b_in, m_ssm_log_dt, m_ssm_lam_re, m_ssm_lam_im, m_ssm_b_re, m_ssm_b_im,
                           m_ssm_c_re, m_ssm_c_im, m_ssm_d, m_ssm_w_glu, m_ssm_b_glu, m_pool_w, m_pool_scale,
                           m_w_branch_a, m_w_branch_b, m_w_out, m_final_norm_g)))
    v = dict(zip(WEIGHTS, (v_norm_g, v_w_in, v_b_in, v_ssm_log_dt, v_ssm_lam_re, v_ssm_lam_im, v_ssm_b_re, v_ssm_b_im,
                           v_ssm_c_re, v_ssm_c_im, v_ssm_d, v_ssm_w_glu, v_ssm_b_glu, v_pool_w, v_pool_scale,
                           v_w_branch_a, v_w_branch_b, v_w_out, v_final_norm_g)))
    place = jnp.stack([lax.axis_index("c"), 2 * lax.axis_index("x") + lax.axis_index("y")]).astype(jnp.int32)

    total_loss, dx, results = _step(x[0], loss_target[0], w, m, v, place)
    return (total_loss, dx[None], *[results[n][q] for q in range(4) for n in WEIGHTS])
```

```python
import functools

import jax
import jax.numpy as jnp
from jax import lax
from jax.experimental import pallas as pl
from jax.experimental.pallas import tpu as pltpu

F32, BF16 = jnp.float32, jnp.bfloat16
SDS = jax.ShapeDtypeStruct
MESH = pl.DeviceIdType.MESH

DEPTH = 2
L = 2048
D = 1024
NIN = 4096
W = 512
G, P, C = 32, 64, 16
GP = G * P
WINS = (2, 4, 8, 16)
TM = 256
NT = L // TM
TMM = 512
TK = 1024
EPS = 1e-6
VMEM_LIMIT = 56 * 2**20

LR, B1, B2, EPS_A, WD, STEP = 0.001, 0.9, 0.999, 1e-08, 0.01, 10


def _params(*sem):
    return pltpu.CompilerParams(dimension_semantics=sem, vmem_limit_bytes=VMEM_LIMIT)


_ANY = pl.BlockSpec(memory_space=pl.ANY)


def _full(shape):
    zeros = (0,) * len(shape)
    return pl.BlockSpec(shape, lambda *_: zeros)


def _layer(l, shape):
    zeros = (0,) * len(shape)
    return pl.BlockSpec((None,) + shape, lambda *_: (l,) + zeros)


def _rows(width, col=0, reverse=False, tm=TM):
    if reverse:
        return pl.BlockSpec((tm, width), lambda i: (L // tm - 1 - i, col))
    return pl.BlockSpec((tm, width), lambda i: (i, col))


def _rows_mm(width, col=0):
    return _rows(width, col, False, TMM)


def _pcall(body, name, grid, in_specs, out_specs, out_shape, args, scratch=(), aliases=None, jobs=()):
    in_specs, out_specs, out_shape, args, scratch = list(in_specs), list(out_specs), list(out_shape), list(args), list(scratch)
    aliases = dict(aliases or {})
    jobs = [j for j in jobs if j is not None]
    n_in, n_out, n_scr = len(in_specs), len(out_specs), len(scratch)
    srcs = [s for j in jobs for s in j.srcs]
    bufs = [b for j in jobs for b in j.bufs]
    news = [s for j in jobs for s in j.news]
    aliases.update({n_in + len(srcs) + k: n_out + k for k in range(len(bufs))})

    def hosted(*refs):
        cuts = [n_in, len(srcs), len(bufs), n_out, len(bufs), len(news), n_scr]
        parts, p = [], 0
        for n in cuts:
            parts.append(refs[p:p + n])
            p += n
        ins, src_r, _, outs, buf_r, new_r, scr = parts
        sem_r = refs[p:]
        views, ps, pb, pn, pm = [], 0, 0, 0, 0
        for j in jobs:
            views.append((src_r[ps:ps + len(j.srcs)], buf_r[pb:pb + len(j.bufs)], new_r[pn:pn + len(j.news)],
                          sem_r[pm:pm + len(j.scratch)]))
            ps, pb, pn, pm = ps + len(j.srcs), pb + len(j.bufs), pn + len(j.news), pm + len(j.scratch)

        def run(phase):
            for j, v in zip(jobs, views):
                getattr(j, phase)(*v)

        def at_step(step):
            return functools.reduce(jnp.logical_and, [pl.program_id(d) == step(d) for d in range(len(grid))])

        if not grid:
            run("start")
            run("finish")
            return
        pl.when(at_step(lambda d: 0))(lambda: run("start"))
        body(*ins, *outs, *scr)
        pl.when(at_step(lambda d: grid[d] - 1))(lambda: run("finish"))

    outs = pl.pallas_call(
        hosted if jobs else body, name=name, **({"grid": grid} if grid else {}),
        in_specs=in_specs + [_ANY] * (len(srcs) + len(bufs)), out_specs=out_specs + [_ANY] * (len(bufs) + len(news)),
        out_shape=out_shape + [SDS(b.shape, b.dtype) for b in bufs] + news,
        input_output_aliases=aliases, scratch_shapes=scratch + [s for j in jobs for s in j.scratch],
        compiler_params=_params(*(("arbitrary",) * len(grid))))(*args, *srcs, *bufs)
    res, pb, pn = [], n_out, n_out + len(bufs)
    for j in jobs:
        res.append((list(outs[pb:pb + len(j.bufs)]), list(outs[pn:pn + len(j.news)])))
        pb, pn = pb + len(j.bufs), pn + len(j.news)
    return list(outs[:n_out]), res


def _fused(name, grid, parts, shared=None, jobs=()):
    def body(*refs):
        pos = [0]

        def take(n):
            pos[0] += n
            return refs[pos[0] - n:pos[0]]

        ins = [take(len(p["in_specs"])) for p in parts]
        if shared:
            take(1)
            block = take(1)[0]
        outs = [take(len(p["out_specs"])) for p in parts]
        scr = [take(len(p["scratch"])) for p in parts]
        col = 0
        for p, i, o, s in zip(parts, ins, outs, scr):
            view = []
            if shared:
                view = [block.at[:, pl.ds(col, p["width"])]]
                col += p["width"]
            p["body"](*i, *view, *o, *s)

    in_specs = [s for p in parts for s in p["in_specs"]] + ([_ANY] if shared else [])
    out_specs = ([shared[1]] if shared else []) + [s for p in parts for s in p["out_specs"]]
    out_shape = ([SDS(shared[0].shape, shared[0].dtype)] if shared else []) + [s for p in parts for s in p["out_shape"]]
    args = [a for p in parts for a in p["args"]] + ([shared[0]] if shared else [])
    return _pcall(body, name, grid, in_specs, out_specs, out_shape, args, [s for p in parts for s in p["scratch"]],
                  {len(in_specs) - 1: 0} if shared else None, jobs)


def _dot(a, b):
    return jnp.dot(a, b, preferred_element_type=F32)


def _dot_nt(a, b):
    return lax.dot_general(a, b, (((1,), (1,)), ((), ())), preferred_element_type=F32)


def _dot_tn(a, b):
    return lax.dot_general(a, b, (((0,), (0,)), ((), ())), preferred_element_type=F32)


_K0 = 0.7978845608028654
_K1 = 0.044715


def _gelu(x):
    return 0.5 * x * (1.0 + jnp.tanh(_K0 * (x + _K1 * (x * x * x))))


def _gelu_grad(x):
    t = jnp.tanh(_K0 * (x + _K1 * (x * x * x)))
    return 0.5 * (1.0 + t) + 0.5 * x * (1.0 - t * t) * (_K0 * (1.0 + 3.0 * _K1 * x * x))


def _sigmoid(x):
    return jax.nn.sigmoid(x)


def _norm_inproj(l, x, g, w, b, jobs=()):
    def body(x_ref, g_ref, w_ref, b_ref, h_ref, proj_ref):
        xv = x_ref[...]
        r = lax.rsqrt(jnp.mean(xv * xv, axis=-1, keepdims=True) + EPS)
        hb = ((xv * r) * g_ref[...]).astype(BF16)
        h_ref[...] = hb
        for j in range(4):
            cs = slice(j * 1024, (j + 1) * 1024)
            proj_ref[:, cs] = _dot(hb, w_ref[j]) + b_ref[:, cs]

    return _pcall(
        body, f"norm_inproj_l{l}", (L // TMM,),
        [_rows_mm(D), _layer(l, (1, D)), _layer(l, (4, D, 1024)), _layer(l, (1, NIN))],
        [_rows_mm(D), _rows_mm(NIN)],
        [SDS((L, D), BF16), SDS((L, NIN), F32)],
        (x, g, w, b), jobs=jobs)


def _scan_tile(re_ref, im_ref, st_re, st_im, cr_re, cr_im, carry, reverse):
    def chunk(ci, carry):
        c = (TM // 8 - 1 - ci) if reverse else ci
        rows = pl.ds(pl.multiple_of(c * 8, 8), 8)
        new = []
        for lb in range(GP // 512):
            cols = slice(lb * 512, (lb + 1) * 512)
            vr = re_ref[rows, cols]
            vi = im_ref[rows, cols]
            for s, d in enumerate((1, 2, 4)):
                ar = st_re[8 * s:8 * s + 8, cols]
                ai = st_im[8 * s:8 * s + 8, cols]
                sr = pltpu.roll(vr, 8 - d if reverse else d, 0)
                si = pltpu.roll(vi, 8 - d if reverse else d, 0)
                vr, vi = vr + ar * sr - ai * si, vi + ar * si + ai * sr
            cr, ci_ = carry[2 * lb], carry[2 * lb + 1]
            pr = cr_re[:, cols]
            pi = cr_im[:, cols]
            vr, vi = vr + pr * cr - pi * ci_, vi + pr * ci_ + pi * cr
            re_ref[rows, cols] = vr
            im_ref[rows, cols] = vi
            if reverse:
                new += [vr[0:1], vi[0:1]]
            else:
                new += [vr[7:8], vi[7:8]]
        return tuple(new)

    return lax.fori_loop(0, TM // 8, chunk, carry)


def _load_carry(car_ref):
    return tuple(car_ref[r:r + 1, lb * 512:(lb + 1) * 512] for lb in range(GP // 512) for r in (0, 1))


def _store_carry(car_ref, carry):
    for lb in range(GP // 512):
        car_ref[0:1, lb * 512:(lb + 1) * 512] = carry[2 * lb]
        car_ref[1:2, lb * 512:(lb + 1) * 512] = carry[2 * lb + 1]


def _s5_fwd(l, proj, bexp, cre, cimn, powers, dsk, wglu, bglu):
    def body(ua_ref, za_ref, bexp_ref, cre_ref, cimn_ref, st_re_ref, st_im_ref, cr_re_ref, cr_im_ref,
             d_ref, wg_ref, bg_ref, sre_ref, sim_ref, y1_ref, q_ref, ya_ref, car_ref):
        @pl.when(pl.program_id(0) == 0)
        def _():
            car_ref[...] = jnp.zeros_like(car_ref)

        u = ua_ref[...]
        ub = u.astype(BF16)
        for k in range(4):
            bu = _dot(ub[:, 128 * k:128 * (k + 1)], bexp_ref[k])
            sre_ref[:, 512 * k:512 * (k + 1)] = bu[:, :512]
            sim_ref[:, 512 * k:512 * (k + 1)] = bu[:, 512:]
        carry = _scan_tile(sre_ref, sim_ref, st_re_ref, st_im_ref, cr_re_ref, cr_im_ref, _load_carry(car_ref), False)
        _store_carry(car_ref, carry)
        for k in range(4):
            blk = slice(512 * k, 512 * (k + 1))
            ks = slice(128 * k, 128 * (k + 1))
            y0 = _dot(sre_ref[:, blk].astype(BF16), cre_ref[k]) + _dot(sim_ref[:, blk].astype(BF16), cimn_ref[k])
            y1_ref[:, ks] = y0 + d_ref[:, ks] * u[:, ks]
        y2 = _gelu(y1_ref[...])
        q = _dot(y2.astype(BF16), wg_ref[...]) + bg_ref[...]
        q_ref[...] = q
        za = za_ref[...]
        ya_ref[...] = ((y2 * _sigmoid(q)) * (za * _sigmoid(za))).astype(BF16)

    return dict(
        body=body,
        in_specs=[_rows(W, 0), _rows(W, 1), _layer(l, (4, 128, 1024)), _layer(l, (4, 512, 128)),
                  _layer(l, (4, 512, 128)), _layer(l, (24, GP)), _layer(l, (24, GP)), _layer(l, (8, GP)),
                  _layer(l, (8, GP)), _layer(l, (1, W)), _layer(l, (W, W)), _layer(l, (1, W))],
        out_specs=[_rows(GP), _rows(GP), _rows(W), _rows(W), _rows(W)],
        out_shape=[SDS((L, GP), F32), SDS((L, GP), F32), SDS((L, W), F32), SDS((L, W), F32), SDS((L, W), BF16)],
        args=(proj, proj, bexp, cre, cimn, *powers, dsk, wglu, bglu),
        scratch=[pltpu.VMEM((8, GP), F32)])


def _pool_fwd(l, proj, pw, scale):
    def body(ub_ref, zb_ref, pw_ref, sc_ref, pooled_ref, mixed_ref, yb_ref, buf):
        i = pl.program_id(0)

        @pl.when(i == 0)
        def _():
            buf[0:16, :] = jnp.zeros((16, W), F32)

        u = ub_ref[...]
        buf[16:16 + TM, :] = u
        t = i * TM + lax.broadcasted_iota(jnp.int32, (TM, 128), 0)
        for gi, win in enumerate(WINS):
            cs = slice(128 * gi, 128 * (gi + 1))
            acc = u[:, cs]
            for k in range(1, win):
                acc = acc + buf[16 - k:16 - k + TM, cs]
            cnt = jnp.minimum(t + 1, win).astype(F32)
            pb = (acc / cnt - u[:, cs]).astype(BF16)
            pooled_ref[:, cs] = pb
            mixed_ref[:, cs] = _dot(pb, pw_ref[gi])
        zb = zb_ref[...]
        yb_ref[...] = ((mixed_ref[...] * sc_ref[...]) * (zb * _sigmoid(zb))).astype(BF16)
        buf[0:16, :] = buf[TM:TM + 16, :]

    return dict(
        body=body,
        in_specs=[_rows(W, 2), _rows(W, 3), _layer(l, (4, 128, 128)), _layer(l, (1, W))],
        out_specs=[_rows(W), _rows(W), _rows(W)],
        out_shape=[SDS((L, W), BF16), SDS((L, W), F32), SDS((L, W), BF16)],
        args=(proj, proj, pw, scale),
        scratch=[pltpu.VMEM((TM + 16, W), F32)])


def _merge_out(l, ya, yb, proj, x, wa, wb, wo):
    def body(ya_ref, yb_ref, ga_ref, gb_ref, x_ref, wa_ref, wb_ref, wo_ref, pa_ref, pb_ref, mg_ref, xo_ref):
        ya = ya_ref[...]
        yb = yb_ref[...]
        for j in range(4):
            cs = slice(256 * j, 256 * (j + 1))
            pa_ref[:, cs] = _dot(ya, wa_ref[j])
            pb_ref[:, cs] = _dot(yb, wb_ref[j])
        merged = _sigmoid(ga_ref[...]) * pa_ref[...] + _sigmoid(gb_ref[...]) * pb_ref[...]
        mb = merged.astype(BF16)
        mg_ref[...] = mb
        xo_ref[...] = x_ref[...] + _dot(mb, wo_ref[...])

    return pl.pallas_call(
        body, name=f"merge_out_l{l}", grid=(L // TMM,),
        in_specs=[_rows_mm(W), _rows_mm(W), _rows_mm(D, 2), _rows_mm(D, 3), _rows_mm(D),
                  _layer(l, (4, W, 256)), _layer(l, (4, W, 256)), _layer(l, (D, D))],
        out_specs=[_rows_mm(D), _rows_mm(D), _rows_mm(D), _rows_mm(D)],
        out_shape=[SDS((L, D), F32), SDS((L, D), F32), SDS((L, D), BF16), SDS((L, D), F32)],
        compiler_params=_params("arbitrary"),
    )(ya, yb, proj, proj, x, wa, wb, wo)


def _loss_head(x, gf, target):
    def body(x_ref, g_ref, t_ref, loss_ref, dx_ref, dg_ref):
        @pl.when(pl.program_id(0) == 0)
        def _():
            loss_ref[...] = jnp.zeros_like(loss_ref)
            dg_ref[...] = jnp.zeros_like(dg_ref)

        xv = x_ref[...]
        g = g_ref[...]
        r = lax.rsqrt(jnp.mean(xv * xv, axis=-1, keepdims=True) + EPS)
        xn = xv * r
        err = xn * g - t_ref[...]
        part = jnp.sum(jnp.mean(err * err, axis=-1, keepdims=True), axis=0, keepdims=True)
        loss_ref[...] += 0.5 * part
        dy = err * (1.0 / D)
        dg_ref[...] += jnp.sum(dy * xn, axis=0, keepdims=True)
        dxn = dy * g
        dx_ref[...] = r * (dxn - xn * jnp.mean(dxn * xn, axis=-1, keepdims=True))

    return pl.pallas_call(
        body, name="loss_head", grid=(NT,),
        in_specs=[_rows(D), _full((1, D)), _rows(D)],
        out_specs=[_full((2, 128)), _rows(D), _full((1, D))],
        out_shape=[SDS((2, 128), F32), SDS((L, D), F32), SDS((1, D), F32)],
        compiler_params=_params("arbitrary"),
    )(x, gf, target)


def _merge_out_bwd(l, dxn, mg, proj, pa, pb, ya, yb, wo, wa, wb, jobs=()):
    def body(dx_ref, mg_ref, ga_ref, gb_ref, pa_ref, pb_ref, ya_ref, yb_ref, wo_ref, wa_ref, wb_ref,
             dg_ref, dya_ref, dyb_ref, dwo_ref, dwa_ref, dwb_ref, dbias_ref):
        @pl.when(pl.program_id(0) == 0)
        def _():
            for ref in (dwo_ref, dwa_ref, dwb_ref, dbias_ref):
                ref[...] = jnp.zeros_like(ref)

        dxb = dx_ref[...].astype(BF16)
        dm = _dot_nt(dxb, wo_ref[...])
        sa = _sigmoid(ga_ref[...])
        sb = _sigmoid(gb_ref[...])
        dga = dm * pa_ref[...] * (sa * (1.0 - sa))
        dgb = dm * pb_ref[...] * (sb * (1.0 - sb))
        dg_ref[:, :D] = dga.astype(BF16)
        dg_ref[:, D:] = dgb.astype(BF16)
        dbias_ref[:, :D] += jnp.sum(dga, axis=0, keepdims=True)
        dbias_ref[:, D:] += jnp.sum(dgb, axis=0, keepdims=True)
        dpa = (dm * sa).astype(BF16)
        dpb = (dm * sb).astype(BF16)
        ya = ya_ref[...]
        yb = yb_ref[...]
        dya = jnp.zeros((TM, W), F32)
        dyb = jnp.zeros((TM, W), F32)
        for j in range(4):
            cs = slice(256 * j, 256 * (j + 1))
            dya = dya + _dot_nt(dpa[:, cs], wa_ref[j])
            dyb = dyb + _dot_nt(dpb[:, cs], wb_ref[j])
            dwa_ref[j] += _dot_tn(ya, dpa[:, cs])
            dwb_ref[j] += _dot_tn(yb, dpb[:, cs])
        dya_ref[...] = dya
        dyb_ref[...] = dyb
        dwo_ref[...] += _dot_tn(mg_ref[...], dxb)

    return _pcall(
        body, f"merge_out_bwd_l{l}", (NT,),
        [_rows(D), _rows(D), _rows(D, 2), _rows(D, 3), _rows(D), _rows(D), _rows(W), _rows(W),
         _layer(l, (D, D)), _layer(l, (4, W, 256)), _layer(l, (4, W, 256))],
        [_rows(2 * D, 1), _rows(W), _rows(W), _full((D, D)), _full((4, W, 256)), _full((4, W, 256)), _full((1, 2 * D))],
        [SDS((L, NIN), BF16), SDS((L, W), F32), SDS((L, W), F32),
         SDS((D, D), F32), SDS((4, W, 256), F32), SDS((4, W, 256), F32), SDS((1, 2 * D), F32)],
        (dxn, mg, proj, proj, pa, pb, ya, yb, wo, wa, wb), jobs=jobs)


def _pool_bwd(l, dyb, proj, mixed, pooled, pw, scale):
    def body(dyb_ref, zb_ref, mixed_ref, pooled_ref, pw_ref, sc_ref, db_ref, dpw_ref, dsc_ref, dbias_ref, buf):
        i = pl.program_id(0)
        tile = NT - 1 - i

        @pl.when(i == 0)
        def _():
            dpw_ref[...] = jnp.zeros_like(dpw_ref)
            dsc_ref[...] = jnp.zeros_like(dsc_ref)
            dbias_ref[...] = jnp.zeros_like(dbias_ref)
            buf[TM:TM + 16, :] = jnp.zeros((16, W), F32)

        dyb = dyb_ref[...]
        zb = zb_ref[...]
        mixed = mixed_ref[...]
        sc = sc_ref[...]
        sg = _sigmoid(zb)
        dyb0 = dyb * (zb * sg)
        dzb = dyb * (mixed * sc) * (sg * (1.0 + zb * (1.0 - sg)))
        db_ref[:, W:] = dzb.astype(BF16)
        dbias_ref[:, W:] += jnp.sum(dzb, axis=0, keepdims=True)
        dsc_ref[...] += jnp.sum(dyb0 * mixed, axis=0, keepdims=True)
        dmix = (dyb0 * sc).astype(BF16)
        t = tile * TM + lax.broadcasted_iota(jnp.int32, (TM, 128), 0)
        for gi, win in enumerate(WINS):
            cs = slice(128 * gi, 128 * (gi + 1))
            dpw_ref[gi] += _dot_tn(pooled_ref[:, cs], dmix[:, cs])
            dpool = _dot_nt(dmix[:, cs], pw_ref[gi])
            cnt = jnp.minimum(t + 1, win).astype(F32)
            e = dpool / cnt
            buf[0:TM, cs] = e
            acc = e - dpool
            for k in range(1, win):
                acc = acc + buf[k:k + TM, cs]
            db_ref[:, cs] = acc.astype(BF16)
            dbias_ref[:, cs] += jnp.sum(acc, axis=0, keepdims=True)
        buf[TM:TM + 16, :] = buf[0:16, :]

    return dict(
        body=body, width=2 * W,
        in_specs=[_rows(W, 0, True), _rows(W, 3, True), _rows(W, 0, True), _rows(W, 0, True),
                  _layer(l, (4, 128, 128)), _layer(l, (1, W))],
        out_specs=[_full((4, 128, 128)), _full((1, W)), _full((1, 2 * W))],
        out_shape=[SDS((4, 128, 128), F32), SDS((1, W), F32), SDS((1, 2 * W), F32)],
        args=(dyb, proj, mixed, pooled, pw, scale),
        scratch=[pltpu.VMEM((TM + 16, W), F32)])


def _s5_bwd(l, dya, proj, y1, q, sre, sim, cret, cimnt, bret, bimt, st_re, st_im, cr_re, cr_im, dsk, wglu):
    def halo(i):
        return (jnp.maximum((NT - 1 - i) * (TM // 8) - 1, 0), 0)

    def body(dya_ref, ua_ref, za_ref, y1_ref, q_ref, sre_ref, sim_ref, hre_ref, him_ref,
             cret_ref, cimnt_ref, bret_ref, bimt_ref, st_re_ref, st_im_ref, cr_re_ref, cr_im_ref, d_ref, wg_ref,
             da_ref, dwg_ref, dbg_ref, dd_ref, dcre_ref, dcimn_ref, dbre_ref, dbim_ref, dare_ref, daim_ref, dbias_ref,
             lre, lim, car_ref):
        i = pl.program_id(0)
        tile = NT - 1 - i

        @pl.when(i == 0)
        def _():
            for ref in (dwg_ref, dbg_ref, dd_ref, dcre_ref, dcimn_ref, dbre_ref, dbim_ref, dare_ref, daim_ref, dbias_ref,
                        car_ref):
                ref[...] = jnp.zeros_like(ref)

        u = ua_ref[...]
        za = za_ref[...]
        y1 = y1_ref[...]
        dya = dya_ref[...]
        y2 = _gelu(y1)
        sg = _sigmoid(q_ref[...])
        sgz = _sigmoid(za)
        dy3 = dya * (za * sgz)
        dza = dya * (y2 * sg) * (sgz * (1.0 + za * (1.0 - sgz)))
        da_ref[:, W:] = dza.astype(BF16)
        dbias_ref[:, W:] += jnp.sum(dza, axis=0, keepdims=True)
        dq = dy3 * y2 * (sg * (1.0 - sg))
        dqb = dq.astype(BF16)
        dy2 = dy3 * sg + _dot_nt(dqb, wg_ref[...])
        dwg_ref[...] += _dot_tn(y2.astype(BF16), dqb)
        dbg_ref[...] += jnp.sum(dq, axis=0, keepdims=True)
        dy1 = dy2 * _gelu_grad(y1)
        dd_ref[...] += jnp.sum(dy1 * u, axis=0, keepdims=True)
        dy1b = dy1.astype(BF16)
        ub = u.astype(BF16)
        for k in range(4):
            blk = slice(512 * k, 512 * (k + 1))
            ks = slice(128 * k, 128 * (k + 1))
            lre[:, blk] = _dot(dy1b[:, ks], cret_ref[k])
            lim[:, blk] = _dot(dy1b[:, ks], cimnt_ref[k])
            dcre_ref[k] += _dot_tn(sre_ref[:, blk].astype(BF16), dy1b[:, ks])
            dcimn_ref[k] += _dot_tn(sim_ref[:, blk].astype(BF16), dy1b[:, ks])
        carry = _scan_tile(lre, lim, st_re_ref, st_im_ref, cr_re_ref, cr_im_ref, _load_carry(car_ref), True)
        _store_carry(car_ref, carry)

        rowid = lax.broadcasted_iota(jnp.int32, (8, 512), 0)
        gate = (tile > 0).astype(F32)

        def chunk(c, _):
            rows = pl.ds(pl.multiple_of(c * 8, 8), 8)
            prows = pl.ds(pl.multiple_of(jnp.maximum(c - 1, 0) * 8, 8), 8)
            for lb in range(GP // 512):
                cols = slice(lb * 512, (lb + 1) * 512)
                sr = sre_ref[rows, cols]
                si = sim_ref[rows, cols]
                pr = jnp.where(c == 0, hre_ref[7:8, cols] * gate, sre_ref[prows, cols][7:8])
                pi = jnp.where(c == 0, him_ref[7:8, cols] * gate, sim_ref[prows, cols][7:8])
                sr = jnp.where(rowid == 0, pr, pltpu.roll(sr, 1, 0))
                si = jnp.where(rowid == 0, pi, pltpu.roll(si, 1, 0))
                lr = lre[rows, cols]
                li = lim[rows, cols]
                dare_ref[:, cols] += sr * lr + si * li
                daim_ref[:, cols] += sr * li - si * lr
            return 0

        lax.fori_loop(0, TM // 8, chunk, 0)

        for k in range(4):
            blk = slice(512 * k, 512 * (k + 1))
            ks = slice(128 * k, 128 * (k + 1))
            lrb = lre[:, blk].astype(BF16)
            lib = lim[:, blk].astype(BF16)
            du = dy1[:, ks] * d_ref[:, ks] + _dot(lrb, bret_ref[k]) + _dot(lib, bimt_ref[k])
            da_ref[:, ks] = du.astype(BF16)
            dbias_ref[:, ks] += jnp.sum(du, axis=0, keepdims=True)
            dbre_ref[k] += _dot_tn(ub[:, ks], lrb)
            dbim_ref[k] += _dot_tn(ub[:, ks], lib)

    return dict(
        body=body, width=2 * W,
        in_specs=[_rows(W, 0, True), _rows(W, 0, True), _rows(W, 1, True), _rows(W, 0, True), _rows(W, 0, True),
                  _rows(GP, 0, True), _rows(GP, 0, True),
                  pl.BlockSpec((8, GP), halo), pl.BlockSpec((8, GP), halo),
                  _layer(l, (4, 128, 512)), _layer(l, (4, 128, 512)), _layer(l, (4, 512, 128)), _layer(l, (4, 512, 128)),
                  _layer(l, (24, GP)), _layer(l, (24, GP)), _layer(l, (8, GP)), _layer(l, (8, GP)), _layer(l, (1, W)),
                  _layer(l, (W, W))],
        out_specs=[_full((W, W)), _full((1, W)), _full((1, W)),
                   _full((4, 512, 128)), _full((4, 512, 128)), _full((4, 128, 512)), _full((4, 128, 512)),
                   _full((8, GP)), _full((8, GP)), _full((1, 2 * W))],
        out_shape=[SDS((W, W), F32), SDS((1, W), F32), SDS((1, W), F32),
                   SDS((4, 512, 128), F32), SDS((4, 512, 128), F32), SDS((4, 128, 512), F32), SDS((4, 128, 512), F32),
                   SDS((8, GP), F32), SDS((8, GP), F32), SDS((1, 2 * W), F32)],
        args=(dya, proj, proj, y1, q, sre, sim, sre, sim, cret, cimnt, bret, bimt, st_re, st_im, cr_re, cr_im, dsk,
              wglu),
        scratch=[pltpu.VMEM((TM, GP), F32), pltpu.VMEM((TM, GP), F32), pltpu.VMEM((8, GP), F32)])


def _inproj_dw(l, r, h, dproj, jobs=()):
    def body(h_ref, dp_ref, dw_ref):
        part = _dot_tn(h_ref[...], dp_ref[...])

        @pl.when(pl.program_id(1) == 0)
        def _():
            dw_ref[...] = part

        @pl.when(pl.program_id(1) > 0)
        def _():
            dw_ref[...] += part

    return _pcall(
        body, f"inproj_dw{r}_l{l}", (4, L // TK),
        [pl.BlockSpec((TK, D // 2), lambda j, i: (i, r)), pl.BlockSpec((TK, 1024), lambda j, i: (i, j))],
        [pl.BlockSpec((None, D // 2, 1024), lambda j, i: (j, 0, 0))],
        [SDS((4, D // 2, 1024), F32)],
        (h, dproj), jobs=jobs)


def _inproj_dx(l, dproj, w, x, g, dxn, jobs=()):
    def body(dp_ref, w_ref, x_ref, g_ref, dxn_ref, dx_ref, dg_ref):
        @pl.when(pl.program_id(0) == 0)
        def _():
            dg_ref[...] = jnp.zeros_like(dg_ref)

        dh = _dot_nt(dp_ref[:, 0:1024], w_ref[0])
        for j in range(1, 4):
            dh = dh + _dot_nt(dp_ref[:, j * 1024:(j + 1) * 1024], w_ref[j])
        xv = x_ref[...]
        r = lax.rsqrt(jnp.mean(xv * xv, axis=-1, keepdims=True) + EPS)
        xn = xv * r
        dg_ref[...] += jnp.sum(dh * xn, axis=0, keepdims=True)
        dn = dh * g_ref[...]
        dx_ref[...] = dxn_ref[...] + r * (dn - xn * jnp.mean(dn * xn, axis=-1, keepdims=True))

    return _pcall(
        body, f"inproj_dx_l{l}", (L // TMM,),
        [_rows_mm(NIN), _layer(l, (4, D, 1024)), _rows_mm(D), _layer(l, (1, D)), _rows_mm(D)],
        [_rows_mm(D), _full((1, D))],
        [SDS((L, D), F32), SDS((1, D), F32)],
        (dproj, w, x, g, dxn), jobs=jobs)


def _discretize(log_dt, lam_re, lam_im, b_re, b_im):
    dt = jnp.exp(log_dt)[..., None]
    mag = jnp.exp(lam_re * dt)
    ang = lam_im * dt
    abar_re = mag * jnp.cos(ang)
    abar_im = mag * jnp.sin(ang)
    num_re = abar_re - 1.0
    num_im = abar_im
    den = lam_re * lam_re + lam_im * lam_im
    coef_re = (num_re * lam_re + num_im * lam_im) / den
    coef_im = (num_im * lam_re - num_re * lam_im) / den
    bbar_re = coef_re[..., None] * b_re - coef_im[..., None] * b_im
    bbar_im = coef_re[..., None] * b_im + coef_im[..., None] * b_re
    return abar_re, abar_im, bbar_re, bbar_im


def _powers(abar_re, abar_im):
    ar, ai = abar_re.reshape(DEPTH, 1, GP), abar_im.reshape(DEPTH, 1, GP)
    rows_re, rows_im = [ar], [ai]
    for _ in range(7):
        pr, pi = rows_re[-1], rows_im[-1]
        rows_re.append(pr * ar - pi * ai)
        rows_im.append(pr * ai + pi * ar)
    row = jnp.arange(8)[:, None]

    def steps(rows, keep):
        return jnp.concatenate([jnp.where(keep(d), rows[d - 1], 0.0) for d in (1, 2, 4)], axis=1)

    neg_im = [-r for r in rows_im]
    fwd = (steps(rows_re, lambda d: row >= d), steps(rows_im, lambda d: row >= d),
           jnp.concatenate(rows_re, axis=1), jnp.concatenate(rows_im, axis=1))
    rev = (steps(rows_re, lambda d: row < 8 - d), steps(neg_im, lambda d: row < 8 - d),
           jnp.concatenate(rows_re[::-1], axis=1), jnp.concatenate(neg_im[::-1], axis=1))
    return fwd, rev


_EYE8 = functools.partial(jnp.eye, 8, dtype=F32)


def _expand_in(b):
    return jnp.einsum("lkgpc,gh->lkgchp", b.reshape(DEPTH, 4, 8, P, C), _EYE8()).reshape(DEPTH, 4, 128, 512)


def _extract_in(e):
    return jnp.einsum("lkgchp,gh->lkgpc", e.reshape(DEPTH, 4, 8, C, 8, P), _EYE8()).reshape(DEPTH, G, P, C)


def _expand_out(c):
    return jnp.einsum("lkgcp,gh->lkgphc", c.reshape(DEPTH, 4, 8, C, P), _EYE8()).reshape(DEPTH, 4, 512, 128)


def _extract_out(e):
    return jnp.einsum("lkgphc,gh->lkgcp", e.reshape(DEPTH, 4, 8, P, 8, C), _EYE8()).reshape(DEPTH, G, C, P)


SMALL = ("norm_g", "b_in", "ssm_log_dt", "ssm_lam_re", "ssm_lam_im", "ssm_b_re", "ssm_b_im",
         "ssm_c_re", "ssm_c_im", "ssm_d", "ssm_b_glu", "pool_w", "pool_scale")
BIG = ("w_in", "ssm_w_glu", "w_branch_a", "w_branch_b", "w_out")


def _step(x, target, w, m, v, place):
    sp = {n: w[n] for n in SMALL}
    final_norm_g = w["final_norm_g"]
    wbuf = dict(zip(BIG, _cast_own(place, [w[n] for n in BIG])))
    (abar_re, abar_im, bbar_re, bbar_im), disc_vjp = jax.vjp(
        _discretize, *(sp[n] for n in ("ssm_log_dt", "ssm_lam_re", "ssm_lam_im", "ssm_b_re", "ssm_b_im")))
    powers_fwd, powers_rev = _powers(abar_re, abar_im)
    b_re_x, b_im_x = _expand_in(bbar_re), _expand_in(bbar_im)
    c_re_x, c_imn_x = _expand_out(sp["ssm_c_re"]), _expand_out(-sp["ssm_c_im"])
    b_x = jnp.concatenate([b_re_x, b_im_x], axis=3).astype(BF16)
    t = lambda a: jnp.swapaxes(a, 2, 3).astype(BF16)
    c_re_t, c_imn_t, b_re_t, b_im_t = t(c_re_x), t(c_imn_x), t(b_re_x), t(b_im_x)
    c_re_x, c_imn_x = c_re_x.astype(BF16), c_imn_x.astype(BF16)
    row = lambda n: sp[n].reshape(DEPTH, 1, -1)
    g, b_in, dsk, b_glu, scale = row("norm_g"), row("b_in"), row("ssm_d"), row("ssm_b_glu"), row("pool_scale")
    pw = sp["pool_w"].astype(BF16)

    saved = []
    for l in range(DEPTH):
        three = BIG[2:]
        if l == 0:
            wbuf["w_in"], wbuf["ssm_w_glu"] = _comm_only(
                "gather_first", _GatherJob([wbuf["w_in"], wbuf["ssm_w_glu"]], 0))[0]
            jobs = [_GatherJob([wbuf[n] for n in three], 0), _GatherJob([wbuf["ssm_w_glu"]], 1)]
        else:
            jobs = []
        (h, proj), res = _norm_inproj(l, x, g, wbuf["w_in"], b_in, jobs)
        if res:
            wbuf.update(zip(three, res[0][0]))
            (wbuf["ssm_w_glu"],) = res[1][0]
        wg = dict(wbuf, ssm_w_glu=wbuf["ssm_w_glu"].reshape(DEPTH, W, W), w_out=wbuf["w_out"].reshape(DEPTH, D, D))
        job = _GatherJob([wbuf["w_in"]], l + 1) if l + 1 < DEPTH else _GatherJob([wbuf[n] for n in three], l)
        (sre, sim, y1, q, ya, pooled, mixed, yb), res = _fused(
            f"branches_fwd_l{l}", (NT,),
            [_s5_fwd(l, proj, b_x, c_re_x, c_imn_x, powers_fwd, dsk, wg["ssm_w_glu"], b_glu),
             _pool_fwd(l, proj, pw, scale)], jobs=[job])
        if l + 1 < DEPTH:
            (wbuf["w_in"],) = res[0][0]
        else:
            wbuf.update(zip(three, res[0][0]))
        wg = dict(wbuf, ssm_w_glu=wbuf["ssm_w_glu"].reshape(DEPTH, W, W), w_out=wbuf["w_out"].reshape(DEPTH, D, D))
        pa, pb, mg, x_next = _merge_out(l, ya, yb, proj, x, wg["w_branch_a"], wg["w_branch_b"], wg["w_out"])
        saved.append(dict(x=x, h=h, proj=proj, sre=sre, sim=sim, y1=y1, q=q, ya=ya,
                          pooled=pooled, mixed=mixed, yb=yb, pa=pa, pb=pb, mg=mg))
        x = x_next

    loss, dx, dgf = _loss_head(x, final_norm_g.reshape(1, D), target)

    per_layer = {n: [None] * DEPTH for n in ("norm_g", "b_in", "ssm_d", "ssm_b_glu", "pool_w", "pool_scale",
                                             "dare", "daim", "dbre", "dbim", "dcre", "dcimn")}
    red = _Reducer(place, w, m, v)
    for l in reversed(range(DEPTH)):
        s = saved[l]
        (dproj, dya, dyb, dwo, dwa, dwb, dbias_g), res = _merge_out_bwd(
            l, dx, s["mg"], s["proj"], s["pa"], s["pb"], s["ya"], s["yb"],
            wg["w_out"], wg["w_branch_a"], wg["w_branch_b"], red.jobs())
        red.land(res)
        (dproj, dwg, dbg, dd, dcre, dcimn, dbre, dbim, dare, daim, dbias_a, dpw, dsc, dbias_b), res = _fused(
            f"branches_bwd_l{l}", (NT,),
            [_s5_bwd(l, dya, s["proj"], s["y1"], s["q"], s["sre"], s["sim"], c_re_t, c_imn_t, b_re_t, b_im_t,
                     *powers_rev, dsk, wg["ssm_w_glu"]),
             _pool_bwd(l, dyb, s["proj"], s["mixed"], s["pooled"], pw, scale)],
            shared=(dproj, _rows(4 * W, 0, True)), jobs=red.jobs())
        red.land(res)
        rest = [dwg.reshape(4, W // 4, W), dwa, dwb, dwo.reshape(4, D // 4, D)]
        if l == 0:
            red.add(l, "rest", BIG[1:], rest)
        halves = []
        for r in range(2):
            outs, res = _inproj_dw(l, r, s["h"], dproj, red.jobs())
            red.land(res)
            halves += outs
            if l == 0:
                red.add(l, f"in{r}", BIG[:1], outs, [r * (D // 2)])
        (dx, dg), res = _inproj_dx(l, dproj, wg["w_in"], s["x"], g, dx, red.jobs())
        red.land(res)
        if l > 0:
            red.add(l, "all", BIG[:1] * 2 + BIG[1:], halves + rest, [0, D // 2, 0, 0, 0, 0])
        for n, a in (("norm_g", dg.reshape(D)), ("b_in", jnp.concatenate([dbias_a, dbias_b, dbias_g], axis=1).reshape(NIN)),
                     ("ssm_d", dd.reshape(W)), ("ssm_b_glu", dbg.reshape(W)), ("pool_w", dpw), ("pool_scale", dsc.reshape(W)),
                     ("dare", dare), ("daim", daim), ("dbre", dbre), ("dbim", dbim), ("dcre", dcre), ("dcimn", dcimn)):
            per_layer[n][l] = a
    gs = {n: jnp.stack(a) for n, a in per_layer.items()}
    d_abar = [jnp.sum(gs.pop(n), axis=1).reshape(DEPTH, G, P) for n in ("dare", "daim")]
    (gs["ssm_log_dt"], gs["ssm_lam_re"], gs["ssm_lam_im"], gs["ssm_b_re"], gs["ssm_b_im"]) = disc_vjp(
        (*d_abar, _extract_in(gs.pop("dbre")), _extract_in(gs.pop("dbim"))))
    gs["ssm_c_re"], gs["ssm_c_im"] = _extract_out(gs.pop("dcre")), -_extract_out(gs.pop("dcimn"))
    gs["final_norm_g"] = dgf

    natural = {n: w[n].shape for n in REPLICATED}
    rw, rm, rv = {}, {}, {}
    for n in REPLICATED:
        shape = DENSE.get(n, natural[n])
        gs[n], rw[n], rm[n], rv[n] = (a.reshape(shape) for a in (gs[n], w[n], m[n], v[n]))
    small = [gs[n] for n in REPLICATED] + [loss]
    jobs = red.jobs()
    res = _pcall(None, "tail_exchange", (), [], [], [], [], jobs=jobs + [_SiblingJob(small, False)])[1]
    red.land(res[:len(jobs)])
    pair_small = _small_pair_sum(place, small, res[-1][1], [BF16 if a.ndim > 2 else F32 for a in small])
    jobs = red.jobs()
    res = _pcall(None, "tail_gather", (), [], [], [], [], jobs=jobs + [_ChipGatherJob(pair_small)])[1]
    red.land(res[:len(jobs)])
    assert not red.active
    small_parts = dict(zip(REPLICATED + ("loss",), res[-1][0]))

    k = len(REPLICATED)
    outs = _adamw_small("adamw_small", [rw[n] for n in REPLICATED], [small_parts[n] for n in REPLICATED],
                        [rm[n] for n in REPLICATED], [rv[n] for n in REPLICATED], small_parts["loss"])
    results = {n: red.big[n] for n in BIG}
    results.update({n: [outs[1 + q * k + i].reshape(natural[n]) for q in range(4)] for i, n in enumerate(REPLICATED)})
    return outs[0][0, 0], dx, results


def _place():
    x, y, c = lax.axis_index("x"), lax.axis_index("y"), lax.axis_index("c")
    chips = [(1 - x, y), (x, 1 - y), (1 - x, 1 - y)]
    return x, y, c, 2 * x + y, chips, [2 * cx + cy for cx, cy in chips]


def _remote(src, dst, ssem, rsem, dev):
    return pltpu.make_async_remote_copy(src_ref=src, dst_ref=dst, send_sem=ssem, recv_sem=rsem,
                                        device_id=dev, device_id_type=MESH)


class _GatherJob:
    def __init__(self, bufs, l):
        self.srcs, self.bufs, self.news, self.l = [], list(bufs), [], l
        self.scratch = [pltpu.SemaphoreType.DMA((len(self.bufs), 3))] * 4

    def _half(self, ref, k, h):
        rows = ref.shape[2] // 2
        return ref.at[self.l, k, pl.ds(pl.multiple_of(h * rows, 8), rows), :]

    def _ici(self, bufs, sems, a, j, k):
        _, _, c, _, chips, _ = _place()
        blk = self._half(bufs[a], k, c)
        return _remote(blk, blk, sems[0].at[a, j], sems[1].at[a, j], (*chips[j], c))

    def _d2d(self, bufs, sems, a, j, k, h):
        x, y, c, _, _, _ = _place()
        blk = self._half(bufs[a], k, h)
        return _remote(blk, blk, sems[2].at[a, j], sems[3].at[a, j], (x, y, 1 - c))

    def start(self, srcs, bufs, news, sems):
        me = _place()[3]
        for a in range(len(self.bufs)):
            for j in range(3):
                self._ici(bufs, sems, a, j, me).start()

    def finish(self, srcs, bufs, news, sems):
        _, _, c, me, _, cid = _place()
        pairs = [(a, j) for a in range(len(self.bufs)) for j in range(3)]
        for a, j in pairs:
            self._ici(bufs, sems, a, j, cid[j]).wait_recv()
            self._d2d(bufs, sems, a, j, cid[j], c).start()
        for a, j in pairs:
            self._d2d(bufs, sems, a, j, cid[j], 1 - c).wait_recv()
        for a, j in pairs:
            self._ici(bufs, sems, a, j, me).wait_send()
            self._d2d(bufs, sems, a, j, cid[j], c).wait_send()


class _SiblingJob:
    def __init__(self, srcs, rows_half):
        self.srcs, self.bufs, self.rows_half = list(srcs), [], rows_half
        self.news = [SDS((s.shape[0], s.shape[1] // 2, s.shape[2]) if rows_half else s.shape, s.dtype) for s in srcs]
        self.scratch = [pltpu.SemaphoreType.DMA((len(self.srcs),))] * 2

    def _copy(self, srcs, news, sems, a):
        x, y, c, _, _, _ = _place()
        src = srcs[a]
        if self.rows_half:
            rows = src.shape[1] // 2
            src = src.at[:, pl.ds(pl.multiple_of((1 - c) * rows, 8), rows), :]
        return _remote(src, news[a], sems[0].at[a], sems[1].at[a], (x, y, 1 - c))

    def start(self, srcs, bufs, news, sems):
        for a in range(len(self.srcs)):
            self._copy(srcs, news, sems, a).start()

    def finish(self, srcs, bufs, news, sems):
        for a in range(len(self.srcs)):
            self._copy(srcs, news, sems, a).wait()


class _ScatterJob:
    def __init__(self, parts):
        self.srcs, self.bufs = list(parts), []
        self.news = [SDS((3,) + p.shape[1:], p.dtype) for p in parts]
        self.scratch = [pltpu.SemaphoreType.DMA((len(self.srcs), 3))] * 2

    def _copy(self, srcs, news, sems, a, j):
        _, _, c, _, chips, cid = _place()
        return _remote(srcs[a].at[cid[j]], news[a].at[j], sems[0].at[a, j], sems[1].at[a, j], (*chips[j], c))

    def start(self, srcs, bufs, news, sems):
        for a in range(len(self.srcs)):
            for j in range(3):
                self._copy(srcs, news, sems, a, j).start()

    def finish(self, srcs, bufs, news, sems):
        for a in range(len(self.srcs)):
            for j in range(3):
                self._copy(srcs, news, sems, a, j).wait()


def _comm_only(name, job):
    return _pcall(None, name, (), [], [], [], [], jobs=[job])[1][0]


class _ChipGatherJob(_GatherJob):
    def __init__(self, bufs):
        super().__init__(bufs, None)

    def _half(self, ref, k, h):
        return ref.at[k, h]


def _cast_own(place, ws):
    n = len(ws)

    def body(p_ref, *refs):
        for i_ref, o_ref in zip(refs[:n], refs[n:]):
            o_ref[...] = i_ref[...].astype(BF16)

    return pl.pallas_call(
        body, name="cast_own_shards",
        grid_spec=pltpu.PrefetchScalarGridSpec(
            num_scalar_prefetch=1, grid=(DEPTH,),
            in_specs=[pl.BlockSpec((None,) + a.shape[1:], lambda l, p: (l, 0, 0)) for a in ws],
            out_specs=[pl.BlockSpec((None, None) + a.shape[1:], lambda l, p: (l, p[1], 0, 0)) for a in ws]),
        out_shape=[SDS((DEPTH, 4) + a.shape[1:], BF16) for a in ws],
        compiler_params=_params("arbitrary"),
    )(place, *ws)


def _half_tiles(a_):
    rows = a_ // 2
    ta = min(rows, 256)
    return rows, ta, rows // ta


def _pair_sums_bf16(name, place, owns, recvs):
    n = len(owns)

    def body(p_ref, *refs):
        for own_ref, recv_ref, out_ref in zip(refs[:n], refs[n:2 * n], refs[2 * n:]):
            out_ref[...] = (own_ref[...] + recv_ref[...]).astype(BF16)

    def own_half(a):
        return pl.BlockSpec((None, a.shape[1] // 2, a.shape[2]), lambda s, p: (s, p[0], 0))

    def block(a):
        return pl.BlockSpec((None,) + a.shape[1:], lambda s, p: (s, 0, 0))

    return pl.pallas_call(
        body, name=name,
        grid_spec=pltpu.PrefetchScalarGridSpec(
            num_scalar_prefetch=1, grid=(4,),
            in_specs=[own_half(a) for a in owns] + [block(r) for r in recvs],
            out_specs=[block(r) for r in recvs]),
        out_shape=[SDS(r.shape, BF16) for r in recvs],
        compiler_params=_params("arbitrary"),
    )(place, *owns, *recvs)


def _shard_sums(name, place, owns, recvs, rbufs):
    n = len(owns)

    def body(p_ref, *refs):
        for own_ref, recv_ref, r_ref, out_ref in zip(refs[:n], refs[n:2 * n], refs[2 * n:3 * n], refs[3 * n:]):
            acc = own_ref[...] + recv_ref[...]
            for j in range(3):
                acc = acc + r_ref[j].astype(F32)
            out_ref[...] = acc

    def own_half(a):
        return pl.BlockSpec((None, a.shape[1] // 2, a.shape[2]), lambda i, p: (p[1], p[0], 0))

    def recv_block(a):
        return pl.BlockSpec((None,) + a.shape[1:], lambda i, p: (p[1], 0, 0))

    return pl.pallas_call(
        body, name=name,
        grid_spec=pltpu.PrefetchScalarGridSpec(
            num_scalar_prefetch=1, grid=(1,),
            in_specs=([own_half(a) for a in owns] + [recv_block(r) for r in recvs]
                      + [pl.BlockSpec(rb.shape, lambda i, p: (0, 0, 0)) for rb in rbufs]),
            out_specs=[pl.BlockSpec(r.shape[1:], lambda i, p: (0, 0)) for r in recvs]),
        out_shape=[SDS(r.shape[1:], F32) for r in recvs],
        compiler_params=_params("arbitrary"),
    )(place, *owns, *recvs, *rbufs)


def _small_pair_sum(place, mine, recv, dtypes):
    n = len(mine)

    def body(p_ref, *refs):
        for m_ref, r_ref, o_ref in zip(refs[:n], refs[n:2 * n], refs[2 * n:]):
            o_ref[...] = (m_ref[...] + r_ref[...]).astype(o_ref.dtype)

    def whole(a):
        zeros = (0,) * a.ndim
        return pl.BlockSpec(a.shape, lambda i, p: zeros)

    def mine_blk(a):
        zeros = (0,) * a.ndim
        return pl.BlockSpec((None,) + a.shape, lambda i, p: (p[1],) + zeros)

    return pl.pallas_call(
        body, name="small_pair_sum",
        grid_spec=pltpu.PrefetchScalarGridSpec(
            num_scalar_prefetch=1, grid=(1,),
            in_specs=[whole(a) for a in mine] + [whole(a) for a in recv],
            out_specs=[mine_blk(a) for a in mine]),
        out_shape=[SDS((4,) + a.shape, dt) for a, dt in zip(mine, dtypes)],
        compiler_params=_params("arbitrary"),
    )(place, *mine, *recv)


def _adam_math(w, g, m, v):
    m = B1 * m + (1.0 - B1) * g
    v = B2 * v + (1.0 - B2) * (g * g)
    m_hat = m / (1.0 - B1 ** STEP)
    v_hat = v / (1.0 - B2 ** STEP)
    delta = -LR * (m_hat / (jnp.sqrt(v_hat) + EPS_A) + WD * w)
    return delta, m, v


def _adamw_big(name, l, row0, w, m, v, mine, other, prev, jobs=()):
    _, _, b_ = w.shape
    _, ta, nh = _half_tiles(2 * mine.shape[0])
    prev = list(prev or [])

    def body(w_ref, m_ref, v_ref, mine_ref, other_ref, *rest):
        g_ref, d_ref, mo_ref, vo_ref = rest[len(prev):]
        g = jnp.where(pl.program_id(0) == lax.axis_index("c"), mine_ref[...], other_ref[...])
        g_ref[...] = g
        d_ref[...], mo_ref[...], vo_ref[...] = _adam_math(w_ref[...], g, m_ref[...], v_ref[...])

    slab = pl.BlockSpec((None, ta, b_), lambda h, i: (l, row0 // ta + h * nh + i, 0))
    half = pl.BlockSpec((ta, b_), lambda h, i: (i, 0))
    outs, res = _pcall(
        body, name, (2, nh), [slab, slab, slab, half, half] + [_ANY] * len(prev), [slab] * 4, [SDS(w.shape, F32)] * 4,
        (w, m, v, mine, other, *prev), aliases={5 + k: k for k in range(len(prev))}, jobs=jobs)
    return outs, res


def _adamw_small(name, ws, parts, ms, vs, loss_parts=None):
    k = len(ws)
    extra = [] if loss_parts is None else [loss_parts]

    def chip_sum(p_ref):
        p = [p_ref[k].astype(F32) for k in range(4)]
        return ((p[0] + p[1]) + p[2]) + p[3]

    def body(*refs):
        w_refs, p_refs, m_refs, v_refs = refs[:k], refs[k:2 * k], refs[2 * k:3 * k], refs[3 * k:4 * k]
        outs = refs[4 * k + len(extra):]
        if extra:
            outs[0][...] = chip_sum(refs[4 * k])
            outs = outs[1:]
        for a in range(k):
            g = chip_sum(p_refs[a])
            outs[a][...] = g
            outs[k + a][...], outs[2 * k + a][...], outs[3 * k + a][...] = _adam_math(
                w_refs[a][...], g, m_refs[a][...], v_refs[a][...])

    like = [SDS(a.shape, F32) for a in ws]
    return pl.pallas_call(
        body, name=name,
        out_shape=([SDS(loss_parts.shape[1:], F32)] if extra else []) + like * 4,
        compiler_params=pltpu.CompilerParams(vmem_limit_bytes=VMEM_LIMIT),
    )(*ws, *parts, *ms, *vs, *extra)


class _Reducer:
    def __init__(self, place, w, m, v):
        self.place, self.w, self.m, self.v = place, w, m, v
        self.active, self.riding = [], []
        self.big = {n: None for n in BIG}

    def add(self, l, tag, names, own, row0s=None):
        self.active.append(dict(l=l, key=f"{tag}_l{l}", names=names, own=list(own), row0s=row0s or [0] * len(names),
                                stage=0))

    def jobs(self):
        self.riding = list(self.active)
        return [(_SiblingJob(g["own"], True), _ScatterJob(g.get("parts", [])), _SiblingJob(g.get("shard", []), False))
                [g["stage"]] for g in self.riding]

    def land(self, res):
        for g, (_, news) in zip(self.riding, res):
            if g["stage"] == 0:
                g["recv"] = news
                g["parts"] = _pair_sums_bf16(f"pair_sums_{g['key']}", self.place, g["own"], news)
            elif g["stage"] == 1:
                g["shard"] = _shard_sums(f"shard_sums_{g['key']}", self.place, g["own"], g["recv"], news)
            else:
                for n, mine, other, row0 in zip(g["names"], g["shard"], news, g["row0s"]):
                    self.big[n] = _adamw_big(f"adamw_{n}_{row0}_{g['key']}", g["l"], row0, self.w[n], self.m[n], self.v[n],
                                             mine, other, self.big[n])[0]
                self.active.remove(g)
            g["stage"] += 1
        self.riding = []


WEIGHTS = ("norm_g", "w_in", "b_in", "ssm_log_dt", "ssm_lam_re", "ssm_lam_im", "ssm_b_re", "ssm_b_im", "ssm_c_re",
           "ssm_c_im", "ssm_d", "ssm_w_glu", "ssm_b_glu", "pool_w", "pool_scale", "w_branch_a", "w_branch_b", "w_out",
           "final_norm_g")
REPLICATED = SMALL + ("final_norm_g",)
DENSE = {"ssm_b_re": (DEPTH, G, P * C), "ssm_b_im": (DEPTH, G, P * C), "final_norm_g": (2, D // 2)}


def kernel(x, norm_g, w_in, b_in, ssm_log_dt, ssm_lam_re, ssm_lam_im, ssm_b_re, ssm_b_im, ssm_c_re, ssm_c_im, ssm_d, ssm_w_glu, ssm_b_glu, pool_w, pool_scale, w_branch_a, w_branch_b, w_out, final_norm_g, loss_target, m_norm_g, m_w_in, m_b_in, m_ssm_log_dt, m_ssm_lam_re, m_ssm_lam_im, m_ssm_b_re, m_ssm_b_im, m_ssm_c_re, m_ssm_c_im, m_ssm_d, m_ssm_w_glu, m_ssm_b_glu, m_pool_w, m_pool_scale, m_w_branch_a, m_w_branch_b, m_w_out, m_final_norm_g, v_norm_g, v_w_in, v_b_in, v_ssm_log_dt, v_ssm_lam_re, v_ssm_lam_im, v_ssm_b_re, v_ssm_b_im, v_ssm_c_re, v_ssm_c_im, v_ssm_d, v_ssm_w_glu, v_ssm_b_glu, v_pool_w, v_pool_scale, v_w_branch_a, v_w_branch_b, v_w_out, v_final_norm_g):
    w = dict(zip(WEIGHTS, (norm_g, w_in, b_in, ssm_log_dt, ssm_lam_re, ssm_lam_im, ssm_b_re, ssm_b_im, ssm_c_re,
                           ssm_c_im, ssm_d, ssm_w_glu, ssm_b_glu, pool_w, pool_scale, w_branch_a, w_branch_b, w_out,
                           final_norm_g)))
    m = dict(zip(WEIGHTS, (m_norm_g, m_w_in, m_b_in, m_ssm_log_dt, m_ssm_lam_re, m_ssm_lam_im, m_ssm_b_re, m_ssm_b_im,
                           m_ssm_c_re, m_ssm_c_im, m_ssm_d, m_ssm_w_glu, m_ssm_b_glu, m_pool_w, m_pool_scale,
                           m_w_branch_a, m_w_branch_b, m_w_out, m_final_norm_g)))
    v = dict(zip(WEIGHTS, (v_norm_g, v_w_in, v_b_in, v_ssm_log_dt, v_ssm_lam_re, v_ssm_lam_im, v_ssm_b_re, v_ssm_b_im,
                           v_ssm_c_re, v_ssm_c_im, v_ssm_d, v_ssm_w_glu, v_ssm_b_glu, v_pool_w, v_pool_scale,
                           v_w_branch_a, v_w_branch_b, v_w_out, v_final_norm_g)))
    place = jnp.stack([lax.axis_index("c"), 2 * lax.axis_index("x") + lax.axis_index("y")]).astype(jnp.int32)

    total_loss, dx, results = _step(x[0], loss_target[0], w, m, v, place)
    return (total_loss, dx[None], *[results[n][q] for q in range(4) for n in WEIGHTS])
```

```python
import functools

import jax
import jax.numpy as jnp
from jax import lax
from jax.experimental import pallas as pl
from jax.experimental.pallas import tpu as pltpu

F32, BF16 = jnp.float32, jnp.bfloat16
SDS = jax.ShapeDtypeStruct
MESH = pl.DeviceIdType.MESH

DEPTH = 2
L = 2048
D = 1024
NIN = 4096
W = 512
G, P, C = 32, 64, 16
GP = G * P
WINS = (2, 4, 8, 16)
TM = 256
NT = L // TM
TMM = 512
TK = 1024
EPS = 1e-6
VMEM_LIMIT = 56 * 2**20

LR, B1, B2, EPS_A, WD, STEP = 0.001, 0.9, 0.999, 1e-08, 0.01, 10


def _params(*sem):
    return pltpu.CompilerParams(dimension_semantics=sem, vmem_limit_bytes=VMEM_LIMIT)


_ANY = pl.BlockSpec(memory_space=pl.ANY)


def _full(shape):
    zeros = (0,) * len(shape)
    return pl.BlockSpec(shape, lambda *_: zeros)


def _layer(l, shape):
    zeros = (0,) * len(shape)
    return pl.BlockSpec((None,) + shape, lambda *_: (l,) + zeros)


def _rows(width, col=0, reverse=False, tm=TM):
    if reverse:
        return pl.BlockSpec((tm, width), lambda i: (L // tm - 1 - i, col))
    return pl.BlockSpec((tm, width), lambda i: (i, col))


def _rows_mm(width, col=0):
    return _rows(width, col, False, TMM)


def _pcall(body, name, grid, in_specs, out_specs, out_shape, args, scratch=(), aliases=None, jobs=()):
    in_specs, out_specs, out_shape, args, scratch = list(in_specs), list(out_specs), list(out_shape), list(args), list(scratch)
    aliases = dict(aliases or {})
    jobs = [j for j in jobs if j is not None]
    n_in, n_out, n_scr = len(in_specs), len(out_specs), len(scratch)
    srcs = [s for j in jobs for s in j.srcs]
    bufs = [b for j in jobs for b in j.bufs]
    news = [s for j in jobs for s in j.news]
    aliases.update({n_in + len(srcs) + k: n_out + k for k in range(len(bufs))})

    def hosted(*refs):
        cuts = [n_in, len(srcs), len(bufs), n_out, len(bufs), len(news), n_scr]
        parts, p = [], 0
        for n in cuts:
            parts.append(refs[p:p + n])
            p += n
        ins, src_r, _, outs, buf_r, new_r, scr = parts
        sem_r = refs[p:]
        views, ps, pb, pn, pm = [], 0, 0, 0, 0
        for j in jobs:
            views.append((src_r[ps:ps + len(j.srcs)], buf_r[pb:pb + len(j.bufs)], new_r[pn:pn + len(j.news)],
                          sem_r[pm:pm + len(j.scratch)]))
            ps, pb, pn, pm = ps + len(j.srcs), pb + len(j.bufs), pn + len(j.news), pm + len(j.scratch)

        def run(phase):
            for j, v in zip(jobs, views):
                getattr(j, phase)(*v)

        def at_step(step):
            return functools.reduce(jnp.logical_and, [pl.program_id(d) == step(d) for d in range(len(grid))])

        if not grid:
            run("start")
            run("finish")
            return
        pl.when(at_step(lambda d: 0))(lambda: run("start"))
        body(*ins, *outs, *scr)
        pl.when(at_step(lambda d: grid[d] - 1))(lambda: run("finish"))

    outs = pl.pallas_call(
        hosted if jobs else body, name=name, **({"grid": grid} if grid else {}),
        in_specs=in_specs + [_ANY] * (len(srcs) + len(bufs)), out_specs=out_specs + [_ANY] * (len(bufs) + len(news)),
        out_shape=out_shape + [SDS(b.shape, b.dtype) for b in bufs] + news,
        input_output_aliases=aliases, scratch_shapes=scratch + [s for j in jobs for s in j.scratch],
        compiler_params=_params(*(("arbitrary",) * len(grid))))(*args, *srcs, *bufs)
    res, pb, pn = [], n_out, n_out + len(bufs)
    for j in jobs:
        res.append((list(outs[pb:pb + len(j.bufs)]), list(outs[pn:pn + len(j.news)])))
        pb, pn = pb + len(j.bufs), pn + len(j.news)
    return list(outs[:n_out]), res


def _fused(name, grid, parts, shared=None, jobs=()):
    def body(*refs):
        pos = [0]

        def take(n):
            pos[0] += n
            return refs[pos[0] - n:pos[0]]

        ins = [take(len(p["in_specs"])) for p in parts]
        if shared:
            take(1)
            block = take(1)[0]
        outs = [take(len(p["out_specs"])) for p in parts]
        scr = [take(len(p["scratch"])) for p in parts]
        col = 0
        for p, i, o, s in zip(parts, ins, outs, scr):
            view = []
            if shared:
                view = [block.at[:, pl.ds(col, p["width"])]]
                col += p["width"]
            p["body"](*i, *view, *o, *s)

    in_specs = [s for p in parts for s in p["in_specs"]] + ([_ANY] if shared else [])
    out_specs = ([shared[1]] if shared else []) + [s for p in parts for s in p["out_specs"]]
    out_shape = ([SDS(shared[0].shape, shared[0].dtype)] if shared else []) + [s for p in parts for s in p["out_shape"]]
    args = [a for p in parts for a in p["args"]] + ([shared[0]] if shared else [])
    return _pcall(body, name, grid, in_specs, out_specs, out_shape, args, [s for p in parts for s in p["scratch"]],
                  {len(in_specs) - 1: 0} if shared else None, jobs)


def _dot(a, b):
    return jnp.dot(a, b, preferred_element_type=F32)


def _dot_nt(a, b):
    return lax.dot_general(a, b, (((1,), (1,)), ((), ())), preferred_element_type=F32)


def _dot_tn(a, b):
    return lax.dot_general(a, b, (((0,), (0,)), ((), ())), preferred_element_type=F32)


_K0 = 0.7978845608028654
_K1 = 0.044715


def _gelu(x):
    return 0.5 * x * (1.0 + jnp.tanh(_K0 * (x + _K1 * (x * x * x))))


def _gelu_grad(x):
    t = jnp.tanh(_K0 * (x + _K1 * (x * x * x)))
    return 0.5 * (1.0 + t) + 0.5 * x * (1.0 - t * t) * (_K0 * (1.0 + 3.0 * _K1 * x * x))


def _sigmoid(x):
    return jax.nn.sigmoid(x)


def _norm_inproj(l, x, g, w, b, jobs=()):
    def body(x_ref, g_ref, w_ref, b_ref, h_ref, proj_ref):
        xv = x_ref[...]
        r = lax.rsqrt(jnp.mean(xv * xv, axis=-1, keepdims=True) + EPS)
        hb = ((xv * r) * g_ref[...]).astype(BF16)
        h_ref[...] = hb
        for j in range(4):
            cs = slice(j * 1024, (j + 1) * 1024)
            proj_ref[:, cs] = _dot(hb, w_ref[j]) + b_ref[:, cs]

    return _pcall(
        body, f"norm_inproj_l{l}", (L // TMM,),
        [_rows_mm(D), _layer(l, (1, D)), _layer(l, (4, D, 1024)), _layer(l, (1, NIN))],
        [_rows_mm(D), _rows_mm(NIN)],
        [SDS((L, D), BF16), SDS((L, NIN), F32)],
        (x, g, w, b), jobs=jobs)


def _scan_tile(re_ref, im_ref, st_re, st_im, cr_re, cr_im, carry, reverse):
    def chunk(ci, carry):
        c = (TM // 8 - 1 - ci) if reverse else ci
        rows = pl.ds(pl.multiple_of(c * 8, 8), 8)
        new = []
        for lb in range(GP // 512):
            cols = slice(lb * 512, (lb + 1) * 512)
            vr = re_ref[rows, cols]
            vi = im_ref[rows, cols]
            for s, d in enumerate((1, 2, 4)):
                ar = st_re[8 * s:8 * s + 8, cols]
                ai = st_im[8 * s:8 * s + 8, cols]
                sr = pltpu.roll(vr, 8 - d if reverse else d, 0)
                si = pltpu.roll(vi, 8 - d if reverse else d, 0)
                vr, vi = vr + ar * sr - ai * si, vi + ar * si + ai * sr
            cr, ci_ = carry[2 * lb], carry[2 * lb + 1]
            pr = cr_re[:, cols]
            pi = cr_im[:, cols]
            vr, vi = vr + pr * cr - pi * ci_, vi + pr * ci_ + pi * cr
            re_ref[rows, cols] = vr
            im_ref[rows, cols] = vi
            if reverse:
                new += [vr[0:1], vi[0:1]]
            else:
                new += [vr[7:8], vi[7:8]]
        return tuple(new)

    return lax.fori_loop(0, TM // 8, chunk, carry)


def _load_carry(car_ref):
    return tuple(car_ref[r:r + 1, lb * 512:(lb + 1) * 512] for lb in range(GP // 512) for r in (0, 1))


def _store_carry(car_ref, carry):
    for lb in range(GP // 512):
        car_ref[0:1, lb * 512:(lb + 1) * 512] = carry[2 * lb]
        car_ref[1:2, lb * 512:(lb + 1) * 512] = carry[2 * lb + 1]


def _s5_fwd(l, proj, bexp, cre, cimn, powers, dsk, wglu, bglu):
    def body(ua_ref, za_ref, bexp_ref, cre_ref, cimn_ref, st_re_ref, st_im_ref, cr_re_ref, cr_im_ref,
             d_ref, wg_ref, bg_ref, sre_ref, sim_ref, y1_ref, q_ref, ya_ref, car_ref):
        @pl.when(pl.program_id(0) == 0)
        def _():
            car_ref[...] = jnp.zeros_like(car_ref)

        u = ua_ref[...]
        ub = u.astype(BF16)
        for k in range(4):
            bu = _dot(ub[:, 128 * k:128 * (k + 1)], bexp_ref[k])
            sre_ref[:, 512 * k:512 * (k + 1)] = bu[:, :512]
            sim_ref[:, 512 * k:512 * (k + 1)] = bu[:, 512:]
        carry = _scan_tile(sre_ref, sim_ref, st_re_ref, st_im_ref, cr_re_ref, cr_im_ref, _load_carry(car_ref), False)
        _store_carry(car_ref, carry)
        for k in range(4):
            blk = slice(512 * k, 512 * (k + 1))
            ks = slice(128 * k, 128 * (k + 1))
            y0 = _dot(sre_ref[:, blk].astype(BF16), cre_ref[k]) + _dot(sim_ref[:, blk].astype(BF16), cimn_ref[k])
            y1_ref[:, ks] = y0 + d_ref[:, ks] * u[:, ks]
        y2 = _gelu(y1_ref[...])
        q = _dot(y2.astype(BF16), wg_ref[...]) + bg_ref[...]
        q_ref[...] = q
        za = za_ref[...]
        ya_ref[...] = ((y2 * _sigmoid(q)) * (za * _sigmoid(za))).astype(BF16)

    return dict(
        body=body,
        in_specs=[_rows(W, 0), _rows(W, 1), _layer(l, (4, 128, 1024)), _layer(l, (4, 512, 128)),
                  _layer(l, (4, 512, 128)), _layer(l, (24, GP)), _layer(l, (24, GP)), _layer(l, (8, GP)),
                  _layer(l, (8, GP)), _layer(l, (1, W)), _layer(l, (W, W)), _layer(l, (1, W))],
        out_specs=[_rows(GP), _rows(GP), _rows(W), _rows(W), _rows(W)],
        out_shape=[SDS((L, GP), F32), SDS((L, GP), F32), SDS((L, W), F32), SDS((L, W), F32), SDS((L, W), BF16)],
        args=(proj, proj, bexp, cre, cimn, *powers, dsk, wglu, bglu),
        scratch=[pltpu.VMEM((8, GP), F32)])


def _pool_fwd(l, proj, pw, scale):
    def body(ub_ref, zb_ref, pw_ref, sc_ref, pooled_ref, mixed_ref, yb_ref, buf):
        i = pl.program_id(0)

        @pl.when(i == 0)
        def _():
            buf[0:16, :] = jnp.zeros((16, W), F32)

        u = ub_ref[...]
        buf[16:16 + TM, :] = u
        t = i * TM + lax.broadcasted_iota(jnp.int32, (TM, 128), 0)
        for gi, win in enumerate(WINS):
            cs = slice(128 * gi, 128 * (gi + 1))
            acc = u[:, cs]
            for k in range(1, win):
                acc = acc + buf[16 - k:16 - k + TM, cs]
            cnt = jnp.minimum(t + 1, win).astype(F32)
            pb = (acc / cnt - u[:, cs]).astype(BF16)
            pooled_ref[:, cs] = pb
            mixed_ref[:, cs] = _dot(pb, pw_ref[gi])
        zb = zb_ref[...]
        yb_ref[...] = ((mixed_ref[...] * sc_ref[...]) * (zb * _sigmoid(zb))).astype(BF16)
        buf[0:16, :] = buf[TM:TM + 16, :]

    return dict(
        body=body,
        in_specs=[_rows(W, 2), _rows(W, 3), _layer(l, (4, 128, 128)), _layer(l, (1, W))],
        out_specs=[_rows(W), _rows(W), _rows(W)],
        out_shape=[SDS((L, W), BF16), SDS((L, W), F32), SDS((L, W), BF16)],
        args=(proj, proj, pw, scale),
        scratch=[pltpu.VMEM((TM + 16, W), F32)])


def _merge_out(l, ya, yb, proj, x, wa, wb, wo):
    def body(ya_ref, yb_ref, ga_ref, gb_ref, x_ref, wa_ref, wb_ref, wo_ref, pa_ref, pb_ref, mg_ref, xo_ref):
        ya = ya_ref[...]
        yb = yb_ref[...]
        for j in range(4):
            cs = slice(256 * j, 256 * (j + 1))
            pa_ref[:, cs] = _dot(ya, wa_ref[j])
            pb_ref[:, cs] = _dot(yb, wb_ref[j])
        merged = _sigmoid(ga_ref[...]) * pa_ref[...] + _sigmoid(gb_ref[...]) * pb_ref[...]
        mb = merged.astype(BF16)
        mg_ref[...] = mb
        xo_ref[...] = x_ref[...] + _dot(mb, wo_ref[...])

    return pl.pallas_call(
        body, name=f"merge_out_l{l}", grid=(L // TMM,),
        in_specs=[_rows_mm(W), _rows_mm(W), _rows_mm(D, 2), _rows_mm(D, 3), _rows_mm(D),
                  _layer(l, (4, W, 256)), _layer(l, (4, W, 256)), _layer(l, (D, D))],
        out_specs=[_rows_mm(D), _rows_mm(D), _rows_mm(D), _rows_mm(D)],
        out_shape=[SDS((L, D), F32), SDS((L, D), F32), SDS((L, D), BF16), SDS((L, D), F32)],
        compiler_params=_params("arbitrary"),
    )(ya, yb, proj, proj, x, wa, wb, wo)


def _loss_head(x, gf, target):
    def body(x_ref, g_ref, t_ref, loss_ref, dx_ref, dg_ref):
        @pl.when(pl.program_id(0) == 0)
        def _():
            loss_ref[...] = jnp.zeros_like(loss_ref)
            dg_ref[...] = jnp.zeros_like(dg_ref)

        xv = x_ref[...]
        g = g_ref[...]
        r = lax.rsqrt(jnp.mean(xv * xv, axis=-1, keepdims=True) + EPS)
        xn = xv * r
        err = xn * g - t_ref[...]
        part = jnp.sum(jnp.mean(err * err, axis=-1, keepdims=True), axis=0, keepdims=True)
        loss_ref[...] += 0.5 * part
        dy = err * (1.0 / D)
        dg_ref[...] += jnp.sum(dy * xn, axis=0, keepdims=True)
        dxn = dy * g
        dx_ref[...] = r * (dxn - xn * jnp.mean(dxn * xn, axis=-1, keepdims=True))

    return pl.pallas_call(
        body, name="loss_head", grid=(NT,),
        in_specs=[_rows(D), _full((1, D)), _rows(D)],
        out_specs=[_full((2, 128)), _rows(D), _full((1, D))],
        out_shape=[SDS((2, 128), F32), SDS((L, D), F32), SDS((1, D), F32)],
        compiler_params=_params("arbitrary"),
    )(x, gf, target)


def _merge_out_bwd(l, dxn, mg, proj, pa, pb, ya, yb, wo, wa, wb, jobs=()):
    def body(dx_ref, mg_ref, ga_ref, gb_ref, pa_ref, pb_ref, ya_ref, yb_ref, wo_ref, wa_ref, wb_ref,
             dg_ref, dya_ref, dyb_ref, dwo_ref, dwa_ref, dwb_ref, dbias_ref):
        @pl.when(pl.program_id(0) == 0)
        def _():
            for ref in (dwo_ref, dwa_ref, dwb_ref, dbias_ref):
                ref[...] = jnp.zeros_like(ref)

        dxb = dx_ref[...].astype(BF16)
        dm = _dot_nt(dxb, wo_ref[...])
        sa = _sigmoid(ga_ref[...])
        sb = _sigmoid(gb_ref[...])
        dga = dm * pa_ref[...] * (sa * (1.0 - sa))
        dgb = dm * pb_ref[...] * (sb * (1.0 - sb))
        dg_ref[:, :D] = dga.astype(BF16)
        dg_ref[:, D:] = dgb.astype(BF16)
        dbias_ref[:, :D] += jnp.sum(dga, axis=0, keepdims=True)
        dbias_ref[:, D:] += jnp.sum(dgb, axis=0, keepdims=True)
        dpa = (dm * sa).astype(BF16)
        dpb = (dm * sb).astype(BF16)
        ya = ya_ref[...]
        yb = yb_ref[...]
        dya = jnp.zeros((TM, W), F32)
        dyb = jnp.zeros((TM, W), F32)
        for j in range(4):
            cs = slice(256 * j, 256 * (j + 1))
            dya = dya + _dot_nt(dpa[:, cs], wa_ref[j])
            dyb = dyb + _dot_nt(dpb[:, cs], wb_ref[j])
            dwa_ref[j] += _dot_tn(ya, dpa[:, cs])
            dwb_ref[j] += _dot_tn(yb, dpb[:, cs])
        dya_ref[...] = dya
        dyb_ref[...] = dyb
        dwo_ref[...] += _dot_tn(mg_ref[...], dxb)

    return _pcall(
        body, f"merge_out_bwd_l{l}", (NT,),
        [_rows(D), _rows(D), _rows(D, 2), _rows(D, 3), _rows(D), _rows(D), _rows(W), _rows(W),
         _layer(l, (D, D)), _layer(l, (4, W, 256)), _layer(l, (4, W, 256))],
        [_rows(2 * D, 1), _rows(W), _rows(W), _full((D, D)), _full((4, W, 256)), _full((4, W, 256)), _full((1, 2 * D))],
        [SDS((L, NIN), BF16), SDS((L, W), F32), SDS((L, W), F32),
         SDS((D, D), F32), SDS((4, W, 256), F32), SDS((4, W, 256), F32), SDS((1, 2 * D), F32)],
        (dxn, mg, proj, proj, pa, pb, ya, yb, wo, wa, wb), jobs=jobs)


def _pool_bwd(l, dyb, proj, mixed, pooled, pw, scale):
    def body(dyb_ref, zb_ref, mixed_ref, pooled_ref, pw_ref, sc_ref, db_ref, dpw_ref, dsc_ref, dbias_ref, buf):
        i = pl.program_id(0)
        tile = NT - 1 - i

        @pl.when(i == 0)
        def _():
            dpw_ref[...] = jnp.zeros_like(dpw_ref)
            dsc_ref[...] = jnp.zeros_like(dsc_ref)
            dbias_ref[...] = jnp.zeros_like(dbias_ref)
            buf[TM:TM + 16, :] = jnp.zeros((16, W), F32)

        dyb = dyb_ref[...]
        zb = zb_ref[...]
        mixed = mixed_ref[...]
        sc = sc_ref[...]
        sg = _sigmoid(zb)
        dyb0 = dyb * (zb * sg)
        dzb = dyb * (mixed * sc) * (sg * (1.0 + zb * (1.0 - sg)))
        db_ref[:, W:] = dzb.astype(BF16)
        dbias_ref[:, W:] += jnp.sum(dzb, axis=0, keepdims=True)
        dsc_ref[...] += jnp.sum(dyb0 * mixed, axis=0, keepdims=True)
        dmix = (dyb0 * sc).astype(BF16)
        t = tile * TM + lax.broadcasted_iota(jnp.int32, (TM, 128), 0)
        for gi, win in enumerate(WINS):
            cs = slice(128 * gi, 128 * (gi + 1))
            dpw_ref[gi] += _dot_tn(pooled_ref[:, cs], dmix[:, cs])
            dpool = _dot_nt(dmix[:, cs], pw_ref[gi])
            cnt = jnp.minimum(t + 1, win).astype(F32)
            e = dpool / cnt
            buf[0:TM, cs] = e
            acc = e - dpool
            for k in range(1, win):
                acc = acc + buf[k:k + TM, cs]
            db_ref[:, cs] = acc.astype(BF16)
            dbias_ref[:, cs] += jnp.sum(acc, axis=0, keepdims=True)
        buf[TM:TM + 16, :] = buf[0:16, :]

    return dict(
        body=body, width=2 * W,
        in_specs=[_rows(W, 0, True), _rows(W, 3, True), _rows(W, 0, True), _rows(W, 0, True),
                  _layer(l, (4, 128, 128)), _layer(l, (1, W))],
        out_specs=[_full((4, 128, 128)), _full((1, W)), _full((1, 2 * W))],
        out_shape=[SDS((4, 128, 128), F32), SDS((1, W), F32), SDS((1, 2 * W), F32)],
        args=(dyb, proj, mixed, pooled, pw, scale),
        scratch=[pltpu.VMEM((TM + 16, W), F32)])


def _s5_bwd(l, dya, proj, y1, q, sre, sim, cret, cimnt, bret, bimt, st_re, st_im, cr_re, cr_im, dsk, wglu):
    def halo(i):
        return (jnp.maximum((NT - 1 - i) * (TM // 8) - 1, 0), 0)

    def body(dya_ref, ua_ref, za_ref, y1_ref, q_ref, sre_ref, sim_ref, hre_ref, him_ref,
             cret_ref, cimnt_ref, bret_ref, bimt_ref, st_re_ref, st_im_ref, cr_re_ref, cr_im_ref, d_ref, wg_ref,
             da_ref, dwg_ref, dbg_ref, dd_ref, dcre_ref, dcimn_ref, dbre_ref, dbim_ref, dare_ref, daim_ref, dbias_ref,
             lre, lim, car_ref):
        i = pl.program_id(0)
        tile = NT - 1 - i

        @pl.when(i == 0)
        def _():
            for ref in (dwg_ref, dbg_ref, dd_ref, dcre_ref, dcimn_ref, dbre_ref, dbim_ref, dare_ref, daim_ref, dbias_ref,
                        car_ref):
                ref[...] = jnp.zeros_like(ref)

        u = ua_ref[...]
        za = za_ref[...]
        y1 = y1_ref[...]
        dya = dya_ref[...]
        y2 = _gelu(y1)
        sg = _sigmoid(q_ref[...])
        sgz = _sigmoid(za)
        dy3 = dya * (za * sgz)
        dza = dya * (y2 * sg) * (sgz * (1.0 + za * (1.0 - sgz)))
        da_ref[:, W:] = dza.astype(BF16)
        dbias_ref[:, W:] += jnp.sum(dza, axis=0, keepdims=True)
        dq = dy3 * y2 * (sg * (1.0 - sg))
        dqb = dq.astype(BF16)
        dy2 = dy3 * sg + _dot_nt(dqb, wg_ref[...])
        dwg_ref[...] += _dot_tn(y2.astype(BF16), dqb)
        dbg_ref[...] += jnp.sum(dq, axis=0, keepdims=True)
        dy1 = dy2 * _gelu_grad(y1)
        dd_ref[...] += jnp.sum(dy1 * u, axis=0, keepdims=True)
        dy1b = dy1.astype(BF16)
        ub = u.astype(BF16)
        for k in range(4):
            blk = slice(512 * k, 512 * (k + 1))
            ks = slice(128 * k, 128 * (k + 1))
            lre[:, blk] = _dot(dy1b[:, ks], cret_ref[k])
            lim[:, blk] = _dot(dy1b[:, ks], cimnt_ref[k])
            dcre_ref[k] += _dot_tn(sre_ref[:, blk].astype(BF16), dy1b[:, ks])
            dcimn_ref[k] += _dot_tn(sim_ref[:, blk].astype(BF16), dy1b[:, ks])
        carry = _scan_tile(lre, lim, st_re_ref, st_im_ref, cr_re_ref, cr_im_ref, _load_carry(car_ref), True)
        _store_carry(car_ref, carry)

        rowid = lax.broadcasted_iota(jnp.int32, (8, 512), 0)
        gate = (tile > 0).astype(F32)

        def chunk(c, _):
            rows = pl.ds(pl.multiple_of(c * 8, 8), 8)
            prows = pl.ds(pl.multiple_of(jnp.maximum(c - 1, 0) * 8, 8), 8)
            for lb in range(GP // 512):
                cols = slice(lb * 512, (lb + 1) * 512)
                sr = sre_ref[rows, cols]
                si = sim_ref[rows, cols]
                pr = jnp.where(c == 0, hre_ref[7:8, cols] * gate, sre_ref[prows, cols][7:8])
                pi = jnp.where(c == 0, him_ref[7:8, cols] * gate, sim_ref[prows, cols][7:8])
                sr = jnp.where(rowid == 0, pr, pltpu.roll(sr, 1, 0))
                si = jnp.where(rowid == 0, pi, pltpu.roll(si, 1, 0))
                lr = lre[rows, cols]
                li = lim[rows, cols]
                dare_ref[:, cols] += sr * lr + si * li
                daim_ref[:, cols] += sr * li - si * lr
            return 0

        lax.fori_loop(0, TM // 8, chunk, 0)

        for k in range(4):
            blk = slice(512 * k, 512 * (k + 1))
            ks = slice(128 * k, 128 * (k + 1))
            lrb = lre[:, blk].astype(BF16)
            lib = lim[:, blk].astype(BF16)
            du = dy1[:, ks] * d_ref[:, ks] + _dot(lrb, bret_ref[k]) + _dot(lib, bimt_ref[k])
            da_ref[:, ks] = du.astype(BF16)
            dbias_ref[:, ks] += jnp.sum(du, axis=0, keepdims=True)
            dbre_ref[k] += _dot_tn(ub[:, ks], lrb)
            dbim_ref[k] += _dot_tn(ub[:, ks], lib)

    return dict(
        body=body, width=2 * W,
        in_specs=[_rows(W, 0, True), _rows(W, 0, True), _rows(W, 1, True), _rows(W, 0, True), _rows(W, 0, True),
                  _rows(GP, 0, True), _rows(GP, 0, True),
                  pl.BlockSpec((8, GP), halo), pl.BlockSpec((8, GP), halo),
                  _layer(l, (4, 128, 512)), _layer(l, (4, 128, 512)), _layer(l, (4, 512, 128)), _layer(l, (4, 512, 128)),
                  _layer(l, (24, GP)), _layer(l, (24, GP)), _layer(l, (8, GP)), _layer(l, (8, GP)), _layer(l, (1, W)),
                  _layer(l, (W, W))],
        out_specs=[_full((W, W)), _full((1, W)), _full((1, W)),
                   _full((4, 512, 128)), _full((4, 512, 128)), _full((4, 128, 512)), _full((4, 128, 512)),
                   _full((8, GP)), _full((8, GP)), _full((1, 2 * W))],
        out_shape=[SDS((W, W), F32), SDS((1, W), F32), SDS((1, W), F32),
                   SDS((4, 512, 128), F32), SDS((4, 512, 128), F32), SDS((4, 128, 512), F32), SDS((4, 128, 512), F32),
                   SDS((8, GP), F32), SDS((8, GP), F32), SDS((1, 2 * W), F32)],
        args=(dya, proj, proj, y1, q, sre, sim, sre, sim, cret, cimnt, bret, bimt, st_re, st_im, cr_re, cr_im, dsk,
              wglu),
        scratch=[pltpu.VMEM((TM, GP), F32), pltpu.VMEM((TM, GP), F32), pltpu.VMEM((8, GP), F32)])


def _inproj_dw(l, r, h, dproj, jobs=()):
    def body(h_ref, dp_ref, dw_ref):
        part = _dot_tn(h_ref[...], dp_ref[...])

        @pl.when(pl.program_id(1) == 0)
        def _():
            dw_ref[...] = part

        @pl.when(pl.program_id(1) > 0)
        def _():
            dw_ref[...] += part

    return _pcall(
        body, f"inproj_dw{r}_l{l}", (4, L // TK),
        [pl.BlockSpec((TK, D // 2), lambda j, i: (i, r)), pl.BlockSpec((TK, 1024), lambda j, i: (i, j))],
        [pl.BlockSpec((None, D // 2, 1024), lambda j, i: (j, 0, 0))],
        [SDS((4, D // 2, 1024), F32)],
        (h, dproj), jobs=jobs)


def _inproj_dx(l, dproj, w, x, g, dxn, jobs=()):
    def body(dp_ref, w_ref, x_ref, g_ref, dxn_ref, dx_ref, dg_ref):
        @pl.when(pl.program_id(0) == 0)
        def _():
            dg_ref[...] = jnp.zeros_like(dg_ref)

        dh = _dot_nt(dp_ref[:, 0:1024], w_ref[0])
        for j in range(1, 4):
            dh = dh + _dot_nt(dp_ref[:, j * 1024:(j + 1) * 1024], w_ref[j])
        xv = x_ref[...]
        r = lax.rsqrt(jnp.mean(xv * xv, axis=-1, keepdims=True) + EPS)
        xn = xv * r
        dg_ref[...] += jnp.sum(dh * xn, axis=0, keepdims=True)
        dn = dh * g_ref[...]
        dx_ref[...] = dxn_ref[...] + r * (dn - xn * jnp.mean(dn * xn, axis=-1, keepdims=True))

    return _pcall(
        body, f"inproj_dx_l{l}", (L // TMM,),
        [_rows_mm(NIN), _layer(l, (4, D, 1024)), _rows_mm(D), _layer(l, (1, D)), _rows_mm(D)],
        [_rows_mm(D), _full((1, D))],
        [SDS((L, D), F32), SDS((1, D), F32)],
        (dproj, w, x, g, dxn), jobs=jobs)


def _discretize(log_dt, lam_re, lam_im, b_re, b_im):
    dt = jnp.exp(log_dt)[..., None]
    mag = jnp.exp(lam_re * dt)
    ang = lam_im * dt
    abar_re = mag * jnp.cos(ang)
    abar_im = mag * jnp.sin(ang)
    num_re = abar_re - 1.0
    num_im = abar_im
    den = lam_re * lam_re + lam_im * lam_im
    coef_re = (num_re * lam_re + num_im * lam_im) / den
    coef_im = (num_im * lam_re - num_re * lam_im) / den
    bbar_re = coef_re[..., None] * b_re - coef_im[..., None] * b_im
    bbar_im = coef_re[..., None] * b_im + coef_im[..., None] * b_re
    return abar_re, abar_im, bbar_re, bbar_im


def _powers(abar_re, abar_im):
    ar, ai = abar_re.reshape(DEPTH, 1, GP), abar_im.reshape(DEPTH, 1, GP)
    rows_re, rows_im = [ar], [ai]
    for _ in range(7):
        pr, pi = rows_re[-1], rows_im[-1]
        rows_re.append(pr * ar - pi * ai)
        rows_im.append(pr * ai + pi * ar)
    row = jnp.arange(8)[:, None]

    def steps(rows, keep):
        return jnp.concatenate([jnp.where(keep(d), rows[d - 1], 0.0) for d in (1, 2, 4)], axis=1)

    neg_im = [-r for r in rows_im]
    fwd = (steps(rows_re, lambda d: row >= d), steps(rows_im, lambda d: row >= d),
           jnp.concatenate(rows_re, axis=1), jnp.concatenate(rows_im, axis=1))
    rev = (steps(rows_re, lambda d: row < 8 - d), steps(neg_im, lambda d: row < 8 - d),
           jnp.concatenate(rows_re[::-1], axis=1), jnp.concatenate(neg_im[::-1], axis=1))
    return fwd, rev


_EYE8 = functools.partial(jnp.eye, 8, dtype=F32)


def _expand_in(b):
    return jnp.einsum("lkgpc,gh->lkgchp", b.reshape(DEPTH, 4, 8, P, C), _EYE8()).reshape(DEPTH, 4, 128, 512)


def _extract_in(e):
    return jnp.einsum("lkgchp,gh->lkgpc", e.reshape(DEPTH, 4, 8, C, 8, P), _EYE8()).reshape(DEPTH, G, P, C)


def _expand_out(c):
    return jnp.einsum("lkgcp,gh->lkgphc", c.reshape(DEPTH, 4, 8, C, P), _EYE8()).reshape(DEPTH, 4, 512, 128)


def _extract_out(e):
    return jnp.einsum("lkgphc,gh->lkgcp", e.reshape(DEPTH, 4, 8, P, 8, C), _EYE8()).reshape(DEPTH, G, C, P)


SMALL = ("norm_g", "b_in", "ssm_log_dt", "ssm_lam_re", "ssm_lam_im", "ssm_b_re", "ssm_b_im",
         "ssm_c_re", "ssm_c_im", "ssm_d", "ssm_b_glu", "pool_w", "pool_scale")
BIG = ("w_in", "ssm_w_glu", "w_branch_a", "w_branch_b", "w_out")


def _step(x, target, w, m, v, place):
    sp = {n: w[n] for n in SMALL}
    final_norm_g = w["final_norm_g"]
    wbuf = dict(zip(BIG, _cast_own(place, [w[n] for n in BIG])))
    (abar_re, abar_im, bbar_re, bbar_im), disc_vjp = jax.vjp(
        _discretize, *(sp[n] for n in ("ssm_log_dt", "ssm_lam_re", "ssm_lam_im", "ssm_b_re", "ssm_b_im")))
    powers_fwd, powers_rev = _powers(abar_re, abar_im)
    b_re_x, b_im_x = _expand_in(bbar_re), _expand_in(bbar_im)
    c_re_x, c_imn_x = _expand_out(sp["ssm_c_re"]), _expand_out(-sp["ssm_c_im"])
    b_x = jnp.concatenate([b_re_x, b_im_x], axis=3).astype(BF16)
    t = lambda a: jnp.swapaxes(a, 2, 3).astype(BF16)
    c_re_t, c_imn_t, b_re_t, b_im_t = t(c_re_x), t(c_imn_x), t(b_re_x), t(b_im_x)
    c_re_x, c_imn_x = c_re_x.astype(BF16), c_imn_x.astype(BF16)
    row = lambda n: sp[n].reshape(DEPTH, 1, -1)
    g, b_in, dsk, b_glu, scale = row("norm_g"), row("b_in"), row("ssm_d"), row("ssm_b_glu"), row("pool_scale")
    pw = sp["pool_w"].astype(BF16)

    saved = []
    for l in range(DEPTH):
        three = BIG[2:]
        if l == 0:
            wbuf["w_in"], wbuf["ssm_w_glu"] = _comm_only(
                "gather_first", _RingGatherJob([wbuf["w_in"], wbuf["ssm_w_glu"]], 0))[0]
            jobs = [_RingGatherJob([wbuf[n] for n in three], 0), _RingGatherJob([wbuf["ssm_w_glu"]], 1)]
        else:
            jobs = []
        (h, proj), res = _norm_inproj(l, x, g, wbuf["w_in"], b_in, jobs)
        if res:
            wbuf.update(zip(three, res[0][0]))
            (wbuf["ssm_w_glu"],) = res[1][0]
        wg = dict(wbuf, ssm_w_glu=wbuf["ssm_w_glu"].reshape(DEPTH, W, W), w_out=wbuf["w_out"].reshape(DEPTH, D, D))
        job = _RingGatherJob([wbuf["w_in"]], l + 1) if l + 1 < DEPTH else _RingGatherJob([wbuf[n] for n in three], l)
        (sre, sim, y1, q, ya, pooled, mixed, yb), res = _fused(
            f"branches_fwd_l{l}", (NT,),
            [_s5_fwd(l, proj, b_x, c_re_x, c_imn_x, powers_fwd, dsk, wg["ssm_w_glu"], b_glu),
             _pool_fwd(l, proj, pw, scale)], jobs=[job])
        if l + 1 < DEPTH:
            (wbuf["w_in"],) = res[0][0]
        else:
            wbuf.update(zip(three, res[0][0]))
        wg = dict(wbuf, ssm_w_glu=wbuf["ssm_w_glu"].reshape(DEPTH, W, W), w_out=wbuf["w_out"].reshape(DEPTH, D, D))
        pa, pb, mg, x_next = _merge_out(l, ya, yb, proj, x, wg["w_branch_a"], wg["w_branch_b"], wg["w_out"])
        saved.append(dict(x=x, h=h, proj=proj, sre=sre, sim=sim, y1=y1, q=q, ya=ya,
                          pooled=pooled, mixed=mixed, yb=yb, pa=pa, pb=pb, mg=mg))
        x = x_next

    loss, dx, dgf = _loss_head(x, final_norm_g.reshape(1, D), target)

    per_layer = {n: [None] * DEPTH for n in ("norm_g", "b_in", "ssm_d", "ssm_b_glu", "pool_w", "pool_scale",
                                             "dare", "daim", "dbre", "dbim", "dcre", "dcimn")}
    red = _Reducer(place, w, m, v)
    for l in reversed(range(DEPTH)):
        s = saved[l]
        (dproj, dya, dyb, dwo, dwa, dwb, dbias_g), res = _merge_out_bwd(
            l, dx, s["mg"], s["proj"], s["pa"], s["pb"], s["ya"], s["yb"],
            wg["w_out"], wg["w_branch_a"], wg["w_branch_b"], red.jobs())
        red.land(res)
        (dproj, dwg, dbg, dd, dcre, dcimn, dbre, dbim, dare, daim, dbias_a, dpw, dsc, dbias_b), res = _fused(
            f"branches_bwd_l{l}", (NT,),
            [_s5_bwd(l, dya, s["proj"], s["y1"], s["q"], s["sre"], s["sim"], c_re_t, c_imn_t, b_re_t, b_im_t,
                     *powers_rev, dsk, wg["ssm_w_glu"]),
             _pool_bwd(l, dyb, s["proj"], s["mixed"], s["pooled"], pw, scale)],
            shared=(dproj, _rows(4 * W, 0, True)), jobs=red.jobs())
        red.land(res)
        rest = [dwg.reshape(4, W // 4, W), dwa, dwb, dwo.reshape(4, D // 4, D)]
        if l == 0:
            red.add(l, "rest", BIG[1:], rest)
        halves = []
        for r in range(2):
            outs, res = _inproj_dw(l, r, s["h"], dproj, red.jobs())
            red.land(res)
            halves += outs
            if l == 0:
                red.add(l, f"in{r}", BIG[:1], outs, [r * (D // 2)])
        (dx, dg), res = _inproj_dx(l, dproj, wg["w_in"], s["x"], g, dx, red.jobs())
        red.land(res)
        if l > 0:
            red.add(l, "all", BIG[:1] * 2 + BIG[1:], halves + rest, [0, D // 2, 0, 0, 0, 0])
        for n, a in (("norm_g", dg.reshape(D)), ("b_in", jnp.concatenate([dbias_a, dbias_b, dbias_g], axis=1).reshape(NIN)),
                     ("ssm_d", dd.reshape(W)), ("ssm_b_glu", dbg.reshape(W)), ("pool_w", dpw), ("pool_scale", dsc.reshape(W)),
                     ("dare", dare), ("daim", daim), ("dbre", dbre), ("dbim", dbim), ("dcre", dcre), ("dcimn", dcimn)):
            per_layer[n][l] = a
    gs = {n: jnp.stack(a) for n, a in per_layer.items()}
    d_abar = [jnp.sum(gs.pop(n), axis=1).reshape(DEPTH, G, P) for n in ("dare", "daim")]
    (gs["ssm_log_dt"], gs["ssm_lam_re"], gs["ssm_lam_im"], gs["ssm_b_re"], gs["ssm_b_im"]) = disc_vjp(
        (*d_abar, _extract_in(gs.pop("dbre")), _extract_in(gs.pop("dbim"))))
    gs["ssm_c_re"], gs["ssm_c_im"] = _extract_out(gs.pop("dcre")), -_extract_out(gs.pop("dcimn"))
    gs["final_norm_g"] = dgf

    natural = {n: w[n].shape for n in REPLICATED}
    rw, rm, rv = {}, {}, {}
    for n in REPLICATED:
        shape = DENSE.get(n, natural[n])
        gs[n], rw[n], rm[n], rv[n] = (a.reshape(shape) for a in (gs[n], w[n], m[n], v[n]))
    small = [gs[n] for n in REPLICATED] + [loss]
    jobs = red.jobs()
    res = _pcall(None, "tail_exchange", (), [], [], [], [], jobs=jobs + [_SiblingJob(small, False)])[1]
    red.land(res[:len(jobs)])
    pair_small = _small_pair_sum(place, small, res[-1][1], [BF16 if a.ndim > 2 else F32 for a in small])
    jobs = red.jobs()
    res = _pcall(None, "tail_gather", (), [], [], [], [], jobs=jobs + [_ChipGatherJob(pair_small)])[1]
    red.land(res[:len(jobs)])
    assert not red.active
    small_parts = dict(zip(REPLICATED + ("loss",), res[-1][0]))

    k = len(REPLICATED)
    outs = _adamw_small("adamw_small", [rw[n] for n in REPLICATED], [small_parts[n] for n in REPLICATED],
                        [rm[n] for n in REPLICATED], [rv[n] for n in REPLICATED], small_parts["loss"])
    results = {n: red.big[n] for n in BIG}
    results.update({n: [outs[1 + q * k + i].reshape(natural[n]) for q in range(4)] for i, n in enumerate(REPLICATED)})
    return outs[0][0, 0], dx, results


def _place():
    x, y, c = lax.axis_index("x"), lax.axis_index("y"), lax.axis_index("c")
    chips = [(1 - x, y), (x, 1 - y), (1 - x, 1 - y)]
    return x, y, c, 2 * x + y, chips, [2 * cx + cy for cx, cy in chips]


def _remote(src, dst, ssem, rsem, dev):
    return pltpu.make_async_remote_copy(src_ref=src, dst_ref=dst, send_sem=ssem, recv_sem=rsem,
                                        device_id=dev, device_id_type=MESH)


class _GatherJob:
    def __init__(self, bufs, l):
        self.srcs, self.bufs, self.news, self.l = [], list(bufs), [], l
        self.scratch = [pltpu.SemaphoreType.DMA((len(self.bufs), 3))] * 4

    def _half(self, ref, k, h):
        rows = ref.shape[2] // 2
        return ref.at[self.l, k, pl.ds(pl.multiple_of(h * rows, 8), rows), :]

    def _ici(self, bufs, sems, a, j, k):
        _, _, c, _, chips, _ = _place()
        blk = self._half(bufs[a], k, c)
        return _remote(blk, blk, sems[0].at[a, j], sems[1].at[a, j], (*chips[j], c))

    def _d2d(self, bufs, sems, a, j, k, h):
        x, y, c, _, _, _ = _place()
        blk = self._half(bufs[a], k, h)
        return _remote(blk, blk, sems[2].at[a, j], sems[3].at[a, j], (x, y, 1 - c))

    def start(self, srcs, bufs, news, sems):
        me = _place()[3]
        for a in range(len(self.bufs)):
            for j in range(3):
                self._ici(bufs, sems, a, j, me).start()

    def finish(self, srcs, bufs, news, sems):
        _, _, c, me, _, cid = _place()
        pairs = [(a, j) for a in range(len(self.bufs)) for j in range(3)]
        for a, j in pairs:
            self._ici(bufs, sems, a, j, cid[j]).wait_recv()
            self._d2d(bufs, sems, a, j, cid[j], c).start()
        for a, j in pairs:
            self._d2d(bufs, sems, a, j, cid[j], 1 - c).wait_recv()
        for a, j in pairs:
            self._ici(bufs, sems, a, j, me).wait_send()
            self._d2d(bufs, sems, a, j, cid[j], c).wait_send()


class _RingGatherJob(_GatherJob):
    def __init__(self, bufs, l):
        super().__init__(bufs, l)
        n = len(self.bufs)
        self.scratch = [pltpu.SemaphoreType.DMA((n, 2))] * 4 + [pltpu.SemaphoreType.DMA((n, 4))] * 2

    def _rows(self, ref, k, h, part):
        half = ref.shape[2] // 2
        start, rows = (h * half, half) if part is None else (h * half + part * (half // 2), half // 2)
        return ref.at[self.l, k, pl.ds(pl.multiple_of(start, 8), rows), :]

    def _to_chip(self, bufs, sems, base, a, j, k, part):
        _, _, c, _, chips, _ = _place()
        blk = self._rows(bufs[a], k, c, part)
        return _remote(blk, blk, sems[base].at[a, j], sems[base + 1].at[a, j], (*chips[j], c))

    def _to_sibling(self, bufs, sems, a, i, k, h, part):
        x, y, c, _, _, _ = _place()
        blk = self._rows(bufs[a], k, h, part)
        return _remote(blk, blk, sems[4].at[a, i], sems[5].at[a, i], (x, y, 1 - c))

    def start(self, srcs, bufs, news, sems):
        me = _place()[3]
        for a in range(len(self.bufs)):
            for j in range(2):
                self._to_chip(bufs, sems, 0, a, j, me, None).start()

    def finish(self, srcs, bufs, news, sems):
        _, _, c, me, _, cid = _place()
        arrays = range(len(self.bufs))
        for a in arrays:
            for j in (1, 0):
                self._to_chip(bufs, sems, 0, a, j, cid[j], None).wait_recv()
                self._to_chip(bufs, sems, 2, a, 1 - j, cid[j], 1 - j).start()
                self._to_sibling(bufs, sems, a, j, cid[j], c, None).start()
        for a in arrays:
            for part in range(2):
                self._to_chip(bufs, sems, 2, a, part, cid[2], part).wait_recv()
                self._to_sibling(bufs, sems, a, 2 + part, cid[2], c, part).start()
        for a in arrays:
            for j in range(2):
                self._to_sibling(bufs, sems, a, j, cid[j], 1 - c, None).wait_recv()
                self._to_sibling(bufs, sems, a, 2 + j, cid[2], 1 - c, j).wait_recv()
        for a in arrays:
            for j in range(2):
                self._to_chip(bufs, sems, 0, a, j, me, None).wait_send()
                self._to_chip(bufs, sems, 2, a, 1 - j, cid[j], 1 - j).wait_send()
                self._to_sibling(bufs, sems, a, j, cid[j], c, None).wait_send()
                self._to_sibling(bufs, sems, a, 2 + j, cid[2], c, j).wait_send()


class _SiblingJob:
    def __init__(self, srcs, rows_half):
        self.srcs, self.bufs, self.rows_half = list(srcs), [], rows_half
        self.news = [SDS((s.shape[0], s.shape[1] // 2, s.shape[2]) if rows_half else s.shape, s.dtype) for s in srcs]
        self.scratch = [pltpu.SemaphoreType.DMA((len(self.srcs),))] * 2

    def _copy(self, srcs, news, sems, a):
        x, y, c, _, _, _ = _place()
        src = srcs[a]
        if self.rows_half:
            rows = src.shape[1] // 2
            src = src.at[:, pl.ds(pl.multiple_of((1 - c) * rows, 8), rows), :]
        return _remote(src, news[a], sems[0].at[a], sems[1].at[a], (x, y, 1 - c))

    def start(self, srcs, bufs, news, sems):
        for a in range(len(self.srcs)):
            self._copy(srcs, news, sems, a).start()

    def finish(self, srcs, bufs, news, sems):
        for a in range(len(self.srcs)):
            self._copy(srcs, news, sems, a).wait()


class _ScatterJob:
    def __init__(self, parts):
        self.srcs, self.bufs = list(parts), []
        self.news = [SDS((3,) + p.shape[1:], p.dtype) for p in parts]
        self.scratch = [pltpu.SemaphoreType.DMA((len(self.srcs), 3))] * 2

    def _copy(self, srcs, news, sems, a, j):
        _, _, c, _, chips, cid = _place()
        return _remote(srcs[a].at[cid[j]], news[a].at[j], sems[0].at[a, j], sems[1].at[a, j], (*chips[j], c))

    def start(self, srcs, bufs, news, sems):
        for a in range(len(self.srcs)):
            for j in range(3):
                self._copy(srcs, news, sems, a, j).start()

    def finish(self, srcs, bufs, news, sems):
        for a in range(len(self.srcs)):
            for j in range(3):
                self._copy(srcs, news, sems, a, j).wait()


def _comm_only(name, job):
    return _pcall(None, name, (), [], [], [], [], jobs=[job])[1][0]


class _ChipGatherJob(_GatherJob):
    def __init__(self, bufs):
        super().__init__(bufs, None)

    def _half(self, ref, k, h):
        return ref.at[k, h]


def _cast_own(place, ws):
    n = len(ws)

    def body(p_ref, *refs):
        for i_ref, o_ref in zip(refs[:n], refs[n:]):
            o_ref[...] = i_ref[...].astype(BF16)

    return pl.pallas_call(
        body, name="cast_own_shards",
        grid_spec=pltpu.PrefetchScalarGridSpec(
            num_scalar_prefetch=1, grid=(DEPTH,),
            in_specs=[pl.BlockSpec((None,) + a.shape[1:], lambda l, p: (l, 0, 0)) for a in ws],
            out_specs=[pl.BlockSpec((None, None) + a.shape[1:], lambda l, p: (l, p[1], 0, 0)) for a in ws]),
        out_shape=[SDS((DEPTH, 4) + a.shape[1:], BF16) for a in ws],
        compiler_params=_params("arbitrary"),
    )(place, *ws)


def _half_tiles(a_):
    rows = a_ // 2
    ta = min(rows, 256)
    return rows, ta, rows // ta


def _pair_sums_bf16(name, place, owns, recvs):
    n = len(owns)

    def body(p_ref, *refs):
        for own_ref, recv_ref, out_ref in zip(refs[:n], refs[n:2 * n], refs[2 * n:]):
            out_ref[...] = (own_ref[...] + recv_ref[...]).astype(BF16)

    def own_half(a):
        return pl.BlockSpec((None, a.shape[1] // 2, a.shape[2]), lambda s, p: (s, p[0], 0))

    def block(a):
        return pl.BlockSpec((None,) + a.shape[1:], lambda s, p: (s, 0, 0))

    return pl.pallas_call(
        body, name=name,
        grid_spec=pltpu.PrefetchScalarGridSpec(
            num_scalar_prefetch=1, grid=(4,),
            in_specs=[own_half(a) for a in owns] + [block(r) for r in recvs],
            out_specs=[block(r) for r in recvs]),
        out_shape=[SDS(r.shape, BF16) for r in recvs],
        compiler_params=_params("arbitrary"),
    )(place, *owns, *recvs)


def _shard_sums(name, place, owns, recvs, rbufs):
    n = len(owns)

    def body(p_ref, *refs):
        for own_ref, recv_ref, r_ref, out_ref in zip(refs[:n], refs[n:2 * n], refs[2 * n:3 * n], refs[3 * n:]):
            acc = own_ref[...] + recv_ref[...]
            for j in range(3):
                acc = acc + r_ref[j].astype(F32)
            out_ref[...] = acc

    def own_half(a):
        return pl.BlockSpec((None, a.shape[1] // 2, a.shape[2]), lambda i, p: (p[1], p[0], 0))

    def recv_block(a):
        return pl.BlockSpec((None,) + a.shape[1:], lambda i, p: (p[1], 0, 0))

    return pl.pallas_call(
        body, name=name,
        grid_spec=pltpu.PrefetchScalarGridSpec(
            num_scalar_prefetch=1, grid=(1,),
            in_specs=([own_half(a) for a in owns] + [recv_block(r) for r in recvs]
                      + [pl.BlockSpec(rb.shape, lambda i, p: (0, 0, 0)) for rb in rbufs]),
            out_specs=[pl.BlockSpec(r.shape[1:], lambda i, p: (0, 0)) for r in recvs]),
        out_shape=[SDS(r.shape[1:], F32) for r in recvs],
        compiler_params=_params("arbitrary"),
    )(place, *owns, *recvs, *rbufs)


def _small_pair_sum(place, mine, recv, dtypes):
    n = len(mine)

    def body(p_ref, *refs):
        for m_ref, r_ref, o_ref in zip(refs[:n], refs[n:2 * n], refs[2 * n:]):
            o_ref[...] = (m_ref[...] + r_ref[...]).astype(o_ref.dtype)

    def whole(a):
        zeros = (0,) * a.ndim
        return pl.BlockSpec(a.shape, lambda i, p: zeros)

    def mine_blk(a):
        zeros = (0,) * a.ndim
        return pl.BlockSpec((None,) + a.shape, lambda i, p: (p[1],) + zeros)

    return pl.pallas_call(
        body, name="small_pair_sum",
        grid_spec=pltpu.PrefetchScalarGridSpec(
            num_scalar_prefetch=1, grid=(1,),
            in_specs=[whole(a) for a in mine] + [whole(a) for a in recv],
            out_specs=[mine_blk(a) for a in mine]),
        out_shape=[SDS((4,) + a.shape, dt) for a, dt in zip(mine, dtypes)],
        compiler_params=_params("arbitrary"),
    )(place, *mine, *recv)


def _adam_math(w, g, m, v):
    m = B1 * m + (1.0 - B1) * g
    v = B2 * v + (1.0 - B2) * (g * g)
    m_hat = m / (1.0 - B1 ** STEP)
    v_hat = v / (1.0 - B2 ** STEP)
    delta = -LR * (m_hat / (jnp.sqrt(v_hat) + EPS_A) + WD * w)
    return delta, m, v


def _adamw_big(name, l, row0, w, m, v, mine, other, prev, jobs=()):
    _, _, b_ = w.shape
    _, ta, nh = _half_tiles(2 * mine.shape[0])
    prev = list(prev or [])

    def body(w_ref, m_ref, v_ref, mine_ref, other_ref, *rest):
        g_ref, d_ref, mo_ref, vo_ref = rest[len(prev):]
        g = jnp.where(pl.program_id(0) == lax.axis_index("c"), mine_ref[...], other_ref[...])
        g_ref[...] = g
        d_ref[...], mo_ref[...], vo_ref[...] = _adam_math(w_ref[...], g, m_ref[...], v_ref[...])

    slab = pl.BlockSpec((None, ta, b_), lambda h, i: (l, row0 // ta + h * nh + i, 0))
    half = pl.BlockSpec((ta, b_), lambda h, i: (i, 0))
    outs, res = _pcall(
        body, name, (2, nh), [slab, slab, slab, half, half] + [_ANY] * len(prev), [slab] * 4, [SDS(w.shape, F32)] * 4,
        (w, m, v, mine, other, *prev), aliases={5 + k: k for k in range(len(prev))}, jobs=jobs)
    return outs, res


def _adamw_small(name, ws, parts, ms, vs, loss_parts=None):
    k = len(ws)
    extra = [] if loss_parts is None else [loss_parts]

    def chip_sum(p_ref):
        p = [p_ref[k].astype(F32) for k in range(4)]
        return ((p[0] + p[1]) + p[2]) + p[3]

    def body(*refs):
        w_refs, p_refs, m_refs, v_refs = refs[:k], refs[k:2 * k], refs[2 * k:3 * k], refs[3 * k:4 * k]
        outs = refs[4 * k + len(extra):]
        if extra:
            outs[0][...] = chip_sum(refs[4 * k])
            outs = outs[1:]
        for a in range(k):
            g = chip_sum(p_refs[a])
            outs[a][...] = g
            outs[k + a][...], outs[2 * k + a][...], outs[3 * k + a][...] = _adam_math(
                w_refs[a][...], g, m_refs[a][...], v_refs[a][...])

    like = [SDS(a.shape, F32) for a in ws]
    return pl.pallas_call(
        body, name=name,
        out_shape=([SDS(loss_parts.shape[1:], F32)] if extra else []) + like * 4,
        compiler_params=pltpu.CompilerParams(vmem_limit_bytes=VMEM_LIMIT),
    )(*ws, *parts, *ms, *vs, *extra)


class _Reducer:
    def __init__(self, place, w, m, v):
        self.place, self.w, self.m, self.v = place, w, m, v
        self.active, self.riding = [], []
        self.big = {n: None for n in BIG}

    def add(self, l, tag, names, own, row0s=None):
        self.active.append(dict(l=l, key=f"{tag}_l{l}", names=names, own=list(own), row0s=row0s or [0] * len(names),
                                stage=0))

    def jobs(self):
        self.riding = list(self.active)
        return [(_SiblingJob(g["own"], True), _ScatterJob(g.get("parts", [])), _SiblingJob(g.get("shard", []), False))
                [g["stage"]] for g in self.riding]

    def land(self, res):
        for g, (_, news) in zip(self.riding, res):
            if g["stage"] == 0:
                g["recv"] = news
                g["parts"] = _pair_sums_bf16(f"pair_sums_{g['key']}", self.place, g["own"], news)
            elif g["stage"] == 1:
                g["shard"] = _shard_sums(f"shard_sums_{g['key']}", self.place, g["own"], g["recv"], news)
            else:
                for n, mine, other, row0 in zip(g["names"], g["shard"], news, g["row0s"]):
                    self.big[n] = _adamw_big(f"adamw_{n}_{row0}_{g['key']}", g["l"], row0, self.w[n], self.m[n], self.v[n],
                                             mine, other, self.big[n])[0]
                self.active.remove(g)
            g["stage"] += 1
        self.riding = []


WEIGHTS = ("norm_g", "w_in", "b_in", "ssm_log_dt", "ssm_lam_re", "ssm_lam_im", "ssm_b_re", "ssm_b_im", "ssm_c_re",
           "ssm_c_im", "ssm_d", "ssm_w_glu", "ssm_b_glu", "pool_w", "pool_scale", "w_branch_a", "w_branch_b", "w_out",
           "final_norm_g")
REPLICATED = SMALL + ("final_norm_g",)
DENSE = {"ssm_b_re": (DEPTH, G, P * C), "ssm_b_im": (DEPTH, G, P * C), "final_norm_g": (2, D // 2)}


def kernel(x, norm_g, w_in, b_in, ssm_log_dt, ssm_lam_re, ssm_lam_im, ssm_b_re, ssm_b_im, ssm_c_re, ssm_c_im, ssm_d, ssm_w_glu, ssm_b_glu, pool_w, pool_scale, w_branch_a, w_branch_b, w_out, final_norm_g, loss_target, m_norm_g, m_w_in, m_b_in, m_ssm_log_dt, m_ssm_lam_re, m_ssm_lam_im, m_ssm_b_re, m_ssm_b_im, m_ssm_c_re, m_ssm_c_im, m_ssm_d, m_ssm_w_glu, m_ssm_b_glu, m_pool_w, m_pool_scale, m_w_branch_a, m_w_branch_b, m_w_out, m_final_norm_g, v_norm_g, v_w_in, v_b_in, v_ssm_log_dt, v_ssm_lam_re, v_ssm_lam_im, v_ssm_b_re, v_ssm_b_im, v_ssm_c_re, v_ssm_c_im, v_ssm_d, v_ssm_w_glu, v_ssm_b_glu, v_pool_w, v_pool_scale, v_w_branch_a, v_w_branch_b, v_w_out, v_final_norm_g):
    w = dict(zip(WEIGHTS, (norm_g, w_in, b_in, ssm_log_dt, ssm_lam_re, ssm_lam_im, ssm_b_re, ssm_b_im, ssm_c_re,
                           ssm_c_im, ssm_d, ssm_w_glu, ssm_b_glu, pool_w, pool_scale, w_branch_a, w_branch_b, w_out,
                           final_norm_g)))
    m = dict(zip(WEIGHTS, (m_norm_g, m_w_in, m_b_in, m_ssm_log_dt, m_ssm_lam_re, m_ssm_lam_im, m_ssm_b_re, m_ssm_b_im,
                           m_ssm_c_re, m_ssm_c_im, m_ssm_d, m_ssm_w_glu, m_ssm_b_glu, m_pool_w, m_pool_scale,
                           m_w_branch_a, m_w_branch_b, m_w_out, m_final_norm_g)))
    v = dict(zip(WEIGHTS, (v_norm_g, v_w_in, v_b_in, v_ssm_log_dt, v_ssm_lam_re, v_ssm_lam_im, v_ssm_b_re, v_ssm_b_im,
                           v_ssm_c_re, v_ssm_c_im, v_ssm_d, v_ssm_w_glu, v_ssm_b_glu, v_pool_w, v_pool_scale,
                           v_w_branch_a, v_w_branch_b, v_w_out, v_final_norm_g)))
    place = jnp.stack([lax.axis_index("c"), 2 * lax.axis_index("x") + lax.axis_index("y")]).astype(jnp.int32)

    total_loss, dx, results = _step(x[0], loss_target[0], w, m, v, place)
    return (total_loss, dx[None], *[results[n][q] for q in range(4) for n in WEIGHTS])
```

```python
import functools

import jax
import jax.numpy as jnp
from jax import lax
from jax.experimental import pallas as pl
from jax.experimental.pallas import tpu as pltpu

F32, BF16 = jnp.float32, jnp.bfloat16
SDS = jax.ShapeDtypeStruct
MESH = pl.DeviceIdType.MESH

DEPTH = 2
L = 2048
D = 1024
NIN = 4096
W = 512
G, P, C = 32, 64, 16
GP = G * P
WINS = (2, 4, 8, 16)
TM = 256
NT = L // TM
TMM = 512
TK = 1024
EPS = 1e-6
VMEM_LIMIT = 56 * 2**20

LR, B1, B2, EPS_A, WD, STEP = 0.001, 0.9, 0.999, 1e-08, 0.01, 10


def _params(*sem):
    return pltpu.CompilerParams(dimension_semantics=sem, vmem_limit_bytes=VMEM_LIMIT)


_ANY = pl.BlockSpec(memory_space=pl.ANY)


def _full(shape):
    zeros = (0,) * len(shape)
    return pl.BlockSpec(shape, lambda *_: zeros)


def _layer(l, shape):
    zeros = (0,) * len(shape)
    return pl.BlockSpec((None,) + shape, lambda *_: (l,) + zeros)


def _rows(width, col=0, reverse=False, tm=TM):
    if reverse:
        return pl.BlockSpec((tm, width), lambda i: (L // tm - 1 - i, col))
    return pl.BlockSpec((tm, width), lambda i: (i, col))


def _rows_mm(width, col=0):
    return _rows(width, col, False, TMM)


def _pcall(body, name, grid, in_specs, out_specs, out_shape, args, scratch=(), aliases=None, jobs=()):
    in_specs, out_specs, out_shape, args, scratch = list(in_specs), list(out_specs), list(out_shape), list(args), list(scratch)
    aliases = dict(aliases or {})
    jobs = [j for j in jobs if j is not None]
    n_in, n_out, n_scr = len(in_specs), len(out_specs), len(scratch)
    srcs = [s for j in jobs for s in j.srcs]
    bufs = [b for j in jobs for b in j.bufs]
    news = [s for j in jobs for s in j.news]
    aliases.update({n_in + len(srcs) + k: n_out + k for k in range(len(bufs))})

    def hosted(*refs):
        cuts = [n_in, len(srcs), len(bufs), n_out, len(bufs), len(news), n_scr]
        parts, p = [], 0
        for n in cuts:
            parts.append(refs[p:p + n])
            p += n
        ins, src_r, _, outs, buf_r, new_r, scr = parts
        sem_r = refs[p:]
        views, ps, pb, pn, pm = [], 0, 0, 0, 0
        for j in jobs:
            views.append((src_r[ps:ps + len(j.srcs)], buf_r[pb:pb + len(j.bufs)], new_r[pn:pn + len(j.news)],
                          sem_r[pm:pm + len(j.scratch)]))
            ps, pb, pn, pm = ps + len(j.srcs), pb + len(j.bufs), pn + len(j.news), pm + len(j.scratch)

        def run(phase):
            for j, v in zip(jobs, views):
                getattr(j, phase)(*v)

        def at_step(step):
            return functools.reduce(jnp.logical_and, [pl.program_id(d) == step(d) for d in range(len(grid))])

        if not grid:
            run("start")
            run("finish")
            return
        pl.when(at_step(lambda d: 0))(lambda: run("start"))
        body(*ins, *outs, *scr)
        pl.when(at_step(lambda d: grid[d] - 1))(lambda: run("finish"))

    outs = pl.pallas_call(
        hosted if jobs else body, name=name, **({"grid": grid} if grid else {}),
        in_specs=in_specs + [_ANY] * (len(srcs) + len(bufs)), out_specs=out_specs + [_ANY] * (len(bufs) + len(news)),
        out_shape=out_shape + [SDS(b.shape, b.dtype) for b in bufs] + news,
        input_output_aliases=aliases, scratch_shapes=scratch + [s for j in jobs for s in j.scratch],
        compiler_params=_params(*(("arbitrary",) * len(grid))))(*args, *srcs, *bufs)
    res, pb, pn = [], n_out, n_out + len(bufs)
    for j in jobs:
        res.append((list(outs[pb:pb + len(j.bufs)]), list(outs[pn:pn + len(j.news)])))
        pb, pn = pb + len(j.bufs), pn + len(j.news)
    return list(outs[:n_out]), res


def _fused(name, grid, parts, shared=None, jobs=()):
    def body(*refs):
        pos = [0]

        def take(n):
            pos[0] += n
            return refs[pos[0] - n:pos[0]]

        ins = [take(len(p["in_specs"])) for p in parts]
        if shared:
            take(1)
            block = take(1)[0]
        outs = [take(len(p["out_specs"])) for p in parts]
        scr = [take(len(p["scratch"])) for p in parts]
        col = 0
        for p, i, o, s in zip(parts, ins, outs, scr):
            view = []
            if shared:
                view = [block.at[:, pl.ds(col, p["width"])]]
                col += p["width"]
            p["body"](*i, *view, *o, *s)

    in_specs = [s for p in parts for s in p["in_specs"]] + ([_ANY] if shared else [])
    out_specs = ([shared[1]] if shared else []) + [s for p in parts for s in p["out_specs"]]
    out_shape = ([SDS(shared[0].shape, shared[0].dtype)] if shared else []) + [s for p in parts for s in p["out_shape"]]
    args = [a for p in parts for a in p["args"]] + ([shared[0]] if shared else [])
    return _pcall(body, name, grid, in_specs, out_specs, out_shape, args, [s for p in parts for s in p["scratch"]],
                  {len(in_specs) - 1: 0} if shared else None, jobs)


def _dot(a, b):
    return jnp.dot(a, b, preferred_element_type=F32)


def _dot_nt(a, b):
    return lax.dot_general(a, b, (((1,), (1,)), ((), ())), preferred_element_type=F32)


def _dot_tn(a, b):
    return lax.dot_general(a, b, (((0,), (0,)), ((), ())), preferred_element_type=F32)


_K0 = 0.7978845608028654
_K1 = 0.044715


def _gelu(x):
    return 0.5 * x * (1.0 + jnp.tanh(_K0 * (x + _K1 * (x * x * x))))


def _gelu_grad(x):
    t = jnp.tanh(_K0 * (x + _K1 * (x * x * x)))
    return 0.5 * (1.0 + t) + 0.5 * x * (1.0 - t * t) * (_K0 * (1.0 + 3.0 * _K1 * x * x))


def _sigmoid(x):
    return jax.nn.sigmoid(x)


def _norm_inproj(l, x, g, w, b, jobs=()):
    def body(x_ref, g_ref, w_ref, b_ref, h_ref, proj_ref):
        xv = x_ref[...]
        r = lax.rsqrt(jnp.mean(xv * xv, axis=-1, keepdims=True) + EPS)
        hb = ((xv * r) * g_ref[...]).astype(BF16)
        h_ref[...] = hb
        for j in range(4):
            cs = slice(j * 1024, (j + 1) * 1024)
            proj_ref[:, cs] = _dot(hb, w_ref[j]) + b_ref[:, cs]

    return _pcall(
        body, f"norm_inproj_l{l}", (L // TMM,),
        [_rows_mm(D), _layer(l, (1, D)), _layer(l, (4, D, 1024)), _layer(l, (1, NIN))],
        [_rows_mm(D), _rows_mm(NIN)],
        [SDS((L, D), BF16), SDS((L, NIN), F32)],
        (x, g, w, b), jobs=jobs)


def _scan_tile(re_ref, im_ref, st_re, st_im, cr_re, cr_im, carry, reverse):
    def chunk(ci, carry):
        c = (TM // 8 - 1 - ci) if reverse else ci
        rows = pl.ds(pl.multiple_of(c * 8, 8), 8)
        new = []
        for lb in range(GP // 512):
            cols = slice(lb * 512, (lb + 1) * 512)
            vr = re_ref[rows, cols]
            vi = im_ref[rows, cols]
            for s, d in enumerate((1, 2, 4)):
                ar = st_re[8 * s:8 * s + 8, cols]
                ai = st_im[8 * s:8 * s + 8, cols]
                sr = pltpu.roll(vr, 8 - d if reverse else d, 0)
                si = pltpu.roll(vi, 8 - d if reverse else d, 0)
                vr, vi = vr + ar * sr - ai * si, vi + ar * si + ai * sr
            cr, ci_ = carry[2 * lb], carry[2 * lb + 1]
            pr = cr_re[:, cols]
            pi = cr_im[:, cols]
            vr, vi = vr + pr * cr - pi * ci_, vi + pr * ci_ + pi * cr
            re_ref[rows, cols] = vr
            im_ref[rows, cols] = vi
            if reverse:
                new += [vr[0:1], vi[0:1]]
            else:
                new += [vr[7:8], vi[7:8]]
        return tuple(new)

    return lax.fori_loop(0, TM // 8, chunk, carry)


def _load_carry(car_ref):
    return tuple(car_ref[r:r + 1, lb * 512:(lb + 1) * 512] for lb in range(GP // 512) for r in (0, 1))


def _store_carry(car_ref, carry):
    for lb in range(GP // 512):
        car_ref[0:1, lb * 512:(lb + 1) * 512] = carry[2 * lb]
        car_ref[1:2, lb * 512:(lb + 1) * 512] = carry[2 * lb + 1]


def _s5_fwd(l, proj, bexp, cre, cimn, powers, dsk, wglu, bglu):
    def body(ua_ref, za_ref, bexp_ref, cre_ref, cimn_ref, st_re_ref, st_im_ref, cr_re_ref, cr_im_ref,
             d_ref, wg_ref, bg_ref, sre_ref, sim_ref, y1_ref, q_ref, ya_ref, car_ref):
        @pl.when(pl.program_id(0) == 0)
        def _():
            car_ref[...] = jnp.zeros_like(car_ref)

        u = ua_ref[...]
        ub = u.astype(BF16)
        for k in range(4):
            bu = _dot(ub[:, 128 * k:128 * (k + 1)], bexp_ref[k])
            sre_ref[:, 512 * k:512 * (k + 1)] = bu[:, :512]
            sim_ref[:, 512 * k:512 * (k + 1)] = bu[:, 512:]
        carry = _scan_tile(sre_ref, sim_ref, st_re_ref, st_im_ref, cr_re_ref, cr_im_ref, _load_carry(car_ref), False)
        _store_carry(car_ref, carry)
        for k in range(4):
            blk = slice(512 * k, 512 * (k + 1))
            ks = slice(128 * k, 128 * (k + 1))
            y0 = _dot(sre_ref[:, blk].astype(BF16), cre_ref[k]) + _dot(sim_ref[:, blk].astype(BF16), cimn_ref[k])
            y1_ref[:, ks] = y0 + d_ref[:, ks] * u[:, ks]
        y2 = _gelu(y1_ref[...])
        q = _dot(y2.astype(BF16), wg_ref[...]) + bg_ref[...]
        q_ref[...] = q
        za = za_ref[...]
        ya_ref[...] = ((y2 * _sigmoid(q)) * (za * _sigmoid(za))).astype(BF16)

    return dict(
        body=body,
        in_specs=[_rows(W, 0), _rows(W, 1), _layer(l, (4, 128, 1024)), _layer(l, (4, 512, 128)),
                  _layer(l, (4, 512, 128)), _layer(l, (24, GP)), _layer(l, (24, GP)), _layer(l, (8, GP)),
                  _layer(l, (8, GP)), _layer(l, (1, W)), _layer(l, (W, W)), _layer(l, (1, W))],
        out_specs=[_rows(GP), _rows(GP), _rows(W), _rows(W), _rows(W)],
        out_shape=[SDS((L, GP), F32), SDS((L, GP), F32), SDS((L, W), F32), SDS((L, W), F32), SDS((L, W), BF16)],
        args=(proj, proj, bexp, cre, cimn, *powers, dsk, wglu, bglu),
        scratch=[pltpu.VMEM((8, GP), F32)])


def _pool_fwd(l, proj, pw, scale):
    def body(ub_ref, zb_ref, pw_ref, sc_ref, pooled_ref, mixed_ref, yb_ref, buf):
        i = pl.program_id(0)

        @pl.when(i == 0)
        def _():
            buf[0:16, :] = jnp.zeros((16, W), F32)

        u = ub_ref[...]
        buf[16:16 + TM, :] = u
        t = i * TM + lax.broadcasted_iota(jnp.int32, (TM, 128), 0)
        for gi, win in enumerate(WINS):
            cs = slice(128 * gi, 128 * (gi + 1))
            acc = u[:, cs]
            for k in range(1, win):
                acc = acc + buf[16 - k:16 - k + TM, cs]
            cnt = jnp.minimum(t + 1, win).astype(F32)
            pb = (acc / cnt - u[:, cs]).astype(BF16)
            pooled_ref[:, cs] = pb
            mixed_ref[:, cs] = _dot(pb, pw_ref[gi])
        zb = zb_ref[...]
        yb_ref[...] = ((mixed_ref[...] * sc_ref[...]) * (zb * _sigmoid(zb))).astype(BF16)
        buf[0:16, :] = buf[TM:TM + 16, :]

    return dict(
        body=body,
        in_specs=[_rows(W, 2), _rows(W, 3), _layer(l, (4, 128, 128)), _layer(l, (1, W))],
        out_specs=[_rows(W), _rows(W), _rows(W)],
        out_shape=[SDS((L, W), BF16), SDS((L, W), F32), SDS((L, W), BF16)],
        args=(proj, proj, pw, scale),
        scratch=[pltpu.VMEM((TM + 16, W), F32)])


def _merge_out(l, ya, yb, proj, x, wa, wb, wo):
    def body(ya_ref, yb_ref, ga_ref, gb_ref, x_ref, wa_ref, wb_ref, wo_ref, pa_ref, pb_ref, mg_ref, xo_ref):
        ya = ya_ref[...]
        yb = yb_ref[...]
        for j in range(4):
            cs = slice(256 * j, 256 * (j + 1))
            pa_ref[:, cs] = _dot(ya, wa_ref[j])
            pb_ref[:, cs] = _dot(yb, wb_ref[j])
        merged = _sigmoid(ga_ref[...]) * pa_ref[...] + _sigmoid(gb_ref[...]) * pb_ref[...]
        mb = merged.astype(BF16)
        mg_ref[...] = mb
        xo_ref[...] = x_ref[...] + _dot(mb, wo_ref[...])

    return pl.pallas_call(
        body, name=f"merge_out_l{l}", grid=(L // TMM,),
        in_specs=[_rows_mm(W), _rows_mm(W), _rows_mm(D, 2), _rows_mm(D, 3), _rows_mm(D),
                  _layer(l, (4, W, 256)), _layer(l, (4, W, 256)), _layer(l, (D, D))],
        out_specs=[_rows_mm(D), _rows_mm(D), _rows_mm(D), _rows_mm(D)],
        out_shape=[SDS((L, D), F32), SDS((L, D), F32), SDS((L, D), BF16), SDS((L, D), F32)],
        compiler_params=_params("arbitrary"),
    )(ya, yb, proj, proj, x, wa, wb, wo)


def _loss_head(x, gf, target):
    def body(x_ref, g_ref, t_ref, loss_ref, dx_ref, dg_ref):
        @pl.when(pl.program_id(0) == 0)
        def _():
            loss_ref[...] = jnp.zeros_like(loss_ref)
            dg_ref[...] = jnp.zeros_like(dg_ref)

        xv = x_ref[...]
        g = g_ref[...]
        r = lax.rsqrt(jnp.mean(xv * xv, axis=-1, keepdims=True) + EPS)
        xn = xv * r
        err = xn * g - t_ref[...]
        part = jnp.sum(jnp.mean(err * err, axis=-1, keepdims=True), axis=0, keepdims=True)
        loss_ref[...] += 0.5 * part
        dy = err * (1.0 / D)
        dg_ref[...] += jnp.sum(dy * xn, axis=0, keepdims=True)
        dxn = dy * g
        dx_ref[...] = r * (dxn - xn * jnp.mean(dxn * xn, axis=-1, keepdims=True))

    return pl.pallas_call(
        body, name="loss_head", grid=(NT,),
        in_specs=[_rows(D), _full((1, D)), _rows(D)],
        out_specs=[_full((2, 128)), _rows(D), _full((1, D))],
        out_shape=[SDS((2, 128), F32), SDS((L, D), F32), SDS((1, D), F32)],
        compiler_params=_params("arbitrary"),
    )(x, gf, target)


def _merge_out_bwd(l, dxn, mg, proj, pa, pb, ya, yb, wo, wa, wb, jobs=()):
    def body(dx_ref, mg_ref, ga_ref, gb_ref, pa_ref, pb_ref, ya_ref, yb_ref, wo_ref, wa_ref, wb_ref,
             dg_ref, dya_ref, dyb_ref, dwo_ref, dwa_ref, dwb_ref, dbias_ref):
        @pl.when(pl.program_id(0) == 0)
        def _():
            for ref in (dwo_ref, dwa_ref, dwb_ref, dbias_ref):
                ref[...] = jnp.zeros_like(ref)

        dxb = dx_ref[...].astype(BF16)
        dm = _dot_nt(dxb, wo_ref[...])
        sa = _sigmoid(ga_ref[...])
        sb = _sigmoid(gb_ref[...])
        dga = dm * pa_ref[...] * (sa * (1.0 - sa))
        dgb = dm * pb_ref[...] * (sb * (1.0 - sb))
        dg_ref[:, :D] = dga.astype(BF16)
        dg_ref[:, D:] = dgb.astype(BF16)
        dbias_ref[:, :D] += jnp.sum(dga, axis=0, keepdims=True)
        dbias_ref[:, D:] += jnp.sum(dgb, axis=0, keepdims=True)
        dpa = (dm * sa).astype(BF16)
        dpb = (dm * sb).astype(BF16)
        ya = ya_ref[...]
        yb = yb_ref[...]
        dya = jnp.zeros((TM, W), F32)
        dyb = jnp.zeros((TM, W), F32)
        for j in range(4):
            cs = slice(256 * j, 256 * (j + 1))
            dya = dya + _dot_nt(dpa[:, cs], wa_ref[j])
            dyb = dyb + _dot_nt(dpb[:, cs], wb_ref[j])
            dwa_ref[j] += _dot_tn(ya, dpa[:, cs])
            dwb_ref[j] += _dot_tn(yb, dpb[:, cs])
        dya_ref[...] = dya
        dyb_ref[...] = dyb
        dwo_ref[...] += _dot_tn(mg_ref[...], dxb)

    return _pcall(
        body, f"merge_out_bwd_l{l}", (NT,),
        [_rows(D), _rows(D), _rows(D, 2), _rows(D, 3), _rows(D), _rows(D), _rows(W), _rows(W),
         _layer(l, (D, D)), _layer(l, (4, W, 256)), _layer(l, (4, W, 256))],
        [_rows(2 * D, 1), _rows(W), _rows(W), _full((D, D)), _full((4, W, 256)), _full((4, W, 256)), _full((1, 2 * D))],
        [SDS((L, NIN), BF16), SDS((L, W), F32), SDS((L, W), F32),
         SDS((D, D), F32), SDS((4, W, 256), F32), SDS((4, W, 256), F32), SDS((1, 2 * D), F32)],
        (dxn, mg, proj, proj, pa, pb, ya, yb, wo, wa, wb), jobs=jobs)


def _pool_bwd(l, dyb, proj, mixed, pooled, pw, scale):
    def body(dyb_ref, zb_ref, mixed_ref, pooled_ref, pw_ref, sc_ref, db_ref, dpw_ref, dsc_ref, dbias_ref, buf):
        i = pl.program_id(0)
        tile = NT - 1 - i

        @pl.when(i == 0)
        def _():
            dpw_ref[...] = jnp.zeros_like(dpw_ref)
            dsc_ref[...] = jnp.zeros_like(dsc_ref)
            dbias_ref[...] = jnp.zeros_like(dbias_ref)
            buf[TM:TM + 16, :] = jnp.zeros((16, W), F32)

        dyb = dyb_ref[...]
        zb = zb_ref[...]
        mixed = mixed_ref[...]
        sc = sc_ref[...]
        sg = _sigmoid(zb)
        dyb0 = dyb * (zb * sg)
        dzb = dyb * (mixed * sc) * (sg * (1.0 + zb * (1.0 - sg)))
        db_ref[:, W:] = dzb.astype(BF16)
        dbias_ref[:, W:] += jnp.sum(dzb, axis=0, keepdims=True)
        dsc_ref[...] += jnp.sum(dyb0 * mixed, axis=0, keepdims=True)
        dmix = (dyb0 * sc).astype(BF16)
        t = tile * TM + lax.broadcasted_iota(jnp.int32, (TM, 128), 0)
        for gi, win in enumerate(WINS):
            cs = slice(128 * gi, 128 * (gi + 1))
            dpw_ref[gi] += _dot_tn(pooled_ref[:, cs], dmix[:, cs])
            dpool = _dot_nt(dmix[:, cs], pw_ref[gi])
            cnt = jnp.minimum(t + 1, win).astype(F32)
            e = dpool / cnt
            buf[0:TM, cs] = e
            acc = e - dpool
            for k in range(1, win):
                acc = acc + buf[k:k + TM, cs]
            db_ref[:, cs] = acc.astype(BF16)
            dbias_ref[:, cs] += jnp.sum(acc, axis=0, keepdims=True)
        buf[TM:TM + 16, :] = buf[0:16, :]

    return dict(
        body=body, width=2 * W,
        in_specs=[_rows(W, 0, True), _rows(W, 3, True), _rows(W, 0, True), _rows(W, 0, True),
                  _layer(l, (4, 128, 128)), _layer(l, (1, W))],
        out_specs=[_full((4, 128, 128)), _full((1, W)), _full((1, 2 * W))],
        out_shape=[SDS((4, 128, 128), F32), SDS((1, W), F32), SDS((1, 2 * W), F32)],
        args=(dyb, proj, mixed, pooled, pw, scale),
        scratch=[pltpu.VMEM((TM + 16, W), F32)])


def _s5_bwd(l, dya, proj, y1, q, sre, sim, cret, cimnt, bret, bimt, st_re, st_im, cr_re, cr_im, dsk, wglu):
    def halo(i):
        return (jnp.maximum((NT - 1 - i) * (TM // 8) - 1, 0), 0)

    def body(dya_ref, ua_ref, za_ref, y1_ref, q_ref, sre_ref, sim_ref, hre_ref, him_ref,
             cret_ref, cimnt_ref, bret_ref, bimt_ref, st_re_ref, st_im_ref, cr_re_ref, cr_im_ref, d_ref, wg_ref,
             da_ref, dwg_ref, dbg_ref, dd_ref, dcre_ref, dcimn_ref, dbre_ref, dbim_ref, dare_ref, daim_ref, dbias_ref,
             lre, lim, car_ref):
        i = pl.program_id(0)
        tile = NT - 1 - i

        @pl.when(i == 0)
        def _():
            for ref in (dwg_ref, dbg_ref, dd_ref, dcre_ref, dcimn_ref, dbre_ref, dbim_ref, dare_ref, daim_ref, dbias_ref,
                        car_ref):
                ref[...] = jnp.zeros_like(ref)

        u = ua_ref[...]
        za = za_ref[...]
        y1 = y1_ref[...]
        dya = dya_ref[...]
        y2 = _gelu(y1)
        sg = _sigmoid(q_ref[...])
        sgz = _sigmoid(za)
        dy3 = dya * (za * sgz)
        dza = dya * (y2 * sg) * (sgz * (1.0 + za * (1.0 - sgz)))
        da_ref[:, W:] = dza.astype(BF16)
        dbias_ref[:, W:] += jnp.sum(dza, axis=0, keepdims=True)
        dq = dy3 * y2 * (sg * (1.0 - sg))
        dqb = dq.astype(BF16)
        dy2 = dy3 * sg + _dot_nt(dqb, wg_ref[...])
        dwg_ref[...] += _dot_tn(y2.astype(BF16), dqb)
        dbg_ref[...] += jnp.sum(dq, axis=0, keepdims=True)
        dy1 = dy2 * _gelu_grad(y1)
        dd_ref[...] += jnp.sum(dy1 * u, axis=0, keepdims=True)
        dy1b = dy1.astype(BF16)
        ub = u.astype(BF16)
        for k in range(4):
            blk = slice(512 * k, 512 * (k + 1))
            ks = slice(128 * k, 128 * (k + 1))
            lre[:, blk] = _dot(dy1b[:, ks], cret_ref[k])
            lim[:, blk] = _dot(dy1b[:, ks], cimnt_ref[k])
            dcre_ref[k] += _dot_tn(sre_ref[:, blk].astype(BF16), dy1b[:, ks])
            dcimn_ref[k] += _dot_tn(sim_ref[:, blk].astype(BF16), dy1b[:, ks])
        carry = _scan_tile(lre, lim, st_re_ref, st_im_ref, cr_re_ref, cr_im_ref, _load_carry(car_ref), True)
        _store_carry(car_ref, carry)

        rowid = lax.broadcasted_iota(jnp.int32, (8, 512), 0)
        gate = (tile > 0).astype(F32)

        def chunk(c, _):
            rows = pl.ds(pl.multiple_of(c * 8, 8), 8)
            prows = pl.ds(pl.multiple_of(jnp.maximum(c - 1, 0) * 8, 8), 8)
            for lb in range(GP // 512):
                cols = slice(lb * 512, (lb + 1) * 512)
                sr = sre_ref[rows, cols]
                si = sim_ref[rows, cols]
                pr = jnp.where(c == 0, hre_ref[7:8, cols] * gate, sre_ref[prows, cols][7:8])
                pi = jnp.where(c == 0, him_ref[7:8, cols] * gate, sim_ref[prows, cols][7:8])
                sr = jnp.where(rowid == 0, pr, pltpu.roll(sr, 1, 0))
                si = jnp.where(rowid == 0, pi, pltpu.roll(si, 1, 0))
                lr = lre[rows, cols]
                li = lim[rows, cols]
                dare_ref[:, cols] += sr * lr + si * li
                daim_ref[:, cols] += sr * li - si * lr
            return 0

        lax.fori_loop(0, TM // 8, chunk, 0)

        for k in range(4):
            blk = slice(512 * k, 512 * (k + 1))
            ks = slice(128 * k, 128 * (k + 1))
            lrb = lre[:, blk].astype(BF16)
            lib = lim[:, blk].astype(BF16)
            du = dy1[:, ks] * d_ref[:, ks] + _dot(lrb, bret_ref[k]) + _dot(lib, bimt_ref[k])
            da_ref[:, ks] = du.astype(BF16)
            dbias_ref[:, ks] += jnp.sum(du, axis=0, keepdims=True)
            dbre_ref[k] += _dot_tn(ub[:, ks], lrb)
            dbim_ref[k] += _dot_tn(ub[:, ks], lib)

    return dict(
        body=body, width=2 * W,
        in_specs=[_rows(W, 0, True), _rows(W, 0, True), _rows(W, 1, True), _rows(W, 0, True), _rows(W, 0, True),
                  _rows(GP, 0, True), _rows(GP, 0, True),
                  pl.BlockSpec((8, GP), halo), pl.BlockSpec((8, GP), halo),
                  _layer(l, (4, 128, 512)), _layer(l, (4, 128, 512)), _layer(l, (4, 512, 128)), _layer(l, (4, 512, 128)),
                  _layer(l, (24, GP)), _layer(l, (24, GP)), _layer(l, (8, GP)), _layer(l, (8, GP)), _layer(l, (1, W)),
                  _layer(l, (W, W))],
        out_specs=[_full((W, W)), _full((1, W)), _full((1, W)),
                   _full((4, 512, 128)), _full((4, 512, 128)), _full((4, 128, 512)), _full((4, 128, 512)),
                   _full((8, GP)), _full((8, GP)), _full((1, 2 * W))],
        out_shape=[SDS((W, W), F32), SDS((1, W), F32), SDS((1, W), F32),
                   SDS((4, 512, 128), F32), SDS((4, 512, 128), F32), SDS((4, 128, 512), F32), SDS((4, 128, 512), F32),
                   SDS((8, GP), F32), SDS((8, GP), F32), SDS((1, 2 * W), F32)],
        args=(dya, proj, proj, y1, q, sre, sim, sre, sim, cret, cimnt, bret, bimt, st_re, st_im, cr_re, cr_im, dsk,
              wglu),
        scratch=[pltpu.VMEM((TM, GP), F32), pltpu.VMEM((TM, GP), F32), pltpu.VMEM((8, GP), F32)])


def _inproj_dw(l, r, h, dproj, jobs=(), rows=D // 2):
    def body(h_ref, dp_ref, dw_ref):
        part = _dot_tn(h_ref[...], dp_ref[...])

        @pl.when(pl.program_id(1) == 0)
        def _():
            dw_ref[...] = part

        @pl.when(pl.program_id(1) > 0)
        def _():
            dw_ref[...] += part

    return _pcall(
        body, f"inproj_dw{r}_l{l}", (4, L // TK),
        [pl.BlockSpec((TK, rows), lambda j, i: (i, r)), pl.BlockSpec((TK, 1024), lambda j, i: (i, j))],
        [pl.BlockSpec((None, rows, 1024), lambda j, i: (j, 0, 0))],
        [SDS((4, rows, 1024), F32)],
        (h, dproj), jobs=jobs)


def _inproj_dx(l, dproj, w, x, g, dxn, jobs=()):
    def body(dp_ref, w_ref, x_ref, g_ref, dxn_ref, dx_ref, dg_ref):
        @pl.when(pl.program_id(0) == 0)
        def _():
            dg_ref[...] = jnp.zeros_like(dg_ref)

        dh = _dot_nt(dp_ref[:, 0:1024], w_ref[0])
        for j in range(1, 4):
            dh = dh + _dot_nt(dp_ref[:, j * 1024:(j + 1) * 1024], w_ref[j])
        xv = x_ref[...]
        r = lax.rsqrt(jnp.mean(xv * xv, axis=-1, keepdims=True) + EPS)
        xn = xv * r
        dg_ref[...] += jnp.sum(dh * xn, axis=0, keepdims=True)
        dn = dh * g_ref[...]
        dx_ref[...] = dxn_ref[...] + r * (dn - xn * jnp.mean(dn * xn, axis=-1, keepdims=True))

    return _pcall(
        body, f"inproj_dx_l{l}", (L // TMM,),
        [_rows_mm(NIN), _layer(l, (4, D, 1024)), _rows_mm(D), _layer(l, (1, D)), _rows_mm(D)],
        [_rows_mm(D), _full((1, D))],
        [SDS((L, D), F32), SDS((1, D), F32)],
        (dproj, w, x, g, dxn), jobs=jobs)


def _discretize(log_dt, lam_re, lam_im, b_re, b_im):
    dt = jnp.exp(log_dt)[..., None]
    mag = jnp.exp(lam_re * dt)
    ang = lam_im * dt
    abar_re = mag * jnp.cos(ang)
    abar_im = mag * jnp.sin(ang)
    num_re = abar_re - 1.0
    num_im = abar_im
    den = lam_re * lam_re + lam_im * lam_im
    coef_re = (num_re * lam_re + num_im * lam_im) / den
    coef_im = (num_im * lam_re - num_re * lam_im) / den
    bbar_re = coef_re[..., None] * b_re - coef_im[..., None] * b_im
    bbar_im = coef_re[..., None] * b_im + coef_im[..., None] * b_re
    return abar_re, abar_im, bbar_re, bbar_im


def _powers(abar_re, abar_im):
    ar, ai = abar_re.reshape(DEPTH, 1, GP), abar_im.reshape(DEPTH, 1, GP)
    rows_re, rows_im = [ar], [ai]
    for _ in range(7):
        pr, pi = rows_re[-1], rows_im[-1]
        rows_re.append(pr * ar - pi * ai)
        rows_im.append(pr * ai + pi * ar)
    row = jnp.arange(8)[:, None]

    def steps(rows, keep):
        return jnp.concatenate([jnp.where(keep(d), rows[d - 1], 0.0) for d in (1, 2, 4)], axis=1)

    neg_im = [-r for r in rows_im]
    fwd = (steps(rows_re, lambda d: row >= d), steps(rows_im, lambda d: row >= d),
           jnp.concatenate(rows_re, axis=1), jnp.concatenate(rows_im, axis=1))
    rev = (steps(rows_re, lambda d: row < 8 - d), steps(neg_im, lambda d: row < 8 - d),
           jnp.concatenate(rows_re[::-1], axis=1), jnp.concatenate(neg_im[::-1], axis=1))
    return fwd, rev


_EYE8 = functools.partial(jnp.eye, 8, dtype=F32)


def _expand_in(b):
    return jnp.einsum("lkgpc,gh->lkgchp", b.reshape(DEPTH, 4, 8, P, C), _EYE8()).reshape(DEPTH, 4, 128, 512)


def _extract_in(e):
    return jnp.einsum("lkgchp,gh->lkgpc", e.reshape(DEPTH, 4, 8, C, 8, P), _EYE8()).reshape(DEPTH, G, P, C)


def _expand_out(c):
    return jnp.einsum("lkgcp,gh->lkgphc", c.reshape(DEPTH, 4, 8, C, P), _EYE8()).reshape(DEPTH, 4, 512, 128)


def _extract_out(e):
    return jnp.einsum("lkgphc,gh->lkgcp", e.reshape(DEPTH, 4, 8, P, 8, C), _EYE8()).reshape(DEPTH, G, C, P)


SMALL = ("norm_g", "b_in", "ssm_log_dt", "ssm_lam_re", "ssm_lam_im", "ssm_b_re", "ssm_b_im",
         "ssm_c_re", "ssm_c_im", "ssm_d", "ssm_b_glu", "pool_w", "pool_scale")
BIG = ("w_in", "ssm_w_glu", "w_branch_a", "w_branch_b", "w_out")


def _step(x, target, w, m, v, place):
    sp = {n: w[n] for n in SMALL}
    final_norm_g = w["final_norm_g"]
    wbuf = dict(zip(BIG, _cast_own(place, [w[n] for n in BIG])))
    (abar_re, abar_im, bbar_re, bbar_im), disc_vjp = jax.vjp(
        _discretize, *(sp[n] for n in ("ssm_log_dt", "ssm_lam_re", "ssm_lam_im", "ssm_b_re", "ssm_b_im")))
    powers_fwd, powers_rev = _powers(abar_re, abar_im)
    b_re_x, b_im_x = _expand_in(bbar_re), _expand_in(bbar_im)
    c_re_x, c_imn_x = _expand_out(sp["ssm_c_re"]), _expand_out(-sp["ssm_c_im"])
    b_x = jnp.concatenate([b_re_x, b_im_x], axis=3).astype(BF16)
    t = lambda a: jnp.swapaxes(a, 2, 3).astype(BF16)
    c_re_t, c_imn_t, b_re_t, b_im_t = t(c_re_x), t(c_imn_x), t(b_re_x), t(b_im_x)
    c_re_x, c_imn_x = c_re_x.astype(BF16), c_imn_x.astype(BF16)
    row = lambda n: sp[n].reshape(DEPTH, 1, -1)
    g, b_in, dsk, b_glu, scale = row("norm_g"), row("b_in"), row("ssm_d"), row("ssm_b_glu"), row("pool_scale")
    pw = sp["pool_w"].astype(BF16)

    saved = []
    for l in range(DEPTH):
        three = BIG[2:]
        if l == 0:
            wbuf["w_in"], wbuf["ssm_w_glu"] = _comm_only(
                "gather_first", _RingGatherJob([wbuf["w_in"], wbuf["ssm_w_glu"]], 0))[0]
            jobs = [_GatherJob([wbuf[n] for n in three], 0), _GatherJob([wbuf["ssm_w_glu"]], 1)]
        else:
            jobs = []
        (h, proj), res = _norm_inproj(l, x, g, wbuf["w_in"], b_in, jobs)
        if res:
            wbuf.update(zip(three, res[0][0]))
            (wbuf["ssm_w_glu"],) = res[1][0]
        wg = dict(wbuf, ssm_w_glu=wbuf["ssm_w_glu"].reshape(DEPTH, W, W), w_out=wbuf["w_out"].reshape(DEPTH, D, D))
        job = _GatherJob([wbuf["w_in"]], l + 1) if l + 1 < DEPTH else _GatherJob([wbuf[n] for n in three], l)
        (sre, sim, y1, q, ya, pooled, mixed, yb), res = _fused(
            f"branches_fwd_l{l}", (NT,),
            [_s5_fwd(l, proj, b_x, c_re_x, c_imn_x, powers_fwd, dsk, wg["ssm_w_glu"], b_glu),
             _pool_fwd(l, proj, pw, scale)], jobs=[job])
        if l + 1 < DEPTH:
            (wbuf["w_in"],) = res[0][0]
        else:
            wbuf.update(zip(three, res[0][0]))
        wg = dict(wbuf, ssm_w_glu=wbuf["ssm_w_glu"].reshape(DEPTH, W, W), w_out=wbuf["w_out"].reshape(DEPTH, D, D))
        pa, pb, mg, x_next = _merge_out(l, ya, yb, proj, x, wg["w_branch_a"], wg["w_branch_b"], wg["w_out"])
        saved.append(dict(x=x, h=h, proj=proj, sre=sre, sim=sim, y1=y1, q=q, ya=ya,
                          pooled=pooled, mixed=mixed, yb=yb, pa=pa, pb=pb, mg=mg))
        x = x_next

    loss, dx, dgf = _loss_head(x, final_norm_g.reshape(1, D), target)

    per_layer = {n: [None] * DEPTH for n in ("norm_g", "b_in", "ssm_d", "ssm_b_glu", "pool_w", "pool_scale",
                                             "dare", "daim", "dbre", "dbim", "dcre", "dcimn")}
    red = _Reducer(place, w, m, v)
    for l in reversed(range(DEPTH)):
        s = saved[l]
        (dproj, dya, dyb, dwo, dwa, dwb, dbias_g), res = _merge_out_bwd(
            l, dx, s["mg"], s["proj"], s["pa"], s["pb"], s["ya"], s["yb"],
            wg["w_out"], wg["w_branch_a"], wg["w_branch_b"], red.jobs())
        red.land(res)
        (dproj, dwg, dbg, dd, dcre, dcimn, dbre, dbim, dare, daim, dbias_a, dpw, dsc, dbias_b), res = _fused(
            f"branches_bwd_l{l}", (NT,),
            [_s5_bwd(l, dya, s["proj"], s["y1"], s["q"], s["sre"], s["sim"], c_re_t, c_imn_t, b_re_t, b_im_t,
                     *powers_rev, dsk, wg["ssm_w_glu"]),
             _pool_bwd(l, dyb, s["proj"], s["mixed"], s["pooled"], pw, scale)],
            shared=(dproj, _rows(4 * W, 0, True)), jobs=red.jobs())
        red.land(res)
        rest = [dwg.reshape(4, W // 4, W), dwa, dwb, dwo.reshape(4, D // 4, D)]
        if l == 0:
            red.add(l, "rest", BIG[1:], rest)
        if l == 0:
            for r in range(2):
                outs, res = _inproj_dw(l, r, s["h"], dproj, red.jobs())
                red.land(res)
                red.add(l, f"in{r}", BIG[:1], outs, [r * (D // 2)])
        else:
            dwin, res = _inproj_dw(l, 0, s["h"], dproj, red.jobs(), rows=D)
            red.land(res)
        (dx, dg), res = _inproj_dx(l, dproj, wg["w_in"], s["x"], g, dx, red.jobs())
        red.land(res)
        if l > 0:
            red.add(l, "all", BIG, dwin + rest)
        for n, a in (("norm_g", dg.reshape(D)), ("b_in", jnp.concatenate([dbias_a, dbias_b, dbias_g], axis=1).reshape(NIN)),
                     ("ssm_d", dd.reshape(W)), ("ssm_b_glu", dbg.reshape(W)), ("pool_w", dpw), ("pool_scale", dsc.reshape(W)),
                     ("dare", dare), ("daim", daim), ("dbre", dbre), ("dbim", dbim), ("dcre", dcre), ("dcimn", dcimn)):
            per_layer[n][l] = a
    gs = {n: jnp.stack(a) for n, a in per_layer.items()}
    d_abar = [jnp.sum(gs.pop(n), axis=1).reshape(DEPTH, G, P) for n in ("dare", "daim")]
    (gs["ssm_log_dt"], gs["ssm_lam_re"], gs["ssm_lam_im"], gs["ssm_b_re"], gs["ssm_b_im"]) = disc_vjp(
        (*d_abar, _extract_in(gs.pop("dbre")), _extract_in(gs.pop("dbim"))))
    gs["ssm_c_re"], gs["ssm_c_im"] = _extract_out(gs.pop("dcre")), -_extract_out(gs.pop("dcimn"))
    gs["final_norm_g"] = dgf

    natural = {n: w[n].shape for n in REPLICATED}
    rw, rm, rv = {}, {}, {}
    for n in REPLICATED:
        shape = DENSE.get(n, natural[n])
        gs[n], rw[n], rm[n], rv[n] = (a.reshape(shape) for a in (gs[n], w[n], m[n], v[n]))
    small = [gs[n] for n in REPLICATED] + [loss]
    jobs = red.jobs()
    res = _pcall(None, "tail_exchange", (), [], [], [], [], jobs=jobs + [_SiblingJob(small, False)])[1]
    red.land(res[:len(jobs)])
    pair_small = _small_pair_sum(place, small, res[-1][1], [BF16 if a.ndim > 2 else F32 for a in small])
    jobs = red.jobs()
    res = _pcall(None, "tail_gather", (), [], [], [], [], jobs=jobs + [_ChipGatherJob(pair_small)])[1]
    red.land(res[:len(jobs)])
    assert not red.active
    small_parts = dict(zip(REPLICATED + ("loss",), res[-1][0]))

    k = len(REPLICATED)
    outs = _adamw_small("adamw_small", [rw[n] for n in REPLICATED], [small_parts[n] for n in REPLICATED],
                        [rm[n] for n in REPLICATED], [rv[n] for n in REPLICATED], small_parts["loss"])
    results = {n: red.big[n] for n in BIG}
    results.update({n: [outs[1 + q * k + i].reshape(natural[n]) for q in range(4)] for i, n in enumerate(REPLICATED)})
    return outs[0][0, 0], dx, results


def _place():
    x, y, c = lax.axis_index("x"), lax.axis_index("y"), lax.axis_index("c")
    chips = [(1 - x, y), (x, 1 - y), (1 - x, 1 - y)]
    return x, y, c, 2 * x + y, chips, [2 * cx + cy for cx, cy in chips]


def _remote(src, dst, ssem, rsem, dev):
    return pltpu.make_async_remote_copy(src_ref=src, dst_ref=dst, send_sem=ssem, recv_sem=rsem,
                                        device_id=dev, device_id_type=MESH)


class _GatherJob:
    def __init__(self, bufs, l):
        self.srcs, self.bufs, self.news, self.l = [], list(bufs), [], l
        self.scratch = [pltpu.SemaphoreType.DMA((len(self.bufs), 3))] * 4

    def _half(self, ref, k, h):
        rows = ref.shape[2] // 2
        return ref.at[self.l, k, pl.ds(pl.multiple_of(h * rows, 8), rows), :]

    def _ici(self, bufs, sems, a, j, k):
        _, _, c, _, chips, _ = _place()
        blk = self._half(bufs[a], k, c)
        return _remote(blk, blk, sems[0].at[a, j], sems[1].at[a, j], (*chips[j], c))

    def _d2d(self, bufs, sems, a, j, k, h):
        x, y, c, _, _, _ = _place()
        blk = self._half(bufs[a], k, h)
        return _remote(blk, blk, sems[2].at[a, j], sems[3].at[a, j], (x, y, 1 - c))

    def start(self, srcs, bufs, news, sems):
        me = _place()[3]
        for a in range(len(self.bufs)):
            for j in range(3):
                self._ici(bufs, sems, a, j, me).start()

    def finish(self, srcs, bufs, news, sems):
        _, _, c, me, _, cid = _place()
        pairs = [(a, j) for a in range(len(self.bufs)) for j in range(3)]
        for a, j in pairs:
            self._ici(bufs, sems, a, j, cid[j]).wait_recv()
            self._d2d(bufs, sems, a, j, cid[j], c).start()
        for a, j in pairs:
            self._d2d(bufs, sems, a, j, cid[j], 1 - c).wait_recv()
        for a, j in pairs:
            self._ici(bufs, sems, a, j, me).wait_send()
            self._d2d(bufs, sems, a, j, cid[j], c).wait_send()


class _RingGatherJob(_GatherJob):
    def __init__(self, bufs, l):
        super().__init__(bufs, l)
        n = len(self.bufs)
        self.scratch = [pltpu.SemaphoreType.DMA((n, 2))] * 4 + [pltpu.SemaphoreType.DMA((n, 4))] * 2

    def _rows(self, ref, k, h, part):
        half = ref.shape[2] // 2
        start, rows = (h * half, half) if part is None else (h * half + part * (half // 2), half // 2)
        return ref.at[self.l, k, pl.ds(pl.multiple_of(start, 8), rows), :]

    def _to_chip(self, bufs, sems, base, a, j, k, part):
        _, _, c, _, chips, _ = _place()
        blk = self._rows(bufs[a], k, c, part)
        return _remote(blk, blk, sems[base].at[a, j], sems[base + 1].at[a, j], (*chips[j], c))

    def _to_sibling(self, bufs, sems, a, i, k, h, part):
        x, y, c, _, _, _ = _place()
        blk = self._rows(bufs[a], k, h, part)
        return _remote(blk, blk, sems[4].at[a, i], sems[5].at[a, i], (x, y, 1 - c))

    def start(self, srcs, bufs, news, sems):
        me = _place()[3]
        for a in range(len(self.bufs)):
            for j in range(2):
                self._to_chip(bufs, sems, 0, a, j, me, None).start()

    def finish(self, srcs, bufs, news, sems):
        _, _, c, me, _, cid = _place()
        arrays = range(len(self.bufs))
        for a in arrays:
            for j in (1, 0):
                self._to_chip(bufs, sems, 0, a, j, cid[j], None).wait_recv()
                self._to_chip(bufs, sems, 2, a, 1 - j, cid[j], 1 - j).start()
                self._to_sibling(bufs, sems, a, j, cid[j], c, None).start()
        for a in arrays:
            for part in range(2):
                self._to_chip(bufs, sems, 2, a, part, cid[2], part).wait_recv()
                self._to_sibling(bufs, sems, a, 2 + part, cid[2], c, part).start()
        for a in arrays:
            for j in range(2):
                self._to_sibling(bufs, sems, a, j, cid[j], 1 - c, None).wait_recv()
                self._to_sibling(bufs, sems, a, 2 + j, cid[2], 1 - c, j).wait_recv()
        for a in arrays:
            for j in range(2):
                self._to_chip(bufs, sems, 0, a, j, me, None).wait_send()
                self._to_chip(bufs, sems, 2, a, 1 - j, cid[j], 1 - j).wait_send()
                self._to_sibling(bufs, sems, a, j, cid[j], c, None).wait_send()
                self._to_sibling(bufs, sems, a, 2 + j, cid[2], c, j).wait_send()


class _SiblingJob:
    def __init__(self, srcs, rows_half):
        self.srcs, self.bufs, self.rows_half = list(srcs), [], rows_half
        self.news = [SDS((s.shape[0], s.shape[1] // 2, s.shape[2]) if rows_half else s.shape, s.dtype) for s in srcs]
        self.scratch = [pltpu.SemaphoreType.DMA((len(self.srcs),))] * 2

    def _copy(self, srcs, news, sems, a):
        x, y, c, _, _, _ = _place()
        src = srcs[a]
        if self.rows_half:
            rows = src.shape[1] // 2
            src = src.at[:, pl.ds(pl.multiple_of((1 - c) * rows, 8), rows), :]
        return _remote(src, news[a], sems[0].at[a], sems[1].at[a], (x, y, 1 - c))

    def start(self, srcs, bufs, news, sems):
        for a in range(len(self.srcs)):
            self._copy(srcs, news, sems, a).start()

    def finish(self, srcs, bufs, news, sems):
        for a in range(len(self.srcs)):
            self._copy(srcs, news, sems, a).wait()


class _ScatterJob:
    def __init__(self, parts):
        self.srcs, self.bufs = list(parts), []
        self.news = [SDS((3,) + p.shape[1:], p.dtype) for p in parts]
        self.scratch = [pltpu.SemaphoreType.DMA((len(self.srcs), 3))] * 2

    def _copy(self, srcs, news, sems, a, j):
        _, _, c, _, chips, cid = _place()
        return _remote(srcs[a].at[cid[j]], news[a].at[j], sems[0].at[a, j], sems[1].at[a, j], (*chips[j], c))

    def start(self, srcs, bufs, news, sems):
        for a in range(len(self.srcs)):
            for j in range(3):
                self._copy(srcs, news, sems, a, j).start()

    def finish(self, srcs, bufs, news, sems):
        for a in range(len(self.srcs)):
            for j in range(3):
                self._copy(srcs, news, sems, a, j).wait()


def _comm_only(name, job):
    return _pcall(None, name, (), [], [], [], [], jobs=[job])[1][0]


class _ChipGatherJob(_GatherJob):
    def __init__(self, bufs):
        super().__init__(bufs, None)

    def _half(self, ref, k, h):
        return ref.at[k, h]


def _cast_own(place, ws):
    n = len(ws)

    def body(p_ref, *refs):
        for i_ref, o_ref in zip(refs[:n], refs[n:]):
            o_ref[...] = i_ref[...].astype(BF16)

    return pl.pallas_call(
        body, name="cast_own_shards",
        grid_spec=pltpu.PrefetchScalarGridSpec(
            num_scalar_prefetch=1, grid=(DEPTH,),
            in_specs=[pl.BlockSpec((None,) + a.shape[1:], lambda l, p: (l, 0, 0)) for a in ws],
            out_specs=[pl.BlockSpec((None, None) + a.shape[1:], lambda l, p: (l, p[1], 0, 0)) for a in ws]),
        out_shape=[SDS((DEPTH, 4) + a.shape[1:], BF16) for a in ws],
        compiler_params=_params("arbitrary"),
    )(place, *ws)


def _half_tiles(a_):
    rows = a_ // 2
    ta = min(rows, 256)
    return rows, ta, rows // ta


def _pair_sums_bf16(name, place, owns, recvs):
    n = len(owns)

    def body(p_ref, *refs):
        for own_ref, recv_ref, out_ref in zip(refs[:n], refs[n:2 * n], refs[2 * n:]):
            out_ref[...] = (own_ref[...] + recv_ref[...]).astype(BF16)

    def own_half(a):
        return pl.BlockSpec((None, a.shape[1] // 2, a.shape[2]), lambda s, p: (s, p[0], 0))

    def block(a):
        return pl.BlockSpec((None,) + a.shape[1:], lambda s, p: (s, 0, 0))

    return pl.pallas_call(
        body, name=name,
        grid_spec=pltpu.PrefetchScalarGridSpec(
            num_scalar_prefetch=1, grid=(4,),
            in_specs=[own_half(a) for a in owns] + [block(r) for r in recvs],
            out_specs=[block(r) for r in recvs]),
        out_shape=[SDS(r.shape, BF16) for r in recvs],
        compiler_params=_params("arbitrary"),
    )(place, *owns, *recvs)


def _shard_sums(name, place, owns, recvs, rbufs):
    n = len(owns)

    def body(p_ref, *refs):
        for own_ref, recv_ref, r_ref, out_ref in zip(refs[:n], refs[n:2 * n], refs[2 * n:3 * n], refs[3 * n:]):
            acc = own_ref[...] + recv_ref[...]
            for j in range(3):
                acc = acc + r_ref[j].astype(F32)
            out_ref[...] = acc

    def own_half(a):
        return pl.BlockSpec((None, a.shape[1] // 2, a.shape[2]), lambda i, p: (p[1], p[0], 0))

    def recv_block(a):
        return pl.BlockSpec((None,) + a.shape[1:], lambda i, p: (p[1], 0, 0))

    return pl.pallas_call(
        body, name=name,
        grid_spec=pltpu.PrefetchScalarGridSpec(
            num_scalar_prefetch=1, grid=(1,),
            in_specs=([own_half(a) for a in owns] + [recv_block(r) for r in recvs]
                      + [pl.BlockSpec(rb.shape, lambda i, p: (0, 0, 0)) for rb in rbufs]),
            out_specs=[pl.BlockSpec(r.shape[1:], lambda i, p: (0, 0)) for r in recvs]),
        out_shape=[SDS(r.shape[1:], F32) for r in recvs],
        compiler_params=_params("arbitrary"),
    )(place, *owns, *recvs, *rbufs)


def _small_pair_sum(place, mine, recv, dtypes):
    n = len(mine)

    def body(p_ref, *refs):
        for m_ref, r_ref, o_ref in zip(refs[:n], refs[n:2 * n], refs[2 * n:]):
            o_ref[...] = (m_ref[...] + r_ref[...]).astype(o_ref.dtype)

    def whole(a):
        zeros = (0,) * a.ndim
        return pl.BlockSpec(a.shape, lambda i, p: zeros)

    def mine_blk(a):
        zeros = (0,) * a.ndim
        return pl.BlockSpec((None,) + a.shape, lambda i, p: (p[1],) + zeros)

    return pl.pallas_call(
        body, name="small_pair_sum",
        grid_spec=pltpu.PrefetchScalarGridSpec(
            num_scalar_prefetch=1, grid=(1,),
            in_specs=[whole(a) for a in mine] + [whole(a) for a in recv],
            out_specs=[mine_blk(a) for a in mine]),
        out_shape=[SDS((4,) + a.shape, dt) for a, dt in zip(mine, dtypes)],
        compiler_params=_params("arbitrary"),
    )(place, *mine, *recv)


def _adam_math(w, g, m, v):
    m = B1 * m + (1.0 - B1) * g
    v = B2 * v + (1.0 - B2) * (g * g)
    m_hat = m / (1.0 - B1 ** STEP)
    v_hat = v / (1.0 - B2 ** STEP)
    delta = -LR * (m_hat / (jnp.sqrt(v_hat) + EPS_A) + WD * w)
    return delta, m, v


def _adamw_big(name, l, row0, w, m, v, mine, other, prev, jobs=()):
    _, _, b_ = w.shape
    _, ta, nh = _half_tiles(2 * mine.shape[0])
    prev = list(prev or [])

    def body(w_ref, m_ref, v_ref, mine_ref, other_ref, *rest):
        g_ref, d_ref, mo_ref, vo_ref = rest[len(prev):]
        g = jnp.where(pl.program_id(0) == lax.axis_index("c"), mine_ref[...], other_ref[...])
        g_ref[...] = g
        d_ref[...], mo_ref[...], vo_ref[...] = _adam_math(w_ref[...], g, m_ref[...], v_ref[...])

    slab = pl.BlockSpec((None, ta, b_), lambda h, i: (l, row0 // ta + h * nh + i, 0))
    half = pl.BlockSpec((ta, b_), lambda h, i: (i, 0))
    outs, res = _pcall(
        body, name, (2, nh), [slab, slab, slab, half, half] + [_ANY] * len(prev), [slab] * 4, [SDS(w.shape, F32)] * 4,
        (w, m, v, mine, other, *prev), aliases={5 + k: k for k in range(len(prev))}, jobs=jobs)
    return outs, res


def _adamw_small(name, ws, parts, ms, vs, loss_parts=None):
    k = len(ws)
    extra = [] if loss_parts is None else [loss_parts]

    def chip_sum(p_ref):
        p = [p_ref[k].astype(F32) for k in range(4)]
        return ((p[0] + p[1]) + p[2]) + p[3]

    def body(*refs):
        w_refs, p_refs, m_refs, v_refs = refs[:k], refs[k:2 * k], refs[2 * k:3 * k], refs[3 * k:4 * k]
        outs = refs[4 * k + len(extra):]
        if extra:
            outs[0][...] = chip_sum(refs[4 * k])
            outs = outs[1:]
        for a in range(k):
            g = chip_sum(p_refs[a])
            outs[a][...] = g
            outs[k + a][...], outs[2 * k + a][...], outs[3 * k + a][...] = _adam_math(
                w_refs[a][...], g, m_refs[a][...], v_refs[a][...])

    like = [SDS(a.shape, F32) for a in ws]
    return pl.pallas_call(
        body, name=name,
        out_shape=([SDS(loss_parts.shape[1:], F32)] if extra else []) + like * 4,
        compiler_params=pltpu.CompilerParams(vmem_limit_bytes=VMEM_LIMIT),
    )(*ws, *parts, *ms, *vs, *extra)


class _Reducer:
    def __init__(self, place, w, m, v):
        self.place, self.w, self.m, self.v = place, w, m, v
        self.active, self.riding = [], []
        self.big = {n: None for n in BIG}

    def add(self, l, tag, names, own, row0s=None):
        self.active.append(dict(l=l, key=f"{tag}_l{l}", names=names, own=list(own), row0s=row0s or [0] * len(names),
                                stage=0))

    def jobs(self):
        self.riding = list(self.active)
        return [(_SiblingJob(g["own"], True), _ScatterJob(g.get("parts", [])), _SiblingJob(g.get("shard", []), False))
                [g["stage"]] for g in self.riding]

    def land(self, res):
        for g, (_, news) in zip(self.riding, res):
            if g["stage"] == 0:
                g["recv"] = news
                g["parts"] = _pair_sums_bf16(f"pair_sums_{g['key']}", self.place, g["own"], news)
            elif g["stage"] == 1:
                g["shard"] = _shard_sums(f"shard_sums_{g['key']}", self.place, g["own"], g["recv"], news)
            else:
                for n, mine, other, row0 in zip(g["names"], g["shard"], news, g["row0s"]):
                    self.big[n] = _adamw_big(f"adamw_{n}_{row0}_{g['key']}", g["l"], row0, self.w[n], self.m[n], self.v[n],
                                             mine, other, self.big[n])[0]
                self.active.remove(g)
            g["stage"] += 1
        self.riding = []


WEIGHTS = ("norm_g", "w_in", "b_in", "ssm_log_dt", "ssm_lam_re", "ssm_lam_im", "ssm_b_re", "ssm_b_im", "ssm_c_re",
           "ssm_c_im", "ssm_d", "ssm_w_glu", "ssm_b_glu", "pool_w", "pool_scale", "w_branch_a", "w_branch_b", "w_out",
           "final_norm_g")
REPLICATED = SMALL + ("final_norm_g",)
DENSE = {"ssm_b_re": (DEPTH, G, P * C), "ssm_b_im": (DEPTH, G, P * C), "final_norm_g": (2, D // 2)}


def kernel(x, norm_g, w_in, b_in, ssm_log_dt, ssm_lam_re, ssm_lam_im, ssm_b_re, ssm_b_im, ssm_c_re, ssm_c_im, ssm_d, ssm_w_glu, ssm_b_glu, pool_w, pool_scale, w_branch_a, w_branch_b, w_out, final_norm_g, loss_target, m_norm_g, m_w_in, m_b_in, m_ssm_log_dt, m_ssm_lam_re, m_ssm_lam_im, m_ssm_b_re, m_ssm_b_im, m_ssm_c_re, m_ssm_c_im, m_ssm_d, m_ssm_w_glu, m_ssm_b_glu, m_pool_w, m_pool_scale, m_w_branch_a, m_w_branch_b, m_w_out, m_final_norm_g, v_norm_g, v_w_in, v_b_in, v_ssm_log_dt, v_ssm_lam_re, v_ssm_lam_im, v_ssm_b_re, v_ssm_b_im, v_ssm_c_re, v_ssm_c_im, v_ssm_d, v_ssm_w_glu, v_ssm_b_glu, v_pool_w, v_pool_scale, v_w_branch_a, v_w_branch_b, v_w_out, v_final_norm_g):
    w = dict(zip(WEIGHTS, (norm_g, w_in, b_in, ssm_log_dt, ssm_lam_re, ssm_lam_im, ssm_b_re, ssm_b_im, ssm_c_re,
                           ssm_c_im, ssm_d, ssm_w_glu, ssm_b_glu, pool_w, pool_scale, w_branch_a, w_branch_b, w_out,
                           final_norm_g)))
    m = dict(zip(WEIGHTS, (m_norm_g, m_w_in, m_b_in, m_ssm_log_dt, m_ssm_lam_re, m_ssm_lam_im, m_ssm_b_re, m_ssm_b_im,
                           m_ssm_c_re, m_ssm_c_im, m_ssm_d, m_ssm_w_glu, m_ssm_b_glu, m_pool_w, m_pool_scale,
                           m_w_branch_a, m_w_branch_b, m_w_out, m_final_norm_g)))
    v = dict(zip(WEIGHTS, (v_norm_g, v_w_in, v_b_in, v_ssm_log_dt, v_ssm_lam_re, v_ssm_lam_im, v_ssm_b_re, v_ssm_b_im,
                           v_ssm_c_re, v_ssm_c_im, v_ssm_d, v_ssm_w_glu, v_ssm_b_glu, v_pool_w, v_pool_scale,
                           v_w_branch_a, v_w_branch_b, v_w_out, v_final_norm_g)))
    place = jnp.stack([lax.axis_index("c"), 2 * lax.axis_index("x") + lax.axis_index("y")]).astype(jnp.int32)

    total_loss, dx, results = _step(x[0], loss_target[0], w, m, v, place)
    return (total_loss, dx[None], *[results[n][q] for q in range(4) for n in WEIGHTS])
```

```python
import functools

import jax
import jax.numpy as jnp
from jax import lax
from jax.experimental import pallas as pl
from jax.experimental.pallas import tpu as pltpu

F32, BF16 = jnp.float32, jnp.bfloat16
SDS = jax.ShapeDtypeStruct
MESH = pl.DeviceIdType.MESH

DEPTH = 2
L = 2048
D = 1024
NIN = 4096
W = 512
G, P, C = 32, 64, 16
GP = G * P
WINS = (2, 4, 8, 16)
TM = 256
NT = L // TM
TMM = 512
TK = 1024
LAST_ROWS = 256
EPS = 1e-6
VMEM_LIMIT = 56 * 2**20

LR, B1, B2, EPS_A, WD, STEP = 0.001, 0.9, 0.999, 1e-08, 0.01, 10


def _params(*sem):
    return pltpu.CompilerParams(dimension_semantics=sem, vmem_limit_bytes=VMEM_LIMIT)


_ANY = pl.BlockSpec(memory_space=pl.ANY)


def _full(shape):
    zeros = (0,) * len(shape)
    return pl.BlockSpec(shape, lambda *_: zeros)


def _layer(l, shape):
    zeros = (0,) * len(shape)
    return pl.BlockSpec((None,) + shape, lambda *_: (l,) + zeros)


def _rows(width, col=0, reverse=False, tm=TM):
    if reverse:
        return pl.BlockSpec((tm, width), lambda i: (L // tm - 1 - i, col))
    return pl.BlockSpec((tm, width), lambda i: (i, col))


def _rows_mm(width, col=0):
    return _rows(width, col, False, TMM)


def _pcall(body, name, grid, in_specs, out_specs, out_shape, args, scratch=(), aliases=None, jobs=()):
    in_specs, out_specs, out_shape, args, scratch = list(in_specs), list(out_specs), list(out_shape), list(args), list(scratch)
    aliases = dict(aliases or {})
    jobs = [j for j in jobs if j is not None]
    n_in, n_out, n_scr = len(in_specs), len(out_specs), len(scratch)
    srcs = [s for j in jobs for s in j.srcs]
    bufs = [b for j in jobs for b in j.bufs]
    news = [s for j in jobs for s in j.news]
    aliases.update({n_in + len(srcs) + k: n_out + k for k in range(len(bufs))})

    def hosted(*refs):
        cuts = [n_in, len(srcs), len(bufs), n_out, len(bufs), len(news), n_scr]
        parts, p = [], 0
        for n in cuts:
            parts.append(refs[p:p + n])
            p += n
        ins, src_r, _, outs, buf_r, new_r, scr = parts
        sem_r = refs[p:]
        views, ps, pb, pn, pm = [], 0, 0, 0, 0
        for j in jobs:
            views.append((src_r[ps:ps + len(j.srcs)], buf_r[pb:pb + len(j.bufs)], new_r[pn:pn + len(j.news)],
                          sem_r[pm:pm + len(j.scratch)]))
            ps, pb, pn, pm = ps + len(j.srcs), pb + len(j.bufs), pn + len(j.news), pm + len(j.scratch)

        def run(phase):
            for j, v in zip(jobs, views):
                getattr(j, phase)(*v)

        def at_step(step):
            return functools.reduce(jnp.logical_and, [pl.program_id(d) == step(d) for d in range(len(grid))])

        if not grid:
            run("start")
            run("finish")
            return
        pl.when(at_step(lambda d: 0))(lambda: run("start"))
        body(*ins, *outs, *scr)
        pl.when(at_step(lambda d: grid[d] - 1))(lambda: run("finish"))

    outs = pl.pallas_call(
        hosted if jobs else body, name=name, **({"grid": grid} if grid else {}),
        in_specs=in_specs + [_ANY] * (len(srcs) + len(bufs)), out_specs=out_specs + [_ANY] * (len(bufs) + len(news)),
        out_shape=out_shape + [SDS(b.shape, b.dtype) for b in bufs] + news,
        input_output_aliases=aliases, scratch_shapes=scratch + [s for j in jobs for s in j.scratch],
        compiler_params=_params(*(("arbitrary",) * len(grid))))(*args, *srcs, *bufs)
    res, pb, pn = [], n_out, n_out + len(bufs)
    for j in jobs:
        res.append((list(outs[pb:pb + len(j.bufs)]), list(outs[pn:pn + len(j.news)])))
        pb, pn = pb + len(j.bufs), pn + len(j.news)
    return list(outs[:n_out]), res


def _fused(name, grid, parts, shared=None, jobs=()):
    def body(*refs):
        pos = [0]

        def take(n):
            pos[0] += n
            return refs[pos[0] - n:pos[0]]

        ins = [take(len(p["in_specs"])) for p in parts]
        if shared:
            take(1)
            block = take(1)[0]
        outs = [take(len(p["out_specs"])) for p in parts]
        scr = [take(len(p["scratch"])) for p in parts]
        col = 0
        for p, i, o, s in zip(parts, ins, outs, scr):
            view = []
            if shared:
                view = [block.at[:, pl.ds(col, p["width"])]]
                col += p["width"]
            p["body"](*i, *view, *o, *s)

    in_specs = [s for p in parts for s in p["in_specs"]] + ([_ANY] if shared else [])
    out_specs = ([shared[1]] if shared else []) + [s for p in parts for s in p["out_specs"]]
    out_shape = ([SDS(shared[0].shape, shared[0].dtype)] if shared else []) + [s for p in parts for s in p["out_shape"]]
    args = [a for p in parts for a in p["args"]] + ([shared[0]] if shared else [])
    return _pcall(body, name, grid, in_specs, out_specs, out_shape, args, [s for p in parts for s in p["scratch"]],
                  {len(in_specs) - 1: 0} if shared else None, jobs)


def _dot(a, b):
    return jnp.dot(a, b, preferred_element_type=F32)


def _dot_nt(a, b):
    return lax.dot_general(a, b, (((1,), (1,)), ((), ())), preferred_element_type=F32)


def _dot_tn(a, b):
    return lax.dot_general(a, b, (((0,), (0,)), ((), ())), preferred_element_type=F32)


_K0 = 0.7978845608028654
_K1 = 0.044715


def _gelu(x):
    return 0.5 * x * (1.0 + jnp.tanh(_K0 * (x + _K1 * (x * x * x))))


def _gelu_grad(x):
    t = jnp.tanh(_K0 * (x + _K1 * (x * x * x)))
    return 0.5 * (1.0 + t) + 0.5 * x * (1.0 - t * t) * (_K0 * (1.0 + 3.0 * _K1 * x * x))


def _sigmoid(x):
    return jax.nn.sigmoid(x)


def _norm_inproj(l, x, g, w, b, jobs=()):
    def body(x_ref, g_ref, w_ref, b_ref, h_ref, proj_ref):
        xv = x_ref[...]
        r = lax.rsqrt(jnp.mean(xv * xv, axis=-1, keepdims=True) + EPS)
        hb = ((xv * r) * g_ref[...]).astype(BF16)
        h_ref[...] = hb
        for j in range(4):
            cs = slice(j * 1024, (j + 1) * 1024)
            proj_ref[:, cs] = _dot(hb, w_ref[j]) + b_ref[:, cs]

    return _pcall(
        body, f"norm_inproj_l{l}", (L // TMM,),
        [_rows_mm(D), _layer(l, (1, D)), _layer(l, (4, D, 1024)), _layer(l, (1, NIN))],
        [_rows_mm(D), _rows_mm(NIN)],
        [SDS((L, D), BF16), SDS((L, NIN), F32)],
        (x, g, w, b), jobs=jobs)


def _scan_tile(re_ref, im_ref, st_re, st_im, cr_re, cr_im, carry, reverse):
    def chunk(ci, carry):
        c = (TM // 8 - 1 - ci) if reverse else ci
        rows = pl.ds(pl.multiple_of(c * 8, 8), 8)
        new = []
        for lb in range(GP // 512):
            cols = slice(lb * 512, (lb + 1) * 512)
            vr = re_ref[rows, cols]
            vi = im_ref[rows, cols]
            for s, d in enumerate((1, 2, 4)):
                ar = st_re[8 * s:8 * s + 8, cols]
                ai = st_im[8 * s:8 * s + 8, cols]
                sr = pltpu.roll(vr, 8 - d if reverse else d, 0)
                si = pltpu.roll(vi, 8 - d if reverse else d, 0)
                vr, vi = vr + ar * sr - ai * si, vi + ar * si + ai * sr
            cr, ci_ = carry[2 * lb], carry[2 * lb + 1]
            pr = cr_re[:, cols]
            pi = cr_im[:, cols]
            vr, vi = vr + pr * cr - pi * ci_, vi + pr * ci_ + pi * cr
            re_ref[rows, cols] = vr
            im_ref[rows, cols] = vi
            if reverse:
                new += [vr[0:1], vi[0:1]]
            else:
                new += [vr[7:8], vi[7:8]]
        return tuple(new)

    return lax.fori_loop(0, TM // 8, chunk, carry)


def _load_carry(car_ref):
    return tuple(car_ref[r:r + 1, lb * 512:(lb + 1) * 512] for lb in range(GP // 512) for r in (0, 1))


def _store_carry(car_ref, carry):
    for lb in range(GP // 512):
        car_ref[0:1, lb * 512:(lb + 1) * 512] = carry[2 * lb]
        car_ref[1:2, lb * 512:(lb + 1) * 512] = carry[2 * lb + 1]


def _s5_fwd(l, proj, bexp, cre, cimn, powers, dsk, wglu, bglu):
    def body(ua_ref, za_ref, bexp_ref, cre_ref, cimn_ref, st_re_ref, st_im_ref, cr_re_ref, cr_im_ref,
             d_ref, wg_ref, bg_ref, sre_ref, sim_ref, y1_ref, q_ref, ya_ref, car_ref):
        @pl.when(pl.program_id(0) == 0)
        def _():
            car_ref[...] = jnp.zeros_like(car_ref)

        u = ua_ref[...]
        ub = u.astype(BF16)
        for k in range(4):
            bu = _dot(ub[:, 128 * k:128 * (k + 1)], bexp_ref[k])
            sre_ref[:, 512 * k:512 * (k + 1)] = bu[:, :512]
            sim_ref[:, 512 * k:512 * (k + 1)] = bu[:, 512:]
        carry = _scan_tile(sre_ref, sim_ref, st_re_ref, st_im_ref, cr_re_ref, cr_im_ref, _load_carry(car_ref), False)
        _store_carry(car_ref, carry)
        for k in range(4):
            blk = slice(512 * k, 512 * (k + 1))
            ks = slice(128 * k, 128 * (k + 1))
            y0 = _dot(sre_ref[:, blk].astype(BF16), cre_ref[k]) + _dot(sim_ref[:, blk].astype(BF16), cimn_ref[k])
            y1_ref[:, ks] = y0 + d_ref[:, ks] * u[:, ks]
        y2 = _gelu(y1_ref[...])
        q = _dot(y2.astype(BF16), wg_ref[...]) + bg_ref[...]
        q_ref[...] = q
        za = za_ref[...]
        ya_ref[...] = ((y2 * _sigmoid(q)) * (za * _sigmoid(za))).astype(BF16)

    return dict(
        body=body,
        in_specs=[_rows(W, 0), _rows(W, 1), _layer(l, (4, 128, 1024)), _layer(l, (4, 512, 128)),
                  _layer(l, (4, 512, 128)), _layer(l, (24, GP)), _layer(l, (24, GP)), _layer(l, (8, GP)),
                  _layer(l, (8, GP)), _layer(l, (1, W)), _layer(l, (W, W)), _layer(l, (1, W))],
        out_specs=[_rows(GP), _rows(GP), _rows(W), _rows(W), _rows(W)],
        out_shape=[SDS((L, GP), F32), SDS((L, GP), F32), SDS((L, W), F32), SDS((L, W), F32), SDS((L, W), BF16)],
        args=(proj, proj, bexp, cre, cimn, *powers, dsk, wglu, bglu),
        scratch=[pltpu.VMEM((8, GP), F32)])


def _pool_fwd(l, proj, pw, scale):
    def body(ub_ref, zb_ref, pw_ref, sc_ref, pooled_ref, mixed_ref, yb_ref, buf):
        i = pl.program_id(0)

        @pl.when(i == 0)
        def _():
            buf[0:16, :] = jnp.zeros((16, W), F32)

        u = ub_ref[...]
        buf[16:16 + TM, :] = u
        t = i * TM + lax.broadcasted_iota(jnp.int32, (TM, 128), 0)
        for gi, win in enumerate(WINS):
            cs = slice(128 * gi, 128 * (gi + 1))
            acc = u[:, cs]
            for k in range(1, win):
                acc = acc + buf[16 - k:16 - k + TM, cs]
            cnt = jnp.minimum(t + 1, win).astype(F32)
            pb = (acc / cnt - u[:, cs]).astype(BF16)
            pooled_ref[:, cs] = pb
            mixed_ref[:, cs] = _dot(pb, pw_ref[gi])
        zb = zb_ref[...]
        yb_ref[...] = ((mixed_ref[...] * sc_ref[...]) * (zb * _sigmoid(zb))).astype(BF16)
        buf[0:16, :] = buf[TM:TM + 16, :]

    return dict(
        body=body,
        in_specs=[_rows(W, 2), _rows(W, 3), _layer(l, (4, 128, 128)), _layer(l, (1, W))],
        out_specs=[_rows(W), _rows(W), _rows(W)],
        out_shape=[SDS((L, W), BF16), SDS((L, W), F32), SDS((L, W), BF16)],
        args=(proj, proj, pw, scale),
        scratch=[pltpu.VMEM((TM + 16, W), F32)])


def _merge_out(l, ya, yb, proj, x, wa, wb, wo):
    def body(ya_ref, yb_ref, ga_ref, gb_ref, x_ref, wa_ref, wb_ref, wo_ref, pa_ref, pb_ref, mg_ref, xo_ref):
        ya = ya_ref[...]
        yb = yb_ref[...]
        for j in range(4):
            cs = slice(256 * j, 256 * (j + 1))
            pa_ref[:, cs] = _dot(ya, wa_ref[j])
            pb_ref[:, cs] = _dot(yb, wb_ref[j])
        merged = _sigmoid(ga_ref[...]) * pa_ref[...] + _sigmoid(gb_ref[...]) * pb_ref[...]
        mb = merged.astype(BF16)
        mg_ref[...] = mb
        xo_ref[...] = x_ref[...] + _dot(mb, wo_ref[...])

    return pl.pallas_call(
        body, name=f"merge_out_l{l}", grid=(L // TMM,),
        in_specs=[_rows_mm(W), _rows_mm(W), _rows_mm(D, 2), _rows_mm(D, 3), _rows_mm(D),
                  _layer(l, (4, W, 256)), _layer(l, (4, W, 256)), _layer(l, (D, D))],
        out_specs=[_rows_mm(D), _rows_mm(D), _rows_mm(D), _rows_mm(D)],
        out_shape=[SDS((L, D), F32), SDS((L, D), F32), SDS((L, D), BF16), SDS((L, D), F32)],
        compiler_params=_params("arbitrary"),
    )(ya, yb, proj, proj, x, wa, wb, wo)


def _loss_head(x, gf, target):
    def body(x_ref, g_ref, t_ref, loss_ref, dx_ref, dg_ref):
        @pl.when(pl.program_id(0) == 0)
        def _():
            loss_ref[...] = jnp.zeros_like(loss_ref)
            dg_ref[...] = jnp.zeros_like(dg_ref)

        xv = x_ref[...]
        g = g_ref[...]
        r = lax.rsqrt(jnp.mean(xv * xv, axis=-1, keepdims=True) + EPS)
        xn = xv * r
        err = xn * g - t_ref[...]
        part = jnp.sum(jnp.mean(err * err, axis=-1, keepdims=True), axis=0, keepdims=True)
        loss_ref[...] += 0.5 * part
        dy = err * (1.0 / D)
        dg_ref[...] += jnp.sum(dy * xn, axis=0, keepdims=True)
        dxn = dy * g
        dx_ref[...] = r * (dxn - xn * jnp.mean(dxn * xn, axis=-1, keepdims=True))

    return pl.pallas_call(
        body, name="loss_head", grid=(NT,),
        in_specs=[_rows(D), _full((1, D)), _rows(D)],
        out_specs=[_full((2, 128)), _rows(D), _full((1, D))],
        out_shape=[SDS((2, 128), F32), SDS((L, D), F32), SDS((1, D), F32)],
        compiler_params=_params("arbitrary"),
    )(x, gf, target)


def _merge_out_bwd(l, dxn, mg, proj, pa, pb, ya, yb, wo, wa, wb, jobs=()):
    def body(dx_ref, mg_ref, ga_ref, gb_ref, pa_ref, pb_ref, ya_ref, yb_ref, wo_ref, wa_ref, wb_ref,
             dg_ref, dya_ref, dyb_ref, dwo_ref, dwa_ref, dwb_ref, dbias_ref):
        @pl.when(pl.program_id(0) == 0)
        def _():
            for ref in (dwo_ref, dwa_ref, dwb_ref, dbias_ref):
                ref[...] = jnp.zeros_like(ref)

        dxb = dx_ref[...].astype(BF16)
        dm = _dot_nt(dxb, wo_ref[...])
        sa = _sigmoid(ga_ref[...])
        sb = _sigmoid(gb_ref[...])
        dga = dm * pa_ref[...] * (sa * (1.0 - sa))
        dgb = dm * pb_ref[...] * (sb * (1.0 - sb))
        dg_ref[:, :D] = dga.astype(BF16)
        dg_ref[:, D:] = dgb.astype(BF16)
        dbias_ref[:, :D] += jnp.sum(dga, axis=0, keepdims=True)
        dbias_ref[:, D:] += jnp.sum(dgb, axis=0, keepdims=True)
        dpa = (dm * sa).astype(BF16)
        dpb = (dm * sb).astype(BF16)
        ya = ya_ref[...]
        yb = yb_ref[...]
        dya = jnp.zeros((TM, W), F32)
        dyb = jnp.zeros((TM, W), F32)
        for j in range(4):
            cs = slice(256 * j, 256 * (j + 1))
            dya = dya + _dot_nt(dpa[:, cs], wa_ref[j])
            dyb = dyb + _dot_nt(dpb[:, cs], wb_ref[j])
            dwa_ref[j] += _dot_tn(ya, dpa[:, cs])
            dwb_ref[j] += _dot_tn(yb, dpb[:, cs])
        dya_ref[...] = dya
        dyb_ref[...] = dyb
        dwo_ref[...] += _dot_tn(mg_ref[...], dxb)

    return _pcall(
        body, f"merge_out_bwd_l{l}", (NT,),
        [_rows(D), _rows(D), _rows(D, 2), _rows(D, 3), _rows(D), _rows(D), _rows(W), _rows(W),
         _layer(l, (D, D)), _layer(l, (4, W, 256)), _layer(l, (4, W, 256))],
        [_rows(2 * D, 1), _rows(W), _rows(W), _full((D, D)), _full((4, W, 256)), _full((4, W, 256)), _full((1, 2 * D))],
        [SDS((L, NIN), BF16), SDS((L, W), F32), SDS((L, W), F32),
         SDS((D, D), F32), SDS((4, W, 256), F32), SDS((4, W, 256), F32), SDS((1, 2 * D), F32)],
        (dxn, mg, proj, proj, pa, pb, ya, yb, wo, wa, wb), jobs=jobs)


def _pool_bwd(l, dyb, proj, mixed, pooled, pw, scale):
    def body(dyb_ref, zb_ref, mixed_ref, pooled_ref, pw_ref, sc_ref, db_ref, dpw_ref, dsc_ref, dbias_ref, buf):
        i = pl.program_id(0)
        tile = NT - 1 - i

        @pl.when(i == 0)
        def _():
            dpw_ref[...] = jnp.zeros_like(dpw_ref)
            dsc_ref[...] = jnp.zeros_like(dsc_ref)
            dbias_ref[...] = jnp.zeros_like(dbias_ref)
            buf[TM:TM + 16, :] = jnp.zeros((16, W), F32)

        dyb = dyb_ref[...]
        zb = zb_ref[...]
        mixed = mixed_ref[...]
        sc = sc_ref[...]
        sg = _sigmoid(zb)
        dyb0 = dyb * (zb * sg)
        dzb = dyb * (mixed * sc) * (sg * (1.0 + zb * (1.0 - sg)))
        db_ref[:, W:] = dzb.astype(BF16)
        dbias_ref[:, W:] += jnp.sum(dzb, axis=0, keepdims=True)
        dsc_ref[...] += jnp.sum(dyb0 * mixed, axis=0, keepdims=True)
        dmix = (dyb0 * sc).astype(BF16)
        t = tile * TM + lax.broadcasted_iota(jnp.int32, (TM, 128), 0)
        for gi, win in enumerate(WINS):
            cs = slice(128 * gi, 128 * (gi + 1))
            dpw_ref[gi] += _dot_tn(pooled_ref[:, cs], dmix[:, cs])
            dpool = _dot_nt(dmix[:, cs], pw_ref[gi])
            cnt = jnp.minimum(t + 1, win).astype(F32)
            e = dpool / cnt
            buf[0:TM, cs] = e
            acc = e - dpool
            for k in range(1, win):
                acc = acc + buf[k:k + TM, cs]
            db_ref[:, cs] = acc.astype(BF16)
            dbias_ref[:, cs] += jnp.sum(acc, axis=0, keepdims=True)
        buf[TM:TM + 16, :] = buf[0:16, :]

    return dict(
        body=body, width=2 * W,
        in_specs=[_rows(W, 0, True), _rows(W, 3, True), _rows(W, 0, True), _rows(W, 0, True),
                  _layer(l, (4, 128, 128)), _layer(l, (1, W))],
        out_specs=[_full((4, 128, 128)), _full((1, W)), _full((1, 2 * W))],
        out_shape=[SDS((4, 128, 128), F32), SDS((1, W), F32), SDS((1, 2 * W), F32)],
        args=(dyb, proj, mixed, pooled, pw, scale),
        scratch=[pltpu.VMEM((TM + 16, W), F32)])


def _s5_bwd(l, dya, proj, y1, q, sre, sim, cret, cimnt, bret, bimt, st_re, st_im, cr_re, cr_im, dsk, wglu):
    def halo(i):
        return (jnp.maximum((NT - 1 - i) * (TM // 8) - 1, 0), 0)

    def body(dya_ref, ua_ref, za_ref, y1_ref, q_ref, sre_ref, sim_ref, hre_ref, him_ref,
             cret_ref, cimnt_ref, bret_ref, bimt_ref, st_re_ref, st_im_ref, cr_re_ref, cr_im_ref, d_ref, wg_ref,
             da_ref, dwg_ref, dbg_ref, dd_ref, dcre_ref, dcimn_ref, dbre_ref, dbim_ref, dare_ref, daim_ref, dbias_ref,
             lre, lim, car_ref):
        i = pl.program_id(0)
        tile = NT - 1 - i

        @pl.when(i == 0)
        def _():
            for ref in (dwg_ref, dbg_ref, dd_ref, dcre_ref, dcimn_ref, dbre_ref, dbim_ref, dare_ref, daim_ref, dbias_ref,
                        car_ref):
                ref[...] = jnp.zeros_like(ref)

        u = ua_ref[...]
        za = za_ref[...]
        y1 = y1_ref[...]
        dya = dya_ref[...]
        y2 = _gelu(y1)
        sg = _sigmoid(q_ref[...])
        sgz = _sigmoid(za)
        dy3 = dya * (za * sgz)
        dza = dya * (y2 * sg) * (sgz * (1.0 + za * (1.0 - sgz)))
        da_ref[:, W:] = dza.astype(BF16)
        dbias_ref[:, W:] += jnp.sum(dza, axis=0, keepdims=True)
        dq = dy3 * y2 * (sg * (1.0 - sg))
        dqb = dq.astype(BF16)
        dy2 = dy3 * sg + _dot_nt(dqb, wg_ref[...])
        dwg_ref[...] += _dot_tn(y2.astype(BF16), dqb)
        dbg_ref[...] += jnp.sum(dq, axis=0, keepdims=True)
        dy1 = dy2 * _gelu_grad(y1)
        dd_ref[...] += jnp.sum(dy1 * u, axis=0, keepdims=True)
        dy1b = dy1.astype(BF16)
        ub = u.astype(BF16)
        for k in range(4):
            blk = slice(512 * k, 512 * (k + 1))
            ks = slice(128 * k, 128 * (k + 1))
            lre[:, blk] = _dot(dy1b[:, ks], cret_ref[k])
            lim[:, blk] = _dot(dy1b[:, ks], cimnt_ref[k])
            dcre_ref[k] += _dot_tn(sre_ref[:, blk].astype(BF16), dy1b[:, ks])
            dcimn_ref[k] += _dot_tn(sim_ref[:, blk].astype(BF16), dy1b[:, ks])
        carry = _scan_tile(lre, lim, st_re_ref, st_im_ref, cr_re_ref, cr_im_ref, _load_carry(car_ref), True)
        _store_carry(car_ref, carry)

        rowid = lax.broadcasted_iota(jnp.int32, (8, 512), 0)
        gate = (tile > 0).astype(F32)

        def chunk(c, _):
            rows = pl.ds(pl.multiple_of(c * 8, 8), 8)
            prows = pl.ds(pl.multiple_of(jnp.maximum(c - 1, 0) * 8, 8), 8)
            for lb in range(GP // 512):
                cols = slice(lb * 512, (lb + 1) * 512)
                sr = sre_ref[rows, cols]
                si = sim_ref[rows, cols]
                pr = jnp.where(c == 0, hre_ref[7:8, cols] * gate, sre_ref[prows, cols][7:8])
                pi = jnp.where(c == 0, him_ref[7:8, cols] * gate, sim_ref[prows, cols][7:8])
                sr = jnp.where(rowid == 0, pr, pltpu.roll(sr, 1, 0))
                si = jnp.where(rowid == 0, pi, pltpu.roll(si, 1, 0))
                lr = lre[rows, cols]
                li = lim[rows, cols]
                dare_ref[:, cols] += sr * lr + si * li
                daim_ref[:, cols] += sr * li - si * lr
            return 0

        lax.fori_loop(0, TM // 8, chunk, 0)

        for k in range(4):
            blk = slice(512 * k, 512 * (k + 1))
            ks = slice(128 * k, 128 * (k + 1))
            lrb = lre[:, blk].astype(BF16)
            lib = lim[:, blk].astype(BF16)
            du = dy1[:, ks] * d_ref[:, ks] + _dot(lrb, bret_ref[k]) + _dot(lib, bimt_ref[k])
            da_ref[:, ks] = du.astype(BF16)
            dbias_ref[:, ks] += jnp.sum(du, axis=0, keepdims=True)
            dbre_ref[k] += _dot_tn(ub[:, ks], lrb)
            dbim_ref[k] += _dot_tn(ub[:, ks], lib)

    return dict(
        body=body, width=2 * W,
        in_specs=[_rows(W, 0, True), _rows(W, 0, True), _rows(W, 1, True), _rows(W, 0, True), _rows(W, 0, True),
                  _rows(GP, 0, True), _rows(GP, 0, True),
                  pl.BlockSpec((8, GP), halo), pl.BlockSpec((8, GP), halo),
                  _layer(l, (4, 128, 512)), _layer(l, (4, 128, 512)), _layer(l, (4, 512, 128)), _layer(l, (4, 512, 128)),
                  _layer(l, (24, GP)), _layer(l, (24, GP)), _layer(l, (8, GP)), _layer(l, (8, GP)), _layer(l, (1, W)),
                  _layer(l, (W, W))],
        out_specs=[_full((W, W)), _full((1, W)), _full((1, W)),
                   _full((4, 512, 128)), _full((4, 512, 128)), _full((4, 128, 512)), _full((4, 128, 512)),
                   _full((8, GP)), _full((8, GP)), _full((1, 2 * W))],
        out_shape=[SDS((W, W), F32), SDS((1, W), F32), SDS((1, W), F32),
                   SDS((4, 512, 128), F32), SDS((4, 512, 128), F32), SDS((4, 128, 512), F32), SDS((4, 128, 512), F32),
                   SDS((8, GP), F32), SDS((8, GP), F32), SDS((1, 2 * W), F32)],
        args=(dya, proj, proj, y1, q, sre, sim, sre, sim, cret, cimnt, bret, bimt, st_re, st_im, cr_re, cr_im, dsk,
              wglu),
        scratch=[pltpu.VMEM((TM, GP), F32), pltpu.VMEM((TM, GP), F32), pltpu.VMEM((8, GP), F32)])


def _inproj_dw(l, r, h, dproj, jobs=(), rows=D // 2):
    def body(h_ref, dp_ref, dw_ref):
        part = _dot_tn(h_ref[...], dp_ref[...])

        @pl.when(pl.program_id(1) == 0)
        def _():
            dw_ref[...] = part

        @pl.when(pl.program_id(1) > 0)
        def _():
            dw_ref[...] += part

    return _pcall(
        body, f"inproj_dw{r}_l{l}", (4, L // TK),
        [pl.BlockSpec((TK, rows), lambda j, i: (i, r)), pl.BlockSpec((TK, 1024), lambda j, i: (i, j))],
        [pl.BlockSpec((None, rows, 1024), lambda j, i: (j, 0, 0))],
        [SDS((4, rows, 1024), F32)],
        (h, dproj), jobs=jobs)


def _inproj_dx(l, dproj, w, x, g, dxn, jobs=()):
    def body(dp_ref, w_ref, x_ref, g_ref, dxn_ref, dx_ref, dg_ref):
        @pl.when(pl.program_id(0) == 0)
        def _():
            dg_ref[...] = jnp.zeros_like(dg_ref)

        dh = _dot_nt(dp_ref[:, 0:1024], w_ref[0])
        for j in range(1, 4):
            dh = dh + _dot_nt(dp_ref[:, j * 1024:(j + 1) * 1024], w_ref[j])
        xv = x_ref[...]
        r = lax.rsqrt(jnp.mean(xv * xv, axis=-1, keepdims=True) + EPS)
        xn = xv * r
        dg_ref[...] += jnp.sum(dh * xn, axis=0, keepdims=True)
        dn = dh * g_ref[...]
        dx_ref[...] = dxn_ref[...] + r * (dn - xn * jnp.mean(dn * xn, axis=-1, keepdims=True))

    return _pcall(
        body, f"inproj_dx_l{l}", (L // TMM,),
        [_rows_mm(NIN), _layer(l, (4, D, 1024)), _rows_mm(D), _layer(l, (1, D)), _rows_mm(D)],
        [_rows_mm(D), _full((1, D))],
        [SDS((L, D), F32), SDS((1, D), F32)],
        (dproj, w, x, g, dxn), jobs=jobs)


def _discretize(log_dt, lam_re, lam_im, b_re, b_im):
    dt = jnp.exp(log_dt)[..., None]
    mag = jnp.exp(lam_re * dt)
    ang = lam_im * dt
    abar_re = mag * jnp.cos(ang)
    abar_im = mag * jnp.sin(ang)
    num_re = abar_re - 1.0
    num_im = abar_im
    den = lam_re * lam_re + lam_im * lam_im
    coef_re = (num_re * lam_re + num_im * lam_im) / den
    coef_im = (num_im * lam_re - num_re * lam_im) / den
    bbar_re = coef_re[..., None] * b_re - coef_im[..., None] * b_im
    bbar_im = coef_re[..., None] * b_im + coef_im[..., None] * b_re
    return abar_re, abar_im, bbar_re, bbar_im


def _powers(abar_re, abar_im):
    ar, ai = abar_re.reshape(DEPTH, 1, GP), abar_im.reshape(DEPTH, 1, GP)
    rows_re, rows_im = [ar], [ai]
    for _ in range(7):
        pr, pi = rows_re[-1], rows_im[-1]
        rows_re.append(pr * ar - pi * ai)
        rows_im.append(pr * ai + pi * ar)
    row = jnp.arange(8)[:, None]

    def steps(rows, keep):
        return jnp.concatenate([jnp.where(keep(d), rows[d - 1], 0.0) for d in (1, 2, 4)], axis=1)

    neg_im = [-r for r in rows_im]
    fwd = (steps(rows_re, lambda d: row >= d), steps(rows_im, lambda d: row >= d),
           jnp.concatenate(rows_re, axis=1), jnp.concatenate(rows_im, axis=1))
    rev = (steps(rows_re, lambda d: row < 8 - d), steps(neg_im, lambda d: row < 8 - d),
           jnp.concatenate(rows_re[::-1], axis=1), jnp.concatenate(neg_im[::-1], axis=1))
    return fwd, rev


_EYE8 = functools.partial(jnp.eye, 8, dtype=F32)


def _expand_in(b):
    return jnp.einsum("lkgpc,gh->lkgchp", b.reshape(DEPTH, 4, 8, P, C), _EYE8()).reshape(DEPTH, 4, 128, 512)


def _extract_in(e):
    return jnp.einsum("lkgchp,gh->lkgpc", e.reshape(DEPTH, 4, 8, C, 8, P), _EYE8()).reshape(DEPTH, G, P, C)


def _expand_out(c):
    return jnp.einsum("lkgcp,gh->lkgphc", c.reshape(DEPTH, 4, 8, C, P), _EYE8()).reshape(DEPTH, 4, 512, 128)


def _extract_out(e):
    return jnp.einsum("lkgphc,gh->lkgcp", e.reshape(DEPTH, 4, 8, P, 8, C), _EYE8()).reshape(DEPTH, G, C, P)


SMALL = ("norm_g", "b_in", "ssm_log_dt", "ssm_lam_re", "ssm_lam_im", "ssm_b_re", "ssm_b_im",
         "ssm_c_re", "ssm_c_im", "ssm_d", "ssm_b_glu", "pool_w", "pool_scale")
BIG = ("w_in", "ssm_w_glu", "w_branch_a", "w_branch_b", "w_out")


def _step(x, target, w, m, v, place):
    sp = {n: w[n] for n in SMALL}
    final_norm_g = w["final_norm_g"]
    wbuf = dict(zip(BIG, _cast_own(place, [w[n] for n in BIG])))
    (abar_re, abar_im, bbar_re, bbar_im), disc_vjp = jax.vjp(
        _discretize, *(sp[n] for n in ("ssm_log_dt", "ssm_lam_re", "ssm_lam_im", "ssm_b_re", "ssm_b_im")))
    powers_fwd, powers_rev = _powers(abar_re, abar_im)
    b_re_x, b_im_x = _expand_in(bbar_re), _expand_in(bbar_im)
    c_re_x, c_imn_x = _expand_out(sp["ssm_c_re"]), _expand_out(-sp["ssm_c_im"])
    b_x = jnp.concatenate([b_re_x, b_im_x], axis=3).astype(BF16)
    t = lambda a: jnp.swapaxes(a, 2, 3).astype(BF16)
    c_re_t, c_imn_t, b_re_t, b_im_t = t(c_re_x), t(c_imn_x), t(b_re_x), t(b_im_x)
    c_re_x, c_imn_x = c_re_x.astype(BF16), c_imn_x.astype(BF16)
    row = lambda n: sp[n].reshape(DEPTH, 1, -1)
    g, b_in, dsk, b_glu, scale = row("norm_g"), row("b_in"), row("ssm_d"), row("ssm_b_glu"), row("pool_scale")
    pw = sp["pool_w"].astype(BF16)

    saved = []
    for l in range(DEPTH):
        three = BIG[2:]
        if l == 0:
            wbuf["w_in"], wbuf["ssm_w_glu"] = _comm_only(
                "gather_first", _RingGatherJob([wbuf["w_in"], wbuf["ssm_w_glu"]], 0))[0]
            jobs = [_GatherJob([wbuf[n] for n in three], 0), _GatherJob([wbuf["ssm_w_glu"]], 1)]
        else:
            jobs = []
        (h, proj), res = _norm_inproj(l, x, g, wbuf["w_in"], b_in, jobs)
        if res:
            wbuf.update(zip(three, res[0][0]))
            (wbuf["ssm_w_glu"],) = res[1][0]
        wg = dict(wbuf, ssm_w_glu=wbuf["ssm_w_glu"].reshape(DEPTH, W, W), w_out=wbuf["w_out"].reshape(DEPTH, D, D))
        job = _GatherJob([wbuf["w_in"]], l + 1) if l + 1 < DEPTH else _GatherJob([wbuf[n] for n in three], l)
        (sre, sim, y1, q, ya, pooled, mixed, yb), res = _fused(
            f"branches_fwd_l{l}", (NT,),
            [_s5_fwd(l, proj, b_x, c_re_x, c_imn_x, powers_fwd, dsk, wg["ssm_w_glu"], b_glu),
             _pool_fwd(l, proj, pw, scale)], jobs=[job])
        if l + 1 < DEPTH:
            (wbuf["w_in"],) = res[0][0]
        else:
            wbuf.update(zip(three, res[0][0]))
        wg = dict(wbuf, ssm_w_glu=wbuf["ssm_w_glu"].reshape(DEPTH, W, W), w_out=wbuf["w_out"].reshape(DEPTH, D, D))
        pa, pb, mg, x_next = _merge_out(l, ya, yb, proj, x, wg["w_branch_a"], wg["w_branch_b"], wg["w_out"])
        saved.append(dict(x=x, h=h, proj=proj, sre=sre, sim=sim, y1=y1, q=q, ya=ya,
                          pooled=pooled, mixed=mixed, yb=yb, pa=pa, pb=pb, mg=mg))
        x = x_next

    loss, dx, dgf = _loss_head(x, final_norm_g.reshape(1, D), target)

    per_layer = {n: [None] * DEPTH for n in ("norm_g", "b_in", "ssm_d", "ssm_b_glu", "pool_w", "pool_scale",
                                             "dare", "daim", "dbre", "dbim", "dcre", "dcimn")}
    red = _Reducer(place, w, m, v)
    for l in reversed(range(DEPTH)):
        s = saved[l]
        (dproj, dya, dyb, dwo, dwa, dwb, dbias_g), res = _merge_out_bwd(
            l, dx, s["mg"], s["proj"], s["pa"], s["pb"], s["ya"], s["yb"],
            wg["w_out"], wg["w_branch_a"], wg["w_branch_b"], red.jobs())
        red.land(res)
        (dproj, dwg, dbg, dd, dcre, dcimn, dbre, dbim, dare, daim, dbias_a, dpw, dsc, dbias_b), res = _fused(
            f"branches_bwd_l{l}", (NT,),
            [_s5_bwd(l, dya, s["proj"], s["y1"], s["q"], s["sre"], s["sim"], c_re_t, c_imn_t, b_re_t, b_im_t,
                     *powers_rev, dsk, wg["ssm_w_glu"]),
             _pool_bwd(l, dyb, s["proj"], s["mixed"], s["pooled"], pw, scale)],
            shared=(dproj, _rows(4 * W, 0, True)), jobs=red.jobs())
        red.land(res)
        rest = [dwg.reshape(4, W // 4, W), dwa, dwb, dwo.reshape(4, D // 4, D)]
        if l == 0:
            red.add(l, "rest", BIG[1:], rest)
        if l == 0:
            for r, rows in ((0, D - LAST_ROWS), (D // LAST_ROWS - 1, LAST_ROWS)):
                outs, res = _inproj_dw(l, r, s["h"], dproj, red.jobs(), rows=rows)
                red.land(res)
                red.add(l, f"in{r}", BIG[:1], outs, [r * rows])
        else:
            dwin, res = _inproj_dw(l, 0, s["h"], dproj, red.jobs(), rows=D)
            red.land(res)
        (dx, dg), res = _inproj_dx(l, dproj, wg["w_in"], s["x"], g, dx, red.jobs())
        red.land(res)
        if l > 0:
            red.add(l, "all", BIG, dwin + rest)
        for n, a in (("norm_g", dg.reshape(D)), ("b_in", jnp.concatenate([dbias_a, dbias_b, dbias_g], axis=1).reshape(NIN)),
                     ("ssm_d", dd.reshape(W)), ("ssm_b_glu", dbg.reshape(W)), ("pool_w", dpw), ("pool_scale", dsc.reshape(W)),
                     ("dare", dare), ("daim", daim), ("dbre", dbre), ("dbim", dbim), ("dcre", dcre), ("dcimn", dcimn)):
            per_layer[n][l] = a
    gs = {n: jnp.stack(a) for n, a in per_layer.items()}
    d_abar = [jnp.sum(gs.pop(n), axis=1).reshape(DEPTH, G, P) for n in ("dare", "daim")]
    (gs["ssm_log_dt"], gs["ssm_lam_re"], gs["ssm_lam_im"], gs["ssm_b_re"], gs["ssm_b_im"]) = disc_vjp(
        (*d_abar, _extract_in(gs.pop("dbre")), _extract_in(gs.pop("dbim"))))
    gs["ssm_c_re"], gs["ssm_c_im"] = _extract_out(gs.pop("dcre")), -_extract_out(gs.pop("dcimn"))
    gs["final_norm_g"] = dgf

    natural = {n: w[n].shape for n in REPLICATED}
    rw, rm, rv = {}, {}, {}
    for n in REPLICATED:
        shape = DENSE.get(n, natural[n])
        gs[n], rw[n], rm[n], rv[n] = (a.reshape(shape) for a in (gs[n], w[n], m[n], v[n]))
    small = [gs[n] for n in REPLICATED] + [loss]
    jobs = red.jobs()
    res = _pcall(None, "tail_exchange", (), [], [], [], [], jobs=jobs + [_SiblingJob(small, False)])[1]
    red.land(res[:len(jobs)])
    pair_small = _small_pair_sum(place, small, res[-1][1], [BF16 if a.ndim > 2 else F32 for a in small])
    jobs = red.jobs()
    res = _pcall(None, "tail_gather", (), [], [], [], [], jobs=jobs + [_ChipGatherJob(pair_small)])[1]
    red.land(res[:len(jobs)])
    assert not red.active
    small_parts = dict(zip(REPLICATED + ("loss",), res[-1][0]))

    k = len(REPLICATED)
    outs = _adamw_small("adamw_small", [rw[n] for n in REPLICATED], [small_parts[n] for n in REPLICATED],
                        [rm[n] for n in REPLICATED], [rv[n] for n in REPLICATED], small_parts["loss"])
    results = {n: red.big[n] for n in BIG}
    results.update({n: [outs[1 + q * k + i].reshape(natural[n]) for q in range(4)] for i, n in enumerate(REPLICATED)})
    return outs[0][0, 0], dx, results


def _place():
    x, y, c = lax.axis_index("x"), lax.axis_index("y"), lax.axis_index("c")
    chips = [(1 - x, y), (x, 1 - y), (1 - x, 1 - y)]
    return x, y, c, 2 * x + y, chips, [2 * cx + cy for cx, cy in chips]


def _remote(src, dst, ssem, rsem, dev):
    return pltpu.make_async_remote_copy(src_ref=src, dst_ref=dst, send_sem=ssem, recv_sem=rsem,
                                        device_id=dev, device_id_type=MESH)


class _GatherJob:
    def __init__(self, bufs, l):
        self.srcs, self.bufs, self.news, self.l = [], list(bufs), [], l
        self.scratch = [pltpu.SemaphoreType.DMA((len(self.bufs), 3))] * 4

    def _half(self, ref, k, h):
        rows = ref.shape[2] // 2
        return ref.at[self.l, k, pl.ds(pl.multiple_of(h * rows, 8), rows), :]

    def _ici(self, bufs, sems, a, j, k):
        _, _, c, _, chips, _ = _place()
        blk = self._half(bufs[a], k, c)
        return _remote(blk, blk, sems[0].at[a, j], sems[1].at[a, j], (*chips[j], c))

    def _d2d(self, bufs, sems, a, j, k, h):
        x, y, c, _, _, _ = _place()
        blk = self._half(bufs[a], k, h)
        return _remote(blk, blk, sems[2].at[a, j], sems[3].at[a, j], (x, y, 1 - c))

    def start(self, srcs, bufs, news, sems):
        me = _place()[3]
        for a in range(len(self.bufs)):
            for j in range(3):
                self._ici(bufs, sems, a, j, me).start()

    def finish(self, srcs, bufs, news, sems):
        _, _, c, me, _, cid = _place()
        pairs = [(a, j) for a in range(len(self.bufs)) for j in range(3)]
        for a, j in pairs:
            self._ici(bufs, sems, a, j, cid[j]).wait_recv()
            self._d2d(bufs, sems, a, j, cid[j], c).start()
        for a, j in pairs:
            self._d2d(bufs, sems, a, j, cid[j], 1 - c).wait_recv()
        for a, j in pairs:
            self._ici(bufs, sems, a, j, me).wait_send()
            self._d2d(bufs, sems, a, j, cid[j], c).wait_send()


class _RingGatherJob(_GatherJob):
    def __init__(self, bufs, l):
        super().__init__(bufs, l)
        n = len(self.bufs)
        self.scratch = [pltpu.SemaphoreType.DMA((n, 2))] * 4 + [pltpu.SemaphoreType.DMA((n, 4))] * 2

    def _rows(self, ref, k, h, part):
        half = ref.shape[2] // 2
        start, rows = (h * half, half) if part is None else (h * half + part * (half // 2), half // 2)
        return ref.at[self.l, k, pl.ds(pl.multiple_of(start, 8), rows), :]

    def _to_chip(self, bufs, sems, base, a, j, k, part):
        _, _, c, _, chips, _ = _place()
        blk = self._rows(bufs[a], k, c, part)
        return _remote(blk, blk, sems[base].at[a, j], sems[base + 1].at[a, j], (*chips[j], c))

    def _to_sibling(self, bufs, sems, a, i, k, h, part):
        x, y, c, _, _, _ = _place()
        blk = self._rows(bufs[a], k, h, part)
        return _remote(blk, blk, sems[4].at[a, i], sems[5].at[a, i], (x, y, 1 - c))

    def start(self, srcs, bufs, news, sems):
        me = _place()[3]
        for a in range(len(self.bufs)):
            for j in range(2):
                self._to_chip(bufs, sems, 0, a, j, me, None).start()

    def finish(self, srcs, bufs, news, sems):
        _, _, c, me, _, cid = _place()
        arrays = range(len(self.bufs))
        for a in arrays:
            for j in (1, 0):
                self._to_chip(bufs, sems, 0, a, j, cid[j], None).wait_recv()
                self._to_chip(bufs, sems, 2, a, 1 - j, cid[j], 1 - j).start()
                self._to_sibling(bufs, sems, a, j, cid[j], c, None).start()
        for a in arrays:
            for part in range(2):
                self._to_chip(bufs, sems, 2, a, part, cid[2], part).wait_recv()
                self._to_sibling(bufs, sems, a, 2 + part, cid[2], c, part).start()
        for a in arrays:
            for j in range(2):
                self._to_sibling(bufs, sems, a, j, cid[j], 1 - c, None).wait_recv()
                self._to_sibling(bufs, sems, a, 2 + j, cid[2], 1 - c, j).wait_recv()
        for a in arrays:
            for j in range(2):
                self._to_chip(bufs, sems, 0, a, j, me, None).wait_send()
                self._to_chip(bufs, sems, 2, a, 1 - j, cid[j], 1 - j).wait_send()
                self._to_sibling(bufs, sems, a, j, cid[j], c, None).wait_send()
                self._to_sibling(bufs, sems, a, 2 + j, cid[2], c, j).wait_send()


class _SiblingJob:
    def __init__(self, srcs, rows_half):
        self.srcs, self.bufs, self.rows_half = list(srcs), [], rows_half
        self.news = [SDS((s.shape[0], s.shape[1] // 2, s.shape[2]) if rows_half else s.shape, s.dtype) for s in srcs]
        self.scratch = [pltpu.SemaphoreType.DMA((len(self.srcs),))] * 2

    def _copy(self, srcs, news, sems, a):
        x, y, c, _, _, _ = _place()
        src = srcs[a]
        if self.rows_half:
            rows = src.shape[1] // 2
            src = src.at[:, pl.ds(pl.multiple_of((1 - c) * rows, 8), rows), :]
        return _remote(src, news[a], sems[0].at[a], sems[1].at[a], (x, y, 1 - c))

    def start(self, srcs, bufs, news, sems):
        for a in range(len(self.srcs)):
            self._copy(srcs, news, sems, a).start()

    def finish(self, srcs, bufs, news, sems):
        for a in range(len(self.srcs)):
            self._copy(srcs, news, sems, a).wait()


class _ScatterJob:
    def __init__(self, parts):
        self.srcs, self.bufs = list(parts), []
        self.news = [SDS((3,) + p.shape[1:], p.dtype) for p in parts]
        self.scratch = [pltpu.SemaphoreType.DMA((len(self.srcs), 3))] * 2

    def _copy(self, srcs, news, sems, a, j):
        _, _, c, _, chips, cid = _place()
        return _remote(srcs[a].at[cid[j]], news[a].at[j], sems[0].at[a, j], sems[1].at[a, j], (*chips[j], c))

    def start(self, srcs, bufs, news, sems):
        for a in range(len(self.srcs)):
            for j in range(3):
                self._copy(srcs, news, sems, a, j).start()

    def finish(self, srcs, bufs, news, sems):
        for a in range(len(self.srcs)):
            for j in range(3):
                self._copy(srcs, news, sems, a, j).wait()


def _comm_only(name, job):
    return _pcall(None, name, (), [], [], [], [], jobs=[job])[1][0]


class _ChipGatherJob(_GatherJob):
    def __init__(self, bufs):
        super().__init__(bufs, None)

    def _half(self, ref, k, h):
        return ref.at[k, h]


def _cast_own(place, ws):
    n = len(ws)

    def body(p_ref, *refs):
        for i_ref, o_ref in zip(refs[:n], refs[n:]):
            o_ref[...] = i_ref[...].astype(BF16)

    return pl.pallas_call(
        body, name="cast_own_shards",
        grid_spec=pltpu.PrefetchScalarGridSpec(
            num_scalar_prefetch=1, grid=(DEPTH,),
            in_specs=[pl.BlockSpec((None,) + a.shape[1:], lambda l, p: (l, 0, 0)) for a in ws],
            out_specs=[pl.BlockSpec((None, None) + a.shape[1:], lambda l, p: (l, p[1], 0, 0)) for a in ws]),
        out_shape=[SDS((DEPTH, 4) + a.shape[1:], BF16) for a in ws],
        compiler_params=_params("arbitrary"),
    )(place, *ws)


def _half_tiles(a_):
    rows = a_ // 2
    ta = next(t for t in (256, 128, 64, 32, 16, 8) if rows % t == 0)
    return rows, ta, rows // ta


def _pair_sums_bf16(name, place, owns, recvs):
    n = len(owns)

    def body(p_ref, *refs):
        for own_ref, recv_ref, out_ref in zip(refs[:n], refs[n:2 * n], refs[2 * n:]):
            out_ref[...] = (own_ref[...] + recv_ref[...]).astype(BF16)

    def own_half(a):
        return pl.BlockSpec((None, a.shape[1] // 2, a.shape[2]), lambda s, p: (s, p[0], 0))

    def block(a):
        return pl.BlockSpec((None,) + a.shape[1:], lambda s, p: (s, 0, 0))

    return pl.pallas_call(
        body, name=name,
        grid_spec=pltpu.PrefetchScalarGridSpec(
            num_scalar_prefetch=1, grid=(4,),
            in_specs=[own_half(a) for a in owns] + [block(r) for r in recvs],
            out_specs=[block(r) for r in recvs]),
        out_shape=[SDS(r.shape, BF16) for r in recvs],
        compiler_params=_params("arbitrary"),
    )(place, *owns, *recvs)


def _shard_sums(name, place, owns, recvs, rbufs):
    n = len(owns)

    def body(p_ref, *refs):
        for own_ref, recv_ref, r_ref, out_ref in zip(refs[:n], refs[n:2 * n], refs[2 * n:3 * n], refs[3 * n:]):
            acc = own_ref[...] + recv_ref[...]
            for j in range(3):
                acc = acc + r_ref[j].astype(F32)
            out_ref[...] = acc

    def own_half(a):
        return pl.BlockSpec((None, a.shape[1] // 2, a.shape[2]), lambda i, p: (p[1], p[0], 0))

    def recv_block(a):
        return pl.BlockSpec((None,) + a.shape[1:], lambda i, p: (p[1], 0, 0))

    return pl.pallas_call(
        body, name=name,
        grid_spec=pltpu.PrefetchScalarGridSpec(
            num_scalar_prefetch=1, grid=(1,),
            in_specs=([own_half(a) for a in owns] + [recv_block(r) for r in recvs]
                      + [pl.BlockSpec(rb.shape, lambda i, p: (0, 0, 0)) for rb in rbufs]),
            out_specs=[pl.BlockSpec(r.shape[1:], lambda i, p: (0, 0)) for r in recvs]),
        out_shape=[SDS(r.shape[1:], F32) for r in recvs],
        compiler_params=_params("arbitrary"),
    )(place, *owns, *recvs, *rbufs)


def _small_pair_sum(place, mine, recv, dtypes):
    n = len(mine)

    def body(p_ref, *refs):
        for m_ref, r_ref, o_ref in zip(refs[:n], refs[n:2 * n], refs[2 * n:]):
            o_ref[...] = (m_ref[...] + r_ref[...]).astype(o_ref.dtype)

    def whole(a):
        zeros = (0,) * a.ndim
        return pl.BlockSpec(a.shape, lambda i, p: zeros)

    def mine_blk(a):
        zeros = (0,) * a.ndim
        return pl.BlockSpec((None,) + a.shape, lambda i, p: (p[1],) + zeros)

    return pl.pallas_call(
        body, name="small_pair_sum",
        grid_spec=pltpu.PrefetchScalarGridSpec(
            num_scalar_prefetch=1, grid=(1,),
            in_specs=[whole(a) for a in mine] + [whole(a) for a in recv],
            out_specs=[mine_blk(a) for a in mine]),
        out_shape=[SDS((4,) + a.shape, dt) for a, dt in zip(mine, dtypes)],
        compiler_params=_params("arbitrary"),
    )(place, *mine, *recv)


def _adam_math(w, g, m, v):
    m = B1 * m + (1.0 - B1) * g
    v = B2 * v + (1.0 - B2) * (g * g)
    m_hat = m / (1.0 - B1 ** STEP)
    v_hat = v / (1.0 - B2 ** STEP)
    delta = -LR * (m_hat / (jnp.sqrt(v_hat) + EPS_A) + WD * w)
    return delta, m, v


def _adamw_big(name, l, row0, w, m, v, mine, other, prev, jobs=()):
    _, _, b_ = w.shape
    _, ta, nh = _half_tiles(2 * mine.shape[0])
    prev = list(prev or [])

    def body(w_ref, m_ref, v_ref, mine_ref, other_ref, *rest):
        g_ref, d_ref, mo_ref, vo_ref = rest[len(prev):]
        g = jnp.where(pl.program_id(0) == lax.axis_index("c"), mine_ref[...], other_ref[...])
        g_ref[...] = g
        d_ref[...], mo_ref[...], vo_ref[...] = _adam_math(w_ref[...], g, m_ref[...], v_ref[...])

    slab = pl.BlockSpec((None, ta, b_), lambda h, i: (l, row0 // ta + h * nh + i, 0))
    half = pl.BlockSpec((ta, b_), lambda h, i: (i, 0))
    outs, res = _pcall(
        body, name, (2, nh), [slab, slab, slab, half, half] + [_ANY] * len(prev), [slab] * 4, [SDS(w.shape, F32)] * 4,
        (w, m, v, mine, other, *prev), aliases={5 + k: k for k in range(len(prev))}, jobs=jobs)
    return outs, res


def _adamw_small(name, ws, parts, ms, vs, loss_parts=None):
    k = len(ws)
    extra = [] if loss_parts is None else [loss_parts]

    def chip_sum(p_ref):
        p = [p_ref[k].astype(F32) for k in range(4)]
        return ((p[0] + p[1]) + p[2]) + p[3]

    def body(*refs):
        w_refs, p_refs, m_refs, v_refs = refs[:k], refs[k:2 * k], refs[2 * k:3 * k], refs[3 * k:4 * k]
        outs = refs[4 * k + len(extra):]
        if extra:
            outs[0][...] = chip_sum(refs[4 * k])
            outs = outs[1:]
        for a in range(k):
            g = chip_sum(p_refs[a])
            outs[a][...] = g
            outs[k + a][...], outs[2 * k + a][...], outs[3 * k + a][...] = _adam_math(
                w_refs[a][...], g, m_refs[a][...], v_refs[a][...])

    like = [SDS(a.shape, F32) for a in ws]
    return pl.pallas_call(
        body, name=name,
        out_shape=([SDS(loss_parts.shape[1:], F32)] if extra else []) + like * 4,
        compiler_params=pltpu.CompilerParams(vmem_limit_bytes=VMEM_LIMIT),
    )(*ws, *parts, *ms, *vs, *extra)


class _Reducer:
    def __init__(self, place, w, m, v):
        self.place, self.w, self.m, self.v = place, w, m, v
        self.active, self.riding = [], []
        self.big = {n: None for n in BIG}

    def add(self, l, tag, names, own, row0s=None):
        self.active.append(dict(l=l, key=f"{tag}_l{l}", names=names, own=list(own), row0s=row0s or [0] * len(names),
                                stage=0))

    def jobs(self):
        self.riding = list(self.active)
        return [(_SiblingJob(g["own"], True), _ScatterJob(g.get("parts", [])), _SiblingJob(g.get("shard", []), False))
                [g["stage"]] for g in self.riding]

    def land(self, res):
        for g, (_, news) in zip(self.riding, res):
            if g["stage"] == 0:
                g["recv"] = news
                g["parts"] = _pair_sums_bf16(f"pair_sums_{g['key']}", self.place, g["own"], news)
            elif g["stage"] == 1:
                g["shard"] = _shard_sums(f"shard_sums_{g['key']}", self.place, g["own"], g["recv"], news)
            else:
                for n, mine, other, row0 in zip(g["names"], g["shard"], news, g["row0s"]):
                    self.big[n] = _adamw_big(f"adamw_{n}_{row0}_{g['key']}", g["l"], row0, self.w[n], self.m[n], self.v[n],
                                             mine, other, self.big[n])[0]
                self.active.remove(g)
            g["stage"] += 1
        self.riding = []


WEIGHTS = ("norm_g", "w_in", "b_in", "ssm_log_dt", "ssm_lam_re", "ssm_lam_im", "ssm_b_re", "ssm_b_im", "ssm_c_re",
           "ssm_c_im", "ssm_d", "ssm_w_glu", "ssm_b_glu", "pool_w", "pool_scale", "w_branch_a", "w_branch_b", "w_out",
           "final_norm_g")
REPLICATED = SMALL + ("final_norm_g",)
DENSE = {"ssm_b_re": (DEPTH, G, P * C), "ssm_b_im": (DEPTH, G, P * C), "final_norm_g": (2, D // 2)}


def kernel(x, norm_g, w_in, b_in, ssm_log_dt, ssm_lam_re, ssm_lam_im, ssm_b_re, ssm_b_im, ssm_c_re, ssm_c_im, ssm_d, ssm_w_glu, ssm_b_glu, pool_w, pool_scale, w_branch_a, w_branch_b, w_out, final_norm_g, loss_target, m_norm_g, m_w_in, m_b_in, m_ssm_log_dt, m_ssm_lam_re, m_ssm_lam_im, m_ssm_b_re, m_ssm_b_im, m_ssm_c_re, m_ssm_c_im, m_ssm_d, m_ssm_w_glu, m_ssm_b_glu, m_pool_w, m_pool_scale, m_w_branch_a, m_w_branch_b, m_w_out, m_final_norm_g, v_norm_g, v_w_in, v_b_in, v_ssm_log_dt, v_ssm_lam_re, v_ssm_lam_im, v_ssm_b_re, v_ssm_b_im, v_ssm_c_re, v_ssm_c_im, v_ssm_d, v_ssm_w_glu, v_ssm_b_glu, v_pool_w, v_pool_scale, v_w_branch_a, v_w_branch_b, v_w_out, v_final_norm_g):
    w = dict(zip(WEIGHTS, (norm_g, w_in, b_in, ssm_log_dt, ssm_lam_re, ssm_lam_im, ssm_b_re, ssm_b_im, ssm_c_re,
                           ssm_c_im, ssm_d, ssm_w_glu, ssm_b_glu, pool_w, pool_scale, w_branch_a, w_branch_b, w_out,
                           final_norm_g)))
    m = dict(zip(WEIGHTS, (m_norm_g, m_w_in, m_b_in, m_ssm_log_dt, m_ssm_lam_re, m_ssm_lam_im, m_ssm_b_re, m_ssm_b_im,
                           m_ssm_c_re, m_ssm_c_im, m_ssm_d, m_ssm_w_glu, m_ssm_b_glu, m_pool_w, m_pool_scale,
                           m_w_branch_a, m_w_branch_b, m_w_out, m_final_norm_g)))
    v = dict(zip(WEIGHTS, (v_norm_g, v_w_in, v_b_in, v_ssm_log_dt, v_ssm_lam_re, v_ssm_lam_im, v_ssm_b_re, v_ssm_b_im,
                           v_ssm_c_re, v_ssm_c_im, v_ssm_d, v_ssm_w_glu, v_ssm_b_glu, v_pool_w, v_pool_scale,
                           v_w_branch_a, v_w_branch_b, v_w_out, v_final_norm_g)))
    place = jnp.stack([lax.axis_index("c"), 2 * lax.axis_index("x") + lax.axis_index("y")]).astype(jnp.int32)

    total_loss, dx, results = _step(x[0], loss_target[0], w, m, v, place)
    return (total_loss, dx[None], *[results[n][q] for q in range(4) for n in WEIGHTS])
```

```python
import functools

import jax
import jax.numpy as jnp
from jax import lax
from jax.experimental import pallas as pl
from jax.experimental.pallas import tpu as pltpu

F32, BF16 = jnp.float32, jnp.bfloat16
SDS = jax.ShapeDtypeStruct
MESH = pl.DeviceIdType.MESH

DEPTH = 2
L = 2048
D = 1024
NIN = 4096
W = 512
G, P, C = 32, 64, 16
GP = G * P
WINS = (2, 4, 8, 16)
TM = 256
NT = L // TM
TMM = 512
TK = 1024
LAST_ROWS = 512
EPS = 1e-6
VMEM_LIMIT = 56 * 2**20

LR, B1, B2, EPS_A, WD, STEP = 0.001, 0.9, 0.999, 1e-08, 0.01, 10


def _params(*sem):
    return pltpu.CompilerParams(dimension_semantics=sem, vmem_limit_bytes=VMEM_LIMIT)


_ANY = pl.BlockSpec(memory_space=pl.ANY)


def _full(shape):
    zeros = (0,) * len(shape)
    return pl.BlockSpec(shape, lambda *_: zeros)


def _layer(l, shape):
    zeros = (0,) * len(shape)
    return pl.BlockSpec((None,) + shape, lambda *_: (l,) + zeros)


def _rows(width, col=0, reverse=False, tm=TM):
    if reverse:
        return pl.BlockSpec((tm, width), lambda i: (L // tm - 1 - i, col))
    return pl.BlockSpec((tm, width), lambda i: (i, col))


def _rows_mm(width, col=0):
    return _rows(width, col, False, TMM)


def _pcall(body, name, grid, in_specs, out_specs, out_shape, args, scratch=(), aliases=None, jobs=()):
    in_specs, out_specs, out_shape, args, scratch = list(in_specs), list(out_specs), list(out_shape), list(args), list(scratch)
    aliases = dict(aliases or {})
    jobs = [j for j in jobs if j is not None]
    n_in, n_out, n_scr = len(in_specs), len(out_specs), len(scratch)
    srcs = [s for j in jobs for s in j.srcs]
    bufs = [b for j in jobs for b in j.bufs]
    news = [s for j in jobs for s in j.news]
    aliases.update({n_in + len(srcs) + k: n_out + k for k in range(len(bufs))})

    def hosted(*refs):
        cuts = [n_in, len(srcs), len(bufs), n_out, len(bufs), len(news), n_scr]
        parts, p = [], 0
        for n in cuts:
            parts.append(refs[p:p + n])
            p += n
        ins, src_r, _, outs, buf_r, new_r, scr = parts
        sem_r = refs[p:]
        views, ps, pb, pn, pm = [], 0, 0, 0, 0
        for j in jobs:
            views.append((src_r[ps:ps + len(j.srcs)], buf_r[pb:pb + len(j.bufs)], new_r[pn:pn + len(j.news)],
                          sem_r[pm:pm + len(j.scratch)]))
            ps, pb, pn, pm = ps + len(j.srcs), pb + len(j.bufs), pn + len(j.news), pm + len(j.scratch)

        def run(phase):
            for j, v in zip(jobs, views):
                getattr(j, phase)(*v)

        def at_step(step):
            return functools.reduce(jnp.logical_and, [pl.program_id(d) == step(d) for d in range(len(grid))])

        if not grid:
            run("start")
            run("finish")
            return
        pl.when(at_step(lambda d: 0))(lambda: run("start"))
        body(*ins, *outs, *scr)
        pl.when(at_step(lambda d: grid[d] - 1))(lambda: run("finish"))

    outs = pl.pallas_call(
        hosted if jobs else body, name=name, **({"grid": grid} if grid else {}),
        in_specs=in_specs + [_ANY] * (len(srcs) + len(bufs)), out_specs=out_specs + [_ANY] * (len(bufs) + len(news)),
        out_shape=out_shape + [SDS(b.shape, b.dtype) for b in bufs] + news,
        input_output_aliases=aliases, scratch_shapes=scratch + [s for j in jobs for s in j.scratch],
        compiler_params=_params(*(("arbitrary",) * len(grid))))(*args, *srcs, *bufs)
    res, pb, pn = [], n_out, n_out + len(bufs)
    for j in jobs:
        res.append((list(outs[pb:pb + len(j.bufs)]), list(outs[pn:pn + len(j.news)])))
        pb, pn = pb + len(j.bufs), pn + len(j.news)
    return list(outs[:n_out]), res


def _fused(name, grid, parts, shared=None, jobs=()):
    def body(*refs):
        pos = [0]

        def take(n):
            pos[0] += n
            return refs[pos[0] - n:pos[0]]

        ins = [take(len(p["in_specs"])) for p in parts]
        if shared:
            take(1)
            block = take(1)[0]
        outs = [take(len(p["out_specs"])) for p in parts]
        scr = [take(len(p["scratch"])) for p in parts]
        col = 0
        for p, i, o, s in zip(parts, ins, outs, scr):
            view = []
            if shared:
                view = [block.at[:, pl.ds(col, p["width"])]]
                col += p["width"]
            p["body"](*i, *view, *o, *s)

    in_specs = [s for p in parts for s in p["in_specs"]] + ([_ANY] if shared else [])
    out_specs = ([shared[1]] if shared else []) + [s for p in parts for s in p["out_specs"]]
    out_shape = ([SDS(shared[0].shape, shared[0].dtype)] if shared else []) + [s for p in parts for s in p["out_shape"]]
    args = [a for p in parts for a in p["args"]] + ([shared[0]] if shared else [])
    return _pcall(body, name, grid, in_specs, out_specs, out_shape, args, [s for p in parts for s in p["scratch"]],
                  {len(in_specs) - 1: 0} if shared else None, jobs)


def _dot(a, b):
    return jnp.dot(a, b, preferred_element_type=F32)


def _dot_nt(a, b):
    return lax.dot_general(a, b, (((1,), (1,)), ((), ())), preferred_element_type=F32)


def _dot_tn(a, b):
    return lax.dot_general(a, b, (((0,), (0,)), ((), ())), preferred_element_type=F32)


_K0 = 0.7978845608028654
_K1 = 0.044715


def _gelu(x):
    return 0.5 * x * (1.0 + jnp.tanh(_K0 * (x + _K1 * (x * x * x))))


def _gelu_grad(x):
    t = jnp.tanh(_K0 * (x + _K1 * (x * x * x)))
    return 0.5 * (1.0 + t) + 0.5 * x * (1.0 - t * t) * (_K0 * (1.0 + 3.0 * _K1 * x * x))


def _sigmoid(x):
    return jax.nn.sigmoid(x)


def _norm_inproj(l, x, g, w, b, jobs=()):
    def body(x_ref, g_ref, w_ref, b_ref, h_ref, proj_ref):
        xv = x_ref[...]
        r = lax.rsqrt(jnp.mean(xv * xv, axis=-1, keepdims=True) + EPS)
        hb = ((xv * r) * g_ref[...]).astype(BF16)
        h_ref[...] = hb
        for j in range(4):
            cs = slice(j * 1024, (j + 1) * 1024)
            proj_ref[:, cs] = _dot(hb, w_ref[j]) + b_ref[:, cs]

    return _pcall(
        body, f"norm_inproj_l{l}", (L // TMM,),
        [_rows_mm(D), _layer(l, (1, D)), _layer(l, (4, D, 1024)), _layer(l, (1, NIN))],
        [_rows_mm(D), _rows_mm(NIN)],
        [SDS((L, D), BF16), SDS((L, NIN), F32)],
        (x, g, w, b), jobs=jobs)


def _scan_tile(re_ref, im_ref, st_re, st_im, cr_re, cr_im, carry, reverse):
    def chunk(ci, carry):
        c = (TM // 8 - 1 - ci) if reverse else ci
        rows = pl.ds(pl.multiple_of(c * 8, 8), 8)
        new = []
        for lb in range(GP // 512):
            cols = slice(lb * 512, (lb + 1) * 512)
            vr = re_ref[rows, cols]
            vi = im_ref[rows, cols]
            for s, d in enumerate((1, 2, 4)):
                ar = st_re[8 * s:8 * s + 8, cols]
                ai = st_im[8 * s:8 * s + 8, cols]
                sr = pltpu.roll(vr, 8 - d if reverse else d, 0)
                si = pltpu.roll(vi, 8 - d if reverse else d, 0)
                vr, vi = vr + ar * sr - ai * si, vi + ar * si + ai * sr
            cr, ci_ = carry[2 * lb], carry[2 * lb + 1]
            pr = cr_re[:, cols]
            pi = cr_im[:, cols]
            vr, vi = vr + pr * cr - pi * ci_, vi + pr * ci_ + pi * cr
            re_ref[rows, cols] = vr
            im_ref[rows, cols] = vi
            if reverse:
                new += [vr[0:1], vi[0:1]]
            else:
                new += [vr[7:8], vi[7:8]]
        return tuple(new)

    return lax.fori_loop(0, TM // 8, chunk, carry)


def _load_carry(car_ref):
    return tuple(car_ref[r:r + 1, lb * 512:(lb + 1) * 512] for lb in range(GP // 512) for r in (0, 1))


def _store_carry(car_ref, carry):
    for lb in range(GP // 512):
        car_ref[0:1, lb * 512:(lb + 1) * 512] = carry[2 * lb]
        car_ref[1:2, lb * 512:(lb + 1) * 512] = carry[2 * lb + 1]


def _s5_fwd(l, proj, bexp, cre, cimn, powers, dsk, wglu, bglu):
    def body(ua_ref, za_ref, bexp_ref, cre_ref, cimn_ref, st_re_ref, st_im_ref, cr_re_ref, cr_im_ref,
             d_ref, wg_ref, bg_ref, sre_ref, sim_ref, y1_ref, q_ref, ya_ref, car_ref):
        @pl.when(pl.program_id(0) == 0)
        def _():
            car_ref[...] = jnp.zeros_like(car_ref)

        u = ua_ref[...]
        ub = u.astype(BF16)
        for k in range(4):
            bu = _dot(ub[:, 128 * k:128 * (k + 1)], bexp_ref[k])
            sre_ref[:, 512 * k:512 * (k + 1)] = bu[:, :512]
            sim_ref[:, 512 * k:512 * (k + 1)] = bu[:, 512:]
        carry = _scan_tile(sre_ref, sim_ref, st_re_ref, st_im_ref, cr_re_ref, cr_im_ref, _load_carry(car_ref), False)
        _store_carry(car_ref, carry)
        for k in range(4):
            blk = slice(512 * k, 512 * (k + 1))
            ks = slice(128 * k, 128 * (k + 1))
            y0 = _dot(sre_ref[:, blk].astype(BF16), cre_ref[k]) + _dot(sim_ref[:, blk].astype(BF16), cimn_ref[k])
            y1_ref[:, ks] = y0 + d_ref[:, ks] * u[:, ks]
        y2 = _gelu(y1_ref[...])
        q = _dot(y2.astype(BF16), wg_ref[...]) + bg_ref[...]
        q_ref[...] = q
        za = za_ref[...]
        ya_ref[...] = ((y2 * _sigmoid(q)) * (za * _sigmoid(za))).astype(BF16)

    return dict(
        body=body,
        in_specs=[_rows(W, 0), _rows(W, 1), _layer(l, (4, 128, 1024)), _layer(l, (4, 512, 128)),
                  _layer(l, (4, 512, 128)), _layer(l, (24, GP)), _layer(l, (24, GP)), _layer(l, (8, GP)),
                  _layer(l, (8, GP)), _layer(l, (1, W)), _layer(l, (W, W)), _layer(l, (1, W))],
        out_specs=[_rows(GP), _rows(GP), _rows(W), _rows(W), _rows(W)],
        out_shape=[SDS((L, GP), F32), SDS((L, GP), F32), SDS((L, W), F32), SDS((L, W), F32), SDS((L, W), BF16)],
        args=(proj, proj, bexp, cre, cimn, *powers, dsk, wglu, bglu),
        scratch=[pltpu.VMEM((8, GP), F32)])


def _pool_fwd(l, proj, pw, scale):
    def body(ub_ref, zb_ref, pw_ref, sc_ref, pooled_ref, mixed_ref, yb_ref, buf):
        i = pl.program_id(0)

        @pl.when(i == 0)
        def _():
            buf[0:16, :] = jnp.zeros((16, W), F32)

        u = ub_ref[...]
        buf[16:16 + TM, :] = u
        t = i * TM + lax.broadcasted_iota(jnp.int32, (TM, 128), 0)
        for gi, win in enumerate(WINS):
            cs = slice(128 * gi, 128 * (gi + 1))
            acc = u[:, cs]
            for k in range(1, win):
                acc = acc + buf[16 - k:16 - k + TM, cs]
            cnt = jnp.minimum(t + 1, win).astype(F32)
            pb = (acc / cnt - u[:, cs]).astype(BF16)
            pooled_ref[:, cs] = pb
            mixed_ref[:, cs] = _dot(pb, pw_ref[gi])
        zb = zb_ref[...]
        yb_ref[...] = ((mixed_ref[...] * sc_ref[...]) * (zb * _sigmoid(zb))).astype(BF16)
        buf[0:16, :] = buf[TM:TM + 16, :]

    return dict(
        body=body,
        in_specs=[_rows(W, 2), _rows(W, 3), _layer(l, (4, 128, 128)), _layer(l, (1, W))],
        out_specs=[_rows(W), _rows(W), _rows(W)],
        out_shape=[SDS((L, W), BF16), SDS((L, W), F32), SDS((L, W), BF16)],
        args=(proj, proj, pw, scale),
        scratch=[pltpu.VMEM((TM + 16, W), F32)])


def _merge_out(l, ya, yb, proj, x, wa, wb, wo, head=None):
    def body(ya_ref, yb_ref, ga_ref, gb_ref, x_ref, wa_ref, wb_ref, wo_ref, *rest):
        pa_ref, pb_ref, mg_ref = rest[2:5] if head else rest[0:3]
        ya = ya_ref[...]
        yb = yb_ref[...]
        for j in range(4):
            cs = slice(256 * j, 256 * (j + 1))
            pa_ref[:, cs] = _dot(ya, wa_ref[j])
            pb_ref[:, cs] = _dot(yb, wb_ref[j])
        merged = _sigmoid(ga_ref[...]) * pa_ref[...] + _sigmoid(gb_ref[...]) * pb_ref[...]
        mb = merged.astype(BF16)
        mg_ref[...] = mb
        x_next = x_ref[...] + _dot(mb, wo_ref[...])
        if head:
            _loss_tile(x_next, rest[0], rest[1], *rest[5:])
        else:
            rest[3][...] = x_next

    rows, out_rows = _rows_mm(D), SDS((L, D), F32)
    return pl.pallas_call(
        body, name=f"merge_out_l{l}", grid=(L // TMM,),
        in_specs=[_rows_mm(W), _rows_mm(W), _rows_mm(D, 2), _rows_mm(D, 3), rows,
                  _layer(l, (4, W, 256)), _layer(l, (4, W, 256)), _layer(l, (D, D))] + ([_full((1, D)), rows] if head else []),
        out_specs=[rows, rows, rows] + ([_full((2, 128)), rows, _full((1, D))] if head else [rows]),
        out_shape=[out_rows, out_rows, SDS((L, D), BF16)]
        + ([SDS((2, 128), F32), out_rows, SDS((1, D), F32)] if head else [out_rows]),
        compiler_params=_params("arbitrary"),
    )(ya, yb, proj, proj, x, wa, wb, wo, *(head or ()))


def _loss_tile(xv, g_ref, t_ref, loss_ref, dx_ref, dg_ref):
    @pl.when(pl.program_id(0) == 0)
    def _():
        loss_ref[...] = jnp.zeros_like(loss_ref)
        dg_ref[...] = jnp.zeros_like(dg_ref)

    g = g_ref[...]
    r = lax.rsqrt(jnp.mean(xv * xv, axis=-1, keepdims=True) + EPS)
    xn = xv * r
    err = xn * g - t_ref[...]
    part = jnp.sum(jnp.mean(err * err, axis=-1, keepdims=True), axis=0, keepdims=True)
    loss_ref[...] += 0.5 * part
    dy = err * (1.0 / D)
    dg_ref[...] += jnp.sum(dy * xn, axis=0, keepdims=True)
    dxn = dy * g
    dx_ref[...] = r * (dxn - xn * jnp.mean(dxn * xn, axis=-1, keepdims=True))


def _merge_out_bwd(l, dxn, mg, proj, pa, pb, ya, yb, wo, wa, wb, jobs=()):
    def body(dx_ref, mg_ref, ga_ref, gb_ref, pa_ref, pb_ref, ya_ref, yb_ref, wo_ref, wa_ref, wb_ref,
             dg_ref, dya_ref, dyb_ref, dwo_ref, dwa_ref, dwb_ref, dbias_ref):
        @pl.when(pl.program_id(0) == 0)
        def _():
            for ref in (dwo_ref, dwa_ref, dwb_ref, dbias_ref):
                ref[...] = jnp.zeros_like(ref)

        dxb = dx_ref[...].astype(BF16)
        dm = _dot_nt(dxb, wo_ref[...])
        sa = _sigmoid(ga_ref[...])
        sb = _sigmoid(gb_ref[...])
        dga = dm * pa_ref[...] * (sa * (1.0 - sa))
        dgb = dm * pb_ref[...] * (sb * (1.0 - sb))
        dg_ref[:, :D] = dga.astype(BF16)
        dg_ref[:, D:] = dgb.astype(BF16)
        dbias_ref[:, :D] += jnp.sum(dga, axis=0, keepdims=True)
        dbias_ref[:, D:] += jnp.sum(dgb, axis=0, keepdims=True)
        dpa = (dm * sa).astype(BF16)
        dpb = (dm * sb).astype(BF16)
        ya = ya_ref[...]
        yb = yb_ref[...]
        dya = jnp.zeros((TM, W), F32)
        dyb = jnp.zeros((TM, W), F32)
        for j in range(4):
            cs = slice(256 * j, 256 * (j + 1))
            dya = dya + _dot_nt(dpa[:, cs], wa_ref[j])
            dyb = dyb + _dot_nt(dpb[:, cs], wb_ref[j])
            dwa_ref[j] += _dot_tn(ya, dpa[:, cs])
            dwb_ref[j] += _dot_tn(yb, dpb[:, cs])
        dya_ref[...] = dya
        dyb_ref[...] = dyb
        dwo_ref[...] += _dot_tn(mg_ref[...], dxb)

    return _pcall(
        body, f"merge_out_bwd_l{l}", (NT,),
        [_rows(D), _rows(D), _rows(D, 2), _rows(D, 3), _rows(D), _rows(D), _rows(W), _rows(W),
         _layer(l, (D, D)), _layer(l, (4, W, 256)), _layer(l, (4, W, 256))],
        [_rows(2 * D, 1), _rows(W), _rows(W), _full((D, D)), _full((4, W, 256)), _full((4, W, 256)), _full((1, 2 * D))],
        [SDS((L, NIN), BF16), SDS((L, W), F32), SDS((L, W), F32),
         SDS((D, D), F32), SDS((4, W, 256), F32), SDS((4, W, 256), F32), SDS((1, 2 * D), F32)],
        (dxn, mg, proj, proj, pa, pb, ya, yb, wo, wa, wb), jobs=jobs)


def _pool_bwd(l, dyb, proj, mixed, pooled, pw, scale):
    def body(dyb_ref, zb_ref, mixed_ref, pooled_ref, pw_ref, sc_ref, db_ref, dpw_ref, dsc_ref, dbias_ref, buf):
        i = pl.program_id(0)
        tile = NT - 1 - i

        @pl.when(i == 0)
        def _():
            dpw_ref[...] = jnp.zeros_like(dpw_ref)
            dsc_ref[...] = jnp.zeros_like(dsc_ref)
            dbias_ref[...] = jnp.zeros_like(dbias_ref)
            buf[TM:TM + 16, :] = jnp.zeros((16, W), F32)

        dyb = dyb_ref[...]
        zb = zb_ref[...]
        mixed = mixed_ref[...]
        sc = sc_ref[...]
        sg = _sigmoid(zb)
        dyb0 = dyb * (zb * sg)
        dzb = dyb * (mixed * sc) * (sg * (1.0 + zb * (1.0 - sg)))
        db_ref[:, W:] = dzb.astype(BF16)
        dbias_ref[:, W:] += jnp.sum(dzb, axis=0, keepdims=True)
        dsc_ref[...] += jnp.sum(dyb0 * mixed, axis=0, keepdims=True)
        dmix = (dyb0 * sc).astype(BF16)
        t = tile * TM + lax.broadcasted_iota(jnp.int32, (TM, 128), 0)
        for gi, win in enumerate(WINS):
            cs = slice(128 * gi, 128 * (gi + 1))
            dpw_ref[gi] += _dot_tn(pooled_ref[:, cs], dmix[:, cs])
            dpool = _dot_nt(dmix[:, cs], pw_ref[gi])
            cnt = jnp.minimum(t + 1, win).astype(F32)
            e = dpool / cnt
            buf[0:TM, cs] = e
            acc = e - dpool
            for k in range(1, win):
                acc = acc + buf[k:k + TM, cs]
            db_ref[:, cs] = acc.astype(BF16)
            dbias_ref[:, cs] += jnp.sum(acc, axis=0, keepdims=True)
        buf[TM:TM + 16, :] = buf[0:16, :]

    return dict(
        body=body, width=2 * W,
        in_specs=[_rows(W, 0, True), _rows(W, 3, True), _rows(W, 0, True), _rows(W, 0, True),
                  _layer(l, (4, 128, 128)), _layer(l, (1, W))],
        out_specs=[_full((4, 128, 128)), _full((1, W)), _full((1, 2 * W))],
        out_shape=[SDS((4, 128, 128), F32), SDS((1, W), F32), SDS((1, 2 * W), F32)],
        args=(dyb, proj, mixed, pooled, pw, scale),
        scratch=[pltpu.VMEM((TM + 16, W), F32)])


def _s5_bwd(l, dya, proj, y1, q, sre, sim, cret, cimnt, bret, bimt, st_re, st_im, cr_re, cr_im, dsk, wglu):
    def halo(i):
        return (jnp.maximum((NT - 1 - i) * (TM // 8) - 1, 0), 0)

    def body(dya_ref, ua_ref, za_ref, y1_ref, q_ref, sre_ref, sim_ref, hre_ref, him_ref,
             cret_ref, cimnt_ref, bret_ref, bimt_ref, st_re_ref, st_im_ref, cr_re_ref, cr_im_ref, d_ref, wg_ref,
             da_ref, dwg_ref, dbg_ref, dd_ref, dcre_ref, dcimn_ref, dbre_ref, dbim_ref, dare_ref, daim_ref, dbias_ref,
             lre, lim, car_ref):
        i = pl.program_id(0)
        tile = NT - 1 - i

        @pl.when(i == 0)
        def _():
            for ref in (dwg_ref, dbg_ref, dd_ref, dcre_ref, dcimn_ref, dbre_ref, dbim_ref, dare_ref, daim_ref, dbias_ref,
                        car_ref):
                ref[...] = jnp.zeros_like(ref)

        u = ua_ref[...]
        za = za_ref[...]
        y1 = y1_ref[...]
        dya = dya_ref[...]
        y2 = _gelu(y1)
        sg = _sigmoid(q_ref[...])
        sgz = _sigmoid(za)
        dy3 = dya * (za * sgz)
        dza = dya * (y2 * sg) * (sgz * (1.0 + za * (1.0 - sgz)))
        da_ref[:, W:] = dza.astype(BF16)
        dbias_ref[:, W:] += jnp.sum(dza, axis=0, keepdims=True)
        dq = dy3 * y2 * (sg * (1.0 - sg))
        dqb = dq.astype(BF16)
        dy2 = dy3 * sg + _dot_nt(dqb, wg_ref[...])
        dwg_ref[...] += _dot_tn(y2.astype(BF16), dqb)
        dbg_ref[...] += jnp.sum(dq, axis=0, keepdims=True)
        dy1 = dy2 * _gelu_grad(y1)
        dd_ref[...] += jnp.sum(dy1 * u, axis=0, keepdims=True)
        dy1b = dy1.astype(BF16)
        ub = u.astype(BF16)
        for k in range(4):
            blk = slice(512 * k, 512 * (k + 1))
            ks = slice(128 * k, 128 * (k + 1))
            lre[:, blk] = _dot(dy1b[:, ks], cret_ref[k])
            lim[:, blk] = _dot(dy1b[:, ks], cimnt_ref[k])
            dcre_ref[k] += _dot_tn(sre_ref[:, blk].astype(BF16), dy1b[:, ks])
            dcimn_ref[k] += _dot_tn(sim_ref[:, blk].astype(BF16), dy1b[:, ks])
        carry = _scan_tile(lre, lim, st_re_ref, st_im_ref, cr_re_ref, cr_im_ref, _load_carry(car_ref), True)
        _store_carry(car_ref, carry)

        rowid = lax.broadcasted_iota(jnp.int32, (8, 512), 0)
        gate = (tile > 0).astype(F32)

        def chunk(c, _):
            rows = pl.ds(pl.multiple_of(c * 8, 8), 8)
            prows = pl.ds(pl.multiple_of(jnp.maximum(c - 1, 0) * 8, 8), 8)
            for lb in range(GP // 512):
                cols = slice(lb * 512, (lb + 1) * 512)
                sr = sre_ref[rows, cols]
                si = sim_ref[rows, cols]
                pr = jnp.where(c == 0, hre_ref[7:8, cols] * gate, sre_ref[prows, cols][7:8])
                pi = jnp.where(c == 0, him_ref[7:8, cols] * gate, sim_ref[prows, cols][7:8])
                sr = jnp.where(rowid == 0, pr, pltpu.roll(sr, 1, 0))
                si = jnp.where(rowid == 0, pi, pltpu.roll(si, 1, 0))
                lr = lre[rows, cols]
                li = lim[rows, cols]
                dare_ref[:, cols] += sr * lr + si * li
                daim_ref[:, cols] += sr * li - si * lr
            return 0

        lax.fori_loop(0, TM // 8, chunk, 0)

        for k in range(4):
            blk = slice(512 * k, 512 * (k + 1))
            ks = slice(128 * k, 128 * (k + 1))
            lrb = lre[:, blk].astype(BF16)
            lib = lim[:, blk].astype(BF16)
            du = dy1[:, ks] * d_ref[:, ks] + _dot(lrb, bret_ref[k]) + _dot(lib, bimt_ref[k])
            da_ref[:, ks] = du.astype(BF16)
            dbias_ref[:, ks] += jnp.sum(du, axis=0, keepdims=True)
            dbre_ref[k] += _dot_tn(ub[:, ks], lrb)
            dbim_ref[k] += _dot_tn(ub[:, ks], lib)

    return dict(
        body=body, width=2 * W,
        in_specs=[_rows(W, 0, True), _rows(W, 0, True), _rows(W, 1, True), _rows(W, 0, True), _rows(W, 0, True),
                  _rows(GP, 0, True), _rows(GP, 0, True),
                  pl.BlockSpec((8, GP), halo), pl.BlockSpec((8, GP), halo),
                  _layer(l, (4, 128, 512)), _layer(l, (4, 128, 512)), _layer(l, (4, 512, 128)), _layer(l, (4, 512, 128)),
                  _layer(l, (24, GP)), _layer(l, (24, GP)), _layer(l, (8, GP)), _layer(l, (8, GP)), _layer(l, (1, W)),
                  _layer(l, (W, W))],
        out_specs=[_full((W, W)), _full((1, W)), _full((1, W)),
                   _full((4, 512, 128)), _full((4, 512, 128)), _full((4, 128, 512)), _full((4, 128, 512)),
                   _full((8, GP)), _full((8, GP)), _full((1, 2 * W))],
        out_shape=[SDS((W, W), F32), SDS((1, W), F32), SDS((1, W), F32),
                   SDS((4, 512, 128), F32), SDS((4, 512, 128), F32), SDS((4, 128, 512), F32), SDS((4, 128, 512), F32),
                   SDS((8, GP), F32), SDS((8, GP), F32), SDS((1, 2 * W), F32)],
        args=(dya, proj, proj, y1, q, sre, sim, sre, sim, cret, cimnt, bret, bimt, st_re, st_im, cr_re, cr_im, dsk,
              wglu),
        scratch=[pltpu.VMEM((TM, GP), F32), pltpu.VMEM((TM, GP), F32), pltpu.VMEM((8, GP), F32)])


def _inproj_dw(l, r, h, dproj, jobs=(), rows=D // 2):
    def body(h_ref, dp_ref, dw_ref):
        part = _dot_tn(h_ref[...], dp_ref[...])

        @pl.when(pl.program_id(1) == 0)
        def _():
            dw_ref[...] = part

        @pl.when(pl.program_id(1) > 0)
        def _():
            dw_ref[...] += part

    return _pcall(
        body, f"inproj_dw{r}_l{l}", (4, L // TK),
        [pl.BlockSpec((TK, rows), lambda j, i: (i, r)), pl.BlockSpec((TK, 1024), lambda j, i: (i, j))],
        [pl.BlockSpec((None, rows, 1024), lambda j, i: (j, 0, 0))],
        [SDS((4, rows, 1024), F32)],
        (h, dproj), jobs=jobs)


def _inproj_dx(l, dproj, w, x, g, dxn, jobs=()):
    def body(dp_ref, w_ref, x_ref, g_ref, dxn_ref, dx_ref, dg_ref):
        @pl.when(pl.program_id(0) == 0)
        def _():
            dg_ref[...] = jnp.zeros_like(dg_ref)

        dh = _dot_nt(dp_ref[:, 0:1024], w_ref[0])
        for j in range(1, 4):
            dh = dh + _dot_nt(dp_ref[:, j * 1024:(j + 1) * 1024], w_ref[j])
        xv = x_ref[...]
        r = lax.rsqrt(jnp.mean(xv * xv, axis=-1, keepdims=True) + EPS)
        xn = xv * r
        dg_ref[...] += jnp.sum(dh * xn, axis=0, keepdims=True)
        dn = dh * g_ref[...]
        dx_ref[...] = dxn_ref[...] + r * (dn - xn * jnp.mean(dn * xn, axis=-1, keepdims=True))

    return _pcall(
        body, f"inproj_dx_l{l}", (L // TMM,),
        [_rows_mm(NIN), _layer(l, (4, D, 1024)), _rows_mm(D), _layer(l, (1, D)), _rows_mm(D)],
        [_rows_mm(D), _full((1, D))],
        [SDS((L, D), F32), SDS((1, D), F32)],
        (dproj, w, x, g, dxn), jobs=jobs)


def _discretize(log_dt, lam_re, lam_im, b_re, b_im):
    dt = jnp.exp(log_dt)[..., None]
    mag = jnp.exp(lam_re * dt)
    ang = lam_im * dt
    abar_re = mag * jnp.cos(ang)
    abar_im = mag * jnp.sin(ang)
    num_re = abar_re - 1.0
    num_im = abar_im
    den = lam_re * lam_re + lam_im * lam_im
    coef_re = (num_re * lam_re + num_im * lam_im) / den
    coef_im = (num_im * lam_re - num_re * lam_im) / den
    bbar_re = coef_re[..., None] * b_re - coef_im[..., None] * b_im
    bbar_im = coef_re[..., None] * b_im + coef_im[..., None] * b_re
    return abar_re, abar_im, bbar_re, bbar_im


def _powers(abar_re, abar_im):
    ar, ai = abar_re.reshape(DEPTH, 1, GP), abar_im.reshape(DEPTH, 1, GP)
    rows_re, rows_im = [ar], [ai]
    for _ in range(7):
        pr, pi = rows_re[-1], rows_im[-1]
        rows_re.append(pr * ar - pi * ai)
        rows_im.append(pr * ai + pi * ar)
    row = jnp.arange(8)[:, None]

    def steps(rows, keep):
        return jnp.concatenate([jnp.where(keep(d), rows[d - 1], 0.0) for d in (1, 2, 4)], axis=1)

    neg_im = [-r for r in rows_im]
    fwd = (steps(rows_re, lambda d: row >= d), steps(rows_im, lambda d: row >= d),
           jnp.concatenate(rows_re, axis=1), jnp.concatenate(rows_im, axis=1))
    rev = (steps(rows_re, lambda d: row < 8 - d), steps(neg_im, lambda d: row < 8 - d),
           jnp.concatenate(rows_re[::-1], axis=1), jnp.concatenate(neg_im[::-1], axis=1))
    return fwd, rev


_EYE8 = functools.partial(jnp.eye, 8, dtype=F32)


def _expand_in(b):
    return jnp.einsum("lkgpc,gh->lkgchp", b.reshape(DEPTH, 4, 8, P, C), _EYE8()).reshape(DEPTH, 4, 128, 512)


def _extract_in(e):
    return jnp.einsum("lkgchp,gh->lkgpc", e.reshape(DEPTH, 4, 8, C, 8, P), _EYE8()).reshape(DEPTH, G, P, C)


def _expand_out(c):
    return jnp.einsum("lkgcp,gh->lkgphc", c.reshape(DEPTH, 4, 8, C, P), _EYE8()).reshape(DEPTH, 4, 512, 128)


def _extract_out(e):
    return jnp.einsum("lkgphc,gh->lkgcp", e.reshape(DEPTH, 4, 8, P, 8, C), _EYE8()).reshape(DEPTH, G, C, P)


SMALL = ("norm_g", "b_in", "ssm_log_dt", "ssm_lam_re", "ssm_lam_im", "ssm_b_re", "ssm_b_im",
         "ssm_c_re", "ssm_c_im", "ssm_d", "ssm_b_glu", "pool_w", "pool_scale")
BIG = ("w_in", "ssm_w_glu", "w_branch_a", "w_branch_b", "w_out")


def _step(x, target, w, m, v, place):
    sp = {n: w[n] for n in SMALL}
    final_norm_g = w["final_norm_g"]
    wbuf = dict(zip(BIG, _cast_own(place, [w[n] for n in BIG])))
    (abar_re, abar_im, bbar_re, bbar_im), disc_vjp = jax.vjp(
        _discretize, *(sp[n] for n in ("ssm_log_dt", "ssm_lam_re", "ssm_lam_im", "ssm_b_re", "ssm_b_im")))
    powers_fwd, powers_rev = _powers(abar_re, abar_im)
    b_re_x, b_im_x = _expand_in(bbar_re), _expand_in(bbar_im)
    c_re_x, c_imn_x = _expand_out(sp["ssm_c_re"]), _expand_out(-sp["ssm_c_im"])
    b_x = jnp.concatenate([b_re_x, b_im_x], axis=3).astype(BF16)
    t = lambda a: jnp.swapaxes(a, 2, 3).astype(BF16)
    c_re_t, c_imn_t, b_re_t, b_im_t = t(c_re_x), t(c_imn_x), t(b_re_x), t(b_im_x)
    c_re_x, c_imn_x = c_re_x.astype(BF16), c_imn_x.astype(BF16)
    row = lambda n: sp[n].reshape(DEPTH, 1, -1)
    g, b_in, dsk, b_glu, scale = row("norm_g"), row("b_in"), row("ssm_d"), row("ssm_b_glu"), row("pool_scale")
    pw = sp["pool_w"].astype(BF16)

    saved = []
    for l in range(DEPTH):
        three = BIG[2:]
        if l == 0:
            wbuf["w_in"], wbuf["ssm_w_glu"] = _comm_only(
                "gather_first", _RingGatherJob([wbuf["w_in"], wbuf["ssm_w_glu"]], 0))[0]
            jobs = [_GatherJob([wbuf[n] for n in three], 0), _GatherJob([wbuf["ssm_w_glu"]], 1)]
        else:
            jobs = []
        (h, proj), res = _norm_inproj(l, x, g, wbuf["w_in"], b_in, jobs)
        if res:
            wbuf.update(zip(three, res[0][0]))
            (wbuf["ssm_w_glu"],) = res[1][0]
        wg = dict(wbuf, ssm_w_glu=wbuf["ssm_w_glu"].reshape(DEPTH, W, W), w_out=wbuf["w_out"].reshape(DEPTH, D, D))
        job = _GatherJob([wbuf["w_in"]], l + 1) if l + 1 < DEPTH else _GatherJob([wbuf[n] for n in three], l)
        (sre, sim, y1, q, ya, pooled, mixed, yb), res = _fused(
            f"branches_fwd_l{l}", (NT,),
            [_s5_fwd(l, proj, b_x, c_re_x, c_imn_x, powers_fwd, dsk, wg["ssm_w_glu"], b_glu),
             _pool_fwd(l, proj, pw, scale)], jobs=[job])
        if l + 1 < DEPTH:
            (wbuf["w_in"],) = res[0][0]
        else:
            wbuf.update(zip(three, res[0][0]))
        wg = dict(wbuf, ssm_w_glu=wbuf["ssm_w_glu"].reshape(DEPTH, W, W), w_out=wbuf["w_out"].reshape(DEPTH, D, D))
        last = l + 1 == DEPTH
        pa, pb, mg, *tail = _merge_out(l, ya, yb, proj, x, wg["w_branch_a"], wg["w_branch_b"], wg["w_out"],
                                       (final_norm_g.reshape(1, D), target) if last else None)
        saved.append(dict(x=x, h=h, proj=proj, sre=sre, sim=sim, y1=y1, q=q, ya=ya,
                          pooled=pooled, mixed=mixed, yb=yb, pa=pa, pb=pb, mg=mg))
        if last:
            loss, dx, dgf = tail
        else:
            (x,) = tail

    per_layer = {n: [None] * DEPTH for n in ("norm_g", "b_in", "ssm_d", "ssm_b_glu", "pool_w", "pool_scale",
                                             "dare", "daim", "dbre", "dbim", "dcre", "dcimn")}
    red = _Reducer(place, w, m, v)
    for l in reversed(range(DEPTH)):
        s = saved[l]
        (dproj, dya, dyb, dwo, dwa, dwb, dbias_g), res = _merge_out_bwd(
            l, dx, s["mg"], s["proj"], s["pa"], s["pb"], s["ya"], s["yb"],
            wg["w_out"], wg["w_branch_a"], wg["w_branch_b"], red.jobs())
        red.land(res)
        (dproj, dwg, dbg, dd, dcre, dcimn, dbre, dbim, dare, daim, dbias_a, dpw, dsc, dbias_b), res = _fused(
            f"branches_bwd_l{l}", (NT,),
            [_s5_bwd(l, dya, s["proj"], s["y1"], s["q"], s["sre"], s["sim"], c_re_t, c_imn_t, b_re_t, b_im_t,
                     *powers_rev, dsk, wg["ssm_w_glu"]),
             _pool_bwd(l, dyb, s["proj"], s["mixed"], s["pooled"], pw, scale)],
            shared=(dproj, _rows(4 * W, 0, True)), jobs=red.jobs())
        red.land(res)
        rest = [dwg.reshape(4, W // 4, W), dwa, dwb, dwo.reshape(4, D // 4, D)]
        if l == 0:
            red.add(l, "rest", BIG[1:], rest)
        if l == 0:
            for r, rows in ((0, D - LAST_ROWS), (D // LAST_ROWS - 1, LAST_ROWS)):
                outs, res = _inproj_dw(l, r, s["h"], dproj, red.jobs(), rows=rows)
                red.land(res)
                red.add(l, f"in{r}", BIG[:1], outs, [r * rows])
        else:
            dwin, res = _inproj_dw(l, 0, s["h"], dproj, red.jobs(), rows=D)
            red.land(res)
        (dx, dg), res = _inproj_dx(l, dproj, wg["w_in"], s["x"], g, dx, red.jobs())
        red.land(res)
        if l > 0:
            red.add(l, "all", BIG, dwin + rest)
        for n, a in (("norm_g", dg.reshape(D)), ("b_in", jnp.concatenate([dbias_a, dbias_b, dbias_g], axis=1).reshape(NIN)),
                     ("ssm_d", dd.reshape(W)), ("ssm_b_glu", dbg.reshape(W)), ("pool_w", dpw), ("pool_scale", dsc.reshape(W)),
                     ("dare", dare), ("daim", daim), ("dbre", dbre), ("dbim", dbim), ("dcre", dcre), ("dcimn", dcimn)):
            per_layer[n][l] = a
    gs = {n: jnp.stack(a) for n, a in per_layer.items()}
    d_abar = [jnp.sum(gs.pop(n), axis=1).reshape(DEPTH, G, P) for n in ("dare", "daim")]
    (gs["ssm_log_dt"], gs["ssm_lam_re"], gs["ssm_lam_im"], gs["ssm_b_re"], gs["ssm_b_im"]) = disc_vjp(
        (*d_abar, _extract_in(gs.pop("dbre")), _extract_in(gs.pop("dbim"))))
    gs["ssm_c_re"], gs["ssm_c_im"] = _extract_out(gs.pop("dcre")), -_extract_out(gs.pop("dcimn"))
    gs["final_norm_g"] = dgf

    natural = {n: w[n].shape for n in REPLICATED}
    rw, rm, rv = {}, {}, {}
    for n in REPLICATED:
        shape = DENSE.get(n, natural[n])
        gs[n], rw[n], rm[n], rv[n] = (a.reshape(shape) for a in (gs[n], w[n], m[n], v[n]))
    small = [gs[n] for n in REPLICATED] + [loss]
    jobs = red.jobs()
    res = _pcall(None, "tail_exchange", (), [], [], [], [], jobs=jobs + [_SiblingJob(small, False)])[1]
    red.land(res[:len(jobs)])
    pair_small = _small_pair_sum(place, small, res[-1][1], [BF16 if a.ndim > 2 else F32 for a in small])
    jobs = red.jobs()
    res = _pcall(None, "tail_gather", (), [], [], [], [], jobs=jobs + [_ChipGatherJob(pair_small)])[1]
    red.land(res[:len(jobs)])
    assert not red.active
    small_parts = dict(zip(REPLICATED + ("loss",), res[-1][0]))

    k = len(REPLICATED)
    outs = _adamw_small("adamw_small", [rw[n] for n in REPLICATED], [small_parts[n] for n in REPLICATED],
                        [rm[n] for n in REPLICATED], [rv[n] for n in REPLICATED], small_parts["loss"])
    results = {n: red.big[n] for n in BIG}
    results.update({n: [outs[1 + q * k + i].reshape(natural[n]) for q in range(4)] for i, n in enumerate(REPLICATED)})
    return outs[0][0, 0], dx, results


def _place():
    x, y, c = lax.axis_index("x"), lax.axis_index("y"), lax.axis_index("c")
    chips = [(1 - x, y), (x, 1 - y), (1 - x, 1 - y)]
    return x, y, c, 2 * x + y, chips, [2 * cx + cy for cx, cy in chips]


def _remote(src, dst, ssem, rsem, dev):
    return pltpu.make_async_remote_copy(src_ref=src, dst_ref=dst, send_sem=ssem, recv_sem=rsem,
                                        device_id=dev, device_id_type=MESH)


class _GatherJob:
    def __init__(self, bufs, l):
        self.srcs, self.bufs, self.news, self.l = [], list(bufs), [], l
        self.scratch = [pltpu.SemaphoreType.DMA((len(self.bufs), 3))] * 4

    def _half(self, ref, k, h):
        rows = ref.shape[2] // 2
        return ref.at[self.l, k, pl.ds(pl.multiple_of(h * rows, 8), rows), :]

    def _ici(self, bufs, sems, a, j, k):
        _, _, c, _, chips, _ = _place()
        blk = self._half(bufs[a], k, c)
        return _remote(blk, blk, sems[0].at[a, j], sems[1].at[a, j], (*chips[j], c))

    def _d2d(self, bufs, sems, a, j, k, h):
        x, y, c, _, _, _ = _place()
        blk = self._half(bufs[a], k, h)
        return _remote(blk, blk, sems[2].at[a, j], sems[3].at[a, j], (x, y, 1 - c))

    def start(self, srcs, bufs, news, sems):
        me = _place()[3]
        for a in range(len(self.bufs)):
            for j in range(3):
                self._ici(bufs, sems, a, j, me).start()

    def finish(self, srcs, bufs, news, sems):
        _, _, c, me, _, cid = _place()
        pairs = [(a, j) for a in range(len(self.bufs)) for j in range(3)]
        for a, j in pairs:
            self._ici(bufs, sems, a, j, cid[j]).wait_recv()
            self._d2d(bufs, sems, a, j, cid[j], c).start()
        for a, j in pairs:
            self._d2d(bufs, sems, a, j, cid[j], 1 - c).wait_recv()
        for a, j in pairs:
            self._ici(bufs, sems, a, j, me).wait_send()
            self._d2d(bufs, sems, a, j, cid[j], c).wait_send()


class _RingGatherJob(_GatherJob):
    def __init__(self, bufs, l):
        super().__init__(bufs, l)
        n = len(self.bufs)
        self.scratch = [pltpu.SemaphoreType.DMA((n, 2))] * 4 + [pltpu.SemaphoreType.DMA((n, 4))] * 2

    def _rows(self, ref, k, h, part):
        half = ref.shape[2] // 2
        start, rows = (h * half, half) if part is None else (h * half + part * (half // 2), half // 2)
        return ref.at[self.l, k, pl.ds(pl.multiple_of(start, 8), rows), :]

    def _to_chip(self, bufs, sems, base, a, j, k, part):
        _, _, c, _, chips, _ = _place()
        blk = self._rows(bufs[a], k, c, part)
        return _remote(blk, blk, sems[base].at[a, j], sems[base + 1].at[a, j], (*chips[j], c))

    def _to_sibling(self, bufs, sems, a, i, k, h, part):
        x, y, c, _, _, _ = _place()
        blk = self._rows(bufs[a], k, h, part)
        return _remote(blk, blk, sems[4].at[a, i], sems[5].at[a, i], (x, y, 1 - c))

    def start(self, srcs, bufs, news, sems):
        me = _place()[3]
        for a in range(len(self.bufs)):
            for j in range(2):
                self._to_chip(bufs, sems, 0, a, j, me, None).start()

    def finish(self, srcs, bufs, news, sems):
        _, _, c, me, _, cid = _place()
        arrays = range(len(self.bufs))
        for a in arrays:
            for j in (1, 0):
                self._to_chip(bufs, sems, 0, a, j, cid[j], None).wait_recv()
                self._to_chip(bufs, sems, 2, a, 1 - j, cid[j], 1 - j).start()
                self._to_sibling(bufs, sems, a, j, cid[j], c, None).start()
        for a in arrays:
            for part in range(2):
                self._to_chip(bufs, sems, 2, a, part, cid[2], part).wait_recv()
                self._to_sibling(bufs, sems, a, 2 + part, cid[2], c, part).start()
        for a in arrays:
            for j in range(2):
                self._to_sibling(bufs, sems, a, j, cid[j], 1 - c, None).wait_recv()
                self._to_sibling(bufs, sems, a, 2 + j, cid[2], 1 - c, j).wait_recv()
        for a in arrays:
            for j in range(2):
                self._to_chip(bufs, sems, 0, a, j, me, None).wait_send()
                self._to_chip(bufs, sems, 2, a, 1 - j, cid[j], 1 - j).wait_send()
                self._to_sibling(bufs, sems, a, j, cid[j], c, None).wait_send()
                self._to_sibling(bufs, sems, a, 2 + j, cid[2], c, j).wait_send()


class _SiblingJob:
    def __init__(self, srcs, rows_half):
        self.srcs, self.bufs, self.rows_half = list(srcs), [], rows_half
        self.news = [SDS((s.shape[0], s.shape[1] // 2, s.shape[2]) if rows_half else s.shape, s.dtype) for s in srcs]
        self.scratch = [pltpu.SemaphoreType.DMA((len(self.srcs),))] * 2

    def _copy(self, srcs, news, sems, a):
        x, y, c, _, _, _ = _place()
        src = srcs[a]
        if self.rows_half:
            rows = src.shape[1] // 2
            src = src.at[:, pl.ds(pl.multiple_of((1 - c) * rows, 8), rows), :]
        return _remote(src, news[a], sems[0].at[a], sems[1].at[a], (x, y, 1 - c))

    def start(self, srcs, bufs, news, sems):
        for a in range(len(self.srcs)):
            self._copy(srcs, news, sems, a).start()

    def finish(self, srcs, bufs, news, sems):
        for a in range(len(self.srcs)):
            self._copy(srcs, news, sems, a).wait()


class _ScatterJob:
    def __init__(self, parts):
        self.srcs, self.bufs = list(parts), []
        self.news = [SDS((3,) + p.shape[1:], p.dtype) for p in parts]
        self.scratch = [pltpu.SemaphoreType.DMA((len(self.srcs), 3))] * 2

    def _copy(self, srcs, news, sems, a, j):
        _, _, c, _, chips, cid = _place()
        return _remote(srcs[a].at[cid[j]], news[a].at[j], sems[0].at[a, j], sems[1].at[a, j], (*chips[j], c))

    def start(self, srcs, bufs, news, sems):
        for a in range(len(self.srcs)):
            for j in range(3):
                self._copy(srcs, news, sems, a, j).start()

    def finish(self, srcs, bufs, news, sems):
        for a in range(len(self.srcs)):
            for j in range(3):
                self._copy(srcs, news, sems, a, j).wait()


def _comm_only(name, job):
    return _pcall(None, name, (), [], [], [], [], jobs=[job])[1][0]


class _ChipGatherJob(_GatherJob):
    def __init__(self, bufs):
        super().__init__(bufs, None)

    def _half(self, ref, k, h):
        return ref.at[k, h]


def _cast_own(place, ws):
    n = len(ws)

    def body(p_ref, *refs):
        for i_ref, o_ref in zip(refs[:n], refs[n:]):
            o_ref[...] = i_ref[...].astype(BF16)

    return pl.pallas_call(
        body, name="cast_own_shards",
        grid_spec=pltpu.PrefetchScalarGridSpec(
            num_scalar_prefetch=1, grid=(DEPTH,),
            in_specs=[pl.BlockSpec((None,) + a.shape[1:], lambda l, p: (l, 0, 0)) for a in ws],
            out_specs=[pl.BlockSpec((None, None) + a.shape[1:], lambda l, p: (l, p[1], 0, 0)) for a in ws]),
        out_shape=[SDS((DEPTH, 4) + a.shape[1:], BF16) for a in ws],
        compiler_params=_params("arbitrary"),
    )(place, *ws)


def _half_tiles(a_):
    rows = a_ // 2
    ta = next(t for t in (256, 128, 64, 32, 16, 8) if rows % t == 0)
    return rows, ta, rows // ta


def _pair_sums_bf16(name, place, owns, recvs):
    n = len(owns)

    def body(p_ref, *refs):
        for own_ref, recv_ref, out_ref in zip(refs[:n], refs[n:2 * n], refs[2 * n:]):
            out_ref[...] = (own_ref[...] + recv_ref[...]).astype(BF16)

    def own_half(a):
        return pl.BlockSpec((None, a.shape[1] // 2, a.shape[2]), lambda s, p: (s, p[0], 0))

    def block(a):
        return pl.BlockSpec((None,) + a.shape[1:], lambda s, p: (s, 0, 0))

    return pl.pallas_call(
        body, name=name,
        grid_spec=pltpu.PrefetchScalarGridSpec(
            num_scalar_prefetch=1, grid=(4,),
            in_specs=[own_half(a) for a in owns] + [block(r) for r in recvs],
            out_specs=[block(r) for r in recvs]),
        out_shape=[SDS(r.shape, BF16) for r in recvs],
        compiler_params=_params("arbitrary"),
    )(place, *owns, *recvs)


def _shard_sums(name, place, owns, recvs, rbufs):
    n = len(owns)

    def body(p_ref, *refs):
        for own_ref, recv_ref, r_ref, out_ref in zip(refs[:n], refs[n:2 * n], refs[2 * n:3 * n], refs[3 * n:]):
            acc = own_ref[...] + recv_ref[...]
            for j in range(3):
                acc = acc + r_ref[j].astype(F32)
            out_ref[...] = acc

    def own_half(a):
        return pl.BlockSpec((None, a.shape[1] // 2, a.shape[2]), lambda i, p: (p[1], p[0], 0))

    def recv_block(a):
        return pl.BlockSpec((None,) + a.shape[1:], lambda i, p: (p[1], 0, 0))

    return pl.pallas_call(
        body, name=name,
        grid_spec=pltpu.PrefetchScalarGridSpec(
            num_scalar_prefetch=1, grid=(1,),
            in_specs=([own_half(a) for a in owns] + [recv_block(r) for r in recvs]
                      + [pl.BlockSpec(rb.shape, lambda i, p: (0, 0, 0)) for rb in rbufs]),
            out_specs=[pl.BlockSpec(r.shape[1:], lambda i, p: (0, 0)) for r in recvs]),
        out_shape=[SDS(r.shape[1:], F32) for r in recvs],
        compiler_params=_params("arbitrary"),
    )(place, *owns, *recvs, *rbufs)


def _small_pair_sum(place, mine, recv, dtypes):
    n = len(mine)

    def body(p_ref, *refs):
        for m_ref, r_ref, o_ref in zip(refs[:n], refs[n:2 * n], refs[2 * n:]):
            o_ref[...] = (m_ref[...] + r_ref[...]).astype(o_ref.dtype)

    def whole(a):
        zeros = (0,) * a.ndim
        return pl.BlockSpec(a.shape, lambda i, p: zeros)

    def mine_blk(a):
        zeros = (0,) * a.ndim
        return pl.BlockSpec((None,) + a.shape, lambda i, p: (p[1],) + zeros)

    return pl.pallas_call(
        body, name="small_pair_sum",
        grid_spec=pltpu.PrefetchScalarGridSpec(
            num_scalar_prefetch=1, grid=(1,),
            in_specs=[whole(a) for a in mine] + [whole(a) for a in recv],
            out_specs=[mine_blk(a) for a in mine]),
        out_shape=[SDS((4,) + a.shape, dt) for a, dt in zip(mine, dtypes)],
        compiler_params=_params("arbitrary"),
    )(place, *mine, *recv)


def _adam_math(w, g, m, v):
    m = B1 * m + (1.0 - B1) * g
    v = B2 * v + (1.0 - B2) * (g * g)
    m_hat = m / (1.0 - B1 ** STEP)
    v_hat = v / (1.0 - B2 ** STEP)
    delta = -LR * (m_hat / (jnp.sqrt(v_hat) + EPS_A) + WD * w)
    return delta, m, v


def _adamw_big(name, l, row0, w, m, v, mine, other, prev, jobs=()):
    _, _, b_ = w.shape
    _, ta, nh = _half_tiles(2 * mine.shape[0])
    prev = list(prev or [])

    def body(w_ref, m_ref, v_ref, mine_ref, other_ref, *rest):
        g_ref, d_ref, mo_ref, vo_ref = rest[len(prev):]
        g = jnp.where(pl.program_id(0) == lax.axis_index("c"), mine_ref[...], other_ref[...])
        g_ref[...] = g
        d_ref[...], mo_ref[...], vo_ref[...] = _adam_math(w_ref[...], g, m_ref[...], v_ref[...])

    slab = pl.BlockSpec((None, ta, b_), lambda h, i: (l, row0 // ta + h * nh + i, 0))
    half = pl.BlockSpec((ta, b_), lambda h, i: (i, 0))
    outs, res = _pcall(
        body, name, (2, nh), [slab, slab, slab, half, half] + [_ANY] * len(prev), [slab] * 4, [SDS(w.shape, F32)] * 4,
        (w, m, v, mine, other, *prev), aliases={5 + k: k for k in range(len(prev))}, jobs=jobs)
    return outs, res


def _adamw_small(name, ws, parts, ms, vs, loss_parts=None):
    k = len(ws)
    extra = [] if loss_parts is None else [loss_parts]

    def chip_sum(p_ref):
        p = [p_ref[k].astype(F32) for k in range(4)]
        return ((p[0] + p[1]) + p[2]) + p[3]

    def body(*refs):
        w_refs, p_refs, m_refs, v_refs = refs[:k], refs[k:2 * k], refs[2 * k:3 * k], refs[3 * k:4 * k]
        outs = refs[4 * k + len(extra):]
        if extra:
            outs[0][...] = chip_sum(refs[4 * k])
            outs = outs[1:]
        for a in range(k):
            g = chip_sum(p_refs[a])
            outs[a][...] = g
            outs[k + a][...], outs[2 * k + a][...], outs[3 * k + a][...] = _adam_math(
                w_refs[a][...], g, m_refs[a][...], v_refs[a][...])

    like = [SDS(a.shape, F32) for a in ws]
    return pl.pallas_call(
        body, name=name,
        out_shape=([SDS(loss_parts.shape[1:], F32)] if extra else []) + like * 4,
        compiler_params=pltpu.CompilerParams(vmem_limit_bytes=VMEM_LIMIT),
    )(*ws, *parts, *ms, *vs, *extra)


class _Reducer:
    def __init__(self, place, w, m, v):
        self.place, self.w, self.m, self.v = place, w, m, v
        self.active, self.riding = [], []
        self.big = {n: None for n in BIG}

    def add(self, l, tag, names, own, row0s=None):
        self.active.append(dict(l=l, key=f"{tag}_l{l}", names=names, own=list(own), row0s=row0s or [0] * len(names),
                                stage=0))

    def jobs(self):
        self.riding = list(self.active)
        return [(_SiblingJob(g["own"], True), _ScatterJob(g.get("parts", [])), _SiblingJob(g.get("shard", []), False))
                [g["stage"]] for g in self.riding]

    def land(self, res):
        for g, (_, news) in zip(self.riding, res):
            if g["stage"] == 0:
                g["recv"] = news
                g["parts"] = _pair_sums_bf16(f"pair_sums_{g['key']}", self.place, g["own"], news)
            elif g["stage"] == 1:
                g["shard"] = _shard_sums(f"shard_sums_{g['key']}", self.place, g["own"], g["recv"], news)
            else:
                for n, mine, other, row0 in zip(g["names"], g["shard"], news, g["row0s"]):
                    self.big[n] = _adamw_big(f"adamw_{n}_{row0}_{g['key']}", g["l"], row0, self.w[n], self.m[n], self.v[n],
                                             mine, other, self.big[n])[0]
                self.active.remove(g)
            g["stage"] += 1
        self.riding = []


WEIGHTS = ("norm_g", "w_in", "b_in", "ssm_log_dt", "ssm_lam_re", "ssm_lam_im", "ssm_b_re", "ssm_b_im", "ssm_c_re",
           "ssm_c_im", "ssm_d", "ssm_w_glu", "ssm_b_glu", "pool_w", "pool_scale", "w_branch_a", "w_branch_b", "w_out",
           "final_norm_g")
REPLICATED = SMALL + ("final_norm_g",)
DENSE = {"ssm_b_re": (DEPTH, G, P * C), "ssm_b_im": (DEPTH, G, P * C), "final_norm_g": (2, D // 2)}


def kernel(x, norm_g, w_in, b_in, ssm_log_dt, ssm_lam_re, ssm_lam_im, ssm_b_re, ssm_b_im, ssm_c_re, ssm_c_im, ssm_d, ssm_w_glu, ssm_b_glu, pool_w, pool_scale, w_branch_a, w_branch_b, w_out, final_norm_g, loss_target, m_norm_g, m_w_in, m_b_in, m_ssm_log_dt, m_ssm_lam_re, m_ssm_lam_im, m_ssm_b_re, m_ssm_b_im, m_ssm_c_re, m_ssm_c_im, m_ssm_d, m_ssm_w_glu, m_ssm_b_glu, m_pool_w, m_pool_scale, m_w_branch_a, m_w_branch_b, m_w_out, m_final_norm_g, v_norm_g, v_w_in, v_b_in, v_ssm_log_dt, v_ssm_lam_re, v_ssm_lam_im, v_ssm_b_re, v_ssm_b_im, v_ssm_c_re, v_ssm_c_im, v_ssm_d, v_ssm_w_glu, v_ssm_b_glu, v_pool_w, v_pool_scale, v_w_branch_a, v_w_branch_b, v_w_out, v_final_norm_g):
    w = dict(zip(WEIGHTS, (norm_g, w_in, b_in, ssm_log_dt, ssm_lam_re, ssm_lam_im, ssm_b_re, ssm_b_im, ssm_c_re,
                           ssm_c_im, ssm_d, ssm_w_glu, ssm_b_glu, pool_w, pool_scale, w_branch_a, w_branch_b, w_out,
                           final_norm_g)))
    m = dict(zip(WEIGHTS, (m_norm_g, m_w_in, m_b_in, m_ssm_log_dt, m_ssm_lam_re, m_ssm_lam_im, m_ssm_b_re, m_ssm_b_im,
                           m_ssm_c_re, m_ssm_c_im, m_ssm_d, m_ssm_w_glu, m_ssm_b_glu, m_pool_w, m_pool_scale,
                           m_w_branch_a, m_w_branch_b, m_w_out, m_final_norm_g)))
    v = dict(zip(WEIGHTS, (v_norm_g, v_w_in, v_b_in, v_ssm_log_dt, v_ssm_lam_re, v_ssm_lam_im, v_ssm_b_re, v_ssm_b_im,
                           v_ssm_c_re, v_ssm_c_im, v_ssm_d, v_ssm_w_glu, v_ssm_b_glu, v_pool_w, v_pool_scale,
                           v_w_branch_a, v_w_branch_b, v_w_out, v_final_norm_g)))
    place = jnp.stack([lax.axis_index("c"), 2 * lax.axis_index("x") + lax.axis_index("y")]).astype(jnp.int32)

    total_loss, dx, results = _step(x[0], loss_target[0], w, m, v, place)
    return (total_loss, dx[None], *[results[n][q] for q in range(4) for n in WEIGHTS])
```

```python
import functools

import jax
import jax.numpy as jnp
from jax import lax
from jax.experimental import pallas as pl
from jax.experimental.pallas import tpu as pltpu

F32, BF16 = jnp.float32, jnp.bfloat16
SDS = jax.ShapeDtypeStruct
MESH = pl.DeviceIdType.MESH

DEPTH = 2
L = 2048
D = 1024
NIN = 4096
W = 512
G, P, C = 32, 64, 16
GP = G * P
WINS = (2, 4, 8, 16)
TM = 256
NT = L // TM
TMM = 512
TK = 1024
LAST_ROWS = 512
EPS = 1e-6
VMEM_LIMIT = 56 * 2**20

LR, B1, B2, EPS_A, WD, STEP = 0.001, 0.9, 0.999, 1e-08, 0.01, 10


def _params(*sem):
    return pltpu.CompilerParams(dimension_semantics=sem, vmem_limit_bytes=VMEM_LIMIT)


_ANY = pl.BlockSpec(memory_space=pl.ANY)


def _full(shape):
    zeros = (0,) * len(shape)
    return pl.BlockSpec(shape, lambda *_: zeros)


def _layer(l, shape):
    zeros = (0,) * len(shape)
    return pl.BlockSpec((None,) + shape, lambda *_: (l,) + zeros)


def _rows(width, col=0, reverse=False, tm=TM):
    if reverse:
        return pl.BlockSpec((tm, width), lambda i: (L // tm - 1 - i, col))
    return pl.BlockSpec((tm, width), lambda i: (i, col))


def _rows_mm(width, col=0):
    return _rows(width, col, False, TMM)


def _pcall(body, name, grid, in_specs, out_specs, out_shape, args, scratch=(), aliases=None, jobs=()):
    in_specs, out_specs, out_shape, args, scratch = list(in_specs), list(out_specs), list(out_shape), list(args), list(scratch)
    aliases = dict(aliases or {})
    jobs = [j for j in jobs if j is not None]
    n_in, n_out, n_scr = len(in_specs), len(out_specs), len(scratch)
    srcs = [s for j in jobs for s in j.srcs]
    bufs = [b for j in jobs for b in j.bufs]
    news = [s for j in jobs for s in j.news]
    aliases.update({n_in + len(srcs) + k: n_out + k for k in range(len(bufs))})

    def hosted(*refs):
        cuts = [n_in, len(srcs), len(bufs), n_out, len(bufs), len(news), n_scr]
        parts, p = [], 0
        for n in cuts:
            parts.append(refs[p:p + n])
            p += n
        ins, src_r, _, outs, buf_r, new_r, scr = parts
        sem_r = refs[p:]
        views, ps, pb, pn, pm = [], 0, 0, 0, 0
        for j in jobs:
            views.append((src_r[ps:ps + len(j.srcs)], buf_r[pb:pb + len(j.bufs)], new_r[pn:pn + len(j.news)],
                          sem_r[pm:pm + len(j.scratch)]))
            ps, pb, pn, pm = ps + len(j.srcs), pb + len(j.bufs), pn + len(j.news), pm + len(j.scratch)

        def run(phase):
            for j, v in zip(jobs, views):
                getattr(j, phase)(*v)

        def at_step(step):
            return functools.reduce(jnp.logical_and, [pl.program_id(d) == step(d) for d in range(len(grid))])

        if not grid:
            run("start")
            run("finish")
            return
        pl.when(at_step(lambda d: 0))(lambda: run("start"))
        body(*ins, *outs, *scr)
        pl.when(at_step(lambda d: grid[d] - 1))(lambda: run("finish"))

    outs = pl.pallas_call(
        hosted if jobs else body, name=name, **({"grid": grid} if grid else {}),
        in_specs=in_specs + [_ANY] * (len(srcs) + len(bufs)), out_specs=out_specs + [_ANY] * (len(bufs) + len(news)),
        out_shape=out_shape + [SDS(b.shape, b.dtype) for b in bufs] + news,
        input_output_aliases=aliases, scratch_shapes=scratch + [s for j in jobs for s in j.scratch],
        compiler_params=_params(*(("arbitrary",) * len(grid))))(*args, *srcs, *bufs)
    res, pb, pn = [], n_out, n_out + len(bufs)
    for j in jobs:
        res.append((list(outs[pb:pb + len(j.bufs)]), list(outs[pn:pn + len(j.news)])))
        pb, pn = pb + len(j.bufs), pn + len(j.news)
    return list(outs[:n_out]), res


def _fused(name, grid, parts, shared=None, jobs=()):
    def body(*refs):
        pos = [0]

        def take(n):
            pos[0] += n
            return refs[pos[0] - n:pos[0]]

        ins = [take(len(p["in_specs"])) for p in parts]
        if shared:
            take(1)
            block = take(1)[0]
        outs = [take(len(p["out_specs"])) for p in parts]
        scr = [take(len(p["scratch"])) for p in parts]
        col = 0
        for p, i, o, s in zip(parts, ins, outs, scr):
            view = []
            if shared:
                view = [block.at[:, pl.ds(col, p["width"])]]
                col += p["width"]
            p["body"](*i, *view, *o, *s)

    in_specs = [s for p in parts for s in p["in_specs"]] + ([_ANY] if shared else [])
    out_specs = ([shared[1]] if shared else []) + [s for p in parts for s in p["out_specs"]]
    out_shape = ([SDS(shared[0].shape, shared[0].dtype)] if shared else []) + [s for p in parts for s in p["out_shape"]]
    args = [a for p in parts for a in p["args"]] + ([shared[0]] if shared else [])
    return _pcall(body, name, grid, in_specs, out_specs, out_shape, args, [s for p in parts for s in p["scratch"]],
                  {len(in_specs) - 1: 0} if shared else None, jobs)


def _dot(a, b):
    return jnp.dot(a, b, preferred_element_type=F32)


def _dot_nt(a, b):
    return lax.dot_general(a, b, (((1,), (1,)), ((), ())), preferred_element_type=F32)


def _dot_tn(a, b):
    return lax.dot_general(a, b, (((0,), (0,)), ((), ())), preferred_element_type=F32)


_K0 = 0.7978845608028654
_K1 = 0.044715


def _gelu(x):
    return 0.5 * x * (1.0 + jnp.tanh(_K0 * (x + _K1 * (x * x * x))))


def _gelu_grad(x):
    t = jnp.tanh(_K0 * (x + _K1 * (x * x * x)))
    return 0.5 * (1.0 + t) + 0.5 * x * (1.0 - t * t) * (_K0 * (1.0 + 3.0 * _K1 * x * x))


def _sigmoid(x):
    return jax.nn.sigmoid(x)


def _norm_inproj(l, x, g, w, b, jobs=()):
    def body(x_ref, g_ref, w_ref, b_ref, h_ref, proj_ref):
        xv = x_ref[...]
        r = lax.rsqrt(jnp.mean(xv * xv, axis=-1, keepdims=True) + EPS)
        hb = ((xv * r) * g_ref[...]).astype(BF16)
        h_ref[...] = hb
        for j in range(4):
            cs = slice(j * 1024, (j + 1) * 1024)
            proj_ref[:, cs] = _dot(hb, w_ref[j]) + b_ref[:, cs]

    return _pcall(
        body, f"norm_inproj_l{l}", (L // TMM,),
        [_rows_mm(D), _layer(l, (1, D)), _layer(l, (4, D, 1024)), _layer(l, (1, NIN))],
        [_rows_mm(D), _rows_mm(NIN)],
        [SDS((L, D), BF16), SDS((L, NIN), F32)],
        (x, g, w, b), jobs=jobs)


def _scan_tile(re_ref, im_ref, st_re, st_im, cr_re, cr_im, carry, reverse, each_chunk=None):
    def chunk(ci, carry):
        c = (TM // 8 - 1 - ci) if reverse else ci
        rows = pl.ds(pl.multiple_of(c * 8, 8), 8)
        new = []
        for lb in range(GP // 512):
            cols = slice(lb * 512, (lb + 1) * 512)
            vr = re_ref[rows, cols]
            vi = im_ref[rows, cols]
            for s, d in enumerate((1, 2, 4)):
                ar = st_re[8 * s:8 * s + 8, cols]
                ai = st_im[8 * s:8 * s + 8, cols]
                sr = pltpu.roll(vr, 8 - d if reverse else d, 0)
                si = pltpu.roll(vi, 8 - d if reverse else d, 0)
                vr, vi = vr + ar * sr - ai * si, vi + ar * si + ai * sr
            cr, ci_ = carry[2 * lb], carry[2 * lb + 1]
            pr = cr_re[:, cols]
            pi = cr_im[:, cols]
            vr, vi = vr + pr * cr - pi * ci_, vi + pr * ci_ + pi * cr
            re_ref[rows, cols] = vr
            im_ref[rows, cols] = vi
            if each_chunk is not None:
                each_chunk(c, cols, vr, vi)
            if reverse:
                new += [vr[0:1], vi[0:1]]
            else:
                new += [vr[7:8], vi[7:8]]
        return tuple(new)

    return lax.fori_loop(0, TM // 8, chunk, carry)


def _load_carry(car_ref):
    return tuple(car_ref[r:r + 1, lb * 512:(lb + 1) * 512] for lb in range(GP // 512) for r in (0, 1))


def _store_carry(car_ref, carry):
    for lb in range(GP // 512):
        car_ref[0:1, lb * 512:(lb + 1) * 512] = carry[2 * lb]
        car_ref[1:2, lb * 512:(lb + 1) * 512] = carry[2 * lb + 1]


def _s5_fwd(l, proj, bexp, cre, cimn, powers, dsk, wglu, bglu):
    def body(ua_ref, za_ref, bexp_ref, cre_ref, cimn_ref, st_re_ref, st_im_ref, cr_re_ref, cr_im_ref,
             d_ref, wg_ref, bg_ref, sre_ref, sim_ref, y1_ref, q_ref, ya_ref, car_ref):
        @pl.when(pl.program_id(0) == 0)
        def _():
            car_ref[...] = jnp.zeros_like(car_ref)

        u = ua_ref[...]
        ub = u.astype(BF16)
        for k in range(4):
            bu = _dot(ub[:, 128 * k:128 * (k + 1)], bexp_ref[k])
            sre_ref[:, 512 * k:512 * (k + 1)] = bu[:, :512]
            sim_ref[:, 512 * k:512 * (k + 1)] = bu[:, 512:]
        carry = _scan_tile(sre_ref, sim_ref, st_re_ref, st_im_ref, cr_re_ref, cr_im_ref, _load_carry(car_ref), False)
        _store_carry(car_ref, carry)
        for k in range(4):
            blk = slice(512 * k, 512 * (k + 1))
            ks = slice(128 * k, 128 * (k + 1))
            y0 = _dot(sre_ref[:, blk].astype(BF16), cre_ref[k]) + _dot(sim_ref[:, blk].astype(BF16), cimn_ref[k])
            y1_ref[:, ks] = y0 + d_ref[:, ks] * u[:, ks]
        y2 = _gelu(y1_ref[...])
        q = _dot(y2.astype(BF16), wg_ref[...]) + bg_ref[...]
        q_ref[...] = q
        za = za_ref[...]
        ya_ref[...] = ((y2 * _sigmoid(q)) * (za * _sigmoid(za))).astype(BF16)

    return dict(
        body=body,
        in_specs=[_rows(W, 0), _rows(W, 1), _layer(l, (4, 128, 1024)), _layer(l, (4, 512, 128)),
                  _layer(l, (4, 512, 128)), _layer(l, (24, GP)), _layer(l, (24, GP)), _layer(l, (8, GP)),
                  _layer(l, (8, GP)), _layer(l, (1, W)), _layer(l, (W, W)), _layer(l, (1, W))],
        out_specs=[_rows(GP), _rows(GP), _rows(W), _rows(W), _rows(W)],
        out_shape=[SDS((L, GP), F32), SDS((L, GP), F32), SDS((L, W), F32), SDS((L, W), F32), SDS((L, W), BF16)],
        args=(proj, proj, bexp, cre, cimn, *powers, dsk, wglu, bglu),
        scratch=[pltpu.VMEM((8, GP), F32)])


def _pool_fwd(l, proj, pw, scale):
    def body(ub_ref, zb_ref, pw_ref, sc_ref, pooled_ref, mixed_ref, yb_ref, buf):
        i = pl.program_id(0)

        @pl.when(i == 0)
        def _():
            buf[0:16, :] = jnp.zeros((16, W), F32)

        u = ub_ref[...]
        buf[16:16 + TM, :] = u
        t = i * TM + lax.broadcasted_iota(jnp.int32, (TM, 128), 0)
        for gi, win in enumerate(WINS):
            cs = slice(128 * gi, 128 * (gi + 1))
            acc = u[:, cs]
            for k in range(1, win):
                acc = acc + buf[16 - k:16 - k + TM, cs]
            cnt = jnp.minimum(t + 1, win).astype(F32)
            pb = (acc / cnt - u[:, cs]).astype(BF16)
            pooled_ref[:, cs] = pb
            mixed_ref[:, cs] = _dot(pb, pw_ref[gi])
        zb = zb_ref[...]
        yb_ref[...] = ((mixed_ref[...] * sc_ref[...]) * (zb * _sigmoid(zb))).astype(BF16)
        buf[0:16, :] = buf[TM:TM + 16, :]

    return dict(
        body=body,
        in_specs=[_rows(W, 2), _rows(W, 3), _layer(l, (4, 128, 128)), _layer(l, (1, W))],
        out_specs=[_rows(W), _rows(W), _rows(W)],
        out_shape=[SDS((L, W), BF16), SDS((L, W), F32), SDS((L, W), BF16)],
        args=(proj, proj, pw, scale),
        scratch=[pltpu.VMEM((TM + 16, W), F32)])


def _merge_out(l, ya, yb, proj, x, wa, wb, wo, head=None):
    def body(ya_ref, yb_ref, ga_ref, gb_ref, x_ref, wa_ref, wb_ref, wo_ref, *rest):
        pa_ref, pb_ref, mg_ref = rest[2:5] if head else rest[0:3]
        ya = ya_ref[...]
        yb = yb_ref[...]
        for j in range(4):
            cs = slice(256 * j, 256 * (j + 1))
            pa_ref[:, cs] = _dot(ya, wa_ref[j])
            pb_ref[:, cs] = _dot(yb, wb_ref[j])
        merged = _sigmoid(ga_ref[...]) * pa_ref[...] + _sigmoid(gb_ref[...]) * pb_ref[...]
        mb = merged.astype(BF16)
        mg_ref[...] = mb
        x_next = x_ref[...] + _dot(mb, wo_ref[...])
        if head:
            _loss_tile(x_next, rest[0], rest[1], *rest[5:])
        else:
            rest[3][...] = x_next

    rows, out_rows = _rows_mm(D), SDS((L, D), F32)
    return pl.pallas_call(
        body, name=f"merge_out_l{l}", grid=(L // TMM,),
        in_specs=[_rows_mm(W), _rows_mm(W), _rows_mm(D, 2), _rows_mm(D, 3), rows,
                  _layer(l, (4, W, 256)), _layer(l, (4, W, 256)), _layer(l, (D, D))] + ([_full((1, D)), rows] if head else []),
        out_specs=[rows, rows, rows] + ([_full((2, 128)), rows, _full((1, D))] if head else [rows]),
        out_shape=[out_rows, out_rows, SDS((L, D), BF16)]
        + ([SDS((2, 128), F32), out_rows, SDS((1, D), F32)] if head else [out_rows]),
        compiler_params=_params("arbitrary"),
    )(ya, yb, proj, proj, x, wa, wb, wo, *(head or ()))


def _loss_tile(xv, g_ref, t_ref, loss_ref, dx_ref, dg_ref):
    @pl.when(pl.program_id(0) == 0)
    def _():
        loss_ref[...] = jnp.zeros_like(loss_ref)
        dg_ref[...] = jnp.zeros_like(dg_ref)

    g = g_ref[...]
    r = lax.rsqrt(jnp.mean(xv * xv, axis=-1, keepdims=True) + EPS)
    xn = xv * r
    err = xn * g - t_ref[...]
    part = jnp.sum(jnp.mean(err * err, axis=-1, keepdims=True), axis=0, keepdims=True)
    loss_ref[...] += 0.5 * part
    dy = err * (1.0 / D)
    dg_ref[...] += jnp.sum(dy * xn, axis=0, keepdims=True)
    dxn = dy * g
    dx_ref[...] = r * (dxn - xn * jnp.mean(dxn * xn, axis=-1, keepdims=True))


def _merge_out_bwd(l, dxn, mg, proj, pa, pb, ya, yb, wo, wa, wb, jobs=()):
    def body(dx_ref, mg_ref, ga_ref, gb_ref, pa_ref, pb_ref, ya_ref, yb_ref, wo_ref, wa_ref, wb_ref,
             dg_ref, dya_ref, dyb_ref, dwo_ref, dwa_ref, dwb_ref, dbias_ref):
        @pl.when(pl.program_id(0) == 0)
        def _():
            for ref in (dwo_ref, dwa_ref, dwb_ref, dbias_ref):
                ref[...] = jnp.zeros_like(ref)

        dxb = dx_ref[...].astype(BF16)
        dm = _dot_nt(dxb, wo_ref[...])
        sa = _sigmoid(ga_ref[...])
        sb = _sigmoid(gb_ref[...])
        dga = dm * pa_ref[...] * (sa * (1.0 - sa))
        dgb = dm * pb_ref[...] * (sb * (1.0 - sb))
        dg_ref[:, :D] = dga.astype(BF16)
        dg_ref[:, D:] = dgb.astype(BF16)
        dbias_ref[:, :D] += jnp.sum(dga, axis=0, keepdims=True)
        dbias_ref[:, D:] += jnp.sum(dgb, axis=0, keepdims=True)
        dpa = (dm * sa).astype(BF16)
        dpb = (dm * sb).astype(BF16)
        ya = ya_ref[...]
        yb = yb_ref[...]
        dya = jnp.zeros((TM, W), F32)
        dyb = jnp.zeros((TM, W), F32)
        for j in range(4):
            cs = slice(256 * j, 256 * (j + 1))
            dya = dya + _dot_nt(dpa[:, cs], wa_ref[j])
            dyb = dyb + _dot_nt(dpb[:, cs], wb_ref[j])
            dwa_ref[j] += _dot_tn(ya, dpa[:, cs])
            dwb_ref[j] += _dot_tn(yb, dpb[:, cs])
        dya_ref[...] = dya
        dyb_ref[...] = dyb
        dwo_ref[...] += _dot_tn(mg_ref[...], dxb)

    return _pcall(
        body, f"merge_out_bwd_l{l}", (NT,),
        [_rows(D), _rows(D), _rows(D, 2), _rows(D, 3), _rows(D), _rows(D), _rows(W), _rows(W),
         _layer(l, (D, D)), _layer(l, (4, W, 256)), _layer(l, (4, W, 256))],
        [_rows(2 * D, 1), _rows(W), _rows(W), _full((D, D)), _full((4, W, 256)), _full((4, W, 256)), _full((1, 2 * D))],
        [SDS((L, NIN), BF16), SDS((L, W), F32), SDS((L, W), F32),
         SDS((D, D), F32), SDS((4, W, 256), F32), SDS((4, W, 256), F32), SDS((1, 2 * D), F32)],
        (dxn, mg, proj, proj, pa, pb, ya, yb, wo, wa, wb), jobs=jobs)


def _pool_bwd(l, dyb, proj, mixed, pooled, pw, scale):
    def body(dyb_ref, zb_ref, mixed_ref, pooled_ref, pw_ref, sc_ref, db_ref, dpw_ref, dsc_ref, dbias_ref, buf):
        i = pl.program_id(0)
        tile = NT - 1 - i

        @pl.when(i == 0)
        def _():
            dpw_ref[...] = jnp.zeros_like(dpw_ref)
            dsc_ref[...] = jnp.zeros_like(dsc_ref)
            dbias_ref[...] = jnp.zeros_like(dbias_ref)
            buf[TM:TM + 16, :] = jnp.zeros((16, W), F32)

        dyb = dyb_ref[...]
        zb = zb_ref[...]
        mixed = mixed_ref[...]
        sc = sc_ref[...]
        sg = _sigmoid(zb)
        dyb0 = dyb * (zb * sg)
        dzb = dyb * (mixed * sc) * (sg * (1.0 + zb * (1.0 - sg)))
        db_ref[:, W:] = dzb.astype(BF16)
        dbias_ref[:, W:] += jnp.sum(dzb, axis=0, keepdims=True)
        dsc_ref[...] += jnp.sum(dyb0 * mixed, axis=0, keepdims=True)
        dmix = (dyb0 * sc).astype(BF16)
        t = tile * TM + lax.broadcasted_iota(jnp.int32, (TM, 128), 0)
        for gi, win in enumerate(WINS):
            cs = slice(128 * gi, 128 * (gi + 1))
            dpw_ref[gi] += _dot_tn(pooled_ref[:, cs], dmix[:, cs])
            dpool = _dot_nt(dmix[:, cs], pw_ref[gi])
            cnt = jnp.minimum(t + 1, win).astype(F32)
            e = dpool / cnt
            buf[0:TM, cs] = e
            acc = e - dpool
            for k in range(1, win):
                acc = acc + buf[k:k + TM, cs]
            db_ref[:, cs] = acc.astype(BF16)
            dbias_ref[:, cs] += jnp.sum(acc, axis=0, keepdims=True)
        buf[TM:TM + 16, :] = buf[0:16, :]

    return dict(
        body=body, width=2 * W,
        in_specs=[_rows(W, 0, True), _rows(W, 3, True), _rows(W, 0, True), _rows(W, 0, True),
                  _layer(l, (4, 128, 128)), _layer(l, (1, W))],
        out_specs=[_full((4, 128, 128)), _full((1, W)), _full((1, 2 * W))],
        out_shape=[SDS((4, 128, 128), F32), SDS((1, W), F32), SDS((1, 2 * W), F32)],
        args=(dyb, proj, mixed, pooled, pw, scale),
        scratch=[pltpu.VMEM((TM + 16, W), F32)])


def _s5_bwd(l, dya, proj, y1, q, sre, sim, cret, cimnt, bret, bimt, st_re, st_im, cr_re, cr_im, dsk, wglu):
    def halo(i):
        return (jnp.maximum((NT - 1 - i) * (TM // 8) - 1, 0), 0)

    def body(dya_ref, ua_ref, za_ref, y1_ref, q_ref, sre_ref, sim_ref, hre_ref, him_ref,
             cret_ref, cimnt_ref, bret_ref, bimt_ref, st_re_ref, st_im_ref, cr_re_ref, cr_im_ref, d_ref, wg_ref,
             da_ref, dwg_ref, dbg_ref, dd_ref, dcre_ref, dcimn_ref, dbre_ref, dbim_ref, dare_ref, daim_ref, dbias_ref,
             lre, lim, car_ref):
        i = pl.program_id(0)
        tile = NT - 1 - i

        @pl.when(i == 0)
        def _():
            for ref in (dwg_ref, dbg_ref, dd_ref, dcre_ref, dcimn_ref, dbre_ref, dbim_ref, dare_ref, daim_ref, dbias_ref,
                        car_ref):
                ref[...] = jnp.zeros_like(ref)

        u = ua_ref[...]
        za = za_ref[...]
        y1 = y1_ref[...]
        dya = dya_ref[...]
        y2 = _gelu(y1)
        sg = _sigmoid(q_ref[...])
        sgz = _sigmoid(za)
        dy3 = dya * (za * sgz)
        dza = dya * (y2 * sg) * (sgz * (1.0 + za * (1.0 - sgz)))
        da_ref[:, W:] = dza.astype(BF16)
        dbias_ref[:, W:] += jnp.sum(dza, axis=0, keepdims=True)
        dq = dy3 * y2 * (sg * (1.0 - sg))
        dqb = dq.astype(BF16)
        dy2 = dy3 * sg + _dot_nt(dqb, wg_ref[...])
        dwg_ref[...] += _dot_tn(y2.astype(BF16), dqb)
        dbg_ref[...] += jnp.sum(dq, axis=0, keepdims=True)
        dy1 = dy2 * _gelu_grad(y1)
        dd_ref[...] += jnp.sum(dy1 * u, axis=0, keepdims=True)
        dy1b = dy1.astype(BF16)
        ub = u.astype(BF16)
        for k in range(4):
            blk = slice(512 * k, 512 * (k + 1))
            ks = slice(128 * k, 128 * (k + 1))
            lre[:, blk] = _dot(dy1b[:, ks], cret_ref[k])
            lim[:, blk] = _dot(dy1b[:, ks], cimnt_ref[k])
            dcre_ref[k] += _dot_tn(sre_ref[:, blk].astype(BF16), dy1b[:, ks])
            dcimn_ref[k] += _dot_tn(sim_ref[:, blk].astype(BF16), dy1b[:, ks])
        rowid = lax.broadcasted_iota(jnp.int32, (8, 512), 0)
        gate = (tile > 0).astype(F32)

        def d_abar(c, cols, lr, li):
            rows = pl.ds(pl.multiple_of(c * 8, 8), 8)
            prows = pl.ds(pl.multiple_of(jnp.maximum(c - 1, 0) * 8, 8), 8)
            pr = jnp.where(c == 0, hre_ref[7:8, cols] * gate, sre_ref[prows, cols][7:8])
            pi = jnp.where(c == 0, him_ref[7:8, cols] * gate, sim_ref[prows, cols][7:8])
            sr = jnp.where(rowid == 0, pr, pltpu.roll(sre_ref[rows, cols], 1, 0))
            si = jnp.where(rowid == 0, pi, pltpu.roll(sim_ref[rows, cols], 1, 0))
            dare_ref[:, cols] += sr * lr + si * li
            daim_ref[:, cols] += sr * li - si * lr

        carry = _scan_tile(lre, lim, st_re_ref, st_im_ref, cr_re_ref, cr_im_ref, _load_carry(car_ref), True, d_abar)
        _store_carry(car_ref, carry)

        for k in range(4):
            blk = slice(512 * k, 512 * (k + 1))
            ks = slice(128 * k, 128 * (k + 1))
            lrb = lre[:, blk].astype(BF16)
            lib = lim[:, blk].astype(BF16)
            du = dy1[:, ks] * d_ref[:, ks] + _dot(lrb, bret_ref[k]) + _dot(lib, bimt_ref[k])
            da_ref[:, ks] = du.astype(BF16)
            dbias_ref[:, ks] += jnp.sum(du, axis=0, keepdims=True)
            dbre_ref[k] += _dot_tn(ub[:, ks], lrb)
            dbim_ref[k] += _dot_tn(ub[:, ks], lib)

    return dict(
        body=body, width=2 * W,
        in_specs=[_rows(W, 0, True), _rows(W, 0, True), _rows(W, 1, True), _rows(W, 0, True), _rows(W, 0, True),
                  _rows(GP, 0, True), _rows(GP, 0, True),
                  pl.BlockSpec((8, GP), halo), pl.BlockSpec((8, GP), halo),
                  _layer(l, (4, 128, 512)), _layer(l, (4, 128, 512)), _layer(l, (4, 512, 128)), _layer(l, (4, 512, 128)),
                  _layer(l, (24, GP)), _layer(l, (24, GP)), _layer(l, (8, GP)), _layer(l, (8, GP)), _layer(l, (1, W)),
                  _layer(l, (W, W))],
        out_specs=[_full((W, W)), _full((1, W)), _full((1, W)),
                   _full((4, 512, 128)), _full((4, 512, 128)), _full((4, 128, 512)), _full((4, 128, 512)),
                   _full((8, GP)), _full((8, GP)), _full((1, 2 * W))],
        out_shape=[SDS((W, W), F32), SDS((1, W), F32), SDS((1, W), F32),
                   SDS((4, 512, 128), F32), SDS((4, 512, 128), F32), SDS((4, 128, 512), F32), SDS((4, 128, 512), F32),
                   SDS((8, GP), F32), SDS((8, GP), F32), SDS((1, 2 * W), F32)],
        args=(dya, proj, proj, y1, q, sre, sim, sre, sim, cret, cimnt, bret, bimt, st_re, st_im, cr_re, cr_im, dsk,
              wglu),
        scratch=[pltpu.VMEM((TM, GP), F32), pltpu.VMEM((TM, GP), F32), pltpu.VMEM((8, GP), F32)])


def _inproj_dw(l, r, h, dproj, jobs=(), rows=D // 2):
    def body(h_ref, dp_ref, dw_ref):
        part = _dot_tn(h_ref[...], dp_ref[...])

        @pl.when(pl.program_id(1) == 0)
        def _():
            dw_ref[...] = part

        @pl.when(pl.program_id(1) > 0)
        def _():
            dw_ref[...] += part

    return _pcall(
        body, f"inproj_dw{r}_l{l}", (4, L // TK),
        [pl.BlockSpec((TK, rows), lambda j, i: (i, r)), pl.BlockSpec((TK, 1024), lambda j, i: (i, j))],
        [pl.BlockSpec((None, rows, 1024), lambda j, i: (j, 0, 0))],
        [SDS((4, rows, 1024), F32)],
        (h, dproj), jobs=jobs)


def _inproj_dx(l, dproj, w, x, g, dxn, jobs=()):
    def body(dp_ref, w_ref, x_ref, g_ref, dxn_ref, dx_ref, dg_ref):
        @pl.when(pl.program_id(0) == 0)
        def _():
            dg_ref[...] = jnp.zeros_like(dg_ref)

        dh = _dot_nt(dp_ref[:, 0:1024], w_ref[0])
        for j in range(1, 4):
            dh = dh + _dot_nt(dp_ref[:, j * 1024:(j + 1) * 1024], w_ref[j])
        xv = x_ref[...]
        r = lax.rsqrt(jnp.mean(xv * xv, axis=-1, keepdims=True) + EPS)
        xn = xv * r
        dg_ref[...] += jnp.sum(dh * xn, axis=0, keepdims=True)
        dn = dh * g_ref[...]
        dx_ref[...] = dxn_ref[...] + r * (dn - xn * jnp.mean(dn * xn, axis=-1, keepdims=True))

    return _pcall(
        body, f"inproj_dx_l{l}", (L // TMM,),
        [_rows_mm(NIN), _layer(l, (4, D, 1024)), _rows_mm(D), _layer(l, (1, D)), _rows_mm(D)],
        [_rows_mm(D), _full((1, D))],
        [SDS((L, D), F32), SDS((1, D), F32)],
        (dproj, w, x, g, dxn), jobs=jobs)


def _discretize(log_dt, lam_re, lam_im, b_re, b_im):
    dt = jnp.exp(log_dt)[..., None]
    mag = jnp.exp(lam_re * dt)
    ang = lam_im * dt
    abar_re = mag * jnp.cos(ang)
    abar_im = mag * jnp.sin(ang)
    num_re = abar_re - 1.0
    num_im = abar_im
    den = lam_re * lam_re + lam_im * lam_im
    coef_re = (num_re * lam_re + num_im * lam_im) / den
    coef_im = (num_im * lam_re - num_re * lam_im) / den
    bbar_re = coef_re[..., None] * b_re - coef_im[..., None] * b_im
    bbar_im = coef_re[..., None] * b_im + coef_im[..., None] * b_re
    return abar_re, abar_im, bbar_re, bbar_im


def _powers(abar_re, abar_im):
    ar, ai = abar_re.reshape(DEPTH, 1, GP), abar_im.reshape(DEPTH, 1, GP)
    rows_re, rows_im = [ar], [ai]
    for _ in range(7):
        pr, pi = rows_re[-1], rows_im[-1]
        rows_re.append(pr * ar - pi * ai)
        rows_im.append(pr * ai + pi * ar)
    row = jnp.arange(8)[:, None]

    def steps(rows, keep):
        return jnp.concatenate([jnp.where(keep(d), rows[d - 1], 0.0) for d in (1, 2, 4)], axis=1)

    neg_im = [-r for r in rows_im]
    fwd = (steps(rows_re, lambda d: row >= d), steps(rows_im, lambda d: row >= d),
           jnp.concatenate(rows_re, axis=1), jnp.concatenate(rows_im, axis=1))
    rev = (steps(rows_re, lambda d: row < 8 - d), steps(neg_im, lambda d: row < 8 - d),
           jnp.concatenate(rows_re[::-1], axis=1), jnp.concatenate(neg_im[::-1], axis=1))
    return fwd, rev


_EYE8 = functools.partial(jnp.eye, 8, dtype=F32)


def _expand_in(b):
    return jnp.einsum("lkgpc,gh->lkgchp", b.reshape(DEPTH, 4, 8, P, C), _EYE8()).reshape(DEPTH, 4, 128, 512)


def _extract_in(e):
    return jnp.einsum("lkgchp,gh->lkgpc", e.reshape(DEPTH, 4, 8, C, 8, P), _EYE8()).reshape(DEPTH, G, P, C)


def _expand_out(c):
    return jnp.einsum("lkgcp,gh->lkgphc", c.reshape(DEPTH, 4, 8, C, P), _EYE8()).reshape(DEPTH, 4, 512, 128)


def _extract_out(e):
    return jnp.einsum("lkgphc,gh->lkgcp", e.reshape(DEPTH, 4, 8, P, 8, C), _EYE8()).reshape(DEPTH, G, C, P)


SMALL = ("norm_g", "b_in", "ssm_log_dt", "ssm_lam_re", "ssm_lam_im", "ssm_b_re", "ssm_b_im",
         "ssm_c_re", "ssm_c_im", "ssm_d", "ssm_b_glu", "pool_w", "pool_scale")
BIG = ("w_in", "ssm_w_glu", "w_branch_a", "w_branch_b", "w_out")


def _step(x, target, w, m, v, place):
    sp = {n: w[n] for n in SMALL}
    final_norm_g = w["final_norm_g"]
    wbuf = dict(zip(BIG, _cast_own(place, [w[n] for n in BIG])))
    (abar_re, abar_im, bbar_re, bbar_im), disc_vjp = jax.vjp(
        _discretize, *(sp[n] for n in ("ssm_log_dt", "ssm_lam_re", "ssm_lam_im", "ssm_b_re", "ssm_b_im")))
    powers_fwd, powers_rev = _powers(abar_re, abar_im)
    b_re_x, b_im_x = _expand_in(bbar_re), _expand_in(bbar_im)
    c_re_x, c_imn_x = _expand_out(sp["ssm_c_re"]), _expand_out(-sp["ssm_c_im"])
    b_x = jnp.concatenate([b_re_x, b_im_x], axis=3).astype(BF16)
    t = lambda a: jnp.swapaxes(a, 2, 3).astype(BF16)
    c_re_t, c_imn_t, b_re_t, b_im_t = t(c_re_x), t(c_imn_x), t(b_re_x), t(b_im_x)
    c_re_x, c_imn_x = c_re_x.astype(BF16), c_imn_x.astype(BF16)
    row = lambda n: sp[n].reshape(DEPTH, 1, -1)
    g, b_in, dsk, b_glu, scale = row("norm_g"), row("b_in"), row("ssm_d"), row("ssm_b_glu"), row("pool_scale")
    pw = sp["pool_w"].astype(BF16)

    saved = []
    for l in range(DEPTH):
        three = BIG[2:]
        if l == 0:
            wbuf["w_in"], wbuf["ssm_w_glu"] = _comm_only(
                "gather_first", _RingGatherJob([wbuf["w_in"], wbuf["ssm_w_glu"]], 0))[0]
            jobs = [_GatherJob([wbuf[n] for n in three], 0), _GatherJob([wbuf["ssm_w_glu"]], 1)]
        else:
            jobs = []
        (h, proj), res = _norm_inproj(l, x, g, wbuf["w_in"], b_in, jobs)
        if res:
            wbuf.update(zip(three, res[0][0]))
            (wbuf["ssm_w_glu"],) = res[1][0]
        wg = dict(wbuf, ssm_w_glu=wbuf["ssm_w_glu"].reshape(DEPTH, W, W), w_out=wbuf["w_out"].reshape(DEPTH, D, D))
        job = _GatherJob([wbuf["w_in"]], l + 1) if l + 1 < DEPTH else _GatherJob([wbuf[n] for n in three], l)
        (sre, sim, y1, q, ya, pooled, mixed, yb), res = _fused(
            f"branches_fwd_l{l}", (NT,),
            [_s5_fwd(l, proj, b_x, c_re_x, c_imn_x, powers_fwd, dsk, wg["ssm_w_glu"], b_glu),
             _pool_fwd(l, proj, pw, scale)], jobs=[job])
        if l + 1 < DEPTH:
            (wbuf["w_in"],) = res[0][0]
        else:
            wbuf.update(zip(three, res[0][0]))
        wg = dict(wbuf, ssm_w_glu=wbuf["ssm_w_glu"].reshape(DEPTH, W, W), w_out=wbuf["w_out"].reshape(DEPTH, D, D))
        last = l + 1 == DEPTH
        pa, pb, mg, *tail = _merge_out(l, ya, yb, proj, x, wg["w_branch_a"], wg["w_branch_b"], wg["w_out"],
                                       (final_norm_g.reshape(1, D), target) if last else None)
        saved.append(dict(x=x, h=h, proj=proj, sre=sre, sim=sim, y1=y1, q=q, ya=ya,
                          pooled=pooled, mixed=mixed, yb=yb, pa=pa, pb=pb, mg=mg))
        if last:
            loss, dx, dgf = tail
        else:
            (x,) = tail

    per_layer = {n: [None] * DEPTH for n in ("norm_g", "b_in", "ssm_d", "ssm_b_glu", "pool_w", "pool_scale",
                                             "dare", "daim", "dbre", "dbim", "dcre", "dcimn")}
    red = _Reducer(place, w, m, v)
    for l in reversed(range(DEPTH)):
        s = saved[l]
        (dproj, dya, dyb, dwo, dwa, dwb, dbias_g), res = _merge_out_bwd(
            l, dx, s["mg"], s["proj"], s["pa"], s["pb"], s["ya"], s["yb"],
            wg["w_out"], wg["w_branch_a"], wg["w_branch_b"], red.jobs())
        red.land(res)
        (dproj, dwg, dbg, dd, dcre, dcimn, dbre, dbim, dare, daim, dbias_a, dpw, dsc, dbias_b), res = _fused(
            f"branches_bwd_l{l}", (NT,),
            [_s5_bwd(l, dya, s["proj"], s["y1"], s["q"], s["sre"], s["sim"], c_re_t, c_imn_t, b_re_t, b_im_t,
                     *powers_rev, dsk, wg["ssm_w_glu"]),
             _pool_bwd(l, dyb, s["proj"], s["mixed"], s["pooled"], pw, scale)],
            shared=(dproj, _rows(4 * W, 0, True)), jobs=red.jobs())
        red.land(res)
        rest = [dwg.reshape(4, W // 4, W), dwa, dwb, dwo.reshape(4, D // 4, D)]
        if l == 0:
            red.add(l, "rest", BIG[1:], rest)
        if l == 0:
            for r, rows in ((0, D - LAST_ROWS), (D // LAST_ROWS - 1, LAST_ROWS)):
                outs, res = _inproj_dw(l, r, s["h"], dproj, red.jobs(), rows=rows)
                red.land(res)
                red.add(l, f"in{r}", BIG[:1], outs, [r * rows])
        else:
            dwin, res = _inproj_dw(l, 0, s["h"], dproj, red.jobs(), rows=D)
            red.land(res)
        (dx, dg), res = _inproj_dx(l, dproj, wg["w_in"], s["x"], g, dx, red.jobs())
        red.land(res)
        if l > 0:
            red.add(l, "all", BIG, dwin + rest)
        for n, a in (("norm_g", dg.reshape(D)), ("b_in", jnp.concatenate([dbias_a, dbias_b, dbias_g], axis=1).reshape(NIN)),
                     ("ssm_d", dd.reshape(W)), ("ssm_b_glu", dbg.reshape(W)), ("pool_w", dpw), ("pool_scale", dsc.reshape(W)),
                     ("dare", dare), ("daim", daim), ("dbre", dbre), ("dbim", dbim), ("dcre", dcre), ("dcimn", dcimn)):
            per_layer[n][l] = a
    gs = {n: jnp.stack(a) for n, a in per_layer.items()}
    d_abar = [jnp.sum(gs.pop(n), axis=1).reshape(DEPTH, G, P) for n in ("dare", "daim")]
    (gs["ssm_log_dt"], gs["ssm_lam_re"], gs["ssm_lam_im"], gs["ssm_b_re"], gs["ssm_b_im"]) = disc_vjp(
        (*d_abar, _extract_in(gs.pop("dbre")), _extract_in(gs.pop("dbim"))))
    gs["ssm_c_re"], gs["ssm_c_im"] = _extract_out(gs.pop("dcre")), -_extract_out(gs.pop("dcimn"))
    gs["final_norm_g"] = dgf

    natural = {n: w[n].shape for n in REPLICATED}
    rw, rm, rv = {}, {}, {}
    for n in REPLICATED:
        shape = DENSE.get(n, natural[n])
        gs[n], rw[n], rm[n], rv[n] = (a.reshape(shape) for a in (gs[n], w[n], m[n], v[n]))
    small = [gs[n] for n in REPLICATED] + [loss]
    jobs = red.jobs()
    res = _pcall(None, "tail_exchange", (), [], [], [], [], jobs=jobs + [_SiblingJob(small, False)])[1]
    red.land(res[:len(jobs)])
    pair_small = _small_pair_sum(place, small, res[-1][1], [BF16 if a.ndim > 2 else F32 for a in small])
    jobs = red.jobs()
    res = _pcall(None, "tail_gather", (), [], [], [], [], jobs=jobs + [_ChipGatherJob(pair_small)])[1]
    red.land(res[:len(jobs)])
    assert not red.active
    small_parts = dict(zip(REPLICATED + ("loss",), res[-1][0]))

    k = len(REPLICATED)
    outs = _adamw_small("adamw_small", [rw[n] for n in REPLICATED], [small_parts[n] for n in REPLICATED],
                        [rm[n] for n in REPLICATED], [rv[n] for n in REPLICATED], small_parts["loss"])
    results = {n: red.big[n] for n in BIG}
    results.update({n: [outs[1 + q * k + i].reshape(natural[n]) for q in range(4)] for i, n in enumerate(REPLICATED)})
    return outs[0][0, 0], dx, results


def _place():
    x, y, c = lax.axis_index("x"), lax.axis_index("y"), lax.axis_index("c")
    chips = [(1 - x, y), (x, 1 - y), (1 - x, 1 - y)]
    return x, y, c, 2 * x + y, chips, [2 * cx + cy for cx, cy in chips]


def _remote(src, dst, ssem, rsem, dev):
    return pltpu.make_async_remote_copy(src_ref=src, dst_ref=dst, send_sem=ssem, recv_sem=rsem,
                                        device_id=dev, device_id_type=MESH)


class _GatherJob:
    def __init__(self, bufs, l):
        self.srcs, self.bufs, self.news, self.l = [], list(bufs), [], l
        self.scratch = [pltpu.SemaphoreType.DMA((len(self.bufs), 3))] * 4

    def _half(self, ref, k, h):
        rows = ref.shape[2] // 2
        return ref.at[self.l, k, pl.ds(pl.multiple_of(h * rows, 8), rows), :]

    def _ici(self, bufs, sems, a, j, k):
        _, _, c, _, chips, _ = _place()
        blk = self._half(bufs[a], k, c)
        return _remote(blk, blk, sems[0].at[a, j], sems[1].at[a, j], (*chips[j], c))

    def _d2d(self, bufs, sems, a, j, k, h):
        x, y, c, _, _, _ = _place()
        blk = self._half(bufs[a], k, h)
        return _remote(blk, blk, sems[2].at[a, j], sems[3].at[a, j], (x, y, 1 - c))

    def start(self, srcs, bufs, news, sems):
        me = _place()[3]
        for a in range(len(self.bufs)):
            for j in range(3):
                self._ici(bufs, sems, a, j, me).start()

    def finish(self, srcs, bufs, news, sems):
        _, _, c, me, _, cid = _place()
        pairs = [(a, j) for a in range(len(self.bufs)) for j in range(3)]
        for a, j in pairs:
            self._ici(bufs, sems, a, j, cid[j]).wait_recv()
            self._d2d(bufs, sems, a, j, cid[j], c).start()
        for a, j in pairs:
            self._d2d(bufs, sems, a, j, cid[j], 1 - c).wait_recv()
        for a, j in pairs:
            self._ici(bufs, sems, a, j, me).wait_send()
            self._d2d(bufs, sems, a, j, cid[j], c).wait_send()


class _RingGatherJob(_GatherJob):
    def __init__(self, bufs, l):
        super().__init__(bufs, l)
        n = len(self.bufs)
        self.scratch = [pltpu.SemaphoreType.DMA((n, 2))] * 4 + [pltpu.SemaphoreType.DMA((n, 4))] * 2

    def _rows(self, ref, k, h, part):
        half = ref.shape[2] // 2
        start, rows = (h * half, half) if part is None else (h * half + part * (half // 2), half // 2)
        return ref.at[self.l, k, pl.ds(pl.multiple_of(start, 8), rows), :]

    def _to_chip(self, bufs, sems, base, a, j, k, part):
        _, _, c, _, chips, _ = _place()
        blk = self._rows(bufs[a], k, c, part)
        return _remote(blk, blk, sems[base].at[a, j], sems[base + 1].at[a, j], (*chips[j], c))

    def _to_sibling(self, bufs, sems, a, i, k, h, part):
        x, y, c, _, _, _ = _place()
        blk = self._rows(bufs[a], k, h, part)
        return _remote(blk, blk, sems[4].at[a, i], sems[5].at[a, i], (x, y, 1 - c))

    def start(self, srcs, bufs, news, sems):
        me = _place()[3]
        for a in range(len(self.bufs)):
            for j in range(2):
                self._to_chip(bufs, sems, 0, a, j, me, None).start()

    def finish(self, srcs, bufs, news, sems):
        _, _, c, me, _, cid = _place()
        arrays = range(len(self.bufs))
        for a in arrays:
            for j in (1, 0):
                self._to_chip(bufs, sems, 0, a, j, cid[j], None).wait_recv()
                self._to_chip(bufs, sems, 2, a, 1 - j, cid[j], 1 - j).start()
                self._to_sibling(bufs, sems, a, j, cid[j], c, None).start()
        for a in arrays:
            for part in range(2):
                self._to_chip(bufs, sems, 2, a, part, cid[2], part).wait_recv()
                self._to_sibling(bufs, sems, a, 2 + part, cid[2], c, part).start()
        for a in arrays:
            for j in range(2):
                self._to_sibling(bufs, sems, a, j, cid[j], 1 - c, None).wait_recv()
                self._to_sibling(bufs, sems, a, 2 + j, cid[2], 1 - c, j).wait_recv()
        for a in arrays:
            for j in range(2):
                self._to_chip(bufs, sems, 0, a, j, me, None).wait_send()
                self._to_chip(bufs, sems, 2, a, 1 - j, cid[j], 1 - j).wait_send()
                self._to_sibling(bufs, sems, a, j, cid[j], c, None).wait_send()
                self._to_sibling(bufs, sems, a, 2 + j, cid[2], c, j).wait_send()


class _SiblingJob:
    def __init__(self, srcs, rows_half):
        self.srcs, self.bufs, self.rows_half = list(srcs), [], rows_half
        self.news = [SDS((s.shape[0], s.shape[1] // 2, s.shape[2]) if rows_half else s.shape, s.dtype) for s in srcs]
        self.scratch = [pltpu.SemaphoreType.DMA((len(self.srcs),))] * 2

    def _copy(self, srcs, news, sems, a):
        x, y, c, _, _, _ = _place()
        src = srcs[a]
        if self.rows_half:
            rows = src.shape[1] // 2
            src = src.at[:, pl.ds(pl.multiple_of((1 - c) * rows, 8), rows), :]
        return _remote(src, news[a], sems[0].at[a], sems[1].at[a], (x, y, 1 - c))

    def start(self, srcs, bufs, news, sems):
        for a in range(len(self.srcs)):
            self._copy(srcs, news, sems, a).start()

    def finish(self, srcs, bufs, news, sems):
        for a in range(len(self.srcs)):
            self._copy(srcs, news, sems, a).wait()


class _ScatterJob:
    def __init__(self, parts):
        self.srcs, self.bufs = list(parts), []
        self.news = [SDS((3,) + p.shape[1:], p.dtype) for p in parts]
        self.scratch = [pltpu.SemaphoreType.DMA((len(self.srcs), 3))] * 2

    def _copy(self, srcs, news, sems, a, j):
        _, _, c, _, chips, cid = _place()
        return _remote(srcs[a].at[cid[j]], news[a].at[j], sems[0].at[a, j], sems[1].at[a, j], (*chips[j], c))

    def start(self, srcs, bufs, news, sems):
        for a in range(len(self.srcs)):
            for j in range(3):
                self._copy(srcs, news, sems, a, j).start()

    def finish(self, srcs, bufs, news, sems):
        for a in range(len(self.srcs)):
            for j in range(3):
                self._copy(srcs, news, sems, a, j).wait()


def _comm_only(name, job):
    return _pcall(None, name, (), [], [], [], [], jobs=[job])[1][0]


class _ChipGatherJob(_GatherJob):
    def __init__(self, bufs):
        super().__init__(bufs, None)

    def _half(self, ref, k, h):
        return ref.at[k, h]


def _cast_own(place, ws):
    n = len(ws)

    def body(p_ref, *refs):
        for i_ref, o_ref in zip(refs[:n], refs[n:]):
            o_ref[...] = i_ref[...].astype(BF16)

    return pl.pallas_call(
        body, name="cast_own_shards",
        grid_spec=pltpu.PrefetchScalarGridSpec(
            num_scalar_prefetch=1, grid=(DEPTH,),
            in_specs=[pl.BlockSpec((None,) + a.shape[1:], lambda l, p: (l, 0, 0)) for a in ws],
            out_specs=[pl.BlockSpec((None, None) + a.shape[1:], lambda l, p: (l, p[1], 0, 0)) for a in ws]),
        out_shape=[SDS((DEPTH, 4) + a.shape[1:], BF16) for a in ws],
        compiler_params=_params("arbitrary"),
    )(place, *ws)


def _half_tiles(a_):
    rows = a_ // 2
    ta = next(t for t in (256, 128, 64, 32, 16, 8) if rows % t == 0)
    return rows, ta, rows // ta


def _pair_sums_bf16(name, place, owns, recvs):
    n = len(owns)

    def body(p_ref, *refs):
        for own_ref, recv_ref, out_ref in zip(refs[:n], refs[n:2 * n], refs[2 * n:]):
            out_ref[...] = (own_ref[...] + recv_ref[...]).astype(BF16)

    def own_half(a):
        return pl.BlockSpec((None, a.shape[1] // 2, a.shape[2]), lambda s, p: (s, p[0], 0))

    def block(a):
        return pl.BlockSpec((None,) + a.shape[1:], lambda s, p: (s, 0, 0))

    return pl.pallas_call(
        body, name=name,
        grid_spec=pltpu.PrefetchScalarGridSpec(
            num_scalar_prefetch=1, grid=(4,),
            in_specs=[own_half(a) for a in owns] + [block(r) for r in recvs],
            out_specs=[block(r) for r in recvs]),
        out_shape=[SDS(r.shape, BF16) for r in recvs],
        compiler_params=_params("arbitrary"),
    )(place, *owns, *recvs)


def _shard_sums(name, place, owns, recvs, rbufs):
    n = len(owns)

    def body(p_ref, *refs):
        for own_ref, recv_ref, r_ref, out_ref in zip(refs[:n], refs[n:2 * n], refs[2 * n:3 * n], refs[3 * n:]):
            acc = own_ref[...] + recv_ref[...]
            for j in range(3):
                acc = acc + r_ref[j].astype(F32)
            out_ref[...] = acc

    def own_half(a):
        return pl.BlockSpec((None, a.shape[1] // 2, a.shape[2]), lambda i, p: (p[1], p[0], 0))

    def recv_block(a):
        return pl.BlockSpec((None,) + a.shape[1:], lambda i, p: (p[1], 0, 0))

    return pl.pallas_call(
        body, name=name,
        grid_spec=pltpu.PrefetchScalarGridSpec(
            num_scalar_prefetch=1, grid=(1,),
            in_specs=([own_half(a) for a in owns] + [recv_block(r) for r in recvs]
                      + [pl.BlockSpec(rb.shape, lambda i, p: (0, 0, 0)) for rb in rbufs]),
            out_specs=[pl.BlockSpec(r.shape[1:], lambda i, p: (0, 0)) for r in recvs]),
        out_shape=[SDS(r.shape[1:], F32) for r in recvs],
        compiler_params=_params("arbitrary"),
    )(place, *owns, *recvs, *rbufs)


def _small_pair_sum(place, mine, recv, dtypes):
    n = len(mine)

    def body(p_ref, *refs):
        for m_ref, r_ref, o_ref in zip(refs[:n], refs[n:2 * n], refs[2 * n:]):
            o_ref[...] = (m_ref[...] + r_ref[...]).astype(o_ref.dtype)

    def whole(a):
        zeros = (0,) * a.ndim
        return pl.BlockSpec(a.shape, lambda i, p: zeros)

    def mine_blk(a):
        zeros = (0,) * a.ndim
        return pl.BlockSpec((None,) + a.shape, lambda i, p: (p[1],) + zeros)

    return pl.pallas_call(
        body, name="small_pair_sum",
        grid_spec=pltpu.PrefetchScalarGridSpec(
            num_scalar_prefetch=1, grid=(1,),
            in_specs=[whole(a) for a in mine] + [whole(a) for a in recv],
            out_specs=[mine_blk(a) for a in mine]),
        out_shape=[SDS((4,) + a.shape, dt) for a, dt in zip(mine, dtypes)],
        compiler_params=_params("arbitrary"),
    )(place, *mine, *recv)


def _adam_math(w, g, m, v):
    m = B1 * m + (1.0 - B1) * g
    v = B2 * v + (1.0 - B2) * (g * g)
    m_hat = m / (1.0 - B1 ** STEP)
    v_hat = v / (1.0 - B2 ** STEP)
    delta = -LR * (m_hat / (jnp.sqrt(v_hat) + EPS_A) + WD * w)
    return delta, m, v


def _adamw_big(name, l, row0, w, m, v, mine, other, prev, jobs=()):
    _, _, b_ = w.shape
    _, ta, nh = _half_tiles(2 * mine.shape[0])
    prev = list(prev or [])

    def body(w_ref, m_ref, v_ref, mine_ref, other_ref, *rest):
        g_ref, d_ref, mo_ref, vo_ref = rest[len(prev):]
        g = jnp.where(pl.program_id(0) == lax.axis_index("c"), mine_ref[...], other_ref[...])
        g_ref[...] = g
        d_ref[...], mo_ref[...], vo_ref[...] = _adam_math(w_ref[...], g, m_ref[...], v_ref[...])

    slab = pl.BlockSpec((None, ta, b_), lambda h, i: (l, row0 // ta + h * nh + i, 0))
    half = pl.BlockSpec((ta, b_), lambda h, i: (i, 0))
    outs, res = _pcall(
        body, name, (2, nh), [slab, slab, slab, half, half] + [_ANY] * len(prev), [slab] * 4, [SDS(w.shape, F32)] * 4,
        (w, m, v, mine, other, *prev), aliases={5 + k: k for k in range(len(prev))}, jobs=jobs)
    return outs, res


def _adamw_small(name, ws, parts, ms, vs, loss_parts=None):
    k = len(ws)
    extra = [] if loss_parts is None else [loss_parts]

    def chip_sum(p_ref):
        p = [p_ref[k].astype(F32) for k in range(4)]
        return ((p[0] + p[1]) + p[2]) + p[3]

    def body(*refs):
        w_refs, p_refs, m_refs, v_refs = refs[:k], refs[k:2 * k], refs[2 * k:3 * k], refs[3 * k:4 * k]
        outs = refs[4 * k + len(extra):]
        if extra:
            outs[0][...] = chip_sum(refs[4 * k])
            outs = outs[1:]
        for a in range(k):
            g = chip_sum(p_refs[a])
            outs[a][...] = g
            outs[k + a][...], outs[2 * k + a][...], outs[3 * k + a][...] = _adam_math(
                w_refs[a][...], g, m_refs[a][...], v_refs[a][...])

    like = [SDS(a.shape, F32) for a in ws]
    return pl.pallas_call(
        body, name=name,
        out_shape=([SDS(loss_parts.shape[1:], F32)] if extra else []) + like * 4,
        compiler_params=pltpu.CompilerParams(vmem_limit_bytes=VMEM_LIMIT),
    )(*ws, *parts, *ms, *vs, *extra)


class _Reducer:
    def __init__(self, place, w, m, v):
        self.place, self.w, self.m, self.v = place, w, m, v
        self.active, self.riding = [], []
        self.big = {n: None for n in BIG}

    def add(self, l, tag, names, own, row0s=None):
        self.active.append(dict(l=l, key=f"{tag}_l{l}", names=names, own=list(own), row0s=row0s or [0] * len(names),
                                stage=0))

    def jobs(self):
        self.riding = list(self.active)
        return [(_SiblingJob(g["own"], True), _ScatterJob(g.get("parts", [])), _SiblingJob(g.get("shard", []), False))
                [g["stage"]] for g in self.riding]

    def land(self, res):
        for g, (_, news) in zip(self.riding, res):
            if g["stage"] == 0:
                g["recv"] = news
                g["parts"] = _pair_sums_bf16(f"pair_sums_{g['key']}", self.place, g["own"], news)
            elif g["stage"] == 1:
                g["shard"] = _shard_sums(f"shard_sums_{g['key']}", self.place, g["own"], g["recv"], news)
            else:
                for n, mine, other, row0 in zip(g["names"], g["shard"], news, g["row0s"]):
                    self.big[n] = _adamw_big(f"adamw_{n}_{row0}_{g['key']}", g["l"], row0, self.w[n], self.m[n], self.v[n],
                                             mine, other, self.big[n])[0]
                self.active.remove(g)
            g["stage"] += 1
        self.riding = []


WEIGHTS = ("norm_g", "w_in", "b_in", "ssm_log_dt", "ssm_lam_re", "ssm_lam_im", "ssm_b_re", "ssm_b_im", "ssm_c_re",
           "ssm_c_im", "ssm_d", "ssm_w_glu", "ssm_b_glu", "pool_w", "pool_scale", "w_branch_a", "w_branch_b", "w_out",
           "final_norm_g")
REPLICATED = SMALL + ("final_norm_g",)
DENSE = {"ssm_b_re": (DEPTH, G, P * C), "ssm_b_im": (DEPTH, G, P * C), "final_norm_g": (2, D // 2)}


def kernel(x, norm_g, w_in, b_in, ssm_log_dt, ssm_lam_re, ssm_lam_im, ssm_b_re, ssm_b_im, ssm_c_re, ssm_c_im, ssm_d, ssm_w_glu, ssm_b_glu, pool_w, pool_scale, w_branch_a, w_branch_b, w_out, final_norm_g, loss_target, m_norm_g, m_w_in, m_b_in, m_ssm_log_dt, m_ssm_lam_re, m_ssm_lam_im, m_ssm_b_re, m_ssm_b_im, m_ssm_c_re, m_ssm_c_im, m_ssm_d, m_ssm_w_glu, m_ssm_b_glu, m_pool_w, m_pool_scale, m_w_branch_a, m_w_branch_b, m_w_out, m_final_norm_g, v_norm_g, v_w_in, v_b_in, v_ssm_log_dt, v_ssm_lam_re, v_ssm_lam_im, v_ssm_b_re, v_ssm_b_im, v_ssm_c_re, v_ssm_c_im, v_ssm_d, v_ssm_w_glu, v_ssm_b_glu, v_pool_w, v_pool_scale, v_w_branch_a, v_w_branch_b, v_w_out, v_final_norm_g):
    w = dict(zip(WEIGHTS, (norm_g, w_in, b_in, ssm_log_dt, ssm_lam_re, ssm_lam_im, ssm_b_re, ssm_b_im, ssm_c_re,
                           ssm_c_im, ssm_d, ssm_w_glu, ssm_b_glu, pool_w, pool_scale, w_branch_a, w_branch_b, w_out,
                           final_norm_g)))
    m = dict(zip(WEIGHTS, (m_norm_g, m_w_in, m_b_in, m_ssm_log_dt, m_ssm_lam_re, m_ssm_lam_im, m_ssm_b_re, m_ssm_b_im,
                           m_ssm_c_re, m_ssm_c_im, m_ssm_d, m_ssm_w_glu, m_ssm_b_glu, m_pool_w, m_pool_scale,
                           m_w_branch_a, m_w_branch_b, m_w_out, m_final_norm_g)))
    v = dict(zip(WEIGHTS, (v_norm_g, v_w_in, v_b_in, v_ssm_log_dt, v_ssm_lam_re, v_ssm_lam_im, v_ssm_b_re, v_ssm_b_im,
                           v_ssm_c_re, v_ssm_c_im, v_ssm_d, v_ssm_w_glu, v_ssm_b_glu, v_pool_w, v_pool_scale,
                           v_w_branch_a, v_w_branch_b, v_w_out, v_final_norm_g)))
    place = jnp.stack([lax.axis_index("c"), 2 * lax.axis_index("x") + lax.axis_index("y")]).astype(jnp.int32)

    total_loss, dx, results = _step(x[0], loss_target[0], w, m, v, place)
    return (total_loss, dx[None], *[results[n][q] for q in range(4) for n in WEIGHTS])
```

```python
import functools

import jax
import jax.numpy as jnp
from jax import lax
from jax.experimental import pallas as pl
from jax.experimental.pallas import tpu as pltpu

F32, BF16 = jnp.float32, jnp.bfloat16
SDS = jax.ShapeDtypeStruct
MESH = pl.DeviceIdType.MESH

DEPTH = 2
L = 2048
D = 1024
NIN = 4096
W = 512
G, P, C = 32, 64, 16
GP = G * P
WINS = (2, 4, 8, 16)
TM = 256
NT = L // TM
TMM = 512
TK = 1024
LAST_ROWS = 512
EPS = 1e-6
VMEM_LIMIT = 56 * 2**20

LR, B1, B2, EPS_A, WD, STEP = 0.001, 0.9, 0.999, 1e-08, 0.01, 10


def _params(*sem):
    return pltpu.CompilerParams(dimension_semantics=sem, vmem_limit_bytes=VMEM_LIMIT)


_ANY = pl.BlockSpec(memory_space=pl.ANY)


def _full(shape):
    zeros = (0,) * len(shape)
    return pl.BlockSpec(shape, lambda *_: zeros)


def _layer(l, shape):
    zeros = (0,) * len(shape)
    return pl.BlockSpec((None,) + shape, lambda *_: (l,) + zeros)


def _rows(width, col=0, reverse=False, tm=TM):
    if reverse:
        return pl.BlockSpec((tm, width), lambda i: (L // tm - 1 - i, col))
    return pl.BlockSpec((tm, width), lambda i: (i, col))


def _rows_mm(width, col=0):
    return _rows(width, col, False, TMM)


def _pcall(body, name, grid, in_specs, out_specs, out_shape, args, scratch=(), aliases=None, jobs=()):
    in_specs, out_specs, out_shape, args, scratch = list(in_specs), list(out_specs), list(out_shape), list(args), list(scratch)
    aliases = dict(aliases or {})
    jobs = [j for j in jobs if j is not None]
    n_in, n_out, n_scr = len(in_specs), len(out_specs), len(scratch)
    srcs = [s for j in jobs for s in j.srcs]
    bufs = [b for j in jobs for b in j.bufs]
    news = [s for j in jobs for s in j.news]
    aliases.update({n_in + len(srcs) + k: n_out + k for k in range(len(bufs))})

    def hosted(*refs):
        cuts = [n_in, len(srcs), len(bufs), n_out, len(bufs), len(news), n_scr]
        parts, p = [], 0
        for n in cuts:
            parts.append(refs[p:p + n])
            p += n
        ins, src_r, _, outs, buf_r, new_r, scr = parts
        sem_r = refs[p:]
        views, ps, pb, pn, pm = [], 0, 0, 0, 0
        for j in jobs:
            views.append((src_r[ps:ps + len(j.srcs)], buf_r[pb:pb + len(j.bufs)], new_r[pn:pn + len(j.news)],
                          sem_r[pm:pm + len(j.scratch)]))
            ps, pb, pn, pm = ps + len(j.srcs), pb + len(j.bufs), pn + len(j.news), pm + len(j.scratch)

        def run(phase):
            for j, v in zip(jobs, views):
                getattr(j, phase)(*v)

        def at_step(step):
            return functools.reduce(jnp.logical_and, [pl.program_id(d) == step(d) for d in range(len(grid))])

        if not grid:
            run("start")
            run("finish")
            return
        pl.when(at_step(lambda d: 0))(lambda: run("start"))
        body(*ins, *outs, *scr)
        pl.when(at_step(lambda d: grid[d] - 1))(lambda: run("finish"))

    outs = pl.pallas_call(
        hosted if jobs else body, name=name, **({"grid": grid} if grid else {}),
        in_specs=in_specs + [_ANY] * (len(srcs) + len(bufs)), out_specs=out_specs + [_ANY] * (len(bufs) + len(news)),
        out_shape=out_shape + [SDS(b.shape, b.dtype) for b in bufs] + news,
        input_output_aliases=aliases, scratch_shapes=scratch + [s for j in jobs for s in j.scratch],
        compiler_params=_params(*(("arbitrary",) * len(grid))))(*args, *srcs, *bufs)
    res, pb, pn = [], n_out, n_out + len(bufs)
    for j in jobs:
        res.append((list(outs[pb:pb + len(j.bufs)]), list(outs[pn:pn + len(j.news)])))
        pb, pn = pb + len(j.bufs), pn + len(j.news)
    return list(outs[:n_out]), res


def _fused(name, grid, parts, shared=None, jobs=()):
    def body(*refs):
        pos = [0]

        def take(n):
            pos[0] += n
            return refs[pos[0] - n:pos[0]]

        ins = [take(len(p["in_specs"])) for p in parts]
        if shared:
            take(1)
            block = take(1)[0]
        outs = [take(len(p["out_specs"])) for p in parts]
        scr = [take(len(p["scratch"])) for p in parts]
        col = 0
        for p, i, o, s in zip(parts, ins, outs, scr):
            view = []
            if shared:
                view = [block.at[:, pl.ds(col, p["width"])]]
                col += p["width"]
            p["body"](*i, *view, *o, *s)

    in_specs = [s for p in parts for s in p["in_specs"]] + ([_ANY] if shared else [])
    out_specs = ([shared[1]] if shared else []) + [s for p in parts for s in p["out_specs"]]
    out_shape = ([SDS(shared[0].shape, shared[0].dtype)] if shared else []) + [s for p in parts for s in p["out_shape"]]
    args = [a for p in parts for a in p["args"]] + ([shared[0]] if shared else [])
    return _pcall(body, name, grid, in_specs, out_specs, out_shape, args, [s for p in parts for s in p["scratch"]],
                  {len(in_specs) - 1: 0} if shared else None, jobs)


def _dot(a, b):
    return jnp.dot(a, b, preferred_element_type=F32)


def _dot_nt(a, b):
    return lax.dot_general(a, b, (((1,), (1,)), ((), ())), preferred_element_type=F32)


def _dot_tn(a, b):
    return lax.dot_general(a, b, (((0,), (0,)), ((), ())), preferred_element_type=F32)


_K0 = 0.7978845608028654
_K1 = 0.044715


def _gelu(x):
    return 0.5 * x * (1.0 + jnp.tanh(_K0 * (x + _K1 * (x * x * x))))


def _gelu_grad(x):
    t = jnp.tanh(_K0 * (x + _K1 * (x * x * x)))
    return 0.5 * (1.0 + t) + 0.5 * x * (1.0 - t * t) * (_K0 * (1.0 + 3.0 * _K1 * x * x))


def _sigmoid(x):
    return jax.nn.sigmoid(x)


def _norm_inproj(l, x, g, w, b, jobs=()):
    def body(x_ref, g_ref, w_ref, b_ref, h_ref, proj_ref):
        xv = x_ref[...]
        r = lax.rsqrt(jnp.mean(xv * xv, axis=-1, keepdims=True) + EPS)
        hb = ((xv * r) * g_ref[...]).astype(BF16)
        h_ref[...] = hb
        for j in range(4):
            cs = slice(j * 1024, (j + 1) * 1024)
            proj_ref[:, cs] = _dot(hb, w_ref[j]) + b_ref[:, cs]

    return _pcall(
        body, f"norm_inproj_l{l}", (L // TMM,),
        [_rows_mm(D), _layer(l, (1, D)), _layer(l, (4, D, 1024)), _layer(l, (1, NIN))],
        [_rows_mm(D), _rows_mm(NIN)],
        [SDS((L, D), BF16), SDS((L, NIN), F32)],
        (x, g, w, b), jobs=jobs)


def _scan_tile(re_ref, im_ref, st_re, st_im, cr_re, cr_im, carry, reverse, each_chunk=None):
    def chunk(ci, carry):
        c = (TM // 8 - 1 - ci) if reverse else ci
        rows = pl.ds(pl.multiple_of(c * 8, 8), 8)
        new = []
        for lb in range(GP // 512):
            cols = slice(lb * 512, (lb + 1) * 512)
            vr = re_ref[rows, cols]
            vi = im_ref[rows, cols]
            for s, d in enumerate((1, 2, 4)):
                ar = st_re[8 * s:8 * s + 8, cols]
                ai = st_im[8 * s:8 * s + 8, cols]
                sr = pltpu.roll(vr, 8 - d if reverse else d, 0)
                si = pltpu.roll(vi, 8 - d if reverse else d, 0)
                vr, vi = vr + ar * sr - ai * si, vi + ar * si + ai * sr
            cr, ci_ = carry[2 * lb], carry[2 * lb + 1]
            pr = cr_re[:, cols]
            pi = cr_im[:, cols]
            vr, vi = vr + pr * cr - pi * ci_, vi + pr * ci_ + pi * cr
            re_ref[rows, cols] = vr
            im_ref[rows, cols] = vi
            if each_chunk is not None:
                each_chunk(c, cols, vr, vi)
            if reverse:
                new += [vr[0:1], vi[0:1]]
            else:
                new += [vr[7:8], vi[7:8]]
        return tuple(new)

    return lax.fori_loop(0, TM // 8, chunk, carry)


def _load_carry(car_ref):
    return tuple(car_ref[r:r + 1, lb * 512:(lb + 1) * 512] for lb in range(GP // 512) for r in (0, 1))


def _store_carry(car_ref, carry):
    for lb in range(GP // 512):
        car_ref[0:1, lb * 512:(lb + 1) * 512] = carry[2 * lb]
        car_ref[1:2, lb * 512:(lb + 1) * 512] = carry[2 * lb + 1]


def _s5_fwd(l, proj, bexp, cre, cimn, powers, dsk, wglu, bglu):
    def body(ua_ref, za_ref, bexp_ref, cre_ref, cimn_ref, st_re_ref, st_im_ref, cr_re_ref, cr_im_ref,
             d_ref, wg_ref, bg_ref, sre_ref, sim_ref, y1_ref, q_ref, ya_ref, car_ref):
        @pl.when(pl.program_id(0) == 0)
        def _():
            car_ref[...] = jnp.zeros_like(car_ref)

        u = ua_ref[...]
        ub = u.astype(BF16)
        for k in range(4):
            bu = _dot(ub[:, 128 * k:128 * (k + 1)], bexp_ref[k])
            sre_ref[:, 512 * k:512 * (k + 1)] = bu[:, :512]
            sim_ref[:, 512 * k:512 * (k + 1)] = bu[:, 512:]
        carry = _scan_tile(sre_ref, sim_ref, st_re_ref, st_im_ref, cr_re_ref, cr_im_ref, _load_carry(car_ref), False)
        _store_carry(car_ref, carry)
        for k in range(4):
            blk = slice(512 * k, 512 * (k + 1))
            ks = slice(128 * k, 128 * (k + 1))
            y0 = _dot(sre_ref[:, blk].astype(BF16), cre_ref[k]) + _dot(sim_ref[:, blk].astype(BF16), cimn_ref[k])
            y1_ref[:, ks] = y0 + d_ref[:, ks] * u[:, ks]
        y2 = _gelu(y1_ref[...])
        q = _dot(y2.astype(BF16), wg_ref[...]) + bg_ref[...]
        q_ref[...] = q
        za = za_ref[...]
        ya_ref[...] = ((y2 * _sigmoid(q)) * (za * _sigmoid(za))).astype(BF16)

    return dict(
        body=body,
        in_specs=[_rows(W, 0), _rows(W, 1), _layer(l, (4, 128, 1024)), _layer(l, (4, 512, 128)),
                  _layer(l, (4, 512, 128)), _layer(l, (24, GP)), _layer(l, (24, GP)), _layer(l, (8, GP)),
                  _layer(l, (8, GP)), _layer(l, (1, W)), _layer(l, (W, W)), _layer(l, (1, W))],
        out_specs=[_rows(GP), _rows(GP), _rows(W), _rows(W), _rows(W)],
        out_shape=[SDS((L, GP), F32), SDS((L, GP), F32), SDS((L, W), F32), SDS((L, W), F32), SDS((L, W), BF16)],
        args=(proj, proj, bexp, cre, cimn, *powers, dsk, wglu, bglu),
        scratch=[pltpu.VMEM((8, GP), F32)])


def _pool_fwd(l, proj, pw, scale):
    def body(ub_ref, zb_ref, pw_ref, sc_ref, pooled_ref, mixed_ref, yb_ref, buf):
        i = pl.program_id(0)

        @pl.when(i == 0)
        def _():
            buf[0:16, :] = jnp.zeros((16, W), F32)

        u = ub_ref[...]
        buf[16:16 + TM, :] = u
        t = i * TM + lax.broadcasted_iota(jnp.int32, (TM, 128), 0)
        for gi, win in enumerate(WINS):
            cs = slice(128 * gi, 128 * (gi + 1))
            acc = u[:, cs]
            for k in range(1, win):
                acc = acc + buf[16 - k:16 - k + TM, cs]
            cnt = jnp.minimum(t + 1, win).astype(F32)
            pb = (acc / cnt - u[:, cs]).astype(BF16)
            pooled_ref[:, cs] = pb
            mixed_ref[:, cs] = _dot(pb, pw_ref[gi])
        zb = zb_ref[...]
        yb_ref[...] = ((mixed_ref[...] * sc_ref[...]) * (zb * _sigmoid(zb))).astype(BF16)
        buf[0:16, :] = buf[TM:TM + 16, :]

    return dict(
        body=body,
        in_specs=[_rows(W, 2), _rows(W, 3), _layer(l, (4, 128, 128)), _layer(l, (1, W))],
        out_specs=[_rows(W), _rows(W), _rows(W)],
        out_shape=[SDS((L, W), BF16), SDS((L, W), F32), SDS((L, W), BF16)],
        args=(proj, proj, pw, scale),
        scratch=[pltpu.VMEM((TM + 16, W), F32)])


def _merge_out(l, ya, yb, proj, x, wa, wb, wo, head=None):
    def body(ya_ref, yb_ref, ga_ref, gb_ref, x_ref, wa_ref, wb_ref, wo_ref, *rest):
        pa_ref, pb_ref, mg_ref = rest[2:5] if head else rest[0:3]
        ya = ya_ref[...]
        yb = yb_ref[...]
        pa_ref[...] = _dot(ya, wa_ref[...])
        pb_ref[...] = _dot(yb, wb_ref[...])
        merged = _sigmoid(ga_ref[...]) * pa_ref[...] + _sigmoid(gb_ref[...]) * pb_ref[...]
        mb = merged.astype(BF16)
        mg_ref[...] = mb
        x_next = x_ref[...] + _dot(mb, wo_ref[...])
        if head:
            _loss_tile(x_next, rest[0], rest[1], *rest[5:])
        else:
            rest[3][...] = x_next

    rows, out_rows = _rows_mm(D), SDS((L, D), F32)
    return pl.pallas_call(
        body, name=f"merge_out_l{l}", grid=(L // TMM,),
        in_specs=[_rows_mm(W), _rows_mm(W), _rows_mm(D, 2), _rows_mm(D, 3), rows,
                  _full((W, D)), _full((W, D)), _layer(l, (D, D))] + ([_full((1, D)), rows] if head else []),
        out_specs=[rows, rows, rows] + ([_full((2, 128)), rows, _full((1, D))] if head else [rows]),
        out_shape=[out_rows, out_rows, SDS((L, D), BF16)]
        + ([SDS((2, 128), F32), out_rows, SDS((1, D), F32)] if head else [out_rows]),
        compiler_params=_params("arbitrary"),
    )(ya, yb, proj, proj, x, wa, wb, wo, *(head or ()))


def _loss_tile(xv, g_ref, t_ref, loss_ref, dx_ref, dg_ref):
    @pl.when(pl.program_id(0) == 0)
    def _():
        loss_ref[...] = jnp.zeros_like(loss_ref)
        dg_ref[...] = jnp.zeros_like(dg_ref)

    g = g_ref[...]
    r = lax.rsqrt(jnp.mean(xv * xv, axis=-1, keepdims=True) + EPS)
    xn = xv * r
    err = xn * g - t_ref[...]
    part = jnp.sum(jnp.mean(err * err, axis=-1, keepdims=True), axis=0, keepdims=True)
    loss_ref[...] += 0.5 * part
    dy = err * (1.0 / D)
    dg_ref[...] += jnp.sum(dy * xn, axis=0, keepdims=True)
    dxn = dy * g
    dx_ref[...] = r * (dxn - xn * jnp.mean(dxn * xn, axis=-1, keepdims=True))


def _merge_out_bwd(l, dxn, mg, proj, pa, pb, ya, yb, wo, wa, wb, jobs=()):
    def body(dx_ref, mg_ref, ga_ref, gb_ref, pa_ref, pb_ref, ya_ref, yb_ref, wo_ref, wa_ref, wb_ref,
             dg_ref, dya_ref, dyb_ref, dwo_ref, dwa_ref, dwb_ref, dbias_ref):
        @pl.when(pl.program_id(0) == 0)
        def _():
            for ref in (dwo_ref, dwa_ref, dwb_ref, dbias_ref):
                ref[...] = jnp.zeros_like(ref)

        dxb = dx_ref[...].astype(BF16)
        dm = _dot_nt(dxb, wo_ref[...])
        sa = _sigmoid(ga_ref[...])
        sb = _sigmoid(gb_ref[...])
        dga = dm * pa_ref[...] * (sa * (1.0 - sa))
        dgb = dm * pb_ref[...] * (sb * (1.0 - sb))
        dg_ref[:, :D] = dga.astype(BF16)
        dg_ref[:, D:] = dgb.astype(BF16)
        dbias_ref[:, :D] += jnp.sum(dga, axis=0, keepdims=True)
        dbias_ref[:, D:] += jnp.sum(dgb, axis=0, keepdims=True)
        dpa = (dm * sa).astype(BF16)
        dpb = (dm * sb).astype(BF16)
        ya = ya_ref[...]
        yb = yb_ref[...]
        dya_ref[...] = _dot_nt(dpa, wa_ref[...])
        dyb_ref[...] = _dot_nt(dpb, wb_ref[...])
        dwa = _dot_tn(ya, dpa)
        dwb = _dot_tn(yb, dpb)
        for j in range(4):
            cs = slice(256 * j, 256 * (j + 1))
            dwa_ref[j] += dwa[:, cs]
            dwb_ref[j] += dwb[:, cs]
        dwo_ref[...] += _dot_tn(mg_ref[...], dxb)

    return _pcall(
        body, f"merge_out_bwd_l{l}", (NT,),
        [_rows(D), _rows(D), _rows(D, 2), _rows(D, 3), _rows(D), _rows(D), _rows(W), _rows(W),
         _layer(l, (D, D)), _full((W, D)), _full((W, D))],
        [_rows(2 * D, 1), _rows(W), _rows(W), _full((D, D)), _full((4, W, 256)), _full((4, W, 256)), _full((1, 2 * D))],
        [SDS((L, NIN), BF16), SDS((L, W), F32), SDS((L, W), F32),
         SDS((D, D), F32), SDS((4, W, 256), F32), SDS((4, W, 256), F32), SDS((1, 2 * D), F32)],
        (dxn, mg, proj, proj, pa, pb, ya, yb, wo, wa, wb), jobs=jobs)


def _pool_bwd(l, dyb, proj, mixed, pooled, pw, scale):
    def body(dyb_ref, zb_ref, mixed_ref, pooled_ref, pw_ref, sc_ref, db_ref, dpw_ref, dsc_ref, dbias_ref, buf):
        i = pl.program_id(0)
        tile = NT - 1 - i

        @pl.when(i == 0)
        def _():
            dpw_ref[...] = jnp.zeros_like(dpw_ref)
            dsc_ref[...] = jnp.zeros_like(dsc_ref)
            dbias_ref[...] = jnp.zeros_like(dbias_ref)
            buf[TM:TM + 16, :] = jnp.zeros((16, W), F32)

        dyb = dyb_ref[...]
        zb = zb_ref[...]
        mixed = mixed_ref[...]
        sc = sc_ref[...]
        sg = _sigmoid(zb)
        dyb0 = dyb * (zb * sg)
        dzb = dyb * (mixed * sc) * (sg * (1.0 + zb * (1.0 - sg)))
        db_ref[:, W:] = dzb.astype(BF16)
        dbias_ref[:, W:] += jnp.sum(dzb, axis=0, keepdims=True)
        dsc_ref[...] += jnp.sum(dyb0 * mixed, axis=0, keepdims=True)
        dmix = (dyb0 * sc).astype(BF16)
        t = tile * TM + lax.broadcasted_iota(jnp.int32, (TM, 128), 0)
        for gi, win in enumerate(WINS):
            cs = slice(128 * gi, 128 * (gi + 1))
            dpw_ref[gi] += _dot_tn(pooled_ref[:, cs], dmix[:, cs])
            dpool = _dot_nt(dmix[:, cs], pw_ref[gi])
            cnt = jnp.minimum(t + 1, win).astype(F32)
            e = dpool / cnt
            buf[0:TM, cs] = e
            acc = e - dpool
            for k in range(1, win):
                acc = acc + buf[k:k + TM, cs]
            db_ref[:, cs] = acc.astype(BF16)
            dbias_ref[:, cs] += jnp.sum(acc, axis=0, keepdims=True)
        buf[TM:TM + 16, :] = buf[0:16, :]

    return dict(
        body=body, width=2 * W,
        in_specs=[_rows(W, 0, True), _rows(W, 3, True), _rows(W, 0, True), _rows(W, 0, True),
                  _layer(l, (4, 128, 128)), _layer(l, (1, W))],
        out_specs=[_full((4, 128, 128)), _full((1, W)), _full((1, 2 * W))],
        out_shape=[SDS((4, 128, 128), F32), SDS((1, W), F32), SDS((1, 2 * W), F32)],
        args=(dyb, proj, mixed, pooled, pw, scale),
        scratch=[pltpu.VMEM((TM + 16, W), F32)])


def _s5_bwd(l, dya, proj, y1, q, sre, sim, cret, cimnt, bret, bimt, st_re, st_im, cr_re, cr_im, dsk, wglu):
    def halo(i):
        return (jnp.maximum((NT - 1 - i) * (TM // 8) - 1, 0), 0)

    def body(dya_ref, ua_ref, za_ref, y1_ref, q_ref, sre_ref, sim_ref, hre_ref, him_ref,
             cret_ref, cimnt_ref, bret_ref, bimt_ref, st_re_ref, st_im_ref, cr_re_ref, cr_im_ref, d_ref, wg_ref,
             da_ref, dwg_ref, dbg_ref, dd_ref, dcre_ref, dcimn_ref, dbre_ref, dbim_ref, dare_ref, daim_ref, dbias_ref,
             lre, lim, car_ref):
        i = pl.program_id(0)
        tile = NT - 1 - i

        @pl.when(i == 0)
        def _():
            for ref in (dwg_ref, dbg_ref, dd_ref, dcre_ref, dcimn_ref, dbre_ref, dbim_ref, dare_ref, daim_ref, dbias_ref,
                        car_ref):
                ref[...] = jnp.zeros_like(ref)

        u = ua_ref[...]
        za = za_ref[...]
        y1 = y1_ref[...]
        dya = dya_ref[...]
        y2 = _gelu(y1)
        sg = _sigmoid(q_ref[...])
        sgz = _sigmoid(za)
        dy3 = dya * (za * sgz)
        dza = dya * (y2 * sg) * (sgz * (1.0 + za * (1.0 - sgz)))
        da_ref[:, W:] = dza.astype(BF16)
        dbias_ref[:, W:] += jnp.sum(dza, axis=0, keepdims=True)
        dq = dy3 * y2 * (sg * (1.0 - sg))
        dqb = dq.astype(BF16)
        dy2 = dy3 * sg + _dot_nt(dqb, wg_ref[...])
        dwg_ref[...] += _dot_tn(y2.astype(BF16), dqb)
        dbg_ref[...] += jnp.sum(dq, axis=0, keepdims=True)
        dy1 = dy2 * _gelu_grad(y1)
        dd_ref[...] += jnp.sum(dy1 * u, axis=0, keepdims=True)
        dy1b = dy1.astype(BF16)
        ub = u.astype(BF16)
        for k in range(4):
            blk = slice(512 * k, 512 * (k + 1))
            ks = slice(128 * k, 128 * (k + 1))
            lre[:, blk] = _dot(dy1b[:, ks], cret_ref[k])
            lim[:, blk] = _dot(dy1b[:, ks], cimnt_ref[k])
            dcre_ref[k] += _dot_tn(sre_ref[:, blk].astype(BF16), dy1b[:, ks])
            dcimn_ref[k] += _dot_tn(sim_ref[:, blk].astype(BF16), dy1b[:, ks])
        rowid = lax.broadcasted_iota(jnp.int32, (8, 512), 0)
        gate = (tile > 0).astype(F32)

        def d_abar(c, cols, lr, li):
            rows = pl.ds(pl.multiple_of(c * 8, 8), 8)
            prows = pl.ds(pl.multiple_of(jnp.maximum(c - 1, 0) * 8, 8), 8)
            pr = jnp.where(c == 0, hre_ref[7:8, cols] * gate, sre_ref[prows, cols][7:8])
            pi = jnp.where(c == 0, him_ref[7:8, cols] * gate, sim_ref[prows, cols][7:8])
            sr = jnp.where(rowid == 0, pr, pltpu.roll(sre_ref[rows, cols], 1, 0))
            si = jnp.where(rowid == 0, pi, pltpu.roll(sim_ref[rows, cols], 1, 0))
            dare_ref[:, cols] += sr * lr + si * li
            daim_ref[:, cols] += sr * li - si * lr

        carry = _scan_tile(lre, lim, st_re_ref, st_im_ref, cr_re_ref, cr_im_ref, _load_carry(car_ref), True, d_abar)
        _store_carry(car_ref, carry)

        for k in range(4):
            blk = slice(512 * k, 512 * (k + 1))
            ks = slice(128 * k, 128 * (k + 1))
            lrb = lre[:, blk].astype(BF16)
            lib = lim[:, blk].astype(BF16)
            du = dy1[:, ks] * d_ref[:, ks] + _dot(lrb, bret_ref[k]) + _dot(lib, bimt_ref[k])
            da_ref[:, ks] = du.astype(BF16)
            dbias_ref[:, ks] += jnp.sum(du, axis=0, keepdims=True)
            dbre_ref[k] += _dot_tn(ub[:, ks], lrb)
            dbim_ref[k] += _dot_tn(ub[:, ks], lib)

    return dict(
        body=body, width=2 * W,
        in_specs=[_rows(W, 0, True), _rows(W, 0, True), _rows(W, 1, True), _rows(W, 0, True), _rows(W, 0, True),
                  _rows(GP, 0, True), _rows(GP, 0, True),
                  pl.BlockSpec((8, GP), halo), pl.BlockSpec((8, GP), halo),
                  _layer(l, (4, 128, 512)), _layer(l, (4, 128, 512)), _layer(l, (4, 512, 128)), _layer(l, (4, 512, 128)),
                  _layer(l, (24, GP)), _layer(l, (24, GP)), _layer(l, (8, GP)), _layer(l, (8, GP)), _layer(l, (1, W)),
                  _layer(l, (W, W))],
        out_specs=[_full((W, W)), _full((1, W)), _full((1, W)),
                   _full((4, 512, 128)), _full((4, 512, 128)), _full((4, 128, 512)), _full((4, 128, 512)),
                   _full((8, GP)), _full((8, GP)), _full((1, 2 * W))],
        out_shape=[SDS((W, W), F32), SDS((1, W), F32), SDS((1, W), F32),
                   SDS((4, 512, 128), F32), SDS((4, 512, 128), F32), SDS((4, 128, 512), F32), SDS((4, 128, 512), F32),
                   SDS((8, GP), F32), SDS((8, GP), F32), SDS((1, 2 * W), F32)],
        args=(dya, proj, proj, y1, q, sre, sim, sre, sim, cret, cimnt, bret, bimt, st_re, st_im, cr_re, cr_im, dsk,
              wglu),
        scratch=[pltpu.VMEM((TM, GP), F32), pltpu.VMEM((TM, GP), F32), pltpu.VMEM((8, GP), F32)])


def _inproj_dw(l, r, h, dproj, jobs=(), rows=D // 2):
    def body(h_ref, dp_ref, dw_ref):
        part = _dot_tn(h_ref[...], dp_ref[...])

        @pl.when(pl.program_id(1) == 0)
        def _():
            dw_ref[...] = part

        @pl.when(pl.program_id(1) > 0)
        def _():
            dw_ref[...] += part

    return _pcall(
        body, f"inproj_dw{r}_l{l}", (4, L // TK),
        [pl.BlockSpec((TK, rows), lambda j, i: (i, r)), pl.BlockSpec((TK, 1024), lambda j, i: (i, j))],
        [pl.BlockSpec((None, rows, 1024), lambda j, i: (j, 0, 0))],
        [SDS((4, rows, 1024), F32)],
        (h, dproj), jobs=jobs)


def _inproj_dx(l, dproj, w, x, g, dxn, jobs=()):
    def body(dp_ref, w_ref, x_ref, g_ref, dxn_ref, dx_ref, dg_ref):
        @pl.when(pl.program_id(0) == 0)
        def _():
            dg_ref[...] = jnp.zeros_like(dg_ref)

        dh = _dot_nt(dp_ref[:, 0:1024], w_ref[0])
        for j in range(1, 4):
            dh = dh + _dot_nt(dp_ref[:, j * 1024:(j + 1) * 1024], w_ref[j])
        xv = x_ref[...]
        r = lax.rsqrt(jnp.mean(xv * xv, axis=-1, keepdims=True) + EPS)
        xn = xv * r
        dg_ref[...] += jnp.sum(dh * xn, axis=0, keepdims=True)
        dn = dh * g_ref[...]
        dx_ref[...] = dxn_ref[...] + r * (dn - xn * jnp.mean(dn * xn, axis=-1, keepdims=True))

    return _pcall(
        body, f"inproj_dx_l{l}", (L // TMM,),
        [_rows_mm(NIN), _layer(l, (4, D, 1024)), _rows_mm(D), _layer(l, (1, D)), _rows_mm(D)],
        [_rows_mm(D), _full((1, D))],
        [SDS((L, D), F32), SDS((1, D), F32)],
        (dproj, w, x, g, dxn), jobs=jobs)


def _discretize(log_dt, lam_re, lam_im, b_re, b_im):
    dt = jnp.exp(log_dt)[..., None]
    mag = jnp.exp(lam_re * dt)
    ang = lam_im * dt
    abar_re = mag * jnp.cos(ang)
    abar_im = mag * jnp.sin(ang)
    num_re = abar_re - 1.0
    num_im = abar_im
    den = lam_re * lam_re + lam_im * lam_im
    coef_re = (num_re * lam_re + num_im * lam_im) / den
    coef_im = (num_im * lam_re - num_re * lam_im) / den
    bbar_re = coef_re[..., None] * b_re - coef_im[..., None] * b_im
    bbar_im = coef_re[..., None] * b_im + coef_im[..., None] * b_re
    return abar_re, abar_im, bbar_re, bbar_im


def _powers(abar_re, abar_im):
    ar, ai = abar_re.reshape(DEPTH, 1, GP), abar_im.reshape(DEPTH, 1, GP)
    rows_re, rows_im = [ar], [ai]
    for _ in range(7):
        pr, pi = rows_re[-1], rows_im[-1]
        rows_re.append(pr * ar - pi * ai)
        rows_im.append(pr * ai + pi * ar)
    row = jnp.arange(8)[:, None]

    def steps(rows, keep):
        return jnp.concatenate([jnp.where(keep(d), rows[d - 1], 0.0) for d in (1, 2, 4)], axis=1)

    neg_im = [-r for r in rows_im]
    fwd = (steps(rows_re, lambda d: row >= d), steps(rows_im, lambda d: row >= d),
           jnp.concatenate(rows_re, axis=1), jnp.concatenate(rows_im, axis=1))
    rev = (steps(rows_re, lambda d: row < 8 - d), steps(neg_im, lambda d: row < 8 - d),
           jnp.concatenate(rows_re[::-1], axis=1), jnp.concatenate(neg_im[::-1], axis=1))
    return fwd, rev


_EYE8 = functools.partial(jnp.eye, 8, dtype=F32)


def _expand_in(b):
    return jnp.einsum("lkgpc,gh->lkgchp", b.reshape(DEPTH, 4, 8, P, C), _EYE8()).reshape(DEPTH, 4, 128, 512)


def _extract_in(e):
    return jnp.einsum("lkgchp,gh->lkgpc", e.reshape(DEPTH, 4, 8, C, 8, P), _EYE8()).reshape(DEPTH, G, P, C)


def _expand_out(c):
    return jnp.einsum("lkgcp,gh->lkgphc", c.reshape(DEPTH, 4, 8, C, P), _EYE8()).reshape(DEPTH, 4, 512, 128)


def _extract_out(e):
    return jnp.einsum("lkgphc,gh->lkgcp", e.reshape(DEPTH, 4, 8, P, 8, C), _EYE8()).reshape(DEPTH, G, C, P)


SMALL = ("norm_g", "b_in", "ssm_log_dt", "ssm_lam_re", "ssm_lam_im", "ssm_b_re", "ssm_b_im",
         "ssm_c_re", "ssm_c_im", "ssm_d", "ssm_b_glu", "pool_w", "pool_scale")
BIG = ("w_in", "ssm_w_glu", "w_branch_a", "w_branch_b", "w_out")


def _step(x, target, w, m, v, place):
    sp = {n: w[n] for n in SMALL}
    final_norm_g = w["final_norm_g"]
    wbuf = dict(zip(BIG, _cast_own(place, [w[n] for n in BIG])))
    (abar_re, abar_im, bbar_re, bbar_im), disc_vjp = jax.vjp(
        _discretize, *(sp[n] for n in ("ssm_log_dt", "ssm_lam_re", "ssm_lam_im", "ssm_b_re", "ssm_b_im")))
    powers_fwd, powers_rev = _powers(abar_re, abar_im)
    b_re_x, b_im_x = _expand_in(bbar_re), _expand_in(bbar_im)
    c_re_x, c_imn_x = _expand_out(sp["ssm_c_re"]), _expand_out(-sp["ssm_c_im"])
    b_x = jnp.concatenate([b_re_x, b_im_x], axis=3).astype(BF16)
    t = lambda a: jnp.swapaxes(a, 2, 3).astype(BF16)
    c_re_t, c_imn_t, b_re_t, b_im_t = t(c_re_x), t(c_imn_x), t(b_re_x), t(b_im_x)
    c_re_x, c_imn_x = c_re_x.astype(BF16), c_imn_x.astype(BF16)
    row = lambda n: sp[n].reshape(DEPTH, 1, -1)
    g, b_in, dsk, b_glu, scale = row("norm_g"), row("b_in"), row("ssm_d"), row("ssm_b_glu"), row("pool_scale")
    pw = sp["pool_w"].astype(BF16)

    saved = []
    for l in range(DEPTH):
        three = BIG[2:]
        if l == 0:
            wbuf["w_in"], wbuf["ssm_w_glu"] = _comm_only(
                "gather_first", _RingGatherJob([wbuf["w_in"], wbuf["ssm_w_glu"]], 0))[0]
            jobs = [_GatherJob([wbuf[n] for n in three], 0), _GatherJob([wbuf["ssm_w_glu"]], 1)]
        else:
            jobs = []
        (h, proj), res = _norm_inproj(l, x, g, wbuf["w_in"], b_in, jobs)
        if res:
            wbuf.update(zip(three, res[0][0]))
            (wbuf["ssm_w_glu"],) = res[1][0]
        wg = dict(wbuf, ssm_w_glu=wbuf["ssm_w_glu"].reshape(DEPTH, W, W), w_out=wbuf["w_out"].reshape(DEPTH, D, D))
        job = _GatherJob([wbuf["w_in"]], l + 1) if l + 1 < DEPTH else _GatherJob([wbuf[n] for n in three], l)
        (sre, sim, y1, q, ya, pooled, mixed, yb), res = _fused(
            f"branches_fwd_l{l}", (NT,),
            [_s5_fwd(l, proj, b_x, c_re_x, c_imn_x, powers_fwd, dsk, wg["ssm_w_glu"], b_glu),
             _pool_fwd(l, proj, pw, scale)], jobs=[job])
        if l + 1 < DEPTH:
            (wbuf["w_in"],) = res[0][0]
        else:
            wbuf.update(zip(three, res[0][0]))
        wg = dict(wbuf, ssm_w_glu=wbuf["ssm_w_glu"].reshape(DEPTH, W, W), w_out=wbuf["w_out"].reshape(DEPTH, D, D))
        last = l + 1 == DEPTH
        wa, wb = (wg[n][l].transpose(1, 0, 2).reshape(W, D) for n in ("w_branch_a", "w_branch_b"))
        pa, pb, mg, *tail = _merge_out(l, ya, yb, proj, x, wa, wb, wg["w_out"],
                                       (final_norm_g.reshape(1, D), target) if last else None)
        saved.append(dict(x=x, h=h, proj=proj, sre=sre, sim=sim, y1=y1, q=q, ya=ya, wa=wa, wb=wb,
                          pooled=pooled, mixed=mixed, yb=yb, pa=pa, pb=pb, mg=mg))
        if last:
            loss, dx, dgf = tail
        else:
            (x,) = tail

    per_layer = {n: [None] * DEPTH for n in ("norm_g", "b_in", "ssm_d", "ssm_b_glu", "pool_w", "pool_scale",
                                             "dare", "daim", "dbre", "dbim", "dcre", "dcimn")}
    red = _Reducer(place, w, m, v)
    for l in reversed(range(DEPTH)):
        s = saved[l]
        (dproj, dya, dyb, dwo, dwa, dwb, dbias_g), res = _merge_out_bwd(
            l, dx, s["mg"], s["proj"], s["pa"], s["pb"], s["ya"], s["yb"],
            wg["w_out"], s["wa"], s["wb"], red.jobs())
        red.land(res)
        (dproj, dwg, dbg, dd, dcre, dcimn, dbre, dbim, dare, daim, dbias_a, dpw, dsc, dbias_b), res = _fused(
            f"branches_bwd_l{l}", (NT,),
            [_s5_bwd(l, dya, s["proj"], s["y1"], s["q"], s["sre"], s["sim"], c_re_t, c_imn_t, b_re_t, b_im_t,
                     *powers_rev, dsk, wg["ssm_w_glu"]),
             _pool_bwd(l, dyb, s["proj"], s["mixed"], s["pooled"], pw, scale)],
            shared=(dproj, _rows(4 * W, 0, True)), jobs=red.jobs())
        red.land(res)
        rest = [dwg.reshape(4, W // 4, W), dwa, dwb, dwo.reshape(4, D // 4, D)]
        if l == 0:
            red.add(l, "rest", BIG[1:], rest)
        if l == 0:
            for r, rows in ((0, D - LAST_ROWS), (D // LAST_ROWS - 1, LAST_ROWS)):
                outs, res = _inproj_dw(l, r, s["h"], dproj, red.jobs(), rows=rows)
                red.land(res)
                red.add(l, f"in{r}", BIG[:1], outs, [r * rows])
        else:
            dwin, res = _inproj_dw(l, 0, s["h"], dproj, red.jobs(), rows=D)
            red.land(res)
        (dx, dg), res = _inproj_dx(l, dproj, wg["w_in"], s["x"], g, dx, red.jobs())
        red.land(res)
        if l > 0:
            red.add(l, "all", BIG, dwin + rest)
        for n, a in (("norm_g", dg.reshape(D)), ("b_in", jnp.concatenate([dbias_a, dbias_b, dbias_g], axis=1).reshape(NIN)),
                     ("ssm_d", dd.reshape(W)), ("ssm_b_glu", dbg.reshape(W)), ("pool_w", dpw), ("pool_scale", dsc.reshape(W)),
                     ("dare", dare), ("daim", daim), ("dbre", dbre), ("dbim", dbim), ("dcre", dcre), ("dcimn", dcimn)):
            per_layer[n][l] = a
    gs = {n: jnp.stack(a) for n, a in per_layer.items()}
    d_abar = [jnp.sum(gs.pop(n), axis=1).reshape(DEPTH, G, P) for n in ("dare", "daim")]
    (gs["ssm_log_dt"], gs["ssm_lam_re"], gs["ssm_lam_im"], gs["ssm_b_re"], gs["ssm_b_im"]) = disc_vjp(
        (*d_abar, _extract_in(gs.pop("dbre")), _extract_in(gs.pop("dbim"))))
    gs["ssm_c_re"], gs["ssm_c_im"] = _extract_out(gs.pop("dcre")), -_extract_out(gs.pop("dcimn"))
    gs["final_norm_g"] = dgf

    natural = {n: w[n].shape for n in REPLICATED}
    rw, rm, rv = {}, {}, {}
    for n in REPLICATED:
        shape = DENSE.get(n, natural[n])
        gs[n], rw[n], rm[n], rv[n] = (a.reshape(shape) for a in (gs[n], w[n], m[n], v[n]))
    small = [gs[n] for n in REPLICATED] + [loss]
    jobs = red.jobs()
    res = _pcall(None, "tail_exchange", (), [], [], [], [], jobs=jobs + [_SiblingJob(small, False)])[1]
    red.land(res[:len(jobs)])
    pair_small = _small_pair_sum(place, small, res[-1][1], [BF16 if a.ndim > 2 else F32 for a in small])
    jobs = red.jobs()
    res = _pcall(None, "tail_gather", (), [], [], [], [], jobs=jobs + [_ChipGatherJob(pair_small)])[1]
    red.land(res[:len(jobs)])
    assert not red.active
    small_parts = dict(zip(REPLICATED + ("loss",), res[-1][0]))

    k = len(REPLICATED)
    outs = _adamw_small("adamw_small", [rw[n] for n in REPLICATED], [small_parts[n] for n in REPLICATED],
                        [rm[n] for n in REPLICATED], [rv[n] for n in REPLICATED], small_parts["loss"])
    results = {n: red.big[n] for n in BIG}
    results.update({n: [outs[1 + q * k + i].reshape(natural[n]) for q in range(4)] for i, n in enumerate(REPLICATED)})
    return outs[0][0, 0], dx, results


def _place():
    x, y, c = lax.axis_index("x"), lax.axis_index("y"), lax.axis_index("c")
    chips = [(1 - x, y), (x, 1 - y), (1 - x, 1 - y)]
    return x, y, c, 2 * x + y, chips, [2 * cx + cy for cx, cy in chips]


def _remote(src, dst, ssem, rsem, dev):
    return pltpu.make_async_remote_copy(src_ref=src, dst_ref=dst, send_sem=ssem, recv_sem=rsem,
                                        device_id=dev, device_id_type=MESH)


class _GatherJob:
    def __init__(self, bufs, l):
        self.srcs, self.bufs, self.news, self.l = [], list(bufs), [], l
        self.scratch = [pltpu.SemaphoreType.DMA((len(self.bufs), 3))] * 4

    def _half(self, ref, k, h):
        rows = ref.shape[2] // 2
        return ref.at[self.l, k, pl.ds(pl.multiple_of(h * rows, 8), rows), :]

    def _ici(self, bufs, sems, a, j, k):
        _, _, c, _, chips, _ = _place()
        blk = self._half(bufs[a], k, c)
        return _remote(blk, blk, sems[0].at[a, j], sems[1].at[a, j], (*chips[j], c))

    def _d2d(self, bufs, sems, a, j, k, h):
        x, y, c, _, _, _ = _place()
        blk = self._half(bufs[a], k, h)
        return _remote(blk, blk, sems[2].at[a, j], sems[3].at[a, j], (x, y, 1 - c))

    def start(self, srcs, bufs, news, sems):
        me = _place()[3]
        for a in range(len(self.bufs)):
            for j in range(3):
                self._ici(bufs, sems, a, j, me).start()

    def finish(self, srcs, bufs, news, sems):
        _, _, c, me, _, cid = _place()
        pairs = [(a, j) for a in range(len(self.bufs)) for j in range(3)]
        for a, j in pairs:
            self._ici(bufs, sems, a, j, cid[j]).wait_recv()
            self._d2d(bufs, sems, a, j, cid[j], c).start()
        for a, j in pairs:
            self._d2d(bufs, sems, a, j, cid[j], 1 - c).wait_recv()
        for a, j in pairs:
            self._ici(bufs, sems, a, j, me).wait_send()
            self._d2d(bufs, sems, a, j, cid[j], c).wait_send()


class _RingGatherJob(_GatherJob):
    def __init__(self, bufs, l):
        super().__init__(bufs, l)
        n = len(self.bufs)
        self.scratch = [pltpu.SemaphoreType.DMA((n, 2))] * 4 + [pltpu.SemaphoreType.DMA((n, 4))] * 2

    def _rows(self, ref, k, h, part):
        half = ref.shape[2] // 2
        start, rows = (h * half, half) if part is None else (h * half + part * (half // 2), half // 2)
        return ref.at[self.l, k, pl.ds(pl.multiple_of(start, 8), rows), :]

    def _to_chip(self, bufs, sems, base, a, j, k, part):
        _, _, c, _, chips, _ = _place()
        blk = self._rows(bufs[a], k, c, part)
        return _remote(blk, blk, sems[base].at[a, j], sems[base + 1].at[a, j], (*chips[j], c))

    def _to_sibling(self, bufs, sems, a, i, k, h, part):
        x, y, c, _, _, _ = _place()
        blk = self._rows(bufs[a], k, h, part)
        return _remote(blk, blk, sems[4].at[a, i], sems[5].at[a, i], (x, y, 1 - c))

    def start(self, srcs, bufs, news, sems):
        me = _place()[3]
        for a in range(len(self.bufs)):
            for j in range(2):
                self._to_chip(bufs, sems, 0, a, j, me, None).start()

    def finish(self, srcs, bufs, news, sems):
        _, _, c, me, _, cid = _place()
        arrays = range(len(self.bufs))
        for a in arrays:
            for j in (1, 0):
                self._to_chip(bufs, sems, 0, a, j, cid[j], None).wait_recv()
                self._to_chip(bufs, sems, 2, a, 1 - j, cid[j], 1 - j).start()
                self._to_sibling(bufs, sems, a, j, cid[j], c, None).start()
        for a in arrays:
            for part in range(2):
                self._to_chip(bufs, sems, 2, a, part, cid[2], part).wait_recv()
                self._to_sibling(bufs, sems, a, 2 + part, cid[2], c, part).start()
        for a in arrays:
            for j in range(2):
                self._to_sibling(bufs, sems, a, j, cid[j], 1 - c, None).wait_recv()
                self._to_sibling(bufs, sems, a, 2 + j, cid[2], 1 - c, j).wait_recv()
        for a in arrays:
            for j in range(2):
                self._to_chip(bufs, sems, 0, a, j, me, None).wait_send()
                self._to_chip(bufs, sems, 2, a, 1 - j, cid[j], 1 - j).wait_send()
                self._to_sibling(bufs, sems, a, j, cid[j], c, None).wait_send()
                self._to_sibling(bufs, sems, a, 2 + j, cid[2], c, j).wait_send()


class _SiblingJob:
    def __init__(self, srcs, rows_half):
        self.srcs, self.bufs, self.rows_half = list(srcs), [], rows_half
        self.news = [SDS((s.shape[0], s.shape[1] // 2, s.shape[2]) if rows_half else s.shape, s.dtype) for s in srcs]
        self.scratch = [pltpu.SemaphoreType.DMA((len(self.srcs),))] * 2

    def _copy(self, srcs, news, sems, a):
        x, y, c, _, _, _ = _place()
        src = srcs[a]
        if self.rows_half:
            rows = src.shape[1] // 2
            src = src.at[:, pl.ds(pl.multiple_of((1 - c) * rows, 8), rows), :]
        return _remote(src, news[a], sems[0].at[a], sems[1].at[a], (x, y, 1 - c))

    def start(self, srcs, bufs, news, sems):
        for a in range(len(self.srcs)):
            self._copy(srcs, news, sems, a).start()

    def finish(self, srcs, bufs, news, sems):
        for a in range(len(self.srcs)):
            self._copy(srcs, news, sems, a).wait()


class _ScatterJob:
    def __init__(self, parts):
        self.srcs, self.bufs = list(parts), []
        self.news = [SDS((3,) + p.shape[1:], p.dtype) for p in parts]
        self.scratch = [pltpu.SemaphoreType.DMA((len(self.srcs), 3))] * 2

    def _copy(self, srcs, news, sems, a, j):
        _, _, c, _, chips, cid = _place()
        return _remote(srcs[a].at[cid[j]], news[a].at[j], sems[0].at[a, j], sems[1].at[a, j], (*chips[j], c))

    def start(self, srcs, bufs, news, sems):
        for a in range(len(self.srcs)):
            for j in range(3):
                self._copy(srcs, news, sems, a, j).start()

    def finish(self, srcs, bufs, news, sems):
        for a in range(len(self.srcs)):
            for j in range(3):
                self._copy(srcs, news, sems, a, j).wait()


def _comm_only(name, job):
    return _pcall(None, name, (), [], [], [], [], jobs=[job])[1][0]


class _ChipGatherJob(_GatherJob):
    def __init__(self, bufs):
        super().__init__(bufs, None)

    def _half(self, ref, k, h):
        return ref.at[k, h]


def _cast_own(place, ws):
    n = len(ws)

    def body(p_ref, *refs):
        for i_ref, o_ref in zip(refs[:n], refs[n:]):
            o_ref[...] = i_ref[...].astype(BF16)

    return pl.pallas_call(
        body, name="cast_own_shards",
        grid_spec=pltpu.PrefetchScalarGridSpec(
            num_scalar_prefetch=1, grid=(DEPTH,),
            in_specs=[pl.BlockSpec((None,) + a.shape[1:], lambda l, p: (l, 0, 0)) for a in ws],
            out_specs=[pl.BlockSpec((None, None) + a.shape[1:], lambda l, p: (l, p[1], 0, 0)) for a in ws]),
        out_shape=[SDS((DEPTH, 4) + a.shape[1:], BF16) for a in ws],
        compiler_params=_params("arbitrary"),
    )(place, *ws)


def _half_tiles(a_):
    rows = a_ // 2
    ta = next(t for t in (256, 128, 64, 32, 16, 8) if rows % t == 0)
    return rows, ta, rows // ta


def _pair_sums_bf16(name, place, owns, recvs):
    n = len(owns)

    def body(p_ref, *refs):
        for own_ref, recv_ref, out_ref in zip(refs[:n], refs[n:2 * n], refs[2 * n:]):
            out_ref[...] = (own_ref[...] + recv_ref[...]).astype(BF16)

    def own_half(a):
        return pl.BlockSpec((None, a.shape[1] // 2, a.shape[2]), lambda s, p: (s, p[0], 0))

    def block(a):
        return pl.BlockSpec((None,) + a.shape[1:], lambda s, p: (s, 0, 0))

    return pl.pallas_call(
        body, name=name,
        grid_spec=pltpu.PrefetchScalarGridSpec(
            num_scalar_prefetch=1, grid=(4,),
            in_specs=[own_half(a) for a in owns] + [block(r) for r in recvs],
            out_specs=[block(r) for r in recvs]),
        out_shape=[SDS(r.shape, BF16) for r in recvs],
        compiler_params=_params("arbitrary"),
    )(place, *owns, *recvs)


def _shard_sums(name, place, owns, recvs, rbufs):
    n = len(owns)

    def body(p_ref, *refs):
        for own_ref, recv_ref, r_ref, out_ref in zip(refs[:n], refs[n:2 * n], refs[2 * n:3 * n], refs[3 * n:]):
            acc = own_ref[...] + recv_ref[...]
            for j in range(3):
                acc = acc + r_ref[j].astype(F32)
            out_ref[...] = acc

    def own_half(a):
        return pl.BlockSpec((None, a.shape[1] // 2, a.shape[2]), lambda i, p: (p[1], p[0], 0))

    def recv_block(a):
        return pl.BlockSpec((None,) + a.shape[1:], lambda i, p: (p[1], 0, 0))

    return pl.pallas_call(
        body, name=name,
        grid_spec=pltpu.PrefetchScalarGridSpec(
            num_scalar_prefetch=1, grid=(1,),
            in_specs=([own_half(a) for a in owns] + [recv_block(r) for r in recvs]
                      + [pl.BlockSpec(rb.shape, lambda i, p: (0, 0, 0)) for rb in rbufs]),
            out_specs=[pl.BlockSpec(r.shape[1:], lambda i, p: (0, 0)) for r in recvs]),
        out_shape=[SDS(r.shape[1:], F32) for r in recvs],
        compiler_params=_params("arbitrary"),
    )(place, *owns, *recvs, *rbufs)


def _small_pair_sum(place, mine, recv, dtypes):
    n = len(mine)

    def body(p_ref, *refs):
        for m_ref, r_ref, o_ref in zip(refs[:n], refs[n:2 * n], refs[2 * n:]):
            o_ref[...] = (m_ref[...] + r_ref[...]).astype(o_ref.dtype)

    def whole(a):
        zeros = (0,) * a.ndim
        return pl.BlockSpec(a.shape, lambda i, p: zeros)

    def mine_blk(a):
        zeros = (0,) * a.ndim
        return pl.BlockSpec((None,) + a.shape, lambda i, p: (p[1],) + zeros)

    return pl.pallas_call(
        body, name="small_pair_sum",
        grid_spec=pltpu.PrefetchScalarGridSpec(
            num_scalar_prefetch=1, grid=(1,),
            in_specs=[whole(a) for a in mine] + [whole(a) for a in recv],
            out_specs=[mine_blk(a) for a in mine]),
        out_shape=[SDS((4,) + a.shape, dt) for a, dt in zip(mine, dtypes)],
        compiler_params=_params("arbitrary"),
    )(place, *mine, *recv)


def _adam_math(w, g, m, v):
    m = B1 * m + (1.0 - B1) * g
    v = B2 * v + (1.0 - B2) * (g * g)
    m_hat = m / (1.0 - B1 ** STEP)
    v_hat = v / (1.0 - B2 ** STEP)
    delta = -LR * (m_hat / (jnp.sqrt(v_hat) + EPS_A) + WD * w)
    return delta, m, v


def _adamw_big(name, l, row0, w, m, v, mine, other, prev, jobs=()):
    _, _, b_ = w.shape
    _, ta, nh = _half_tiles(2 * mine.shape[0])
    prev = list(prev or [])

    def body(w_ref, m_ref, v_ref, mine_ref, other_ref, *rest):
        g_ref, d_ref, mo_ref, vo_ref = rest[len(prev):]
        g = jnp.where(pl.program_id(0) == lax.axis_index("c"), mine_ref[...], other_ref[...])
        g_ref[...] = g
        d_ref[...], mo_ref[...], vo_ref[...] = _adam_math(w_ref[...], g, m_ref[...], v_ref[...])

    slab = pl.BlockSpec((None, ta, b_), lambda h, i: (l, row0 // ta + h * nh + i, 0))
    half = pl.BlockSpec((ta, b_), lambda h, i: (i, 0))
    outs, res = _pcall(
        body, name, (2, nh), [slab, slab, slab, half, half] + [_ANY] * len(prev), [slab] * 4, [SDS(w.shape, F32)] * 4,
        (w, m, v, mine, other, *prev), aliases={5 + k: k for k in range(len(prev))}, jobs=jobs)
    return outs, res


def _adamw_small(name, ws, parts, ms, vs, loss_parts=None):
    k = len(ws)
    extra = [] if loss_parts is None else [loss_parts]

    def chip_sum(p_ref):
        p = [p_ref[k].astype(F32) for k in range(4)]
        return ((p[0] + p[1]) + p[2]) + p[3]

    def body(*refs):
        w_refs, p_refs, m_refs, v_refs = refs[:k], refs[k:2 * k], refs[2 * k:3 * k], refs[3 * k:4 * k]
        outs = refs[4 * k + len(extra):]
        if extra:
            outs[0][...] = chip_sum(refs[4 * k])
            outs = outs[1:]
        for a in range(k):
            g = chip_sum(p_refs[a])
            outs[a][...] = g
            outs[k + a][...], outs[2 * k + a][...], outs[3 * k + a][...] = _adam_math(
                w_refs[a][...], g, m_refs[a][...], v_refs[a][...])

    like = [SDS(a.shape, F32) for a in ws]
    return pl.pallas_call(
        body, name=name,
        out_shape=([SDS(loss_parts.shape[1:], F32)] if extra else []) + like * 4,
        compiler_params=pltpu.CompilerParams(vmem_limit_bytes=VMEM_LIMIT),
    )(*ws, *parts, *ms, *vs, *extra)


class _Reducer:
    def __init__(self, place, w, m, v):
        self.place, self.w, self.m, self.v = place, w, m, v
        self.active, self.riding = [], []
        self.big = {n: None for n in BIG}

    def add(self, l, tag, names, own, row0s=None):
        self.active.append(dict(l=l, key=f"{tag}_l{l}", names=names, own=list(own), row0s=row0s or [0] * len(names),
                                stage=0))

    def jobs(self):
        self.riding = list(self.active)
        return [(_SiblingJob(g["own"], True), _ScatterJob(g.get("parts", [])), _SiblingJob(g.get("shard", []), False))
                [g["stage"]] for g in self.riding]

    def land(self, res):
        for g, (_, news) in zip(self.riding, res):
            if g["stage"] == 0:
                g["recv"] = news
                g["parts"] = _pair_sums_bf16(f"pair_sums_{g['key']}", self.place, g["own"], news)
            elif g["stage"] == 1:
                g["shard"] = _shard_sums(f"shard_sums_{g['key']}", self.place, g["own"], g["recv"], news)
            else:
                for n, mine, other, row0 in zip(g["names"], g["shard"], news, g["row0s"]):
                    self.big[n] = _adamw_big(f"adamw_{n}_{row0}_{g['key']}", g["l"], row0, self.w[n], self.m[n], self.v[n],
                                             mine, other, self.big[n])[0]
                self.active.remove(g)
            g["stage"] += 1
        self.riding = []


WEIGHTS = ("norm_g", "w_in", "b_in", "ssm_log_dt", "ssm_lam_re", "ssm_lam_im", "ssm_b_re", "ssm_b_im", "ssm_c_re",
           "ssm_c_im", "ssm_d", "ssm_w_glu", "ssm_b_glu", "pool_w", "pool_scale", "w_branch_a", "w_branch_b", "w_out",
           "final_norm_g")
REPLICATED = SMALL + ("final_norm_g",)
DENSE = {"ssm_b_re": (DEPTH, G, P * C), "ssm_b_im": (DEPTH, G, P * C), "final_norm_g": (2, D // 2)}


def kernel(x, norm_g, w_in, b_in, ssm_log_dt, ssm_lam_re, ssm_lam_im, ssm_b_re, ssm_b_im, ssm_c_re, ssm_c_im, ssm_d, ssm_w_glu, ssm_b_glu, pool_w, pool_scale, w_branch_a, w_branch_b, w_out, final_norm_g, loss_target, m_norm_g, m_w_in, m_b_in, m_ssm_log_dt, m_ssm_lam_re, m_ssm_lam_im, m_ssm_b_re, m_ssm_b_im, m_ssm_c_re, m_ssm_c_im, m_ssm_d, m_ssm_w_glu, m_ssm_b_glu, m_pool_w, m_pool_scale, m_w_branch_a, m_w_branch_b, m_w_out, m_final_norm_g, v_norm_g, v_w_in, v_b_in, v_ssm_log_dt, v_ssm_lam_re, v_ssm_lam_im, v_ssm_b_re, v_ssm_b_im, v_ssm_c_re, v_ssm_c_im, v_ssm_d, v_ssm_w_glu, v_ssm_b_glu, v_pool_w, v_pool_scale, v_w_branch_a, v_w_branch_b, v_w_out, v_final_norm_g):
    w = dict(zip(WEIGHTS, (norm_g, w_in, b_in, ssm_log_dt, ssm_lam_re, ssm_lam_im, ssm_b_re, ssm_b_im, ssm_c_re,
                           ssm_c_im, ssm_d, ssm_w_glu, ssm_b_glu, pool_w, pool_scale, w_branch_a, w_branch_b, w_out,
                           final_norm_g)))
    m = dict(zip(WEIGHTS, (m_norm_g, m_w_in, m_b_in, m_ssm_log_dt, m_ssm_lam_re, m_ssm_lam_im, m_ssm_b_re, m_ssm_b_im,
                           m_ssm_c_re, m_ssm_c_im, m_ssm_d, m_ssm_w_glu, m_ssm_b_glu, m_pool_w, m_pool_scale,
                           m_w_branch_a, m_w_branch_b, m_w_out, m_final_norm_g)))
    v = dict(zip(WEIGHTS, (v_norm_g, v_w_in, v_b_in, v_ssm_log_dt, v_ssm_lam_re, v_ssm_lam_im, v_ssm_b_re, v_ssm_b_im,
                           v_ssm_c_re, v_ssm_c_im, v_ssm_d, v_ssm_w_glu, v_ssm_b_glu, v_pool_w, v_pool_scale,
                           v_w_branch_a, v_w_branch_b, v_w_out, v_final_norm_g)))
    place = jnp.stack([lax.axis_index("c"), 2 * lax.axis_index("x") + lax.axis_index("y")]).astype(jnp.int32)

    total_loss, dx, results = _step(x[0], loss_target[0], w, m, v, place)
    return (total_loss, dx[None], *[results[n][q] for q in range(4) for n in WEIGHTS])
```

```python
import functools

import jax
import jax.numpy as jnp
from jax import lax
from jax.experimental import pallas as pl
from jax.experimental.pallas import tpu as pltpu

F32, BF16 = jnp.float32, jnp.bfloat16
SDS = jax.ShapeDtypeStruct
MESH = pl.DeviceIdType.MESH

DEPTH = 2
L = 2048
D = 1024
NIN = 4096
W = 512
G, P, C = 32, 64, 16
GP = G * P
WINS = (2, 4, 8, 16)
TM = 256
NT = L // TM
TMM = 512
TK = 2048
LAST_ROWS = 512
EPS = 1e-6
VMEM_LIMIT = 56 * 2**20

LR, B1, B2, EPS_A, WD, STEP = 0.001, 0.9, 0.999, 1e-08, 0.01, 10


def _params(*sem):
    return pltpu.CompilerParams(dimension_semantics=sem, vmem_limit_bytes=VMEM_LIMIT)


_ANY = pl.BlockSpec(memory_space=pl.ANY)


def _full(shape):
    zeros = (0,) * len(shape)
    return pl.BlockSpec(shape, lambda *_: zeros)


def _layer(l, shape):
    zeros = (0,) * len(shape)
    return pl.BlockSpec((None,) + shape, lambda *_: (l,) + zeros)


def _rows(width, col=0, reverse=False, tm=TM):
    if reverse:
        return pl.BlockSpec((tm, width), lambda i: (L // tm - 1 - i, col))
    return pl.BlockSpec((tm, width), lambda i: (i, col))


def _rows_mm(width, col=0):
    return _rows(width, col, False, TMM)


def _pcall(body, name, grid, in_specs, out_specs, out_shape, args, scratch=(), aliases=None, jobs=()):
    in_specs, out_specs, out_shape, args, scratch = list(in_specs), list(out_specs), list(out_shape), list(args), list(scratch)
    aliases = dict(aliases or {})
    jobs = [j for j in jobs if j is not None]
    n_in, n_out, n_scr = len(in_specs), len(out_specs), len(scratch)
    srcs = [s for j in jobs for s in j.srcs]
    bufs = [b for j in jobs for b in j.bufs]
    news = [s for j in jobs for s in j.news]
    aliases.update({n_in + len(srcs) + k: n_out + k for k in range(len(bufs))})

    def hosted(*refs):
        cuts = [n_in, len(srcs), len(bufs), n_out, len(bufs), len(news), n_scr]
        parts, p = [], 0
        for n in cuts:
            parts.append(refs[p:p + n])
            p += n
        ins, src_r, _, outs, buf_r, new_r, scr = parts
        sem_r = refs[p:]
        views, ps, pb, pn, pm = [], 0, 0, 0, 0
        for j in jobs:
            views.append((src_r[ps:ps + len(j.srcs)], buf_r[pb:pb + len(j.bufs)], new_r[pn:pn + len(j.news)],
                          sem_r[pm:pm + len(j.scratch)]))
            ps, pb, pn, pm = ps + len(j.srcs), pb + len(j.bufs), pn + len(j.news), pm + len(j.scratch)

        def run(phase):
            for j, v in zip(jobs, views):
                getattr(j, phase)(*v)

        def at_step(step):
            return functools.reduce(jnp.logical_and, [pl.program_id(d) == step(d) for d in range(len(grid))])

        if not grid:
            run("start")
            run("finish")
            return
        pl.when(at_step(lambda d: 0))(lambda: run("start"))
        body(*ins, *outs, *scr)
        pl.when(at_step(lambda d: grid[d] - 1))(lambda: run("finish"))

    outs = pl.pallas_call(
        hosted if jobs else body, name=name, **({"grid": grid} if grid else {}),
        in_specs=in_specs + [_ANY] * (len(srcs) + len(bufs)), out_specs=out_specs + [_ANY] * (len(bufs) + len(news)),
        out_shape=out_shape + [SDS(b.shape, b.dtype) for b in bufs] + news,
        input_output_aliases=aliases, scratch_shapes=scratch + [s for j in jobs for s in j.scratch],
        compiler_params=_params(*(("arbitrary",) * len(grid))))(*args, *srcs, *bufs)
    res, pb, pn = [], n_out, n_out + len(bufs)
    for j in jobs:
        res.append((list(outs[pb:pb + len(j.bufs)]), list(outs[pn:pn + len(j.news)])))
        pb, pn = pb + len(j.bufs), pn + len(j.news)
    return list(outs[:n_out]), res


def _fused(name, grid, parts, shared=None, jobs=()):
    def body(*refs):
        pos = [0]

        def take(n):
            pos[0] += n
            return refs[pos[0] - n:pos[0]]

        ins = [take(len(p["in_specs"])) for p in parts]
        if shared:
            take(1)
            block = take(1)[0]
        outs = [take(len(p["out_specs"])) for p in parts]
        scr = [take(len(p["scratch"])) for p in parts]
        col = 0
        for p, i, o, s in zip(parts, ins, outs, scr):
            view = []
            if shared:
                view = [block.at[:, pl.ds(col, p["width"])]]
                col += p["width"]
            p["body"](*i, *view, *o, *s)

    in_specs = [s for p in parts for s in p["in_specs"]] + ([_ANY] if shared else [])
    out_specs = ([shared[1]] if shared else []) + [s for p in parts for s in p["out_specs"]]
    out_shape = ([SDS(shared[0].shape, shared[0].dtype)] if shared else []) + [s for p in parts for s in p["out_shape"]]
    args = [a for p in parts for a in p["args"]] + ([shared[0]] if shared else [])
    return _pcall(body, name, grid, in_specs, out_specs, out_shape, args, [s for p in parts for s in p["scratch"]],
                  {len(in_specs) - 1: 0} if shared else None, jobs)


def _dot(a, b):
    return jnp.dot(a, b, preferred_element_type=F32)


def _dot_nt(a, b):
    return lax.dot_general(a, b, (((1,), (1,)), ((), ())), preferred_element_type=F32)


def _dot_tn(a, b):
    return lax.dot_general(a, b, (((0,), (0,)), ((), ())), preferred_element_type=F32)


_K0 = 0.7978845608028654
_K1 = 0.044715


def _gelu(x):
    return 0.5 * x * (1.0 + jnp.tanh(_K0 * (x + _K1 * (x * x * x))))


def _gelu_grad(x):
    t = jnp.tanh(_K0 * (x + _K1 * (x * x * x)))
    return 0.5 * (1.0 + t) + 0.5 * x * (1.0 - t * t) * (_K0 * (1.0 + 3.0 * _K1 * x * x))


def _sigmoid(x):
    return jax.nn.sigmoid(x)


def _norm_inproj(l, x, g, w, b, jobs=()):
    def body(x_ref, g_ref, w_ref, b_ref, h_ref, proj_ref):
        xv = x_ref[...]
        r = lax.rsqrt(jnp.mean(xv * xv, axis=-1, keepdims=True) + EPS)
        hb = ((xv * r) * g_ref[...]).astype(BF16)
        h_ref[...] = hb
        for j in range(4):
            cs = slice(j * 1024, (j + 1) * 1024)
            proj_ref[:, cs] = _dot(hb, w_ref[j]) + b_ref[:, cs]

    return _pcall(
        body, f"norm_inproj_l{l}", (L // TMM,),
        [_rows_mm(D), _layer(l, (1, D)), _layer(l, (4, D, 1024)), _layer(l, (1, NIN))],
        [_rows_mm(D), _rows_mm(NIN)],
        [SDS((L, D), BF16), SDS((L, NIN), F32)],
        (x, g, w, b), jobs=jobs)


def _scan_tile(re_ref, im_ref, st_re, st_im, cr_re, cr_im, carry, reverse, each_chunk=None):
    def chunk(ci, carry):
        c = (TM // 8 - 1 - ci) if reverse else ci
        rows = pl.ds(pl.multiple_of(c * 8, 8), 8)
        new = []
        for lb in range(GP // 512):
            cols = slice(lb * 512, (lb + 1) * 512)
            vr = re_ref[rows, cols]
            vi = im_ref[rows, cols]
            for s, d in enumerate((1, 2, 4)):
                ar = st_re[8 * s:8 * s + 8, cols]
                ai = st_im[8 * s:8 * s + 8, cols]
                sr = pltpu.roll(vr, 8 - d if reverse else d, 0)
                si = pltpu.roll(vi, 8 - d if reverse else d, 0)
                vr, vi = vr + ar * sr - ai * si, vi + ar * si + ai * sr
            cr, ci_ = carry[2 * lb], carry[2 * lb + 1]
            pr = cr_re[:, cols]
            pi = cr_im[:, cols]
            vr, vi = vr + pr * cr - pi * ci_, vi + pr * ci_ + pi * cr
            re_ref[rows, cols] = vr
            im_ref[rows, cols] = vi
            if each_chunk is not None:
                each_chunk(c, cols, vr, vi)
            if reverse:
                new += [vr[0:1], vi[0:1]]
            else:
                new += [vr[7:8], vi[7:8]]
        return tuple(new)

    return lax.fori_loop(0, TM // 8, chunk, carry)


def _load_carry(car_ref):
    return tuple(car_ref[r:r + 1, lb * 512:(lb + 1) * 512] for lb in range(GP // 512) for r in (0, 1))


def _store_carry(car_ref, carry):
    for lb in range(GP // 512):
        car_ref[0:1, lb * 512:(lb + 1) * 512] = carry[2 * lb]
        car_ref[1:2, lb * 512:(lb + 1) * 512] = carry[2 * lb + 1]


def _s5_fwd(l, proj, bexp, cre, cimn, powers, dsk, wglu, bglu):
    def body(ua_ref, za_ref, bexp_ref, cre_ref, cimn_ref, st_re_ref, st_im_ref, cr_re_ref, cr_im_ref,
             d_ref, wg_ref, bg_ref, sre_ref, sim_ref, y1_ref, q_ref, ya_ref, car_ref):
        @pl.when(pl.program_id(0) == 0)
        def _():
            car_ref[...] = jnp.zeros_like(car_ref)

        u = ua_ref[...]
        ub = u.astype(BF16)
        for k in range(4):
            bu = _dot(ub[:, 128 * k:128 * (k + 1)], bexp_ref[k])
            sre_ref[:, 512 * k:512 * (k + 1)] = bu[:, :512]
            sim_ref[:, 512 * k:512 * (k + 1)] = bu[:, 512:]
        carry = _scan_tile(sre_ref, sim_ref, st_re_ref, st_im_ref, cr_re_ref, cr_im_ref, _load_carry(car_ref), False)
        _store_carry(car_ref, carry)
        for k in range(4):
            blk = slice(512 * k, 512 * (k + 1))
            ks = slice(128 * k, 128 * (k + 1))
            y0 = _dot(sre_ref[:, blk].astype(BF16), cre_ref[k]) + _dot(sim_ref[:, blk].astype(BF16), cimn_ref[k])
            y1_ref[:, ks] = y0 + d_ref[:, ks] * u[:, ks]
        y2 = _gelu(y1_ref[...])
        q = _dot(y2.astype(BF16), wg_ref[...]) + bg_ref[...]
        q_ref[...] = q
        za = za_ref[...]
        ya_ref[...] = ((y2 * _sigmoid(q)) * (za * _sigmoid(za))).astype(BF16)

    return dict(
        body=body,
        in_specs=[_rows(W, 0), _rows(W, 1), _layer(l, (4, 128, 1024)), _layer(l, (4, 512, 128)),
                  _layer(l, (4, 512, 128)), _layer(l, (24, GP)), _layer(l, (24, GP)), _layer(l, (8, GP)),
                  _layer(l, (8, GP)), _layer(l, (1, W)), _layer(l, (W, W)), _layer(l, (1, W))],
        out_specs=[_rows(GP), _rows(GP), _rows(W), _rows(W), _rows(W)],
        out_shape=[SDS((L, GP), F32), SDS((L, GP), F32), SDS((L, W), F32), SDS((L, W), F32), SDS((L, W), BF16)],
        args=(proj, proj, bexp, cre, cimn, *powers, dsk, wglu, bglu),
        scratch=[pltpu.VMEM((8, GP), F32)])


def _pool_fwd(l, proj, pw, scale):
    def body(ub_ref, zb_ref, pw_ref, sc_ref, pooled_ref, mixed_ref, yb_ref, buf):
        i = pl.program_id(0)

        @pl.when(i == 0)
        def _():
            buf[0:16, :] = jnp.zeros((16, W), F32)

        u = ub_ref[...]
        buf[16:16 + TM, :] = u
        t = i * TM + lax.broadcasted_iota(jnp.int32, (TM, 128), 0)
        for gi, win in enumerate(WINS):
            cs = slice(128 * gi, 128 * (gi + 1))
            acc = u[:, cs]
            for k in range(1, win):
                acc = acc + buf[16 - k:16 - k + TM, cs]
            cnt = jnp.minimum(t + 1, win).astype(F32)
            pb = (acc / cnt - u[:, cs]).astype(BF16)
            pooled_ref[:, cs] = pb
            mixed_ref[:, cs] = _dot(pb, pw_ref[gi])
        zb = zb_ref[...]
        yb_ref[...] = ((mixed_ref[...] * sc_ref[...]) * (zb * _sigmoid(zb))).astype(BF16)
        buf[0:16, :] = buf[TM:TM + 16, :]

    return dict(
        body=body,
        in_specs=[_rows(W, 2), _rows(W, 3), _layer(l, (4, 128, 128)), _layer(l, (1, W))],
        out_specs=[_rows(W), _rows(W), _rows(W)],
        out_shape=[SDS((L, W), BF16), SDS((L, W), F32), SDS((L, W), BF16)],
        args=(proj, proj, pw, scale),
        scratch=[pltpu.VMEM((TM + 16, W), F32)])


def _merge_out(l, ya, yb, proj, x, wa, wb, wo, head=None):
    def body(ya_ref, yb_ref, ga_ref, gb_ref, x_ref, wa_ref, wb_ref, wo_ref, *rest):
        pa_ref, pb_ref, mg_ref = rest[2:5] if head else rest[0:3]
        ya = ya_ref[...]
        yb = yb_ref[...]
        for j in range(4):
            cs = slice(256 * j, 256 * (j + 1))
            pa_ref[:, cs] = _dot(ya, wa_ref[j])
            pb_ref[:, cs] = _dot(yb, wb_ref[j])
        merged = _sigmoid(ga_ref[...]) * pa_ref[...] + _sigmoid(gb_ref[...]) * pb_ref[...]
        mb = merged.astype(BF16)
        mg_ref[...] = mb
        x_next = x_ref[...] + _dot(mb, wo_ref[...])
        if head:
            _loss_tile(x_next, rest[0], rest[1], *rest[5:])
        else:
            rest[3][...] = x_next

    rows, out_rows = _rows_mm(D), SDS((L, D), F32)
    return pl.pallas_call(
        body, name=f"merge_out_l{l}", grid=(L // TMM,),
        in_specs=[_rows_mm(W), _rows_mm(W), _rows_mm(D, 2), _rows_mm(D, 3), rows,
                  _layer(l, (4, W, 256)), _layer(l, (4, W, 256)), _layer(l, (D, D))] + ([_full((1, D)), rows] if head else []),
        out_specs=[rows, rows, rows] + ([_full((2, 128)), rows, _full((1, D))] if head else [rows]),
        out_shape=[out_rows, out_rows, SDS((L, D), BF16)]
        + ([SDS((2, 128), F32), out_rows, SDS((1, D), F32)] if head else [out_rows]),
        compiler_params=_params("arbitrary"),
    )(ya, yb, proj, proj, x, wa, wb, wo, *(head or ()))


def _loss_tile(xv, g_ref, t_ref, loss_ref, dx_ref, dg_ref):
    @pl.when(pl.program_id(0) == 0)
    def _():
        loss_ref[...] = jnp.zeros_like(loss_ref)
        dg_ref[...] = jnp.zeros_like(dg_ref)

    g = g_ref[...]
    r = lax.rsqrt(jnp.mean(xv * xv, axis=-1, keepdims=True) + EPS)
    xn = xv * r
    err = xn * g - t_ref[...]
    part = jnp.sum(jnp.mean(err * err, axis=-1, keepdims=True), axis=0, keepdims=True)
    loss_ref[...] += 0.5 * part
    dy = err * (1.0 / D)
    dg_ref[...] += jnp.sum(dy * xn, axis=0, keepdims=True)
    dxn = dy * g
    dx_ref[...] = r * (dxn - xn * jnp.mean(dxn * xn, axis=-1, keepdims=True))


def _merge_out_bwd(l, dxn, mg, proj, pa, pb, ya, yb, wo, wa, wb, jobs=()):
    def body(dx_ref, mg_ref, ga_ref, gb_ref, pa_ref, pb_ref, ya_ref, yb_ref, wo_ref, wa_ref, wb_ref,
             dg_ref, dya_ref, dyb_ref, dwo_ref, dwa_ref, dwb_ref, dbias_ref):
        @pl.when(pl.program_id(0) == 0)
        def _():
            for ref in (dwo_ref, dwa_ref, dwb_ref, dbias_ref):
                ref[...] = jnp.zeros_like(ref)

        dxb = dx_ref[...].astype(BF16)
        dm = _dot_nt(dxb, wo_ref[...])
        sa = _sigmoid(ga_ref[...])
        sb = _sigmoid(gb_ref[...])
        dga = dm * pa_ref[...] * (sa * (1.0 - sa))
        dgb = dm * pb_ref[...] * (sb * (1.0 - sb))
        dg_ref[:, :D] = dga.astype(BF16)
        dg_ref[:, D:] = dgb.astype(BF16)
        dbias_ref[:, :D] += jnp.sum(dga, axis=0, keepdims=True)
        dbias_ref[:, D:] += jnp.sum(dgb, axis=0, keepdims=True)
        dpa = (dm * sa).astype(BF16)
        dpb = (dm * sb).astype(BF16)
        ya = ya_ref[...]
        yb = yb_ref[...]
        dya = jnp.zeros((TM, W), F32)
        dyb = jnp.zeros((TM, W), F32)
        for j in range(4):
            cs = slice(256 * j, 256 * (j + 1))
            dya = dya + _dot_nt(dpa[:, cs], wa_ref[j])
            dyb = dyb + _dot_nt(dpb[:, cs], wb_ref[j])
            dwa_ref[j] += _dot_tn(ya, dpa[:, cs])
            dwb_ref[j] += _dot_tn(yb, dpb[:, cs])
        dya_ref[...] = dya
        dyb_ref[...] = dyb
        dwo_ref[...] += _dot_tn(mg_ref[...], dxb)

    return _pcall(
        body, f"merge_out_bwd_l{l}", (NT,),
        [_rows(D), _rows(D), _rows(D, 2), _rows(D, 3), _rows(D), _rows(D), _rows(W), _rows(W),
         _layer(l, (D, D)), _layer(l, (4, W, 256)), _layer(l, (4, W, 256))],
        [_rows(2 * D, 1), _rows(W), _rows(W), _full((D, D)), _full((4, W, 256)), _full((4, W, 256)), _full((1, 2 * D))],
        [SDS((L, NIN), BF16), SDS((L, W), F32), SDS((L, W), F32),
         SDS((D, D), F32), SDS((4, W, 256), F32), SDS((4, W, 256), F32), SDS((1, 2 * D), F32)],
        (dxn, mg, proj, proj, pa, pb, ya, yb, wo, wa, wb), jobs=jobs)


def _pool_bwd(l, dyb, proj, mixed, pooled, pw, scale):
    def body(dyb_ref, zb_ref, mixed_ref, pooled_ref, pw_ref, sc_ref, db_ref, dpw_ref, dsc_ref, dbias_ref, buf):
        i = pl.program_id(0)
        tile = NT - 1 - i

        @pl.when(i == 0)
        def _():
            dpw_ref[...] = jnp.zeros_like(dpw_ref)
            dsc_ref[...] = jnp.zeros_like(dsc_ref)
            dbias_ref[...] = jnp.zeros_like(dbias_ref)
            buf[TM:TM + 16, :] = jnp.zeros((16, W), F32)

        dyb = dyb_ref[...]
        zb = zb_ref[...]
        mixed = mixed_ref[...]
        sc = sc_ref[...]
        sg = _sigmoid(zb)
        dyb0 = dyb * (zb * sg)
        dzb = dyb * (mixed * sc) * (sg * (1.0 + zb * (1.0 - sg)))
        db_ref[:, W:] = dzb.astype(BF16)
        dbias_ref[:, W:] += jnp.sum(dzb, axis=0, keepdims=True)
        dsc_ref[...] += jnp.sum(dyb0 * mixed, axis=0, keepdims=True)
        dmix = (dyb0 * sc).astype(BF16)
        t = tile * TM + lax.broadcasted_iota(jnp.int32, (TM, 128), 0)
        for gi, win in enumerate(WINS):
            cs = slice(128 * gi, 128 * (gi + 1))
            dpw_ref[gi] += _dot_tn(pooled_ref[:, cs], dmix[:, cs])
            dpool = _dot_nt(dmix[:, cs], pw_ref[gi])
            cnt = jnp.minimum(t + 1, win).astype(F32)
            e = dpool / cnt
            buf[0:TM, cs] = e
            acc = e - dpool
            for k in range(1, win):
                acc = acc + buf[k:k + TM, cs]
            db_ref[:, cs] = acc.astype(BF16)
            dbias_ref[:, cs] += jnp.sum(acc, axis=0, keepdims=True)
        buf[TM:TM + 16, :] = buf[0:16, :]

    return dict(
        body=body, width=2 * W,
        in_specs=[_rows(W, 0, True), _rows(W, 3, True), _rows(W, 0, True), _rows(W, 0, True),
                  _layer(l, (4, 128, 128)), _layer(l, (1, W))],
        out_specs=[_full((4, 128, 128)), _full((1, W)), _full((1, 2 * W))],
        out_shape=[SDS((4, 128, 128), F32), SDS((1, W), F32), SDS((1, 2 * W), F32)],
        args=(dyb, proj, mixed, pooled, pw, scale),
        scratch=[pltpu.VMEM((TM + 16, W), F32)])


def _s5_bwd(l, dya, proj, y1, q, sre, sim, cret, cimnt, bret, bimt, st_re, st_im, cr_re, cr_im, dsk, wglu):
    def halo(i):
        return (jnp.maximum((NT - 1 - i) * (TM // 8) - 1, 0), 0)

    def body(dya_ref, ua_ref, za_ref, y1_ref, q_ref, sre_ref, sim_ref, hre_ref, him_ref,
             cret_ref, cimnt_ref, bret_ref, bimt_ref, st_re_ref, st_im_ref, cr_re_ref, cr_im_ref, d_ref, wg_ref,
             da_ref, dwg_ref, dbg_ref, dd_ref, dcre_ref, dcimn_ref, dbre_ref, dbim_ref, dare_ref, daim_ref, dbias_ref,
             lre, lim, car_ref):
        i = pl.program_id(0)
        tile = NT - 1 - i

        @pl.when(i == 0)
        def _():
            for ref in (dwg_ref, dbg_ref, dd_ref, dcre_ref, dcimn_ref, dbre_ref, dbim_ref, dare_ref, daim_ref, dbias_ref,
                        car_ref):
                ref[...] = jnp.zeros_like(ref)

        u = ua_ref[...]
        za = za_ref[...]
        y1 = y1_ref[...]
        dya = dya_ref[...]
        y2 = _gelu(y1)
        sg = _sigmoid(q_ref[...])
        sgz = _sigmoid(za)
        dy3 = dya * (za * sgz)
        dza = dya * (y2 * sg) * (sgz * (1.0 + za * (1.0 - sgz)))
        da_ref[:, W:] = dza.astype(BF16)
        dbias_ref[:, W:] += jnp.sum(dza, axis=0, keepdims=True)
        dq = dy3 * y2 * (sg * (1.0 - sg))
        dqb = dq.astype(BF16)
        dy2 = dy3 * sg + _dot_nt(dqb, wg_ref[...])
        dwg_ref[...] += _dot_tn(y2.astype(BF16), dqb)
        dbg_ref[...] += jnp.sum(dq, axis=0, keepdims=True)
        dy1 = dy2 * _gelu_grad(y1)
        dd_ref[...] += jnp.sum(dy1 * u, axis=0, keepdims=True)
        dy1b = dy1.astype(BF16)
        ub = u.astype(BF16)
        for k in range(4):
            blk = slice(512 * k, 512 * (k + 1))
            ks = slice(128 * k, 128 * (k + 1))
            lre[:, blk] = _dot(dy1b[:, ks], cret_ref[k])
            lim[:, blk] = _dot(dy1b[:, ks], cimnt_ref[k])
            dcre_ref[k] += _dot_tn(sre_ref[:, blk].astype(BF16), dy1b[:, ks])
            dcimn_ref[k] += _dot_tn(sim_ref[:, blk].astype(BF16), dy1b[:, ks])
        rowid = lax.broadcasted_iota(jnp.int32, (8, 512), 0)
        gate = (tile > 0).astype(F32)

        def d_abar(c, cols, lr, li):
            rows = pl.ds(pl.multiple_of(c * 8, 8), 8)
            prows = pl.ds(pl.multiple_of(jnp.maximum(c - 1, 0) * 8, 8), 8)
            pr = jnp.where(c == 0, hre_ref[7:8, cols] * gate, sre_ref[prows, cols][7:8])
            pi = jnp.where(c == 0, him_ref[7:8, cols] * gate, sim_ref[prows, cols][7:8])
            sr = jnp.where(rowid == 0, pr, pltpu.roll(sre_ref[rows, cols], 1, 0))
            si = jnp.where(rowid == 0, pi, pltpu.roll(sim_ref[rows, cols], 1, 0))
            dare_ref[:, cols] += sr * lr + si * li
            daim_ref[:, cols] += sr * li - si * lr

        carry = _scan_tile(lre, lim, st_re_ref, st_im_ref, cr_re_ref, cr_im_ref, _load_carry(car_ref), True, d_abar)
        _store_carry(car_ref, carry)

        for k in range(4):
            blk = slice(512 * k, 512 * (k + 1))
            ks = slice(128 * k, 128 * (k + 1))
            lrb = lre[:, blk].astype(BF16)
            lib = lim[:, blk].astype(BF16)
            du = dy1[:, ks] * d_ref[:, ks] + _dot(lrb, bret_ref[k]) + _dot(lib, bimt_ref[k])
            da_ref[:, ks] = du.astype(BF16)
            dbias_ref[:, ks] += jnp.sum(du, axis=0, keepdims=True)
            dbre_ref[k] += _dot_tn(ub[:, ks], lrb)
            dbim_ref[k] += _dot_tn(ub[:, ks], lib)

    return dict(
        body=body, width=2 * W,
        in_specs=[_rows(W, 0, True), _rows(W, 0, True), _rows(W, 1, True), _rows(W, 0, True), _rows(W, 0, True),
                  _rows(GP, 0, True), _rows(GP, 0, True),
                  pl.BlockSpec((8, GP), halo), pl.BlockSpec((8, GP), halo),
                  _layer(l, (4, 128, 512)), _layer(l, (4, 128, 512)), _layer(l, (4, 512, 128)), _layer(l, (4, 512, 128)),
                  _layer(l, (24, GP)), _layer(l, (24, GP)), _layer(l, (8, GP)), _layer(l, (8, GP)), _layer(l, (1, W)),
                  _layer(l, (W, W))],
        out_specs=[_full((W, W)), _full((1, W)), _full((1, W)),
                   _full((4, 512, 128)), _full((4, 512, 128)), _full((4, 128, 512)), _full((4, 128, 512)),
                   _full((8, GP)), _full((8, GP)), _full((1, 2 * W))],
        out_shape=[SDS((W, W), F32), SDS((1, W), F32), SDS((1, W), F32),
                   SDS((4, 512, 128), F32), SDS((4, 512, 128), F32), SDS((4, 128, 512), F32), SDS((4, 128, 512), F32),
                   SDS((8, GP), F32), SDS((8, GP), F32), SDS((1, 2 * W), F32)],
        args=(dya, proj, proj, y1, q, sre, sim, sre, sim, cret, cimnt, bret, bimt, st_re, st_im, cr_re, cr_im, dsk,
              wglu),
        scratch=[pltpu.VMEM((TM, GP), F32), pltpu.VMEM((TM, GP), F32), pltpu.VMEM((8, GP), F32)])


def _inproj_dw(l, r, h, dproj, jobs=(), rows=D // 2):
    def body(h_ref, dp_ref, dw_ref):
        part = _dot_tn(h_ref[...], dp_ref[...])

        @pl.when(pl.program_id(1) == 0)
        def _():
            dw_ref[...] = part

        @pl.when(pl.program_id(1) > 0)
        def _():
            dw_ref[...] += part

    return _pcall(
        body, f"inproj_dw{r}_l{l}", (4, L // TK),
        [pl.BlockSpec((TK, rows), lambda j, i: (i, r)), pl.BlockSpec((TK, 1024), lambda j, i: (i, j))],
        [pl.BlockSpec((None, rows, 1024), lambda j, i: (j, 0, 0))],
        [SDS((4, rows, 1024), F32)],
        (h, dproj), jobs=jobs)


def _inproj_dx(l, dproj, w, x, g, dxn, jobs=()):
    def body(dp_ref, w_ref, x_ref, g_ref, dxn_ref, dx_ref, dg_ref):
        @pl.when(pl.program_id(0) == 0)
        def _():
            dg_ref[...] = jnp.zeros_like(dg_ref)

        dh = _dot_nt(dp_ref[:, 0:1024], w_ref[0])
        for j in range(1, 4):
            dh = dh + _dot_nt(dp_ref[:, j * 1024:(j + 1) * 1024], w_ref[j])
        xv = x_ref[...]
        r = lax.rsqrt(jnp.mean(xv * xv, axis=-1, keepdims=True) + EPS)
        xn = xv * r
        dg_ref[...] += jnp.sum(dh * xn, axis=0, keepdims=True)
        dn = dh * g_ref[...]
        dx_ref[...] = dxn_ref[...] + r * (dn - xn * jnp.mean(dn * xn, axis=-1, keepdims=True))

    return _pcall(
        body, f"inproj_dx_l{l}", (L // TMM,),
        [_rows_mm(NIN), _layer(l, (4, D, 1024)), _rows_mm(D), _layer(l, (1, D)), _rows_mm(D)],
        [_rows_mm(D), _full((1, D))],
        [SDS((L, D), F32), SDS((1, D), F32)],
        (dproj, w, x, g, dxn), jobs=jobs)


def _discretize(log_dt, lam_re, lam_im, b_re, b_im):
    dt = jnp.exp(log_dt)[..., None]
    mag = jnp.exp(lam_re * dt)
    ang = lam_im * dt
    abar_re = mag * jnp.cos(ang)
    abar_im = mag * jnp.sin(ang)
    num_re = abar_re - 1.0
    num_im = abar_im
    den = lam_re * lam_re + lam_im * lam_im
    coef_re = (num_re * lam_re + num_im * lam_im) / den
    coef_im = (num_im * lam_re - num_re * lam_im) / den
    bbar_re = coef_re[..., None] * b_re - coef_im[..., None] * b_im
    bbar_im = coef_re[..., None] * b_im + coef_im[..., None] * b_re
    return abar_re, abar_im, bbar_re, bbar_im


def _powers(abar_re, abar_im):
    ar, ai = abar_re.reshape(DEPTH, 1, GP), abar_im.reshape(DEPTH, 1, GP)
    rows_re, rows_im = [ar], [ai]
    for _ in range(7):
        pr, pi = rows_re[-1], rows_im[-1]
        rows_re.append(pr * ar - pi * ai)
        rows_im.append(pr * ai + pi * ar)
    row = jnp.arange(8)[:, None]

    def steps(rows, keep):
        return jnp.concatenate([jnp.where(keep(d), rows[d - 1], 0.0) for d in (1, 2, 4)], axis=1)

    neg_im = [-r for r in rows_im]
    fwd = (steps(rows_re, lambda d: row >= d), steps(rows_im, lambda d: row >= d),
           jnp.concatenate(rows_re, axis=1), jnp.concatenate(rows_im, axis=1))
    rev = (steps(rows_re, lambda d: row < 8 - d), steps(neg_im, lambda d: row < 8 - d),
           jnp.concatenate(rows_re[::-1], axis=1), jnp.concatenate(neg_im[::-1], axis=1))
    return fwd, rev


_EYE8 = functools.partial(jnp.eye, 8, dtype=F32)


def _expand_in(b):
    return jnp.einsum("lkgpc,gh->lkgchp", b.reshape(DEPTH, 4, 8, P, C), _EYE8()).reshape(DEPTH, 4, 128, 512)


def _extract_in(e):
    return jnp.einsum("lkgchp,gh->lkgpc", e.reshape(DEPTH, 4, 8, C, 8, P), _EYE8()).reshape(DEPTH, G, P, C)


def _expand_out(c):
    return jnp.einsum("lkgcp,gh->lkgphc", c.reshape(DEPTH, 4, 8, C, P), _EYE8()).reshape(DEPTH, 4, 512, 128)


def _extract_out(e):
    return jnp.einsum("lkgphc,gh->lkgcp", e.reshape(DEPTH, 4, 8, P, 8, C), _EYE8()).reshape(DEPTH, G, C, P)


SMALL = ("norm_g", "b_in", "ssm_log_dt", "ssm_lam_re", "ssm_lam_im", "ssm_b_re", "ssm_b_im",
         "ssm_c_re", "ssm_c_im", "ssm_d", "ssm_b_glu", "pool_w", "pool_scale")
BIG = ("w_in", "ssm_w_glu", "w_branch_a", "w_branch_b", "w_out")


def _step(x, target, w, m, v, place):
    sp = {n: w[n] for n in SMALL}
    final_norm_g = w["final_norm_g"]
    wbuf = dict(zip(BIG, _cast_own(place, [w[n] for n in BIG])))
    (abar_re, abar_im, bbar_re, bbar_im), disc_vjp = jax.vjp(
        _discretize, *(sp[n] for n in ("ssm_log_dt", "ssm_lam_re", "ssm_lam_im", "ssm_b_re", "ssm_b_im")))
    powers_fwd, powers_rev = _powers(abar_re, abar_im)
    b_re_x, b_im_x = _expand_in(bbar_re), _expand_in(bbar_im)
    c_re_x, c_imn_x = _expand_out(sp["ssm_c_re"]), _expand_out(-sp["ssm_c_im"])
    b_x = jnp.concatenate([b_re_x, b_im_x], axis=3).astype(BF16)
    t = lambda a: jnp.swapaxes(a, 2, 3).astype(BF16)
    c_re_t, c_imn_t, b_re_t, b_im_t = t(c_re_x), t(c_imn_x), t(b_re_x), t(b_im_x)
    c_re_x, c_imn_x = c_re_x.astype(BF16), c_imn_x.astype(BF16)
    row = lambda n: sp[n].reshape(DEPTH, 1, -1)
    g, b_in, dsk, b_glu, scale = row("norm_g"), row("b_in"), row("ssm_d"), row("ssm_b_glu"), row("pool_scale")
    pw = sp["pool_w"].astype(BF16)

    saved = []
    for l in range(DEPTH):
        three = BIG[2:]
        if l == 0:
            wbuf["w_in"], wbuf["ssm_w_glu"] = _comm_only(
                "gather_first", _RingGatherJob([wbuf["w_in"], wbuf["ssm_w_glu"]], 0))[0]
            jobs = [_GatherJob([wbuf[n] for n in three], 0), _GatherJob([wbuf["ssm_w_glu"]], 1)]
        else:
            jobs = []
        (h, proj), res = _norm_inproj(l, x, g, wbuf["w_in"], b_in, jobs)
        if res:
            wbuf.update(zip(three, res[0][0]))
            (wbuf["ssm_w_glu"],) = res[1][0]
        wg = dict(wbuf, ssm_w_glu=wbuf["ssm_w_glu"].reshape(DEPTH, W, W), w_out=wbuf["w_out"].reshape(DEPTH, D, D))
        job = _GatherJob([wbuf["w_in"]], l + 1) if l + 1 < DEPTH else _GatherJob([wbuf[n] for n in three], l)
        (sre, sim, y1, q, ya, pooled, mixed, yb), res = _fused(
            f"branches_fwd_l{l}", (NT,),
            [_s5_fwd(l, proj, b_x, c_re_x, c_imn_x, powers_fwd, dsk, wg["ssm_w_glu"], b_glu),
             _pool_fwd(l, proj, pw, scale)], jobs=[job])
        if l + 1 < DEPTH:
            (wbuf["w_in"],) = res[0][0]
        else:
            wbuf.update(zip(three, res[0][0]))
        wg = dict(wbuf, ssm_w_glu=wbuf["ssm_w_glu"].reshape(DEPTH, W, W), w_out=wbuf["w_out"].reshape(DEPTH, D, D))
        last = l + 1 == DEPTH
        pa, pb, mg, *tail = _merge_out(l, ya, yb, proj, x, wg["w_branch_a"], wg["w_branch_b"], wg["w_out"],
                                       (final_norm_g.reshape(1, D), target) if last else None)
        saved.append(dict(x=x, h=h, proj=proj, sre=sre, sim=sim, y1=y1, q=q, ya=ya,
                          pooled=pooled, mixed=mixed, yb=yb, pa=pa, pb=pb, mg=mg))
        if last:
            loss, dx, dgf = tail
        else:
            (x,) = tail

    per_layer = {n: [None] * DEPTH for n in ("norm_g", "b_in", "ssm_d", "ssm_b_glu", "pool_w", "pool_scale",
                                             "dare", "daim", "dbre", "dbim", "dcre", "dcimn")}
    red = _Reducer(place, w, m, v)
    for l in reversed(range(DEPTH)):
        s = saved[l]
        (dproj, dya, dyb, dwo, dwa, dwb, dbias_g), res = _merge_out_bwd(
            l, dx, s["mg"], s["proj"], s["pa"], s["pb"], s["ya"], s["yb"],
            wg["w_out"], wg["w_branch_a"], wg["w_branch_b"], red.jobs())
        red.land(res)
        (dproj, dwg, dbg, dd, dcre, dcimn, dbre, dbim, dare, daim, dbias_a, dpw, dsc, dbias_b), res = _fused(
            f"branches_bwd_l{l}", (NT,),
            [_s5_bwd(l, dya, s["proj"], s["y1"], s["q"], s["sre"], s["sim"], c_re_t, c_imn_t, b_re_t, b_im_t,
                     *powers_rev, dsk, wg["ssm_w_glu"]),
             _pool_bwd(l, dyb, s["proj"], s["mixed"], s["pooled"], pw, scale)],
            shared=(dproj, _rows(4 * W, 0, True)), jobs=red.jobs())
        red.land(res)
        rest = [dwg.reshape(4, W // 4, W), dwa, dwb, dwo.reshape(4, D // 4, D)]
        if l == 0:
            red.add(l, "rest", BIG[1:], rest)
        if l == 0:
            for r, rows in ((0, D - LAST_ROWS), (D // LAST_ROWS - 1, LAST_ROWS)):
                outs, res = _inproj_dw(l, r, s["h"], dproj, red.jobs(), rows=rows)
                red.land(res)
                red.add(l, f"in{r}", BIG[:1], outs, [r * rows])
        else:
            dwin, res = _inproj_dw(l, 0, s["h"], dproj, red.jobs(), rows=D)
            red.land(res)
        (dx, dg), res = _inproj_dx(l, dproj, wg["w_in"], s["x"], g, dx, red.jobs())
        red.land(res)
        if l > 0:
            red.add(l, "all", BIG, dwin + rest)
        for n, a in (("norm_g", dg.reshape(D)), ("b_in", jnp.concatenate([dbias_a, dbias_b, dbias_g], axis=1).reshape(NIN)),
                     ("ssm_d", dd.reshape(W)), ("ssm_b_glu", dbg.reshape(W)), ("pool_w", dpw), ("pool_scale", dsc.reshape(W)),
                     ("dare", dare), ("daim", daim), ("dbre", dbre), ("dbim", dbim), ("dcre", dcre), ("dcimn", dcimn)):
            per_layer[n][l] = a
    gs = {n: jnp.stack(a) for n, a in per_layer.items()}
    d_abar = [jnp.sum(gs.pop(n), axis=1).reshape(DEPTH, G, P) for n in ("dare", "daim")]
    (gs["ssm_log_dt"], gs["ssm_lam_re"], gs["ssm_lam_im"], gs["ssm_b_re"], gs["ssm_b_im"]) = disc_vjp(
        (*d_abar, _extract_in(gs.pop("dbre")), _extract_in(gs.pop("dbim"))))
    gs["ssm_c_re"], gs["ssm_c_im"] = _extract_out(gs.pop("dcre")), -_extract_out(gs.pop("dcimn"))
    gs["final_norm_g"] = dgf

    natural = {n: w[n].shape for n in REPLICATED}
    rw, rm, rv = {}, {}, {}
    for n in REPLICATED:
        shape = DENSE.get(n, natural[n])
        gs[n], rw[n], rm[n], rv[n] = (a.reshape(shape) for a in (gs[n], w[n], m[n], v[n]))
    small = [gs[n] for n in REPLICATED] + [loss]
    jobs = red.jobs()
    res = _pcall(None, "tail_exchange", (), [], [], [], [], jobs=jobs + [_SiblingJob(small, False)])[1]
    red.land(res[:len(jobs)])
    pair_small = _small_pair_sum(place, small, res[-1][1], [BF16 if a.ndim > 2 else F32 for a in small])
    jobs = red.jobs()
    res = _pcall(None, "tail_gather", (), [], [], [], [], jobs=jobs + [_ChipGatherJob(pair_small)])[1]
    red.land(res[:len(jobs)])
    assert not red.active
    small_parts = dict(zip(REPLICATED + ("loss",), res[-1][0]))

    k = len(REPLICATED)
    outs = _adamw_small("adamw_small", [rw[n] for n in REPLICATED], [small_parts[n] for n in REPLICATED],
                        [rm[n] for n in REPLICATED], [rv[n] for n in REPLICATED], small_parts["loss"])
    results = {n: red.big[n] for n in BIG}
    results.update({n: [outs[1 + q * k + i].reshape(natural[n]) for q in range(4)] for i, n in enumerate(REPLICATED)})
    return outs[0][0, 0], dx, results


def _place():
    x, y, c = lax.axis_index("x"), lax.axis_index("y"), lax.axis_index("c")
    chips = [(1 - x, y), (x, 1 - y), (1 - x, 1 - y)]
    return x, y, c, 2 * x + y, chips, [2 * cx + cy for cx, cy in chips]


def _remote(src, dst, ssem, rsem, dev):
    return pltpu.make_async_remote_copy(src_ref=src, dst_ref=dst, send_sem=ssem, recv_sem=rsem,
                                        device_id=dev, device_id_type=MESH)


class _GatherJob:
    def __init__(self, bufs, l):
        self.srcs, self.bufs, self.news, self.l = [], list(bufs), [], l
        self.scratch = [pltpu.SemaphoreType.DMA((len(self.bufs), 3))] * 4

    def _half(self, ref, k, h):
        rows = ref.shape[2] // 2
        return ref.at[self.l, k, pl.ds(pl.multiple_of(h * rows, 8), rows), :]

    def _ici(self, bufs, sems, a, j, k):
        _, _, c, _, chips, _ = _place()
        blk = self._half(bufs[a], k, c)
        return _remote(blk, blk, sems[0].at[a, j], sems[1].at[a, j], (*chips[j], c))

    def _d2d(self, bufs, sems, a, j, k, h):
        x, y, c, _, _, _ = _place()
        blk = self._half(bufs[a], k, h)
        return _remote(blk, blk, sems[2].at[a, j], sems[3].at[a, j], (x, y, 1 - c))

    def start(self, srcs, bufs, news, sems):
        me = _place()[3]
        for a in range(len(self.bufs)):
            for j in range(3):
                self._ici(bufs, sems, a, j, me).start()

    def finish(self, srcs, bufs, news, sems):
        _, _, c, me, _, cid = _place()
        pairs = [(a, j) for a in range(len(self.bufs)) for j in range(3)]
        for a, j in pairs:
            self._ici(bufs, sems, a, j, cid[j]).wait_recv()
            self._d2d(bufs, sems, a, j, cid[j], c).start()
        for a, j in pairs:
            self._d2d(bufs, sems, a, j, cid[j], 1 - c).wait_recv()
        for a, j in pairs:
            self._ici(bufs, sems, a, j, me).wait_send()
            self._d2d(bufs, sems, a, j, cid[j], c).wait_send()


class _RingGatherJob(_GatherJob):
    def __init__(self, bufs, l):
        super().__init__(bufs, l)
        n = len(self.bufs)
        self.scratch = [pltpu.SemaphoreType.DMA((n, 2))] * 4 + [pltpu.SemaphoreType.DMA((n, 4))] * 2

    def _rows(self, ref, k, h, part):
        half = ref.shape[2] // 2
        start, rows = (h * half, half) if part is None else (h * half + part * (half // 2), half // 2)
        return ref.at[self.l, k, pl.ds(pl.multiple_of(start, 8), rows), :]

    def _to_chip(self, bufs, sems, base, a, j, k, part):
        _, _, c, _, chips, _ = _place()
        blk = self._rows(bufs[a], k, c, part)
        return _remote(blk, blk, sems[base].at[a, j], sems[base + 1].at[a, j], (*chips[j], c))

    def _to_sibling(self, bufs, sems, a, i, k, h, part):
        x, y, c, _, _, _ = _place()
        blk = self._rows(bufs[a], k, h, part)
        return _remote(blk, blk, sems[4].at[a, i], sems[5].at[a, i], (x, y, 1 - c))

    def start(self, srcs, bufs, news, sems):
        me = _place()[3]
        for a in range(len(self.bufs)):
            for j in range(2):
                self._to_chip(bufs, sems, 0, a, j, me, None).start()

    def finish(self, srcs, bufs, news, sems):
        _, _, c, me, _, cid = _place()
        arrays = range(len(self.bufs))
        for a in arrays:
            for j in (1, 0):
                self._to_chip(bufs, sems, 0, a, j, cid[j], None).wait_recv()
                self._to_chip(bufs, sems, 2, a, 1 - j, cid[j], 1 - j).start()
                self._to_sibling(bufs, sems, a, j, cid[j], c, None).start()
        for a in arrays:
            for part in range(2):
                self._to_chip(bufs, sems, 2, a, part, cid[2], part).wait_recv()
                self._to_sibling(bufs, sems, a, 2 + part, cid[2], c, part).start()
        for a in arrays:
            for j in range(2):
                self._to_sibling(bufs, sems, a, j, cid[j], 1 - c, None).wait_recv()
                self._to_sibling(bufs, sems, a, 2 + j, cid[2], 1 - c, j).wait_recv()
        for a in arrays:
            for j in range(2):
                self._to_chip(bufs, sems, 0, a, j, me, None).wait_send()
                self._to_chip(bufs, sems, 2, a, 1 - j, cid[j], 1 - j).wait_send()
                self._to_sibling(bufs, sems, a, j, cid[j], c, None).wait_send()
                self._to_sibling(bufs, sems, a, 2 + j, cid[2], c, j).wait_send()


class _SiblingJob:
    def __init__(self, srcs, rows_half):
        self.srcs, self.bufs, self.rows_half = list(srcs), [], rows_half
        self.news = [SDS((s.shape[0], s.shape[1] // 2, s.shape[2]) if rows_half else s.shape, s.dtype) for s in srcs]
        self.scratch = [pltpu.SemaphoreType.DMA((len(self.srcs),))] * 2

    def _copy(self, srcs, news, sems, a):
        x, y, c, _, _, _ = _place()
        src = srcs[a]
        if self.rows_half:
            rows = src.shape[1] // 2
            src = src.at[:, pl.ds(pl.multiple_of((1 - c) * rows, 8), rows), :]
        return _remote(src, news[a], sems[0].at[a], sems[1].at[a], (x, y, 1 - c))

    def start(self, srcs, bufs, news, sems):
        for a in range(len(self.srcs)):
            self._copy(srcs, news, sems, a).start()

    def finish(self, srcs, bufs, news, sems):
        for a in range(len(self.srcs)):
            self._copy(srcs, news, sems, a).wait()


class _ScatterJob:
    def __init__(self, parts):
        self.srcs, self.bufs = list(parts), []
        self.news = [SDS((3,) + p.shape[1:], p.dtype) for p in parts]
        self.scratch = [pltpu.SemaphoreType.DMA((len(self.srcs), 3))] * 2

    def _copy(self, srcs, news, sems, a, j):
        _, _, c, _, chips, cid = _place()
        return _remote(srcs[a].at[cid[j]], news[a].at[j], sems[0].at[a, j], sems[1].at[a, j], (*chips[j], c))

    def start(self, srcs, bufs, news, sems):
        for a in range(len(self.srcs)):
            for j in range(3):
                self._copy(srcs, news, sems, a, j).start()

    def finish(self, srcs, bufs, news, sems):
        for a in range(len(self.srcs)):
            for j in range(3):
                self._copy(srcs, news, sems, a, j).wait()


def _comm_only(name, job):
    return _pcall(None, name, (), [], [], [], [], jobs=[job])[1][0]


class _ChipGatherJob(_GatherJob):
    def __init__(self, bufs):
        super().__init__(bufs, None)

    def _half(self, ref, k, h):
        return ref.at[k, h]


def _cast_own(place, ws):
    n = len(ws)

    def body(p_ref, *refs):
        for i_ref, o_ref in zip(refs[:n], refs[n:]):
            o_ref[...] = i_ref[...].astype(BF16)

    return pl.pallas_call(
        body, name="cast_own_shards",
        grid_spec=pltpu.PrefetchScalarGridSpec(
            num_scalar_prefetch=1, grid=(DEPTH,),
            in_specs=[pl.BlockSpec((None,) + a.shape[1:], lambda l, p: (l, 0, 0)) for a in ws],
            out_specs=[pl.BlockSpec((None, None) + a.shape[1:], lambda l, p: (l, p[1], 0, 0)) for a in ws]),
        out_shape=[SDS((DEPTH, 4) + a.shape[1:], BF16) for a in ws],
        compiler_params=_params("arbitrary"),
    )(place, *ws)


def _half_tiles(a_):
    rows = a_ // 2
    ta = next(t for t in (256, 128, 64, 32, 16, 8) if rows % t == 0)
    return rows, ta, rows // ta


def _pair_sums_bf16(name, place, owns, recvs):
    n = len(owns)

    def body(p_ref, *refs):
        for own_ref, recv_ref, out_ref in zip(refs[:n], refs[n:2 * n], refs[2 * n:]):
            out_ref[...] = (own_ref[...] + recv_ref[...]).astype(BF16)

    def own_half(a):
        return pl.BlockSpec((None, a.shape[1] // 2, a.shape[2]), lambda s, p: (s, p[0], 0))

    def block(a):
        return pl.BlockSpec((None,) + a.shape[1:], lambda s, p: (s, 0, 0))

    return pl.pallas_call(
        body, name=name,
        grid_spec=pltpu.PrefetchScalarGridSpec(
            num_scalar_prefetch=1, grid=(4,),
            in_specs=[own_half(a) for a in owns] + [block(r) for r in recvs],
            out_specs=[block(r) for r in recvs]),
        out_shape=[SDS(r.shape, BF16) for r in recvs],
        compiler_params=_params("arbitrary"),
    )(place, *owns, *recvs)


def _shard_sums(name, place, owns, recvs, rbufs):
    n = len(owns)

    def body(p_ref, *refs):
        for own_ref, recv_ref, r_ref, out_ref in zip(refs[:n], refs[n:2 * n], refs[2 * n:3 * n], refs[3 * n:]):
            acc = own_ref[...] + recv_ref[...]
            for j in range(3):
                acc = acc + r_ref[j].astype(F32)
            out_ref[...] = acc

    def own_half(a):
        return pl.BlockSpec((None, a.shape[1] // 2, a.shape[2]), lambda i, p: (p[1], p[0], 0))

    def recv_block(a):
        return pl.BlockSpec((None,) + a.shape[1:], lambda i, p: (p[1], 0, 0))

    return pl.pallas_call(
        body, name=name,
        grid_spec=pltpu.PrefetchScalarGridSpec(
            num_scalar_prefetch=1, grid=(1,),
            in_specs=([own_half(a) for a in owns] + [recv_block(r) for r in recvs]
                      + [pl.BlockSpec(rb.shape, lambda i, p: (0, 0, 0)) for rb in rbufs]),
            out_specs=[pl.BlockSpec(r.shape[1:], lambda i, p: (0, 0)) for r in recvs]),
        out_shape=[SDS(r.shape[1:], F32) for r in recvs],
        compiler_params=_params("arbitrary"),
    )(place, *owns, *recvs, *rbufs)


def _small_pair_sum(place, mine, recv, dtypes):
    n = len(mine)

    def body(p_ref, *refs):
        for m_ref, r_ref, o_ref in zip(refs[:n], refs[n:2 * n], refs[2 * n:]):
            o_ref[...] = (m_ref[...] + r_ref[...]).astype(o_ref.dtype)

    def whole(a):
        zeros = (0,) * a.ndim
        return pl.BlockSpec(a.shape, lambda i, p: zeros)

    def mine_blk(a):
        zeros = (0,) * a.ndim
        return pl.BlockSpec((None,) + a.shape, lambda i, p: (p[1],) + zeros)

    return pl.pallas_call(
        body, name="small_pair_sum",
        grid_spec=pltpu.PrefetchScalarGridSpec(
            num_scalar_prefetch=1, grid=(1,),
            in_specs=[whole(a) for a in mine] + [whole(a) for a in recv],
            out_specs=[mine_blk(a) for a in mine]),
        out_shape=[SDS((4,) + a.shape, dt) for a, dt in zip(mine, dtypes)],
        compiler_params=_params("arbitrary"),
    )(place, *mine, *recv)


def _adam_math(w, g, m, v):
    m = B1 * m + (1.0 - B1) * g
    v = B2 * v + (1.0 - B2) * (g * g)
    m_hat = m / (1.0 - B1 ** STEP)
    v_hat = v / (1.0 - B2 ** STEP)
    delta = -LR * (m_hat / (jnp.sqrt(v_hat) + EPS_A) + WD * w)
    return delta, m, v


def _adamw_big(name, l, row0, w, m, v, mine, other, prev, jobs=()):
    _, _, b_ = w.shape
    _, ta, nh = _half_tiles(2 * mine.shape[0])
    prev = list(prev or [])

    def body(w_ref, m_ref, v_ref, mine_ref, other_ref, *rest):
        g_ref, d_ref, mo_ref, vo_ref = rest[len(prev):]
        g = jnp.where(pl.program_id(0) == lax.axis_index("c"), mine_ref[...], other_ref[...])
        g_ref[...] = g
        d_ref[...], mo_ref[...], vo_ref[...] = _adam_math(w_ref[...], g, m_ref[...], v_ref[...])

    slab = pl.BlockSpec((None, ta, b_), lambda h, i: (l, row0 // ta + h * nh + i, 0))
    half = pl.BlockSpec((ta, b_), lambda h, i: (i, 0))
    outs, res = _pcall(
        body, name, (2, nh), [slab, slab, slab, half, half] + [_ANY] * len(prev), [slab] * 4, [SDS(w.shape, F32)] * 4,
        (w, m, v, mine, other, *prev), aliases={5 + k: k for k in range(len(prev))}, jobs=jobs)
    return outs, res


def _adamw_small(name, ws, parts, ms, vs, loss_parts=None):
    k = len(ws)
    extra = [] if loss_parts is None else [loss_parts]

    def chip_sum(p_ref):
        p = [p_ref[k].astype(F32) for k in range(4)]
        return ((p[0] + p[1]) + p[2]) + p[3]

    def body(*refs):
        w_refs, p_refs, m_refs, v_refs = refs[:k], refs[k:2 * k], refs[2 * k:3 * k], refs[3 * k:4 * k]
        outs = refs[4 * k + len(extra):]
        if extra:
            outs[0][...] = chip_sum(refs[4 * k])
            outs = outs[1:]
        for a in range(k):
            g = chip_sum(p_refs[a])
            outs[a][...] = g
            outs[k + a][...], outs[2 * k + a][...], outs[3 * k + a][...] = _adam_math(
                w_refs[a][...], g, m_refs[a][...], v_refs[a][...])

    like = [SDS(a.shape, F32) for a in ws]
    return pl.pallas_call(
        body, name=name,
        out_shape=([SDS(loss_parts.shape[1:], F32)] if extra else []) + like * 4,
        compiler_params=pltpu.CompilerParams(vmem_limit_bytes=VMEM_LIMIT),
    )(*ws, *parts, *ms, *vs, *extra)


class _Reducer:
    def __init__(self, place, w, m, v):
        self.place, self.w, self.m, self.v = place, w, m, v
        self.active, self.riding = [], []
        self.big = {n: None for n in BIG}

    def add(self, l, tag, names, own, row0s=None):
        self.active.append(dict(l=l, key=f"{tag}_l{l}", names=names, own=list(own), row0s=row0s or [0] * len(names),
                                stage=0))

    def jobs(self):
        self.riding = list(self.active)
        return [(_SiblingJob(g["own"], True), _ScatterJob(g.get("parts", [])), _SiblingJob(g.get("shard", []), False))
                [g["stage"]] for g in self.riding]

    def land(self, res):
        for g, (_, news) in zip(self.riding, res):
            if g["stage"] == 0:
                g["recv"] = news
                g["parts"] = _pair_sums_bf16(f"pair_sums_{g['key']}", self.place, g["own"], news)
            elif g["stage"] == 1:
                g["shard"] = _shard_sums(f"shard_sums_{g['key']}", self.place, g["own"], g["recv"], news)
            else:
                for n, mine, other, row0 in zip(g["names"], g["shard"], news, g["row0s"]):
                    self.big[n] = _adamw_big(f"adamw_{n}_{row0}_{g['key']}", g["l"], row0, self.w[n], self.m[n], self.v[n],
                                             mine, other, self.big[n])[0]
                self.active.remove(g)
            g["stage"] += 1
        self.riding = []


WEIGHTS = ("norm_g", "w_in", "b_in", "ssm_log_dt", "ssm_lam_re", "ssm_lam_im", "ssm_b_re", "ssm_b_im", "ssm_c_re",
           "ssm_c_im", "ssm_d", "ssm_w_glu", "ssm_b_glu", "pool_w", "pool_scale", "w_branch_a", "w_branch_b", "w_out",
           "final_norm_g")
REPLICATED = SMALL + ("final_norm_g",)
DENSE = {"ssm_b_re": (DEPTH, G, P * C), "ssm_b_im": (DEPTH, G, P * C), "final_norm_g": (2, D // 2)}


def kernel(x, norm_g, w_in, b_in, ssm_log_dt, ssm_lam_re, ssm_lam_im, ssm_b_re, ssm_b_im, ssm_c_re, ssm_c_im, ssm_d, ssm_w_glu, ssm_b_glu, pool_w, pool_scale, w_branch_a, w_branch_b, w_out, final_norm_g, loss_target, m_norm_g, m_w_in, m_b_in, m_ssm_log_dt, m_ssm_lam_re, m_ssm_lam_im, m_ssm_b_re, m_ssm_b_im, m_ssm_c_re, m_ssm_c_im, m_ssm_d, m_ssm_w_glu, m_ssm_b_glu, m_pool_w, m_pool_scale, m_w_branch_a, m_w_branch_b, m_w_out, m_final_norm_g, v_norm_g, v_w_in, v_b_in, v_ssm_log_dt, v_ssm_lam_re, v_ssm_lam_im, v_ssm_b_re, v_ssm_b_im, v_ssm_c_re, v_ssm_c_im, v_ssm_d, v_ssm_w_glu, v_ssm_b_glu, v_pool_w, v_pool_scale, v_w_branch_a, v_w_branch_b, v_w_out, v_final_norm_g):
    w = dict(zip(WEIGHTS, (norm_g, w_in, b_in, ssm_log_dt, ssm_lam_re, ssm_lam_im, ssm_b_re, ssm_b_im, ssm_c_re,
                           ssm_c_im, ssm_d, ssm_w_glu, ssm_b_glu, pool_w, pool_scale, w_branch_a, w_branch_b, w_out,
                           final_norm_g)))
    m = dict(zip(WEIGHTS, (m_norm_g, m_w_in, m_b_in, m_ssm_log_dt, m_ssm_lam_re, m_ssm_lam_im, m_ssm_b_re, m_ssm_b_im,
                           m_ssm_c_re, m_ssm_c_im, m_ssm_d, m_ssm_w_glu, m_ssm_b_glu, m_pool_w, m_pool_scale,
                           m_w_branch_a, m_w_branch_b, m_w_out, m_final_norm_g)))
    v = dict(zip(WEIGHTS, (v_norm_g, v_w_in, v_b_in, v_ssm_log_dt, v_ssm_lam_re, v_ssm_lam_im, v_ssm_b_re, v_ssm_b_im,
                           v_ssm_c_re, v_ssm_c_im, v_ssm_d, v_ssm_w_glu, v_ssm_b_glu, v_pool_w, v_pool_scale,
                           v_w_branch_a, v_w_branch_b, v_w_out, v_final_norm_g)))
    place = jnp.stack([lax.axis_index("c"), 2 * lax.axis_index("x") + lax.axis_index("y")]).astype(jnp.int32)

    total_loss, dx, results = _step(x[0], loss_target[0], w, m, v, place)
    return (total_loss, dx[None], *[results[n][q] for q in range(4) for n in WEIGHTS])
```

```python
import functools

import jax
import jax.numpy as jnp
from jax import lax
from jax.experimental import pallas as pl
from jax.experimental.pallas import tpu as pltpu

F32, BF16 = jnp.float32, jnp.bfloat16
SDS = jax.ShapeDtypeStruct
MESH = pl.DeviceIdType.MESH

DEPTH = 2
L = 2048
D = 1024
NIN = 4096
W = 512
G, P, C = 32, 64, 16
GP = G * P
WINS = (2, 4, 8, 16)
TM = 256
NT = L // TM
TMM = 512
TK = 2048
LAST_ROWS = 512
EPS = 1e-6
VMEM_LIMIT = 56 * 2**20

LR, B1, B2, EPS_A, WD, STEP = 0.001, 0.9, 0.999, 1e-08, 0.01, 10


def _params(*sem):
    return pltpu.CompilerParams(dimension_semantics=sem, vmem_limit_bytes=VMEM_LIMIT)


_ANY = pl.BlockSpec(memory_space=pl.ANY)


def _full(shape):
    zeros = (0,) * len(shape)
    return pl.BlockSpec(shape, lambda *_: zeros)


def _layer(l, shape, buffers=None):
    zeros = (0,) * len(shape)
    mode = {} if buffers is None else {"pipeline_mode": pl.Buffered(buffers)}
    return pl.BlockSpec((None,) + shape, lambda *_: (l,) + zeros, **mode)


def _rows(width, col=0, reverse=False, tm=TM):
    if reverse:
        return pl.BlockSpec((tm, width), lambda i: (L // tm - 1 - i, col))
    return pl.BlockSpec((tm, width), lambda i: (i, col))


def _rows_mm(width, col=0):
    return _rows(width, col, False, TMM)


def _pcall(body, name, grid, in_specs, out_specs, out_shape, args, scratch=(), aliases=None, jobs=()):
    in_specs, out_specs, out_shape, args, scratch = list(in_specs), list(out_specs), list(out_shape), list(args), list(scratch)
    aliases = dict(aliases or {})
    jobs = [j for j in jobs if j is not None]
    n_in, n_out, n_scr = len(in_specs), len(out_specs), len(scratch)
    srcs = [s for j in jobs for s in j.srcs]
    bufs = [b for j in jobs for b in j.bufs]
    news = [s for j in jobs for s in j.news]
    aliases.update({n_in + len(srcs) + k: n_out + k for k in range(len(bufs))})

    def hosted(*refs):
        cuts = [n_in, len(srcs), len(bufs), n_out, len(bufs), len(news), n_scr]
        parts, p = [], 0
        for n in cuts:
            parts.append(refs[p:p + n])
            p += n
        ins, src_r, _, outs, buf_r, new_r, scr = parts
        sem_r = refs[p:]
        views, ps, pb, pn, pm = [], 0, 0, 0, 0
        for j in jobs:
            views.append((src_r[ps:ps + len(j.srcs)], buf_r[pb:pb + len(j.bufs)], new_r[pn:pn + len(j.news)],
                          sem_r[pm:pm + len(j.scratch)]))
            ps, pb, pn, pm = ps + len(j.srcs), pb + len(j.bufs), pn + len(j.news), pm + len(j.scratch)

        def run(phase):
            for j, v in zip(jobs, views):
                getattr(j, phase)(*v)

        def at_step(step):
            return functools.reduce(jnp.logical_and, [pl.program_id(d) == step(d) for d in range(len(grid))])

        if not grid:
            run("start")
            run("finish")
            return
        pl.when(at_step(lambda d: 0))(lambda: run("start"))
        body(*ins, *outs, *scr)
        pl.when(at_step(lambda d: grid[d] - 1))(lambda: run("finish"))

    outs = pl.pallas_call(
        hosted if jobs else body, name=name, **({"grid": grid} if grid else {}),
        in_specs=in_specs + [_ANY] * (len(srcs) + len(bufs)), out_specs=out_specs + [_ANY] * (len(bufs) + len(news)),
        out_shape=out_shape + [SDS(b.shape, b.dtype) for b in bufs] + news,
        input_output_aliases=aliases, scratch_shapes=scratch + [s for j in jobs for s in j.scratch],
        compiler_params=_params(*(("arbitrary",) * len(grid))))(*args, *srcs, *bufs)
    res, pb, pn = [], n_out, n_out + len(bufs)
    for j in jobs:
        res.append((list(outs[pb:pb + len(j.bufs)]), list(outs[pn:pn + len(j.news)])))
        pb, pn = pb + len(j.bufs), pn + len(j.news)
    return list(outs[:n_out]), res


def _fused(name, grid, parts, shared=None, jobs=()):
    def body(*refs):
        pos = [0]

        def take(n):
            pos[0] += n
            return refs[pos[0] - n:pos[0]]

        ins = [take(len(p["in_specs"])) for p in parts]
        if shared:
            take(1)
            block = take(1)[0]
        outs = [take(len(p["out_specs"])) for p in parts]
        scr = [take(len(p["scratch"])) for p in parts]
        col = 0
        for p, i, o, s in zip(parts, ins, outs, scr):
            view = []
            if shared:
                view = [block.at[:, pl.ds(col, p["width"])]]
                col += p["width"]
            p["body"](*i, *view, *o, *s)

    in_specs = [s for p in parts for s in p["in_specs"]] + ([_ANY] if shared else [])
    out_specs = ([shared[1]] if shared else []) + [s for p in parts for s in p["out_specs"]]
    out_shape = ([SDS(shared[0].shape, shared[0].dtype)] if shared else []) + [s for p in parts for s in p["out_shape"]]
    args = [a for p in parts for a in p["args"]] + ([shared[0]] if shared else [])
    return _pcall(body, name, grid, in_specs, out_specs, out_shape, args, [s for p in parts for s in p["scratch"]],
                  {len(in_specs) - 1: 0} if shared else None, jobs)


def _dot(a, b):
    return jnp.dot(a, b, preferred_element_type=F32)


def _dot_nt(a, b):
    return lax.dot_general(a, b, (((1,), (1,)), ((), ())), preferred_element_type=F32)


def _dot_tn(a, b):
    return lax.dot_general(a, b, (((0,), (0,)), ((), ())), preferred_element_type=F32)


_K0 = 0.7978845608028654
_K1 = 0.044715


def _gelu(x):
    return 0.5 * x * (1.0 + jnp.tanh(_K0 * (x + _K1 * (x * x * x))))


def _gelu_grad(x):
    t = jnp.tanh(_K0 * (x + _K1 * (x * x * x)))
    return 0.5 * (1.0 + t) + 0.5 * x * (1.0 - t * t) * (_K0 * (1.0 + 3.0 * _K1 * x * x))


def _sigmoid(x):
    return jax.nn.sigmoid(x)


def _norm_inproj(l, x, g, w, b, jobs=()):
    def body(x_ref, g_ref, w_ref, b_ref, h_ref, proj_ref):
        xv = x_ref[...]
        r = lax.rsqrt(jnp.mean(xv * xv, axis=-1, keepdims=True) + EPS)
        hb = ((xv * r) * g_ref[...]).astype(BF16)
        h_ref[...] = hb
        for j in range(4):
            cs = slice(j * 1024, (j + 1) * 1024)
            proj_ref[:, cs] = _dot(hb, w_ref[j]) + b_ref[:, cs]

    return _pcall(
        body, f"norm_inproj_l{l}", (L // TMM,),
        [_rows_mm(D), _layer(l, (1, D)), _layer(l, (4, D, 1024)), _layer(l, (1, NIN))],
        [_rows_mm(D), _rows_mm(NIN)],
        [SDS((L, D), BF16), SDS((L, NIN), F32)],
        (x, g, w, b), jobs=jobs)


def _scan_tile(re_ref, im_ref, st_re, st_im, cr_re, cr_im, carry, reverse, each_chunk=None):
    def chunk(ci, carry):
        c = (TM // 8 - 1 - ci) if reverse else ci
        rows = pl.ds(pl.multiple_of(c * 8, 8), 8)
        new = []
        for lb in range(GP // 512):
            cols = slice(lb * 512, (lb + 1) * 512)
            vr = re_ref[rows, cols]
            vi = im_ref[rows, cols]
            for s, d in enumerate((1, 2, 4)):
                ar = st_re[8 * s:8 * s + 8, cols]
                ai = st_im[8 * s:8 * s + 8, cols]
                sr = pltpu.roll(vr, 8 - d if reverse else d, 0)
                si = pltpu.roll(vi, 8 - d if reverse else d, 0)
                vr, vi = vr + ar * sr - ai * si, vi + ar * si + ai * sr
            cr, ci_ = carry[2 * lb], carry[2 * lb + 1]
            pr = cr_re[:, cols]
            pi = cr_im[:, cols]
            vr, vi = vr + pr * cr - pi * ci_, vi + pr * ci_ + pi * cr
            re_ref[rows, cols] = vr
            im_ref[rows, cols] = vi
            if each_chunk is not None:
                each_chunk(c, cols, vr, vi)
            if reverse:
                new += [vr[0:1], vi[0:1]]
            else:
                new += [vr[7:8], vi[7:8]]
        return tuple(new)

    return lax.fori_loop(0, TM // 8, chunk, carry)


def _load_carry(car_ref):
    return tuple(car_ref[r:r + 1, lb * 512:(lb + 1) * 512] for lb in range(GP // 512) for r in (0, 1))


def _store_carry(car_ref, carry):
    for lb in range(GP // 512):
        car_ref[0:1, lb * 512:(lb + 1) * 512] = carry[2 * lb]
        car_ref[1:2, lb * 512:(lb + 1) * 512] = carry[2 * lb + 1]


def _s5_fwd(l, proj, bexp, cre, cimn, powers, dsk, wglu, bglu):
    def body(ua_ref, za_ref, bexp_ref, cre_ref, cimn_ref, st_re_ref, st_im_ref, cr_re_ref, cr_im_ref,
             d_ref, wg_ref, bg_ref, sre_ref, sim_ref, y1_ref, q_ref, ya_ref, car_ref):
        @pl.when(pl.program_id(0) == 0)
        def _():
            car_ref[...] = jnp.zeros_like(car_ref)

        u = ua_ref[...]
        ub = u.astype(BF16)
        for k in range(4):
            bu = _dot(ub[:, 128 * k:128 * (k + 1)], bexp_ref[k])
            sre_ref[:, 512 * k:512 * (k + 1)] = bu[:, :512]
            sim_ref[:, 512 * k:512 * (k + 1)] = bu[:, 512:]
        carry = _scan_tile(sre_ref, sim_ref, st_re_ref, st_im_ref, cr_re_ref, cr_im_ref, _load_carry(car_ref), False)
        _store_carry(car_ref, carry)
        for k in range(4):
            blk = slice(512 * k, 512 * (k + 1))
            ks = slice(128 * k, 128 * (k + 1))
            y0 = _dot(sre_ref[:, blk].astype(BF16), cre_ref[k]) + _dot(sim_ref[:, blk].astype(BF16), cimn_ref[k])
            y1_ref[:, ks] = y0 + d_ref[:, ks] * u[:, ks]
        y2 = _gelu(y1_ref[...])
        q = _dot(y2.astype(BF16), wg_ref[...]) + bg_ref[...]
        q_ref[...] = q
        za = za_ref[...]
        ya_ref[...] = ((y2 * _sigmoid(q)) * (za * _sigmoid(za))).astype(BF16)

    return dict(
        body=body,
        in_specs=[_rows(W, 0), _rows(W, 1), _layer(l, (4, 128, 1024)), _layer(l, (4, 512, 128)),
                  _layer(l, (4, 512, 128)), _layer(l, (24, GP)), _layer(l, (24, GP)), _layer(l, (8, GP)),
                  _layer(l, (8, GP)), _layer(l, (1, W)), _layer(l, (W, W)), _layer(l, (1, W))],
        out_specs=[_rows(GP), _rows(GP), _rows(W), _rows(W), _rows(W)],
        out_shape=[SDS((L, GP), F32), SDS((L, GP), F32), SDS((L, W), F32), SDS((L, W), F32), SDS((L, W), BF16)],
        args=(proj, proj, bexp, cre, cimn, *powers, dsk, wglu, bglu),
        scratch=[pltpu.VMEM((8, GP), F32)])


def _pool_fwd(l, proj, pw, scale):
    def body(ub_ref, zb_ref, pw_ref, sc_ref, pooled_ref, mixed_ref, yb_ref, buf):
        i = pl.program_id(0)

        @pl.when(i == 0)
        def _():
            buf[0:16, :] = jnp.zeros((16, W), F32)

        u = ub_ref[...]
        buf[16:16 + TM, :] = u
        t = i * TM + lax.broadcasted_iota(jnp.int32, (TM, 128), 0)
        for gi, win in enumerate(WINS):
            cs = slice(128 * gi, 128 * (gi + 1))
            acc = u[:, cs]
            for k in range(1, win):
                acc = acc + buf[16 - k:16 - k + TM, cs]
            cnt = jnp.minimum(t + 1, win).astype(F32)
            pb = (acc / cnt - u[:, cs]).astype(BF16)
            pooled_ref[:, cs] = pb
            mixed_ref[:, cs] = _dot(pb, pw_ref[gi])
        zb = zb_ref[...]
        yb_ref[...] = ((mixed_ref[...] * sc_ref[...]) * (zb * _sigmoid(zb))).astype(BF16)
        buf[0:16, :] = buf[TM:TM + 16, :]

    return dict(
        body=body,
        in_specs=[_rows(W, 2), _rows(W, 3), _layer(l, (4, 128, 128)), _layer(l, (1, W))],
        out_specs=[_rows(W), _rows(W), _rows(W)],
        out_shape=[SDS((L, W), BF16), SDS((L, W), F32), SDS((L, W), BF16)],
        args=(proj, proj, pw, scale),
        scratch=[pltpu.VMEM((TM + 16, W), F32)])


def _merge_out(l, ya, yb, proj, x, wa, wb, wo, head=None):
    def body(ya_ref, yb_ref, ga_ref, gb_ref, x_ref, wa_ref, wb_ref, wo_ref, *rest):
        pa_ref, pb_ref, mg_ref = rest[2:5] if head else rest[0:3]
        ya = ya_ref[...]
        yb = yb_ref[...]
        for j in range(4):
            cs = slice(256 * j, 256 * (j + 1))
            pa_ref[:, cs] = _dot(ya, wa_ref[j])
            pb_ref[:, cs] = _dot(yb, wb_ref[j])
        merged = _sigmoid(ga_ref[...]) * pa_ref[...] + _sigmoid(gb_ref[...]) * pb_ref[...]
        mb = merged.astype(BF16)
        mg_ref[...] = mb
        x_next = x_ref[...] + _dot(mb, wo_ref[...])
        if head:
            _loss_tile(x_next, rest[0], rest[1], *rest[5:])
        else:
            rest[3][...] = x_next

    rows, out_rows = _rows_mm(D), SDS((L, D), F32)
    return pl.pallas_call(
        body, name=f"merge_out_l{l}", grid=(L // TMM,),
        in_specs=[_rows_mm(W), _rows_mm(W), _rows_mm(D, 2), _rows_mm(D, 3), rows,
                  _layer(l, (4, W, 256)), _layer(l, (4, W, 256)), _layer(l, (D, D))] + ([_full((1, D)), rows] if head else []),
        out_specs=[rows, rows, rows] + ([_full((2, 128)), rows, _full((1, D))] if head else [rows]),
        out_shape=[out_rows, out_rows, SDS((L, D), BF16)]
        + ([SDS((2, 128), F32), out_rows, SDS((1, D), F32)] if head else [out_rows]),
        compiler_params=_params("arbitrary"),
    )(ya, yb, proj, proj, x, wa, wb, wo, *(head or ()))


def _loss_tile(xv, g_ref, t_ref, loss_ref, dx_ref, dg_ref):
    @pl.when(pl.program_id(0) == 0)
    def _():
        loss_ref[...] = jnp.zeros_like(loss_ref)
        dg_ref[...] = jnp.zeros_like(dg_ref)

    g = g_ref[...]
    r = lax.rsqrt(jnp.mean(xv * xv, axis=-1, keepdims=True) + EPS)
    xn = xv * r
    err = xn * g - t_ref[...]
    part = jnp.sum(jnp.mean(err * err, axis=-1, keepdims=True), axis=0, keepdims=True)
    loss_ref[...] += 0.5 * part
    dy = err * (1.0 / D)
    dg_ref[...] += jnp.sum(dy * xn, axis=0, keepdims=True)
    dxn = dy * g
    dx_ref[...] = r * (dxn - xn * jnp.mean(dxn * xn, axis=-1, keepdims=True))


def _merge_out_bwd(l, dxn, mg, proj, pa, pb, ya, yb, wo, wa, wb, jobs=()):
    def body(dx_ref, mg_ref, ga_ref, gb_ref, pa_ref, pb_ref, ya_ref, yb_ref, wo_ref, wa_ref, wb_ref,
             dg_ref, dya_ref, dyb_ref, dwo_ref, dwa_ref, dwb_ref, dbias_ref):
        @pl.when(pl.program_id(0) == 0)
        def _():
            for ref in (dwo_ref, dwa_ref, dwb_ref, dbias_ref):
                ref[...] = jnp.zeros_like(ref)

        dxb = dx_ref[...].astype(BF16)
        dm = _dot_nt(dxb, wo_ref[...])
        sa = _sigmoid(ga_ref[...])
        sb = _sigmoid(gb_ref[...])
        dga = dm * pa_ref[...] * (sa * (1.0 - sa))
        dgb = dm * pb_ref[...] * (sb * (1.0 - sb))
        dg_ref[:, :D] = dga.astype(BF16)
        dg_ref[:, D:] = dgb.astype(BF16)
        dbias_ref[:, :D] += jnp.sum(dga, axis=0, keepdims=True)
        dbias_ref[:, D:] += jnp.sum(dgb, axis=0, keepdims=True)
        dpa = (dm * sa).astype(BF16)
        dpb = (dm * sb).astype(BF16)
        ya = ya_ref[...]
        yb = yb_ref[...]
        dya = jnp.zeros((TMM, W), F32)
        dyb = jnp.zeros((TMM, W), F32)
        for j in range(4):
            cs = slice(256 * j, 256 * (j + 1))
            dya = dya + _dot_nt(dpa[:, cs], wa_ref[j])
            dyb = dyb + _dot_nt(dpb[:, cs], wb_ref[j])
            dwa_ref[j] += _dot_tn(ya, dpa[:, cs])
            dwb_ref[j] += _dot_tn(yb, dpb[:, cs])
        dya_ref[...] = dya
        dyb_ref[...] = dyb
        dwo_ref[...] += _dot_tn(mg_ref[...], dxb)

    return _pcall(
        body, f"merge_out_bwd_l{l}", (L // TMM,),
        [_rows_mm(D), _rows_mm(D), _rows_mm(D, 2), _rows_mm(D, 3), _rows_mm(D), _rows_mm(D), _rows_mm(W), _rows_mm(W),
         _layer(l, (D, D), 1), _layer(l, (4, W, 256), 1), _layer(l, (4, W, 256), 1)],
        [_rows_mm(2 * D, 1), _rows_mm(W), _rows_mm(W), _full((D, D)), _full((4, W, 256)), _full((4, W, 256)),
         _full((1, 2 * D))],
        [SDS((L, NIN), BF16), SDS((L, W), F32), SDS((L, W), F32),
         SDS((D, D), F32), SDS((4, W, 256), F32), SDS((4, W, 256), F32), SDS((1, 2 * D), F32)],
        (dxn, mg, proj, proj, pa, pb, ya, yb, wo, wa, wb), jobs=jobs)


def _pool_bwd(l, dyb, proj, mixed, pooled, pw, scale):
    def body(dyb_ref, zb_ref, mixed_ref, pooled_ref, pw_ref, sc_ref, db_ref, dpw_ref, dsc_ref, dbias_ref, buf):
        i = pl.program_id(0)
        tile = NT - 1 - i

        @pl.when(i == 0)
        def _():
            dpw_ref[...] = jnp.zeros_like(dpw_ref)
            dsc_ref[...] = jnp.zeros_like(dsc_ref)
            dbias_ref[...] = jnp.zeros_like(dbias_ref)
            buf[TM:TM + 16, :] = jnp.zeros((16, W), F32)

        dyb = dyb_ref[...]
        zb = zb_ref[...]
        mixed = mixed_ref[...]
        sc = sc_ref[...]
        sg = _sigmoid(zb)
        dyb0 = dyb * (zb * sg)
        dzb = dyb * (mixed * sc) * (sg * (1.0 + zb * (1.0 - sg)))
        db_ref[:, W:] = dzb.astype(BF16)
        dbias_ref[:, W:] += jnp.sum(dzb, axis=0, keepdims=True)
        dsc_ref[...] += jnp.sum(dyb0 * mixed, axis=0, keepdims=True)
        dmix = (dyb0 * sc).astype(BF16)
        t = tile * TM + lax.broadcasted_iota(jnp.int32, (TM, 128), 0)
        for gi, win in enumerate(WINS):
            cs = slice(128 * gi, 128 * (gi + 1))
            dpw_ref[gi] += _dot_tn(pooled_ref[:, cs], dmix[:, cs])
            dpool = _dot_nt(dmix[:, cs], pw_ref[gi])
            cnt = jnp.minimum(t + 1, win).astype(F32)
            e = dpool / cnt
            buf[0:TM, cs] = e
            acc = e - dpool
            for k in range(1, win):
                acc = acc + buf[k:k + TM, cs]
            db_ref[:, cs] = acc.astype(BF16)
            dbias_ref[:, cs] += jnp.sum(acc, axis=0, keepdims=True)
        buf[TM:TM + 16, :] = buf[0:16, :]

    return dict(
        body=body, width=2 * W,
        in_specs=[_rows(W, 0, True), _rows(W, 3, True), _rows(W, 0, True), _rows(W, 0, True),
                  _layer(l, (4, 128, 128)), _layer(l, (1, W))],
        out_specs=[_full((4, 128, 128)), _full((1, W)), _full((1, 2 * W))],
        out_shape=[SDS((4, 128, 128), F32), SDS((1, W), F32), SDS((1, 2 * W), F32)],
        args=(dyb, proj, mixed, pooled, pw, scale),
        scratch=[pltpu.VMEM((TM + 16, W), F32)])


def _s5_bwd(l, dya, proj, y1, q, sre, sim, cret, cimnt, bret, bimt, st_re, st_im, cr_re, cr_im, dsk, wglu):
    def halo(i):
        return (jnp.maximum((NT - 1 - i) * (TM // 8) - 1, 0), 0)

    def body(dya_ref, ua_ref, za_ref, y1_ref, q_ref, sre_ref, sim_ref, hre_ref, him_ref,
             cret_ref, cimnt_ref, bret_ref, bimt_ref, st_re_ref, st_im_ref, cr_re_ref, cr_im_ref, d_ref, wg_ref,
             da_ref, dwg_ref, dbg_ref, dd_ref, dcre_ref, dcimn_ref, dbre_ref, dbim_ref, dare_ref, daim_ref, dbias_ref,
             lre, lim, car_ref):
        i = pl.program_id(0)
        tile = NT - 1 - i

        @pl.when(i == 0)
        def _():
            for ref in (dwg_ref, dbg_ref, dd_ref, dcre_ref, dcimn_ref, dbre_ref, dbim_ref, dare_ref, daim_ref, dbias_ref,
                        car_ref):
                ref[...] = jnp.zeros_like(ref)

        u = ua_ref[...]
        za = za_ref[...]
        y1 = y1_ref[...]
        dya = dya_ref[...]
        y2 = _gelu(y1)
        sg = _sigmoid(q_ref[...])
        sgz = _sigmoid(za)
        dy3 = dya * (za * sgz)
        dza = dya * (y2 * sg) * (sgz * (1.0 + za * (1.0 - sgz)))
        da_ref[:, W:] = dza.astype(BF16)
        dbias_ref[:, W:] += jnp.sum(dza, axis=0, keepdims=True)
        dq = dy3 * y2 * (sg * (1.0 - sg))
        dqb = dq.astype(BF16)
        dy2 = dy3 * sg + _dot_nt(dqb, wg_ref[...])
        dwg_ref[...] += _dot_tn(y2.astype(BF16), dqb)
        dbg_ref[...] += jnp.sum(dq, axis=0, keepdims=True)
        dy1 = dy2 * _gelu_grad(y1)
        dd_ref[...] += jnp.sum(dy1 * u, axis=0, keepdims=True)
        dy1b = dy1.astype(BF16)
        ub = u.astype(BF16)
        for k in range(4):
            blk = slice(512 * k, 512 * (k + 1))
            ks = slice(128 * k, 128 * (k + 1))
            lre[:, blk] = _dot(dy1b[:, ks], cret_ref[k])
            lim[:, blk] = _dot(dy1b[:, ks], cimnt_ref[k])
            dcre_ref[k] += _dot_tn(sre_ref[:, blk].astype(BF16), dy1b[:, ks])
            dcimn_ref[k] += _dot_tn(sim_ref[:, blk].astype(BF16), dy1b[:, ks])
        rowid = lax.broadcasted_iota(jnp.int32, (8, 512), 0)
        gate = (tile > 0).astype(F32)

        def d_abar(c, cols, lr, li):
            rows = pl.ds(pl.multiple_of(c * 8, 8), 8)
            prows = pl.ds(pl.multiple_of(jnp.maximum(c - 1, 0) * 8, 8), 8)
            pr = jnp.where(c == 0, hre_ref[7:8, cols] * gate, sre_ref[prows, cols][7:8])
            pi = jnp.where(c == 0, him_ref[7:8, cols] * gate, sim_ref[prows, cols][7:8])
            sr = jnp.where(rowid == 0, pr, pltpu.roll(sre_ref[rows, cols], 1, 0))
            si = jnp.where(rowid == 0, pi, pltpu.roll(sim_ref[rows, cols], 1, 0))
            dare_ref[:, cols] += sr * lr + si * li
            daim_ref[:, cols] += sr * li - si * lr

        carry = _scan_tile(lre, lim, st_re_ref, st_im_ref, cr_re_ref, cr_im_ref, _load_carry(car_ref), True, d_abar)
        _store_carry(car_ref, carry)

        for k in range(4):
            blk = slice(512 * k, 512 * (k + 1))
            ks = slice(128 * k, 128 * (k + 1))
            lrb = lre[:, blk].astype(BF16)
            lib = lim[:, blk].astype(BF16)
            du = dy1[:, ks] * d_ref[:, ks] + _dot(lrb, bret_ref[k]) + _dot(lib, bimt_ref[k])
            da_ref[:, ks] = du.astype(BF16)
            dbias_ref[:, ks] += jnp.sum(du, axis=0, keepdims=True)
            dbre_ref[k] += _dot_tn(ub[:, ks], lrb)
            dbim_ref[k] += _dot_tn(ub[:, ks], lib)

    return dict(
        body=body, width=2 * W,
        in_specs=[_rows(W, 0, True), _rows(W, 0, True), _rows(W, 1, True), _rows(W, 0, True), _rows(W, 0, True),
                  _rows(GP, 0, True), _rows(GP, 0, True),
                  pl.BlockSpec((8, GP), halo), pl.BlockSpec((8, GP), halo),
                  _layer(l, (4, 128, 512)), _layer(l, (4, 128, 512)), _layer(l, (4, 512, 128)), _layer(l, (4, 512, 128)),
                  _layer(l, (24, GP)), _layer(l, (24, GP)), _layer(l, (8, GP)), _layer(l, (8, GP)), _layer(l, (1, W)),
                  _layer(l, (W, W))],
        out_specs=[_full((W, W)), _full((1, W)), _full((1, W)),
                   _full((4, 512, 128)), _full((4, 512, 128)), _full((4, 128, 512)), _full((4, 128, 512)),
                   _full((8, GP)), _full((8, GP)), _full((1, 2 * W))],
        out_shape=[SDS((W, W), F32), SDS((1, W), F32), SDS((1, W), F32),
                   SDS((4, 512, 128), F32), SDS((4, 512, 128), F32), SDS((4, 128, 512), F32), SDS((4, 128, 512), F32),
                   SDS((8, GP), F32), SDS((8, GP), F32), SDS((1, 2 * W), F32)],
        args=(dya, proj, proj, y1, q, sre, sim, sre, sim, cret, cimnt, bret, bimt, st_re, st_im, cr_re, cr_im, dsk,
              wglu),
        scratch=[pltpu.VMEM((TM, GP), F32), pltpu.VMEM((TM, GP), F32), pltpu.VMEM((8, GP), F32)])


def _inproj_dw(l, r, h, dproj, jobs=(), rows=D // 2):
    def body(h_ref, dp_ref, dw_ref):
        part = _dot_tn(h_ref[...], dp_ref[...])

        @pl.when(pl.program_id(1) == 0)
        def _():
            dw_ref[...] = part

        @pl.when(pl.program_id(1) > 0)
        def _():
            dw_ref[...] += part

    return _pcall(
        body, f"inproj_dw{r}_l{l}", (4, L // TK),
        [pl.BlockSpec((TK, rows), lambda j, i: (i, r)), pl.BlockSpec((TK, 1024), lambda j, i: (i, j))],
        [pl.BlockSpec((None, rows, 1024), lambda j, i: (j, 0, 0))],
        [SDS((4, rows, 1024), F32)],
        (h, dproj), jobs=jobs)


def _inproj_dx(l, dproj, w, x, g, dxn, jobs=()):
    def body(dp_ref, w_ref, x_ref, g_ref, dxn_ref, dx_ref, dg_ref):
        @pl.when(pl.program_id(0) == 0)
        def _():
            dg_ref[...] = jnp.zeros_like(dg_ref)

        dh = _dot_nt(dp_ref[:, 0:1024], w_ref[0])
        for j in range(1, 4):
            dh = dh + _dot_nt(dp_ref[:, j * 1024:(j + 1) * 1024], w_ref[j])
        xv = x_ref[...]
        r = lax.rsqrt(jnp.mean(xv * xv, axis=-1, keepdims=True) + EPS)
        xn = xv * r
        dg_ref[...] += jnp.sum(dh * xn, axis=0, keepdims=True)
        dn = dh * g_ref[...]
        dx_ref[...] = dxn_ref[...] + r * (dn - xn * jnp.mean(dn * xn, axis=-1, keepdims=True))

    return _pcall(
        body, f"inproj_dx_l{l}", (L // TMM,),
        [_rows_mm(NIN), _layer(l, (4, D, 1024)), _rows_mm(D), _layer(l, (1, D)), _rows_mm(D)],
        [_rows_mm(D), _full((1, D))],
        [SDS((L, D), F32), SDS((1, D), F32)],
        (dproj, w, x, g, dxn), jobs=jobs)


def _discretize(log_dt, lam_re, lam_im, b_re, b_im):
    dt = jnp.exp(log_dt)[..., None]
    mag = jnp.exp(lam_re * dt)
    ang = lam_im * dt
    abar_re = mag * jnp.cos(ang)
    abar_im = mag * jnp.sin(ang)
    num_re = abar_re - 1.0
    num_im = abar_im
    den = lam_re * lam_re + lam_im * lam_im
    coef_re = (num_re * lam_re + num_im * lam_im) / den
    coef_im = (num_im * lam_re - num_re * lam_im) / den
    bbar_re = coef_re[..., None] * b_re - coef_im[..., None] * b_im
    bbar_im = coef_re[..., None] * b_im + coef_im[..., None] * b_re
    return abar_re, abar_im, bbar_re, bbar_im


def _powers(abar_re, abar_im):
    ar, ai = abar_re.reshape(DEPTH, 1, GP), abar_im.reshape(DEPTH, 1, GP)
    rows_re, rows_im = [ar], [ai]
    for _ in range(7):
        pr, pi = rows_re[-1], rows_im[-1]
        rows_re.append(pr * ar - pi * ai)
        rows_im.append(pr * ai + pi * ar)
    row = jnp.arange(8)[:, None]

    def steps(rows, keep):
        return jnp.concatenate([jnp.where(keep(d), rows[d - 1], 0.0) for d in (1, 2, 4)], axis=1)

    neg_im = [-r for r in rows_im]
    fwd = (steps(rows_re, lambda d: row >= d), steps(rows_im, lambda d: row >= d),
           jnp.concatenate(rows_re, axis=1), jnp.concatenate(rows_im, axis=1))
    rev = (steps(rows_re, lambda d: row < 8 - d), steps(neg_im, lambda d: row < 8 - d),
           jnp.concatenate(rows_re[::-1], axis=1), jnp.concatenate(neg_im[::-1], axis=1))
    return fwd, rev


_EYE8 = functools.partial(jnp.eye, 8, dtype=F32)


def _expand_in(b):
    return jnp.einsum("lkgpc,gh->lkgchp", b.reshape(DEPTH, 4, 8, P, C), _EYE8()).reshape(DEPTH, 4, 128, 512)


def _extract_in(e):
    return jnp.einsum("lkgchp,gh->lkgpc", e.reshape(DEPTH, 4, 8, C, 8, P), _EYE8()).reshape(DEPTH, G, P, C)


def _expand_out(c):
    return jnp.einsum("lkgcp,gh->lkgphc", c.reshape(DEPTH, 4, 8, C, P), _EYE8()).reshape(DEPTH, 4, 512, 128)


def _extract_out(e):
    return jnp.einsum("lkgphc,gh->lkgcp", e.reshape(DEPTH, 4, 8, P, 8, C), _EYE8()).reshape(DEPTH, G, C, P)


SMALL = ("norm_g", "b_in", "ssm_log_dt", "ssm_lam_re", "ssm_lam_im", "ssm_b_re", "ssm_b_im",
         "ssm_c_re", "ssm_c_im", "ssm_d", "ssm_b_glu", "pool_w", "pool_scale")
BIG = ("w_in", "ssm_w_glu", "w_branch_a", "w_branch_b", "w_out")


def _step(x, target, w, m, v, place):
    sp = {n: w[n] for n in SMALL}
    final_norm_g = w["final_norm_g"]
    wbuf = dict(zip(BIG, _cast_own(place, [w[n] for n in BIG])))
    (abar_re, abar_im, bbar_re, bbar_im), disc_vjp = jax.vjp(
        _discretize, *(sp[n] for n in ("ssm_log_dt", "ssm_lam_re", "ssm_lam_im", "ssm_b_re", "ssm_b_im")))
    powers_fwd, powers_rev = _powers(abar_re, abar_im)
    b_re_x, b_im_x = _expand_in(bbar_re), _expand_in(bbar_im)
    c_re_x, c_imn_x = _expand_out(sp["ssm_c_re"]), _expand_out(-sp["ssm_c_im"])
    b_x = jnp.concatenate([b_re_x, b_im_x], axis=3).astype(BF16)
    t = lambda a: jnp.swapaxes(a, 2, 3).astype(BF16)
    c_re_t, c_imn_t, b_re_t, b_im_t = t(c_re_x), t(c_imn_x), t(b_re_x), t(b_im_x)
    c_re_x, c_imn_x = c_re_x.astype(BF16), c_imn_x.astype(BF16)
    row = lambda n: sp[n].reshape(DEPTH, 1, -1)
    g, b_in, dsk, b_glu, scale = row("norm_g"), row("b_in"), row("ssm_d"), row("ssm_b_glu"), row("pool_scale")
    pw = sp["pool_w"].astype(BF16)

    saved = []
    for l in range(DEPTH):
        three = BIG[2:]
        if l == 0:
            wbuf["w_in"], wbuf["ssm_w_glu"] = _comm_only(
                "gather_first", _RingGatherJob([wbuf["w_in"], wbuf["ssm_w_glu"]], 0))[0]
            jobs = [_GatherJob([wbuf[n] for n in three], 0), _GatherJob([wbuf["ssm_w_glu"]], 1)]
        else:
            jobs = []
        (h, proj), res = _norm_inproj(l, x, g, wbuf["w_in"], b_in, jobs)
        if res:
            wbuf.update(zip(three, res[0][0]))
            (wbuf["ssm_w_glu"],) = res[1][0]
        wg = dict(wbuf, ssm_w_glu=wbuf["ssm_w_glu"].reshape(DEPTH, W, W), w_out=wbuf["w_out"].reshape(DEPTH, D, D))
        job = _GatherJob([wbuf["w_in"]], l + 1) if l + 1 < DEPTH else _GatherJob([wbuf[n] for n in three], l)
        (sre, sim, y1, q, ya, pooled, mixed, yb), res = _fused(
            f"branches_fwd_l{l}", (NT,),
            [_s5_fwd(l, proj, b_x, c_re_x, c_imn_x, powers_fwd, dsk, wg["ssm_w_glu"], b_glu),
             _pool_fwd(l, proj, pw, scale)], jobs=[job])
        if l + 1 < DEPTH:
            (wbuf["w_in"],) = res[0][0]
        else:
            wbuf.update(zip(three, res[0][0]))
        wg = dict(wbuf, ssm_w_glu=wbuf["ssm_w_glu"].reshape(DEPTH, W, W), w_out=wbuf["w_out"].reshape(DEPTH, D, D))
        last = l + 1 == DEPTH
        pa, pb, mg, *tail = _merge_out(l, ya, yb, proj, x, wg["w_branch_a"], wg["w_branch_b"], wg["w_out"],
                                       (final_norm_g.reshape(1, D), target) if last else None)
        saved.append(dict(x=x, h=h, proj=proj, sre=sre, sim=sim, y1=y1, q=q, ya=ya,
                          pooled=pooled, mixed=mixed, yb=yb, pa=pa, pb=pb, mg=mg))
        if last:
            loss, dx, dgf = tail
        else:
            (x,) = tail

    per_layer = {n: [None] * DEPTH for n in ("norm_g", "b_in", "ssm_d", "ssm_b_glu", "pool_w", "pool_scale",
                                             "dare", "daim", "dbre", "dbim", "dcre", "dcimn")}
    red = _Reducer(place, w, m, v)
    for l in reversed(range(DEPTH)):
        s = saved[l]
        (dproj, dya, dyb, dwo, dwa, dwb, dbias_g), res = _merge_out_bwd(
            l, dx, s["mg"], s["proj"], s["pa"], s["pb"], s["ya"], s["yb"],
            wg["w_out"], wg["w_branch_a"], wg["w_branch_b"], red.jobs())
        red.land(res)
        (dproj, dwg, dbg, dd, dcre, dcimn, dbre, dbim, dare, daim, dbias_a, dpw, dsc, dbias_b), res = _fused(
            f"branches_bwd_l{l}", (NT,),
            [_s5_bwd(l, dya, s["proj"], s["y1"], s["q"], s["sre"], s["sim"], c_re_t, c_imn_t, b_re_t, b_im_t,
                     *powers_rev, dsk, wg["ssm_w_glu"]),
             _pool_bwd(l, dyb, s["proj"], s["mixed"], s["pooled"], pw, scale)],
            shared=(dproj, _rows(4 * W, 0, True)), jobs=red.jobs())
        red.land(res)
        rest = [dwg.reshape(4, W // 4, W), dwa, dwb, dwo.reshape(4, D // 4, D)]
        if l == 0:
            red.add(l, "rest", BIG[1:], rest)
        if l == 0:
            for r, rows in ((0, D - LAST_ROWS), (D // LAST_ROWS - 1, LAST_ROWS)):
                outs, res = _inproj_dw(l, r, s["h"], dproj, red.jobs(), rows=rows)
                red.land(res)
                red.add(l, f"in{r}", BIG[:1], outs, [r * rows])
        else:
            dwin, res = _inproj_dw(l, 0, s["h"], dproj, red.jobs(), rows=D)
            red.land(res)
        (dx, dg), res = _inproj_dx(l, dproj, wg["w_in"], s["x"], g, dx, red.jobs())
        red.land(res)
        if l > 0:
            red.add(l, "all", BIG, dwin + rest)
        for n, a in (("norm_g", dg.reshape(D)), ("b_in", jnp.concatenate([dbias_a, dbias_b, dbias_g], axis=1).reshape(NIN)),
                     ("ssm_d", dd.reshape(W)), ("ssm_b_glu", dbg.reshape(W)), ("pool_w", dpw), ("pool_scale", dsc.reshape(W)),
                     ("dare", dare), ("daim", daim), ("dbre", dbre), ("dbim", dbim), ("dcre", dcre), ("dcimn", dcimn)):
            per_layer[n][l] = a
    gs = {n: jnp.stack(a) for n, a in per_layer.items()}
    d_abar = [jnp.sum(gs.pop(n), axis=1).reshape(DEPTH, G, P) for n in ("dare", "daim")]
    (gs["ssm_log_dt"], gs["ssm_lam_re"], gs["ssm_lam_im"], gs["ssm_b_re"], gs["ssm_b_im"]) = disc_vjp(
        (*d_abar, _extract_in(gs.pop("dbre")), _extract_in(gs.pop("dbim"))))
    gs["ssm_c_re"], gs["ssm_c_im"] = _extract_out(gs.pop("dcre")), -_extract_out(gs.pop("dcimn"))
    gs["final_norm_g"] = dgf

    natural = {n: w[n].shape for n in REPLICATED}
    rw, rm, rv = {}, {}, {}
    for n in REPLICATED:
        shape = DENSE.get(n, natural[n])
        gs[n], rw[n], rm[n], rv[n] = (a.reshape(shape) for a in (gs[n], w[n], m[n], v[n]))
    small = [gs[n] for n in REPLICATED] + [loss]
    jobs = red.jobs()
    res = _pcall(None, "tail_exchange", (), [], [], [], [], jobs=jobs + [_SiblingJob(small, False)])[1]
    red.land(res[:len(jobs)])
    pair_small = _small_pair_sum(place, small, res[-1][1], [BF16 if a.ndim > 2 else F32 for a in small])
    jobs = red.jobs()
    res = _pcall(None, "tail_gather", (), [], [], [], [], jobs=jobs + [_ChipGatherJob(pair_small)])[1]
    red.land(res[:len(jobs)])
    assert not red.active
    small_parts = dict(zip(REPLICATED + ("loss",), res[-1][0]))

    k = len(REPLICATED)
    outs = _adamw_small("adamw_small", [rw[n] for n in REPLICATED], [small_parts[n] for n in REPLICATED],
                        [rm[n] for n in REPLICATED], [rv[n] for n in REPLICATED], small_parts["loss"])
    results = {n: red.big[n] for n in BIG}
    results.update({n: [outs[1 + q * k + i].reshape(natural[n]) for q in range(4)] for i, n in enumerate(REPLICATED)})
    return outs[0][0, 0], dx, results


def _place():
    x, y, c = lax.axis_index("x"), lax.axis_index("y"), lax.axis_index("c")
    chips = [(1 - x, y), (x, 1 - y), (1 - x, 1 - y)]
    return x, y, c, 2 * x + y, chips, [2 * cx + cy for cx, cy in chips]


def _remote(src, dst, ssem, rsem, dev):
    return pltpu.make_async_remote_copy(src_ref=src, dst_ref=dst, send_sem=ssem, recv_sem=rsem,
                                        device_id=dev, device_id_type=MESH)


class _GatherJob:
    def __init__(self, bufs, l):
        self.srcs, self.bufs, self.news, self.l = [], list(bufs), [], l
        self.scratch = [pltpu.SemaphoreType.DMA((len(self.bufs), 3))] * 4

    def _half(self, ref, k, h):
        rows = ref.shape[2] // 2
        return ref.at[self.l, k, pl.ds(pl.multiple_of(h * rows, 8), rows), :]

    def _ici(self, bufs, sems, a, j, k):
        _, _, c, _, chips, _ = _place()
        blk = self._half(bufs[a], k, c)
        return _remote(blk, blk, sems[0].at[a, j], sems[1].at[a, j], (*chips[j], c))

    def _d2d(self, bufs, sems, a, j, k, h):
        x, y, c, _, _, _ = _place()
        blk = self._half(bufs[a], k, h)
        return _remote(blk, blk, sems[2].at[a, j], sems[3].at[a, j], (x, y, 1 - c))

    def start(self, srcs, bufs, news, sems):
        me = _place()[3]
        for a in range(len(self.bufs)):
            for j in range(3):
                self._ici(bufs, sems, a, j, me).start()

    def finish(self, srcs, bufs, news, sems):
        _, _, c, me, _, cid = _place()
        pairs = [(a, j) for a in range(len(self.bufs)) for j in range(3)]
        for a, j in pairs:
            self._ici(bufs, sems, a, j, cid[j]).wait_recv()
            self._d2d(bufs, sems, a, j, cid[j], c).start()
        for a, j in pairs:
            self._d2d(bufs, sems, a, j, cid[j], 1 - c).wait_recv()
        for a, j in pairs:
            self._ici(bufs, sems, a, j, me).wait_send()
            self._d2d(bufs, sems, a, j, cid[j], c).wait_send()


class _RingGatherJob(_GatherJob):
    def __init__(self, bufs, l):
        super().__init__(bufs, l)
        n = len(self.bufs)
        self.scratch = [pltpu.SemaphoreType.DMA((n, 2))] * 4 + [pltpu.SemaphoreType.DMA((n, 4))] * 2

    def _rows(self, ref, k, h, part):
        half = ref.shape[2] // 2
        start, rows = (h * half, half) if part is None else (h * half + part * (half // 2), half // 2)
        return ref.at[self.l, k, pl.ds(pl.multiple_of(start, 8), rows), :]

    def _to_chip(self, bufs, sems, base, a, j, k, part):
        _, _, c, _, chips, _ = _place()
        blk = self._rows(bufs[a], k, c, part)
        return _remote(blk, blk, sems[base].at[a, j], sems[base + 1].at[a, j], (*chips[j], c))

    def _to_sibling(self, bufs, sems, a, i, k, h, part):
        x, y, c, _, _, _ = _place()
        blk = self._rows(bufs[a], k, h, part)
        return _remote(blk, blk, sems[4].at[a, i], sems[5].at[a, i], (x, y, 1 - c))

    def start(self, srcs, bufs, news, sems):
        me = _place()[3]
        for a in range(len(self.bufs)):
            for j in range(2):
                self._to_chip(bufs, sems, 0, a, j, me, None).start()

    def finish(self, srcs, bufs, news, sems):
        _, _, c, me, _, cid = _place()
        arrays = range(len(self.bufs))
        for a in arrays:
            for j in (1, 0):
                self._to_chip(bufs, sems, 0, a, j, cid[j], None).wait_recv()
                self._to_chip(bufs, sems, 2, a, 1 - j, cid[j], 1 - j).start()
                self._to_sibling(bufs, sems, a, j, cid[j], c, None).start()
        for a in arrays:
            for part in range(2):
                self._to_chip(bufs, sems, 2, a, part, cid[2], part).wait_recv()
                self._to_sibling(bufs, sems, a, 2 + part, cid[2], c, part).start()
        for a in arrays:
            for j in range(2):
                self._to_sibling(bufs, sems, a, j, cid[j], 1 - c, None).wait_recv()
                self._to_sibling(bufs, sems, a, 2 + j, cid[2], 1 - c, j).wait_recv()
        for a in arrays:
            for j in range(2):
                self._to_chip(bufs, sems, 0, a, j, me, None).wait_send()
                self._to_chip(bufs, sems, 2, a, 1 - j, cid[j], 1 - j).wait_send()
                self._to_sibling(bufs, sems, a, j, cid[j], c, None).wait_send()
                self._to_sibling(bufs, sems, a, 2 + j, cid[2], c, j).wait_send()


class _SiblingJob:
    def __init__(self, srcs, rows_half):
        self.srcs, self.bufs, self.rows_half = list(srcs), [], rows_half
        self.news = [SDS((s.shape[0], s.shape[1] // 2, s.shape[2]) if rows_half else s.shape, s.dtype) for s in srcs]
        self.scratch = [pltpu.SemaphoreType.DMA((len(self.srcs),))] * 2

    def _copy(self, srcs, news, sems, a):
        x, y, c, _, _, _ = _place()
        src = srcs[a]
        if self.rows_half:
            rows = src.shape[1] // 2
            src = src.at[:, pl.ds(pl.multiple_of((1 - c) * rows, 8), rows), :]
        return _remote(src, news[a], sems[0].at[a], sems[1].at[a], (x, y, 1 - c))

    def start(self, srcs, bufs, news, sems):
        for a in range(len(self.srcs)):
            self._copy(srcs, news, sems, a).start()

    def finish(self, srcs, bufs, news, sems):
        for a in range(len(self.srcs)):
            self._copy(srcs, news, sems, a).wait()


class _ScatterJob:
    def __init__(self, parts):
        self.srcs, self.bufs = list(parts), []
        self.news = [SDS((3,) + p.shape[1:], p.dtype) for p in parts]
        self.scratch = [pltpu.SemaphoreType.DMA((len(self.srcs), 3))] * 2

    def _copy(self, srcs, news, sems, a, j):
        _, _, c, _, chips, cid = _place()
        return _remote(srcs[a].at[cid[j]], news[a].at[j], sems[0].at[a, j], sems[1].at[a, j], (*chips[j], c))

    def start(self, srcs, bufs, news, sems):
        for a in range(len(self.srcs)):
            for j in range(3):
                self._copy(srcs, news, sems, a, j).start()

    def finish(self, srcs, bufs, news, sems):
        for a in range(len(self.srcs)):
            for j in range(3):
                self._copy(srcs, news, sems, a, j).wait()


def _comm_only(name, job):
    return _pcall(None, name, (), [], [], [], [], jobs=[job])[1][0]


class _ChipGatherJob(_GatherJob):
    def __init__(self, bufs):
        super().__init__(bufs, None)

    def _half(self, ref, k, h):
        return ref.at[k, h]


def _cast_own(place, ws):
    n = len(ws)

    def body(p_ref, *refs):
        for i_ref, o_ref in zip(refs[:n], refs[n:]):
            o_ref[...] = i_ref[...].astype(BF16)

    return pl.pallas_call(
        body, name="cast_own_shards",
        grid_spec=pltpu.PrefetchScalarGridSpec(
            num_scalar_prefetch=1, grid=(DEPTH,),
            in_specs=[pl.BlockSpec((None,) + a.shape[1:], lambda l, p: (l, 0, 0)) for a in ws],
            out_specs=[pl.BlockSpec((None, None) + a.shape[1:], lambda l, p: (l, p[1], 0, 0)) for a in ws]),
        out_shape=[SDS((DEPTH, 4) + a.shape[1:], BF16) for a in ws],
        compiler_params=_params("arbitrary"),
    )(place, *ws)


def _half_tiles(a_):
    rows = a_ // 2
    ta = next(t for t in (256, 128, 64, 32, 16, 8) if rows % t == 0)
    return rows, ta, rows // ta


def _pair_sums_bf16(name, place, owns, recvs):
    n = len(owns)

    def body(p_ref, *refs):
        for own_ref, recv_ref, out_ref in zip(refs[:n], refs[n:2 * n], refs[2 * n:]):
            out_ref[...] = (own_ref[...] + recv_ref[...]).astype(BF16)

    def own_half(a):
        return pl.BlockSpec((None, a.shape[1] // 2, a.shape[2]), lambda s, p: (s, p[0], 0))

    def block(a):
        return pl.BlockSpec((None,) + a.shape[1:], lambda s, p: (s, 0, 0))

    return pl.pallas_call(
        body, name=name,
        grid_spec=pltpu.PrefetchScalarGridSpec(
            num_scalar_prefetch=1, grid=(4,),
            in_specs=[own_half(a) for a in owns] + [block(r) for r in recvs],
            out_specs=[block(r) for r in recvs]),
        out_shape=[SDS(r.shape, BF16) for r in recvs],
        compiler_params=_params("arbitrary"),
    )(place, *owns, *recvs)


def _shard_sums(name, place, owns, recvs, rbufs):
    n = len(owns)

    def body(p_ref, *refs):
        for own_ref, recv_ref, r_ref, out_ref in zip(refs[:n], refs[n:2 * n], refs[2 * n:3 * n], refs[3 * n:]):
            acc = own_ref[...] + recv_ref[...]
            for j in range(3):
                acc = acc + r_ref[j].astype(F32)
            out_ref[...] = acc

    def own_half(a):
        return pl.BlockSpec((None, a.shape[1] // 2, a.shape[2]), lambda i, p: (p[1], p[0], 0))

    def recv_block(a):
        return pl.BlockSpec((None,) + a.shape[1:], lambda i, p: (p[1], 0, 0))

    return pl.pallas_call(
        body, name=name,
        grid_spec=pltpu.PrefetchScalarGridSpec(
            num_scalar_prefetch=1, grid=(1,),
            in_specs=([own_half(a) for a in owns] + [recv_block(r) for r in recvs]
                      + [pl.BlockSpec(rb.shape, lambda i, p: (0, 0, 0)) for rb in rbufs]),
            out_specs=[pl.BlockSpec(r.shape[1:], lambda i, p: (0, 0)) for r in recvs]),
        out_shape=[SDS(r.shape[1:], F32) for r in recvs],
        compiler_params=_params("arbitrary"),
    )(place, *owns, *recvs, *rbufs)


def _small_pair_sum(place, mine, recv, dtypes):
    n = len(mine)

    def body(p_ref, *refs):
        for m_ref, r_ref, o_ref in zip(refs[:n], refs[n:2 * n], refs[2 * n:]):
            o_ref[...] = (m_ref[...] + r_ref[...]).astype(o_ref.dtype)

    def whole(a):
        zeros = (0,) * a.ndim
        return pl.BlockSpec(a.shape, lambda i, p: zeros)

    def mine_blk(a):
        zeros = (0,) * a.ndim
        return pl.BlockSpec((None,) + a.shape, lambda i, p: (p[1],) + zeros)

    return pl.pallas_call(
        body, name="small_pair_sum",
        grid_spec=pltpu.PrefetchScalarGridSpec(
            num_scalar_prefetch=1, grid=(1,),
            in_specs=[whole(a) for a in mine] + [whole(a) for a in recv],
            out_specs=[mine_blk(a) for a in mine]),
        out_shape=[SDS((4,) + a.shape, dt) for a, dt in zip(mine, dtypes)],
        compiler_params=_params("arbitrary"),
    )(place, *mine, *recv)


def _adam_math(w, g, m, v):
    m = B1 * m + (1.0 - B1) * g
    v = B2 * v + (1.0 - B2) * (g * g)
    m_hat = m / (1.0 - B1 ** STEP)
    v_hat = v / (1.0 - B2 ** STEP)
    delta = -LR * (m_hat / (jnp.sqrt(v_hat) + EPS_A) + WD * w)
    return delta, m, v


def _adamw_big(name, l, row0, w, m, v, mine, other, prev, jobs=()):
    _, _, b_ = w.shape
    _, ta, nh = _half_tiles(2 * mine.shape[0])
    prev = list(prev or [])

    def body(w_ref, m_ref, v_ref, mine_ref, other_ref, *rest):
        g_ref, d_ref, mo_ref, vo_ref = rest[len(prev):]
        g = jnp.where(pl.program_id(0) == lax.axis_index("c"), mine_ref[...], other_ref[...])
        g_ref[...] = g
        d_ref[...], mo_ref[...], vo_ref[...] = _adam_math(w_ref[...], g, m_ref[...], v_ref[...])

    slab = pl.BlockSpec((None, ta, b_), lambda h, i: (l, row0 // ta + h * nh + i, 0))
    half = pl.BlockSpec((ta, b_), lambda h, i: (i, 0))
    outs, res = _pcall(
        body, name, (2, nh), [slab, slab, slab, half, half] + [_ANY] * len(prev), [slab] * 4, [SDS(w.shape, F32)] * 4,
        (w, m, v, mine, other, *prev), aliases={5 + k: k for k in range(len(prev))}, jobs=jobs)
    return outs, res


def _adamw_small(name, ws, parts, ms, vs, loss_parts=None):
    k = len(ws)
    extra = [] if loss_parts is None else [loss_parts]

    def chip_sum(p_ref):
        p = [p_ref[k].astype(F32) for k in range(4)]
        return ((p[0] + p[1]) + p[2]) + p[3]

    def body(*refs):
        w_refs, p_refs, m_refs, v_refs = refs[:k], refs[k:2 * k], refs[2 * k:3 * k], refs[3 * k:4 * k]
        outs = refs[4 * k + len(extra):]
        if extra:
            outs[0][...] = chip_sum(refs[4 * k])
            outs = outs[1:]
        for a in range(k):
            g = chip_sum(p_refs[a])
            outs[a][...] = g
            outs[k + a][...], outs[2 * k + a][...], outs[3 * k + a][...] = _adam_math(
                w_refs[a][...], g, m_refs[a][...], v_refs[a][...])

    like = [SDS(a.shape, F32) for a in ws]
    return pl.pallas_call(
        body, name=name,
        out_shape=([SDS(loss_parts.shape[1:], F32)] if extra else []) + like * 4,
        compiler_params=pltpu.CompilerParams(vmem_limit_bytes=VMEM_LIMIT),
    )(*ws, *parts, *ms, *vs, *extra)


class _Reducer:
    def __init__(self, place, w, m, v):
        self.place, self.w, self.m, self.v = place, w, m, v
        self.active, self.riding = [], []
        self.big = {n: None for n in BIG}

    def add(self, l, tag, names, own, row0s=None):
        self.active.append(dict(l=l, key=f"{tag}_l{l}", names=names, own=list(own), row0s=row0s or [0] * len(names),
                                stage=0))

    def jobs(self):
        self.riding = list(self.active)
        return [(_SiblingJob(g["own"], True), _ScatterJob(g.get("parts", [])), _SiblingJob(g.get("shard", []), False))
                [g["stage"]] for g in self.riding]

    def land(self, res):
        for g, (_, news) in zip(self.riding, res):
            if g["stage"] == 0:
                g["recv"] = news
                g["parts"] = _pair_sums_bf16(f"pair_sums_{g['key']}", self.place, g["own"], news)
            elif g["stage"] == 1:
                g["shard"] = _shard_sums(f"shard_sums_{g['key']}", self.place, g["own"], g["recv"], news)
            else:
                for n, mine, other, row0 in zip(g["names"], g["shard"], news, g["row0s"]):
                    self.big[n] = _adamw_big(f"adamw_{n}_{row0}_{g['key']}", g["l"], row0, self.w[n], self.m[n], self.v[n],
                                             mine, other, self.big[n])[0]
                self.active.remove(g)
            g["stage"] += 1
        self.riding = []


WEIGHTS = ("norm_g", "w_in", "b_in", "ssm_log_dt", "ssm_lam_re", "ssm_lam_im", "ssm_b_re", "ssm_b_im", "ssm_c_re",
           "ssm_c_im", "ssm_d", "ssm_w_glu", "ssm_b_glu", "pool_w", "pool_scale", "w_branch_a", "w_branch_b", "w_out",
           "final_norm_g")
REPLICATED = SMALL + ("final_norm_g",)
DENSE = {"ssm_b_re": (DEPTH, G, P * C), "ssm_b_im": (DEPTH, G, P * C), "final_norm_g": (2, D // 2)}


def kernel(x, norm_g, w_in, b_in, ssm_log_dt, ssm_lam_re, ssm_lam_im, ssm_b_re, ssm_b_im, ssm_c_re, ssm_c_im, ssm_d, ssm_w_glu, ssm_b_glu, pool_w, pool_scale, w_branch_a, w_branch_b, w_out, final_norm_g, loss_target, m_norm_g, m_w_in, m_b_in, m_ssm_log_dt, m_ssm_lam_re, m_ssm_lam_im, m_ssm_b_re, m_ssm_b_im, m_ssm_c_re, m_ssm_c_im, m_ssm_d, m_ssm_w_glu, m_ssm_b_glu, m_pool_w, m_pool_scale, m_w_branch_a, m_w_branch_b, m_w_out, m_final_norm_g, v_norm_g, v_w_in, v_b_in, v_ssm_log_dt, v_ssm_lam_re, v_ssm_lam_im, v_ssm_b_re, v_ssm_b_im, v_ssm_c_re, v_ssm_c_im, v_ssm_d, v_ssm_w_glu, v_ssm_b_glu, v_pool_w, v_pool_scale, v_w_branch_a, v_w_branch_b, v_w_out, v_final_norm_g):
    w = dict(zip(WEIGHTS, (norm_g, w_in, b_in, ssm_log_dt, ssm_lam_re, ssm_lam_im, ssm_b_re, ssm_b_im, ssm_c_re,
                           ssm_c_im, ssm_d, ssm_w_glu, ssm_b_glu, pool_w, pool_scale, w_branch_a, w_branch_b, w_out,
                           final_norm_g)))
    m = dict(zip(WEIGHTS, (m_norm_g, m_w_in, m_b_in, m_ssm_log_dt, m_ssm_lam_re, m_ssm_lam_im, m_ssm_b_re, m_ssm_b_im,
                           m_ssm_c_re, m_ssm_c_im, m_ssm_d, m_ssm_w_glu, m_ssm_b_glu, m_pool_w, m_pool_scale,
                           m_w_branch_a, m_w_branch_b, m_w_out, m_final_norm_g)))
    v = dict(zip(WEIGHTS, (v_norm_g, v_w_in, v_b_in, v_ssm_log_dt, v_ssm_lam_re, v_ssm_lam_im, v_ssm_b_re, v_ssm_b_im,
                           v_ssm_c_re, v_ssm_c_im, v_ssm_d, v_ssm_w_glu, v_ssm_b_glu, v_pool_w, v_pool_scale,
                           v_w_branch_a, v_w_branch_b, v_w_out, v_final_norm_g)))
    place = jnp.stack([lax.axis_index("c"), 2 * lax.axis_index("x") + lax.axis_index("y")]).astype(jnp.int32)

    total_loss, dx, results = _step(x[0], loss_target[0], w, m, v, place)
    return (total_loss, dx[None], *[results[n][q] for q in range(4) for n in WEIGHTS])
```

```python
import functools

import jax
import jax.numpy as jnp
from jax import lax
from jax.experimental import pallas as pl
from jax.experimental.pallas import tpu as pltpu

F32, BF16 = jnp.float32, jnp.bfloat16
SDS = jax.ShapeDtypeStruct
MESH = pl.DeviceIdType.MESH

DEPTH = 2
L = 2048
D = 1024
NIN = 4096
W = 512
G, P, C = 32, 64, 16
GP = G * P
WINS = (2, 4, 8, 16)
TM = 256
NT = L // TM
TMM = 512
TK = 2048
LAST_ROWS = 512
EPS = 1e-6
VMEM_LIMIT = 56 * 2**20

LR, B1, B2, EPS_A, WD, STEP = 0.001, 0.9, 0.999, 1e-08, 0.01, 10


def _params(*sem):
    return pltpu.CompilerParams(dimension_semantics=sem, vmem_limit_bytes=VMEM_LIMIT)


_ANY = pl.BlockSpec(memory_space=pl.ANY)


def _full(shape):
    zeros = (0,) * len(shape)
    return pl.BlockSpec(shape, lambda *_: zeros)


def _layer(l, shape):
    zeros = (0,) * len(shape)
    return pl.BlockSpec((None,) + shape, lambda *_: (l,) + zeros)


def _rows(width, col=0, reverse=False, tm=TM):
    if reverse:
        return pl.BlockSpec((tm, width), lambda i: (L // tm - 1 - i, col))
    return pl.BlockSpec((tm, width), lambda i: (i, col))


def _rows_mm(width, col=0):
    return _rows(width, col, False, TMM)


def _pcall(body, name, grid, in_specs, out_specs, out_shape, args, scratch=(), aliases=None, jobs=()):
    in_specs, out_specs, out_shape, args, scratch = list(in_specs), list(out_specs), list(out_shape), list(args), list(scratch)
    aliases = dict(aliases or {})
    jobs = [j for j in jobs if j is not None]
    n_in, n_out, n_scr = len(in_specs), len(out_specs), len(scratch)
    srcs = [s for j in jobs for s in j.srcs]
    bufs = [b for j in jobs for b in j.bufs]
    news = [s for j in jobs for s in j.news]
    aliases.update({n_in + len(srcs) + k: n_out + k for k in range(len(bufs))})

    def hosted(*refs):
        cuts = [n_in, len(srcs), len(bufs), n_out, len(bufs), len(news), n_scr]
        parts, p = [], 0
        for n in cuts:
            parts.append(refs[p:p + n])
            p += n
        ins, src_r, _, outs, buf_r, new_r, scr = parts
        sem_r = refs[p:]
        views, ps, pb, pn, pm = [], 0, 0, 0, 0
        for j in jobs:
            views.append((src_r[ps:ps + len(j.srcs)], buf_r[pb:pb + len(j.bufs)], new_r[pn:pn + len(j.news)],
                          sem_r[pm:pm + len(j.scratch)]))
            ps, pb, pn, pm = ps + len(j.srcs), pb + len(j.bufs), pn + len(j.news), pm + len(j.scratch)

        def run(phase):
            for j, v in zip(jobs, views):
                getattr(j, phase)(*v)

        def at_step(step):
            return functools.reduce(jnp.logical_and, [pl.program_id(d) == step(d) for d in range(len(grid))])

        if not grid:
            run("start")
            run("finish")
            return
        pl.when(at_step(lambda d: 0))(lambda: run("start"))
        body(*ins, *outs, *scr)
        pl.when(at_step(lambda d: grid[d] - 1))(lambda: run("finish"))

    outs = pl.pallas_call(
        hosted if jobs else body, name=name, **({"grid": grid} if grid else {}),
        in_specs=in_specs + [_ANY] * (len(srcs) + len(bufs)), out_specs=out_specs + [_ANY] * (len(bufs) + len(news)),
        out_shape=out_shape + [SDS(b.shape, b.dtype) for b in bufs] + news,
        input_output_aliases=aliases, scratch_shapes=scratch + [s for j in jobs for s in j.scratch],
        compiler_params=_params(*(("arbitrary",) * len(grid))))(*args, *srcs, *bufs)
    res, pb, pn = [], n_out, n_out + len(bufs)
    for j in jobs:
        res.append((list(outs[pb:pb + len(j.bufs)]), list(outs[pn:pn + len(j.news)])))
        pb, pn = pb + len(j.bufs), pn + len(j.news)
    return list(outs[:n_out]), res


def _fused(name, grid, parts, shared=None, jobs=()):
    def body(*refs):
        pos = [0]

        def take(n):
            pos[0] += n
            return refs[pos[0] - n:pos[0]]

        ins = [take(len(p["in_specs"])) for p in parts]
        if shared:
            take(1)
            block = take(1)[0]
        outs = [take(len(p["out_specs"])) for p in parts]
        scr = [take(len(p["scratch"])) for p in parts]
        col = 0
        for p, i, o, s in zip(parts, ins, outs, scr):
            view = []
            if shared:
                view = [block.at[:, pl.ds(col, p["width"])]]
                col += p["width"]
            p["body"](*i, *view, *o, *s)

    in_specs = [s for p in parts for s in p["in_specs"]] + ([_ANY] if shared else [])
    out_specs = ([shared[1]] if shared else []) + [s for p in parts for s in p["out_specs"]]
    out_shape = ([SDS(shared[0].shape, shared[0].dtype)] if shared else []) + [s for p in parts for s in p["out_shape"]]
    args = [a for p in parts for a in p["args"]] + ([shared[0]] if shared else [])
    return _pcall(body, name, grid, in_specs, out_specs, out_shape, args, [s for p in parts for s in p["scratch"]],
                  {len(in_specs) - 1: 0} if shared else None, jobs)


def _dot(a, b):
    return jnp.dot(a, b, preferred_element_type=F32)


def _dot_nt(a, b):
    return lax.dot_general(a, b, (((1,), (1,)), ((), ())), preferred_element_type=F32)


def _dot_tn(a, b):
    return lax.dot_general(a, b, (((0,), (0,)), ((), ())), preferred_element_type=F32)


_K0 = 0.7978845608028654
_K1 = 0.044715


def _gelu(x):
    return 0.5 * x * (1.0 + jnp.tanh(_K0 * (x + _K1 * (x * x * x))))


def _gelu_grad(x):
    t = jnp.tanh(_K0 * (x + _K1 * (x * x * x)))
    return 0.5 * (1.0 + t) + 0.5 * x * (1.0 - t * t) * (_K0 * (1.0 + 3.0 * _K1 * x * x))


def _sigmoid(x):
    return jax.nn.sigmoid(x)


def _norm_inproj(l, x, g, w, b, jobs=()):
    def body(x_ref, g_ref, w_ref, b_ref, h_ref, proj_ref):
        xv = x_ref[...]
        r = lax.rsqrt(jnp.mean(xv * xv, axis=-1, keepdims=True) + EPS)
        hb = ((xv * r) * g_ref[...]).astype(BF16)
        h_ref[...] = hb
        for j in range(4):
            cs = slice(j * 1024, (j + 1) * 1024)
            proj_ref[:, cs] = _dot(hb, w_ref[j]) + b_ref[:, cs]

    return _pcall(
        body, f"norm_inproj_l{l}", (L // TMM,),
        [_rows_mm(D), _layer(l, (1, D)), _layer(l, (4, D, 1024)), _layer(l, (1, NIN))],
        [_rows_mm(D), _rows_mm(NIN)],
        [SDS((L, D), BF16), SDS((L, NIN), F32)],
        (x, g, w, b), jobs=jobs)


def _scan_tile(re_ref, im_ref, st_re, st_im, cr_re, cr_im, carry, reverse, each_chunk=None):
    def chunk(ci, carry):
        c = (TM // 8 - 1 - ci) if reverse else ci
        rows = pl.ds(pl.multiple_of(c * 8, 8), 8)
        new = []
        for lb in range(GP // 512):
            cols = slice(lb * 512, (lb + 1) * 512)
            vr = re_ref[rows, cols]
            vi = im_ref[rows, cols]
            for s, d in enumerate((1, 2, 4)):
                ar = st_re[8 * s:8 * s + 8, cols]
                ai = st_im[8 * s:8 * s + 8, cols]
                sr = pltpu.roll(vr, 8 - d if reverse else d, 0)
                si = pltpu.roll(vi, 8 - d if reverse else d, 0)
                vr, vi = vr + ar * sr - ai * si, vi + ar * si + ai * sr
            cr, ci_ = carry[2 * lb], carry[2 * lb + 1]
            pr = cr_re[:, cols]
            pi = cr_im[:, cols]
            vr, vi = vr + pr * cr - pi * ci_, vi + pr * ci_ + pi * cr
            re_ref[rows, cols] = vr
            im_ref[rows, cols] = vi
            if each_chunk is not None:
                each_chunk(c, cols, vr, vi)
            if reverse:
                new += [vr[0:1], vi[0:1]]
            else:
                new += [vr[7:8], vi[7:8]]
        return tuple(new)

    return lax.fori_loop(0, TM // 8, chunk, carry)


def _load_carry(car_ref):
    return tuple(car_ref[r:r + 1, lb * 512:(lb + 1) * 512] for lb in range(GP // 512) for r in (0, 1))


def _store_carry(car_ref, carry):
    for lb in range(GP // 512):
        car_ref[0:1, lb * 512:(lb + 1) * 512] = carry[2 * lb]
        car_ref[1:2, lb * 512:(lb + 1) * 512] = carry[2 * lb + 1]


def _s5_fwd(l, proj, bexp, cre, cimn, powers, dsk, wglu, bglu):
    def body(ua_ref, za_ref, bexp_ref, cre_ref, cimn_ref, st_re_ref, st_im_ref, cr_re_ref, cr_im_ref,
             d_ref, wg_ref, bg_ref, sre_ref, sim_ref, y1_ref, q_ref, ya_ref, car_ref):
        @pl.when(pl.program_id(0) == 0)
        def _():
            car_ref[...] = jnp.zeros_like(car_ref)

        u = ua_ref[...]
        ub = u.astype(BF16)
        for k in range(4):
            bu = _dot(ub[:, 128 * k:128 * (k + 1)], bexp_ref[k])
            sre_ref[:, 512 * k:512 * (k + 1)] = bu[:, :512]
            sim_ref[:, 512 * k:512 * (k + 1)] = bu[:, 512:]
        carry = _scan_tile(sre_ref, sim_ref, st_re_ref, st_im_ref, cr_re_ref, cr_im_ref, _load_carry(car_ref), False)
        _store_carry(car_ref, carry)
        for k in range(4):
            blk = slice(512 * k, 512 * (k + 1))
            ks = slice(128 * k, 128 * (k + 1))
            y0 = _dot(sre_ref[:, blk].astype(BF16), cre_ref[k]) + _dot(sim_ref[:, blk].astype(BF16), cimn_ref[k])
            y1_ref[:, ks] = y0 + d_ref[:, ks] * u[:, ks]
        y2 = _gelu(y1_ref[...])
        q = _dot(y2.astype(BF16), wg_ref[...]) + bg_ref[...]
        q_ref[...] = q
        za = za_ref[...]
        ya_ref[...] = ((y2 * _sigmoid(q)) * (za * _sigmoid(za))).astype(BF16)

    return dict(
        body=body,
        in_specs=[_rows(W, 0), _rows(W, 1), _layer(l, (4, 128, 1024)), _layer(l, (4, 512, 128)),
                  _layer(l, (4, 512, 128)), _layer(l, (24, GP)), _layer(l, (24, GP)), _layer(l, (8, GP)),
                  _layer(l, (8, GP)), _layer(l, (1, W)), _layer(l, (W, W)), _layer(l, (1, W))],
        out_specs=[_rows(GP), _rows(GP), _rows(W), _rows(W), _rows(W)],
        out_shape=[SDS((L, GP), F32), SDS((L, GP), F32), SDS((L, W), F32), SDS((L, W), F32), SDS((L, W), BF16)],
        args=(proj, proj, bexp, cre, cimn, *powers, dsk, wglu, bglu),
        scratch=[pltpu.VMEM((8, GP), F32)])


def _pool_fwd(l, proj, pw, scale):
    def body(ub_ref, zb_ref, pw_ref, sc_ref, pooled_ref, mixed_ref, yb_ref, buf):
        i = pl.program_id(0)

        @pl.when(i == 0)
        def _():
            buf[0:16, :] = jnp.zeros((16, W), F32)

        u = ub_ref[...]
        buf[16:16 + TM, :] = u
        t = i * TM + lax.broadcasted_iota(jnp.int32, (TM, 128), 0)
        for gi, win in enumerate(WINS):
            cs = slice(128 * gi, 128 * (gi + 1))
            acc = u[:, cs]
            for k in range(1, win):
                acc = acc + buf[16 - k:16 - k + TM, cs]
            cnt = jnp.minimum(t + 1, win).astype(F32)
            pb = (acc / cnt - u[:, cs]).astype(BF16)
            pooled_ref[:, cs] = pb
            mixed_ref[:, cs] = _dot(pb, pw_ref[gi])
        zb = zb_ref[...]
        yb_ref[...] = ((mixed_ref[...] * sc_ref[...]) * (zb * _sigmoid(zb))).astype(BF16)
        buf[0:16, :] = buf[TM:TM + 16, :]

    return dict(
        body=body,
        in_specs=[_rows(W, 2), _rows(W, 3), _layer(l, (4, 128, 128)), _layer(l, (1, W))],
        out_specs=[_rows(W), _rows(W), _rows(W)],
        out_shape=[SDS((L, W), BF16), SDS((L, W), F32), SDS((L, W), BF16)],
        args=(proj, proj, pw, scale),
        scratch=[pltpu.VMEM((TM + 16, W), F32)])


def _merge_out(l, ya, yb, proj, x, wa, wb, wo, head=None):
    def body(ya_ref, yb_ref, ga_ref, gb_ref, x_ref, wa_ref, wb_ref, wo_ref, *rest):
        pa_ref, pb_ref, mg_ref = rest[2:5] if head else rest[0:3]
        ya = ya_ref[...]
        yb = yb_ref[...]
        for j in range(4):
            cs = slice(256 * j, 256 * (j + 1))
            pa_ref[:, cs] = _dot(ya, wa_ref[j])
            pb_ref[:, cs] = _dot(yb, wb_ref[j])
        merged = _sigmoid(ga_ref[...]) * pa_ref[...] + _sigmoid(gb_ref[...]) * pb_ref[...]
        mb = merged.astype(BF16)
        mg_ref[...] = mb
        x_next = x_ref[...] + _dot(mb, wo_ref[...])
        if head:
            _loss_tile(x_next, rest[0], rest[1], *rest[5:])
        else:
            rest[3][...] = x_next

    rows, out_rows = _rows_mm(D), SDS((L, D), F32)
    return pl.pallas_call(
        body, name=f"merge_out_l{l}", grid=(L // TMM,),
        in_specs=[_rows_mm(W), _rows_mm(W), _rows_mm(D, 2), _rows_mm(D, 3), rows,
                  _layer(l, (4, W, 256)), _layer(l, (4, W, 256)), _layer(l, (D, D))] + ([_full((1, D)), rows] if head else []),
        out_specs=[rows, rows, rows] + ([_full((2, 128)), rows, _full((1, D))] if head else [rows]),
        out_shape=[out_rows, out_rows, SDS((L, D), BF16)]
        + ([SDS((2, 128), F32), out_rows, SDS((1, D), F32)] if head else [out_rows]),
        compiler_params=_params("arbitrary"),
    )(ya, yb, proj, proj, x, wa, wb, wo, *(head or ()))


def _loss_tile(xv, g_ref, t_ref, loss_ref, dx_ref, dg_ref):
    @pl.when(pl.program_id(0) == 0)
    def _():
        loss_ref[...] = jnp.zeros_like(loss_ref)
        dg_ref[...] = jnp.zeros_like(dg_ref)

    g = g_ref[...]
    r = lax.rsqrt(jnp.mean(xv * xv, axis=-1, keepdims=True) + EPS)
    xn = xv * r
    err = xn * g - t_ref[...]
    part = jnp.sum(jnp.mean(err * err, axis=-1, keepdims=True), axis=0, keepdims=True)
    loss_ref[...] += 0.5 * part
    dy = err * (1.0 / D)
    dg_ref[...] += jnp.sum(dy * xn, axis=0, keepdims=True)
    dxn = dy * g
    dx_ref[...] = r * (dxn - xn * jnp.mean(dxn * xn, axis=-1, keepdims=True))


def _merge_out_bwd(l, dxn, mg, proj, pa, pb, ya, yb, wo, wa, wb, jobs=()):
    def body(dx_ref, mg_ref, ga_ref, gb_ref, pa_ref, pb_ref, ya_ref, yb_ref, wo_ref, wa_ref, wb_ref,
             dg_ref, dya_ref, dyb_ref, dwo_ref, dwa_ref, dwb_ref, dbias_ref):
        @pl.when(pl.program_id(0) == 0)
        def _():
            for ref in (dwo_ref, dwa_ref, dwb_ref, dbias_ref):
                ref[...] = jnp.zeros_like(ref)

        dxb = dx_ref[...].astype(BF16)
        dm = _dot_nt(dxb, wo_ref[...])
        sa = _sigmoid(ga_ref[...])
        sb = _sigmoid(gb_ref[...])
        dga = dm * pa_ref[...] * (sa * (1.0 - sa))
        dgb = dm * pb_ref[...] * (sb * (1.0 - sb))
        dg_ref[:, :D] = dga.astype(BF16)
        dg_ref[:, D:] = dgb.astype(BF16)
        dbias_ref[:, :D] += jnp.sum(dga, axis=0, keepdims=True)
        dbias_ref[:, D:] += jnp.sum(dgb, axis=0, keepdims=True)
        dpa = (dm * sa).astype(BF16)
        dpb = (dm * sb).astype(BF16)
        ya = ya_ref[...]
        yb = yb_ref[...]
        dya = jnp.zeros((TM, W), F32)
        dyb = jnp.zeros((TM, W), F32)
        for j in range(4):
            cs = slice(256 * j, 256 * (j + 1))
            dya = dya + _dot_nt(dpa[:, cs], wa_ref[j])
            dyb = dyb + _dot_nt(dpb[:, cs], wb_ref[j])
            dwa_ref[j] += _dot_tn(ya, dpa[:, cs])
            dwb_ref[j] += _dot_tn(yb, dpb[:, cs])
        dya_ref[...] = dya
        dyb_ref[...] = dyb
        dwo_ref[...] += _dot_tn(mg_ref[...], dxb)

    return _pcall(
        body, f"merge_out_bwd_l{l}", (NT,),
        [_rows(D), _rows(D), _rows(D, 2), _rows(D, 3), _rows(D), _rows(D), _rows(W), _rows(W),
         _layer(l, (D, D)), _layer(l, (4, W, 256)), _layer(l, (4, W, 256))],
        [_rows(2 * D, 1), _rows(W), _rows(W), _full((D, D)), _full((4, W, 256)), _full((4, W, 256)), _full((1, 2 * D))],
        [SDS((L, NIN), BF16), SDS((L, W), F32), SDS((L, W), F32),
         SDS((D, D), F32), SDS((4, W, 256), F32), SDS((4, W, 256), F32), SDS((1, 2 * D), F32)],
        (dxn, mg, proj, proj, pa, pb, ya, yb, wo, wa, wb), jobs=jobs)


def _pool_bwd(l, dyb, proj, mixed, pooled, pw, scale):
    def body(dyb_ref, zb_ref, mixed_ref, pooled_ref, pw_ref, sc_ref, db_ref, dpw_ref, dsc_ref, dbias_ref, buf):
        i = pl.program_id(0)
        tile = NT - 1 - i

        @pl.when(i == 0)
        def _():
            dpw_ref[...] = jnp.zeros_like(dpw_ref)
            dsc_ref[...] = jnp.zeros_like(dsc_ref)
            dbias_ref[...] = jnp.zeros_like(dbias_ref)
            buf[TM:TM + 16, :] = jnp.zeros((16, W), F32)

        dyb = dyb_ref[...]
        zb = zb_ref[...]
        mixed = mixed_ref[...]
        sc = sc_ref[...]
        sg = _sigmoid(zb)
        dyb0 = dyb * (zb * sg)
        dzb = dyb * (mixed * sc) * (sg * (1.0 + zb * (1.0 - sg)))
        db_ref[:, W:] = dzb.astype(BF16)
        dbias_ref[:, W:] += jnp.sum(dzb, axis=0, keepdims=True)
        dsc_ref[...] += jnp.sum(dyb0 * mixed, axis=0, keepdims=True)
        dmix = (dyb0 * sc).astype(BF16)
        t = tile * TM + lax.broadcasted_iota(jnp.int32, (TM, 128), 0)
        for gi, win in enumerate(WINS):
            cs = slice(128 * gi, 128 * (gi + 1))
            dpw_ref[gi] += _dot_tn(pooled_ref[:, cs], dmix[:, cs])
            dpool = _dot_nt(dmix[:, cs], pw_ref[gi])
            cnt = jnp.minimum(t + 1, win).astype(F32)
            e = dpool / cnt
            buf[0:TM, cs] = e
            acc = e - dpool
            for k in range(1, win):
                acc = acc + buf[k:k + TM, cs]
            db_ref[:, cs] = acc.astype(BF16)
            dbias_ref[:, cs] += jnp.sum(acc, axis=0, keepdims=True)
        buf[TM:TM + 16, :] = buf[0:16, :]

    return dict(
        body=body, width=2 * W,
        in_specs=[_rows(W, 0, True), _rows(W, 3, True), _rows(W, 0, True), _rows(W, 0, True),
                  _layer(l, (4, 128, 128)), _layer(l, (1, W))],
        out_specs=[_full((4, 128, 128)), _full((1, W)), _full((1, 2 * W))],
        out_shape=[SDS((4, 128, 128), F32), SDS((1, W), F32), SDS((1, 2 * W), F32)],
        args=(dyb, proj, mixed, pooled, pw, scale),
        scratch=[pltpu.VMEM((TM + 16, W), F32)])


def _s5_bwd(l, dya, proj, y1, q, sre, sim, cret, cimnt, bret, bimt, st_re, st_im, cr_re, cr_im, dsk, wglu):
    def halo(i):
        return (jnp.maximum((NT - 1 - i) * (TM // 8) - 1, 0), 0)

    def body(dya_ref, ua_ref, za_ref, y1_ref, q_ref, sre_ref, sim_ref, hre_ref, him_ref,
             cret_ref, cimnt_ref, bret_ref, bimt_ref, st_re_ref, st_im_ref, cr_re_ref, cr_im_ref, d_ref, wg_ref,
             da_ref, dwg_ref, dbg_ref, dd_ref, dcre_ref, dcimn_ref, dbre_ref, dbim_ref, dare_ref, daim_ref, dbias_ref,
             lre, lim, car_ref):
        i = pl.program_id(0)
        tile = NT - 1 - i

        @pl.when(i == 0)
        def _():
            for ref in (dwg_ref, dbg_ref, dd_ref, dcre_ref, dcimn_ref, dbre_ref, dbim_ref, dare_ref, daim_ref, dbias_ref,
                        car_ref):
                ref[...] = jnp.zeros_like(ref)

        u = ua_ref[...]
        za = za_ref[...]
        y1 = y1_ref[...]
        dya = dya_ref[...]
        y2 = _gelu(y1)
        sg = _sigmoid(q_ref[...])
        sgz = _sigmoid(za)
        dy3 = dya * (za * sgz)
        dza = dya * (y2 * sg) * (sgz * (1.0 + za * (1.0 - sgz)))
        da_ref[:, W:] = dza.astype(BF16)
        dbias_ref[:, W:] += jnp.sum(dza, axis=0, keepdims=True)
        dq = dy3 * y2 * (sg * (1.0 - sg))
        dqb = dq.astype(BF16)
        dy2 = dy3 * sg + _dot_nt(dqb, wg_ref[...])
        dwg_ref[...] += _dot_tn(y2.astype(BF16), dqb)
        dbg_ref[...] += jnp.sum(dq, axis=0, keepdims=True)
        dy1 = dy2 * _gelu_grad(y1)
        dd_ref[...] += jnp.sum(dy1 * u, axis=0, keepdims=True)
        dy1b = dy1.astype(BF16)
        ub = u.astype(BF16)
        for k in range(4):
            blk = slice(512 * k, 512 * (k + 1))
            ks = slice(128 * k, 128 * (k + 1))
            lre[:, blk] = _dot(dy1b[:, ks], cret_ref[k])
            lim[:, blk] = _dot(dy1b[:, ks], cimnt_ref[k])
            dcre_ref[k] += _dot_tn(sre_ref[:, blk].astype(BF16), dy1b[:, ks])
            dcimn_ref[k] += _dot_tn(sim_ref[:, blk].astype(BF16), dy1b[:, ks])
        rowid = lax.broadcasted_iota(jnp.int32, (8, 512), 0)
        gate = (tile > 0).astype(F32)

        def d_abar(c, cols, lr, li):
            rows = pl.ds(pl.multiple_of(c * 8, 8), 8)
            prows = pl.ds(pl.multiple_of(jnp.maximum(c - 1, 0) * 8, 8), 8)
            pr = jnp.where(c == 0, hre_ref[7:8, cols] * gate, sre_ref[prows, cols][7:8])
            pi = jnp.where(c == 0, him_ref[7:8, cols] * gate, sim_ref[prows, cols][7:8])
            sr = jnp.where(rowid == 0, pr, pltpu.roll(sre_ref[rows, cols], 1, 0))
            si = jnp.where(rowid == 0, pi, pltpu.roll(sim_ref[rows, cols], 1, 0))
            dare_ref[:, cols] += sr * lr + si * li
            daim_ref[:, cols] += sr * li - si * lr

        carry = _scan_tile(lre, lim, st_re_ref, st_im_ref, cr_re_ref, cr_im_ref, _load_carry(car_ref), True, d_abar)
        _store_carry(car_ref, carry)

        for k in range(4):
            blk = slice(512 * k, 512 * (k + 1))
            ks = slice(128 * k, 128 * (k + 1))
            lrb = lre[:, blk].astype(BF16)
            lib = lim[:, blk].astype(BF16)
            du = dy1[:, ks] * d_ref[:, ks] + _dot(lrb, bret_ref[k]) + _dot(lib, bimt_ref[k])
            da_ref[:, ks] = du.astype(BF16)
            dbias_ref[:, ks] += jnp.sum(du, axis=0, keepdims=True)
            dbre_ref[k] += _dot_tn(ub[:, ks], lrb)
            dbim_ref[k] += _dot_tn(ub[:, ks], lib)

    return dict(
        body=body, width=2 * W,
        in_specs=[_rows(W, 0, True), _rows(W, 0, True), _rows(W, 1, True), _rows(W, 0, True), _rows(W, 0, True),
                  _rows(GP, 0, True), _rows(GP, 0, True),
                  pl.BlockSpec((8, GP), halo), pl.BlockSpec((8, GP), halo),
                  _layer(l, (4, 128, 512)), _layer(l, (4, 128, 512)), _layer(l, (4, 512, 128)), _layer(l, (4, 512, 128)),
                  _layer(l, (24, GP)), _layer(l, (24, GP)), _layer(l, (8, GP)), _layer(l, (8, GP)), _layer(l, (1, W)),
                  _layer(l, (W, W))],
        out_specs=[_full((W, W)), _full((1, W)), _full((1, W)),
                   _full((4, 512, 128)), _full((4, 512, 128)), _full((4, 128, 512)), _full((4, 128, 512)),
                   _full((8, GP)), _full((8, GP)), _full((1, 2 * W))],
        out_shape=[SDS((W, W), F32), SDS((1, W), F32), SDS((1, W), F32),
                   SDS((4, 512, 128), F32), SDS((4, 512, 128), F32), SDS((4, 128, 512), F32), SDS((4, 128, 512), F32),
                   SDS((8, GP), F32), SDS((8, GP), F32), SDS((1, 2 * W), F32)],
        args=(dya, proj, proj, y1, q, sre, sim, sre, sim, cret, cimnt, bret, bimt, st_re, st_im, cr_re, cr_im, dsk,
              wglu),
        scratch=[pltpu.VMEM((TM, GP), F32), pltpu.VMEM((TM, GP), F32), pltpu.VMEM((8, GP), F32)])


def _inproj_dw(l, r, h, dproj, jobs=(), rows=D // 2):
    def body(h_ref, dp_ref, dw_ref):
        part = _dot_tn(h_ref[...], dp_ref[...])

        @pl.when(pl.program_id(1) == 0)
        def _():
            dw_ref[...] = part

        @pl.when(pl.program_id(1) > 0)
        def _():
            dw_ref[...] += part

    return _pcall(
        body, f"inproj_dw{r}_l{l}", (4, L // TK),
        [pl.BlockSpec((TK, rows), lambda j, i: (i, r)), pl.BlockSpec((TK, 1024), lambda j, i: (i, j))],
        [pl.BlockSpec((None, rows, 1024), lambda j, i: (j, 0, 0))],
        [SDS((4, rows, 1024), F32)],
        (h, dproj), jobs=jobs)


def _inproj_dx(l, dproj, w, x, g, dxn, jobs=()):
    def body(dp_ref, w_ref, x_ref, g_ref, dxn_ref, dx_ref, dg_ref):
        @pl.when(pl.program_id(0) == 0)
        def _():
            dg_ref[...] = jnp.zeros_like(dg_ref)

        dh = _dot_nt(dp_ref[:, 0:1024], w_ref[0])
        for j in range(1, 4):
            dh = dh + _dot_nt(dp_ref[:, j * 1024:(j + 1) * 1024], w_ref[j])
        xv = x_ref[...]
        r = lax.rsqrt(jnp.mean(xv * xv, axis=-1, keepdims=True) + EPS)
        xn = xv * r
        dg_ref[...] += jnp.sum(dh * xn, axis=0, keepdims=True)
        dn = dh * g_ref[...]
        dx_ref[...] = dxn_ref[...] + r * (dn - xn * jnp.mean(dn * xn, axis=-1, keepdims=True))

    return _pcall(
        body, f"inproj_dx_l{l}", (L // TMM,),
        [_rows_mm(NIN), _layer(l, (4, D, 1024)), _rows_mm(D), _layer(l, (1, D)), _rows_mm(D)],
        [_rows_mm(D), _full((1, D))],
        [SDS((L, D), F32), SDS((1, D), F32)],
        (dproj, w, x, g, dxn), jobs=jobs)


def _discretize(log_dt, lam_re, lam_im, b_re, b_im):
    dt = jnp.exp(log_dt)[..., None]
    mag = jnp.exp(lam_re * dt)
    ang = lam_im * dt
    abar_re = mag * jnp.cos(ang)
    abar_im = mag * jnp.sin(ang)
    num_re = abar_re - 1.0
    num_im = abar_im
    den = lam_re * lam_re + lam_im * lam_im
    coef_re = (num_re * lam_re + num_im * lam_im) / den
    coef_im = (num_im * lam_re - num_re * lam_im) / den
    bbar_re = coef_re[..., None] * b_re - coef_im[..., None] * b_im
    bbar_im = coef_re[..., None] * b_im + coef_im[..., None] * b_re
    return abar_re, abar_im, bbar_re, bbar_im


def _powers(abar_re, abar_im):
    ar, ai = abar_re.reshape(DEPTH, 1, GP), abar_im.reshape(DEPTH, 1, GP)
    rows_re, rows_im = [ar], [ai]
    for _ in range(7):
        pr, pi = rows_re[-1], rows_im[-1]
        rows_re.append(pr * ar - pi * ai)
        rows_im.append(pr * ai + pi * ar)
    row = jnp.arange(8)[:, None]

    def steps(rows, keep):
        return jnp.concatenate([jnp.where(keep(d), rows[d - 1], 0.0) for d in (1, 2, 4)], axis=1)

    neg_im = [-r for r in rows_im]
    fwd = (steps(rows_re, lambda d: row >= d), steps(rows_im, lambda d: row >= d),
           jnp.concatenate(rows_re, axis=1), jnp.concatenate(rows_im, axis=1))
    rev = (steps(rows_re, lambda d: row < 8 - d), steps(neg_im, lambda d: row < 8 - d),
           jnp.concatenate(rows_re[::-1], axis=1), jnp.concatenate(neg_im[::-1], axis=1))
    return fwd, rev


_EYE8 = functools.partial(jnp.eye, 8, dtype=F32)


def _expand_in(b):
    return jnp.einsum("lkgpc,gh->lkgchp", b.reshape(DEPTH, 4, 8, P, C), _EYE8()).reshape(DEPTH, 4, 128, 512)


def _extract_in(e):
    return jnp.einsum("lkgchp,gh->lkgpc", e.reshape(DEPTH, 4, 8, C, 8, P), _EYE8()).reshape(DEPTH, G, P, C)


def _expand_out(c):
    return jnp.einsum("lkgcp,gh->lkgphc", c.reshape(DEPTH, 4, 8, C, P), _EYE8()).reshape(DEPTH, 4, 512, 128)


def _extract_out(e):
    return jnp.einsum("lkgphc,gh->lkgcp", e.reshape(DEPTH, 4, 8, P, 8, C), _EYE8()).reshape(DEPTH, G, C, P)


SMALL = ("norm_g", "b_in", "ssm_log_dt", "ssm_lam_re", "ssm_lam_im", "ssm_b_re", "ssm_b_im",
         "ssm_c_re", "ssm_c_im", "ssm_d", "ssm_b_glu", "pool_w", "pool_scale")
BIG = ("w_in", "ssm_w_glu", "w_branch_a", "w_branch_b", "w_out")


def _step(x, target, w, m, v, place):
    sp = {n: w[n] for n in SMALL}
    final_norm_g = w["final_norm_g"]
    wbuf = dict(zip(BIG, _cast_own(place, [w[n] for n in BIG])))
    (abar_re, abar_im, bbar_re, bbar_im), disc_vjp = jax.vjp(
        _discretize, *(sp[n] for n in ("ssm_log_dt", "ssm_lam_re", "ssm_lam_im", "ssm_b_re", "ssm_b_im")))
    powers_fwd, powers_rev = _powers(abar_re, abar_im)
    b_re_x, b_im_x = _expand_in(bbar_re), _expand_in(bbar_im)
    c_re_x, c_imn_x = _expand_out(sp["ssm_c_re"]), _expand_out(-sp["ssm_c_im"])
    b_x = jnp.concatenate([b_re_x, b_im_x], axis=3).astype(BF16)
    t = lambda a: jnp.swapaxes(a, 2, 3).astype(BF16)
    c_re_t, c_imn_t, b_re_t, b_im_t = t(c_re_x), t(c_imn_x), t(b_re_x), t(b_im_x)
    c_re_x, c_imn_x = c_re_x.astype(BF16), c_imn_x.astype(BF16)
    row = lambda n: sp[n].reshape(DEPTH, 1, -1)
    g, b_in, dsk, b_glu, scale = row("norm_g"), row("b_in"), row("ssm_d"), row("ssm_b_glu"), row("pool_scale")
    pw = sp["pool_w"].astype(BF16)

    saved = []
    for l in range(DEPTH):
        three = BIG[2:]
        if l == 0:
            wbuf["w_in"], wbuf["ssm_w_glu"] = _comm_only(
                "gather_first", _RingGatherJob([wbuf["w_in"], wbuf["ssm_w_glu"]], 0))[0]
            jobs = [_GatherJob([wbuf[n] for n in three], 0), _GatherJob([wbuf["ssm_w_glu"]], 1)]
        else:
            jobs = []
        (h, proj), res = _norm_inproj(l, x, g, wbuf["w_in"], b_in, jobs)
        if res:
            wbuf.update(zip(three, res[0][0]))
            (wbuf["ssm_w_glu"],) = res[1][0]
        wg = dict(wbuf, ssm_w_glu=wbuf["ssm_w_glu"].reshape(DEPTH, W, W), w_out=wbuf["w_out"].reshape(DEPTH, D, D))
        job = _GatherJob([wbuf["w_in"]], l + 1) if l + 1 < DEPTH else _GatherJob([wbuf[n] for n in three], l)
        (sre, sim, y1, q, ya, pooled, mixed, yb), res = _fused(
            f"branches_fwd_l{l}", (NT,),
            [_s5_fwd(l, proj, b_x, c_re_x, c_imn_x, powers_fwd, dsk, wg["ssm_w_glu"], b_glu),
             _pool_fwd(l, proj, pw, scale)], jobs=[job])
        if l + 1 < DEPTH:
            (wbuf["w_in"],) = res[0][0]
        else:
            wbuf.update(zip(three, res[0][0]))
        wg = dict(wbuf, ssm_w_glu=wbuf["ssm_w_glu"].reshape(DEPTH, W, W), w_out=wbuf["w_out"].reshape(DEPTH, D, D))
        last = l + 1 == DEPTH
        pa, pb, mg, *tail = _merge_out(l, ya, yb, proj, x, wg["w_branch_a"], wg["w_branch_b"], wg["w_out"],
                                       (final_norm_g.reshape(1, D), target) if last else None)
        saved.append(dict(x=x, h=h, proj=proj, sre=sre, sim=sim, y1=y1, q=q, ya=ya,
                          pooled=pooled, mixed=mixed, yb=yb, pa=pa, pb=pb, mg=mg))
        if last:
            loss, dx, dgf = tail
        else:
            (x,) = tail

    per_layer = {n: [None] * DEPTH for n in ("norm_g", "b_in", "ssm_d", "ssm_b_glu", "pool_w", "pool_scale",
                                             "dare", "daim", "dbre", "dbim", "dcre", "dcimn")}
    red = _Reducer(place, w, m, v)
    for l in reversed(range(DEPTH)):
        s = saved[l]
        (dproj, dya, dyb, dwo, dwa, dwb, dbias_g), res = _merge_out_bwd(
            l, dx, s["mg"], s["proj"], s["pa"], s["pb"], s["ya"], s["yb"],
            wg["w_out"], wg["w_branch_a"], wg["w_branch_b"], red.jobs())
        red.land(res)
        (dproj, dwg, dbg, dd, dcre, dcimn, dbre, dbim, dare, daim, dbias_a, dpw, dsc, dbias_b), res = _fused(
            f"branches_bwd_l{l}", (NT,),
            [_s5_bwd(l, dya, s["proj"], s["y1"], s["q"], s["sre"], s["sim"], c_re_t, c_imn_t, b_re_t, b_im_t,
                     *powers_rev, dsk, wg["ssm_w_glu"]),
             _pool_bwd(l, dyb, s["proj"], s["mixed"], s["pooled"], pw, scale)],
            shared=(dproj, _rows(4 * W, 0, True)), jobs=red.jobs())
        red.land(res)
        rest = [dwg.reshape(4, W // 4, W), dwa, dwb, dwo.reshape(4, D // 4, D)]
        if l == 0:
            red.add(l, "rest", BIG[1:], rest)
        if l == 0:
            for r, rows in ((0, D - LAST_ROWS), (D // LAST_ROWS - 1, LAST_ROWS)):
                outs, res = _inproj_dw(l, r, s["h"], dproj, red.jobs(), rows=rows)
                red.land(res)
                red.add(l, f"in{r}", BIG[:1], outs, [r * rows])
        else:
            dwin, res = _inproj_dw(l, 0, s["h"], dproj, red.jobs(), rows=D)
            red.land(res)
        (dx, dg), res = _inproj_dx(l, dproj, wg["w_in"], s["x"], g, dx, red.jobs())
        red.land(res)
        if l > 0:
            red.add(l, "all", BIG, dwin + rest)
        for n, a in (("norm_g", dg.reshape(D)), ("b_in", jnp.concatenate([dbias_a, dbias_b, dbias_g], axis=1).reshape(NIN)),
                     ("ssm_d", dd.reshape(W)), ("ssm_b_glu", dbg.reshape(W)), ("pool_w", dpw), ("pool_scale", dsc.reshape(W)),
                     ("dare", dare), ("daim", daim), ("dbre", dbre), ("dbim", dbim), ("dcre", dcre), ("dcimn", dcimn)):
            per_layer[n][l] = a
    gs = {n: jnp.stack(a) for n, a in per_layer.items()}
    d_abar = [jnp.sum(gs.pop(n), axis=1).reshape(DEPTH, G, P) for n in ("dare", "daim")]
    (gs["ssm_log_dt"], gs["ssm_lam_re"], gs["ssm_lam_im"], gs["ssm_b_re"], gs["ssm_b_im"]) = disc_vjp(
        (*d_abar, _extract_in(gs.pop("dbre")), _extract_in(gs.pop("dbim"))))
    gs["ssm_c_re"], gs["ssm_c_im"] = _extract_out(gs.pop("dcre")), -_extract_out(gs.pop("dcimn"))
    gs["final_norm_g"] = dgf

    natural = {n: w[n].shape for n in REPLICATED}
    rw, rm, rv = {}, {}, {}
    for n in REPLICATED:
        shape = DENSE.get(n, natural[n])
        gs[n], rw[n], rm[n], rv[n] = (a.reshape(shape) for a in (gs[n], w[n], m[n], v[n]))
    small = [gs[n] for n in REPLICATED] + [loss]
    jobs = red.jobs()
    res = _pcall(None, "tail_exchange", (), [], [], [], [], jobs=jobs + [_SiblingJob(small, False)])[1]
    red.land(res[:len(jobs)])
    pair_small = _small_pair_sum(place, small, res[-1][1], [BF16 if a.ndim > 2 else F32 for a in small])
    jobs = red.jobs()
    res = _pcall(None, "tail_gather", (), [], [], [], [], jobs=jobs + [_ChipGatherJob(pair_small)])[1]
    red.land(res[:len(jobs)])
    assert not red.active
    small_parts = dict(zip(REPLICATED + ("loss",), res[-1][0]))

    k = len(REPLICATED)
    outs = _adamw_small("adamw_small", [rw[n] for n in REPLICATED], [small_parts[n] for n in REPLICATED],
                        [rm[n] for n in REPLICATED], [rv[n] for n in REPLICATED], small_parts["loss"])
    results = {n: red.big[n] for n in BIG}
    results.update({n: [outs[1 + q * k + i].reshape(natural[n]) for q in range(4)] for i, n in enumerate(REPLICATED)})
    return outs[0][0, 0], dx, results


def _place():
    x, y, c = lax.axis_index("x"), lax.axis_index("y"), lax.axis_index("c")
    chips = [(1 - x, y), (x, 1 - y), (1 - x, 1 - y)]
    return x, y, c, 2 * x + y, chips, [2 * cx + cy for cx, cy in chips]


def _remote(src, dst, ssem, rsem, dev):
    return pltpu.make_async_remote_copy(src_ref=src, dst_ref=dst, send_sem=ssem, recv_sem=rsem,
                                        device_id=dev, device_id_type=MESH)


class _GatherJob:
    def __init__(self, bufs, l):
        self.srcs, self.bufs, self.news, self.l = [], list(bufs), [], l
        self.scratch = [pltpu.SemaphoreType.DMA((len(self.bufs), 3))] * 4

    def _half(self, ref, k, h):
        rows = ref.shape[2] // 2
        return ref.at[self.l, k, pl.ds(pl.multiple_of(h * rows, 8), rows), :]

    def _ici(self, bufs, sems, a, j, k):
        _, _, c, _, chips, _ = _place()
        blk = self._half(bufs[a], k, c)
        return _remote(blk, blk, sems[0].at[a, j], sems[1].at[a, j], (*chips[j], c))

    def _d2d(self, bufs, sems, a, j, k, h):
        x, y, c, _, _, _ = _place()
        blk = self._half(bufs[a], k, h)
        return _remote(blk, blk, sems[2].at[a, j], sems[3].at[a, j], (x, y, 1 - c))

    def start(self, srcs, bufs, news, sems):
        me = _place()[3]
        for a in range(len(self.bufs)):
            for j in range(3):
                self._ici(bufs, sems, a, j, me).start()

    def finish(self, srcs, bufs, news, sems):
        _, _, c, me, _, cid = _place()
        pairs = [(a, j) for a in range(len(self.bufs)) for j in range(3)]
        for a, j in pairs:
            self._ici(bufs, sems, a, j, cid[j]).wait_recv()
            self._d2d(bufs, sems, a, j, cid[j], c).start()
        for a, j in pairs:
            self._d2d(bufs, sems, a, j, cid[j], 1 - c).wait_recv()
        for a, j in pairs:
            self._ici(bufs, sems, a, j, me).wait_send()
            self._d2d(bufs, sems, a, j, cid[j], c).wait_send()


class _RingGatherJob(_GatherJob):
    def __init__(self, bufs, l):
        super().__init__(bufs, l)
        n = len(self.bufs)
        self.scratch = [pltpu.SemaphoreType.DMA((n, 2))] * 4 + [pltpu.SemaphoreType.DMA((n, 4))] * 2

    def _rows(self, ref, k, h, part):
        half = ref.shape[2] // 2
        start, rows = (h * half, half) if part is None else (h * half + part * (half // 2), half // 2)
        return ref.at[self.l, k, pl.ds(pl.multiple_of(start, 8), rows), :]

    def _to_chip(self, bufs, sems, base, a, j, k, part):
        _, _, c, _, chips, _ = _place()
        blk = self._rows(bufs[a], k, c, part)
        return _remote(blk, blk, sems[base].at[a, j], sems[base + 1].at[a, j], (*chips[j], c))

    def _to_sibling(self, bufs, sems, a, i, k, h, part):
        x, y, c, _, _, _ = _place()
        blk = self._rows(bufs[a], k, h, part)
        return _remote(blk, blk, sems[4].at[a, i], sems[5].at[a, i], (x, y, 1 - c))

    def start(self, srcs, bufs, news, sems):
        me = _place()[3]
        for a in range(len(self.bufs)):
            for j in range(2):
                self._to_chip(bufs, sems, 0, a, j, me, None).start()

    def finish(self, srcs, bufs, news, sems):
        _, _, c, me, _, cid = _place()
        arrays = range(len(self.bufs))
        for a in arrays:
            for j in (1, 0):
                self._to_chip(bufs, sems, 0, a, j, cid[j], None).wait_recv()
                self._to_chip(bufs, sems, 2, a, 1 - j, cid[j], 1 - j).start()
                self._to_sibling(bufs, sems, a, j, cid[j], c, None).start()
        for a in arrays:
            for part in range(2):
                self._to_chip(bufs, sems, 2, a, part, cid[2], part).wait_recv()
                self._to_sibling(bufs, sems, a, 2 + part, cid[2], c, part).start()
        for a in arrays:
            for j in range(2):
                self._to_sibling(bufs, sems, a, j, cid[j], 1 - c, None).wait_recv()
                self._to_sibling(bufs, sems, a, 2 + j, cid[2], 1 - c, j).wait_recv()
        for a in arrays:
            for j in range(2):
                self._to_chip(bufs, sems, 0, a, j, me, None).wait_send()
                self._to_chip(bufs, sems, 2, a, 1 - j, cid[j], 1 - j).wait_send()
                self._to_sibling(bufs, sems, a, j, cid[j], c, None).wait_send()
                self._to_sibling(bufs, sems, a, 2 + j, cid[2], c, j).wait_send()


class _SiblingJob:
    def __init__(self, srcs, rows_half):
        self.srcs, self.bufs, self.rows_half = list(srcs), [], rows_half
        self.news = [SDS((s.shape[0], s.shape[1] // 2, s.shape[2]) if rows_half else s.shape, s.dtype) for s in srcs]
        self.scratch = [pltpu.SemaphoreType.DMA((len(self.srcs),))] * 2

    def _copy(self, srcs, news, sems, a):
        x, y, c, _, _, _ = _place()
        src = srcs[a]
        if self.rows_half:
            rows = src.shape[1] // 2
            src = src.at[:, pl.ds(pl.multiple_of((1 - c) * rows, 8), rows), :]
        return _remote(src, news[a], sems[0].at[a], sems[1].at[a], (x, y, 1 - c))

    def start(self, srcs, bufs, news, sems):
        for a in range(len(self.srcs)):
            self._copy(srcs, news, sems, a).start()

    def finish(self, srcs, bufs, news, sems):
        for a in range(len(self.srcs)):
            self._copy(srcs, news, sems, a).wait()


class _ScatterJob:
    def __init__(self, parts):
        self.srcs, self.bufs = list(parts), []
        self.news = [SDS((3,) + p.shape[1:], p.dtype) for p in parts]
        self.scratch = [pltpu.SemaphoreType.DMA((len(self.srcs), 3))] * 2

    def _copy(self, srcs, news, sems, a, j):
        _, _, c, _, chips, cid = _place()
        return _remote(srcs[a].at[cid[j]], news[a].at[j], sems[0].at[a, j], sems[1].at[a, j], (*chips[j], c))

    def start(self, srcs, bufs, news, sems):
        for a in range(len(self.srcs)):
            for j in range(3):
                self._copy(srcs, news, sems, a, j).start()

    def finish(self, srcs, bufs, news, sems):
        for a in range(len(self.srcs)):
            for j in range(3):
                self._copy(srcs, news, sems, a, j).wait()


def _comm_only(name, job):
    return _pcall(None, name, (), [], [], [], [], jobs=[job])[1][0]


class _ChipGatherJob(_GatherJob):
    def __init__(self, bufs):
        super().__init__(bufs, None)

    def _half(self, ref, k, h):
        return ref.at[k, h]


def _cast_own(place, ws):
    n = len(ws)

    def body(p_ref, *refs):
        for i_ref, o_ref in zip(refs[:n], refs[n:]):
            o_ref[...] = i_ref[...].astype(BF16)

    return pl.pallas_call(
        body, name="cast_own_shards",
        grid_spec=pltpu.PrefetchScalarGridSpec(
            num_scalar_prefetch=1, grid=(DEPTH,),
            in_specs=[pl.BlockSpec((None,) + a.shape[1:], lambda l, p: (l, 0, 0)) for a in ws],
            out_specs=[pl.BlockSpec((None, None) + a.shape[1:], lambda l, p: (l, p[1], 0, 0)) for a in ws]),
        out_shape=[SDS((DEPTH, 4) + a.shape[1:], BF16) for a in ws],
        compiler_params=_params("arbitrary"),
    )(place, *ws)


def _half_tiles(a_):
    rows = a_ // 2
    ta = next(t for t in (256, 128, 64, 32, 16, 8) if rows % t == 0)
    return rows, ta, rows // ta


def _pair_sums_bf16(name, place, owns, recvs):
    n = len(owns)

    def body(p_ref, *refs):
        for own_ref, recv_ref, out_ref in zip(refs[:n], refs[n:2 * n], refs[2 * n:]):
            out_ref[...] = (own_ref[...] + recv_ref[...]).astype(BF16)

    def own_half(a):
        return pl.BlockSpec((None, a.shape[1] // 2, a.shape[2]), lambda s, p: (s, p[0], 0))

    def block(a):
        return pl.BlockSpec((None,) + a.shape[1:], lambda s, p: (s, 0, 0))

    return pl.pallas_call(
        body, name=name,
        grid_spec=pltpu.PrefetchScalarGridSpec(
            num_scalar_prefetch=1, grid=(4,),
            in_specs=[own_half(a) for a in owns] + [block(r) for r in recvs],
            out_specs=[block(r) for r in recvs]),
        out_shape=[SDS(r.shape, BF16) for r in recvs],
        compiler_params=_params("arbitrary"),
    )(place, *owns, *recvs)


def _shard_sums(name, place, owns, recvs, rbufs):
    n = len(owns)

    def body(p_ref, *refs):
        for own_ref, recv_ref, r_ref, out_ref in zip(refs[:n], refs[n:2 * n], refs[2 * n:3 * n], refs[3 * n:]):
            acc = own_ref[...] + recv_ref[...]
            for j in range(3):
                acc = acc + r_ref[j].astype(F32)
            out_ref[...] = acc

    steps = 2

    def own_half(a):
        return pl.BlockSpec((None, a.shape[1] // 2 // steps, a.shape[2]), lambda i, p: (p[1], steps * p[0] + i, 0))

    def recv_block(a):
        return pl.BlockSpec((None, a.shape[1] // steps, a.shape[2]), lambda i, p: (p[1], i, 0))

    return pl.pallas_call(
        body, name=name,
        grid_spec=pltpu.PrefetchScalarGridSpec(
            num_scalar_prefetch=1, grid=(steps,),
            in_specs=([own_half(a) for a in owns] + [recv_block(r) for r in recvs]
                      + [pl.BlockSpec((3, rb.shape[1] // steps, rb.shape[2]), lambda i, p: (0, i, 0)) for rb in rbufs]),
            out_specs=[pl.BlockSpec((r.shape[1] // steps, r.shape[2]), lambda i, p: (i, 0)) for r in recvs]),
        out_shape=[SDS(r.shape[1:], F32) for r in recvs],
        compiler_params=_params("arbitrary"),
    )(place, *owns, *recvs, *rbufs)


def _small_pair_sum(place, mine, recv, dtypes):
    n = len(mine)

    def body(p_ref, *refs):
        for m_ref, r_ref, o_ref in zip(refs[:n], refs[n:2 * n], refs[2 * n:]):
            o_ref[...] = (m_ref[...] + r_ref[...]).astype(o_ref.dtype)

    def whole(a):
        zeros = (0,) * a.ndim
        return pl.BlockSpec(a.shape, lambda i, p: zeros)

    def mine_blk(a):
        zeros = (0,) * a.ndim
        return pl.BlockSpec((None,) + a.shape, lambda i, p: (p[1],) + zeros)

    return pl.pallas_call(
        body, name="small_pair_sum",
        grid_spec=pltpu.PrefetchScalarGridSpec(
            num_scalar_prefetch=1, grid=(1,),
            in_specs=[whole(a) for a in mine] + [whole(a) for a in recv],
            out_specs=[mine_blk(a) for a in mine]),
        out_shape=[SDS((4,) + a.shape, dt) for a, dt in zip(mine, dtypes)],
        compiler_params=_params("arbitrary"),
    )(place, *mine, *recv)


def _adam_math(w, g, m, v):
    m = B1 * m + (1.0 - B1) * g
    v = B2 * v + (1.0 - B2) * (g * g)
    m_hat = m / (1.0 - B1 ** STEP)
    v_hat = v / (1.0 - B2 ** STEP)
    delta = -LR * (m_hat / (jnp.sqrt(v_hat) + EPS_A) + WD * w)
    return delta, m, v


def _adamw_big(name, l, row0, w, m, v, mine, other, prev, jobs=()):
    _, _, b_ = w.shape
    _, ta, nh = _half_tiles(2 * mine.shape[0])
    prev = list(prev or [])

    def body(w_ref, m_ref, v_ref, mine_ref, other_ref, *rest):
        g_ref, d_ref, mo_ref, vo_ref = rest[len(prev):]
        g = jnp.where(pl.program_id(0) == lax.axis_index("c"), mine_ref[...], other_ref[...])
        g_ref[...] = g
        d_ref[...], mo_ref[...], vo_ref[...] = _adam_math(w_ref[...], g, m_ref[...], v_ref[...])

    slab = pl.BlockSpec((None, ta, b_), lambda h, i: (l, row0 // ta + h * nh + i, 0))
    half = pl.BlockSpec((ta, b_), lambda h, i: (i, 0))
    outs, res = _pcall(
        body, name, (2, nh), [slab, slab, slab, half, half] + [_ANY] * len(prev), [slab] * 4, [SDS(w.shape, F32)] * 4,
        (w, m, v, mine, other, *prev), aliases={5 + k: k for k in range(len(prev))}, jobs=jobs)
    return outs, res


def _adamw_small(name, ws, parts, ms, vs, loss_parts=None):
    k = len(ws)
    extra = [] if loss_parts is None else [loss_parts]

    def chip_sum(p_ref):
        p = [p_ref[k].astype(F32) for k in range(4)]
        return ((p[0] + p[1]) + p[2]) + p[3]

    def body(*refs):
        w_refs, p_refs, m_refs, v_refs = refs[:k], refs[k:2 * k], refs[2 * k:3 * k], refs[3 * k:4 * k]
        outs = refs[4 * k + len(extra):]
        if extra:
            outs[0][...] = chip_sum(refs[4 * k])
            outs = outs[1:]
        for a in range(k):
            g = chip_sum(p_refs[a])
            outs[a][...] = g
            outs[k + a][...], outs[2 * k + a][...], outs[3 * k + a][...] = _adam_math(
                w_refs[a][...], g, m_refs[a][...], v_refs[a][...])

    like = [SDS(a.shape, F32) for a in ws]
    return pl.pallas_call(
        body, name=name,
        out_shape=([SDS(loss_parts.shape[1:], F32)] if extra else []) + like * 4,
        compiler_params=pltpu.CompilerParams(vmem_limit_bytes=VMEM_LIMIT),
    )(*ws, *parts, *ms, *vs, *extra)


class _Reducer:
    def __init__(self, place, w, m, v):
        self.place, self.w, self.m, self.v = place, w, m, v
        self.active, self.riding = [], []
        self.big = {n: None for n in BIG}

    def add(self, l, tag, names, own, row0s=None):
        self.active.append(dict(l=l, key=f"{tag}_l{l}", names=names, own=list(own), row0s=row0s or [0] * len(names),
                                stage=0))

    def jobs(self):
        self.riding = list(self.active)
        return [(_SiblingJob(g["own"], True), _ScatterJob(g.get("parts", [])), _SiblingJob(g.get("shard", []), False))
                [g["stage"]] for g in self.riding]

    def land(self, res):
        for g, (_, news) in zip(self.riding, res):
            if g["stage"] == 0:
                g["recv"] = news
                g["parts"] = _pair_sums_bf16(f"pair_sums_{g['key']}", self.place, g["own"], news)
            elif g["stage"] == 1:
                g["shard"] = _shard_sums(f"shard_sums_{g['key']}", self.place, g["own"], g["recv"], news)
            else:
                for n, mine, other, row0 in zip(g["names"], g["shard"], news, g["row0s"]):
                    self.big[n] = _adamw_big(f"adamw_{n}_{row0}_{g['key']}", g["l"], row0, self.w[n], self.m[n], self.v[n],
                                             mine, other, self.big[n])[0]
                self.active.remove(g)
            g["stage"] += 1
        self.riding = []


WEIGHTS = ("norm_g", "w_in", "b_in", "ssm_log_dt", "ssm_lam_re", "ssm_lam_im", "ssm_b_re", "ssm_b_im", "ssm_c_re",
           "ssm_c_im", "ssm_d", "ssm_w_glu", "ssm_b_glu", "pool_w", "pool_scale", "w_branch_a", "w_branch_b", "w_out",
           "final_norm_g")
REPLICATED = SMALL + ("final_norm_g",)
DENSE = {"ssm_b_re": (DEPTH, G, P * C), "ssm_b_im": (DEPTH, G, P * C), "final_norm_g": (2, D // 2)}


def kernel(x, norm_g, w_in, b_in, ssm_log_dt, ssm_lam_re, ssm_lam_im, ssm_b_re, ssm_b_im, ssm_c_re, ssm_c_im, ssm_d, ssm_w_glu, ssm_b_glu, pool_w, pool_scale, w_branch_a, w_branch_b, w_out, final_norm_g, loss_target, m_norm_g, m_w_in, m_b_in, m_ssm_log_dt, m_ssm_lam_re, m_ssm_lam_im, m_ssm_b_re, m_ssm_b_im, m_ssm_c_re, m_ssm_c_im, m_ssm_d, m_ssm_w_glu, m_ssm_b_glu, m_pool_w, m_pool_scale, m_w_branch_a, m_w_branch_b, m_w_out, m_final_norm_g, v_norm_g, v_w_in, v_b_in, v_ssm_log_dt, v_ssm_lam_re, v_ssm_lam_im, v_ssm_b_re, v_ssm_b_im, v_ssm_c_re, v_ssm_c_im, v_ssm_d, v_ssm_w_glu, v_ssm_b_glu, v_pool_w, v_pool_scale, v_w_branch_a, v_w_branch_b, v_w_out, v_final_norm_g):
    w = dict(zip(WEIGHTS, (norm_g, w_in, b_in, ssm_log_dt, ssm_lam_re, ssm_lam_im, ssm_b_re, ssm_b_im, ssm_c_re,
                           ssm_c_im, ssm_d, ssm_w_glu, ssm_b_glu, pool_w, pool_scale, w_branch_a, w_branch_b, w_out,
                           final_norm_g)))
    m = dict(zip(WEIGHTS, (m_norm_g, m_w_in, m_b_in, m_ssm_log_dt, m_ssm_lam_re, m_ssm_lam_im, m_ssm_b_re, m_ssm_b_im,
                           m_ssm_c_re, m_ssm_c_im, m_ssm_d, m_ssm_w_glu, m_ssm_b_glu, m_pool_w, m_pool_scale,
                           m_w_branch_a, m_w_branch_b, m_w_out, m_final_norm_g)))
    v = dict(zip(WEIGHTS, (v_norm_g, v_w_in, v_b_in, v_ssm_log_dt, v_ssm_lam_re, v_ssm_lam_im, v_ssm_b_re, v_ssm_b_im,
                           v_ssm_c_re, v_ssm_c_im, v_ssm_d, v_ssm_w_glu, v_ssm_b_glu, v_pool_w, v_pool_scale,
                           v_w_branch_a, v_w_branch_b, v_w_out, v_final_norm_g)))
    place = jnp.stack([lax.axis_index("c"), 2 * lax.axis_index("x") + lax.axis_index("y")]).astype(jnp.int32)

    total_loss, dx, results = _step(x[0], loss_target[0], w, m, v, place)
    return (total_loss, dx[None], *[results[n][q] for q in range(4) for n in WEIGHTS])
```

```python
import functools

import jax
import jax.numpy as jnp
from jax import lax
from jax.experimental import pallas as pl
from jax.experimental.pallas import tpu as pltpu

F32, BF16 = jnp.float32, jnp.bfloat16
SDS = jax.ShapeDtypeStruct
MESH = pl.DeviceIdType.MESH

DEPTH = 2
L = 2048
D = 1024
NIN = 4096
W = 512
G, P, C = 32, 64, 16
GP = G * P
WINS = (2, 4, 8, 16)
TM = 256
NT = L // TM
TMM = 512
TK = 2048
LAST_ROWS = 512
EPS = 1e-6
VMEM_LIMIT = 56 * 2**20

LR, B1, B2, EPS_A, WD, STEP = 0.001, 0.9, 0.999, 1e-08, 0.01, 10


def _params(*sem):
    return pltpu.CompilerParams(dimension_semantics=sem, vmem_limit_bytes=VMEM_LIMIT)


_ANY = pl.BlockSpec(memory_space=pl.ANY)


def _full(shape):
    zeros = (0,) * len(shape)
    return pl.BlockSpec(shape, lambda *_: zeros)


def _layer(l, shape):
    zeros = (0,) * len(shape)
    return pl.BlockSpec((None,) + shape, lambda *_: (l,) + zeros)


def _rows(width, col=0, reverse=False, tm=TM):
    if reverse:
        return pl.BlockSpec((tm, width), lambda i: (L // tm - 1 - i, col))
    return pl.BlockSpec((tm, width), lambda i: (i, col))


def _rows_mm(width, col=0):
    return _rows(width, col, False, TMM)


def _pcall(body, name, grid, in_specs, out_specs, out_shape, args, scratch=(), aliases=None, jobs=()):
    in_specs, out_specs, out_shape, args, scratch = list(in_specs), list(out_specs), list(out_shape), list(args), list(scratch)
    aliases = dict(aliases or {})
    jobs = [j for j in jobs if j is not None]
    n_in, n_out, n_scr = len(in_specs), len(out_specs), len(scratch)
    srcs = [s for j in jobs for s in j.srcs]
    bufs = [b for j in jobs for b in j.bufs]
    news = [s for j in jobs for s in j.news]
    aliases.update({n_in + len(srcs) + k: n_out + k for k in range(len(bufs))})

    def hosted(*refs):
        cuts = [n_in, len(srcs), len(bufs), n_out, len(bufs), len(news), n_scr]
        parts, p = [], 0
        for n in cuts:
            parts.append(refs[p:p + n])
            p += n
        ins, src_r, _, outs, buf_r, new_r, scr = parts
        sem_r = refs[p:]
        views, ps, pb, pn, pm = [], 0, 0, 0, 0
        for j in jobs:
            views.append((src_r[ps:ps + len(j.srcs)], buf_r[pb:pb + len(j.bufs)], new_r[pn:pn + len(j.news)],
                          sem_r[pm:pm + len(j.scratch)]))
            ps, pb, pn, pm = ps + len(j.srcs), pb + len(j.bufs), pn + len(j.news), pm + len(j.scratch)

        def run(phase):
            for j, v in zip(jobs, views):
                getattr(j, phase)(*v)

        def at_step(step):
            return functools.reduce(jnp.logical_and, [pl.program_id(d) == step(d) for d in range(len(grid))])

        if not grid:
            run("start")
            run("finish")
            return
        pl.when(at_step(lambda d: 0))(lambda: run("start"))
        body(*ins, *outs, *scr)
        pl.when(at_step(lambda d: grid[d] - 1))(lambda: run("finish"))

    outs = pl.pallas_call(
        hosted if jobs else body, name=name, **({"grid": grid} if grid else {}),
        in_specs=in_specs + [_ANY] * (len(srcs) + len(bufs)), out_specs=out_specs + [_ANY] * (len(bufs) + len(news)),
        out_shape=out_shape + [SDS(b.shape, b.dtype) for b in bufs] + news,
        input_output_aliases=aliases, scratch_shapes=scratch + [s for j in jobs for s in j.scratch],
        compiler_params=_params(*(("arbitrary",) * len(grid))))(*args, *srcs, *bufs)
    res, pb, pn = [], n_out, n_out + len(bufs)
    for j in jobs:
        res.append((list(outs[pb:pb + len(j.bufs)]), list(outs[pn:pn + len(j.news)])))
        pb, pn = pb + len(j.bufs), pn + len(j.news)
    return list(outs[:n_out]), res


def _fused(name, grid, parts, shared=None, jobs=()):
    def body(*refs):
        pos = [0]

        def take(n):
            pos[0] += n
            return refs[pos[0] - n:pos[0]]

        ins = [take(len(p["in_specs"])) for p in parts]
        if shared:
            take(1)
            block = take(1)[0]
        outs = [take(len(p["out_specs"])) for p in parts]
        scr = [take(len(p["scratch"])) for p in parts]
        col = 0
        for p, i, o, s in zip(parts, ins, outs, scr):
            view = []
            if shared:
                view = [block.at[:, pl.ds(col, p["width"])]]
                col += p["width"]
            p["body"](*i, *view, *o, *s)

    in_specs = [s for p in parts for s in p["in_specs"]] + ([_ANY] if shared else [])
    out_specs = ([shared[1]] if shared else []) + [s for p in parts for s in p["out_specs"]]
    out_shape = ([SDS(shared[0].shape, shared[0].dtype)] if shared else []) + [s for p in parts for s in p["out_shape"]]
    args = [a for p in parts for a in p["args"]] + ([shared[0]] if shared else [])
    return _pcall(body, name, grid, in_specs, out_specs, out_shape, args, [s for p in parts for s in p["scratch"]],
                  {len(in_specs) - 1: 0} if shared else None, jobs)


def _dot(a, b):
    return jnp.dot(a, b, preferred_element_type=F32)


def _dot_nt(a, b):
    return lax.dot_general(a, b, (((1,), (1,)), ((), ())), preferred_element_type=F32)


def _dot_tn(a, b):
    return lax.dot_general(a, b, (((0,), (0,)), ((), ())), preferred_element_type=F32)


_K0 = 0.7978845608028654
_K1 = 0.044715


def _gelu(x):
    return 0.5 * x * (1.0 + jnp.tanh(_K0 * (x + _K1 * (x * x * x))))


def _gelu_grad(x):
    t = jnp.tanh(_K0 * (x + _K1 * (x * x * x)))
    return 0.5 * (1.0 + t) + 0.5 * x * (1.0 - t * t) * (_K0 * (1.0 + 3.0 * _K1 * x * x))


def _sigmoid(x):
    return jax.nn.sigmoid(x)


def _norm_inproj(l, x, g, w, b, jobs=()):
    def body(x_ref, g_ref, w_ref, b_ref, h_ref, proj_ref):
        xv = x_ref[...]
        r = lax.rsqrt(jnp.mean(xv * xv, axis=-1, keepdims=True) + EPS)
        hb = ((xv * r) * g_ref[...]).astype(BF16)
        h_ref[...] = hb
        for j in range(4):
            cs = slice(j * 1024, (j + 1) * 1024)
            proj_ref[:, cs] = _dot(hb, w_ref[j]) + b_ref[:, cs]

    return _pcall(
        body, f"norm_inproj_l{l}", (L // TMM,),
        [_rows_mm(D), _layer(l, (1, D)), _layer(l, (4, D, 1024)), _layer(l, (1, NIN))],
        [_rows_mm(D), _rows_mm(NIN)],
        [SDS((L, D), BF16), SDS((L, NIN), F32)],
        (x, g, w, b), jobs=jobs)


def _scan_tile(re_ref, im_ref, st_re, st_im, cr_re, cr_im, carry, reverse, each_chunk=None):
    def chunk(ci, carry):
        c = (TM // 8 - 1 - ci) if reverse else ci
        rows = pl.ds(pl.multiple_of(c * 8, 8), 8)
        new = []
        for lb in range(GP // 512):
            cols = slice(lb * 512, (lb + 1) * 512)
            vr = re_ref[rows, cols]
            vi = im_ref[rows, cols]
            for s, d in enumerate((1, 2, 4)):
                ar = st_re[8 * s:8 * s + 8, cols]
                ai = st_im[8 * s:8 * s + 8, cols]
                sr = pltpu.roll(vr, 8 - d if reverse else d, 0)
                si = pltpu.roll(vi, 8 - d if reverse else d, 0)
                vr, vi = vr + ar * sr - ai * si, vi + ar * si + ai * sr
            cr, ci_ = carry[2 * lb], carry[2 * lb + 1]
            pr = cr_re[:, cols]
            pi = cr_im[:, cols]
            vr, vi = vr + pr * cr - pi * ci_, vi + pr * ci_ + pi * cr
            re_ref[rows, cols] = vr
            im_ref[rows, cols] = vi
            if each_chunk is not None:
                each_chunk(c, cols, vr, vi)
            if reverse:
                new += [vr[0:1], vi[0:1]]
            else:
                new += [vr[7:8], vi[7:8]]
        return tuple(new)

    return lax.fori_loop(0, TM // 8, chunk, carry)


def _load_carry(car_ref):
    return tuple(car_ref[r:r + 1, lb * 512:(lb + 1) * 512] for lb in range(GP // 512) for r in (0, 1))


def _store_carry(car_ref, carry):
    for lb in range(GP // 512):
        car_ref[0:1, lb * 512:(lb + 1) * 512] = carry[2 * lb]
        car_ref[1:2, lb * 512:(lb + 1) * 512] = carry[2 * lb + 1]


def _s5_fwd(l, proj, bexp, cre, cimn, powers, dsk, wglu, bglu):
    def body(ua_ref, za_ref, bexp_ref, cre_ref, cimn_ref, st_re_ref, st_im_ref, cr_re_ref, cr_im_ref,
             d_ref, wg_ref, bg_ref, sre_ref, sim_ref, y1_ref, q_ref, ya_ref, car_ref):
        @pl.when(pl.program_id(0) == 0)
        def _():
            car_ref[...] = jnp.zeros_like(car_ref)

        u = ua_ref[...]
        ub = u.astype(BF16)
        for k in range(4):
            bu = _dot(ub[:, 128 * k:128 * (k + 1)], bexp_ref[k])
            sre_ref[:, 512 * k:512 * (k + 1)] = bu[:, :512]
            sim_ref[:, 512 * k:512 * (k + 1)] = bu[:, 512:]
        carry = _scan_tile(sre_ref, sim_ref, st_re_ref, st_im_ref, cr_re_ref, cr_im_ref, _load_carry(car_ref), False)
        _store_carry(car_ref, carry)
        for k in range(4):
            blk = slice(512 * k, 512 * (k + 1))
            ks = slice(128 * k, 128 * (k + 1))
            y0 = _dot(sre_ref[:, blk].astype(BF16), cre_ref[k]) + _dot(sim_ref[:, blk].astype(BF16), cimn_ref[k])
            y1_ref[:, ks] = y0 + d_ref[:, ks] * u[:, ks]
        y2 = _gelu(y1_ref[...])
        q = _dot(y2.astype(BF16), wg_ref[...]) + bg_ref[...]
        q_ref[...] = q
        za = za_ref[...]
        ya_ref[...] = ((y2 * _sigmoid(q)) * (za * _sigmoid(za))).astype(BF16)

    return dict(
        body=body,
        in_specs=[_rows(W, 0), _rows(W, 1), _layer(l, (4, 128, 1024)), _layer(l, (4, 512, 128)),
                  _layer(l, (4, 512, 128)), _layer(l, (24, GP)), _layer(l, (24, GP)), _layer(l, (8, GP)),
                  _layer(l, (8, GP)), _layer(l, (1, W)), _layer(l, (W, W)), _layer(l, (1, W))],
        out_specs=[_rows(GP), _rows(GP), _rows(W), _rows(W), _rows(W)],
        out_shape=[SDS((L, GP), F32), SDS((L, GP), F32), SDS((L, W), F32), SDS((L, W), F32), SDS((L, W), BF16)],
        args=(proj, proj, bexp, cre, cimn, *powers, dsk, wglu, bglu),
        scratch=[pltpu.VMEM((8, GP), F32)])


def _pool_fwd(l, proj, pw, scale):
    def body(ub_ref, zb_ref, pw_ref, sc_ref, pooled_ref, mixed_ref, yb_ref, buf):
        i = pl.program_id(0)

        @pl.when(i == 0)
        def _():
            buf[0:16, :] = jnp.zeros((16, W), F32)

        u = ub_ref[...]
        buf[16:16 + TM, :] = u
        t = i * TM + lax.broadcasted_iota(jnp.int32, (TM, 128), 0)
        for gi, win in enumerate(WINS):
            cs = slice(128 * gi, 128 * (gi + 1))
            acc = u[:, cs]
            for k in range(1, win):
                acc = acc + buf[16 - k:16 - k + TM, cs]
            cnt = jnp.minimum(t + 1, win).astype(F32)
            pb = (acc / cnt - u[:, cs]).astype(BF16)
            pooled_ref[:, cs] = pb
            mixed_ref[:, cs] = _dot(pb, pw_ref[gi])
        zb = zb_ref[...]
        yb_ref[...] = ((mixed_ref[...] * sc_ref[...]) * (zb * _sigmoid(zb))).astype(BF16)
        buf[0:16, :] = buf[TM:TM + 16, :]

    return dict(
        body=body,
        in_specs=[_rows(W, 2), _rows(W, 3), _layer(l, (4, 128, 128)), _layer(l, (1, W))],
        out_specs=[_rows(W), _rows(W), _rows(W)],
        out_shape=[SDS((L, W), BF16), SDS((L, W), F32), SDS((L, W), BF16)],
        args=(proj, proj, pw, scale),
        scratch=[pltpu.VMEM((TM + 16, W), F32)])


def _merge_out(l, ya, yb, proj, x, wa, wb, wo, head=None):
    def body(ya_ref, yb_ref, ga_ref, gb_ref, x_ref, wa_ref, wb_ref, wo_ref, *rest):
        pa_ref, pb_ref, mg_ref = rest[2:5] if head else rest[0:3]
        ya = ya_ref[...]
        yb = yb_ref[...]
        for j in range(4):
            cs = slice(256 * j, 256 * (j + 1))
            pa_ref[:, cs] = _dot(ya, wa_ref[j])
            pb_ref[:, cs] = _dot(yb, wb_ref[j])
        merged = _sigmoid(ga_ref[...]) * pa_ref[...] + _sigmoid(gb_ref[...]) * pb_ref[...]
        mb = merged.astype(BF16)
        mg_ref[...] = mb
        x_next = x_ref[...] + _dot(mb, wo_ref[...])
        if head:
            _loss_tile(x_next, rest[0], rest[1], *rest[5:])
        else:
            rest[3][...] = x_next

    rows, out_rows = _rows_mm(D), SDS((L, D), F32)
    return pl.pallas_call(
        body, name=f"merge_out_l{l}", grid=(L // TMM,),
        in_specs=[_rows_mm(W), _rows_mm(W), _rows_mm(D, 2), _rows_mm(D, 3), rows,
                  _layer(l, (4, W, 256)), _layer(l, (4, W, 256)), _layer(l, (D, D))] + ([_full((1, D)), rows] if head else []),
        out_specs=[rows, rows, rows] + ([_full((2, 128)), rows, _full((1, D))] if head else [rows]),
        out_shape=[out_rows, out_rows, SDS((L, D), BF16)]
        + ([SDS((2, 128), F32), out_rows, SDS((1, D), F32)] if head else [out_rows]),
        compiler_params=_params("arbitrary"),
    )(ya, yb, proj, proj, x, wa, wb, wo, *(head or ()))


def _loss_tile(xv, g_ref, t_ref, loss_ref, dx_ref, dg_ref):
    @pl.when(pl.program_id(0) == 0)
    def _():
        loss_ref[...] = jnp.zeros_like(loss_ref)
        dg_ref[...] = jnp.zeros_like(dg_ref)

    g = g_ref[...]
    r = lax.rsqrt(jnp.mean(xv * xv, axis=-1, keepdims=True) + EPS)
    xn = xv * r
    err = xn * g - t_ref[...]
    part = jnp.sum(jnp.mean(err * err, axis=-1, keepdims=True), axis=0, keepdims=True)
    loss_ref[...] += 0.5 * part
    dy = err * (1.0 / D)
    dg_ref[...] += jnp.sum(dy * xn, axis=0, keepdims=True)
    dxn = dy * g
    dx_ref[...] = r * (dxn - xn * jnp.mean(dxn * xn, axis=-1, keepdims=True))


def _merge_out_bwd(l, dxn, mg, proj, pa, pb, ya, yb, wo, wa, wb, jobs=()):
    def body(dx_ref, mg_ref, ga_ref, gb_ref, pa_ref, pb_ref, ya_ref, yb_ref, wo_ref, wa_ref, wb_ref,
             dg_ref, dya_ref, dyb_ref, dwo_ref, dwa_ref, dwb_ref, dbias_ref):
        @pl.when(pl.program_id(0) == 0)
        def _():
            for ref in (dwo_ref, dwa_ref, dwb_ref, dbias_ref):
                ref[...] = jnp.zeros_like(ref)

        dxb = dx_ref[...].astype(BF16)
        dm = _dot_nt(dxb, wo_ref[...])
        sa = _sigmoid(ga_ref[...])
        sb = _sigmoid(gb_ref[...])
        dga = dm * pa_ref[...] * (sa * (1.0 - sa))
        dgb = dm * pb_ref[...] * (sb * (1.0 - sb))
        dg_ref[:, :D] = dga.astype(BF16)
        dg_ref[:, D:] = dgb.astype(BF16)
        dbias_ref[:, :D] += jnp.sum(dga, axis=0, keepdims=True)
        dbias_ref[:, D:] += jnp.sum(dgb, axis=0, keepdims=True)
        dpa = (dm * sa).astype(BF16)
        dpb = (dm * sb).astype(BF16)
        ya = ya_ref[...]
        yb = yb_ref[...]
        dya = jnp.zeros((TM, W), F32)
        dyb = jnp.zeros((TM, W), F32)
        for j in range(4):
            cs = slice(256 * j, 256 * (j + 1))
            dya = dya + _dot_nt(dpa[:, cs], wa_ref[j])
            dyb = dyb + _dot_nt(dpb[:, cs], wb_ref[j])
            dwa_ref[j] += _dot_tn(ya, dpa[:, cs])
            dwb_ref[j] += _dot_tn(yb, dpb[:, cs])
        dya_ref[...] = dya
        dyb_ref[...] = dyb
        dwo_ref[...] += _dot_tn(mg_ref[...], dxb)

    return _pcall(
        body, f"merge_out_bwd_l{l}", (NT,),
        [_rows(D), _rows(D), _rows(D, 2), _rows(D, 3), _rows(D), _rows(D), _rows(W), _rows(W),
         _layer(l, (D, D)), _layer(l, (4, W, 256)), _layer(l, (4, W, 256))],
        [_rows(2 * D, 1), _rows(W), _rows(W), _full((D, D)), _full((4, W, 256)), _full((4, W, 256)), _full((1, 2 * D))],
        [SDS((L, NIN), BF16), SDS((L, W), F32), SDS((L, W), F32),
         SDS((D, D), F32), SDS((4, W, 256), F32), SDS((4, W, 256), F32), SDS((1, 2 * D), F32)],
        (dxn, mg, proj, proj, pa, pb, ya, yb, wo, wa, wb), jobs=jobs)


def _pool_bwd(l, dyb, proj, mixed, pooled, pw, scale):
    def body(dyb_ref, zb_ref, mixed_ref, pooled_ref, pw_ref, sc_ref, db_ref, dpw_ref, dsc_ref, dbias_ref, buf):
        i = pl.program_id(0)
        tile = NT - 1 - i

        @pl.when(i == 0)
        def _():
            dpw_ref[...] = jnp.zeros_like(dpw_ref)
            dsc_ref[...] = jnp.zeros_like(dsc_ref)
            dbias_ref[...] = jnp.zeros_like(dbias_ref)
            buf[TM:TM + 16, :] = jnp.zeros((16, W), F32)

        dyb = dyb_ref[...]
        zb = zb_ref[...]
        mixed = mixed_ref[...]
        sc = sc_ref[...]
        sg = _sigmoid(zb)
        dyb0 = dyb * (zb * sg)
        dzb = dyb * (mixed * sc) * (sg * (1.0 + zb * (1.0 - sg)))
        db_ref[:, W:] = dzb.astype(BF16)
        dbias_ref[:, W:] += jnp.sum(dzb, axis=0, keepdims=True)
        dsc_ref[...] += jnp.sum(dyb0 * mixed, axis=0, keepdims=True)
        dmix = (dyb0 * sc).astype(BF16)
        t = tile * TM + lax.broadcasted_iota(jnp.int32, (TM, 128), 0)
        for gi, win in enumerate(WINS):
            cs = slice(128 * gi, 128 * (gi + 1))
            dpw_ref[gi] += _dot_tn(pooled_ref[:, cs], dmix[:, cs])
            dpool = _dot_nt(dmix[:, cs], pw_ref[gi])
            cnt = jnp.minimum(t + 1, win).astype(F32)
            e = dpool / cnt
            buf[0:TM, cs] = e
            acc = e - dpool
            for k in range(1, win):
                acc = acc + buf[k:k + TM, cs]
            db_ref[:, cs] = acc.astype(BF16)
            dbias_ref[:, cs] += jnp.sum(acc, axis=0, keepdims=True)
        buf[TM:TM + 16, :] = buf[0:16, :]

    return dict(
        body=body, width=2 * W,
        in_specs=[_rows(W, 0, True), _rows(W, 3, True), _rows(W, 0, True), _rows(W, 0, True),
                  _layer(l, (4, 128, 128)), _layer(l, (1, W))],
        out_specs=[_full((4, 128, 128)), _full((1, W)), _full((1, 2 * W))],
        out_shape=[SDS((4, 128, 128), F32), SDS((1, W), F32), SDS((1, 2 * W), F32)],
        args=(dyb, proj, mixed, pooled, pw, scale),
        scratch=[pltpu.VMEM((TM + 16, W), F32)])


def _s5_bwd(l, dya, proj, y1, q, sre, sim, cret, cimnt, bret, bimt, st_re, st_im, cr_re, cr_im, dsk, wglu):
    def halo(i):
        return (jnp.maximum((NT - 1 - i) * (TM // 8) - 1, 0), 0)

    def body(dya_ref, ua_ref, za_ref, y1_ref, q_ref, sre_ref, sim_ref, hre_ref, him_ref,
             cret_ref, cimnt_ref, bret_ref, bimt_ref, st_re_ref, st_im_ref, cr_re_ref, cr_im_ref, d_ref, wg_ref,
             da_ref, dwg_ref, dbg_ref, dd_ref, dcre_ref, dcimn_ref, dbre_ref, dbim_ref, dare_ref, daim_ref, dbias_ref,
             lre, lim, car_ref):
        i = pl.program_id(0)
        tile = NT - 1 - i

        @pl.when(i == 0)
        def _():
            for ref in (dwg_ref, dbg_ref, dd_ref, dcre_ref, dcimn_ref, dbre_ref, dbim_ref, dare_ref, daim_ref, dbias_ref,
                        car_ref):
                ref[...] = jnp.zeros_like(ref)

        u = ua_ref[...]
        za = za_ref[...]
        y1 = y1_ref[...]
        dya = dya_ref[...]
        y2 = _gelu(y1)
        sg = _sigmoid(q_ref[...])
        sgz = _sigmoid(za)
        dy3 = dya * (za * sgz)
        dza = dya * (y2 * sg) * (sgz * (1.0 + za * (1.0 - sgz)))
        da_ref[:, W:] = dza.astype(BF16)
        dbias_ref[:, W:] += jnp.sum(dza, axis=0, keepdims=True)
        dq = dy3 * y2 * (sg * (1.0 - sg))
        dqb = dq.astype(BF16)
        dy2 = dy3 * sg + _dot_nt(dqb, wg_ref[...])
        dwg_ref[...] += _dot_tn(y2.astype(BF16), dqb)
        dbg_ref[...] += jnp.sum(dq, axis=0, keepdims=True)
        dy1 = dy2 * _gelu_grad(y1)
        dd_ref[...] += jnp.sum(dy1 * u, axis=0, keepdims=True)
        dy1b = dy1.astype(BF16)
        ub = u.astype(BF16)
        for k in range(4):
            blk = slice(512 * k, 512 * (k + 1))
            ks = slice(128 * k, 128 * (k + 1))
            lre[:, blk] = _dot(dy1b[:, ks], cret_ref[k])
            lim[:, blk] = _dot(dy1b[:, ks], cimnt_ref[k])
            dcre_ref[k] += _dot_tn(sre_ref[:, blk].astype(BF16), dy1b[:, ks])
            dcimn_ref[k] += _dot_tn(sim_ref[:, blk].astype(BF16), dy1b[:, ks])
        rowid = lax.broadcasted_iota(jnp.int32, (8, 512), 0)
        gate = (tile > 0).astype(F32)

        def d_abar(c, cols, lr, li):
            rows = pl.ds(pl.multiple_of(c * 8, 8), 8)
            prows = pl.ds(pl.multiple_of(jnp.maximum(c - 1, 0) * 8, 8), 8)
            pr = jnp.where(c == 0, hre_ref[7:8, cols] * gate, sre_ref[prows, cols][7:8])
            pi = jnp.where(c == 0, him_ref[7:8, cols] * gate, sim_ref[prows, cols][7:8])
            sr = jnp.where(rowid == 0, pr, pltpu.roll(sre_ref[rows, cols], 1, 0))
            si = jnp.where(rowid == 0, pi, pltpu.roll(sim_ref[rows, cols], 1, 0))
            dare_ref[:, cols] += sr * lr + si * li
            daim_ref[:, cols] += sr * li - si * lr

        carry = _scan_tile(lre, lim, st_re_ref, st_im_ref, cr_re_ref, cr_im_ref, _load_carry(car_ref), True, d_abar)
        _store_carry(car_ref, carry)

        for k in range(4):
            blk = slice(512 * k, 512 * (k + 1))
            ks = slice(128 * k, 128 * (k + 1))
            lrb = lre[:, blk].astype(BF16)
            lib = lim[:, blk].astype(BF16)
            du = dy1[:, ks] * d_ref[:, ks] + _dot(lrb, bret_ref[k]) + _dot(lib, bimt_ref[k])
            da_ref[:, ks] = du.astype(BF16)
            dbias_ref[:, ks] += jnp.sum(du, axis=0, keepdims=True)
            dbre_ref[k] += _dot_tn(ub[:, ks], lrb)
            dbim_ref[k] += _dot_tn(ub[:, ks], lib)

    return dict(
        body=body, width=2 * W,
        in_specs=[_rows(W, 0, True), _rows(W, 0, True), _rows(W, 1, True), _rows(W, 0, True), _rows(W, 0, True),
                  _rows(GP, 0, True), _rows(GP, 0, True),
                  pl.BlockSpec((8, GP), halo), pl.BlockSpec((8, GP), halo),
                  _layer(l, (4, 128, 512)), _layer(l, (4, 128, 512)), _layer(l, (4, 512, 128)), _layer(l, (4, 512, 128)),
                  _layer(l, (24, GP)), _layer(l, (24, GP)), _layer(l, (8, GP)), _layer(l, (8, GP)), _layer(l, (1, W)),
                  _layer(l, (W, W))],
        out_specs=[_full((W, W)), _full((1, W)), _full((1, W)),
                   _full((4, 512, 128)), _full((4, 512, 128)), _full((4, 128, 512)), _full((4, 128, 512)),
                   _full((8, GP)), _full((8, GP)), _full((1, 2 * W))],
        out_shape=[SDS((W, W), F32), SDS((1, W), F32), SDS((1, W), F32),
                   SDS((4, 512, 128), F32), SDS((4, 512, 128), F32), SDS((4, 128, 512), F32), SDS((4, 128, 512), F32),
                   SDS((8, GP), F32), SDS((8, GP), F32), SDS((1, 2 * W), F32)],
        args=(dya, proj, proj, y1, q, sre, sim, sre, sim, cret, cimnt, bret, bimt, st_re, st_im, cr_re, cr_im, dsk,
              wglu),
        scratch=[pltpu.VMEM((TM, GP), F32), pltpu.VMEM((TM, GP), F32), pltpu.VMEM((8, GP), F32)])


def _inproj_dw(l, r, h, dproj, jobs=(), rows=D // 2):
    def body(h_ref, dp_ref, dw_ref):
        part = _dot_tn(h_ref[...], dp_ref[...])

        @pl.when(pl.program_id(1) == 0)
        def _():
            dw_ref[...] = part

        @pl.when(pl.program_id(1) > 0)
        def _():
            dw_ref[...] += part

    return _pcall(
        body, f"inproj_dw{r}_l{l}", (4, L // TK),
        [pl.BlockSpec((TK, rows), lambda j, i: (i, r)), pl.BlockSpec((TK, 1024), lambda j, i: (i, j))],
        [pl.BlockSpec((None, rows, 1024), lambda j, i: (j, 0, 0))],
        [SDS((4, rows, 1024), F32)],
        (h, dproj), jobs=jobs)


def _inproj_dx(l, dproj, w, x, g, dxn, jobs=()):
    def body(dp_ref, w_ref, x_ref, g_ref, dxn_ref, dx_ref, dg_ref):
        @pl.when(pl.program_id(0) == 0)
        def _():
            dg_ref[...] = jnp.zeros_like(dg_ref)

        dh = _dot_nt(dp_ref[:, 0:1024], w_ref[0])
        for j in range(1, 4):
            dh = dh + _dot_nt(dp_ref[:, j * 1024:(j + 1) * 1024], w_ref[j])
        xv = x_ref[...]
        r = lax.rsqrt(jnp.mean(xv * xv, axis=-1, keepdims=True) + EPS)
        xn = xv * r
        dg_ref[...] += jnp.sum(dh * xn, axis=0, keepdims=True)
        dn = dh * g_ref[...]
        dx_ref[...] = dxn_ref[...] + r * (dn - xn * jnp.mean(dn * xn, axis=-1, keepdims=True))

    return _pcall(
        body, f"inproj_dx_l{l}", (L // TMM,),
        [_rows_mm(NIN), _layer(l, (4, D, 1024)), _rows_mm(D), _layer(l, (1, D)), _rows_mm(D)],
        [_rows_mm(D), _full((1, D))],
        [SDS((L, D), F32), SDS((1, D), F32)],
        (dproj, w, x, g, dxn), jobs=jobs)


def _discretize(log_dt, lam_re, lam_im, b_re, b_im):
    dt = jnp.exp(log_dt)[..., None]
    mag = jnp.exp(lam_re * dt)
    ang = lam_im * dt
    abar_re = mag * jnp.cos(ang)
    abar_im = mag * jnp.sin(ang)
    num_re = abar_re - 1.0
    num_im = abar_im
    den = lam_re * lam_re + lam_im * lam_im
    coef_re = (num_re * lam_re + num_im * lam_im) / den
    coef_im = (num_im * lam_re - num_re * lam_im) / den
    bbar_re = coef_re[..., None] * b_re - coef_im[..., None] * b_im
    bbar_im = coef_re[..., None] * b_im + coef_im[..., None] * b_re
    return abar_re, abar_im, bbar_re, bbar_im


def _powers(abar_re, abar_im):
    ar, ai = abar_re.reshape(DEPTH, 1, GP), abar_im.reshape(DEPTH, 1, GP)
    rows_re, rows_im = [ar], [ai]
    for _ in range(7):
        pr, pi = rows_re[-1], rows_im[-1]
        rows_re.append(pr * ar - pi * ai)
        rows_im.append(pr * ai + pi * ar)
    row = jnp.arange(8)[:, None]

    def steps(rows, keep):
        return jnp.concatenate([jnp.where(keep(d), rows[d - 1], 0.0) for d in (1, 2, 4)], axis=1)

    neg_im = [-r for r in rows_im]
    fwd = (steps(rows_re, lambda d: row >= d), steps(rows_im, lambda d: row >= d),
           jnp.concatenate(rows_re, axis=1), jnp.concatenate(rows_im, axis=1))
    rev = (steps(rows_re, lambda d: row < 8 - d), steps(neg_im, lambda d: row < 8 - d),
           jnp.concatenate(rows_re[::-1], axis=1), jnp.concatenate(neg_im[::-1], axis=1))
    return fwd, rev


_EYE8 = functools.partial(jnp.eye, 8, dtype=F32)


def _expand_in(b):
    return jnp.einsum("lkgpc,gh->lkgchp", b.reshape(DEPTH, 4, 8, P, C), _EYE8()).reshape(DEPTH, 4, 128, 512)


def _extract_in(e):
    return jnp.einsum("lkgchp,gh->lkgpc", e.reshape(DEPTH, 4, 8, C, 8, P), _EYE8()).reshape(DEPTH, G, P, C)


def _expand_out(c):
    return jnp.einsum("lkgcp,gh->lkgphc", c.reshape(DEPTH, 4, 8, C, P), _EYE8()).reshape(DEPTH, 4, 512, 128)


def _extract_out(e):
    return jnp.einsum("lkgphc,gh->lkgcp", e.reshape(DEPTH, 4, 8, P, 8, C), _EYE8()).reshape(DEPTH, G, C, P)


SMALL = ("norm_g", "b_in", "ssm_log_dt", "ssm_lam_re", "ssm_lam_im", "ssm_b_re", "ssm_b_im",
         "ssm_c_re", "ssm_c_im", "ssm_d", "ssm_b_glu", "pool_w", "pool_scale")
BIG = ("w_in", "ssm_w_glu", "w_branch_a", "w_branch_b", "w_out")


def _step(x, target, w, m, v, place):
    sp = {n: w[n] for n in SMALL}
    final_norm_g = w["final_norm_g"]
    wbuf = dict(zip(BIG, _cast_own(place, [w[n] for n in BIG])))
    (abar_re, abar_im, bbar_re, bbar_im), disc_vjp = jax.vjp(
        _discretize, *(sp[n] for n in ("ssm_log_dt", "ssm_lam_re", "ssm_lam_im", "ssm_b_re", "ssm_b_im")))
    powers_fwd, powers_rev = _powers(abar_re, abar_im)
    b_re_x, b_im_x = _expand_in(bbar_re), _expand_in(bbar_im)
    c_re_x, c_imn_x = _expand_out(sp["ssm_c_re"]), _expand_out(-sp["ssm_c_im"])
    b_x = jnp.concatenate([b_re_x, b_im_x], axis=3).astype(BF16)
    t = lambda a: jnp.swapaxes(a, 2, 3).astype(BF16)
    c_re_t, c_imn_t, b_re_t, b_im_t = t(c_re_x), t(c_imn_x), t(b_re_x), t(b_im_x)
    c_re_x, c_imn_x = c_re_x.astype(BF16), c_imn_x.astype(BF16)
    row = lambda n: sp[n].reshape(DEPTH, 1, -1)
    g, b_in, dsk, b_glu, scale = row("norm_g"), row("b_in"), row("ssm_d"), row("ssm_b_glu"), row("pool_scale")
    pw = sp["pool_w"].astype(BF16)

    saved = []
    for l in range(DEPTH):
        three = BIG[2:]
        if l == 0:
            wbuf["w_in"], wbuf["ssm_w_glu"] = _comm_only(
                "gather_first", _RingGatherJob([wbuf["w_in"], wbuf["ssm_w_glu"]], 0))[0]
            jobs = [_GatherJob([wbuf[n] for n in three], 0), _GatherJob([wbuf["ssm_w_glu"]], 1)]
        else:
            jobs = []
        (h, proj), res = _norm_inproj(l, x, g, wbuf["w_in"], b_in, jobs)
        if res:
            wbuf.update(zip(three, res[0][0]))
            (wbuf["ssm_w_glu"],) = res[1][0]
        wg = dict(wbuf, ssm_w_glu=wbuf["ssm_w_glu"].reshape(DEPTH, W, W), w_out=wbuf["w_out"].reshape(DEPTH, D, D))
        job = _GatherJob([wbuf["w_in"]], l + 1) if l + 1 < DEPTH else _GatherJob([wbuf[n] for n in three], l)
        (sre, sim, y1, q, ya, pooled, mixed, yb), res = _fused(
            f"branches_fwd_l{l}", (NT,),
            [_s5_fwd(l, proj, b_x, c_re_x, c_imn_x, powers_fwd, dsk, wg["ssm_w_glu"], b_glu),
             _pool_fwd(l, proj, pw, scale)], jobs=[job])
        if l + 1 < DEPTH:
            (wbuf["w_in"],) = res[0][0]
        else:
            wbuf.update(zip(three, res[0][0]))
        wg = dict(wbuf, ssm_w_glu=wbuf["ssm_w_glu"].reshape(DEPTH, W, W), w_out=wbuf["w_out"].reshape(DEPTH, D, D))
        last = l + 1 == DEPTH
        pa, pb, mg, *tail = _merge_out(l, ya, yb, proj, x, wg["w_branch_a"], wg["w_branch_b"], wg["w_out"],
                                       (final_norm_g.reshape(1, D), target) if last else None)
        saved.append(dict(x=x, h=h, proj=proj, sre=sre, sim=sim, y1=y1, q=q, ya=ya,
                          pooled=pooled, mixed=mixed, yb=yb, pa=pa, pb=pb, mg=mg))
        if last:
            loss, dx, dgf = tail
        else:
            (x,) = tail

    per_layer = {n: [None] * DEPTH for n in ("norm_g", "b_in", "ssm_d", "ssm_b_glu", "pool_w", "pool_scale",
                                             "dare", "daim", "dbre", "dbim", "dcre", "dcimn")}
    red = _Reducer(place, w, m, v)
    for l in reversed(range(DEPTH)):
        s = saved[l]
        (dproj, dya, dyb, dwo, dwa, dwb, dbias_g), res = _merge_out_bwd(
            l, dx, s["mg"], s["proj"], s["pa"], s["pb"], s["ya"], s["yb"],
            wg["w_out"], wg["w_branch_a"], wg["w_branch_b"], red.jobs())
        red.land(res)
        (dproj, dwg, dbg, dd, dcre, dcimn, dbre, dbim, dare, daim, dbias_a, dpw, dsc, dbias_b), res = _fused(
            f"branches_bwd_l{l}", (NT,),
            [_s5_bwd(l, dya, s["proj"], s["y1"], s["q"], s["sre"], s["sim"], c_re_t, c_imn_t, b_re_t, b_im_t,
                     *powers_rev, dsk, wg["ssm_w_glu"]),
             _pool_bwd(l, dyb, s["proj"], s["mixed"], s["pooled"], pw, scale)],
            shared=(dproj, _rows(4 * W, 0, True)), jobs=red.jobs())
        red.land(res)
        rest = [dwg.reshape(4, W // 4, W), dwa, dwb, dwo.reshape(4, D // 4, D)]
        if l == 0:
            red.add(l, "rest", BIG[1:], rest)
        if l == 0:
            for r, rows in ((0, D - LAST_ROWS), (D // LAST_ROWS - 1, LAST_ROWS)):
                outs, res = _inproj_dw(l, r, s["h"], dproj, red.jobs(), rows=rows)
                red.land(res)
                red.add(l, f"in{r}", BIG[:1], outs, [r * rows])
        else:
            dwin, res = _inproj_dw(l, 0, s["h"], dproj, red.jobs(), rows=D)
            red.land(res)
        (dx, dg), res = _inproj_dx(l, dproj, wg["w_in"], s["x"], g, dx, red.jobs())
        red.land(res)
        if l > 0:
            red.add(l, "all", BIG, dwin + rest)
        for n, a in (("norm_g", dg.reshape(D)), ("b_in", jnp.concatenate([dbias_a, dbias_b, dbias_g], axis=1).reshape(NIN)),
                     ("ssm_d", dd.reshape(W)), ("ssm_b_glu", dbg.reshape(W)), ("pool_w", dpw), ("pool_scale", dsc.reshape(W)),
                     ("dare", dare), ("daim", daim), ("dbre", dbre), ("dbim", dbim), ("dcre", dcre), ("dcimn", dcimn)):
            per_layer[n][l] = a
    gs = {n: jnp.stack(a) for n, a in per_layer.items()}
    d_abar = [jnp.sum(gs.pop(n), axis=1).reshape(DEPTH, G, P) for n in ("dare", "daim")]
    (gs["ssm_log_dt"], gs["ssm_lam_re"], gs["ssm_lam_im"], gs["ssm_b_re"], gs["ssm_b_im"]) = disc_vjp(
        (*d_abar, _extract_in(gs.pop("dbre")), _extract_in(gs.pop("dbim"))))
    gs["ssm_c_re"], gs["ssm_c_im"] = _extract_out(gs.pop("dcre")), -_extract_out(gs.pop("dcimn"))
    gs["final_norm_g"] = dgf

    natural = {n: w[n].shape for n in REPLICATED}
    rw, rm, rv = {}, {}, {}
    for n in REPLICATED:
        shape = DENSE.get(n, natural[n])
        gs[n], rw[n], rm[n], rv[n] = (a.reshape(shape) for a in (gs[n], w[n], m[n], v[n]))
    small = [gs[n] for n in REPLICATED] + [loss]
    jobs = red.jobs()
    res = _pcall(None, "tail_exchange", (), [], [], [], [], jobs=jobs + [_SiblingJob(small, False)])[1]
    red.land(res[:len(jobs)])
    pair_small = _small_pair_sum(place, small, res[-1][1], [BF16 if a.ndim > 2 else F32 for a in small])
    jobs = red.jobs()
    res = _pcall(None, "tail_gather", (), [], [], [], [], jobs=jobs + [_ChipGatherJob(pair_small)])[1]
    red.land(res[:len(jobs)])
    assert not red.active
    small_parts = dict(zip(REPLICATED + ("loss",), res[-1][0]))

    k = len(REPLICATED)
    outs = _adamw_small("adamw_small", [rw[n] for n in REPLICATED], [small_parts[n] for n in REPLICATED],
                        [rm[n] for n in REPLICATED], [rv[n] for n in REPLICATED], small_parts["loss"], through=dx)
    results = {n: red.big[n] for n in BIG}
    results.update({n: [outs[1 + q * k + i].reshape(natural[n]) for q in range(4)] for i, n in enumerate(REPLICATED)})
    return outs[0][0, 0], outs[-1], results


def _place():
    x, y, c = lax.axis_index("x"), lax.axis_index("y"), lax.axis_index("c")
    chips = [(1 - x, y), (x, 1 - y), (1 - x, 1 - y)]
    return x, y, c, 2 * x + y, chips, [2 * cx + cy for cx, cy in chips]


def _remote(src, dst, ssem, rsem, dev):
    return pltpu.make_async_remote_copy(src_ref=src, dst_ref=dst, send_sem=ssem, recv_sem=rsem,
                                        device_id=dev, device_id_type=MESH)


class _GatherJob:
    def __init__(self, bufs, l):
        self.srcs, self.bufs, self.news, self.l = [], list(bufs), [], l
        self.scratch = [pltpu.SemaphoreType.DMA((len(self.bufs), 3))] * 4

    def _half(self, ref, k, h):
        rows = ref.shape[2] // 2
        return ref.at[self.l, k, pl.ds(pl.multiple_of(h * rows, 8), rows), :]

    def _ici(self, bufs, sems, a, j, k):
        _, _, c, _, chips, _ = _place()
        blk = self._half(bufs[a], k, c)
        return _remote(blk, blk, sems[0].at[a, j], sems[1].at[a, j], (*chips[j], c))

    def _d2d(self, bufs, sems, a, j, k, h):
        x, y, c, _, _, _ = _place()
        blk = self._half(bufs[a], k, h)
        return _remote(blk, blk, sems[2].at[a, j], sems[3].at[a, j], (x, y, 1 - c))

    def start(self, srcs, bufs, news, sems):
        me = _place()[3]
        for a in range(len(self.bufs)):
            for j in range(3):
                self._ici(bufs, sems, a, j, me).start()

    def finish(self, srcs, bufs, news, sems):
        _, _, c, me, _, cid = _place()
        pairs = [(a, j) for a in range(len(self.bufs)) for j in range(3)]
        for a, j in pairs:
            self._ici(bufs, sems, a, j, cid[j]).wait_recv()
            self._d2d(bufs, sems, a, j, cid[j], c).start()
        for a, j in pairs:
            self._d2d(bufs, sems, a, j, cid[j], 1 - c).wait_recv()
        for a, j in pairs:
            self._ici(bufs, sems, a, j, me).wait_send()
            self._d2d(bufs, sems, a, j, cid[j], c).wait_send()


class _RingGatherJob(_GatherJob):
    def __init__(self, bufs, l):
        super().__init__(bufs, l)
        n = len(self.bufs)
        self.scratch = [pltpu.SemaphoreType.DMA((n, 2))] * 4 + [pltpu.SemaphoreType.DMA((n, 4))] * 2

    def _rows(self, ref, k, h, part):
        half = ref.shape[2] // 2
        start, rows = (h * half, half) if part is None else (h * half + part * (half // 2), half // 2)
        return ref.at[self.l, k, pl.ds(pl.multiple_of(start, 8), rows), :]

    def _to_chip(self, bufs, sems, base, a, j, k, part):
        _, _, c, _, chips, _ = _place()
        blk = self._rows(bufs[a], k, c, part)
        return _remote(blk, blk, sems[base].at[a, j], sems[base + 1].at[a, j], (*chips[j], c))

    def _to_sibling(self, bufs, sems, a, i, k, h, part):
        x, y, c, _, _, _ = _place()
        blk = self._rows(bufs[a], k, h, part)
        return _remote(blk, blk, sems[4].at[a, i], sems[5].at[a, i], (x, y, 1 - c))

    def start(self, srcs, bufs, news, sems):
        me = _place()[3]
        for a in range(len(self.bufs)):
            for j in range(2):
                self._to_chip(bufs, sems, 0, a, j, me, None).start()

    def finish(self, srcs, bufs, news, sems):
        _, _, c, me, _, cid = _place()
        arrays = range(len(self.bufs))
        for a in arrays:
            for j in (1, 0):
                self._to_chip(bufs, sems, 0, a, j, cid[j], None).wait_recv()
                self._to_chip(bufs, sems, 2, a, 1 - j, cid[j], 1 - j).start()
                self._to_sibling(bufs, sems, a, j, cid[j], c, None).start()
        for a in arrays:
            for part in range(2):
                self._to_chip(bufs, sems, 2, a, part, cid[2], part).wait_recv()
                self._to_sibling(bufs, sems, a, 2 + part, cid[2], c, part).start()
        for a in arrays:
            for j in range(2):
                self._to_sibling(bufs, sems, a, j, cid[j], 1 - c, None).wait_recv()
                self._to_sibling(bufs, sems, a, 2 + j, cid[2], 1 - c, j).wait_recv()
        for a in arrays:
            for j in range(2):
                self._to_chip(bufs, sems, 0, a, j, me, None).wait_send()
                self._to_chip(bufs, sems, 2, a, 1 - j, cid[j], 1 - j).wait_send()
                self._to_sibling(bufs, sems, a, j, cid[j], c, None).wait_send()
                self._to_sibling(bufs, sems, a, 2 + j, cid[2], c, j).wait_send()


class _SiblingJob:
    def __init__(self, srcs, rows_half):
        self.srcs, self.bufs, self.rows_half = list(srcs), [], rows_half
        self.news = [SDS((s.shape[0], s.shape[1] // 2, s.shape[2]) if rows_half else s.shape, s.dtype) for s in srcs]
        self.scratch = [pltpu.SemaphoreType.DMA((len(self.srcs),))] * 2

    def _copy(self, srcs, news, sems, a):
        x, y, c, _, _, _ = _place()
        src = srcs[a]
        if self.rows_half:
            rows = src.shape[1] // 2
            src = src.at[:, pl.ds(pl.multiple_of((1 - c) * rows, 8), rows), :]
        return _remote(src, news[a], sems[0].at[a], sems[1].at[a], (x, y, 1 - c))

    def start(self, srcs, bufs, news, sems):
        for a in range(len(self.srcs)):
            self._copy(srcs, news, sems, a).start()

    def finish(self, srcs, bufs, news, sems):
        for a in range(len(self.srcs)):
            self._copy(srcs, news, sems, a).wait()


class _ScatterJob:
    def __init__(self, parts):
        self.srcs, self.bufs = list(parts), []
        self.news = [SDS((3,) + p.shape[1:], p.dtype) for p in parts]
        self.scratch = [pltpu.SemaphoreType.DMA((len(self.srcs), 3))] * 2

    def _copy(self, srcs, news, sems, a, j):
        _, _, c, _, chips, cid = _place()
        return _remote(srcs[a].at[cid[j]], news[a].at[j], sems[0].at[a, j], sems[1].at[a, j], (*chips[j], c))

    def start(self, srcs, bufs, news, sems):
        for a in range(len(self.srcs)):
            for j in range(3):
                self._copy(srcs, news, sems, a, j).start()

    def finish(self, srcs, bufs, news, sems):
        for a in range(len(self.srcs)):
            for j in range(3):
                self._copy(srcs, news, sems, a, j).wait()


def _comm_only(name, job):
    return _pcall(None, name, (), [], [], [], [], jobs=[job])[1][0]


class _ChipGatherJob(_GatherJob):
    def __init__(self, bufs):
        super().__init__(bufs, None)

    def _half(self, ref, k, h):
        return ref.at[k, h]


def _cast_own(place, ws):
    n = len(ws)

    def body(p_ref, *refs):
        for i_ref, o_ref in zip(refs[:n], refs[n:]):
            o_ref[...] = i_ref[...].astype(BF16)

    return pl.pallas_call(
        body, name="cast_own_shards",
        grid_spec=pltpu.PrefetchScalarGridSpec(
            num_scalar_prefetch=1, grid=(DEPTH,),
            in_specs=[pl.BlockSpec((None,) + a.shape[1:], lambda l, p: (l, 0, 0)) for a in ws],
            out_specs=[pl.BlockSpec((None, None) + a.shape[1:], lambda l, p: (l, p[1], 0, 0)) for a in ws]),
        out_shape=[SDS((DEPTH, 4) + a.shape[1:], BF16) for a in ws],
        compiler_params=_params("arbitrary"),
    )(place, *ws)


def _half_tiles(a_):
    rows = a_ // 2
    ta = next(t for t in (256, 128, 64, 32, 16, 8) if rows % t == 0)
    return rows, ta, rows // ta


def _pair_sums_bf16(name, place, owns, recvs):
    n = len(owns)

    def body(p_ref, *refs):
        for own_ref, recv_ref, out_ref in zip(refs[:n], refs[n:2 * n], refs[2 * n:]):
            out_ref[...] = (own_ref[...] + recv_ref[...]).astype(BF16)

    def own_half(a):
        return pl.BlockSpec((None, a.shape[1] // 2, a.shape[2]), lambda s, p: (s, p[0], 0))

    def block(a):
        return pl.BlockSpec((None,) + a.shape[1:], lambda s, p: (s, 0, 0))

    return pl.pallas_call(
        body, name=name,
        grid_spec=pltpu.PrefetchScalarGridSpec(
            num_scalar_prefetch=1, grid=(4,),
            in_specs=[own_half(a) for a in owns] + [block(r) for r in recvs],
            out_specs=[block(r) for r in recvs]),
        out_shape=[SDS(r.shape, BF16) for r in recvs],
        compiler_params=_params("arbitrary"),
    )(place, *owns, *recvs)


def _shard_sums(name, place, owns, recvs, rbufs):
    n = len(owns)

    def body(p_ref, *refs):
        for own_ref, recv_ref, r_ref, out_ref in zip(refs[:n], refs[n:2 * n], refs[2 * n:3 * n], refs[3 * n:]):
            acc = own_ref[...] + recv_ref[...]
            for j in range(3):
                acc = acc + r_ref[j].astype(F32)
            out_ref[...] = acc

    steps = 2

    def own_half(a):
        return pl.BlockSpec((None, a.shape[1] // 2 // steps, a.shape[2]), lambda i, p: (p[1], steps * p[0] + i, 0))

    def recv_block(a):
        return pl.BlockSpec((None, a.shape[1] // steps, a.shape[2]), lambda i, p: (p[1], i, 0))

    return pl.pallas_call(
        body, name=name,
        grid_spec=pltpu.PrefetchScalarGridSpec(
            num_scalar_prefetch=1, grid=(steps,),
            in_specs=([own_half(a) for a in owns] + [recv_block(r) for r in recvs]
                      + [pl.BlockSpec((3, rb.shape[1] // steps, rb.shape[2]), lambda i, p: (0, i, 0)) for rb in rbufs]),
            out_specs=[pl.BlockSpec((r.shape[1] // steps, r.shape[2]), lambda i, p: (i, 0)) for r in recvs]),
        out_shape=[SDS(r.shape[1:], F32) for r in recvs],
        compiler_params=_params("arbitrary"),
    )(place, *owns, *recvs, *rbufs)


def _small_pair_sum(place, mine, recv, dtypes):
    n = len(mine)

    def body(p_ref, *refs):
        for m_ref, r_ref, o_ref in zip(refs[:n], refs[n:2 * n], refs[2 * n:]):
            o_ref[...] = (m_ref[...] + r_ref[...]).astype(o_ref.dtype)

    def whole(a):
        zeros = (0,) * a.ndim
        return pl.BlockSpec(a.shape, lambda i, p: zeros)

    def mine_blk(a):
        zeros = (0,) * a.ndim
        return pl.BlockSpec((None,) + a.shape, lambda i, p: (p[1],) + zeros)

    return pl.pallas_call(
        body, name="small_pair_sum",
        grid_spec=pltpu.PrefetchScalarGridSpec(
            num_scalar_prefetch=1, grid=(1,),
            in_specs=[whole(a) for a in mine] + [whole(a) for a in recv],
            out_specs=[mine_blk(a) for a in mine]),
        out_shape=[SDS((4,) + a.shape, dt) for a, dt in zip(mine, dtypes)],
        compiler_params=_params("arbitrary"),
    )(place, *mine, *recv)


def _adam_math(w, g, m, v):
    m = B1 * m + (1.0 - B1) * g
    v = B2 * v + (1.0 - B2) * (g * g)
    m_hat = m / (1.0 - B1 ** STEP)
    v_hat = v / (1.0 - B2 ** STEP)
    delta = -LR * (m_hat / (jnp.sqrt(v_hat) + EPS_A) + WD * w)
    return delta, m, v


def _adamw_big(name, l, row0, w, m, v, mine, other, prev, jobs=()):
    _, _, b_ = w.shape
    _, ta, nh = _half_tiles(2 * mine.shape[0])
    prev = list(prev or [])

    def body(w_ref, m_ref, v_ref, mine_ref, other_ref, *rest):
        g_ref, d_ref, mo_ref, vo_ref = rest[len(prev):]
        g = jnp.where(pl.program_id(0) == lax.axis_index("c"), mine_ref[...], other_ref[...])
        g_ref[...] = g
        d_ref[...], mo_ref[...], vo_ref[...] = _adam_math(w_ref[...], g, m_ref[...], v_ref[...])

    slab = pl.BlockSpec((None, ta, b_), lambda h, i: (l, row0 // ta + h * nh + i, 0))
    half = pl.BlockSpec((ta, b_), lambda h, i: (i, 0))
    outs, res = _pcall(
        body, name, (2, nh), [slab, slab, slab, half, half] + [_ANY] * len(prev), [slab] * 4, [SDS(w.shape, F32)] * 4,
        (w, m, v, mine, other, *prev), aliases={5 + k: k for k in range(len(prev))}, jobs=jobs)
    return outs, res


def _adamw_small(name, ws, parts, ms, vs, loss_parts=None, through=None):
    k = len(ws)
    extra = [] if loss_parts is None else [loss_parts]
    n_vmem = 4 * k + len(extra)
    n_out = len(extra) + 4 * k
    hand_on = [] if through is None else [through]

    def chip_sum(p_ref):
        p = [p_ref[k].astype(F32) for k in range(4)]
        return ((p[0] + p[1]) + p[2]) + p[3]

    def body(*refs):
        w_refs, p_refs, m_refs, v_refs = refs[:k], refs[k:2 * k], refs[2 * k:3 * k], refs[3 * k:4 * k]
        outs = refs[n_vmem + len(hand_on):]
        if extra:
            outs[0][...] = chip_sum(refs[4 * k])
            outs = outs[1:]
        for a in range(k):
            g = chip_sum(p_refs[a])
            outs[a][...] = g
            outs[k + a][...], outs[2 * k + a][...], outs[3 * k + a][...] = _adam_math(
                w_refs[a][...], g, m_refs[a][...], v_refs[a][...])

    like = [SDS(a.shape, F32) for a in ws]
    in_vmem = pl.BlockSpec(memory_space=pltpu.VMEM)
    return pl.pallas_call(
        body, name=name,
        in_specs=[in_vmem] * n_vmem + [_ANY] * len(hand_on), out_specs=[in_vmem] * n_out + [_ANY] * len(hand_on),
        out_shape=([SDS(loss_parts.shape[1:], F32)] if extra else []) + like * 4
        + [SDS(a.shape, a.dtype) for a in hand_on],
        input_output_aliases={n_vmem: n_out} if hand_on else {},
        compiler_params=pltpu.CompilerParams(vmem_limit_bytes=VMEM_LIMIT),
    )(*ws, *parts, *ms, *vs, *extra, *hand_on)


class _Reducer:
    def __init__(self, place, w, m, v):
        self.place, self.w, self.m, self.v = place, w, m, v
        self.active, self.riding = [], []
        self.big = {n: None for n in BIG}

    def add(self, l, tag, names, own, row0s=None):
        self.active.append(dict(l=l, key=f"{tag}_l{l}", names=names, own=list(own), row0s=row0s or [0] * len(names),
                                stage=0))

    def jobs(self):
        self.riding = list(self.active)
        return [(_SiblingJob(g["own"], True), _ScatterJob(g.get("parts", [])), _SiblingJob(g.get("shard", []), False))
                [g["stage"]] for g in self.riding]

    def land(self, res):
        for g, (_, news) in zip(self.riding, res):
            if g["stage"] == 0:
                g["recv"] = news
                g["parts"] = _pair_sums_bf16(f"pair_sums_{g['key']}", self.place, g["own"], news)
            elif g["stage"] == 1:
                g["shard"] = _shard_sums(f"shard_sums_{g['key']}", self.place, g["own"], g["recv"], news)
            else:
                for n, mine, other, row0 in zip(g["names"], g["shard"], news, g["row0s"]):
                    self.big[n] = _adamw_big(f"adamw_{n}_{row0}_{g['key']}", g["l"], row0, self.w[n], self.m[n], self.v[n],
                                             mine, other, self.big[n])[0]
                self.active.remove(g)
            g["stage"] += 1
        self.riding = []


WEIGHTS = ("norm_g", "w_in", "b_in", "ssm_log_dt", "ssm_lam_re", "ssm_lam_im", "ssm_b_re", "ssm_b_im", "ssm_c_re",
           "ssm_c_im", "ssm_d", "ssm_w_glu", "ssm_b_glu", "pool_w", "pool_scale", "w_branch_a", "w_branch_b", "w_out",
           "final_norm_g")
REPLICATED = SMALL + ("final_norm_g",)
DENSE = {"ssm_b_re": (DEPTH, G, P * C), "ssm_b_im": (DEPTH, G, P * C), "final_norm_g": (2, D // 2)}


def kernel(x, norm_g, w_in, b_in, ssm_log_dt, ssm_lam_re, ssm_lam_im, ssm_b_re, ssm_b_im, ssm_c_re, ssm_c_im, ssm_d, ssm_w_glu, ssm_b_glu, pool_w, pool_scale, w_branch_a, w_branch_b, w_out, final_norm_g, loss_target, m_norm_g, m_w_in, m_b_in, m_ssm_log_dt, m_ssm_lam_re, m_ssm_lam_im, m_ssm_b_re, m_ssm_b_im, m_ssm_c_re, m_ssm_c_im, m_ssm_d, m_ssm_w_glu, m_ssm_b_glu, m_pool_w, m_pool_scale, m_w_branch_a, m_w_branch_b, m_w_out, m_final_norm_g, v_norm_g, v_w_in, v_b_in, v_ssm_log_dt, v_ssm_lam_re, v_ssm_lam_im, v_ssm_b_re, v_ssm_b_im, v_ssm_c_re, v_ssm_c_im, v_ssm_d, v_ssm_w_glu, v_ssm_b_glu, v_pool_w, v_pool_scale, v_w_branch_a, v_w_branch_b, v_w_out, v_final_norm_g):
    w = dict(zip(WEIGHTS, (norm_g, w_in, b_in, ssm_log_dt, ssm_lam_re, ssm_lam_im, ssm_b_re, ssm_b_im, ssm_c_re,
                           ssm_c_im, ssm_d, ssm_w_glu, ssm_b_glu, pool_w, pool_scale, w_branch_a, w_branch_b, w_out,
                           final_norm_g)))
    m = dict(zip(WEIGHTS, (m_norm_g, m_w_in, m_b_in, m_ssm_log_dt, m_ssm_lam_re, m_ssm_lam_im, m_ssm_b_re, m_ssm_b_im,
                           m_ssm_c_re, m_ssm_c_im, m_ssm_d, m_ssm_w_glu, m_ssm_b_glu, m_pool_w, m_pool_scale,
                           m_w_branch_a, m_w_branch_b, m_w_out, m_final_norm_g)))
    v = dict(zip(WEIGHTS, (v_norm_g, v_w_in, v_b_in, v_ssm_log_dt, v_ssm_lam_re, v_ssm_lam_im, v_ssm_b_re, v_ssm_b_im,
                           v_ssm_c_re, v_ssm_c_im, v_ssm_d, v_ssm_w_glu, v_ssm_b_glu, v_pool_w, v_pool_scale,
                           v_w_branch_a, v_w_branch_b, v_w_out, v_final_norm_g)))
    place = jnp.stack([lax.axis_index("c"), 2 * lax.axis_index("x") + lax.axis_index("y")]).astype(jnp.int32)

    total_loss, dx, results = _step(x[0], loss_target[0], w, m, v, place)
    return (total_loss, dx[None], *[results[n][q] for q in range(4) for n in WEIGHTS])
```

```python
import functools

import jax
import jax.numpy as jnp
from jax import lax
from jax.experimental import pallas as pl
from jax.experimental.pallas import tpu as pltpu

F32, BF16 = jnp.float32, jnp.bfloat16
SDS = jax.ShapeDtypeStruct
MESH = pl.DeviceIdType.MESH

DEPTH = 2
L = 2048
D = 1024
NIN = 4096
W = 512
G, P, C = 32, 64, 16
GP = G * P
WINS = (2, 4, 8, 16)
TM = 256
NT = L // TM
TMM = 512
TK = 2048
LAST_ROWS = 512
EPS = 1e-6
VMEM_LIMIT = 56 * 2**20

LR, B1, B2, EPS_A, WD, STEP = 0.001, 0.9, 0.999, 1e-08, 0.01, 10


def _params(*sem):
    return pltpu.CompilerParams(dimension_semantics=sem, vmem_limit_bytes=VMEM_LIMIT)


_ANY = pl.BlockSpec(memory_space=pl.ANY)


def _full(shape):
    zeros = (0,) * len(shape)
    return pl.BlockSpec(shape, lambda *_: zeros)


def _layer(l, shape):
    zeros = (0,) * len(shape)
    return pl.BlockSpec((None,) + shape, lambda *_: (l,) + zeros)


def _rows(width, col=0, reverse=False, tm=TM):
    if reverse:
        return pl.BlockSpec((tm, width), lambda i: (L // tm - 1 - i, col))
    return pl.BlockSpec((tm, width), lambda i: (i, col))


def _rows_mm(width, col=0):
    return _rows(width, col, False, TMM)


def _pcall(body, name, grid, in_specs, out_specs, out_shape, args, scratch=(), aliases=None, jobs=()):
    in_specs, out_specs, out_shape, args, scratch = list(in_specs), list(out_specs), list(out_shape), list(args), list(scratch)
    aliases = dict(aliases or {})
    jobs = [j for j in jobs if j is not None]
    n_in, n_out, n_scr = len(in_specs), len(out_specs), len(scratch)
    srcs = [s for j in jobs for s in j.srcs]
    bufs = [b for j in jobs for b in j.bufs]
    news = [s for j in jobs for s in j.news]
    aliases.update({n_in + len(srcs) + k: n_out + k for k in range(len(bufs))})

    def hosted(*refs):
        cuts = [n_in, len(srcs), len(bufs), n_out, len(bufs), len(news), n_scr]
        parts, p = [], 0
        for n in cuts:
            parts.append(refs[p:p + n])
            p += n
        ins, src_r, _, outs, buf_r, new_r, scr = parts
        sem_r = refs[p:]
        views, ps, pb, pn, pm = [], 0, 0, 0, 0
        for j in jobs:
            views.append((src_r[ps:ps + len(j.srcs)], buf_r[pb:pb + len(j.bufs)], new_r[pn:pn + len(j.news)],
                          sem_r[pm:pm + len(j.scratch)]))
            ps, pb, pn, pm = ps + len(j.srcs), pb + len(j.bufs), pn + len(j.news), pm + len(j.scratch)

        def run(phase):
            for j, v in zip(jobs, views):
                getattr(j, phase)(*v)

        def at_step(step):
            return functools.reduce(jnp.logical_and, [pl.program_id(d) == step(d) for d in range(len(grid))])

        if not grid:
            run("start")
            run("finish")
            return
        pl.when(at_step(lambda d: 0))(lambda: run("start"))
        body(*ins, *outs, *scr)
        pl.when(at_step(lambda d: grid[d] - 1))(lambda: run("finish"))

    outs = pl.pallas_call(
        hosted if jobs else body, name=name, **({"grid": grid} if grid else {}),
        in_specs=in_specs + [_ANY] * (len(srcs) + len(bufs)), out_specs=out_specs + [_ANY] * (len(bufs) + len(news)),
        out_shape=out_shape + [SDS(b.shape, b.dtype) for b in bufs] + news,
        input_output_aliases=aliases, scratch_shapes=scratch + [s for j in jobs for s in j.scratch],
        compiler_params=_params(*(("arbitrary",) * len(grid))))(*args, *srcs, *bufs)
    res, pb, pn = [], n_out, n_out + len(bufs)
    for j in jobs:
        res.append((list(outs[pb:pb + len(j.bufs)]), list(outs[pn:pn + len(j.news)])))
        pb, pn = pb + len(j.bufs), pn + len(j.news)
    return list(outs[:n_out]), res


def _fused(name, grid, parts, shared=None, jobs=()):
    def body(*refs):
        pos = [0]

        def take(n):
            pos[0] += n
            return refs[pos[0] - n:pos[0]]

        ins = [take(len(p["in_specs"])) for p in parts]
        if shared:
            take(1)
            block = take(1)[0]
        outs = [take(len(p["out_specs"])) for p in parts]
        scr = [take(len(p["scratch"])) for p in parts]
        col = 0
        for p, i, o, s in zip(parts, ins, outs, scr):
            view = []
            if shared:
                view = [block.at[:, pl.ds(col, p["width"])]]
                col += p["width"]
            p["body"](*i, *view, *o, *s)

    in_specs = [s for p in parts for s in p["in_specs"]] + ([_ANY] if shared else [])
    out_specs = ([shared[1]] if shared else []) + [s for p in parts for s in p["out_specs"]]
    out_shape = ([SDS(shared[0].shape, shared[0].dtype)] if shared else []) + [s for p in parts for s in p["out_shape"]]
    args = [a for p in parts for a in p["args"]] + ([shared[0]] if shared else [])
    return _pcall(body, name, grid, in_specs, out_specs, out_shape, args, [s for p in parts for s in p["scratch"]],
                  {len(in_specs) - 1: 0} if shared else None, jobs)


def _dot(a, b):
    return jnp.dot(a, b, preferred_element_type=F32)


def _dot_nt(a, b):
    return lax.dot_general(a, b, (((1,), (1,)), ((), ())), preferred_element_type=F32)


def _dot_tn(a, b):
    return lax.dot_general(a, b, (((0,), (0,)), ((), ())), preferred_element_type=F32)


_K0 = 0.7978845608028654
_K1 = 0.044715


def _gelu(x):
    return 0.5 * x * (1.0 + jnp.tanh(_K0 * (x + _K1 * (x * x * x))))


def _gelu_grad(x):
    t = jnp.tanh(_K0 * (x + _K1 * (x * x * x)))
    return 0.5 * (1.0 + t) + 0.5 * x * (1.0 - t * t) * (_K0 * (1.0 + 3.0 * _K1 * x * x))


def _sigmoid(x):
    return jax.nn.sigmoid(x)


def _norm_inproj(l, x, g, w, b, jobs=()):
    def body(x_ref, g_ref, w_ref, b_ref, h_ref, proj_ref):
        xv = x_ref[...]
        r = lax.rsqrt(jnp.mean(xv * xv, axis=-1, keepdims=True) + EPS)
        hb = ((xv * r) * g_ref[...]).astype(BF16)
        h_ref[...] = hb
        for j in range(4):
            cs = slice(j * 1024, (j + 1) * 1024)
            proj_ref[:, cs] = _dot(hb, w_ref[j]) + b_ref[:, cs]

    return _pcall(
        body, f"norm_inproj_l{l}", (L // TMM,),
        [_rows_mm(D), _layer(l, (1, D)), _layer(l, (4, D, 1024)), _layer(l, (1, NIN))],
        [_rows_mm(D), _rows_mm(NIN)],
        [SDS((L, D), BF16), SDS((L, NIN), F32)],
        (x, g, w, b), jobs=jobs)


def _scan_tile(re_ref, im_ref, st_re, st_im, cr_re, cr_im, carry, reverse, each_chunk=None):
    def chunk(ci, carry):
        c = (TM // 8 - 1 - ci) if reverse else ci
        rows = pl.ds(pl.multiple_of(c * 8, 8), 8)
        new = []
        for lb in range(GP // 512):
            cols = slice(lb * 512, (lb + 1) * 512)
            vr = re_ref[rows, cols]
            vi = im_ref[rows, cols]
            for s, d in enumerate((1, 2, 4)):
                ar = st_re[8 * s:8 * s + 8, cols]
                ai = st_im[8 * s:8 * s + 8, cols]
                sr = pltpu.roll(vr, 8 - d if reverse else d, 0)
                si = pltpu.roll(vi, 8 - d if reverse else d, 0)
                vr, vi = vr + ar * sr - ai * si, vi + ar * si + ai * sr
            cr, ci_ = carry[2 * lb], carry[2 * lb + 1]
            pr = cr_re[:, cols]
            pi = cr_im[:, cols]
            vr, vi = vr + pr * cr - pi * ci_, vi + pr * ci_ + pi * cr
            re_ref[rows, cols] = vr
            im_ref[rows, cols] = vi
            if each_chunk is not None:
                each_chunk(c, cols, vr, vi)
            if reverse:
                new += [vr[0:1], vi[0:1]]
            else:
                new += [vr[7:8], vi[7:8]]
        return tuple(new)

    return lax.fori_loop(0, TM // 8, chunk, carry)


def _load_carry(car_ref):
    return tuple(car_ref[r:r + 1, lb * 512:(lb + 1) * 512] for lb in range(GP // 512) for r in (0, 1))


def _store_carry(car_ref, carry):
    for lb in range(GP // 512):
        car_ref[0:1, lb * 512:(lb + 1) * 512] = carry[2 * lb]
        car_ref[1:2, lb * 512:(lb + 1) * 512] = carry[2 * lb + 1]


def _s5_fwd(l, proj, bexp, cre, cimn, powers, dsk, wglu, bglu):
    def body(ua_ref, za_ref, bexp_ref, cre_ref, cimn_ref, st_re_ref, st_im_ref, cr_re_ref, cr_im_ref,
             d_ref, wg_ref, bg_ref, sre_ref, sim_ref, y1_ref, q_ref, ya_ref, car_ref):
        @pl.when(pl.program_id(0) == 0)
        def _():
            car_ref[...] = jnp.zeros_like(car_ref)

        u = ua_ref[...]
        ub = u.astype(BF16)
        for k in range(4):
            bu = _dot(ub[:, 128 * k:128 * (k + 1)], bexp_ref[k])
            sre_ref[:, 512 * k:512 * (k + 1)] = bu[:, :512]
            sim_ref[:, 512 * k:512 * (k + 1)] = bu[:, 512:]
        carry = _scan_tile(sre_ref, sim_ref, st_re_ref, st_im_ref, cr_re_ref, cr_im_ref, _load_carry(car_ref), False)
        _store_carry(car_ref, carry)
        for k in range(4):
            blk = slice(512 * k, 512 * (k + 1))
            ks = slice(128 * k, 128 * (k + 1))
            y0 = _dot(sre_ref[:, blk].astype(BF16), cre_ref[k]) + _dot(sim_ref[:, blk].astype(BF16), cimn_ref[k])
            y1_ref[:, ks] = y0 + d_ref[:, ks] * u[:, ks]
        y2 = _gelu(y1_ref[...])
        q = _dot(y2.astype(BF16), wg_ref[...]) + bg_ref[...]
        q_ref[...] = q
        za = za_ref[...]
        ya_ref[...] = ((y2 * _sigmoid(q)) * (za * _sigmoid(za))).astype(BF16)

    return dict(
        body=body,
        in_specs=[_rows(W, 0), _rows(W, 1), _layer(l, (4, 128, 1024)), _layer(l, (4, 512, 128)),
                  _layer(l, (4, 512, 128)), _layer(l, (24, GP)), _layer(l, (24, GP)), _layer(l, (8, GP)),
                  _layer(l, (8, GP)), _layer(l, (1, W)), _layer(l, (W, W)), _layer(l, (1, W))],
        out_specs=[_rows(GP), _rows(GP), _rows(W), _rows(W), _rows(W)],
        out_shape=[SDS((L, GP), F32), SDS((L, GP), F32), SDS((L, W), F32), SDS((L, W), F32), SDS((L, W), BF16)],
        args=(proj, proj, bexp, cre, cimn, *powers, dsk, wglu, bglu),
        scratch=[pltpu.VMEM((8, GP), F32)])


def _pool_fwd(l, proj, pw, scale):
    def body(ub_ref, zb_ref, pw_ref, sc_ref, pooled_ref, mixed_ref, yb_ref, buf):
        i = pl.program_id(0)

        @pl.when(i == 0)
        def _():
            buf[0:16, :] = jnp.zeros((16, W), F32)

        u = ub_ref[...]
        buf[16:16 + TM, :] = u
        t = i * TM + lax.broadcasted_iota(jnp.int32, (TM, 128), 0)
        for gi, win in enumerate(WINS):
            cs = slice(128 * gi, 128 * (gi + 1))
            acc = u[:, cs]
            for k in range(1, win):
                acc = acc + buf[16 - k:16 - k + TM, cs]
            cnt = jnp.minimum(t + 1, win).astype(F32)
            pb = (acc / cnt - u[:, cs]).astype(BF16)
            pooled_ref[:, cs] = pb
            mixed_ref[:, cs] = _dot(pb, pw_ref[gi])
        zb = zb_ref[...]
        yb_ref[...] = ((mixed_ref[...] * sc_ref[...]) * (zb * _sigmoid(zb))).astype(BF16)
        buf[0:16, :] = buf[TM:TM + 16, :]

    return dict(
        body=body,
        in_specs=[_rows(W, 2), _rows(W, 3), _layer(l, (4, 128, 128)), _layer(l, (1, W))],
        out_specs=[_rows(W), _rows(W), _rows(W)],
        out_shape=[SDS((L, W), BF16), SDS((L, W), F32), SDS((L, W), BF16)],
        args=(proj, proj, pw, scale),
        scratch=[pltpu.VMEM((TM + 16, W), F32)])


def _merge_out(l, ya, yb, proj, x, wa, wb, wo, head=None):
    def body(ya_ref, yb_ref, ga_ref, gb_ref, x_ref, wa_ref, wb_ref, wo_ref, *rest):
        pa_ref, pb_ref, mg_ref = rest[2:5] if head else rest[0:3]
        ya = ya_ref[...]
        yb = yb_ref[...]
        for j in range(4):
            cs = slice(256 * j, 256 * (j + 1))
            pa_ref[:, cs] = _dot(ya, wa_ref[j])
            pb_ref[:, cs] = _dot(yb, wb_ref[j])
        merged = _sigmoid(ga_ref[...]) * pa_ref[...] + _sigmoid(gb_ref[...]) * pb_ref[...]
        mb = merged.astype(BF16)
        mg_ref[...] = mb
        x_next = x_ref[...] + _dot(mb, wo_ref[...])
        if head:
            _loss_tile(x_next, rest[0], rest[1], *rest[5:])
        else:
            rest[3][...] = x_next

    rows, out_rows = _rows_mm(D), SDS((L, D), F32)
    return pl.pallas_call(
        body, name=f"merge_out_l{l}", grid=(L // TMM,),
        in_specs=[_rows_mm(W), _rows_mm(W), _rows_mm(D, 2), _rows_mm(D, 3), rows,
                  _layer(l, (4, W, 256)), _layer(l, (4, W, 256)), _layer(l, (D, D))] + ([_full((1, D)), rows] if head else []),
        out_specs=[rows, rows, rows] + ([_full((2, 128)), rows, _full((1, D))] if head else [rows]),
        out_shape=[out_rows, out_rows, SDS((L, D), BF16)]
        + ([SDS((2, 128), F32), out_rows, SDS((1, D), F32)] if head else [out_rows]),
        compiler_params=_params("arbitrary"),
    )(ya, yb, proj, proj, x, wa, wb, wo, *(head or ()))


def _loss_tile(xv, g_ref, t_ref, loss_ref, dx_ref, dg_ref):
    @pl.when(pl.program_id(0) == 0)
    def _():
        loss_ref[...] = jnp.zeros_like(loss_ref)
        dg_ref[...] = jnp.zeros_like(dg_ref)

    g = g_ref[...]
    r = lax.rsqrt(jnp.mean(xv * xv, axis=-1, keepdims=True) + EPS)
    xn = xv * r
    err = xn * g - t_ref[...]
    part = jnp.sum(jnp.mean(err * err, axis=-1, keepdims=True), axis=0, keepdims=True)
    loss_ref[...] += 0.5 * part
    dy = err * (1.0 / D)
    dg_ref[...] += jnp.sum(dy * xn, axis=0, keepdims=True)
    dxn = dy * g
    dx_ref[...] = r * (dxn - xn * jnp.mean(dxn * xn, axis=-1, keepdims=True))


def _merge_out_bwd(l, dxn, mg, proj, pa, pb, ya, yb, wo, wa, wb, jobs=()):
    def body(dx_ref, mg_ref, ga_ref, gb_ref, pa_ref, pb_ref, ya_ref, yb_ref, wo_ref, wa_ref, wb_ref,
             dg_ref, dya_ref, dyb_ref, dwo_ref, dwa_ref, dwb_ref, dbias_ref):
        @pl.when(pl.program_id(0) == 0)
        def _():
            for ref in (dwo_ref, dwa_ref, dwb_ref, dbias_ref):
                ref[...] = jnp.zeros_like(ref)

        dxb = dx_ref[...].astype(BF16)
        dm = _dot_nt(dxb, wo_ref[...])
        sa = _sigmoid(ga_ref[...])
        sb = _sigmoid(gb_ref[...])
        dga = dm * pa_ref[...] * (sa * (1.0 - sa))
        dgb = dm * pb_ref[...] * (sb * (1.0 - sb))
        dg_ref[:, :D] = dga.astype(BF16)
        dg_ref[:, D:] = dgb.astype(BF16)
        dbias_ref[:, :D] += jnp.sum(dga, axis=0, keepdims=True)
        dbias_ref[:, D:] += jnp.sum(dgb, axis=0, keepdims=True)
        dpa = (dm * sa).astype(BF16)
        dpb = (dm * sb).astype(BF16)
        ya = ya_ref[...]
        yb = yb_ref[...]
        dya = jnp.zeros((TM, W), F32)
        dyb = jnp.zeros((TM, W), F32)
        for j in range(4):
            cs = slice(256 * j, 256 * (j + 1))
            dya = dya + _dot_nt(dpa[:, cs], wa_ref[j])
            dyb = dyb + _dot_nt(dpb[:, cs], wb_ref[j])
            dwa_ref[j] += _dot_tn(ya, dpa[:, cs])
            dwb_ref[j] += _dot_tn(yb, dpb[:, cs])
        dya_ref[...] = dya
        dyb_ref[...] = dyb
        dwo_ref[...] += _dot_tn(mg_ref[...], dxb)

    return _pcall(
        body, f"merge_out_bwd_l{l}", (NT,),
        [_rows(D), _rows(D), _rows(D, 2), _rows(D, 3), _rows(D), _rows(D), _rows(W), _rows(W),
         _layer(l, (D, D)), _layer(l, (4, W, 256)), _layer(l, (4, W, 256))],
        [_rows(2 * D, 1), _rows(W), _rows(W), _full((D, D)), _full((4, W, 256)), _full((4, W, 256)), _full((1, 2 * D))],
        [SDS((L, NIN), BF16), SDS((L, W), F32), SDS((L, W), F32),
         SDS((D, D), F32), SDS((4, W, 256), F32), SDS((4, W, 256), F32), SDS((1, 2 * D), F32)],
        (dxn, mg, proj, proj, pa, pb, ya, yb, wo, wa, wb), jobs=jobs)


def _pool_bwd(l, dyb, proj, mixed, pooled, pw, scale):
    def body(dyb_ref, zb_ref, mixed_ref, pooled_ref, pw_ref, sc_ref, db_ref, dpw_ref, dsc_ref, dbias_ref, buf):
        i = pl.program_id(0)
        tile = NT - 1 - i

        @pl.when(i == 0)
        def _():
            dpw_ref[...] = jnp.zeros_like(dpw_ref)
            dsc_ref[...] = jnp.zeros_like(dsc_ref)
            dbias_ref[...] = jnp.zeros_like(dbias_ref)
            buf[TM:TM + 16, :] = jnp.zeros((16, W), F32)

        dyb = dyb_ref[...]
        zb = zb_ref[...]
        mixed = mixed_ref[...]
        sc = sc_ref[...]
        sg = _sigmoid(zb)
        dyb0 = dyb * (zb * sg)
        dzb = dyb * (mixed * sc) * (sg * (1.0 + zb * (1.0 - sg)))
        db_ref[:, W:] = dzb.astype(BF16)
        dbias_ref[:, W:] += jnp.sum(dzb, axis=0, keepdims=True)
        dsc_ref[...] += jnp.sum(dyb0 * mixed, axis=0, keepdims=True)
        dmix = (dyb0 * sc).astype(BF16)
        t = tile * TM + lax.broadcasted_iota(jnp.int32, (TM, 128), 0)
        for gi, win in enumerate(WINS):
            cs = slice(128 * gi, 128 * (gi + 1))
            dpw_ref[gi] += _dot_tn(pooled_ref[:, cs], dmix[:, cs])
            dpool = _dot_nt(dmix[:, cs], pw_ref[gi])
            cnt = jnp.minimum(t + 1, win).astype(F32)
            e = dpool / cnt
            buf[0:TM, cs] = e
            acc = e - dpool
            for k in range(1, win):
                acc = acc + buf[k:k + TM, cs]
            db_ref[:, cs] = acc.astype(BF16)
            dbias_ref[:, cs] += jnp.sum(acc, axis=0, keepdims=True)
        buf[TM:TM + 16, :] = buf[0:16, :]

    return dict(
        body=body, width=2 * W,
        in_specs=[_rows(W, 0, True), _rows(W, 3, True), _rows(W, 0, True), _rows(W, 0, True),
                  _layer(l, (4, 128, 128)), _layer(l, (1, W))],
        out_specs=[_full((4, 128, 128)), _full((1, W)), _full((1, 2 * W))],
        out_shape=[SDS((4, 128, 128), F32), SDS((1, W), F32), SDS((1, 2 * W), F32)],
        args=(dyb, proj, mixed, pooled, pw, scale),
        scratch=[pltpu.VMEM((TM + 16, W), F32)])


def _s5_bwd(l, dya, proj, y1, q, sre, sim, cret, cimnt, bret, bimt, st_re, st_im, cr_re, cr_im, dsk, wglu):
    def halo(i):
        return (jnp.maximum((NT - 1 - i) * (TM // 8) - 1, 0), 0)

    def body(dya_ref, ua_ref, za_ref, y1_ref, q_ref, sre_ref, sim_ref, hre_ref, him_ref,
             cret_ref, cimnt_ref, bret_ref, bimt_ref, st_re_ref, st_im_ref, cr_re_ref, cr_im_ref, d_ref, wg_ref,
             da_ref, dwg_ref, dbg_ref, dd_ref, dcre_ref, dcimn_ref, dbre_ref, dbim_ref, dare_ref, daim_ref, dbias_ref,
             lre, lim, car_ref):
        i = pl.program_id(0)
        tile = NT - 1 - i

        @pl.when(i == 0)
        def _():
            for ref in (dwg_ref, dbg_ref, dd_ref, dcre_ref, dcimn_ref, dbre_ref, dbim_ref, dare_ref, daim_ref, dbias_ref,
                        car_ref):
                ref[...] = jnp.zeros_like(ref)

        u = ua_ref[...]
        za = za_ref[...]
        y1 = y1_ref[...]
        dya = dya_ref[...]
        y2 = _gelu(y1)
        sg = _sigmoid(q_ref[...])
        sgz = _sigmoid(za)
        dy3 = dya * (za * sgz)
        dza = dya * (y2 * sg) * (sgz * (1.0 + za * (1.0 - sgz)))
        da_ref[:, W:] = dza.astype(BF16)
        dbias_ref[:, W:] += jnp.sum(dza, axis=0, keepdims=True)
        dq = dy3 * y2 * (sg * (1.0 - sg))
        dqb = dq.astype(BF16)
        dy2 = dy3 * sg + _dot_nt(dqb, wg_ref[...])
        dwg_ref[...] += _dot_tn(y2.astype(BF16), dqb)
        dbg_ref[...] += jnp.sum(dq, axis=0, keepdims=True)
        dy1 = dy2 * _gelu_grad(y1)
        dd_ref[...] += jnp.sum(dy1 * u, axis=0, keepdims=True)
        dy1b = dy1.astype(BF16)
        ub = u.astype(BF16)
        for k in range(4):
            blk = slice(512 * k, 512 * (k + 1))
            ks = slice(128 * k, 128 * (k + 1))
            lre[:, blk] = _dot(dy1b[:, ks], cret_ref[k])
            lim[:, blk] = _dot(dy1b[:, ks], cimnt_ref[k])
            dcre_ref[k] += _dot_tn(sre_ref[:, blk].astype(BF16), dy1b[:, ks])
            dcimn_ref[k] += _dot_tn(sim_ref[:, blk].astype(BF16), dy1b[:, ks])
        rowid = lax.broadcasted_iota(jnp.int32, (8, 512), 0)
        gate = (tile > 0).astype(F32)

        def d_abar(c, cols, lr, li):
            rows = pl.ds(pl.multiple_of(c * 8, 8), 8)
            prows = pl.ds(pl.multiple_of(jnp.maximum(c - 1, 0) * 8, 8), 8)
            pr = jnp.where(c == 0, hre_ref[7:8, cols] * gate, sre_ref[prows, cols][7:8])
            pi = jnp.where(c == 0, him_ref[7:8, cols] * gate, sim_ref[prows, cols][7:8])
            sr = jnp.where(rowid == 0, pr, pltpu.roll(sre_ref[rows, cols], 1, 0))
            si = jnp.where(rowid == 0, pi, pltpu.roll(sim_ref[rows, cols], 1, 0))
            dare_ref[:, cols] += sr * lr + si * li
            daim_ref[:, cols] += sr * li - si * lr

        carry = _scan_tile(lre, lim, st_re_ref, st_im_ref, cr_re_ref, cr_im_ref, _load_carry(car_ref), True, d_abar)
        _store_carry(car_ref, carry)

        for k in range(4):
            blk = slice(512 * k, 512 * (k + 1))
            ks = slice(128 * k, 128 * (k + 1))
            lrb = lre[:, blk].astype(BF16)
            lib = lim[:, blk].astype(BF16)
            du = dy1[:, ks] * d_ref[:, ks] + _dot(lrb, bret_ref[k]) + _dot(lib, bimt_ref[k])
            da_ref[:, ks] = du.astype(BF16)
            dbias_ref[:, ks] += jnp.sum(du, axis=0, keepdims=True)
            dbre_ref[k] += _dot_tn(ub[:, ks], lrb)
            dbim_ref[k] += _dot_tn(ub[:, ks], lib)

    return dict(
        body=body, width=2 * W,
        in_specs=[_rows(W, 0, True), _rows(W, 0, True), _rows(W, 1, True), _rows(W, 0, True), _rows(W, 0, True),
                  _rows(GP, 0, True), _rows(GP, 0, True),
                  pl.BlockSpec((8, GP), halo), pl.BlockSpec((8, GP), halo),
                  _layer(l, (4, 128, 512)), _layer(l, (4, 128, 512)), _layer(l, (4, 512, 128)), _layer(l, (4, 512, 128)),
                  _layer(l, (24, GP)), _layer(l, (24, GP)), _layer(l, (8, GP)), _layer(l, (8, GP)), _layer(l, (1, W)),
                  _layer(l, (W, W))],
        out_specs=[_full((W, W)), _full((1, W)), _full((1, W)),
                   _full((4, 512, 128)), _full((4, 512, 128)), _full((4, 128, 512)), _full((4, 128, 512)),
                   _full((8, GP)), _full((8, GP)), _full((1, 2 * W))],
        out_shape=[SDS((W, W), F32), SDS((1, W), F32), SDS((1, W), F32),
                   SDS((4, 512, 128), F32), SDS((4, 512, 128), F32), SDS((4, 128, 512), F32), SDS((4, 128, 512), F32),
                   SDS((8, GP), F32), SDS((8, GP), F32), SDS((1, 2 * W), F32)],
        args=(dya, proj, proj, y1, q, sre, sim, sre, sim, cret, cimnt, bret, bimt, st_re, st_im, cr_re, cr_im, dsk,
              wglu),
        scratch=[pltpu.VMEM((TM, GP), F32), pltpu.VMEM((TM, GP), F32), pltpu.VMEM((8, GP), F32)])


def _inproj_dw(l, r, h, dproj, jobs=(), rows=D // 2):
    def body(h_ref, dp_ref, dw_ref):
        part = _dot_tn(h_ref[...], dp_ref[...])

        @pl.when(pl.program_id(1) == 0)
        def _():
            dw_ref[...] = part

        @pl.when(pl.program_id(1) > 0)
        def _():
            dw_ref[...] += part

    return _pcall(
        body, f"inproj_dw{r}_l{l}", (4, L // TK),
        [pl.BlockSpec((TK, rows), lambda j, i: (i, r)), pl.BlockSpec((TK, 1024), lambda j, i: (i, j))],
        [pl.BlockSpec((None, rows, 1024), lambda j, i: (j, 0, 0))],
        [SDS((4, rows, 1024), F32)],
        (h, dproj), jobs=jobs)


def _inproj_dx(l, dproj, w, x, g, dxn, jobs=()):
    def body(dp_ref, w_ref, x_ref, g_ref, dxn_ref, dx_ref, dg_ref):
        @pl.when(pl.program_id(0) == 0)
        def _():
            dg_ref[...] = jnp.zeros_like(dg_ref)

        dh = _dot_nt(dp_ref[:, 0:1024], w_ref[0])
        for j in range(1, 4):
            dh = dh + _dot_nt(dp_ref[:, j * 1024:(j + 1) * 1024], w_ref[j])
        xv = x_ref[...]
        r = lax.rsqrt(jnp.mean(xv * xv, axis=-1, keepdims=True) + EPS)
        xn = xv * r
        dg_ref[...] += jnp.sum(dh * xn, axis=0, keepdims=True)
        dn = dh * g_ref[...]
        dx_ref[...] = dxn_ref[...] + r * (dn - xn * jnp.mean(dn * xn, axis=-1, keepdims=True))

    return _pcall(
        body, f"inproj_dx_l{l}", (L // TMM,),
        [_rows_mm(NIN), _layer(l, (4, D, 1024)), _rows_mm(D), _layer(l, (1, D)), _rows_mm(D)],
        [_rows_mm(D), _full((1, D))],
        [SDS((L, D), F32), SDS((1, D), F32)],
        (dproj, w, x, g, dxn), jobs=jobs)


def _discretize(log_dt, lam_re, lam_im, b_re, b_im):
    dt = jnp.exp(log_dt)[..., None]
    mag = jnp.exp(lam_re * dt)
    ang = lam_im * dt
    abar_re = mag * jnp.cos(ang)
    abar_im = mag * jnp.sin(ang)
    num_re = abar_re - 1.0
    num_im = abar_im
    den = lam_re * lam_re + lam_im * lam_im
    coef_re = (num_re * lam_re + num_im * lam_im) / den
    coef_im = (num_im * lam_re - num_re * lam_im) / den
    bbar_re = coef_re[..., None] * b_re - coef_im[..., None] * b_im
    bbar_im = coef_re[..., None] * b_im + coef_im[..., None] * b_re
    return abar_re, abar_im, bbar_re, bbar_im


def _powers(abar_re, abar_im):
    ar, ai = abar_re.reshape(DEPTH, 1, GP), abar_im.reshape(DEPTH, 1, GP)
    rows_re, rows_im = [ar], [ai]
    for _ in range(7):
        pr, pi = rows_re[-1], rows_im[-1]
        rows_re.append(pr * ar - pi * ai)
        rows_im.append(pr * ai + pi * ar)
    row = jnp.arange(8)[:, None]

    def steps(rows, keep):
        return jnp.concatenate([jnp.where(keep(d), rows[d - 1], 0.0) for d in (1, 2, 4)], axis=1)

    neg_im = [-r for r in rows_im]
    fwd = (steps(rows_re, lambda d: row >= d), steps(rows_im, lambda d: row >= d),
           jnp.concatenate(rows_re, axis=1), jnp.concatenate(rows_im, axis=1))
    rev = (steps(rows_re, lambda d: row < 8 - d), steps(neg_im, lambda d: row < 8 - d),
           jnp.concatenate(rows_re[::-1], axis=1), jnp.concatenate(neg_im[::-1], axis=1))
    return fwd, rev


_EYE8 = functools.partial(jnp.eye, 8, dtype=F32)


def _expand_in(b):
    return jnp.einsum("lkgpc,gh->lkgchp", b.reshape(DEPTH, 4, 8, P, C), _EYE8()).reshape(DEPTH, 4, 128, 512)


def _extract_in(e):
    return jnp.einsum("lkgchp,gh->lkgpc", e.reshape(DEPTH, 4, 8, C, 8, P), _EYE8()).reshape(DEPTH, G, P, C)


def _expand_out(c):
    return jnp.einsum("lkgcp,gh->lkgphc", c.reshape(DEPTH, 4, 8, C, P), _EYE8()).reshape(DEPTH, 4, 512, 128)


def _extract_out(e):
    return jnp.einsum("lkgphc,gh->lkgcp", e.reshape(DEPTH, 4, 8, P, 8, C), _EYE8()).reshape(DEPTH, G, C, P)


SMALL = ("norm_g", "b_in", "ssm_log_dt", "ssm_lam_re", "ssm_lam_im", "ssm_b_re", "ssm_b_im",
         "ssm_c_re", "ssm_c_im", "ssm_d", "ssm_b_glu", "pool_w", "pool_scale")
BIG = ("w_in", "ssm_w_glu", "w_branch_a", "w_branch_b", "w_out")


def _step(x, target, w, m, v, place):
    sp = {n: w[n] for n in SMALL}
    final_norm_g = w["final_norm_g"]
    wbuf = dict(zip(BIG, _cast_own(place, [w[n] for n in BIG])))
    (abar_re, abar_im, bbar_re, bbar_im), disc_vjp = jax.vjp(
        _discretize, *(sp[n] for n in ("ssm_log_dt", "ssm_lam_re", "ssm_lam_im", "ssm_b_re", "ssm_b_im")))
    powers_fwd, powers_rev = _powers(abar_re, abar_im)
    b_re_x, b_im_x = _expand_in(bbar_re), _expand_in(bbar_im)
    c_re_x, c_imn_x = _expand_out(sp["ssm_c_re"]), _expand_out(-sp["ssm_c_im"])
    b_x = jnp.concatenate([b_re_x, b_im_x], axis=3).astype(BF16)
    t = lambda a: jnp.swapaxes(a, 2, 3).astype(BF16)
    c_re_t, c_imn_t, b_re_t, b_im_t = t(c_re_x), t(c_imn_x), t(b_re_x), t(b_im_x)
    c_re_x, c_imn_x = c_re_x.astype(BF16), c_imn_x.astype(BF16)
    row = lambda n: sp[n].reshape(DEPTH, 1, -1)
    g, b_in, dsk, b_glu, scale = row("norm_g"), row("b_in"), row("ssm_d"), row("ssm_b_glu"), row("pool_scale")
    pw = sp["pool_w"].astype(BF16)

    saved = []
    for l in range(DEPTH):
        three = BIG[2:]
        if l == 0:
            wbuf["w_in"], wbuf["ssm_w_glu"] = _comm_only(
                "gather_first", _RingGatherJob([wbuf["w_in"], wbuf["ssm_w_glu"]], 0))[0]
            jobs = [_GatherJob([wbuf[n] for n in three], 0), _GatherJob([wbuf["ssm_w_glu"]], 1)]
        else:
            jobs = []
        (h, proj), res = _norm_inproj(l, x, g, wbuf["w_in"], b_in, jobs)
        if res:
            wbuf.update(zip(three, res[0][0]))
            (wbuf["ssm_w_glu"],) = res[1][0]
        wg = dict(wbuf, ssm_w_glu=wbuf["ssm_w_glu"].reshape(DEPTH, W, W), w_out=wbuf["w_out"].reshape(DEPTH, D, D))
        job = _GatherJob([wbuf["w_in"]], l + 1) if l + 1 < DEPTH else _GatherJob([wbuf[n] for n in three], l)
        (sre, sim, y1, q, ya, pooled, mixed, yb), res = _fused(
            f"branches_fwd_l{l}", (NT,),
            [_s5_fwd(l, proj, b_x, c_re_x, c_imn_x, powers_fwd, dsk, wg["ssm_w_glu"], b_glu),
             _pool_fwd(l, proj, pw, scale)], jobs=[job])
        if l + 1 < DEPTH:
            (wbuf["w_in"],) = res[0][0]
        else:
            wbuf.update(zip(three, res[0][0]))
        wg = dict(wbuf, ssm_w_glu=wbuf["ssm_w_glu"].reshape(DEPTH, W, W), w_out=wbuf["w_out"].reshape(DEPTH, D, D))
        last = l + 1 == DEPTH
        pa, pb, mg, *tail = _merge_out(l, ya, yb, proj, x, wg["w_branch_a"], wg["w_branch_b"], wg["w_out"],
                                       (final_norm_g.reshape(1, D), target) if last else None)
        saved.append(dict(x=x, h=h, proj=proj, sre=sre, sim=sim, y1=y1, q=q, ya=ya,
                          pooled=pooled, mixed=mixed, yb=yb, pa=pa, pb=pb, mg=mg))
        if last:
            loss, dx, dgf = tail
        else:
            (x,) = tail

    per_layer = {n: [None] * DEPTH for n in ("norm_g", "b_in", "ssm_d", "ssm_b_glu", "pool_w", "pool_scale",
                                             "dare", "daim", "dbre", "dbim", "dcre", "dcimn")}
    red = _Reducer(place, w, m, v)
    for l in reversed(range(DEPTH)):
        s = saved[l]
        (dproj, dya, dyb, dwo, dwa, dwb, dbias_g), res = _merge_out_bwd(
            l, dx, s["mg"], s["proj"], s["pa"], s["pb"], s["ya"], s["yb"],
            wg["w_out"], wg["w_branch_a"], wg["w_branch_b"], red.jobs())
        red.land(res)
        (dproj, dwg, dbg, dd, dcre, dcimn, dbre, dbim, dare, daim, dbias_a, dpw, dsc, dbias_b), res = _fused(
            f"branches_bwd_l{l}", (NT,),
            [_s5_bwd(l, dya, s["proj"], s["y1"], s["q"], s["sre"], s["sim"], c_re_t, c_imn_t, b_re_t, b_im_t,
                     *powers_rev, dsk, wg["ssm_w_glu"]),
             _pool_bwd(l, dyb, s["proj"], s["mixed"], s["pooled"], pw, scale)],
            shared=(dproj, _rows(4 * W, 0, True)), jobs=red.jobs())
        red.land(res)
        rest = [dwg.reshape(4, W // 4, W), dwa, dwb, dwo.reshape(4, D // 4, D)]
        red.add(l, "rest", BIG[1:], rest)
        if l == 0:
            for r, rows in ((0, D - LAST_ROWS), (D // LAST_ROWS - 1, LAST_ROWS)):
                outs, res = _inproj_dw(l, r, s["h"], dproj, red.jobs(), rows=rows)
                red.land(res)
                red.add(l, f"in{r}", BIG[:1], outs, [r * rows])
        else:
            dwin, res = _inproj_dw(l, 0, s["h"], dproj, red.jobs(), rows=D)
            red.land(res)
            red.add(l, "in", BIG[:1], dwin)
        (dx, dg), res = _inproj_dx(l, dproj, wg["w_in"], s["x"], g, dx, red.jobs())
        red.land(res)
        for n, a in (("norm_g", dg.reshape(D)), ("b_in", jnp.concatenate([dbias_a, dbias_b, dbias_g], axis=1).reshape(NIN)),
                     ("ssm_d", dd.reshape(W)), ("ssm_b_glu", dbg.reshape(W)), ("pool_w", dpw), ("pool_scale", dsc.reshape(W)),
                     ("dare", dare), ("daim", daim), ("dbre", dbre), ("dbim", dbim), ("dcre", dcre), ("dcimn", dcimn)):
            per_layer[n][l] = a
    gs = {n: jnp.stack(a) for n, a in per_layer.items()}
    d_abar = [jnp.sum(gs.pop(n), axis=1).reshape(DEPTH, G, P) for n in ("dare", "daim")]
    (gs["ssm_log_dt"], gs["ssm_lam_re"], gs["ssm_lam_im"], gs["ssm_b_re"], gs["ssm_b_im"]) = disc_vjp(
        (*d_abar, _extract_in(gs.pop("dbre")), _extract_in(gs.pop("dbim"))))
    gs["ssm_c_re"], gs["ssm_c_im"] = _extract_out(gs.pop("dcre")), -_extract_out(gs.pop("dcimn"))
    gs["final_norm_g"] = dgf

    natural = {n: w[n].shape for n in REPLICATED}
    rw, rm, rv = {}, {}, {}
    for n in REPLICATED:
        shape = DENSE.get(n, natural[n])
        gs[n], rw[n], rm[n], rv[n] = (a.reshape(shape) for a in (gs[n], w[n], m[n], v[n]))
    small = [gs[n] for n in REPLICATED] + [loss]
    jobs = red.jobs()
    res = _pcall(None, "tail_exchange", (), [], [], [], [], jobs=jobs + [_SiblingJob(small, False)])[1]
    red.land(res[:len(jobs)])
    pair_small = _small_pair_sum(place, small, res[-1][1], [BF16 if a.ndim > 2 else F32 for a in small])
    jobs = red.jobs()
    res = _pcall(None, "tail_gather", (), [], [], [], [], jobs=jobs + [_ChipGatherJob(pair_small)])[1]
    red.land(res[:len(jobs)])
    assert not red.active
    small_parts = dict(zip(REPLICATED + ("loss",), res[-1][0]))

    k = len(REPLICATED)
    outs = _adamw_small("adamw_small", [rw[n] for n in REPLICATED], [small_parts[n] for n in REPLICATED],
                        [rm[n] for n in REPLICATED], [rv[n] for n in REPLICATED], small_parts["loss"])
    results = {n: red.big[n] for n in BIG}
    results.update({n: [outs[1 + q * k + i].reshape(natural[n]) for q in range(4)] for i, n in enumerate(REPLICATED)})
    return outs[0][0, 0], dx, results


def _place():
    x, y, c = lax.axis_index("x"), lax.axis_index("y"), lax.axis_index("c")
    chips = [(1 - x, y), (x, 1 - y), (1 - x, 1 - y)]
    return x, y, c, 2 * x + y, chips, [2 * cx + cy for cx, cy in chips]


def _remote(src, dst, ssem, rsem, dev):
    return pltpu.make_async_remote_copy(src_ref=src, dst_ref=dst, send_sem=ssem, recv_sem=rsem,
                                        device_id=dev, device_id_type=MESH)


class _GatherJob:
    def __init__(self, bufs, l):
        self.srcs, self.bufs, self.news, self.l = [], list(bufs), [], l
        self.scratch = [pltpu.SemaphoreType.DMA((len(self.bufs), 3))] * 4

    def _half(self, ref, k, h):
        rows = ref.shape[2] // 2
        return ref.at[self.l, k, pl.ds(pl.multiple_of(h * rows, 8), rows), :]

    def _ici(self, bufs, sems, a, j, k):
        _, _, c, _, chips, _ = _place()
        blk = self._half(bufs[a], k, c)
        return _remote(blk, blk, sems[0].at[a, j], sems[1].at[a, j], (*chips[j], c))

    def _d2d(self, bufs, sems, a, j, k, h):
        x, y, c, _, _, _ = _place()
        blk = self._half(bufs[a], k, h)
        return _remote(blk, blk, sems[2].at[a, j], sems[3].at[a, j], (x, y, 1 - c))

    def start(self, srcs, bufs, news, sems):
        me = _place()[3]
        for a in range(len(self.bufs)):
            for j in range(3):
                self._ici(bufs, sems, a, j, me).start()

    def finish(self, srcs, bufs, news, sems):
        _, _, c, me, _, cid = _place()
        pairs = [(a, j) for a in range(len(self.bufs)) for j in range(3)]
        for a, j in pairs:
            self._ici(bufs, sems, a, j, cid[j]).wait_recv()
            self._d2d(bufs, sems, a, j, cid[j], c).start()
        for a, j in pairs:
            self._d2d(bufs, sems, a, j, cid[j], 1 - c).wait_recv()
        for a, j in pairs:
            self._ici(bufs, sems, a, j, me).wait_send()
            self._d2d(bufs, sems, a, j, cid[j], c).wait_send()


class _RingGatherJob(_GatherJob):
    def __init__(self, bufs, l):
        super().__init__(bufs, l)
        n = len(self.bufs)
        self.scratch = [pltpu.SemaphoreType.DMA((n, 2))] * 4 + [pltpu.SemaphoreType.DMA((n, 4))] * 2

    def _rows(self, ref, k, h, part):
        half = ref.shape[2] // 2
        start, rows = (h * half, half) if part is None else (h * half + part * (half // 2), half // 2)
        return ref.at[self.l, k, pl.ds(pl.multiple_of(start, 8), rows), :]

    def _to_chip(self, bufs, sems, base, a, j, k, part):
        _, _, c, _, chips, _ = _place()
        blk = self._rows(bufs[a], k, c, part)
        return _remote(blk, blk, sems[base].at[a, j], sems[base + 1].at[a, j], (*chips[j], c))

    def _to_sibling(self, bufs, sems, a, i, k, h, part):
        x, y, c, _, _, _ = _place()
        blk = self._rows(bufs[a], k, h, part)
        return _remote(blk, blk, sems[4].at[a, i], sems[5].at[a, i], (x, y, 1 - c))

    def start(self, srcs, bufs, news, sems):
        me = _place()[3]
        for a in range(len(self.bufs)):
            for j in range(2):
                self._to_chip(bufs, sems, 0, a, j, me, None).start()

    def finish(self, srcs, bufs, news, sems):
        _, _, c, me, _, cid = _place()
        arrays = range(len(self.bufs))
        for a in arrays:
            for j in (1, 0):
                self._to_chip(bufs, sems, 0, a, j, cid[j], None).wait_recv()
                self._to_chip(bufs, sems, 2, a, 1 - j, cid[j], 1 - j).start()
                self._to_sibling(bufs, sems, a, j, cid[j], c, None).start()
        for a in arrays:
            for part in range(2):
                self._to_chip(bufs, sems, 2, a, part, cid[2], part).wait_recv()
                self._to_sibling(bufs, sems, a, 2 + part, cid[2], c, part).start()
        for a in arrays:
            for j in range(2):
                self._to_sibling(bufs, sems, a, j, cid[j], 1 - c, None).wait_recv()
                self._to_sibling(bufs, sems, a, 2 + j, cid[2], 1 - c, j).wait_recv()
        for a in arrays:
            for j in range(2):
                self._to_chip(bufs, sems, 0, a, j, me, None).wait_send()
                self._to_chip(bufs, sems, 2, a, 1 - j, cid[j], 1 - j).wait_send()
                self._to_sibling(bufs, sems, a, j, cid[j], c, None).wait_send()
                self._to_sibling(bufs, sems, a, 2 + j, cid[2], c, j).wait_send()


class _SiblingJob:
    def __init__(self, srcs, rows_half):
        self.srcs, self.bufs, self.rows_half = list(srcs), [], rows_half
        self.news = [SDS((s.shape[0], s.shape[1] // 2, s.shape[2]) if rows_half else s.shape, s.dtype) for s in srcs]
        self.scratch = [pltpu.SemaphoreType.DMA((len(self.srcs),))] * 2

    def _copy(self, srcs, news, sems, a):
        x, y, c, _, _, _ = _place()
        src = srcs[a]
        if self.rows_half:
            rows = src.shape[1] // 2
            src = src.at[:, pl.ds(pl.multiple_of((1 - c) * rows, 8), rows), :]
        return _remote(src, news[a], sems[0].at[a], sems[1].at[a], (x, y, 1 - c))

    def start(self, srcs, bufs, news, sems):
        for a in range(len(self.srcs)):
            self._copy(srcs, news, sems, a).start()

    def finish(self, srcs, bufs, news, sems):
        for a in range(len(self.srcs)):
            self._copy(srcs, news, sems, a).wait()


class _ScatterJob:
    def __init__(self, parts):
        self.srcs, self.bufs = list(parts), []
        self.news = [SDS((3,) + p.shape[1:], p.dtype) for p in parts]
        self.scratch = [pltpu.SemaphoreType.DMA((len(self.srcs), 3))] * 2

    def _copy(self, srcs, news, sems, a, j):
        _, _, c, _, chips, cid = _place()
        return _remote(srcs[a].at[cid[j]], news[a].at[j], sems[0].at[a, j], sems[1].at[a, j], (*chips[j], c))

    def start(self, srcs, bufs, news, sems):
        for a in range(len(self.srcs)):
            for j in range(3):
                self._copy(srcs, news, sems, a, j).start()

    def finish(self, srcs, bufs, news, sems):
        for a in range(len(self.srcs)):
            for j in range(3):
                self._copy(srcs, news, sems, a, j).wait()


def _comm_only(name, job):
    return _pcall(None, name, (), [], [], [], [], jobs=[job])[1][0]


class _ChipGatherJob(_GatherJob):
    def __init__(self, bufs):
        super().__init__(bufs, None)

    def _half(self, ref, k, h):
        return ref.at[k, h]


def _cast_own(place, ws):
    n = len(ws)

    def body(p_ref, *refs):
        for i_ref, o_ref in zip(refs[:n], refs[n:]):
            o_ref[...] = i_ref[...].astype(BF16)

    return pl.pallas_call(
        body, name="cast_own_shards",
        grid_spec=pltpu.PrefetchScalarGridSpec(
            num_scalar_prefetch=1, grid=(DEPTH,),
            in_specs=[pl.BlockSpec((None,) + a.shape[1:], lambda l, p: (l, 0, 0)) for a in ws],
            out_specs=[pl.BlockSpec((None, None) + a.shape[1:], lambda l, p: (l, p[1], 0, 0)) for a in ws]),
        out_shape=[SDS((DEPTH, 4) + a.shape[1:], BF16) for a in ws],
        compiler_params=_params("arbitrary"),
    )(place, *ws)


def _half_tiles(a_):
    rows = a_ // 2
    ta = next(t for t in (256, 128, 64, 32, 16, 8) if rows % t == 0)
    return rows, ta, rows // ta


def _pair_sums_bf16(name, place, owns, recvs):
    n = len(owns)

    def body(p_ref, *refs):
        for own_ref, recv_ref, out_ref in zip(refs[:n], refs[n:2 * n], refs[2 * n:]):
            out_ref[...] = (own_ref[...] + recv_ref[...]).astype(BF16)

    def own_half(a):
        return pl.BlockSpec((None, a.shape[1] // 2, a.shape[2]), lambda s, p: (s, p[0], 0))

    def block(a):
        return pl.BlockSpec((None,) + a.shape[1:], lambda s, p: (s, 0, 0))

    return pl.pallas_call(
        body, name=name,
        grid_spec=pltpu.PrefetchScalarGridSpec(
            num_scalar_prefetch=1, grid=(4,),
            in_specs=[own_half(a) for a in owns] + [block(r) for r in recvs],
            out_specs=[block(r) for r in recvs]),
        out_shape=[SDS(r.shape, BF16) for r in recvs],
        compiler_params=_params("arbitrary"),
    )(place, *owns, *recvs)


def _shard_sums(name, place, owns, recvs, rbufs):
    n = len(owns)

    def body(p_ref, *refs):
        for own_ref, recv_ref, r_ref, out_ref in zip(refs[:n], refs[n:2 * n], refs[2 * n:3 * n], refs[3 * n:]):
            acc = own_ref[...] + recv_ref[...]
            for j in range(3):
                acc = acc + r_ref[j].astype(F32)
            out_ref[...] = acc

    steps = 2

    def own_half(a):
        return pl.BlockSpec((None, a.shape[1] // 2 // steps, a.shape[2]), lambda i, p: (p[1], steps * p[0] + i, 0))

    def recv_block(a):
        return pl.BlockSpec((None, a.shape[1] // steps, a.shape[2]), lambda i, p: (p[1], i, 0))

    return pl.pallas_call(
        body, name=name,
        grid_spec=pltpu.PrefetchScalarGridSpec(
            num_scalar_prefetch=1, grid=(steps,),
            in_specs=([own_half(a) for a in owns] + [recv_block(r) for r in recvs]
                      + [pl.BlockSpec((3, rb.shape[1] // steps, rb.shape[2]), lambda i, p: (0, i, 0)) for rb in rbufs]),
            out_specs=[pl.BlockSpec((r.shape[1] // steps, r.shape[2]), lambda i, p: (i, 0)) for r in recvs]),
        out_shape=[SDS(r.shape[1:], F32) for r in recvs],
        compiler_params=_params("arbitrary"),
    )(place, *owns, *recvs, *rbufs)


def _small_pair_sum(place, mine, recv, dtypes):
    n = len(mine)

    def body(p_ref, *refs):
        for m_ref, r_ref, o_ref in zip(refs[:n], refs[n:2 * n], refs[2 * n:]):
            o_ref[...] = (m_ref[...] + r_ref[...]).astype(o_ref.dtype)

    def whole(a):
        zeros = (0,) * a.ndim
        return pl.BlockSpec(a.shape, lambda i, p: zeros)

    def mine_blk(a):
        zeros = (0,) * a.ndim
        return pl.BlockSpec((None,) + a.shape, lambda i, p: (p[1],) + zeros)

    return pl.pallas_call(
        body, name="small_pair_sum",
        grid_spec=pltpu.PrefetchScalarGridSpec(
            num_scalar_prefetch=1, grid=(1,),
            in_specs=[whole(a) for a in mine] + [whole(a) for a in recv],
            out_specs=[mine_blk(a) for a in mine]),
        out_shape=[SDS((4,) + a.shape, dt) for a, dt in zip(mine, dtypes)],
        compiler_params=_params("arbitrary"),
    )(place, *mine, *recv)


def _adam_math(w, g, m, v):
    m = B1 * m + (1.0 - B1) * g
    v = B2 * v + (1.0 - B2) * (g * g)
    m_hat = m / (1.0 - B1 ** STEP)
    v_hat = v / (1.0 - B2 ** STEP)
    delta = -LR * (m_hat / (jnp.sqrt(v_hat) + EPS_A) + WD * w)
    return delta, m, v


def _adamw_big(name, l, row0, w, m, v, mine, other, prev, jobs=()):
    _, _, b_ = w.shape
    _, ta, nh = _half_tiles(2 * mine.shape[0])
    prev = list(prev or [])

    def body(w_ref, m_ref, v_ref, mine_ref, other_ref, *rest):
        g_ref, d_ref, mo_ref, vo_ref = rest[len(prev):]
        g = jnp.where(pl.program_id(0) == lax.axis_index("c"), mine_ref[...], other_ref[...])
        g_ref[...] = g
        d_ref[...], mo_ref[...], vo_ref[...] = _adam_math(w_ref[...], g, m_ref[...], v_ref[...])

    slab = pl.BlockSpec((None, ta, b_), lambda h, i: (l, row0 // ta + h * nh + i, 0))
    half = pl.BlockSpec((ta, b_), lambda h, i: (i, 0))
    outs, res = _pcall(
        body, name, (2, nh), [slab, slab, slab, half, half] + [_ANY] * len(prev), [slab] * 4, [SDS(w.shape, F32)] * 4,
        (w, m, v, mine, other, *prev), aliases={5 + k: k for k in range(len(prev))}, jobs=jobs)
    return outs, res


def _adamw_small(name, ws, parts, ms, vs, loss_parts=None):
    k = len(ws)
    extra = [] if loss_parts is None else [loss_parts]

    def chip_sum(p_ref):
        p = [p_ref[k].astype(F32) for k in range(4)]
        return ((p[0] + p[1]) + p[2]) + p[3]

    def body(*refs):
        w_refs, p_refs, m_refs, v_refs = refs[:k], refs[k:2 * k], refs[2 * k:3 * k], refs[3 * k:4 * k]
        outs = refs[4 * k + len(extra):]
        if extra:
            outs[0][...] = chip_sum(refs[4 * k])
            outs = outs[1:]
        for a in range(k):
            g = chip_sum(p_refs[a])
            outs[a][...] = g
            outs[k + a][...], outs[2 * k + a][...], outs[3 * k + a][...] = _adam_math(
                w_refs[a][...], g, m_refs[a][...], v_refs[a][...])

    like = [SDS(a.shape, F32) for a in ws]
    return pl.pallas_call(
        body, name=name,
        out_shape=([SDS(loss_parts.shape[1:], F32)] if extra else []) + like * 4,
        compiler_params=pltpu.CompilerParams(vmem_limit_bytes=VMEM_LIMIT),
    )(*ws, *parts, *ms, *vs, *extra)


class _Reducer:
    def __init__(self, place, w, m, v):
        self.place, self.w, self.m, self.v = place, w, m, v
        self.active, self.riding = [], []
        self.big = {n: None for n in BIG}

    def add(self, l, tag, names, own, row0s=None):
        self.active.append(dict(l=l, key=f"{tag}_l{l}", names=names, own=list(own), row0s=row0s or [0] * len(names),
                                stage=0))

    def jobs(self):
        self.riding = list(self.active)
        return [(_SiblingJob(g["own"], True), _ScatterJob(g.get("parts", [])), _SiblingJob(g.get("shard", []), False))
                [g["stage"]] for g in self.riding]

    def land(self, res):
        for g, (_, news) in zip(self.riding, res):
            if g["stage"] == 0:
                g["recv"] = news
                g["parts"] = _pair_sums_bf16(f"pair_sums_{g['key']}", self.place, g["own"], news)
            elif g["stage"] == 1:
                g["shard"] = _shard_sums(f"shard_sums_{g['key']}", self.place, g["own"], g["recv"], news)
            else:
                for n, mine, other, row0 in zip(g["names"], g["shard"], news, g["row0s"]):
                    self.big[n] = _adamw_big(f"adamw_{n}_{row0}_{g['key']}", g["l"], row0, self.w[n], self.m[n], self.v[n],
                                             mine, other, self.big[n])[0]
                self.active.remove(g)
            g["stage"] += 1
        self.riding = []


WEIGHTS = ("norm_g", "w_in", "b_in", "ssm_log_dt", "ssm_lam_re", "ssm_lam_im", "ssm_b_re", "ssm_b_im", "ssm_c_re",
           "ssm_c_im", "ssm_d", "ssm_w_glu", "ssm_b_glu", "pool_w", "pool_scale", "w_branch_a", "w_branch_b", "w_out",
           "final_norm_g")
REPLICATED = SMALL + ("final_norm_g",)
DENSE = {"ssm_b_re": (DEPTH, G, P * C), "ssm_b_im": (DEPTH, G, P * C), "final_norm_g": (2, D // 2)}


def kernel(x, norm_g, w_in, b_in, ssm_log_dt, ssm_lam_re, ssm_lam_im, ssm_b_re, ssm_b_im, ssm_c_re, ssm_c_im, ssm_d, ssm_w_glu, ssm_b_glu, pool_w, pool_scale, w_branch_a, w_branch_b, w_out, final_norm_g, loss_target, m_norm_g, m_w_in, m_b_in, m_ssm_log_dt, m_ssm_lam_re, m_ssm_lam_im, m_ssm_b_re, m_ssm_b_im, m_ssm_c_re, m_ssm_c_im, m_ssm_d, m_ssm_w_glu, m_ssm_b_glu, m_pool_w, m_pool_scale, m_w_branch_a, m_w_branch_b, m_w_out, m_final_norm_g, v_norm_g, v_w_in, v_b_in, v_ssm_log_dt, v_ssm_lam_re, v_ssm_lam_im, v_ssm_b_re, v_ssm_b_im, v_ssm_c_re, v_ssm_c_im, v_ssm_d, v_ssm_w_glu, v_ssm_b_glu, v_pool_w, v_pool_scale, v_w_branch_a, v_w_branch_b, v_w_out, v_final_norm_g):
    w = dict(zip(WEIGHTS, (norm_g, w_in, b_in, ssm_log_dt, ssm_lam_re, ssm_lam_im, ssm_b_re, ssm_b_im, ssm_c_re,
                           ssm_c_im, ssm_d, ssm_w_glu, ssm_b_glu, pool_w, pool_scale, w_branch_a, w_branch_b, w_out,
                           final_norm_g)))
    m = dict(zip(WEIGHTS, (m_norm_g, m_w_in, m_b_in, m_ssm_log_dt, m_ssm_lam_re, m_ssm_lam_im, m_ssm_b_re, m_ssm_b_im,
                           m_ssm_c_re, m_ssm_c_im, m_ssm_d, m_ssm_w_glu, m_ssm_b_glu, m_pool_w, m_pool_scale,
                           m_w_branch_a, m_w_branch_b, m_w_out, m_final_norm_g)))
    v = dict(zip(WEIGHTS, (v_norm_g, v_w_in, v_b_in, v_ssm_log_dt, v_ssm_lam_re, v_ssm_lam_im, v_ssm_b_re, v_ssm_b_im,
                           v_ssm_c_re, v_ssm_c_im, v_ssm_d, v_ssm_w_glu, v_ssm_b_glu, v_pool_w, v_pool_scale,
                           v_w_branch_a, v_w_branch_b, v_w_out, v_final_norm_g)))
    place = jnp.stack([lax.axis_index("c"), 2 * lax.axis_index("x") + lax.axis_index("y")]).astype(jnp.int32)

    total_loss, dx, results = _step(x[0], loss_target[0], w, m, v, place)
    return (total_loss, dx[None], *[results[n][q] for q in range(4) for n in WEIGHTS])
```

```python
import functools

import jax
import jax.numpy as jnp
from jax import lax
from jax.experimental import pallas as pl
from jax.experimental.pallas import tpu as pltpu

F32, BF16 = jnp.float32, jnp.bfloat16
SDS = jax.ShapeDtypeStruct
MESH = pl.DeviceIdType.MESH

DEPTH = 2
L = 2048
D = 1024
NIN = 4096
W = 512
G, P, C = 32, 64, 16
GP = G * P
WINS = (2, 4, 8, 16)
TM = 256
NT = L // TM
TMM = 512
TK = 2048
LAST_ROWS = 512
EPS = 1e-6
VMEM_LIMIT = 56 * 2**20

LR, B1, B2, EPS_A, WD, STEP = 0.001, 0.9, 0.999, 1e-08, 0.01, 10


def _params(*sem):
    return pltpu.CompilerParams(dimension_semantics=sem, vmem_limit_bytes=VMEM_LIMIT)


_ANY = pl.BlockSpec(memory_space=pl.ANY)


def _full(shape):
    zeros = (0,) * len(shape)
    return pl.BlockSpec(shape, lambda *_: zeros)


def _layer(l, shape):
    zeros = (0,) * len(shape)
    return pl.BlockSpec((None,) + shape, lambda *_: (l,) + zeros)


def _rows(width, col=0, reverse=False, tm=TM):
    if reverse:
        return pl.BlockSpec((tm, width), lambda i: (L // tm - 1 - i, col))
    return pl.BlockSpec((tm, width), lambda i: (i, col))


def _rows_mm(width, col=0):
    return _rows(width, col, False, TMM)


def _pcall(body, name, grid, in_specs, out_specs, out_shape, args, scratch=(), aliases=None, jobs=()):
    in_specs, out_specs, out_shape, args, scratch = list(in_specs), list(out_specs), list(out_shape), list(args), list(scratch)
    aliases = dict(aliases or {})
    jobs = [j for j in jobs if j is not None]
    n_in, n_out, n_scr = len(in_specs), len(out_specs), len(scratch)
    srcs = [s for j in jobs for s in j.srcs]
    bufs = [b for j in jobs for b in j.bufs]
    news = [s for j in jobs for s in j.news]
    aliases.update({n_in + len(srcs) + k: n_out + k for k in range(len(bufs))})

    def hosted(*refs):
        cuts = [n_in, len(srcs), len(bufs), n_out, len(bufs), len(news), n_scr]
        parts, p = [], 0
        for n in cuts:
            parts.append(refs[p:p + n])
            p += n
        ins, src_r, _, outs, buf_r, new_r, scr = parts
        sem_r = refs[p:]
        views, ps, pb, pn, pm = [], 0, 0, 0, 0
        for j in jobs:
            views.append((src_r[ps:ps + len(j.srcs)], buf_r[pb:pb + len(j.bufs)], new_r[pn:pn + len(j.news)],
                          sem_r[pm:pm + len(j.scratch)]))
            ps, pb, pn, pm = ps + len(j.srcs), pb + len(j.bufs), pn + len(j.news), pm + len(j.scratch)

        def run(phase):
            for j, v in zip(jobs, views):
                getattr(j, phase)(*v)

        def at_step(step):
            return functools.reduce(jnp.logical_and, [pl.program_id(d) == step(d) for d in range(len(grid))])

        if not grid:
            run("start")
            run("finish")
            return
        pl.when(at_step(lambda d: 0))(lambda: run("start"))
        body(*ins, *outs, *scr)
        pl.when(at_step(lambda d: grid[d] - 1))(lambda: run("finish"))

    outs = pl.pallas_call(
        hosted if jobs else body, name=name, **({"grid": grid} if grid else {}),
        in_specs=in_specs + [_ANY] * (len(srcs) + len(bufs)), out_specs=out_specs + [_ANY] * (len(bufs) + len(news)),
        out_shape=out_shape + [SDS(b.shape, b.dtype) for b in bufs] + news,
        input_output_aliases=aliases, scratch_shapes=scratch + [s for j in jobs for s in j.scratch],
        compiler_params=_params(*(("arbitrary",) * len(grid))))(*args, *srcs, *bufs)
    res, pb, pn = [], n_out, n_out + len(bufs)
    for j in jobs:
        res.append((list(outs[pb:pb + len(j.bufs)]), list(outs[pn:pn + len(j.news)])))
        pb, pn = pb + len(j.bufs), pn + len(j.news)
    return list(outs[:n_out]), res


def _fused(name, grid, parts, shared=None, jobs=()):
    def body(*refs):
        pos = [0]

        def take(n):
            pos[0] += n
            return refs[pos[0] - n:pos[0]]

        ins = [take(len(p["in_specs"])) for p in parts]
        if shared:
            take(1)
            block = take(1)[0]
        outs = [take(len(p["out_specs"])) for p in parts]
        scr = [take(len(p["scratch"])) for p in parts]
        col = 0
        for p, i, o, s in zip(parts, ins, outs, scr):
            view = []
            if shared:
                view = [block.at[:, pl.ds(col, p["width"])]]
                col += p["width"]
            p["body"](*i, *view, *o, *s)

    in_specs = [s for p in parts for s in p["in_specs"]] + ([_ANY] if shared else [])
    out_specs = ([shared[1]] if shared else []) + [s for p in parts for s in p["out_specs"]]
    out_shape = ([SDS(shared[0].shape, shared[0].dtype)] if shared else []) + [s for p in parts for s in p["out_shape"]]
    args = [a for p in parts for a in p["args"]] + ([shared[0]] if shared else [])
    return _pcall(body, name, grid, in_specs, out_specs, out_shape, args, [s for p in parts for s in p["scratch"]],
                  {len(in_specs) - 1: 0} if shared else None, jobs)


def _dot(a, b):
    return jnp.dot(a, b, preferred_element_type=F32)


def _dot_nt(a, b):
    return lax.dot_general(a, b, (((1,), (1,)), ((), ())), preferred_element_type=F32)


def _dot_tn(a, b):
    return lax.dot_general(a, b, (((0,), (0,)), ((), ())), preferred_element_type=F32)


_K0 = 0.7978845608028654
_K1 = 0.044715


def _gelu(x):
    return 0.5 * x * (1.0 + jnp.tanh(_K0 * (x + _K1 * (x * x * x))))


def _gelu_grad(x):
    t = jnp.tanh(_K0 * (x + _K1 * (x * x * x)))
    return 0.5 * (1.0 + t) + 0.5 * x * (1.0 - t * t) * (_K0 * (1.0 + 3.0 * _K1 * x * x))


def _sigmoid(x):
    return jax.nn.sigmoid(x)


def _norm_inproj(l, x, g, w, b, jobs=()):
    def body(x_ref, g_ref, w_ref, b_ref, h_ref, proj_ref):
        xv = x_ref[...]
        r = lax.rsqrt(jnp.mean(xv * xv, axis=-1, keepdims=True) + EPS)
        hb = ((xv * r) * g_ref[...]).astype(BF16)
        h_ref[...] = hb
        for j in range(4):
            cs = slice(j * 1024, (j + 1) * 1024)
            proj_ref[:, cs] = _dot(hb, w_ref[j]) + b_ref[:, cs]

    return _pcall(
        body, f"norm_inproj_l{l}", (L // TMM,),
        [_rows_mm(D), _layer(l, (1, D)), _layer(l, (4, D, 1024)), _layer(l, (1, NIN))],
        [_rows_mm(D), _rows_mm(NIN)],
        [SDS((L, D), BF16), SDS((L, NIN), F32)],
        (x, g, w, b), jobs=jobs)


def _scan_tile(re_ref, im_ref, st_re, st_im, cr_re, cr_im, carry, reverse, each_chunk=None):
    def chunk(ci, carry):
        c = (TM // 8 - 1 - ci) if reverse else ci
        rows = pl.ds(pl.multiple_of(c * 8, 8), 8)
        new = []
        for lb in range(GP // 512):
            cols = slice(lb * 512, (lb + 1) * 512)
            vr = re_ref[rows, cols]
            vi = im_ref[rows, cols]
            for s, d in enumerate((1, 2, 4)):
                ar = st_re[8 * s:8 * s + 8, cols]
                ai = st_im[8 * s:8 * s + 8, cols]
                sr = pltpu.roll(vr, 8 - d if reverse else d, 0)
                si = pltpu.roll(vi, 8 - d if reverse else d, 0)
                vr, vi = vr + ar * sr - ai * si, vi + ar * si + ai * sr
            cr, ci_ = carry[2 * lb], carry[2 * lb + 1]
            pr = cr_re[:, cols]
            pi = cr_im[:, cols]
            vr, vi = vr + pr * cr - pi * ci_, vi + pr * ci_ + pi * cr
            re_ref[rows, cols] = vr
            im_ref[rows, cols] = vi
            if each_chunk is not None:
                each_chunk(c, cols, vr, vi)
            if reverse:
                new += [vr[0:1], vi[0:1]]
            else:
                new += [vr[7:8], vi[7:8]]
        return tuple(new)

    return lax.fori_loop(0, TM // 8, chunk, carry)


def _load_carry(car_ref):
    return tuple(car_ref[r:r + 1, lb * 512:(lb + 1) * 512] for lb in range(GP // 512) for r in (0, 1))


def _store_carry(car_ref, carry):
    for lb in range(GP // 512):
        car_ref[0:1, lb * 512:(lb + 1) * 512] = carry[2 * lb]
        car_ref[1:2, lb * 512:(lb + 1) * 512] = carry[2 * lb + 1]


def _s5_fwd(l, proj, bexp, cre, cimn, powers, dsk, wglu, bglu):
    def body(ua_ref, za_ref, bexp_ref, cre_ref, cimn_ref, st_re_ref, st_im_ref, cr_re_ref, cr_im_ref,
             d_ref, wg_ref, bg_ref, sre_ref, sim_ref, y1_ref, q_ref, ya_ref, car_ref):
        @pl.when(pl.program_id(0) == 0)
        def _():
            car_ref[...] = jnp.zeros_like(car_ref)

        u = ua_ref[...]
        ub = u.astype(BF16)
        for k in range(4):
            bu = _dot(ub[:, 128 * k:128 * (k + 1)], bexp_ref[k])
            sre_ref[:, 512 * k:512 * (k + 1)] = bu[:, :512]
            sim_ref[:, 512 * k:512 * (k + 1)] = bu[:, 512:]
        carry = _scan_tile(sre_ref, sim_ref, st_re_ref, st_im_ref, cr_re_ref, cr_im_ref, _load_carry(car_ref), False)
        _store_carry(car_ref, carry)
        for k in range(4):
            blk = slice(512 * k, 512 * (k + 1))
            ks = slice(128 * k, 128 * (k + 1))
            y0 = _dot(sre_ref[:, blk].astype(BF16), cre_ref[k]) + _dot(sim_ref[:, blk].astype(BF16), cimn_ref[k])
            y1_ref[:, ks] = y0 + d_ref[:, ks] * u[:, ks]
        y2 = _gelu(y1_ref[...])
        q = _dot(y2.astype(BF16), wg_ref[...]) + bg_ref[...]
        q_ref[...] = q
        za = za_ref[...]
        ya_ref[...] = ((y2 * _sigmoid(q)) * (za * _sigmoid(za))).astype(BF16)

    return dict(
        body=body,
        in_specs=[_rows(W, 0), _rows(W, 1), _layer(l, (4, 128, 1024)), _layer(l, (4, 512, 128)),
                  _layer(l, (4, 512, 128)), _layer(l, (24, GP)), _layer(l, (24, GP)), _layer(l, (8, GP)),
                  _layer(l, (8, GP)), _layer(l, (1, W)), _layer(l, (W, W)), _layer(l, (1, W))],
        out_specs=[_rows(GP), _rows(GP), _rows(W), _rows(W), _rows(W)],
        out_shape=[SDS((L, GP), F32), SDS((L, GP), F32), SDS((L, W), F32), SDS((L, W), F32), SDS((L, W), BF16)],
        args=(proj, proj, bexp, cre, cimn, *powers, dsk, wglu, bglu),
        scratch=[pltpu.VMEM((8, GP), F32)])


def _pool_fwd(l, proj, pw, scale):
    def body(ub_ref, zb_ref, pw_ref, sc_ref, pooled_ref, mixed_ref, yb_ref, buf):
        i = pl.program_id(0)

        @pl.when(i == 0)
        def _():
            buf[0:16, :] = jnp.zeros((16, W), F32)

        u = ub_ref[...]
        buf[16:16 + TM, :] = u
        t = i * TM + lax.broadcasted_iota(jnp.int32, (TM, 128), 0)
        for gi, win in enumerate(WINS):
            cs = slice(128 * gi, 128 * (gi + 1))
            acc = u[:, cs]
            for k in range(1, win):
                acc = acc + buf[16 - k:16 - k + TM, cs]
            cnt = jnp.minimum(t + 1, win).astype(F32)
            pb = (acc / cnt - u[:, cs]).astype(BF16)
            pooled_ref[:, cs] = pb
            mixed_ref[:, cs] = _dot(pb, pw_ref[gi])
        zb = zb_ref[...]
        yb_ref[...] = ((mixed_ref[...] * sc_ref[...]) * (zb * _sigmoid(zb))).astype(BF16)
        buf[0:16, :] = buf[TM:TM + 16, :]

    return dict(
        body=body,
        in_specs=[_rows(W, 2), _rows(W, 3), _layer(l, (4, 128, 128)), _layer(l, (1, W))],
        out_specs=[_rows(W), _rows(W), _rows(W)],
        out_shape=[SDS((L, W), BF16), SDS((L, W), F32), SDS((L, W), BF16)],
        args=(proj, proj, pw, scale),
        scratch=[pltpu.VMEM((TM + 16, W), F32)])


def _merge_out(l, ya, yb, proj, x, wa, wb, wo, head=None):
    def body(ya_ref, yb_ref, ga_ref, gb_ref, x_ref, wa_ref, wb_ref, wo_ref, *rest):
        pa_ref, pb_ref, mg_ref = rest[2:5] if head else rest[0:3]
        ya = ya_ref[...]
        yb = yb_ref[...]
        for j in range(4):
            cs = slice(256 * j, 256 * (j + 1))
            pa_ref[:, cs] = _dot(ya, wa_ref[j])
            pb_ref[:, cs] = _dot(yb, wb_ref[j])
        merged = _sigmoid(ga_ref[...]) * pa_ref[...] + _sigmoid(gb_ref[...]) * pb_ref[...]
        mb = merged.astype(BF16)
        mg_ref[...] = mb
        x_next = x_ref[...] + _dot(mb, wo_ref[...])
        if head:
            _loss_tile(x_next, rest[0], rest[1], *rest[5:])
        else:
            rest[3][...] = x_next

    rows, out_rows = _rows_mm(D), SDS((L, D), F32)
    return pl.pallas_call(
        body, name=f"merge_out_l{l}", grid=(L // TMM,),
        in_specs=[_rows_mm(W), _rows_mm(W), _rows_mm(D, 2), _rows_mm(D, 3), rows,
                  _layer(l, (4, W, 256)), _layer(l, (4, W, 256)), _layer(l, (D, D))] + ([_full((1, D)), rows] if head else []),
        out_specs=[rows, rows, rows] + ([_full((2, 128)), rows, _full((1, D))] if head else [rows]),
        out_shape=[out_rows, out_rows, SDS((L, D), BF16)]
        + ([SDS((2, 128), F32), out_rows, SDS((1, D), F32)] if head else [out_rows]),
        compiler_params=_params("arbitrary"),
    )(ya, yb, proj, proj, x, wa, wb, wo, *(head or ()))


def _loss_tile(xv, g_ref, t_ref, loss_ref, dx_ref, dg_ref):
    @pl.when(pl.program_id(0) == 0)
    def _():
        loss_ref[...] = jnp.zeros_like(loss_ref)
        dg_ref[...] = jnp.zeros_like(dg_ref)

    g = g_ref[...]
    r = lax.rsqrt(jnp.mean(xv * xv, axis=-1, keepdims=True) + EPS)
    xn = xv * r
    err = xn * g - t_ref[...]
    part = jnp.sum(jnp.mean(err * err, axis=-1, keepdims=True), axis=0, keepdims=True)
    loss_ref[...] += 0.5 * part
    dy = err * (1.0 / D)
    dg_ref[...] += jnp.sum(dy * xn, axis=0, keepdims=True)
    dxn = dy * g
    dx_ref[...] = r * (dxn - xn * jnp.mean(dxn * xn, axis=-1, keepdims=True))


def _merge_out_bwd(l, dxn, mg, proj, pa, pb, ya, yb, wo, wa, wb, jobs=()):
    def body(dx_ref, mg_ref, ga_ref, gb_ref, pa_ref, pb_ref, ya_ref, yb_ref, wo_ref, wa_ref, wb_ref,
             dg_ref, dya_ref, dyb_ref, dwo_ref, dwa_ref, dwb_ref, dbias_ref):
        @pl.when(pl.program_id(0) == 0)
        def _():
            for ref in (dwo_ref, dwa_ref, dwb_ref, dbias_ref):
                ref[...] = jnp.zeros_like(ref)

        dxb = dx_ref[...].astype(BF16)
        dm = _dot_nt(dxb, wo_ref[...])
        sa = _sigmoid(ga_ref[...])
        sb = _sigmoid(gb_ref[...])
        dga = dm * pa_ref[...] * (sa * (1.0 - sa))
        dgb = dm * pb_ref[...] * (sb * (1.0 - sb))
        dg_ref[:, :D] = dga.astype(BF16)
        dg_ref[:, D:] = dgb.astype(BF16)
        dbias_ref[:, :D] += jnp.sum(dga, axis=0, keepdims=True)
        dbias_ref[:, D:] += jnp.sum(dgb, axis=0, keepdims=True)
        dpa = (dm * sa).astype(BF16)
        dpb = (dm * sb).astype(BF16)
        ya = ya_ref[...]
        yb = yb_ref[...]
        dya = jnp.zeros((TM, W), F32)
        dyb = jnp.zeros((TM, W), F32)
        for j in range(4):
            cs = slice(256 * j, 256 * (j + 1))
            dya = dya + _dot_nt(dpa[:, cs], wa_ref[j])
            dyb = dyb + _dot_nt(dpb[:, cs], wb_ref[j])
            dwa_ref[j] += _dot_tn(ya, dpa[:, cs])
            dwb_ref[j] += _dot_tn(yb, dpb[:, cs])
        dya_ref[...] = dya
        dyb_ref[...] = dyb
        dwo_ref[...] += _dot_tn(mg_ref[...], dxb)

    return _pcall(
        body, f"merge_out_bwd_l{l}", (NT,),
        [_rows(D), _rows(D), _rows(D, 2), _rows(D, 3), _rows(D), _rows(D), _rows(W), _rows(W),
         _layer(l, (D, D)), _layer(l, (4, W, 256)), _layer(l, (4, W, 256))],
        [_rows(2 * D, 1), _rows(W), _rows(W), _full((D, D)), _full((4, W, 256)), _full((4, W, 256)), _full((1, 2 * D))],
        [SDS((L, NIN), BF16), SDS((L, W), F32), SDS((L, W), F32),
         SDS((D, D), F32), SDS((4, W, 256), F32), SDS((4, W, 256), F32), SDS((1, 2 * D), F32)],
        (dxn, mg, proj, proj, pa, pb, ya, yb, wo, wa, wb), jobs=jobs)


def _pool_bwd(l, dyb, proj, mixed, pooled, pw, scale):
    def body(dyb_ref, zb_ref, mixed_ref, pooled_ref, pw_ref, sc_ref, db_ref, dpw_ref, dsc_ref, dbias_ref, buf):
        i = pl.program_id(0)
        tile = NT - 1 - i

        @pl.when(i == 0)
        def _():
            dpw_ref[...] = jnp.zeros_like(dpw_ref)
            dsc_ref[...] = jnp.zeros_like(dsc_ref)
            dbias_ref[...] = jnp.zeros_like(dbias_ref)
            buf[TM:TM + 16, :] = jnp.zeros((16, W), F32)

        dyb = dyb_ref[...]
        zb = zb_ref[...]
        mixed = mixed_ref[...]
        sc = sc_ref[...]
        sg = _sigmoid(zb)
        dyb0 = dyb * (zb * sg)
        dzb = dyb * (mixed * sc) * (sg * (1.0 + zb * (1.0 - sg)))
        db_ref[:, W:] = dzb.astype(BF16)
        dbias_ref[:, W:] += jnp.sum(dzb, axis=0, keepdims=True)
        dsc_ref[...] += jnp.sum(dyb0 * mixed, axis=0, keepdims=True)
        dmix = (dyb0 * sc).astype(BF16)
        t = tile * TM + lax.broadcasted_iota(jnp.int32, (TM, 128), 0)
        for gi, win in enumerate(WINS):
            cs = slice(128 * gi, 128 * (gi + 1))
            dpw_ref[gi] += _dot_tn(pooled_ref[:, cs], dmix[:, cs])
            dpool = _dot_nt(dmix[:, cs], pw_ref[gi])
            cnt = jnp.minimum(t + 1, win).astype(F32)
            e = dpool / cnt
            buf[0:TM, cs] = e
            acc = e - dpool
            for k in range(1, win):
                acc = acc + buf[k:k + TM, cs]
            db_ref[:, cs] = acc.astype(BF16)
            dbias_ref[:, cs] += jnp.sum(acc, axis=0, keepdims=True)
        buf[TM:TM + 16, :] = buf[0:16, :]

    return dict(
        body=body, width=2 * W,
        in_specs=[_rows(W, 0, True), _rows(W, 3, True), _rows(W, 0, True), _rows(W, 0, True),
                  _layer(l, (4, 128, 128)), _layer(l, (1, W))],
        out_specs=[_full((4, 128, 128)), _full((1, W)), _full((1, 2 * W))],
        out_shape=[SDS((4, 128, 128), F32), SDS((1, W), F32), SDS((1, 2 * W), F32)],
        args=(dyb, proj, mixed, pooled, pw, scale),
        scratch=[pltpu.VMEM((TM + 16, W), F32)])


def _s5_bwd(l, dya, proj, y1, q, sre, sim, cret, cimnt, bret, bimt, st_re, st_im, cr_re, cr_im, dsk, wglu):
    def halo(i):
        return (jnp.maximum((NT - 1 - i) * (TM // 8) - 1, 0), 0)

    def body(dya_ref, ua_ref, za_ref, y1_ref, q_ref, sre_ref, sim_ref, hre_ref, him_ref,
             cret_ref, cimnt_ref, bret_ref, bimt_ref, st_re_ref, st_im_ref, cr_re_ref, cr_im_ref, d_ref, wg_ref,
             da_ref, dwg_ref, dbg_ref, dd_ref, dcre_ref, dcimn_ref, dbre_ref, dbim_ref, dare_ref, daim_ref, dbias_ref,
             lre, lim, car_ref):
        i = pl.program_id(0)
        tile = NT - 1 - i

        @pl.when(i == 0)
        def _():
            for ref in (dwg_ref, dbg_ref, dd_ref, dcre_ref, dcimn_ref, dbre_ref, dbim_ref, dare_ref, daim_ref, dbias_ref,
                        car_ref):
                ref[...] = jnp.zeros_like(ref)

        u = ua_ref[...]
        za = za_ref[...]
        y1 = y1_ref[...]
        dya = dya_ref[...]
        y2 = _gelu(y1)
        sg = _sigmoid(q_ref[...])
        sgz = _sigmoid(za)
        dy3 = dya * (za * sgz)
        dza = dya * (y2 * sg) * (sgz * (1.0 + za * (1.0 - sgz)))
        da_ref[:, W:] = dza.astype(BF16)
        dbias_ref[:, W:] += jnp.sum(dza, axis=0, keepdims=True)
        dq = dy3 * y2 * (sg * (1.0 - sg))
        dqb = dq.astype(BF16)
        dy2 = dy3 * sg + _dot_nt(dqb, wg_ref[...])
        dwg_ref[...] += _dot_tn(y2.astype(BF16), dqb)
        dbg_ref[...] += jnp.sum(dq, axis=0, keepdims=True)
        dy1 = dy2 * _gelu_grad(y1)
        dd_ref[...] += jnp.sum(dy1 * u, axis=0, keepdims=True)
        dy1b = dy1.astype(BF16)
        ub = u.astype(BF16)
        for k in range(4):
            blk = slice(512 * k, 512 * (k + 1))
            ks = slice(128 * k, 128 * (k + 1))
            lre[:, blk] = _dot(dy1b[:, ks], cret_ref[k])
            lim[:, blk] = _dot(dy1b[:, ks], cimnt_ref[k])
            dcre_ref[k] += _dot_tn(sre_ref[:, blk].astype(BF16), dy1b[:, ks])
            dcimn_ref[k] += _dot_tn(sim_ref[:, blk].astype(BF16), dy1b[:, ks])
        rowid = lax.broadcasted_iota(jnp.int32, (8, 512), 0)
        gate = (tile > 0).astype(F32)

        def d_abar(c, cols, lr, li):
            rows = pl.ds(pl.multiple_of(c * 8, 8), 8)
            prows = pl.ds(pl.multiple_of(jnp.maximum(c - 1, 0) * 8, 8), 8)
            pr = jnp.where(c == 0, hre_ref[7:8, cols] * gate, sre_ref[prows, cols][7:8])
            pi = jnp.where(c == 0, him_ref[7:8, cols] * gate, sim_ref[prows, cols][7:8])
            sr = jnp.where(rowid == 0, pr, pltpu.roll(sre_ref[rows, cols], 1, 0))
            si = jnp.where(rowid == 0, pi, pltpu.roll(sim_ref[rows, cols], 1, 0))
            dare_ref[:, cols] += sr * lr + si * li
            daim_ref[:, cols] += sr * li - si * lr

        carry = _scan_tile(lre, lim, st_re_ref, st_im_ref, cr_re_ref, cr_im_ref, _load_carry(car_ref), True, d_abar)
        _store_carry(car_ref, carry)

        for k in range(4):
            blk = slice(512 * k, 512 * (k + 1))
            ks = slice(128 * k, 128 * (k + 1))
            lrb = lre[:, blk].astype(BF16)
            lib = lim[:, blk].astype(BF16)
            du = dy1[:, ks] * d_ref[:, ks] + _dot(lrb, bret_ref[k]) + _dot(lib, bimt_ref[k])
            da_ref[:, ks] = du.astype(BF16)
            dbias_ref[:, ks] += jnp.sum(du, axis=0, keepdims=True)
            dbre_ref[k] += _dot_tn(ub[:, ks], lrb)
            dbim_ref[k] += _dot_tn(ub[:, ks], lib)

    return dict(
        body=body, width=2 * W,
        in_specs=[_rows(W, 0, True), _rows(W, 0, True), _rows(W, 1, True), _rows(W, 0, True), _rows(W, 0, True),
                  _rows(GP, 0, True), _rows(GP, 0, True),
                  pl.BlockSpec((8, GP), halo), pl.BlockSpec((8, GP), halo),
                  _layer(l, (4, 128, 512)), _layer(l, (4, 128, 512)), _layer(l, (4, 512, 128)), _layer(l, (4, 512, 128)),
                  _layer(l, (24, GP)), _layer(l, (24, GP)), _layer(l, (8, GP)), _layer(l, (8, GP)), _layer(l, (1, W)),
                  _layer(l, (W, W))],
        out_specs=[_full((W, W)), _full((1, W)), _full((1, W)),
                   _full((4, 512, 128)), _full((4, 512, 128)), _full((4, 128, 512)), _full((4, 128, 512)),
                   _full((8, GP)), _full((8, GP)), _full((1, 2 * W))],
        out_shape=[SDS((W, W), F32), SDS((1, W), F32), SDS((1, W), F32),
                   SDS((4, 512, 128), F32), SDS((4, 512, 128), F32), SDS((4, 128, 512), F32), SDS((4, 128, 512), F32),
                   SDS((8, GP), F32), SDS((8, GP), F32), SDS((1, 2 * W), F32)],
        args=(dya, proj, proj, y1, q, sre, sim, sre, sim, cret, cimnt, bret, bimt, st_re, st_im, cr_re, cr_im, dsk,
              wglu),
        scratch=[pltpu.VMEM((TM, GP), F32), pltpu.VMEM((TM, GP), F32), pltpu.VMEM((8, GP), F32)])


def _inproj_dw(l, r, h, dproj, jobs=(), rows=D // 2):
    def body(h_ref, dp_ref, dw_ref):
        part = _dot_tn(h_ref[...], dp_ref[...])

        @pl.when(pl.program_id(1) == 0)
        def _():
            dw_ref[...] = part

        @pl.when(pl.program_id(1) > 0)
        def _():
            dw_ref[...] += part

    return _pcall(
        body, f"inproj_dw{r}_l{l}", (4, L // TK),
        [pl.BlockSpec((TK, rows), lambda j, i: (i, r)), pl.BlockSpec((TK, 1024), lambda j, i: (i, j))],
        [pl.BlockSpec((None, rows, 1024), lambda j, i: (j, 0, 0))],
        [SDS((4, rows, 1024), F32)],
        (h, dproj), jobs=jobs)


def _inproj_dx(l, dproj, w, x, g, dxn, jobs=()):
    def body(dp_ref, w_ref, x_ref, g_ref, dxn_ref, dx_ref, dg_ref):
        @pl.when(pl.program_id(0) == 0)
        def _():
            dg_ref[...] = jnp.zeros_like(dg_ref)

        dh = _dot_nt(dp_ref[:, 0:1024], w_ref[0])
        for j in range(1, 4):
            dh = dh + _dot_nt(dp_ref[:, j * 1024:(j + 1) * 1024], w_ref[j])
        xv = x_ref[...]
        r = lax.rsqrt(jnp.mean(xv * xv, axis=-1, keepdims=True) + EPS)
        xn = xv * r
        dg_ref[...] += jnp.sum(dh * xn, axis=0, keepdims=True)
        dn = dh * g_ref[...]
        dx_ref[...] = dxn_ref[...] + r * (dn - xn * jnp.mean(dn * xn, axis=-1, keepdims=True))

    return _pcall(
        body, f"inproj_dx_l{l}", (L // TMM,),
        [_rows_mm(NIN), _layer(l, (4, D, 1024)), _rows_mm(D), _layer(l, (1, D)), _rows_mm(D)],
        [_rows_mm(D), _full((1, D))],
        [SDS((L, D), F32), SDS((1, D), F32)],
        (dproj, w, x, g, dxn), jobs=jobs)


def _discretize(log_dt, lam_re, lam_im, b_re, b_im):
    dt = jnp.exp(log_dt)[..., None]
    mag = jnp.exp(lam_re * dt)
    ang = lam_im * dt
    abar_re = mag * jnp.cos(ang)
    abar_im = mag * jnp.sin(ang)
    num_re = abar_re - 1.0
    num_im = abar_im
    den = lam_re * lam_re + lam_im * lam_im
    coef_re = (num_re * lam_re + num_im * lam_im) / den
    coef_im = (num_im * lam_re - num_re * lam_im) / den
    bbar_re = coef_re[..., None] * b_re - coef_im[..., None] * b_im
    bbar_im = coef_re[..., None] * b_im + coef_im[..., None] * b_re
    return abar_re, abar_im, bbar_re, bbar_im


def _powers(abar_re, abar_im):
    ar, ai = abar_re.reshape(DEPTH, 1, GP), abar_im.reshape(DEPTH, 1, GP)
    rows_re, rows_im = [ar], [ai]
    for _ in range(7):
        pr, pi = rows_re[-1], rows_im[-1]
        rows_re.append(pr * ar - pi * ai)
        rows_im.append(pr * ai + pi * ar)
    row = jnp.arange(8)[:, None]

    def steps(rows, keep):
        return jnp.concatenate([jnp.where(keep(d), rows[d - 1], 0.0) for d in (1, 2, 4)], axis=1)

    neg_im = [-r for r in rows_im]
    fwd = (steps(rows_re, lambda d: row >= d), steps(rows_im, lambda d: row >= d),
           jnp.concatenate(rows_re, axis=1), jnp.concatenate(rows_im, axis=1))
    rev = (steps(rows_re, lambda d: row < 8 - d), steps(neg_im, lambda d: row < 8 - d),
           jnp.concatenate(rows_re[::-1], axis=1), jnp.concatenate(neg_im[::-1], axis=1))
    return fwd, rev


_EYE8 = functools.partial(jnp.eye, 8, dtype=F32)


def _expand_in(b):
    return jnp.einsum("lkgpc,gh->lkgchp", b.reshape(DEPTH, 4, 8, P, C), _EYE8()).reshape(DEPTH, 4, 128, 512)


def _extract_in(e):
    return jnp.einsum("lkgchp,gh->lkgpc", e.reshape(DEPTH, 4, 8, C, 8, P), _EYE8()).reshape(DEPTH, G, P, C)


def _expand_out(c):
    return jnp.einsum("lkgcp,gh->lkgphc", c.reshape(DEPTH, 4, 8, C, P), _EYE8()).reshape(DEPTH, 4, 512, 128)


def _extract_out(e):
    return jnp.einsum("lkgphc,gh->lkgcp", e.reshape(DEPTH, 4, 8, P, 8, C), _EYE8()).reshape(DEPTH, G, C, P)


SMALL = ("norm_g", "b_in", "ssm_log_dt", "ssm_lam_re", "ssm_lam_im", "ssm_b_re", "ssm_b_im",
         "ssm_c_re", "ssm_c_im", "ssm_d", "ssm_b_glu", "pool_w", "pool_scale")
BIG = ("w_in", "ssm_w_glu", "w_branch_a", "w_branch_b", "w_out")


def _step(x, target, w, m, v, place):
    sp = {n: w[n] for n in SMALL}
    final_norm_g = w["final_norm_g"]
    wbuf = dict(zip(BIG, _cast_own(place, [w[n] for n in BIG])))
    (abar_re, abar_im, bbar_re, bbar_im), disc_vjp = jax.vjp(
        _discretize, *(sp[n] for n in ("ssm_log_dt", "ssm_lam_re", "ssm_lam_im", "ssm_b_re", "ssm_b_im")))
    powers_fwd, powers_rev = _powers(abar_re, abar_im)
    b_re_x, b_im_x = _expand_in(bbar_re), _expand_in(bbar_im)
    c_re_x, c_imn_x = _expand_out(sp["ssm_c_re"]), _expand_out(-sp["ssm_c_im"])
    b_x = jnp.concatenate([b_re_x, b_im_x], axis=3).astype(BF16)
    t = lambda a: jnp.swapaxes(a, 2, 3).astype(BF16)
    c_re_t, c_imn_t, b_re_t, b_im_t = t(c_re_x), t(c_imn_x), t(b_re_x), t(b_im_x)
    c_re_x, c_imn_x = c_re_x.astype(BF16), c_imn_x.astype(BF16)
    row = lambda n: sp[n].reshape(DEPTH, 1, -1)
    g, b_in, dsk, b_glu, scale = row("norm_g"), row("b_in"), row("ssm_d"), row("ssm_b_glu"), row("pool_scale")
    pw = sp["pool_w"].astype(BF16)

    saved = []
    for l in range(DEPTH):
        three = BIG[2:]
        if l == 0:
            wbuf["w_in"], wbuf["ssm_w_glu"] = _comm_only(
                "gather_first", _RingGatherJob([wbuf["w_in"], wbuf["ssm_w_glu"]], 0))[0]
            jobs = [_GatherJob([wbuf[n] for n in three], 0), _GatherJob([wbuf["ssm_w_glu"]], 1)]
        else:
            jobs = []
        (h, proj), res = _norm_inproj(l, x, g, wbuf["w_in"], b_in, jobs)
        if res:
            wbuf.update(zip(three, res[0][0]))
            (wbuf["ssm_w_glu"],) = res[1][0]
        wg = dict(wbuf, ssm_w_glu=wbuf["ssm_w_glu"].reshape(DEPTH, W, W), w_out=wbuf["w_out"].reshape(DEPTH, D, D))
        job = _GatherJob([wbuf["w_in"]], l + 1) if l + 1 < DEPTH else _GatherJob([wbuf[n] for n in three], l)
        (sre, sim, y1, q, ya, pooled, mixed, yb), res = _fused(
            f"branches_fwd_l{l}", (NT,),
            [_s5_fwd(l, proj, b_x, c_re_x, c_imn_x, powers_fwd, dsk, wg["ssm_w_glu"], b_glu),
             _pool_fwd(l, proj, pw, scale)], jobs=[job])
        if l + 1 < DEPTH:
            (wbuf["w_in"],) = res[0][0]
        else:
            wbuf.update(zip(three, res[0][0]))
        wg = dict(wbuf, ssm_w_glu=wbuf["ssm_w_glu"].reshape(DEPTH, W, W), w_out=wbuf["w_out"].reshape(DEPTH, D, D))
        last = l + 1 == DEPTH
        pa, pb, mg, *tail = _merge_out(l, ya, yb, proj, x, wg["w_branch_a"], wg["w_branch_b"], wg["w_out"],
                                       (final_norm_g.reshape(1, D), target) if last else None)
        saved.append(dict(x=x, h=h, proj=proj, sre=sre, sim=sim, y1=y1, q=q, ya=ya,
                          pooled=pooled, mixed=mixed, yb=yb, pa=pa, pb=pb, mg=mg))
        if last:
            loss, dx, dgf = tail
        else:
            (x,) = tail

    per_layer = {n: [None] * DEPTH for n in ("norm_g", "b_in", "ssm_d", "ssm_b_glu", "pool_w", "pool_scale",
                                             "dare", "daim", "dbre", "dbim", "dcre", "dcimn")}
    red = _Reducer(place, w, m, v)
    for l in reversed(range(DEPTH)):
        s = saved[l]
        (dproj, dya, dyb, dwo, dwa, dwb, dbias_g), res = _merge_out_bwd(
            l, dx, s["mg"], s["proj"], s["pa"], s["pb"], s["ya"], s["yb"],
            wg["w_out"], wg["w_branch_a"], wg["w_branch_b"], red.jobs())
        red.land(res)
        (dproj, dwg, dbg, dd, dcre, dcimn, dbre, dbim, dare, daim, dbias_a, dpw, dsc, dbias_b), res = _fused(
            f"branches_bwd_l{l}", (NT,),
            [_s5_bwd(l, dya, s["proj"], s["y1"], s["q"], s["sre"], s["sim"], c_re_t, c_imn_t, b_re_t, b_im_t,
                     *powers_rev, dsk, wg["ssm_w_glu"]),
             _pool_bwd(l, dyb, s["proj"], s["mixed"], s["pooled"], pw, scale)],
            shared=(dproj, _rows(4 * W, 0, True)), jobs=red.jobs())
        red.land(res)
        rest = [dwg.reshape(4, W // 4, W), dwa, dwb, dwo.reshape(4, D // 4, D)]
        if l == 0:
            for r, rows in ((0, D - LAST_ROWS), (D // LAST_ROWS - 1, LAST_ROWS)):
                outs, res = _inproj_dw(l, r, s["h"], dproj, red.jobs(), rows=rows)
                red.land(res)
                if r == 0:
                    red.add(l, "in0", BIG, outs + rest)
                else:
                    red.add(l, f"in{r}", BIG[:1], outs, [r * rows])
        else:
            dwin, res = _inproj_dw(l, 0, s["h"], dproj, red.jobs(), rows=D)
            red.land(res)
        (dx, dg), res = _inproj_dx(l, dproj, wg["w_in"], s["x"], g, dx, red.jobs())
        red.land(res)
        if l > 0:
            red.add(l, "all", BIG, dwin + rest)
        for n, a in (("norm_g", dg.reshape(D)), ("b_in", jnp.concatenate([dbias_a, dbias_b, dbias_g], axis=1).reshape(NIN)),
                     ("ssm_d", dd.reshape(W)), ("ssm_b_glu", dbg.reshape(W)), ("pool_w", dpw), ("pool_scale", dsc.reshape(W)),
                     ("dare", dare), ("daim", daim), ("dbre", dbre), ("dbim", dbim), ("dcre", dcre), ("dcimn", dcimn)):
            per_layer[n][l] = a
    gs = {n: jnp.stack(a) for n, a in per_layer.items()}
    d_abar = [jnp.sum(gs.pop(n), axis=1).reshape(DEPTH, G, P) for n in ("dare", "daim")]
    (gs["ssm_log_dt"], gs["ssm_lam_re"], gs["ssm_lam_im"], gs["ssm_b_re"], gs["ssm_b_im"]) = disc_vjp(
        (*d_abar, _extract_in(gs.pop("dbre")), _extract_in(gs.pop("dbim"))))
    gs["ssm_c_re"], gs["ssm_c_im"] = _extract_out(gs.pop("dcre")), -_extract_out(gs.pop("dcimn"))
    gs["final_norm_g"] = dgf

    natural = {n: w[n].shape for n in REPLICATED}
    rw, rm, rv = {}, {}, {}
    for n in REPLICATED:
        shape = DENSE.get(n, natural[n])
        gs[n], rw[n], rm[n], rv[n] = (a.reshape(shape) for a in (gs[n], w[n], m[n], v[n]))
    small = [gs[n] for n in REPLICATED] + [loss]
    jobs = red.jobs()
    res = _pcall(None, "tail_exchange", (), [], [], [], [], jobs=jobs + [_SiblingJob(small, False)])[1]
    red.land(res[:len(jobs)])
    pair_small = _small_pair_sum(place, small, res[-1][1], [BF16 if a.ndim > 2 else F32 for a in small])
    jobs = red.jobs()
    res = _pcall(None, "tail_gather", (), [], [], [], [], jobs=jobs + [_ChipGatherJob(pair_small)])[1]
    red.land(res[:len(jobs)])
    assert not red.active
    small_parts = dict(zip(REPLICATED + ("loss",), res[-1][0]))

    k = len(REPLICATED)
    outs = _adamw_small("adamw_small", [rw[n] for n in REPLICATED], [small_parts[n] for n in REPLICATED],
                        [rm[n] for n in REPLICATED], [rv[n] for n in REPLICATED], small_parts["loss"])
    results = {n: red.big[n] for n in BIG}
    results.update({n: [outs[1 + q * k + i].reshape(natural[n]) for q in range(4)] for i, n in enumerate(REPLICATED)})
    return outs[0][0, 0], dx, results


def _place():
    x, y, c = lax.axis_index("x"), lax.axis_index("y"), lax.axis_index("c")
    chips = [(1 - x, y), (x, 1 - y), (1 - x, 1 - y)]
    return x, y, c, 2 * x + y, chips, [2 * cx + cy for cx, cy in chips]


def _remote(src, dst, ssem, rsem, dev):
    return pltpu.make_async_remote_copy(src_ref=src, dst_ref=dst, send_sem=ssem, recv_sem=rsem,
                                        device_id=dev, device_id_type=MESH)


class _GatherJob:
    def __init__(self, bufs, l):
        self.srcs, self.bufs, self.news, self.l = [], list(bufs), [], l
        self.scratch = [pltpu.SemaphoreType.DMA((len(self.bufs), 3))] * 4

    def _half(self, ref, k, h):
        rows = ref.shape[2] // 2
        return ref.at[self.l, k, pl.ds(pl.multiple_of(h * rows, 8), rows), :]

    def _ici(self, bufs, sems, a, j, k):
        _, _, c, _, chips, _ = _place()
        blk = self._half(bufs[a], k, c)
        return _remote(blk, blk, sems[0].at[a, j], sems[1].at[a, j], (*chips[j], c))

    def _d2d(self, bufs, sems, a, j, k, h):
        x, y, c, _, _, _ = _place()
        blk = self._half(bufs[a], k, h)
        return _remote(blk, blk, sems[2].at[a, j], sems[3].at[a, j], (x, y, 1 - c))

    def start(self, srcs, bufs, news, sems):
        me = _place()[3]
        for a in range(len(self.bufs)):
            for j in range(3):
                self._ici(bufs, sems, a, j, me).start()

    def finish(self, srcs, bufs, news, sems):
        _, _, c, me, _, cid = _place()
        pairs = [(a, j) for a in range(len(self.bufs)) for j in range(3)]
        for a, j in pairs:
            self._ici(bufs, sems, a, j, cid[j]).wait_recv()
            self._d2d(bufs, sems, a, j, cid[j], c).start()
        for a, j in pairs:
            self._d2d(bufs, sems, a, j, cid[j], 1 - c).wait_recv()
        for a, j in pairs:
            self._ici(bufs, sems, a, j, me).wait_send()
            self._d2d(bufs, sems, a, j, cid[j], c).wait_send()


class _RingGatherJob(_GatherJob):
    def __init__(self, bufs, l):
        super().__init__(bufs, l)
        n = len(self.bufs)
        self.scratch = [pltpu.SemaphoreType.DMA((n, 2))] * 4 + [pltpu.SemaphoreType.DMA((n, 4))] * 2

    def _rows(self, ref, k, h, part):
        half = ref.shape[2] // 2
        start, rows = (h * half, half) if part is None else (h * half + part * (half // 2), half // 2)
        return ref.at[self.l, k, pl.ds(pl.multiple_of(start, 8), rows), :]

    def _to_chip(self, bufs, sems, base, a, j, k, part):
        _, _, c, _, chips, _ = _place()
        blk = self._rows(bufs[a], k, c, part)
        return _remote(blk, blk, sems[base].at[a, j], sems[base + 1].at[a, j], (*chips[j], c))

    def _to_sibling(self, bufs, sems, a, i, k, h, part):
        x, y, c, _, _, _ = _place()
        blk = self._rows(bufs[a], k, h, part)
        return _remote(blk, blk, sems[4].at[a, i], sems[5].at[a, i], (x, y, 1 - c))

    def start(self, srcs, bufs, news, sems):
        me = _place()[3]
        for a in range(len(self.bufs)):
            for j in range(2):
                self._to_chip(bufs, sems, 0, a, j, me, None).start()

    def finish(self, srcs, bufs, news, sems):
        _, _, c, me, _, cid = _place()
        arrays = range(len(self.bufs))
        for a in arrays:
            for j in (1, 0):
                self._to_chip(bufs, sems, 0, a, j, cid[j], None).wait_recv()
                self._to_chip(bufs, sems, 2, a, 1 - j, cid[j], 1 - j).start()
                self._to_sibling(bufs, sems, a, j, cid[j], c, None).start()
        for a in arrays:
            for part in range(2):
                self._to_chip(bufs, sems, 2, a, part, cid[2], part).wait_recv()
                self._to_sibling(bufs, sems, a, 2 + part, cid[2], c, part).start()
        for a in arrays:
            for j in range(2):
                self._to_sibling(bufs, sems, a, j, cid[j], 1 - c, None).wait_recv()
                self._to_sibling(bufs, sems, a, 2 + j, cid[2], 1 - c, j).wait_recv()
        for a in arrays:
            for j in range(2):
                self._to_chip(bufs, sems, 0, a, j, me, None).wait_send()
                self._to_chip(bufs, sems, 2, a, 1 - j, cid[j], 1 - j).wait_send()
                self._to_sibling(bufs, sems, a, j, cid[j], c, None).wait_send()
                self._to_sibling(bufs, sems, a, 2 + j, cid[2], c, j).wait_send()


class _SiblingJob:
    def __init__(self, srcs, rows_half):
        self.srcs, self.bufs, self.rows_half = list(srcs), [], rows_half
        self.news = [SDS((s.shape[0], s.shape[1] // 2, s.shape[2]) if rows_half else s.shape, s.dtype) for s in srcs]
        self.scratch = [pltpu.SemaphoreType.DMA((len(self.srcs),))] * 2

    def _copy(self, srcs, news, sems, a):
        x, y, c, _, _, _ = _place()
        src = srcs[a]
        if self.rows_half:
            rows = src.shape[1] // 2
            src = src.at[:, pl.ds(pl.multiple_of((1 - c) * rows, 8), rows), :]
        return _remote(src, news[a], sems[0].at[a], sems[1].at[a], (x, y, 1 - c))

    def start(self, srcs, bufs, news, sems):
        for a in range(len(self.srcs)):
            self._copy(srcs, news, sems, a).start()

    def finish(self, srcs, bufs, news, sems):
        for a in range(len(self.srcs)):
            self._copy(srcs, news, sems, a).wait()


class _ScatterJob:
    def __init__(self, parts):
        self.srcs, self.bufs = list(parts), []
        self.news = [SDS((3,) + p.shape[1:], p.dtype) for p in parts]
        self.scratch = [pltpu.SemaphoreType.DMA((len(self.srcs), 3))] * 2

    def _copy(self, srcs, news, sems, a, j):
        _, _, c, _, chips, cid = _place()
        return _remote(srcs[a].at[cid[j]], news[a].at[j], sems[0].at[a, j], sems[1].at[a, j], (*chips[j], c))

    def start(self, srcs, bufs, news, sems):
        for a in range(len(self.srcs)):
            for j in range(3):
                self._copy(srcs, news, sems, a, j).start()

    def finish(self, srcs, bufs, news, sems):
        for a in range(len(self.srcs)):
            for j in range(3):
                self._copy(srcs, news, sems, a, j).wait()


def _comm_only(name, job):
    return _pcall(None, name, (), [], [], [], [], jobs=[job])[1][0]


class _ChipGatherJob(_GatherJob):
    def __init__(self, bufs):
        super().__init__(bufs, None)

    def _half(self, ref, k, h):
        return ref.at[k, h]


def _cast_own(place, ws):
    n = len(ws)

    def body(p_ref, *refs):
        for i_ref, o_ref in zip(refs[:n], refs[n:]):
            o_ref[...] = i_ref[...].astype(BF16)

    return pl.pallas_call(
        body, name="cast_own_shards",
        grid_spec=pltpu.PrefetchScalarGridSpec(
            num_scalar_prefetch=1, grid=(DEPTH,),
            in_specs=[pl.BlockSpec((None,) + a.shape[1:], lambda l, p: (l, 0, 0)) for a in ws],
            out_specs=[pl.BlockSpec((None, None) + a.shape[1:], lambda l, p: (l, p[1], 0, 0)) for a in ws]),
        out_shape=[SDS((DEPTH, 4) + a.shape[1:], BF16) for a in ws],
        compiler_params=_params("arbitrary"),
    )(place, *ws)


def _half_tiles(a_):
    rows = a_ // 2
    ta = next(t for t in (256, 128, 64, 32, 16, 8) if rows % t == 0)
    return rows, ta, rows // ta


def _pair_sums_bf16(name, place, owns, recvs):
    n = len(owns)

    def body(p_ref, *refs):
        for own_ref, recv_ref, out_ref in zip(refs[:n], refs[n:2 * n], refs[2 * n:]):
            out_ref[...] = (own_ref[...] + recv_ref[...]).astype(BF16)

    def own_half(a):
        return pl.BlockSpec((None, a.shape[1] // 2, a.shape[2]), lambda s, p: (s, p[0], 0))

    def block(a):
        return pl.BlockSpec((None,) + a.shape[1:], lambda s, p: (s, 0, 0))

    return pl.pallas_call(
        body, name=name,
        grid_spec=pltpu.PrefetchScalarGridSpec(
            num_scalar_prefetch=1, grid=(4,),
            in_specs=[own_half(a) for a in owns] + [block(r) for r in recvs],
            out_specs=[block(r) for r in recvs]),
        out_shape=[SDS(r.shape, BF16) for r in recvs],
        compiler_params=_params("arbitrary"),
    )(place, *owns, *recvs)


def _shard_sums(name, place, owns, recvs, rbufs):
    n = len(owns)

    def body(p_ref, *refs):
        for own_ref, recv_ref, r_ref, out_ref in zip(refs[:n], refs[n:2 * n], refs[2 * n:3 * n], refs[3 * n:]):
            acc = own_ref[...] + recv_ref[...]
            for j in range(3):
                acc = acc + r_ref[j].astype(F32)
            out_ref[...] = acc

    steps = 2

    def own_half(a):
        return pl.BlockSpec((None, a.shape[1] // 2 // steps, a.shape[2]), lambda i, p: (p[1], steps * p[0] + i, 0))

    def recv_block(a):
        return pl.BlockSpec((None, a.shape[1] // steps, a.shape[2]), lambda i, p: (p[1], i, 0))

    return pl.pallas_call(
        body, name=name,
        grid_spec=pltpu.PrefetchScalarGridSpec(
            num_scalar_prefetch=1, grid=(steps,),
            in_specs=([own_half(a) for a in owns] + [recv_block(r) for r in recvs]
                      + [pl.BlockSpec((3, rb.shape[1] // steps, rb.shape[2]), lambda i, p: (0, i, 0)) for rb in rbufs]),
            out_specs=[pl.BlockSpec((r.shape[1] // steps, r.shape[2]), lambda i, p: (i, 0)) for r in recvs]),
        out_shape=[SDS(r.shape[1:], F32) for r in recvs],
        compiler_params=_params("arbitrary"),
    )(place, *owns, *recvs, *rbufs)


def _small_pair_sum(place, mine, recv, dtypes):
    n = len(mine)

    def body(p_ref, *refs):
        for m_ref, r_ref, o_ref in zip(refs[:n], refs[n:2 * n], refs[2 * n:]):
            o_ref[...] = (m_ref[...] + r_ref[...]).astype(o_ref.dtype)

    def whole(a):
        zeros = (0,) * a.ndim
        return pl.BlockSpec(a.shape, lambda i, p: zeros)

    def mine_blk(a):
        zeros = (0,) * a.ndim
        return pl.BlockSpec((None,) + a.shape, lambda i, p: (p[1],) + zeros)

    return pl.pallas_call(
        body, name="small_pair_sum",
        grid_spec=pltpu.PrefetchScalarGridSpec(
            num_scalar_prefetch=1, grid=(1,),
            in_specs=[whole(a) for a in mine] + [whole(a) for a in recv],
            out_specs=[mine_blk(a) for a in mine]),
        out_shape=[SDS((4,) + a.shape, dt) for a, dt in zip(mine, dtypes)],
        compiler_params=_params("arbitrary"),
    )(place, *mine, *recv)


def _adam_math(w, g, m, v):
    m = B1 * m + (1.0 - B1) * g
    v = B2 * v + (1.0 - B2) * (g * g)
    m_hat = m / (1.0 - B1 ** STEP)
    v_hat = v / (1.0 - B2 ** STEP)
    delta = -LR * (m_hat / (jnp.sqrt(v_hat) + EPS_A) + WD * w)
    return delta, m, v


def _adamw_big(name, l, row0, w, m, v, mine, other, prev, jobs=()):
    _, _, b_ = w.shape
    _, ta, nh = _half_tiles(2 * mine.shape[0])
    prev = list(prev or [])

    def body(w_ref, m_ref, v_ref, mine_ref, other_ref, *rest):
        g_ref, d_ref, mo_ref, vo_ref = rest[len(prev):]
        g = jnp.where(pl.program_id(0) == lax.axis_index("c"), mine_ref[...], other_ref[...])
        g_ref[...] = g
        d_ref[...], mo_ref[...], vo_ref[...] = _adam_math(w_ref[...], g, m_ref[...], v_ref[...])

    slab = pl.BlockSpec((None, ta, b_), lambda h, i: (l, row0 // ta + h * nh + i, 0))
    half = pl.BlockSpec((ta, b_), lambda h, i: (i, 0))
    outs, res = _pcall(
        body, name, (2, nh), [slab, slab, slab, half, half] + [_ANY] * len(prev), [slab] * 4, [SDS(w.shape, F32)] * 4,
        (w, m, v, mine, other, *prev), aliases={5 + k: k for k in range(len(prev))}, jobs=jobs)
    return outs, res


def _adamw_small(name, ws, parts, ms, vs, loss_parts=None):
    k = len(ws)
    extra = [] if loss_parts is None else [loss_parts]

    def chip_sum(p_ref):
        p = [p_ref[k].astype(F32) for k in range(4)]
        return ((p[0] + p[1]) + p[2]) + p[3]

    def body(*refs):
        w_refs, p_refs, m_refs, v_refs = refs[:k], refs[k:2 * k], refs[2 * k:3 * k], refs[3 * k:4 * k]
        outs = refs[4 * k + len(extra):]
        if extra:
            outs[0][...] = chip_sum(refs[4 * k])
            outs = outs[1:]
        for a in range(k):
            g = chip_sum(p_refs[a])
            outs[a][...] = g
            outs[k + a][...], outs[2 * k + a][...], outs[3 * k + a][...] = _adam_math(
                w_refs[a][...], g, m_refs[a][...], v_refs[a][...])

    like = [SDS(a.shape, F32) for a in ws]
    return pl.pallas_call(
        body, name=name,
        out_shape=([SDS(loss_parts.shape[1:], F32)] if extra else []) + like * 4,
        compiler_params=pltpu.CompilerParams(vmem_limit_bytes=VMEM_LIMIT),
    )(*ws, *parts, *ms, *vs, *extra)


class _Reducer:
    def __init__(self, place, w, m, v):
        self.place, self.w, self.m, self.v = place, w, m, v
        self.active, self.riding = [], []
        self.big = {n: None for n in BIG}

    def add(self, l, tag, names, own, row0s=None):
        self.active.append(dict(l=l, key=f"{tag}_l{l}", names=names, own=list(own), row0s=row0s or [0] * len(names),
                                stage=0))

    def jobs(self):
        self.riding = list(self.active)
        return [(_SiblingJob(g["own"], True), _ScatterJob(g.get("parts", [])), _SiblingJob(g.get("shard", []), False))
                [g["stage"]] for g in self.riding]

    def land(self, res):
        for g, (_, news) in zip(self.riding, res):
            if g["stage"] == 0:
                g["recv"] = news
                g["parts"] = _pair_sums_bf16(f"pair_sums_{g['key']}", self.place, g["own"], news)
            elif g["stage"] == 1:
                g["shard"] = _shard_sums(f"shard_sums_{g['key']}", self.place, g["own"], g["recv"], news)
            else:
                for n, mine, other, row0 in zip(g["names"], g["shard"], news, g["row0s"]):
                    self.big[n] = _adamw_big(f"adamw_{n}_{row0}_{g['key']}", g["l"], row0, self.w[n], self.m[n], self.v[n],
                                             mine, other, self.big[n])[0]
                self.active.remove(g)
            g["stage"] += 1
        self.riding = []


WEIGHTS = ("norm_g", "w_in", "b_in", "ssm_log_dt", "ssm_lam_re", "ssm_lam_im", "ssm_b_re", "ssm_b_im", "ssm_c_re",
           "ssm_c_im", "ssm_d", "ssm_w_glu", "ssm_b_glu", "pool_w", "pool_scale", "w_branch_a", "w_branch_b", "w_out",
           "final_norm_g")
REPLICATED = SMALL + ("final_norm_g",)
DENSE = {"ssm_b_re": (DEPTH, G, P * C), "ssm_b_im": (DEPTH, G, P * C), "final_norm_g": (2, D // 2)}


def kernel(x, norm_g, w_in, b_in, ssm_log_dt, ssm_lam_re, ssm_lam_im, ssm_b_re, ssm_b_im, ssm_c_re, ssm_c_im, ssm_d, ssm_w_glu, ssm_b_glu, pool_w, pool_scale, w_branch_a, w_branch_b, w_out, final_norm_g, loss_target, m_norm_g, m_w_in, m_b_in, m_ssm_log_dt, m_ssm_lam_re, m_ssm_lam_im, m_ssm_b_re, m_ssm_b_im, m_ssm_c_re, m_ssm_c_im, m_ssm_d, m_ssm_w_glu, m_ssm_b_glu, m_pool_w, m_pool_scale, m_w_branch_a, m_w_branch_b, m_w_out, m_final_norm_g, v_norm_g, v_w_in, v_b_in, v_ssm_log_dt, v_ssm_lam_re, v_ssm_lam_im, v_ssm_b_re, v_ssm_b_im, v_ssm_c_re, v_ssm_c_im, v_ssm_d, v_ssm_w_glu, v_ssm_b_glu, v_pool_w, v_pool_scale, v_w_branch_a, v_w_branch_b, v_w_out, v_final_norm_g):
    w = dict(zip(WEIGHTS, (norm_g, w_in, b_in, ssm_log_dt, ssm_lam_re, ssm_lam_im, ssm_b_re, ssm_b_im, ssm_c_re,
                           ssm_c_im, ssm_d, ssm_w_glu, ssm_b_glu, pool_w, pool_scale, w_branch_a, w_branch_b, w_out,
                           final_norm_g)))
    m = dict(zip(WEIGHTS, (m_norm_g, m_w_in, m_b_in, m_ssm_log_dt, m_ssm_lam_re, m_ssm_lam_im, m_ssm_b_re, m_ssm_b_im,
                           m_ssm_c_re, m_ssm_c_im, m_ssm_d, m_ssm_w_glu, m_ssm_b_glu, m_pool_w, m_pool_scale,
                           m_w_branch_a, m_w_branch_b, m_w_out, m_final_norm_g)))
    v = dict(zip(WEIGHTS, (v_norm_g, v_w_in, v_b_in, v_ssm_log_dt, v_ssm_lam_re, v_ssm_lam_im, v_ssm_b_re, v_ssm_b_im,
                           v_ssm_c_re, v_ssm_c_im, v_ssm_d, v_ssm_w_glu, v_ssm_b_glu, v_pool_w, v_pool_scale,
                           v_w_branch_a, v_w_branch_b, v_w_out, v_final_norm_g)))
    place = jnp.stack([lax.axis_index("c"), 2 * lax.axis_index("x") + lax.axis_index("y")]).astype(jnp.int32)

    total_loss, dx, results = _step(x[0], loss_target[0], w, m, v, place)
    return (total_loss, dx[None], *[results[n][q] for q in range(4) for n in WEIGHTS])
```
